```python
import math
import jax, jax.numpy as jnp
from jax import lax
import numpy as np

D_MODEL = 1024
BATCH = 16
SEQ = 4096
DEPTH = 1

HEAD_DIM = 64
MIX_WIDTH = D_MODEL
A_HEADS = (MIX_WIDTH // 2) // HEAD_DIM
A_KV_HEADS = 2
A_GROUP = A_HEADS // A_KV_HEADS
B_HEADS = (MIX_WIDTH // 2) // HEAD_DIM
WINDOW = 128
BLOCK = 128
D_FF = 4 * D_MODEL
EPS = 1e-6

A_Q_W = A_HEADS * HEAD_DIM
A_KV_W = A_KV_HEADS * HEAD_DIM
B_W = B_HEADS * HEAD_DIM
IN_SPLITS = tuple(np.cumsum([A_Q_W, A_KV_W, A_KV_W, B_W, B_W, B_W]).tolist())
IN_WIDTH = A_Q_W + 2 * A_KV_W + 3 * B_W + B_HEADS

kernel_name = "hybrid_swa_sinks_fox_sqrelu"


def rmsnorm(x, g):
    x32 = x.astype(jnp.float32)
    y = x32 * lax.rsqrt(jnp.mean(x32 * x32, axis=-1, keepdims=True) + EPS)
    return (y * g.astype(jnp.float32)).astype(x.dtype)


def alibi_slopes(n):
    return jnp.exp2(-(8.0 / n) * (jnp.arange(n, dtype=jnp.float32) + 1.0))


def swa_sinks_attention(q, k, v, sinks):
    b, s, _, d = q.shape
    nb = s // BLOCK
    scale = 1.0 / math.sqrt(d)
    qb = q.reshape(b, nb, BLOCK, A_KV_HEADS, A_GROUP, d)
    pad = ((0, 0), (BLOCK, 0), (0, 0), (0, 0))
    kp = jnp.pad(k, pad).reshape(b, nb + 1, BLOCK, A_KV_HEADS, d)
    vp = jnp.pad(v, pad).reshape(b, nb + 1, BLOCK, A_KV_HEADS, d)
    kb = jnp.concatenate([kp[:, :-1], kp[:, 1:]], axis=2)
    vb = jnp.concatenate([vp[:, :-1], vp[:, 1:]], axis=2)
    scores = jnp.einsum('bnqkgd,bnskd->bnkgqs', qb, kb).astype(jnp.float32) * scale
    qpos = BLOCK + jnp.arange(BLOCK)
    kpos = jnp.arange(2 * BLOCK)
    dist = qpos[:, None] - kpos[None, :]
    band = (dist >= 0) & (dist < WINDOW)
    first_pad = (jnp.arange(nb) == 0)[:, None, None] & (kpos < BLOCK)[None, None, :]
    valid = band[None] & ~first_pad
    slopes = alibi_slopes(A_HEADS).reshape(A_KV_HEADS, A_GROUP)
    alibi = -slopes[:, :, None, None] * dist.astype(jnp.float32)[None, None]
    scores = scores + alibi[None, None]
    scores = jnp.where(valid[None, :, None, None], scores, -jnp.inf)
    sink = sinks.astype(jnp.float32).reshape(1, 1, A_KV_HEADS, A_GROUP, 1, 1)
    m = jnp.maximum(jnp.max(scores, axis=-1, keepdims=True), sink)
    p = jnp.exp(scores - m)
    denom = jnp.sum(p, axis=-1, keepdims=True) + jnp.exp(sink - m)
    p = (p / denom).astype(v.dtype)
    out = jnp.einsum('bnkgqs,bnskd->bnqkgd', p, vb)
    return out.reshape(b, s, A_HEADS, d)


def forgetting_attention(q, k, v, log_f):
    b, s, h, d = q.shape
    nb = s // BLOCK
    scale = 1.0 / math.sqrt(d)
    c = jnp.cumsum(log_f, axis=1)
    c_keys = jnp.transpose(c, (0, 2, 1))
    qb = jnp.moveaxis(q.reshape(b, nb, BLOCK, h, d), 1, 0)
    cb = jnp.moveaxis(c.reshape(b, nb, BLOCK, h), 1, 0)
    kpos = jnp.arange(s)

    def block_step(args):
        qi, ci, i = args
        sc = jnp.einsum('bqhd,bshd->bhqs', qi, k).astype(jnp.float32) * scale
        bias = jnp.transpose(ci, (0, 2, 1))[..., None] - c_keys[:, :, None, :]
        qpos = i * BLOCK + jnp.arange(BLOCK)
        causal = kpos[None, :] <= qpos[:, None]
        sc = jnp.where(causal[None, None], sc + bias, -jnp.inf)
        p = jax.nn.softmax(sc, axis=-1).astype(v.dtype)
        return jnp.einsum('bhqs,bshd->bqhd', p, v)

    out = lax.map(block_step, (qb, cb, jnp.arange(nb)))
    return jnp.moveaxis(out, 0, 1).reshape(b, s, h, d)


def _fwd_setup_inputs(seed: int = 0) -> dict:
    key = jax.random.key(seed)
    ks = jax.random.split(key, 14)
    f32 = jnp.float32
    x = jax.random.normal(ks[0], (BATCH, SEQ, D_MODEL), f32)
    attn_norm_g = 1.0 + 0.02 * jax.random.normal(ks[1], (D_MODEL,), f32)
    w_in = jax.random.normal(ks[2], (D_MODEL, IN_WIDTH), f32) * D_MODEL ** -0.5
    b_forget = 2.0 + 0.5 * jax.random.normal(ks[3], (B_HEADS,), f32)
    q_norm_a = 1.0 + 0.02 * jax.random.normal(ks[4], (HEAD_DIM,), f32)
    k_norm_a = 1.0 + 0.02 * jax.random.normal(ks[5], (HEAD_DIM,), f32)
    sink_logits = 0.5 * jax.random.normal(ks[6], (A_HEADS,), f32)
    q_norm_b = 1.0 + 0.02 * jax.random.normal(ks[7], (HEAD_DIM,), f32)
    k_norm_b = 1.0 + 0.02 * jax.random.normal(ks[8], (HEAD_DIM,), f32)
    w_out = jax.random.normal(ks[9], (MIX_WIDTH, D_MODEL), f32) * MIX_WIDTH ** -0.5
    mlp_norm_g = 1.0 + 0.02 * jax.random.normal(ks[10], (D_MODEL,), f32)
    w_up = jax.random.normal(ks[11], (D_MODEL, D_FF), f32) * D_MODEL ** -0.5
    w_down = jax.random.normal(ks[12], (D_FF, D_MODEL), f32) * D_FF ** -0.5
    return {"x": x, "attn_norm_g": attn_norm_g, "w_in": w_in, "b_forget": b_forget,
            "q_norm_a": q_norm_a, "k_norm_a": k_norm_a, "sink_logits": sink_logits,
            "q_norm_b": q_norm_b, "k_norm_b": k_norm_b, "w_out": w_out,
            "mlp_norm_g": mlp_norm_g, "w_up": w_up, "w_down": w_down}


def _fwd_reference(x, attn_norm_g, w_in, b_forget, q_norm_a, k_norm_a, sink_logits,
              q_norm_b, k_norm_b, w_out, mlp_norm_g, w_up, w_down):
    b, s, _ = x.shape
    for _layer in range(DEPTH):
        xn = rmsnorm(x, attn_norm_g)
        proj = jnp.einsum('bsd,de->bse', xn, w_in)
        qa, ka, va, qb, kb, vb, f_logit = jnp.split(proj, IN_SPLITS, axis=-1)
        qa = rmsnorm(qa.reshape(b, s, A_HEADS, HEAD_DIM), q_norm_a)
        ka = rmsnorm(ka.reshape(b, s, A_KV_HEADS, HEAD_DIM), k_norm_a)
        va = va.reshape(b, s, A_KV_HEADS, HEAD_DIM)
        out_a = swa_sinks_attention(qa, ka, va, sink_logits)
        qb = rmsnorm(qb.reshape(b, s, B_HEADS, HEAD_DIM), q_norm_b)
        kb = rmsnorm(kb.reshape(b, s, B_HEADS, HEAD_DIM), k_norm_b)
        vb = vb.reshape(b, s, B_HEADS, HEAD_DIM)
        log_f = jax.nn.log_sigmoid(f_logit.astype(jnp.float32) + b_forget.astype(jnp.float32))
        out_b = forgetting_attention(qb, kb, vb, log_f)
        mixed = jnp.concatenate([out_a.reshape(b, s, A_Q_W), out_b.reshape(b, s, B_W)], axis=-1)
        x = x + jnp.einsum('bse,ed->bsd', mixed, w_out)
        hn = rmsnorm(x, mlp_norm_g)
        hid = jnp.square(jax.nn.relu(jnp.einsum('bsd,df->bsf', hn, w_up)))
        x = x + jnp.einsum('bsf,fd->bsd', hid, w_down)
    return x


import jax as _jax
import jax.numpy as _jnp

TWIN_FORMAT = 'train_step'
FWD_PARAMS = ['x', 'attn_norm_g', 'w_in', 'b_forget', 'q_norm_a', 'k_norm_a', 'sink_logits', 'q_norm_b', 'k_norm_b', 'w_out', 'mlp_norm_g', 'w_up', 'w_down']
TWIN_WEIGHTS = ['attn_norm_g', 'w_in', 'b_forget', 'q_norm_a', 'k_norm_a', 'sink_logits', 'q_norm_b', 'k_norm_b', 'w_out', 'mlp_norm_g', 'w_up', 'w_down']
TWIN_DIFF_INPUT = 'x'
TWIN_INPUTS = ['x', 'attn_norm_g', 'w_in', 'b_forget', 'q_norm_a', 'k_norm_a', 'sink_logits', 'q_norm_b', 'k_norm_b', 'w_out', 'mlp_norm_g', 'w_up', 'w_down', 'loss_target', 'm_attn_norm_g', 'm_w_in', 'm_b_forget', 'm_q_norm_a', 'm_k_norm_a', 'm_sink_logits', 'm_q_norm_b', 'm_k_norm_b', 'm_w_out', 'm_mlp_norm_g', 'm_w_up', 'm_w_down', 'v_attn_norm_g', 'v_w_in', 'v_b_forget', 'v_q_norm_a', 'v_k_norm_a', 'v_sink_logits', 'v_q_norm_b', 'v_k_norm_b', 'v_w_out', 'v_mlp_norm_g', 'v_w_up', 'v_w_down']
TWIN_OUTPUTS = ['loss', 'grad_x', 'grad_attn_norm_g', 'grad_w_in', 'grad_b_forget', 'grad_q_norm_a', 'grad_k_norm_a', 'grad_sink_logits', 'grad_q_norm_b', 'grad_k_norm_b', 'grad_w_out', 'grad_mlp_norm_g', 'grad_w_up', 'grad_w_down', 'delta_attn_norm_g', 'delta_w_in', 'delta_b_forget', 'delta_q_norm_a', 'delta_k_norm_a', 'delta_sink_logits', 'delta_q_norm_b', 'delta_k_norm_b', 'delta_w_out', 'delta_mlp_norm_g', 'delta_w_up', 'delta_w_down', 'new_m_attn_norm_g', 'new_m_w_in', 'new_m_b_forget', 'new_m_q_norm_a', 'new_m_k_norm_a', 'new_m_sink_logits', 'new_m_q_norm_b', 'new_m_k_norm_b', 'new_m_w_out', 'new_m_mlp_norm_g', 'new_m_w_up', 'new_m_w_down', 'new_v_attn_norm_g', 'new_v_w_in', 'new_v_b_forget', 'new_v_q_norm_a', 'new_v_k_norm_a', 'new_v_sink_logits', 'new_v_q_norm_b', 'new_v_k_norm_b', 'new_v_w_out', 'new_v_mlp_norm_g', 'new_v_w_up', 'new_v_w_down']
TWIN_LEAF_KINDS = {'loss': 'loss', 'grad_x': 'grad_x', 'grad_attn_norm_g': 'grad_w', 'grad_w_in': 'grad_w', 'grad_b_forget': 'grad_w', 'grad_q_norm_a': 'grad_w', 'grad_k_norm_a': 'grad_w', 'grad_sink_logits': 'grad_w', 'grad_q_norm_b': 'grad_w', 'grad_k_norm_b': 'grad_w', 'grad_w_out': 'grad_w', 'grad_mlp_norm_g': 'grad_w', 'grad_w_up': 'grad_w', 'grad_w_down': 'grad_w', 'delta_attn_norm_g': 'delta_w', 'delta_w_in': 'delta_w', 'delta_b_forget': 'delta_w', 'delta_q_norm_a': 'delta_w', 'delta_k_norm_a': 'delta_w', 'delta_sink_logits': 'delta_w', 'delta_q_norm_b': 'delta_w', 'delta_k_norm_b': 'delta_w', 'delta_w_out': 'delta_w', 'delta_mlp_norm_g': 'delta_w', 'delta_w_up': 'delta_w', 'delta_w_down': 'delta_w', 'new_m_attn_norm_g': 'new_m', 'new_m_w_in': 'new_m', 'new_m_b_forget': 'new_m', 'new_m_q_norm_a': 'new_m', 'new_m_k_norm_a': 'new_m', 'new_m_sink_logits': 'new_m', 'new_m_q_norm_b': 'new_m', 'new_m_k_norm_b': 'new_m', 'new_m_w_out': 'new_m', 'new_m_mlp_norm_g': 'new_m', 'new_m_w_up': 'new_m', 'new_m_w_down': 'new_m', 'new_v_attn_norm_g': 'new_v', 'new_v_w_in': 'new_v', 'new_v_b_forget': 'new_v', 'new_v_q_norm_a': 'new_v', 'new_v_k_norm_a': 'new_v', 'new_v_sink_logits': 'new_v', 'new_v_q_norm_b': 'new_v', 'new_v_k_norm_b': 'new_v', 'new_v_w_out': 'new_v', 'new_v_mlp_norm_g': 'new_v', 'new_v_w_up': 'new_v', 'new_v_w_down': 'new_v'}


def _forward(args):
    return _fwd_reference(*[args[k] for k in FWD_PARAMS])


def _output_shape():
    out = _jax.eval_shape(lambda: _forward(_fwd_setup_inputs(0)))
    return out.shape, out.dtype

N_MICROBATCH = 1
ADAM_LR = 0.001
ADAM_B1 = 0.9
ADAM_B2 = 0.999
ADAM_EPS = 1e-08
ADAM_WD = 0.01
ADAM_STEP = 10
PER_EXAMPLE_BATCH_AXIS = {'x': 0, 'loss_target': 0}
SHARED_INPUTS = []
_WEIGHT_DTYPES = {'attn_norm_g': _jnp.float32, 'w_in': _jnp.float32, 'b_forget': _jnp.float32, 'q_norm_a': _jnp.float32, 'k_norm_a': _jnp.float32, 'sink_logits': _jnp.float32, 'q_norm_b': _jnp.float32, 'k_norm_b': _jnp.float32, 'w_out': _jnp.float32, 'mlp_norm_g': _jnp.float32, 'w_up': _jnp.float32, 'w_down': _jnp.float32}
MOMENT_SCALE = {'attn_norm_g': 7.321469e+00, 'w_in': 4.751112e-01, 'b_forget': 2.141516e+02, 'q_norm_a': 1.923292e+01, 'k_norm_a': 1.911539e+01, 'sink_logits': 6.834999e+01, 'q_norm_b': 2.836298e+01, 'k_norm_b': 2.827914e+01, 'w_out': 5.769792e-01, 'mlp_norm_g': 1.921564e+02, 'w_up': 1.119556e+00, 'w_down': 1.570808e+01}


def _to_microbatches(a, axis):
    t = _jnp.moveaxis(a, axis, 0)
    t = t.reshape((N_MICROBATCH, t.shape[0] // N_MICROBATCH) + t.shape[1:])
    return _jnp.moveaxis(t, 1, axis + 1)


def setup_inputs(seed: int = 0) -> dict:
    inp = _fwd_setup_inputs(seed)
    key = _jax.random.fold_in(_jax.random.key(seed), 7919)
    shape, _ = _output_shape()
    out = dict(inp)
    out["loss_target"] = _jax.random.normal(_jax.random.fold_in(key, 0), shape, _jnp.float32)
    for i, name in enumerate(TWIN_WEIGHTS):
        w = inp[name].astype(_jnp.float32)
        if MOMENT_SCALE is None:
            s = _jnp.sqrt(_jnp.mean(_jnp.square(w)) + 1e-30)
        else:
            s = MOMENT_SCALE[name]
        km, kv = _jax.random.split(_jax.random.fold_in(key, i + 1))
        out[name] = w
        out["m_" + name] = s * _jax.random.normal(km, w.shape, _jnp.float32)
        out["v_" + name] = (s * s) * _jax.random.uniform(kv, w.shape, _jnp.float32, 0.5, 1.5)
    if N_MICROBATCH > 1:
        for name, axis in PER_EXAMPLE_BATCH_AXIS.items():
            out[name] = _to_microbatches(out[name], axis)
    return {'x': out['x'], 'attn_norm_g': out['attn_norm_g'], 'w_in': out['w_in'], 'b_forget': out['b_forget'], 'q_norm_a': out['q_norm_a'], 'k_norm_a': out['k_norm_a'], 'sink_logits': out['sink_logits'], 'q_norm_b': out['q_norm_b'], 'k_norm_b': out['k_norm_b'], 'w_out': out['w_out'], 'mlp_norm_g': out['mlp_norm_g'], 'w_up': out['w_up'], 'w_down': out['w_down'], 'loss_target': out['loss_target'], 'm_attn_norm_g': out['m_attn_norm_g'], 'm_w_in': out['m_w_in'], 'm_b_forget': out['m_b_forget'], 'm_q_norm_a': out['m_q_norm_a'], 'm_k_norm_a': out['m_k_norm_a'], 'm_sink_logits': out['m_sink_logits'], 'm_q_norm_b': out['m_q_norm_b'], 'm_k_norm_b': out['m_k_norm_b'], 'm_w_out': out['m_w_out'], 'm_mlp_norm_g': out['m_mlp_norm_g'], 'm_w_up': out['m_w_up'], 'm_w_down': out['m_w_down'], 'v_attn_norm_g': out['v_attn_norm_g'], 'v_w_in': out['v_w_in'], 'v_b_forget': out['v_b_forget'], 'v_q_norm_a': out['v_q_norm_a'], 'v_k_norm_a': out['v_k_norm_a'], 'v_sink_logits': out['v_sink_logits'], 'v_q_norm_b': out['v_q_norm_b'], 'v_k_norm_b': out['v_k_norm_b'], 'v_w_out': out['v_w_out'], 'v_mlp_norm_g': out['v_mlp_norm_g'], 'v_w_up': out['v_w_up'], 'v_w_down': out['v_w_down']}


def _loss(weights, diff, rest, loss_target):
    with _jax.named_scope("forward"):
        args = {**rest, TWIN_DIFF_INPUT: diff, **{k: w.astype(_WEIGHT_DTYPES[k]) for k, w in weights.items()}}
        y = _forward(args)
    with _jax.named_scope("loss_head"):
        err = _jnp.square(y.astype(_jnp.float32) - loss_target)
        return 0.5 * _jnp.sum(_jnp.mean(err, axis=-1)) if err.ndim else 0.5 * err


def _adamw(w, g, m, v):
    m = ADAM_B1 * m + (1.0 - ADAM_B1) * g
    v = ADAM_B2 * v + (1.0 - ADAM_B2) * _jnp.square(g)
    m_hat = m / (1.0 - ADAM_B1 ** ADAM_STEP)
    v_hat = v / (1.0 - ADAM_B2 ** ADAM_STEP)
    delta = -ADAM_LR * (m_hat / (_jnp.sqrt(v_hat) + ADAM_EPS) + ADAM_WD * w)
    return delta, m, v


def reference(x, attn_norm_g, w_in, b_forget, q_norm_a, k_norm_a, sink_logits, q_norm_b, k_norm_b, w_out, mlp_norm_g, w_up, w_down, loss_target, m_attn_norm_g, m_w_in, m_b_forget, m_q_norm_a, m_k_norm_a, m_sink_logits, m_q_norm_b, m_k_norm_b, m_w_out, m_mlp_norm_g, m_w_up, m_w_down, v_attn_norm_g, v_w_in, v_b_forget, v_q_norm_a, v_k_norm_a, v_sink_logits, v_q_norm_b, v_k_norm_b, v_w_out, v_mlp_norm_g, v_w_up, v_w_down):
    given = dict(x=x, attn_norm_g=attn_norm_g, w_in=w_in, b_forget=b_forget, q_norm_a=q_norm_a, k_norm_a=k_norm_a, sink_logits=sink_logits, q_norm_b=q_norm_b, k_norm_b=k_norm_b, w_out=w_out, mlp_norm_g=mlp_norm_g, w_up=w_up, w_down=w_down, loss_target=loss_target, m_attn_norm_g=m_attn_norm_g, m_w_in=m_w_in, m_b_forget=m_b_forget, m_q_norm_a=m_q_norm_a, m_k_norm_a=m_k_norm_a, m_sink_logits=m_sink_logits, m_q_norm_b=m_q_norm_b, m_k_norm_b=m_k_norm_b, m_w_out=m_w_out, m_mlp_norm_g=m_mlp_norm_g, m_w_up=m_w_up, m_w_down=m_w_down, v_attn_norm_g=v_attn_norm_g, v_w_in=v_w_in, v_b_forget=v_b_forget, v_q_norm_a=v_q_norm_a, v_k_norm_a=v_k_norm_a, v_sink_logits=v_sink_logits, v_q_norm_b=v_q_norm_b, v_k_norm_b=v_k_norm_b, v_w_out=v_w_out, v_mlp_norm_g=v_mlp_norm_g, v_w_up=v_w_up, v_w_down=v_w_down)
    weights = {n: given[n] for n in TWIN_WEIGHTS}
    shared = {n: given[n] for n in SHARED_INPUTS}
    per_example = {n: given[n] for n in ['x']}
    grad_fn = _jax.value_and_grad(_loss, argnums=(0, 1))

    def one_microbatch(ex, loss_target):
        ex = dict(ex)
        diff = ex.pop(TWIN_DIFF_INPUT)
        return grad_fn(weights, diff, {**shared, **ex}, loss_target)

    if N_MICROBATCH == 1:
        loss, (grad_w, grad_x) = one_microbatch(per_example, given["loss_target"])
    else:
        def body(carry, xs):
            loss_sum, grad_sum = carry
            l_k, (gw_k, gx_k) = one_microbatch(xs[0], xs[1])
            with _jax.named_scope("update"):
                return (loss_sum + l_k, _jax.tree.map(_jnp.add, grad_sum, gw_k)), gx_k

        init = (_jnp.zeros((), _jnp.float32), _jax.tree.map(_jnp.zeros_like, weights))
        (loss, grad_w), grad_x = _jax.lax.scan(body, init, (per_example, given["loss_target"]))
    with _jax.named_scope("update"):
        delta_w, new_m, new_v = {}, {}, {}
        for n in TWIN_WEIGHTS:
            delta_w[n], new_m[n], new_v[n] = _adamw(weights[n], grad_w[n], given["m_" + n], given["v_" + n])
    return (loss, grad_x, *[grad_w[n] for n in TWIN_WEIGHTS], *[delta_w[n] for n in TWIN_WEIGHTS],
            *[new_m[n] for n in TWIN_WEIGHTS], *[new_v[n] for n in TWIN_WEIGHTS])
```

```python
import functools

import numpy as np
import jax
import jax.numpy as jnp
from jax import lax
from jax.experimental import pallas as pl
from jax.experimental.pallas import tpu as pltpu

F32 = jnp.float32
BF16 = jnp.bfloat16

D_MODEL = 1024
HEAD_DIM = 64
LANES = 128
A_HEADS = 8
A_KV_HEADS = 2
A_GROUP = A_HEADS // A_KV_HEADS
B_HEADS = 8
WINDOW = 128
D_FF = 4096
IN_WIDTH = 2312
EPS = 1e-6
SCALE = 0.125
NEG = -1e30

G_QA, G_KA, G_QB, G_KB, G_VA, G_VB, G_F = 0, 8, 10, 18, 26, 28, 36
N_NORM_GROUPS = 26
N_GROUPS = 38
NP = N_GROUPS * LANES
MIXED_P = (A_HEADS + B_HEADS) * LANES

N_CHIPS = 4
IN_SHARD = IN_WIDTH // N_CHIPS
IN_SHARD_P = 608
R_OUT = IN_SHARD_P
R_UP = R_OUT + D_MODEL // N_CHIPS
R_DOWN = R_UP + D_MODEL
R_ALL = R_DOWN + D_FF // N_CHIPS
R_HALF = R_ALL // 2

SMALL_ROWS = 24
ROW_LOSS = 22

ADAM_LR = 0.001
ADAM_B1 = 0.9
ADAM_B2 = 0.999
ADAM_EPS = 1e-08
ADAM_WD = 0.01
ADAM_STEP = 10

VMEM_LIMIT = 52 * 1024 * 1024
MESH = pl.DeviceIdType.MESH


def _pcall(body, **kw):
    return pl.pallas_call(body, **kw)


def _params(sem=None):
    return pltpu.CompilerParams(dimension_semantics=sem, vmem_limit_bytes=VMEM_LIMIT)


def _dot(a, b):
    return jnp.dot(a, b, preferred_element_type=F32)


def _dot_nt(a, b):
    return lax.dot_general(a, b, (((1,), (1,)), ((), ())), preferred_element_type=F32)


def _dot_tn(a, b):
    return lax.dot_general(a, b, (((0,), (0,)), ((), ())), preferred_element_type=F32)


def _split3(x):
    hi = x.astype(BF16)
    r1 = x - hi.astype(F32)
    mid = r1.astype(BF16)
    lo = (r1 - mid.astype(F32)).astype(BF16)
    return hi, mid, lo


def _dot_exact(mat, x):
    hi, mid, lo = _split3(x)
    return _dot(mat, lo) + _dot(mat, mid) + _dot(mat, hi)


def _const(shape):
    zeros = (0,) * len(shape)
    return pl.BlockSpec(shape, lambda *_: zeros)


def _rows(tm, n):
    return pl.BlockSpec((tm, n), lambda i: (i, 0))


def _aug_select():
    e = np.zeros((3 * LANES, 2 * B_HEADS * LANES), np.float32)
    for j in range(3):
        for h in range(B_HEADS):
            e[j * LANES + h, h * LANES + HEAD_DIM + j] = 1.0
            e[j * LANES + h, (B_HEADS + h) * LANES + HEAD_DIM + 3 + j] = -1.0
    return jnp.asarray(e, BF16)


def _dc_select():
    e = np.zeros((2 * B_HEADS * LANES, LANES), np.float32)
    for h in range(B_HEADS):
        e[h * LANES + HEAD_DIM, h] = 1.0
        e[(B_HEADS + h) * LANES + HEAD_DIM + 3, h] = -1.0
    return jnp.asarray(e, BF16)


def _tri(n, upper):
    t = np.tril(np.ones((n, n), np.float32))
    return jnp.asarray(t.T if upper else t, BF16)


def _inproj(x2, g1, w_pad, gain_row, b_row, seq):
    t_all = x2.shape[0]
    tm = min(256, seq)
    tiles_per_seq = seq // tm
    tri = _tri(tm, False)
    esel = _aug_select()

    def body(x_ref, g_ref, w_ref, gain_ref, b_ref, tri_ref, e_ref,
             xn_ref, pre_ref, qa_ref, ka_ref, va_ref, qb_ref, kb_ref, vb_ref, z_ref, carry_ref):
        i = pl.program_id(0)

        @pl.when(i % tiles_per_seq == 0)
        def _():
            carry_ref[...] = jnp.zeros_like(carry_ref)

        x = x_ref[...]
        r = lax.rsqrt(jnp.mean(x * x, axis=-1, keepdims=True) + EPS)
        xn = (x * r * g_ref[...]).astype(BF16)
        xn_ref[...] = xn
        proj = _dot(xn, w_ref[...])
        pre_ref[...] = proj[:, :N_NORM_GROUPS * LANES].astype(BF16)
        lane = lax.broadcasted_iota(jnp.int32, (tm, LANES), 1)

        z = proj[:, G_F * LANES:(G_F + 1) * LANES] + b_ref[...]
        z_ref[...] = z
        lf = jnp.minimum(z, 0.0) - jnp.log(1.0 + jnp.exp(-jnp.abs(z)))
        lf = jnp.where(lane < B_HEADS, lf, 0.0)
        c = _dot_exact(tri_ref[...], lf) + carry_ref[...]
        carry_ref[...] += jnp.sum(lf, axis=0, keepdims=True)
        aug = _dot(jnp.concatenate(_split3(c), axis=1), e_ref[...])

        def hnorm(g):
            p = proj[:, g * LANES:(g + 1) * LANES]
            rr = lax.rsqrt(jnp.sum(p * p, axis=-1, keepdims=True) * (1.0 / HEAD_DIM) + EPS)
            return p * rr * gain_ref[:, g * LANES:(g + 1) * LANES]

        ones_q = jnp.where((lane >= HEAD_DIM + 3) & (lane < HEAD_DIM + 6), 1.0, 0.0)
        ones_k = jnp.where((lane >= HEAD_DIM) & (lane < HEAD_DIM + 3), 1.0, 0.0)
        for h in range(A_HEADS):
            qa_ref[:, h * LANES:(h + 1) * LANES] = (hnorm(G_QA + h) * SCALE).astype(BF16)
        for h in range(A_KV_HEADS):
            ka_ref[:, h * LANES:(h + 1) * LANES] = hnorm(G_KA + h).astype(BF16)
        for h in range(B_HEADS):
            qb_ref[:, h * LANES:(h + 1) * LANES] = (
                hnorm(G_QB + h) * SCALE + aug[:, h * LANES:(h + 1) * LANES] + ones_q).astype(BF16)
            kb_ref[:, h * LANES:(h + 1) * LANES] = (
                hnorm(G_KB + h) + aug[:, (B_HEADS + h) * LANES:(B_HEADS + h + 1) * LANES] + ones_k).astype(BF16)
        va_ref[...] = proj[:, G_VA * LANES:G_VB * LANES].astype(BF16)
        vb_ref[...] = proj[:, G_VB * LANES:G_F * LANES].astype(BF16)

    widths = [(D_MODEL, BF16), (N_NORM_GROUPS * LANES, BF16), (A_HEADS * LANES, BF16), (A_KV_HEADS * LANES, BF16),
              (A_KV_HEADS * LANES, BF16), (B_HEADS * LANES, BF16), (B_HEADS * LANES, BF16), (B_HEADS * LANES, BF16),
              (LANES, F32)]
    return _pcall(
        body, name="inproj", grid=(t_all // tm,),
        in_specs=[_rows(tm, D_MODEL), _const((1, D_MODEL)), _const((D_MODEL, NP)), _const((1, NP)),
                  _const((1, LANES)), _const((tm, tm)), _const(esel.shape)],
        out_specs=[_rows(tm, w) for w, _ in widths],
        out_shape=[jax.ShapeDtypeStruct((t_all, w), dt) for w, dt in widths],
        scratch_shapes=[pltpu.VMEM((1, LANES), F32)],
        compiler_params=_params(("arbitrary",)),
    )(x2, g1, w_pad, gain_row, b_row, tri, esel)


def _fox_fwd(qb, kb, vb, nb, seq):
    t_all = qb.shape[0]
    tq = min(512, seq)
    nq = seq // tq

    def body(q_ref, k_ref, v_ref, o_ref, lse_ref):
        qi = pl.program_id(2)
        q = q_ref[...]

        def step(j, carry, masked):
            m, l, acc = carry
            off = pl.multiple_of(j * tq, tq)
            k = k_ref[pl.ds(off, tq), :]
            v = v_ref[pl.ds(off, tq), :]
            s = _dot_nt(q, k)
            if masked:
                row = lax.broadcasted_iota(jnp.int32, (tq, tq), 0)
                col = lax.broadcasted_iota(jnp.int32, (tq, tq), 1)
                s = jnp.where(row >= col, s, NEG)
            m_new = jnp.maximum(m, jnp.max(s, axis=-1, keepdims=True))
            alpha = jnp.exp(m - m_new)
            p = jnp.exp(s - m_new)
            l = alpha * l + jnp.sum(p, axis=-1, keepdims=True)
            acc = alpha * acc + _dot(p.astype(BF16), v)
            return m_new, l, acc

        init = (jnp.full((tq, 1), NEG, F32), jnp.zeros((tq, 1), F32), jnp.zeros((tq, LANES), F32))
        carry = lax.fori_loop(0, qi, lambda j, cr: step(j, cr, False), init)
        m, l, acc = step(qi, carry, True)
        o_ref[...] = (acc / l).astype(BF16)
        lse_ref[...] = jnp.broadcast_to(m + jnp.log(l), (tq, LANES))

    qspec = pl.BlockSpec((tq, LANES), lambda b, h, i: (b * nq + i, h))
    kspec = pl.BlockSpec((seq, LANES), lambda b, h, i: (b, h))
    return _pcall(
        body, name="fox_fwd", grid=(nb, B_HEADS, nq),
        in_specs=[qspec, kspec, kspec], out_specs=[qspec, qspec],
        out_shape=[jax.ShapeDtypeStruct((t_all, B_HEADS * LANES), BF16),
                   jax.ShapeDtypeStruct((t_all, B_HEADS * LANES), F32)],
        compiler_params=_params(("parallel", "parallel", "arbitrary")),
    )(qb, kb, vb)


def _swa_scores(q, k, t_rel, slope):
    s = _dot_nt(q, k)
    row = lax.broadcasted_iota(jnp.int32, (WINDOW, 2 * WINDOW), 0)
    col = lax.broadcasted_iota(jnp.int32, (WINDOW, 2 * WINDOW), 1)
    dist = t_rel + row - col
    valid = (dist >= 0) & (dist < WINDOW)
    return jnp.where(valid, s - slope * dist.astype(F32), NEG), valid


def _swa_fwd(qa, ka, va, sinks, slopes, nb, seq):
    t_all = qa.shape[0]
    tq = min(512, seq)
    nq = seq // tq

    def body(sink_ref, slope_ref, q_ref, k_ref, v_ref, o_ref, l_ref):
        h = pl.program_id(1)
        qi = pl.program_id(2)
        sink = sink_ref[h]
        slope = slope_ref[h]
        for a in range(tq // WINDOW):
            t0 = qi * tq + a * WINDOW
            start = pl.multiple_of(jnp.maximum(t0 - WINDOW, 0), WINDOW)
            rows = slice(a * WINDOW, (a + 1) * WINDOW)
            k = k_ref[pl.ds(start, 2 * WINDOW), :]
            v = v_ref[pl.ds(start, 2 * WINDOW), :]
            s, _ = _swa_scores(q_ref[rows, :], k, t0 - start, slope)
            m = jnp.maximum(jnp.max(s, axis=-1, keepdims=True), sink)
            p = jnp.exp(s - m)
            den = jnp.sum(p, axis=-1, keepdims=True) + jnp.exp(sink - m)
            o_ref[rows, :] = _dot((p / den).astype(BF16), v).astype(BF16)
            l_ref[rows, :] = jnp.broadcast_to(m + jnp.log(den), (WINDOW, LANES))

    smem = pl.BlockSpec(memory_space=pltpu.SMEM)
    qspec = pl.BlockSpec((tq, LANES), lambda b, h, i: (b * nq + i, h))
    kspec = pl.BlockSpec((seq, LANES), lambda b, h, i: (b, h // A_GROUP))
    return _pcall(
        body, name="swa_fwd", grid=(nb, A_HEADS, nq),
        in_specs=[smem, smem, qspec, kspec, kspec], out_specs=[qspec, qspec],
        out_shape=[jax.ShapeDtypeStruct((t_all, A_HEADS * LANES), BF16),
                   jax.ShapeDtypeStruct((t_all, A_HEADS * LANES), F32)],
        compiler_params=_params(("parallel", "parallel", "arbitrary")),
    )(sinks, slopes, qa, ka, va)


def _outproj(x2, oa, ob, wo_pad, g2):
    t_all = x2.shape[0]
    tm = min(512, t_all)
    half = A_HEADS * LANES

    def body(x_ref, oa_ref, ob_ref, w_ref, g_ref, h_ref, hn_ref):
        h = x_ref[...] + _dot(oa_ref[...], w_ref[:half, :]) + _dot(ob_ref[...], w_ref[half:, :])
        h_ref[...] = h
        r = lax.rsqrt(jnp.mean(h * h, axis=-1, keepdims=True) + EPS)
        hn_ref[...] = (h * r * g_ref[...]).astype(BF16)

    return _pcall(
        body, name="outproj", grid=(t_all // tm,),
        in_specs=[_rows(tm, D_MODEL), _rows(tm, half), _rows(tm, half), _const((MIXED_P, D_MODEL)),
                  _const((1, D_MODEL))],
        out_specs=[_rows(tm, D_MODEL), _rows(tm, D_MODEL)],
        out_shape=[jax.ShapeDtypeStruct((t_all, D_MODEL), F32), jax.ShapeDtypeStruct((t_all, D_MODEL), BF16)],
        compiler_params=_params(("parallel",)),
    )(x2, oa, ob, wo_pad, g2)


def _mlp_up(hn, w_up_blocks):
    t_all = hn.shape[0]
    tm = min(512, t_all)
    nj = D_FF // D_MODEL

    def body(a_ref, w_ref, ru_ref, hid_ref):
        ru = jnp.maximum(_dot(a_ref[...], w_ref[...]), 0.0)
        ru_ref[...] = ru.astype(BF16)
        hid_ref[...] = (ru * ru).astype(BF16)

    ospec = pl.BlockSpec((tm, D_MODEL), lambda j, i: (i, j))
    return _pcall(
        body, name="mlp_up", grid=(nj, t_all // tm),
        in_specs=[pl.BlockSpec((tm, D_MODEL), lambda j, i: (i, 0)),
                  pl.BlockSpec((None, D_MODEL, D_MODEL), lambda j, i: (j, 0, 0))],
        out_specs=[ospec, ospec],
        out_shape=[jax.ShapeDtypeStruct((t_all, D_FF), BF16), jax.ShapeDtypeStruct((t_all, D_FF), BF16)],
        compiler_params=_params(("parallel", "parallel")),
    )(hn, w_up_blocks)


def _mlp_down(hid, w_down, h, tgt):
    t_all = h.shape[0]
    tm = min(256, t_all)

    def body(a_ref, w_ref, h_ref, t_ref, dy_ref, dyb_ref, loss_ref):
        @pl.when(pl.program_id(0) == 0)
        def _():
            loss_ref[...] = jnp.zeros_like(loss_ref)

        y = h_ref[...] + _dot(a_ref[...], w_ref[...])
        err = y - t_ref[...]
        loss_ref[...] += jnp.sum(err * err)
        dy = err * (1.0 / D_MODEL)
        dy_ref[...] = dy
        dyb_ref[...] = dy.astype(BF16)

    return _pcall(
        body, name="mlp_down", grid=(t_all // tm,),
        in_specs=[_rows(tm, D_FF), _const((D_FF, D_MODEL)), _rows(tm, D_MODEL), _rows(tm, D_MODEL)],
        out_specs=[_rows(tm, D_MODEL), _rows(tm, D_MODEL), _const((8, LANES))],
        out_shape=[jax.ShapeDtypeStruct((t_all, D_MODEL), F32), jax.ShapeDtypeStruct((t_all, D_MODEL), BF16),
                   jax.ShapeDtypeStruct((8, LANES), F32)],
        compiler_params=_params(("arbitrary",)),
    )(hid, w_down, h, tgt)


def _mlp_dhid(dyb, w_down_t, ru):
    t_all = dyb.shape[0]
    tm = min(512, t_all)
    nj = D_FF // D_MODEL

    def body(a_ref, w_ref, ru_ref, du_ref):
        du_ref[...] = (_dot(a_ref[...], w_ref[...]) * (2.0 * ru_ref[...].astype(F32))).astype(BF16)

    ospec = pl.BlockSpec((tm, D_MODEL), lambda j, i: (i, j))
    return _pcall(
        body, name="mlp_dhid", grid=(nj, t_all // tm),
        in_specs=[pl.BlockSpec((tm, D_MODEL), lambda j, i: (i, 0)),
                  pl.BlockSpec((D_MODEL, D_MODEL), lambda j, i: (0, j)), ospec],
        out_specs=ospec,
        out_shape=jax.ShapeDtypeStruct((t_all, D_FF), BF16),
        compiler_params=_params(("parallel", "parallel")),
    )(dyb, w_down_t, ru)


def _wgrad(a, b, name, a_col_blocks, b_col_blocks):
    t_all = a.shape[0]
    tt = min(512, t_all)
    nj = max(a_col_blocks, b_col_blocks)
    wa = a.shape[1] // a_col_blocks
    wb = b.shape[1] // b_col_blocks

    def body(a_ref, b_ref, o_ref):
        @pl.when(pl.program_id(1) == 0)
        def _():
            o_ref[...] = jnp.zeros_like(o_ref)

        o_ref[...] += _dot_tn(a_ref[...], b_ref[...])

    return _pcall(
        body, name=name, grid=(nj, t_all // tt),
        in_specs=[pl.BlockSpec((tt, wa), lambda j, t: (t, j if a_col_blocks > 1 else 0)),
                  pl.BlockSpec((tt, wb), lambda j, t: (t, j if b_col_blocks > 1 else 0))],
        out_specs=pl.BlockSpec((None, wa, wb), lambda j, t: (j, 0, 0)),
        out_shape=jax.ShapeDtypeStruct((nj, wa, wb), F32),
        compiler_params=_params(("parallel", "arbitrary")),
    )(a, b)


def _mlp_dhn(du, w_up_t, h, dy, g2):
    t_all = h.shape[0]
    tm = min(256, t_all)

    def body(a_ref, w_ref, h_ref, dy_ref, g_ref, dh_ref, dhb_ref, dg_ref):
        @pl.when(pl.program_id(0) == 0)
        def _():
            dg_ref[...] = jnp.zeros_like(dg_ref)

        dhn = _dot(a_ref[...], w_ref[...])
        h = h_ref[...]
        r = lax.rsqrt(jnp.mean(h * h, axis=-1, keepdims=True) + EPS)
        hh = h * r
        dg_ref[...] += jnp.sum(dhn * hh, axis=0, keepdims=True)
        dz = dhn * g_ref[...]
        dh = dy_ref[...] + r * (dz - hh * jnp.mean(dz * hh, axis=-1, keepdims=True))
        dh_ref[...] = dh
        dhb_ref[...] = dh.astype(BF16)

    return _pcall(
        body, name="mlp_dhn", grid=(t_all // tm,),
        in_specs=[_rows(tm, D_FF), _const((D_FF, D_MODEL)), _rows(tm, D_MODEL), _rows(tm, D_MODEL),
                  _const((1, D_MODEL))],
        out_specs=[_rows(tm, D_MODEL), _rows(tm, D_MODEL), _const((1, D_MODEL))],
        out_shape=[jax.ShapeDtypeStruct((t_all, D_MODEL), F32), jax.ShapeDtypeStruct((t_all, D_MODEL), BF16),
                   jax.ShapeDtypeStruct((1, D_MODEL), F32)],
        compiler_params=_params(("arbitrary",)),
    )(du, w_up_t, h, dy, g2)


def _dmixed(dhb, wo_pad_t):
    t_all = dhb.shape[0]
    tm = min(512, t_all)
    half = A_HEADS * LANES

    def body(a_ref, w_ref, da_ref, db_ref):
        d = _dot(a_ref[...], w_ref[...])
        da_ref[...] = d[:, :half].astype(BF16)
        db_ref[...] = d[:, half:].astype(BF16)

    return _pcall(
        body, name="dmixed", grid=(t_all // tm,),
        in_specs=[_rows(tm, D_MODEL), _const((D_MODEL, MIXED_P))],
        out_specs=[_rows(tm, half), _rows(tm, half)],
        out_shape=[jax.ShapeDtypeStruct((t_all, half), BF16), jax.ShapeDtypeStruct((t_all, half), BF16)],
        compiler_params=_params(("parallel",)),
    )(dhb, wo_pad_t)


def _fox_bwd(qb, kb, vb, ob, dob, lse, nb, seq):
    t_all = qb.shape[0]
    tk = min(512, seq)
    nk = seq // tk

    def body(q_ref, k_ref, v_ref, o_ref, do_ref, lse_ref, dq_ref, dk_ref, dv_ref):
        kj = pl.program_id(2)

        @pl.when(kj == 0)
        def _():
            dq_ref[...] = jnp.zeros_like(dq_ref)

        k = k_ref[...]
        v = v_ref[...]

        def step(i, carry, masked):
            dk, dv = carry
            off = pl.multiple_of(i * tk, tk)
            q = q_ref[pl.ds(off, tk), :]
            do = do_ref[pl.ds(off, tk), :]
            o = o_ref[pl.ds(off, tk), :]
            lse_t = jnp.max(lse_ref[pl.ds(off, tk), :], axis=-1, keepdims=True)
            delta = jnp.sum(do.astype(F32) * o.astype(F32), axis=-1, keepdims=True)
            p = jnp.exp(_dot_nt(q, k) - lse_t)
            if masked:
                row = lax.broadcasted_iota(jnp.int32, (tk, tk), 0)
                col = lax.broadcasted_iota(jnp.int32, (tk, tk), 1)
                p = jnp.where(row >= col, p, 0.0)
            ds = (p * (_dot_nt(do, v) - delta)).astype(BF16)
            dv = dv + _dot_tn(p.astype(BF16), do)
            dk = dk + _dot_tn(ds, q)
            dq_ref[pl.ds(off, tk), :] += _dot(ds, k)
            return dk, dv

        zero = jnp.zeros((tk, LANES), F32)
        carry = step(kj, (zero, zero), True)
        dk, dv = lax.fori_loop(kj + 1, nk, lambda i, cr: step(i, cr, False), carry)
        dk_ref[...] = dk
        dv_ref[...] = dv

    full = pl.BlockSpec((seq, LANES), lambda b, h, j: (b, h))
    tile = pl.BlockSpec((tk, LANES), lambda b, h, j: (b * nk + j, h))
    shp = jax.ShapeDtypeStruct((t_all, B_HEADS * LANES), F32)
    return _pcall(
        body, name="fox_bwd", grid=(nb, B_HEADS, nk),
        in_specs=[full, tile, tile, full, full, full], out_specs=[full, tile, tile],
        out_shape=[shp, shp, shp],
        compiler_params=_params(("parallel", "parallel", "arbitrary")),
    )(qb, kb, vb, ob, dob, lse)


def _swa_bwd(qa, ka, va, oa, doa, lrow, sinks, slopes, nb, seq):
    t_all = qa.shape[0]
    tq = min(512, seq)
    nq = seq // tq

    def body(sink_ref, slope_ref, q_ref, k_ref, v_ref, o_ref, do_ref, l_ref, dq_ref, dk_ref, dv_ref, dsink_ref):
        h = pl.program_id(1)
        qi = pl.program_id(2)
        sink = sink_ref[h]
        slope = slope_ref[h]

        @pl.when((h % A_GROUP == 0) & (qi == 0))
        def _():
            dk_ref[...] = jnp.zeros_like(dk_ref)
            dv_ref[...] = jnp.zeros_like(dv_ref)

        @pl.when(qi == 0)
        def _():
            dsink_ref[...] = jnp.zeros_like(dsink_ref)

        dsink = jnp.zeros((1, 1), F32)
        for a in range(tq // WINDOW):
            t0 = qi * tq + a * WINDOW
            start = pl.multiple_of(jnp.maximum(t0 - WINDOW, 0), WINDOW)
            rows = slice(a * WINDOW, (a + 1) * WINDOW)
            win = pl.ds(start, 2 * WINDOW)
            q = q_ref[rows, :]
            k = k_ref[win, :]
            v = v_ref[win, :]
            do = do_ref[rows, :]
            s, valid = _swa_scores(q, k, t0 - start, slope)
            lrow_t = jnp.max(l_ref[rows, :], axis=-1, keepdims=True)
            p = jnp.where(valid, jnp.exp(s - lrow_t), 0.0)
            delta = jnp.sum(do.astype(F32) * o_ref[rows, :].astype(F32), axis=-1, keepdims=True)
            ds = (p * (_dot_nt(do, v) - delta)).astype(BF16)
            dq_ref[rows, :] = _dot(ds, k)
            dk_ref[win, :] += _dot_tn(ds, q)
            dv_ref[win, :] += _dot_tn(p.astype(BF16), do)
            dsink = dsink - jnp.sum(jnp.exp(sink - lrow_t) * delta, axis=0, keepdims=True)
        dsink_ref[...] += jnp.broadcast_to(dsink, (1, LANES))

    smem = pl.BlockSpec(memory_space=pltpu.SMEM)
    qspec = pl.BlockSpec((tq, LANES), lambda b, h, i: (b * nq + i, h))
    kspec = pl.BlockSpec((seq, LANES), lambda b, h, i: (b, h // A_GROUP))
    return _pcall(
        body, name="swa_bwd", grid=(nb, A_HEADS, nq),
        in_specs=[smem, smem, qspec, kspec, kspec, qspec, qspec, qspec],
        out_specs=[qspec, kspec, kspec, pl.BlockSpec((None, 1, LANES), lambda b, h, i: (b * A_HEADS + h, 0, 0))],
        out_shape=[jax.ShapeDtypeStruct((t_all, A_HEADS * LANES), F32),
                   jax.ShapeDtypeStruct((t_all, A_KV_HEADS * LANES), F32),
                   jax.ShapeDtypeStruct((t_all, A_KV_HEADS * LANES), F32),
                   jax.ShapeDtypeStruct((nb * A_HEADS, 1, LANES), F32)],
        compiler_params=_params(("arbitrary", "arbitrary", "arbitrary")),
    )(sinks, slopes, qa, ka, va, oa, doa, lrow)


def _dproj(pre, dqa, dka, dqb, dkb, dva, dvb, z, gain_row, seq):
    t_all = pre.shape[0]
    tm = min(256, seq)
    nt = t_all // tm
    tiles_per_seq = seq // tm
    triu = _tri(tm, True)
    sel = _dc_select()

    def body(pre_ref, dqa_ref, dka_ref, dqb_ref, dkb_ref, dva_ref, dvb_ref, z_ref, gain_ref, triu_ref, sel_ref,
             dproj_ref, small_ref, carry_ref):
        i = pl.program_id(0)

        @pl.when(i == 0)
        def _():
            small_ref[...] = jnp.zeros_like(small_ref)

        @pl.when(i % tiles_per_seq == 0)
        def _():
            carry_ref[...] = jnp.zeros_like(carry_ref)

        def norm_bwd(g, dhat):
            cols = slice(g * LANES, (g + 1) * LANES)
            p = pre_ref[:, cols].astype(F32)
            rr = lax.rsqrt(jnp.sum(p * p, axis=-1, keepdims=True) * (1.0 / HEAD_DIM) + EPS)
            n = p * rr
            dz = dhat * gain_ref[:, cols]
            dproj_ref[:, cols] = (rr * (dz - n * (jnp.sum(dz * n, axis=-1, keepdims=True) * (1.0 / HEAD_DIM)))
                                  ).astype(BF16)
            return jnp.sum(dhat * n, axis=0, keepdims=True)

        def group_sum(g0, d_ref, count, scale):
            acc = jnp.zeros((1, LANES), F32)
            for h in range(count):
                d = d_ref[:, h * LANES:(h + 1) * LANES]
                acc = acc + norm_bwd(g0 + h, d * scale if scale != 1.0 else d)
            return acc

        small_ref[0:1, :] += group_sum(G_QA, dqa_ref, A_HEADS, SCALE)
        small_ref[1:2, :] += group_sum(G_KA, dka_ref, A_KV_HEADS, 1.0)
        small_ref[2:3, :] += group_sum(G_QB, dqb_ref, B_HEADS, SCALE)
        small_ref[3:4, :] += group_sum(G_KB, dkb_ref, B_HEADS, 1.0)
        dproj_ref[:, G_VA * LANES:G_VB * LANES] = dva_ref[...].astype(BF16)
        dproj_ref[:, G_VB * LANES:G_F * LANES] = dvb_ref[...].astype(BF16)

        dc = jnp.zeros((tm, LANES), F32)
        for piece_q, piece_k in zip(_split3(dqb_ref[...]), _split3(dkb_ref[...])):
            dc = dc + _dot(jnp.concatenate([piece_q, piece_k], axis=1), sel_ref[...])
        dlf = _dot_exact(triu_ref[...], dc) + carry_ref[...]
        carry_ref[...] += jnp.sum(dc, axis=0, keepdims=True)
        dz = dlf / (1.0 + jnp.exp(z_ref[...]))
        small_ref[4:5, :] += jnp.sum(dz, axis=0, keepdims=True)
        dproj_ref[:, G_F * LANES:(G_F + 1) * LANES] = dz.astype(BF16)
        dproj_ref[:, (G_F + 1) * LANES:] = jnp.zeros((tm, LANES), BF16)

    def rev(n):
        return pl.BlockSpec((tm, n), lambda i: (nt - 1 - i, 0))

    return _pcall(
        body, name="dproj", grid=(nt,),
        in_specs=[rev(N_NORM_GROUPS * LANES), rev(A_HEADS * LANES), rev(A_KV_HEADS * LANES), rev(B_HEADS * LANES),
                  rev(B_HEADS * LANES), rev(A_KV_HEADS * LANES), rev(B_HEADS * LANES), rev(LANES),
                  _const((1, NP)), _const((tm, tm)), _const(sel.shape)],
        out_specs=[rev(NP), _const((8, LANES))],
        out_shape=[jax.ShapeDtypeStruct((t_all, NP), BF16), jax.ShapeDtypeStruct((8, LANES), F32)],
        scratch_shapes=[pltpu.VMEM((1, LANES), F32)],
        compiler_params=_params(("arbitrary",)),
    )(pre, dqa, dka, dqb, dkb, dva, dvb, z, gain_row, triu, sel)


def _dx(dproj, w_pad_t, x2, dh, g1):
    t_all = x2.shape[0]
    tm = min(256, t_all)

    def body(a_ref, w_ref, x_ref, dh_ref, g_ref, dx_ref, dg_ref):
        @pl.when(pl.program_id(0) == 0)
        def _():
            dg_ref[...] = jnp.zeros_like(dg_ref)

        dxn = _dot(a_ref[...], w_ref[...])
        x = x_ref[...]
        r = lax.rsqrt(jnp.mean(x * x, axis=-1, keepdims=True) + EPS)
        xh = x * r
        dg_ref[...] += jnp.sum(dxn * xh, axis=0, keepdims=True)
        dz = dxn * g_ref[...]
        dx_ref[...] = dh_ref[...] + r * (dz - xh * jnp.mean(dz * xh, axis=-1, keepdims=True))

    return _pcall(
        body, name="dx", grid=(t_all // tm,),
        in_specs=[_rows(tm, NP), _const((NP, D_MODEL)), _rows(tm, D_MODEL), _rows(tm, D_MODEL), _const((1, D_MODEL))],
        out_specs=[_rows(tm, D_MODEL), _const((1, D_MODEL))],
        out_shape=[jax.ShapeDtypeStruct((t_all, D_MODEL), F32), jax.ShapeDtypeStruct((1, D_MODEL), F32)],
        compiler_params=_params(("arbitrary",)),
    )(dproj, w_pad_t, x2, dh, g1)


def _dwin(dproj, xn):
    t_all = xn.shape[0]
    tt = min(512, t_all)
    half = NP // 2

    def body(a_ref, b_ref, o_ref):
        @pl.when(pl.program_id(1) == 0)
        def _():
            o_ref[...] = jnp.zeros_like(o_ref)

        o_ref[...] += _dot_tn(a_ref[...], b_ref[...])

    return _pcall(
        body, name="dwin", grid=(2, t_all // tt),
        in_specs=[pl.BlockSpec((tt, half), lambda j, t: (t, j)), pl.BlockSpec((tt, D_MODEL), lambda j, t: (t, 0))],
        out_specs=pl.BlockSpec((half, D_MODEL), lambda j, t: (j, 0)),
        out_shape=jax.ShapeDtypeStruct((NP, D_MODEL), F32),
        compiler_params=_params(("parallel", "arbitrary")),
    )(dproj, xn)


ANY = pl.BlockSpec(memory_space=pl.ANY)


def _place():
    return lax.axis_index("x"), lax.axis_index("y"), lax.axis_index("c")


def _allgather_halves(mine):
    m_per, n = mine.shape

    def body(x_ref, out_ref, send_sems, recv_sems, local_sem):
        x, y, c = _place()
        me, sibling = (x, y, c), (x, y, 1 - c)
        chips = [(1 - x, y), (x, 1 - y), (1 - x, 1 - y)]

        def rows(px, py, pc):
            return out_ref.at[pl.ds((4 * px + 2 * py + pc) * m_per, m_per), :]

        def copy(k, block, to, src=None):
            return pltpu.make_async_remote_copy(
                src_ref=rows(*block) if src is None else src, dst_ref=rows(*block),
                send_sem=send_sems.at[k], recv_sem=recv_sems.at[k], device_id=to, device_id_type=MESH)

        own = pltpu.make_async_copy(x_ref, rows(*me), local_sem)
        own.start()
        first = [copy(0, me, sibling, src=x_ref)]
        first += [copy(1 + j, me, (*chip, c), src=x_ref) for j, chip in enumerate(chips)]
        for cp in first:
            cp.start()
        passed = [copy(4 + j, (*chip, c), sibling) for j, chip in enumerate(chips)]
        for j, chip in enumerate(chips):
            copy(1 + j, (*chip, c), me).wait_recv()
            passed[j].start()
        copy(0, sibling, me).wait_recv()
        for j, chip in enumerate(chips):
            copy(4 + j, (*chip, 1 - c), me).wait_recv()
        for cp in first + passed:
            cp.wait_send()
        own.wait()

    return _pcall(
        body, name="allgather_weights",
        out_shape=jax.ShapeDtypeStruct((8 * m_per, n), mine.dtype),
        in_specs=[ANY], out_specs=ANY,
        scratch_shapes=[pltpu.SemaphoreType.DMA((7,)), pltpu.SemaphoreType.DMA((7,)), pltpu.SemaphoreType.DMA],
    )(mine)


def _rs_pair_exchange(g4):
    def body(g_ref, out_ref, send_sem, recv_sem):
        x, y, c = _place()
        cp = pltpu.make_async_remote_copy(
            src_ref=g_ref.at[:, 1 - c], dst_ref=out_ref, send_sem=send_sem, recv_sem=recv_sem,
            device_id=(x, y, 1 - c), device_id_type=MESH)
        cp.start()
        cp.wait()

    return _pcall(
        body, name="rs_pair_exchange",
        out_shape=jax.ShapeDtypeStruct((N_CHIPS, R_HALF, D_MODEL), F32),
        in_specs=[ANY], out_specs=ANY,
        scratch_shapes=[pltpu.SemaphoreType.DMA, pltpu.SemaphoreType.DMA],
    )(g4)


def _rs_pair_add(g4, got, c_idx):
    def body(c_ref, a_ref, b_ref, o_ref):
        o_ref[...] = a_ref[...] + b_ref[...]

    blk = pl.BlockSpec((None, R_HALF, D_MODEL), lambda s, c_ref: (s, 0, 0))
    return _pcall(
        body, name="rs_pair_add",
        grid_spec=pltpu.PrefetchScalarGridSpec(
            num_scalar_prefetch=1, grid=(N_CHIPS,),
            in_specs=[pl.BlockSpec((None, None, R_HALF, D_MODEL), lambda s, c_ref: (s, c_ref[0], 0, 0)), blk],
            out_specs=blk),
        out_shape=jax.ShapeDtypeStruct((N_CHIPS, R_HALF, D_MODEL), F32),
        compiler_params=_params(("parallel",)),
    )(c_idx, g4, got)


def _rs_chip_exchange(p4):
    def body(p_ref, out_ref, send_sems, recv_sems):
        x, y, c = _place()
        chips = [(1 - x, y), (x, 1 - y), (1 - x, 1 - y)]
        cps = [pltpu.make_async_remote_copy(
            src_ref=p_ref.at[2 * cx + cy], dst_ref=out_ref.at[j], send_sem=send_sems.at[j],
            recv_sem=recv_sems.at[j], device_id=(cx, cy, c), device_id_type=MESH)
            for j, (cx, cy) in enumerate(chips)]
        for cp in cps:
            cp.start()
        for cp in cps:
            cp.wait()

    return _pcall(
        body, name="rs_chip_exchange",
        out_shape=jax.ShapeDtypeStruct((3, R_HALF, D_MODEL), F32),
        in_specs=[ANY], out_specs=ANY,
        scratch_shapes=[pltpu.SemaphoreType.DMA((3,)), pltpu.SemaphoreType.DMA((3,))],
    )(p4)


def _rs_chip_add(p4, got, s_idx):
    tr = R_HALF // 2

    def body(s_ref, a_ref, b_ref, o_ref):
        o_ref[...] = ((a_ref[...] + b_ref[0]) + b_ref[1]) + b_ref[2]

    return _pcall(
        body, name="rs_chip_add",
        grid_spec=pltpu.PrefetchScalarGridSpec(
            num_scalar_prefetch=1, grid=(2,),
            in_specs=[pl.BlockSpec((None, tr, D_MODEL), lambda i, s_ref: (s_ref[0], i, 0)),
                      pl.BlockSpec((3, tr, D_MODEL), lambda i, s_ref: (0, i, 0))],
            out_specs=pl.BlockSpec((tr, D_MODEL), lambda i, s_ref: (i, 0))),
        out_shape=jax.ShapeDtypeStruct((R_HALF, D_MODEL), F32),
        compiler_params=_params(("parallel",)),
    )(s_idx, p4, got)


def _rs_pair_share(r_half):
    def body(r_ref, out_ref, send_sem, recv_sem, local_sem):
        x, y, c = _place()
        own = pltpu.make_async_copy(r_ref, out_ref.at[c], local_sem)
        own.start()
        cp = pltpu.make_async_remote_copy(
            src_ref=r_ref, dst_ref=out_ref.at[c], send_sem=send_sem, recv_sem=recv_sem,
            device_id=(x, y, 1 - c), device_id_type=MESH)
        cp.start()
        cp.wait()
        own.wait()

    return _pcall(
        body, name="rs_pair_share",
        out_shape=jax.ShapeDtypeStruct((2, R_HALF, D_MODEL), F32),
        in_specs=[ANY], out_specs=ANY,
        scratch_shapes=[pltpu.SemaphoreType.DMA, pltpu.SemaphoreType.DMA, pltpu.SemaphoreType.DMA],
    )(r_half)


def _adam(w, g, m, v):
    m2 = ADAM_B1 * m + (1.0 - ADAM_B1) * g
    v2 = ADAM_B2 * v + (1.0 - ADAM_B2) * (g * g)
    m_hat = m2 / (1.0 - ADAM_B1 ** ADAM_STEP)
    v_hat = v2 / (1.0 - ADAM_B2 ** ADAM_STEP)
    return -ADAM_LR * (m_hat / (jnp.sqrt(v_hat) + ADAM_EPS) + ADAM_WD * w), m2, v2


def _small_allreduce_adamw(part, w, m, v):
    def body(p_ref, w_ref, m_ref, v_ref, g_ref, d_ref, m2_ref, v2_ref, buf, send_sems, recv_sems):
        x, y, c = _place()
        me = 4 * x + 2 * y + c
        cps = []
        for k in range(1, 8):
            peer = (1 - x if k & 4 else x, 1 - y if k & 2 else y, 1 - c if k & 1 else c)
            cps.append(pltpu.make_async_remote_copy(
                src_ref=p_ref, dst_ref=buf.at[me], send_sem=send_sems.at[k - 1], recv_sem=recv_sems.at[k - 1],
                device_id=peer, device_id_type=MESH))
        for cp in cps:
            cp.start()
        buf[me] = p_ref[...]
        for cp in cps:
            cp.wait()
        g = buf[0]
        for k in range(1, 8):
            g = g + buf[k]
        g_ref[...] = g
        d_ref[...], m2_ref[...], v2_ref[...] = _adam(w_ref[...], g, m_ref[...], v_ref[...])

    vm = pl.BlockSpec(memory_space=pltpu.VMEM)
    shp = jax.ShapeDtypeStruct((SMALL_ROWS, LANES), F32)
    return _pcall(
        body, name="small_allreduce_adamw",
        out_shape=[shp, shp, shp, shp], in_specs=[vm, vm, vm, vm], out_specs=[vm, vm, vm, vm],
        scratch_shapes=[pltpu.VMEM((8, SMALL_ROWS, LANES), F32), pltpu.SemaphoreType.DMA((7,)),
                        pltpu.SemaphoreType.DMA((7,))],
    )(part, w, m, v)


def _adamw(w, g, m, v, name):
    rows, cols = w.shape
    tr = min(256, rows)

    def body(w_ref, g_ref, m_ref, v_ref, d_ref, m2_ref, v2_ref):
        d_ref[...], m2_ref[...], v2_ref[...] = _adam(w_ref[...], g_ref[...], m_ref[...], v_ref[...])

    spec = _rows(tr, cols)
    shp = jax.ShapeDtypeStruct((rows, cols), F32)
    return _pcall(
        body, name=name, grid=(rows // tr,), in_specs=[spec] * 4, out_specs=[spec] * 3, out_shape=[shp] * 3,
        compiler_params=_params(("parallel",)),
    )(w, g, m, v)


def _pad_lanes(v):
    return jnp.pad(v, (0, LANES - v.shape[0]))


def _pad_head_rows(w_t, heads):
    n = w_t.shape[1]
    return jnp.pad(w_t.reshape(heads, HEAD_DIM, n), ((0, 0), (0, LANES - HEAD_DIM), (0, 0))).reshape(heads * LANES, n)


def _unpad_head_rows(w_t, heads):
    n = w_t.shape[1]
    return w_t.reshape(heads, LANES, n)[:, :HEAD_DIM].reshape(heads * HEAD_DIM, n)


def _in_rows_pad(w_in_t):
    qa, ka, va, qb, kb, vb, f = jnp.split(w_in_t, [512, 640, 768, 1280, 1792, 2304], axis=0)
    f = jnp.pad(f, ((0, 2 * LANES - B_HEADS), (0, 0)))
    return jnp.concatenate([_pad_head_rows(qa, 8), _pad_head_rows(ka, 2), _pad_head_rows(qb, 8),
                            _pad_head_rows(kb, 8), _pad_head_rows(va, 2), _pad_head_rows(vb, 8), f], axis=0)


def _in_rows_unpad(d):
    qa = _unpad_head_rows(d[G_QA * LANES:G_KA * LANES], 8)
    ka = _unpad_head_rows(d[G_KA * LANES:G_QB * LANES], 2)
    qb = _unpad_head_rows(d[G_QB * LANES:G_KB * LANES], 8)
    kb = _unpad_head_rows(d[G_KB * LANES:G_VA * LANES], 8)
    va = _unpad_head_rows(d[G_VA * LANES:G_VB * LANES], 2)
    vb = _unpad_head_rows(d[G_VB * LANES:G_F * LANES], 8)
    f = d[G_F * LANES:G_F * LANES + B_HEADS]
    return jnp.concatenate([qa, ka, va, qb, kb, vb, f], axis=0)


def _pack_small(g1, bf, qa, ka, sk, qb, kb, g2, loss_row):
    rows = [g1.reshape(8, LANES), g2.reshape(8, LANES)]
    rows += [_pad_lanes(t)[None] for t in (qa, ka, qb, kb, bf, sk)]
    rows += [loss_row, jnp.zeros((1, LANES), F32)]
    return jnp.concatenate(rows, axis=0)


def _unpack_small(p):
    return (p[0:8].reshape(D_MODEL), p[20, :B_HEADS], p[16, :HEAD_DIM], p[17, :HEAD_DIM], p[21, :A_HEADS],
            p[18, :HEAD_DIM], p[19, :HEAD_DIM], p[8:16].reshape(D_MODEL))


def kernel(x, attn_norm_g, w_in, b_forget, q_norm_a, k_norm_a, sink_logits, q_norm_b, k_norm_b, w_out, mlp_norm_g, w_up, w_down, loss_target, m_attn_norm_g, m_w_in, m_b_forget, m_q_norm_a, m_k_norm_a, m_sink_logits, m_q_norm_b, m_k_norm_b, m_w_out, m_mlp_norm_g, m_w_up, m_w_down, v_attn_norm_g, v_w_in, v_b_forget, v_q_norm_a, v_k_norm_a, v_sink_logits, v_q_norm_b, v_k_norm_b, v_w_out, v_mlp_norm_g, v_w_up, v_w_down):
    nb, seq, _ = x.shape
    t_all = nb * seq
    c_idx = lax.axis_index("c")
    s_idx = 2 * lax.axis_index("x") + lax.axis_index("y")

    packed = jnp.concatenate([jnp.pad(w_in.T, ((0, IN_SHARD_P - IN_SHARD), (0, 0))), w_out, w_up, w_down], axis=0)
    mine = lax.dynamic_slice_in_dim(packed.astype(BF16).reshape(2, R_HALF, D_MODEL), c_idx, 1, axis=0)[0]
    gathered = _allgather_halves(mine).reshape(N_CHIPS, R_ALL, D_MODEL)
    w_in_t = gathered[:, :IN_SHARD].reshape(IN_WIDTH, D_MODEL)
    w_pad_t = _in_rows_pad(w_in_t)
    w_pad = w_pad_t.T
    wo_pad = _pad_head_rows(gathered[:, R_OUT:R_UP].reshape(D_MODEL, D_MODEL), A_HEADS + B_HEADS)
    wo_pad_t = wo_pad.T
    w_up_blocks = gathered[:, R_UP:R_DOWN]
    w_up_t = jnp.swapaxes(w_up_blocks, 1, 2).reshape(D_FF, D_MODEL)
    w_down_f = gathered[:, R_DOWN:].reshape(D_FF, D_MODEL)
    w_down_t = w_down_f.T

    ones = jnp.ones((LANES,), F32)
    gain_row = jnp.concatenate(
        [jnp.tile(_pad_lanes(q_norm_a), 8), jnp.tile(_pad_lanes(k_norm_a), 2), jnp.tile(_pad_lanes(q_norm_b), 8),
         jnp.tile(_pad_lanes(k_norm_b), 8), jnp.tile(ones, N_GROUPS - N_NORM_GROUPS)])[None]
    b_row = _pad_lanes(b_forget)[None]
    g1 = attn_norm_g[None]
    g2 = mlp_norm_g[None]
    slopes = jnp.exp2(-(8.0 / A_HEADS) * (jnp.arange(A_HEADS, dtype=F32) + 1.0))

    x2 = x.reshape(t_all, D_MODEL)
    tgt = loss_target.reshape(t_all, D_MODEL)

    xn, pre, qa, ka, va, qb, kb, vb, z = _inproj(x2, g1, w_pad, gain_row, b_row, seq)
    oa, la = _swa_fwd(qa, ka, va, sink_logits, slopes, nb, seq)
    ob, lse = _fox_fwd(qb, kb, vb, nb, seq)
    h, hn = _outproj(x2, oa, ob, wo_pad, g2)
    ru, hid = _mlp_up(hn, w_up_blocks)
    dy, dyb, loss_acc = _mlp_down(hid, w_down_f, h, tgt)

    du = _mlp_dhid(dyb, w_down_t, ru)
    d_w_down = _wgrad(hid, dyb, "dw_down", D_FF // D_MODEL, 1)
    d_w_up = _wgrad(hn, du, "dw_up", 1, D_FF // D_MODEL)
    dh, dhb, d_g2 = _mlp_dhn(du, w_up_t, h, dy, g2)
    doa, dob = _dmixed(dhb, wo_pad_t)
    d_wo_a = _wgrad(oa, dhb, "dw_out_a", 1, 1)[0]
    d_wo_b = _wgrad(ob, dhb, "dw_out_b", 1, 1)[0]
    dqb, dkb, dvb = _fox_bwd(qb, kb, vb, ob, dob, lse, nb, seq)
    dqa, dka, dva, dsink = _swa_bwd(qa, ka, va, oa, doa, la, sink_logits, slopes, nb, seq)
    dproj, small = _dproj(pre, dqa, dka, dqb, dkb, dva, dvb, z, gain_row, seq)
    grad_x, d_g1 = _dx(dproj, w_pad_t, x2, dh, g1)
    d_w_in_t = _dwin(dproj, xn)

    d_w_out = jnp.concatenate([_unpad_head_rows(d_wo_a, 8), _unpad_head_rows(d_wo_b, 8)], axis=0)
    g_pack = jnp.concatenate([
        jnp.pad(_in_rows_unpad(d_w_in_t).reshape(N_CHIPS, IN_SHARD, D_MODEL),
                ((0, 0), (0, IN_SHARD_P - IN_SHARD), (0, 0))),
        d_w_out.reshape(N_CHIPS, D_MODEL // N_CHIPS, D_MODEL), d_w_up, d_w_down], axis=1)
    g4 = g_pack.reshape(N_CHIPS, 2, R_HALF, D_MODEL)
    pair = _rs_pair_add(g4, _rs_pair_exchange(g4), c_idx.reshape(1).astype(jnp.int32))
    r_half = _rs_chip_add(pair, _rs_chip_exchange(pair), s_idx.reshape(1).astype(jnp.int32))
    red = _rs_pair_share(r_half).reshape(R_ALL, D_MODEL)
    g_w_in = red[:IN_SHARD].T
    g_w_out = red[R_OUT:R_UP]
    g_w_up = red[R_UP:R_DOWN]
    g_w_down = red[R_DOWN:]

    loss_row = loss_acc[0:1] * (0.5 / D_MODEL)
    d_sink = dsink[:, 0, 0].reshape(nb, A_HEADS).sum(axis=0)
    part = _pack_small(d_g1[0], small[4, :B_HEADS], small[0, :HEAD_DIM], small[1, :HEAD_DIM], d_sink,
                       small[2, :HEAD_DIM], small[3, :HEAD_DIM], d_g2[0], loss_row)
    zero_row = jnp.zeros((1, LANES), F32)
    smalls = lambda t: _pack_small(*t, zero_row)
    w_small = smalls((attn_norm_g, b_forget, q_norm_a, k_norm_a, sink_logits, q_norm_b, k_norm_b, mlp_norm_g))
    m_small = smalls((m_attn_norm_g, m_b_forget, m_q_norm_a, m_k_norm_a, m_sink_logits, m_q_norm_b, m_k_norm_b,
                      m_mlp_norm_g))
    v_small = smalls((v_attn_norm_g, v_b_forget, v_q_norm_a, v_k_norm_a, v_sink_logits, v_q_norm_b, v_k_norm_b,
                      v_mlp_norm_g))
    g_s, d_s, m_s, v_s = _small_allreduce_adamw(part, w_small, m_small, v_small)
    loss = g_s[ROW_LOSS, 0]

    big = {}
    for name, w, g, m, v in (("adamw_w_in", w_in, g_w_in, m_w_in, v_w_in),
                             ("adamw_w_out", w_out, g_w_out, m_w_out, v_w_out),
                             ("adamw_w_up", w_up, g_w_up, m_w_up, v_w_up),
                             ("adamw_w_down", w_down, g_w_down, m_w_down, v_w_down)):
        big[name] = (g,) + tuple(_adamw(w, g, m, v, name))

    def assemble(k, small_pack):
        s = _unpack_small(small_pack)
        return (s[0], big["adamw_w_in"][k], s[1], s[2], s[3], s[4], s[5], s[6], big["adamw_w_out"][k], s[7],
                big["adamw_w_up"][k], big["adamw_w_down"][k])

    return (loss, grad_x.reshape(nb, seq, D_MODEL), *assemble(0, g_s), *assemble(1, d_s), *assemble(2, m_s),
            *assemble(3, v_s))
```

```python
import functools

import numpy as np
import jax
import jax.numpy as jnp
from jax import lax
from jax.experimental import pallas as pl
from jax.experimental.pallas import tpu as pltpu

F32 = jnp.float32
BF16 = jnp.bfloat16

D_MODEL = 1024
HEAD_DIM = 64
LANES = 128
A_HEADS = 8
A_KV_HEADS = 2
A_GROUP = A_HEADS // A_KV_HEADS
B_HEADS = 8
WINDOW = 128
D_FF = 4096
IN_WIDTH = 2312
EPS = 1e-6
SCALE = 0.125
NEG = -1e30

G_QA, G_KA, G_QB, G_KB, G_VA, G_VB, G_F = 0, 8, 10, 18, 26, 28, 36
N_NORM_GROUPS = 26
N_GROUPS = 38
NP = N_GROUPS * LANES
MIXED_P = (A_HEADS + B_HEADS) * LANES

N_CHIPS = 4
IN_SHARD = IN_WIDTH // N_CHIPS
IN_SHARD_P = 608
R_OUT = IN_SHARD_P
R_UP = R_OUT + D_MODEL // N_CHIPS
R_DOWN = R_UP + D_MODEL
R_ALL = R_DOWN + D_FF // N_CHIPS
R_HALF = R_ALL // 2

SMALL_ROWS = 24
ROW_LOSS = 22

ADAM_LR = 0.001
ADAM_B1 = 0.9
ADAM_B2 = 0.999
ADAM_EPS = 1e-08
ADAM_WD = 0.01
ADAM_STEP = 10

VMEM_LIMIT = 52 * 1024 * 1024
MESH = pl.DeviceIdType.MESH


def _pcall(body, **kw):
    return pl.pallas_call(body, **kw)


def _params(sem=None):
    return pltpu.CompilerParams(dimension_semantics=sem, vmem_limit_bytes=VMEM_LIMIT)


def _dot(a, b):
    return jnp.dot(a, b, preferred_element_type=F32)


def _dot_nt(a, b):
    return lax.dot_general(a, b, (((1,), (1,)), ((), ())), preferred_element_type=F32)


def _dot_tn(a, b):
    return lax.dot_general(a, b, (((0,), (0,)), ((), ())), preferred_element_type=F32)


def _split3(x):
    hi = x.astype(BF16)
    r1 = x - hi.astype(F32)
    mid = r1.astype(BF16)
    lo = (r1 - mid.astype(F32)).astype(BF16)
    return hi, mid, lo


def _dot_exact(mat, x):
    hi, mid, lo = _split3(x)
    return _dot(mat, lo) + _dot(mat, mid) + _dot(mat, hi)


def _const(shape):
    zeros = (0,) * len(shape)
    return pl.BlockSpec(shape, lambda *_: zeros)


def _rows(tm, n):
    return pl.BlockSpec((tm, n), lambda i: (i, 0))


def _aug_select():
    e = np.zeros((3 * LANES, 2 * B_HEADS * LANES), np.float32)
    for j in range(3):
        for h in range(B_HEADS):
            e[j * LANES + h, h * LANES + HEAD_DIM + j] = 1.0
            e[j * LANES + h, (B_HEADS + h) * LANES + HEAD_DIM + 3 + j] = -1.0
    return jnp.asarray(e, BF16)


def _dc_select():
    e = np.zeros((2 * B_HEADS * LANES, LANES), np.float32)
    for h in range(B_HEADS):
        e[h * LANES + HEAD_DIM, h] = 1.0
        e[(B_HEADS + h) * LANES + HEAD_DIM + 3, h] = -1.0
    return jnp.asarray(e, BF16)


def _tri(n, upper):
    t = np.tril(np.ones((n, n), np.float32))
    return jnp.asarray(t.T if upper else t, BF16)


def _inproj(x2, g1, w_pad, gain_row, b_row, seq):
    t_all = x2.shape[0]
    tm = min(256, seq)
    tiles_per_seq = seq // tm
    tri = _tri(tm, False)
    esel = _aug_select()

    def body(x_ref, g_ref, w_ref, gain_ref, b_ref, tri_ref, e_ref,
             xn_ref, pre_ref, qa_ref, ka_ref, va_ref, qb_ref, kb_ref, vb_ref, z_ref, carry_ref):
        i = pl.program_id(0)

        @pl.when(i % tiles_per_seq == 0)
        def _():
            carry_ref[...] = jnp.zeros_like(carry_ref)

        x = x_ref[...]
        r = lax.rsqrt(jnp.mean(x * x, axis=-1, keepdims=True) + EPS)
        xn = (x * r * g_ref[...]).astype(BF16)
        xn_ref[...] = xn
        proj = _dot(xn, w_ref[...])
        pre_ref[...] = proj[:, :N_NORM_GROUPS * LANES].astype(BF16)
        lane = lax.broadcasted_iota(jnp.int32, (tm, LANES), 1)

        z = proj[:, G_F * LANES:(G_F + 1) * LANES] + b_ref[...]
        z_ref[...] = z
        lf = jnp.minimum(z, 0.0) - jnp.log(1.0 + jnp.exp(-jnp.abs(z)))
        lf = jnp.where(lane < B_HEADS, lf, 0.0)
        c = _dot_exact(tri_ref[...], lf) + carry_ref[...]
        carry_ref[...] += jnp.sum(lf, axis=0, keepdims=True)
        aug = _dot(jnp.concatenate(_split3(c), axis=1), e_ref[...])

        def hnorm(g):
            p = proj[:, g * LANES:(g + 1) * LANES]
            rr = lax.rsqrt(jnp.sum(p * p, axis=-1, keepdims=True) * (1.0 / HEAD_DIM) + EPS)
            return p * rr * gain_ref[:, g * LANES:(g + 1) * LANES]

        ones_q = jnp.where((lane >= HEAD_DIM + 3) & (lane < HEAD_DIM + 6), 1.0, 0.0)
        ones_k = jnp.where((lane >= HEAD_DIM) & (lane < HEAD_DIM + 3), 1.0, 0.0)
        for h in range(A_HEADS):
            qa_ref[:, h * LANES:(h + 1) * LANES] = (hnorm(G_QA + h) * SCALE).astype(BF16)
        for h in range(A_KV_HEADS):
            ka_ref[:, h * LANES:(h + 1) * LANES] = hnorm(G_KA + h).astype(BF16)
        for h in range(B_HEADS):
            qb_ref[:, h * LANES:(h + 1) * LANES] = (
                hnorm(G_QB + h) * SCALE + aug[:, h * LANES:(h + 1) * LANES] + ones_q).astype(BF16)
            kb_ref[:, h * LANES:(h + 1) * LANES] = (
                hnorm(G_KB + h) + aug[:, (B_HEADS + h) * LANES:(B_HEADS + h + 1) * LANES] + ones_k).astype(BF16)
        va_ref[...] = proj[:, G_VA * LANES:G_VB * LANES].astype(BF16)
        vb_ref[...] = proj[:, G_VB * LANES:G_F * LANES].astype(BF16)

    widths = [(D_MODEL, BF16), (N_NORM_GROUPS * LANES, BF16), (A_HEADS * LANES, BF16), (A_KV_HEADS * LANES, BF16),
              (A_KV_HEADS * LANES, BF16), (B_HEADS * LANES, BF16), (B_HEADS * LANES, BF16), (B_HEADS * LANES, BF16),
              (LANES, F32)]
    return _pcall(
        body, name="inproj", grid=(t_all // tm,),
        in_specs=[_rows(tm, D_MODEL), _const((1, D_MODEL)), _const((D_MODEL, NP)), _const((1, NP)),
                  _const((1, LANES)), _const((tm, tm)), _const(esel.shape)],
        out_specs=[_rows(tm, w) for w, _ in widths],
        out_shape=[jax.ShapeDtypeStruct((t_all, w), dt) for w, dt in widths],
        scratch_shapes=[pltpu.VMEM((1, LANES), F32)],
        compiler_params=_params(("arbitrary",)),
    )(x2, g1, w_pad, gain_row, b_row, tri, esel)


def _fox_fwd(qb, kb, vb, nb, seq):
    t_all = qb.shape[0]
    tq = min(512, seq)
    nq = seq // tq

    def body(q_ref, k_ref, v_ref, o_ref, lse_ref):
        qi = pl.program_id(2)
        q = q_ref[...]

        def step(j, carry, masked):
            m, l, acc = carry
            off = pl.multiple_of(j * tq, tq)
            k = k_ref[pl.ds(off, tq), :]
            v = v_ref[pl.ds(off, tq), :]
            s = _dot_nt(q, k)
            if masked:
                row = lax.broadcasted_iota(jnp.int32, (tq, tq), 0)
                col = lax.broadcasted_iota(jnp.int32, (tq, tq), 1)
                s = jnp.where(row >= col, s, NEG)
            m_new = jnp.maximum(m, jnp.max(s, axis=-1, keepdims=True))
            alpha = jnp.exp(m - m_new)
            p = jnp.exp(s - m_new)
            l = alpha * l + jnp.sum(p, axis=-1, keepdims=True)
            acc = alpha * acc + _dot(p.astype(BF16), v)
            return m_new, l, acc

        init = (jnp.full((tq, 1), NEG, F32), jnp.zeros((tq, 1), F32), jnp.zeros((tq, LANES), F32))
        carry = lax.fori_loop(0, qi, lambda j, cr: step(j, cr, False), init)
        m, l, acc = step(qi, carry, True)
        o_ref[...] = (acc / l).astype(BF16)
        lse_ref[...] = jnp.broadcast_to(m + jnp.log(l), (tq, LANES))

    qspec = pl.BlockSpec((tq, LANES), lambda b, h, i: (b * nq + i, h))
    kspec = pl.BlockSpec((seq, LANES), lambda b, h, i: (b, h))
    return _pcall(
        body, name="fox_fwd", grid=(nb, B_HEADS, nq),
        in_specs=[qspec, kspec, kspec], out_specs=[qspec, qspec],
        out_shape=[jax.ShapeDtypeStruct((t_all, B_HEADS * LANES), BF16),
                   jax.ShapeDtypeStruct((t_all, B_HEADS * LANES), F32)],
        compiler_params=_params(("parallel", "parallel", "arbitrary")),
    )(qb, kb, vb)


def _swa_bias(slopes):
    row = jnp.arange(A_GROUP * WINDOW, dtype=jnp.int32)[:, None] % WINDOW
    col = jnp.arange(2 * WINDOW, dtype=jnp.int32)[None, :]
    slope_rows = jnp.repeat(slopes.reshape(A_KV_HEADS, A_GROUP), WINDOW, axis=1)[:, :, None]
    out = []
    for t_rel in (0, WINDOW):
        dist = t_rel + row - col
        valid = (dist >= 0) & (dist < WINDOW)
        out.append(jnp.where(valid[None], -slope_rows * dist.astype(F32)[None], NEG))
    return jnp.stack(out)


def _stack_heads(ref, rows):
    return jnp.concatenate([ref[rows, j * LANES:(j + 1) * LANES] for j in range(A_GROUP)], axis=0)


def _sink_column(sink_ref, g):
    return jnp.concatenate([jnp.full((WINDOW, 1), sink_ref[g * A_GROUP + j], F32) for j in range(A_GROUP)], axis=0)


def _swa_specs(nq, tq, seq):
    smem = pl.BlockSpec(memory_space=pltpu.SMEM)
    qspec = pl.BlockSpec((tq, A_GROUP * LANES), lambda b, g, i: (b * nq + i, g))
    kspec = pl.BlockSpec((seq, LANES), lambda b, g, i: (b, g))
    bias_first = pl.BlockSpec((None, None, A_GROUP * WINDOW, 2 * WINDOW),
                              lambda b, g, i: (jnp.minimum(i, 1), g, 0, 0))
    bias_rest = pl.BlockSpec((None, None, A_GROUP * WINDOW, 2 * WINDOW), lambda b, g, i: (1, g, 0, 0))
    return smem, qspec, kspec, bias_first, bias_rest


def _swa_fwd(qa, ka, va, sinks, bias, nb, seq):
    t_all = qa.shape[0]
    tq = min(512, seq)
    nq = seq // tq

    def body(sink_ref, q_ref, k_ref, v_ref, bias0_ref, bias_ref, o_ref, l_ref):
        qi = pl.program_id(2)
        sink = _sink_column(sink_ref, pl.program_id(1))
        for a in range(tq // WINDOW):
            t0 = qi * tq + a * WINDOW
            start = pl.multiple_of(jnp.maximum(t0 - WINDOW, 0), WINDOW)
            rows = slice(a * WINDOW, (a + 1) * WINDOW)
            k = k_ref[pl.ds(start, 2 * WINDOW), :]
            v = v_ref[pl.ds(start, 2 * WINDOW), :]
            s = _dot_nt(_stack_heads(q_ref, rows), k) + (bias0_ref if a == 0 else bias_ref)[...]
            m = jnp.maximum(jnp.max(s, axis=-1, keepdims=True), sink)
            p = jnp.exp(s - m)
            den = jnp.sum(p, axis=-1, keepdims=True) + jnp.exp(sink - m)
            o = _dot((p / den).astype(BF16), v).astype(BF16)
            lrow = jnp.broadcast_to(m + jnp.log(den), (A_GROUP * WINDOW, LANES))
            for j in range(A_GROUP):
                o_ref[rows, j * LANES:(j + 1) * LANES] = o[j * WINDOW:(j + 1) * WINDOW]
                l_ref[rows, j * LANES:(j + 1) * LANES] = lrow[j * WINDOW:(j + 1) * WINDOW]

    smem, qspec, kspec, bias_first, bias_rest = _swa_specs(nq, tq, seq)
    return _pcall(
        body, name="swa_fwd", grid=(nb, A_KV_HEADS, nq),
        in_specs=[smem, qspec, kspec, kspec, bias_first, bias_rest], out_specs=[qspec, qspec],
        out_shape=[jax.ShapeDtypeStruct((t_all, A_HEADS * LANES), BF16),
                   jax.ShapeDtypeStruct((t_all, A_HEADS * LANES), F32)],
        compiler_params=_params(("parallel", "parallel", "arbitrary")),
    )(sinks, qa, ka, va, bias, bias)


def _outproj(x2, oa, ob, wo_pad, g2):
    t_all = x2.shape[0]
    tm = min(512, t_all)
    half = A_HEADS * LANES

    def body(x_ref, oa_ref, ob_ref, w_ref, g_ref, h_ref, hn_ref):
        h = x_ref[...] + _dot(oa_ref[...], w_ref[:half, :]) + _dot(ob_ref[...], w_ref[half:, :])
        h_ref[...] = h
        r = lax.rsqrt(jnp.mean(h * h, axis=-1, keepdims=True) + EPS)
        hn_ref[...] = (h * r * g_ref[...]).astype(BF16)

    return _pcall(
        body, name="outproj", grid=(t_all // tm,),
        in_specs=[_rows(tm, D_MODEL), _rows(tm, half), _rows(tm, half), _const((MIXED_P, D_MODEL)),
                  _const((1, D_MODEL))],
        out_specs=[_rows(tm, D_MODEL), _rows(tm, D_MODEL)],
        out_shape=[jax.ShapeDtypeStruct((t_all, D_MODEL), F32), jax.ShapeDtypeStruct((t_all, D_MODEL), BF16)],
        compiler_params=_params(("parallel",)),
    )(x2, oa, ob, wo_pad, g2)


def _mlp_up(hn, w_up_blocks):
    t_all = hn.shape[0]
    tm = min(512, t_all)
    nj = D_FF // D_MODEL

    def body(a_ref, w_ref, ru_ref, hid_ref):
        ru = jnp.maximum(_dot(a_ref[...], w_ref[...]), 0.0)
        ru_ref[...] = ru.astype(BF16)
        hid_ref[...] = (ru * ru).astype(BF16)

    ospec = pl.BlockSpec((tm, D_MODEL), lambda j, i: (i, j))
    return _pcall(
        body, name="mlp_up", grid=(nj, t_all // tm),
        in_specs=[pl.BlockSpec((tm, D_MODEL), lambda j, i: (i, 0)),
                  pl.BlockSpec((None, D_MODEL, D_MODEL), lambda j, i: (j, 0, 0))],
        out_specs=[ospec, ospec],
        out_shape=[jax.ShapeDtypeStruct((t_all, D_FF), BF16), jax.ShapeDtypeStruct((t_all, D_FF), BF16)],
        compiler_params=_params(("parallel", "parallel")),
    )(hn, w_up_blocks)


def _mlp_down(hid, w_down, h, tgt):
    t_all = h.shape[0]
    tm = min(256, t_all)

    def body(a_ref, w_ref, h_ref, t_ref, dy_ref, dyb_ref, loss_ref):
        @pl.when(pl.program_id(0) == 0)
        def _():
            loss_ref[...] = jnp.zeros_like(loss_ref)

        y = h_ref[...] + _dot(a_ref[...], w_ref[...])
        err = y - t_ref[...]
        loss_ref[...] += jnp.sum(err * err)
        dy = err * (1.0 / D_MODEL)
        dy_ref[...] = dy
        dyb_ref[...] = dy.astype(BF16)

    return _pcall(
        body, name="mlp_down", grid=(t_all // tm,),
        in_specs=[_rows(tm, D_FF), _const((D_FF, D_MODEL)), _rows(tm, D_MODEL), _rows(tm, D_MODEL)],
        out_specs=[_rows(tm, D_MODEL), _rows(tm, D_MODEL), _const((8, LANES))],
        out_shape=[jax.ShapeDtypeStruct((t_all, D_MODEL), F32), jax.ShapeDtypeStruct((t_all, D_MODEL), BF16),
                   jax.ShapeDtypeStruct((8, LANES), F32)],
        compiler_params=_params(("arbitrary",)),
    )(hid, w_down, h, tgt)


def _mlp_dhid(dyb, w_down_t, ru):
    t_all = dyb.shape[0]
    tm = min(512, t_all)
    nj = D_FF // D_MODEL

    def body(a_ref, w_ref, ru_ref, du_ref):
        du_ref[...] = (_dot(a_ref[...], w_ref[...]) * (2.0 * ru_ref[...].astype(F32))).astype(BF16)

    ospec = pl.BlockSpec((tm, D_MODEL), lambda j, i: (i, j))
    return _pcall(
        body, name="mlp_dhid", grid=(nj, t_all // tm),
        in_specs=[pl.BlockSpec((tm, D_MODEL), lambda j, i: (i, 0)),
                  pl.BlockSpec((D_MODEL, D_MODEL), lambda j, i: (0, j)), ospec],
        out_specs=ospec,
        out_shape=jax.ShapeDtypeStruct((t_all, D_FF), BF16),
        compiler_params=_params(("parallel", "parallel")),
    )(dyb, w_down_t, ru)


def _wgrad(a, b, name, a_col_blocks, b_col_blocks):
    t_all = a.shape[0]
    tt = min(512, t_all)
    nj = max(a_col_blocks, b_col_blocks)
    wa = a.shape[1] // a_col_blocks
    wb = b.shape[1] // b_col_blocks

    def body(a_ref, b_ref, o_ref):
        @pl.when(pl.program_id(1) == 0)
        def _():
            o_ref[...] = jnp.zeros_like(o_ref)

        o_ref[...] += _dot_tn(a_ref[...], b_ref[...])

    return _pcall(
        body, name=name, grid=(nj, t_all // tt),
        in_specs=[pl.BlockSpec((tt, wa), lambda j, t: (t, j if a_col_blocks > 1 else 0)),
                  pl.BlockSpec((tt, wb), lambda j, t: (t, j if b_col_blocks > 1 else 0))],
        out_specs=pl.BlockSpec((None, wa, wb), lambda j, t: (j, 0, 0)),
        out_shape=jax.ShapeDtypeStruct((nj, wa, wb), F32),
        compiler_params=_params(("parallel", "arbitrary")),
    )(a, b)


def _mlp_dhn(du, w_up_t, h, dy, g2):
    t_all = h.shape[0]
    tm = min(256, t_all)

    def body(a_ref, w_ref, h_ref, dy_ref, g_ref, dh_ref, dhb_ref, dg_ref):
        @pl.when(pl.program_id(0) == 0)
        def _():
            dg_ref[...] = jnp.zeros_like(dg_ref)

        dhn = _dot(a_ref[...], w_ref[...])
        h = h_ref[...]
        r = lax.rsqrt(jnp.mean(h * h, axis=-1, keepdims=True) + EPS)
        hh = h * r
        dg_ref[...] += jnp.sum(dhn * hh, axis=0, keepdims=True)
        dz = dhn * g_ref[...]
        dh = dy_ref[...] + r * (dz - hh * jnp.mean(dz * hh, axis=-1, keepdims=True))
        dh_ref[...] = dh
        dhb_ref[...] = dh.astype(BF16)

    return _pcall(
        body, name="mlp_dhn", grid=(t_all // tm,),
        in_specs=[_rows(tm, D_FF), _const((D_FF, D_MODEL)), _rows(tm, D_MODEL), _rows(tm, D_MODEL),
                  _const((1, D_MODEL))],
        out_specs=[_rows(tm, D_MODEL), _rows(tm, D_MODEL), _const((1, D_MODEL))],
        out_shape=[jax.ShapeDtypeStruct((t_all, D_MODEL), F32), jax.ShapeDtypeStruct((t_all, D_MODEL), BF16),
                   jax.ShapeDtypeStruct((1, D_MODEL), F32)],
        compiler_params=_params(("arbitrary",)),
    )(du, w_up_t, h, dy, g2)


def _dmixed(dhb, wo_pad_t):
    t_all = dhb.shape[0]
    tm = min(512, t_all)
    half = A_HEADS * LANES

    def body(a_ref, w_ref, da_ref, db_ref):
        d = _dot(a_ref[...], w_ref[...])
        da_ref[...] = d[:, :half].astype(BF16)
        db_ref[...] = d[:, half:].astype(BF16)

    return _pcall(
        body, name="dmixed", grid=(t_all // tm,),
        in_specs=[_rows(tm, D_MODEL), _const((D_MODEL, MIXED_P))],
        out_specs=[_rows(tm, half), _rows(tm, half)],
        out_shape=[jax.ShapeDtypeStruct((t_all, half), BF16), jax.ShapeDtypeStruct((t_all, half), BF16)],
        compiler_params=_params(("parallel",)),
    )(dhb, wo_pad_t)


def _fox_bwd(qb, kb, vb, ob, dob, lse, nb, seq):
    t_all = qb.shape[0]
    tk = min(512, seq)
    nk = seq // tk

    def body(q_ref, k_ref, v_ref, o_ref, do_ref, lse_ref, dq_ref, dk_ref, dv_ref):
        kj = pl.program_id(2)

        @pl.when(kj == 0)
        def _():
            dq_ref[...] = jnp.zeros_like(dq_ref)

        k = k_ref[...]
        v = v_ref[...]

        def step(i, carry, masked):
            dk, dv = carry
            off = pl.multiple_of(i * tk, tk)
            q = q_ref[pl.ds(off, tk), :]
            do = do_ref[pl.ds(off, tk), :]
            o = o_ref[pl.ds(off, tk), :]
            lse_t = jnp.max(lse_ref[pl.ds(off, tk), :], axis=-1, keepdims=True)
            delta = jnp.sum(do.astype(F32) * o.astype(F32), axis=-1, keepdims=True)
            p = jnp.exp(_dot_nt(q, k) - lse_t)
            if masked:
                row = lax.broadcasted_iota(jnp.int32, (tk, tk), 0)
                col = lax.broadcasted_iota(jnp.int32, (tk, tk), 1)
                p = jnp.where(row >= col, p, 0.0)
            ds = (p * (_dot_nt(do, v) - delta)).astype(BF16)
            dv = dv + _dot_tn(p.astype(BF16), do)
            dk = dk + _dot_tn(ds, q)
            dq_ref[pl.ds(off, tk), :] += _dot(ds, k)
            return dk, dv

        zero = jnp.zeros((tk, LANES), F32)
        carry = step(kj, (zero, zero), True)
        dk, dv = lax.fori_loop(kj + 1, nk, lambda i, cr: step(i, cr, False), carry)
        dk_ref[...] = dk
        dv_ref[...] = dv

    full = pl.BlockSpec((seq, LANES), lambda b, h, j: (b, h))
    tile = pl.BlockSpec((tk, LANES), lambda b, h, j: (b * nk + j, h))
    shp = jax.ShapeDtypeStruct((t_all, B_HEADS * LANES), F32)
    return _pcall(
        body, name="fox_bwd", grid=(nb, B_HEADS, nk),
        in_specs=[full, tile, tile, full, full, full], out_specs=[full, tile, tile],
        out_shape=[shp, shp, shp],
        compiler_params=_params(("parallel", "parallel", "arbitrary")),
    )(qb, kb, vb, ob, dob, lse)


def _swa_bwd(qa, ka, va, oa, doa, lrow, sinks, bias, nb, seq):
    t_all = qa.shape[0]
    tq = min(512, seq)
    nq = seq // tq

    def body(sink_ref, q_ref, k_ref, v_ref, bias0_ref, bias_ref, o_ref, do_ref, l_ref,
             dq_ref, dk_ref, dv_ref, dsink_ref):
        qi = pl.program_id(2)
        sink = _sink_column(sink_ref, pl.program_id(1))

        @pl.when(qi == 0)
        def _():
            dk_ref[...] = jnp.zeros_like(dk_ref)
            dv_ref[...] = jnp.zeros_like(dv_ref)
            dsink_ref[...] = jnp.zeros_like(dsink_ref)

        for a in range(tq // WINDOW):
            t0 = qi * tq + a * WINDOW
            start = pl.multiple_of(jnp.maximum(t0 - WINDOW, 0), WINDOW)
            rows = slice(a * WINDOW, (a + 1) * WINDOW)
            win = pl.ds(start, 2 * WINDOW)
            q = _stack_heads(q_ref, rows)
            k = k_ref[win, :]
            v = v_ref[win, :]
            do = _stack_heads(do_ref, rows)
            lrow_t = jnp.max(_stack_heads(l_ref, rows), axis=-1, keepdims=True)
            p = jnp.exp(_dot_nt(q, k) + (bias0_ref if a == 0 else bias_ref)[...] - lrow_t)
            delta = jnp.sum(do.astype(F32) * _stack_heads(o_ref, rows).astype(F32), axis=-1, keepdims=True)
            ds = (p * (_dot_nt(do, v) - delta)).astype(BF16)
            dq = _dot(ds, k)
            dk_ref[win, :] += _dot_tn(ds, q)
            dv_ref[win, :] += _dot_tn(p.astype(BF16), do)
            sink_term = jnp.exp(sink - lrow_t) * delta
            for j in range(A_GROUP):
                part = slice(j * WINDOW, (j + 1) * WINDOW)
                dq_ref[rows, j * LANES:(j + 1) * LANES] = dq[part]
                dsink_ref[j:j + 1, :] -= jnp.broadcast_to(jnp.sum(sink_term[part], axis=0, keepdims=True), (1, LANES))

    smem, qspec, kspec, bias_first, bias_rest = _swa_specs(nq, tq, seq)
    return _pcall(
        body, name="swa_bwd", grid=(nb, A_KV_HEADS, nq),
        in_specs=[smem, qspec, kspec, kspec, bias_first, bias_rest, qspec, qspec, qspec],
        out_specs=[qspec, kspec, kspec, pl.BlockSpec((None, 8, LANES), lambda b, g, i: (b * A_KV_HEADS + g, 0, 0))],
        out_shape=[jax.ShapeDtypeStruct((t_all, A_HEADS * LANES), F32),
                   jax.ShapeDtypeStruct((t_all, A_KV_HEADS * LANES), F32),
                   jax.ShapeDtypeStruct((t_all, A_KV_HEADS * LANES), F32),
                   jax.ShapeDtypeStruct((nb * A_KV_HEADS, 8, LANES), F32)],
        compiler_params=_params(("parallel", "parallel", "arbitrary")),
    )(sinks, qa, ka, va, bias, bias, oa, doa, lrow)


def _dproj(pre, dqa, dka, dqb, dkb, dva, dvb, z, gain_row, seq):
    t_all = pre.shape[0]
    tm = min(256, seq)
    nt = t_all // tm
    tiles_per_seq = seq // tm
    triu = _tri(tm, True)
    sel = _dc_select()

    def body(pre_ref, dqa_ref, dka_ref, dqb_ref, dkb_ref, dva_ref, dvb_ref, z_ref, gain_ref, triu_ref, sel_ref,
             dproj_ref, small_ref, carry_ref):
        i = pl.program_id(0)

        @pl.when(i == 0)
        def _():
            small_ref[...] = jnp.zeros_like(small_ref)

        @pl.when(i % tiles_per_seq == 0)
        def _():
            carry_ref[...] = jnp.zeros_like(carry_ref)

        def norm_bwd(g, dhat):
            cols = slice(g * LANES, (g + 1) * LANES)
            p = pre_ref[:, cols].astype(F32)
            rr = lax.rsqrt(jnp.sum(p * p, axis=-1, keepdims=True) * (1.0 / HEAD_DIM) + EPS)
            n = p * rr
            dz = dhat * gain_ref[:, cols]
            dproj_ref[:, cols] = (rr * (dz - n * (jnp.sum(dz * n, axis=-1, keepdims=True) * (1.0 / HEAD_DIM)))
                                  ).astype(BF16)
            return jnp.sum(dhat * n, axis=0, keepdims=True)

        def group_sum(g0, d_ref, count, scale):
            acc = jnp.zeros((1, LANES), F32)
            for h in range(count):
                d = d_ref[:, h * LANES:(h + 1) * LANES]
                acc = acc + norm_bwd(g0 + h, d * scale if scale != 1.0 else d)
            return acc

        small_ref[0:1, :] += group_sum(G_QA, dqa_ref, A_HEADS, SCALE)
        small_ref[1:2, :] += group_sum(G_KA, dka_ref, A_KV_HEADS, 1.0)
        small_ref[2:3, :] += group_sum(G_QB, dqb_ref, B_HEADS, SCALE)
        small_ref[3:4, :] += group_sum(G_KB, dkb_ref, B_HEADS, 1.0)
        dproj_ref[:, G_VA * LANES:G_VB * LANES] = dva_ref[...].astype(BF16)
        dproj_ref[:, G_VB * LANES:G_F * LANES] = dvb_ref[...].astype(BF16)

        dc = jnp.zeros((tm, LANES), F32)
        for piece_q, piece_k in zip(_split3(dqb_ref[...]), _split3(dkb_ref[...])):
            dc = dc + _dot(jnp.concatenate([piece_q, piece_k], axis=1), sel_ref[...])
        dlf = _dot_exact(triu_ref[...], dc) + carry_ref[...]
        carry_ref[...] += jnp.sum(dc, axis=0, keepdims=True)
        dz = dlf / (1.0 + jnp.exp(z_ref[...]))
        small_ref[4:5, :] += jnp.sum(dz, axis=0, keepdims=True)
        dproj_ref[:, G_F * LANES:(G_F + 1) * LANES] = dz.astype(BF16)
        dproj_ref[:, (G_F + 1) * LANES:] = jnp.zeros((tm, LANES), BF16)

    def rev(n):
        return pl.BlockSpec((tm, n), lambda i: (nt - 1 - i, 0))

    return _pcall(
        body, name="dproj", grid=(nt,),
        in_specs=[rev(N_NORM_GROUPS * LANES), rev(A_HEADS * LANES), rev(A_KV_HEADS * LANES), rev(B_HEADS * LANES),
                  rev(B_HEADS * LANES), rev(A_KV_HEADS * LANES), rev(B_HEADS * LANES), rev(LANES),
                  _const((1, NP)), _const((tm, tm)), _const(sel.shape)],
        out_specs=[rev(NP), _const((8, LANES))],
        out_shape=[jax.ShapeDtypeStruct((t_all, NP), BF16), jax.ShapeDtypeStruct((8, LANES), F32)],
        scratch_shapes=[pltpu.VMEM((1, LANES), F32)],
        compiler_params=_params(("arbitrary",)),
    )(pre, dqa, dka, dqb, dkb, dva, dvb, z, gain_row, triu, sel)


def _dx(dproj, w_pad_t, x2, dh, g1):
    t_all = x2.shape[0]
    tm = min(256, t_all)

    def body(a_ref, w_ref, x_ref, dh_ref, g_ref, dx_ref, dg_ref):
        @pl.when(pl.program_id(0) == 0)
        def _():
            dg_ref[...] = jnp.zeros_like(dg_ref)

        dxn = _dot(a_ref[...], w_ref[...])
        x = x_ref[...]
        r = lax.rsqrt(jnp.mean(x * x, axis=-1, keepdims=True) + EPS)
        xh = x * r
        dg_ref[...] += jnp.sum(dxn * xh, axis=0, keepdims=True)
        dz = dxn * g_ref[...]
        dx_ref[...] = dh_ref[...] + r * (dz - xh * jnp.mean(dz * xh, axis=-1, keepdims=True))

    return _pcall(
        body, name="dx", grid=(t_all // tm,),
        in_specs=[_rows(tm, NP), _const((NP, D_MODEL)), _rows(tm, D_MODEL), _rows(tm, D_MODEL), _const((1, D_MODEL))],
        out_specs=[_rows(tm, D_MODEL), _const((1, D_MODEL))],
        out_shape=[jax.ShapeDtypeStruct((t_all, D_MODEL), F32), jax.ShapeDtypeStruct((1, D_MODEL), F32)],
        compiler_params=_params(("arbitrary",)),
    )(dproj, w_pad_t, x2, dh, g1)


def _dwin(dproj, xn):
    t_all = xn.shape[0]
    tt = min(512, t_all)
    half = NP // 2

    def body(a_ref, b_ref, o_ref):
        @pl.when(pl.program_id(1) == 0)
        def _():
            o_ref[...] = jnp.zeros_like(o_ref)

        o_ref[...] += _dot_tn(a_ref[...], b_ref[...])

    return _pcall(
        body, name="dwin", grid=(2, t_all // tt),
        in_specs=[pl.BlockSpec((tt, half), lambda j, t: (t, j)), pl.BlockSpec((tt, D_MODEL), lambda j, t: (t, 0))],
        out_specs=pl.BlockSpec((half, D_MODEL), lambda j, t: (j, 0)),
        out_shape=jax.ShapeDtypeStruct((NP, D_MODEL), F32),
        compiler_params=_params(("parallel", "arbitrary")),
    )(dproj, xn)


ANY = pl.BlockSpec(memory_space=pl.ANY)


def _place():
    return lax.axis_index("x"), lax.axis_index("y"), lax.axis_index("c")


def _allgather_halves(mine):
    m_per, n = mine.shape

    def body(x_ref, out_ref, send_sems, recv_sems, local_sem):
        x, y, c = _place()
        me, sibling = (x, y, c), (x, y, 1 - c)
        chips = [(1 - x, y), (x, 1 - y), (1 - x, 1 - y)]

        def rows(px, py, pc):
            return out_ref.at[pl.ds((4 * px + 2 * py + pc) * m_per, m_per), :]

        def copy(k, block, to, src=None):
            return pltpu.make_async_remote_copy(
                src_ref=rows(*block) if src is None else src, dst_ref=rows(*block),
                send_sem=send_sems.at[k], recv_sem=recv_sems.at[k], device_id=to, device_id_type=MESH)

        own = pltpu.make_async_copy(x_ref, rows(*me), local_sem)
        own.start()
        first = [copy(0, me, sibling, src=x_ref)]
        first += [copy(1 + j, me, (*chip, c), src=x_ref) for j, chip in enumerate(chips)]
        for cp in first:
            cp.start()
        passed = [copy(4 + j, (*chip, c), sibling) for j, chip in enumerate(chips)]
        for j, chip in enumerate(chips):
            copy(1 + j, (*chip, c), me).wait_recv()
            passed[j].start()
        copy(0, sibling, me).wait_recv()
        for j, chip in enumerate(chips):
            copy(4 + j, (*chip, 1 - c), me).wait_recv()
        for cp in first + passed:
            cp.wait_send()
        own.wait()

    return _pcall(
        body, name="allgather_weights",
        out_shape=jax.ShapeDtypeStruct((8 * m_per, n), mine.dtype),
        in_specs=[ANY], out_specs=ANY,
        scratch_shapes=[pltpu.SemaphoreType.DMA((7,)), pltpu.SemaphoreType.DMA((7,)), pltpu.SemaphoreType.DMA],
    )(mine)


def _rs_pair_exchange(g4):
    def body(g_ref, out_ref, send_sem, recv_sem):
        x, y, c = _place()
        cp = pltpu.make_async_remote_copy(
            src_ref=g_ref.at[:, 1 - c], dst_ref=out_ref, send_sem=send_sem, recv_sem=recv_sem,
            device_id=(x, y, 1 - c), device_id_type=MESH)
        cp.start()
        cp.wait()

    return _pcall(
        body, name="rs_pair_exchange",
        out_shape=jax.ShapeDtypeStruct((N_CHIPS, R_HALF, D_MODEL), F32),
        in_specs=[ANY], out_specs=ANY,
        scratch_shapes=[pltpu.SemaphoreType.DMA, pltpu.SemaphoreType.DMA],
    )(g4)


def _rs_pair_add(g4, got, c_idx):
    def body(c_ref, a_ref, b_ref, o_ref, ob_ref):
        pair = a_ref[...] + b_ref[...]
        o_ref[...] = pair
        ob_ref[...] = pair.astype(BF16)

    blk = pl.BlockSpec((None, R_HALF, D_MODEL), lambda s, c_ref: (s, 0, 0))
    return _pcall(
        body, name="rs_pair_add",
        grid_spec=pltpu.PrefetchScalarGridSpec(
            num_scalar_prefetch=1, grid=(N_CHIPS,),
            in_specs=[pl.BlockSpec((None, None, R_HALF, D_MODEL), lambda s, c_ref: (s, c_ref[0], 0, 0)), blk],
            out_specs=[blk, blk]),
        out_shape=[jax.ShapeDtypeStruct((N_CHIPS, R_HALF, D_MODEL), F32),
                   jax.ShapeDtypeStruct((N_CHIPS, R_HALF, D_MODEL), BF16)],
        compiler_params=_params(("parallel",)),
    )(c_idx, g4, got)


def _rs_chip_exchange(p4):
    def body(p_ref, out_ref, send_sems, recv_sems):
        x, y, c = _place()
        chips = [(1 - x, y), (x, 1 - y), (1 - x, 1 - y)]
        cps = [pltpu.make_async_remote_copy(
            src_ref=p_ref.at[2 * cx + cy], dst_ref=out_ref.at[j], send_sem=send_sems.at[j],
            recv_sem=recv_sems.at[j], device_id=(cx, cy, c), device_id_type=MESH)
            for j, (cx, cy) in enumerate(chips)]
        for cp in cps:
            cp.start()
        for cp in cps:
            cp.wait()

    return _pcall(
        body, name="rs_chip_exchange",
        out_shape=jax.ShapeDtypeStruct((3, R_HALF, D_MODEL), p4.dtype),
        in_specs=[ANY], out_specs=ANY,
        scratch_shapes=[pltpu.SemaphoreType.DMA((3,)), pltpu.SemaphoreType.DMA((3,))],
    )(p4)


def _rs_chip_add(p4, got, s_idx):
    tr = R_HALF // 7

    def body(s_ref, a_ref, b_ref, o_ref):
        o_ref[...] = ((a_ref[...] + b_ref[0].astype(F32)) + b_ref[1].astype(F32)) + b_ref[2].astype(F32)

    return _pcall(
        body, name="rs_chip_add",
        grid_spec=pltpu.PrefetchScalarGridSpec(
            num_scalar_prefetch=1, grid=(R_HALF // tr,),
            in_specs=[pl.BlockSpec((None, tr, D_MODEL), lambda i, s_ref: (s_ref[0], i, 0)),
                      pl.BlockSpec((3, tr, D_MODEL), lambda i, s_ref: (0, i, 0))],
            out_specs=pl.BlockSpec((tr, D_MODEL), lambda i, s_ref: (i, 0))),
        out_shape=jax.ShapeDtypeStruct((R_HALF, D_MODEL), F32),
        compiler_params=_params(("parallel",)),
    )(s_idx, p4, got)


def _rs_pair_share(r_half):
    def body(r_ref, out_ref, send_sem, recv_sem, local_sem):
        x, y, c = _place()
        own = pltpu.make_async_copy(r_ref, out_ref.at[c], local_sem)
        own.start()
        cp = pltpu.make_async_remote_copy(
            src_ref=r_ref, dst_ref=out_ref.at[c], send_sem=send_sem, recv_sem=recv_sem,
            device_id=(x, y, 1 - c), device_id_type=MESH)
        cp.start()
        cp.wait()
        own.wait()

    return _pcall(
        body, name="rs_pair_share",
        out_shape=jax.ShapeDtypeStruct((2, R_HALF, D_MODEL), F32),
        in_specs=[ANY], out_specs=ANY,
        scratch_shapes=[pltpu.SemaphoreType.DMA, pltpu.SemaphoreType.DMA, pltpu.SemaphoreType.DMA],
    )(r_half)


def _adam(w, g, m, v):
    m2 = ADAM_B1 * m + (1.0 - ADAM_B1) * g
    v2 = ADAM_B2 * v + (1.0 - ADAM_B2) * (g * g)
    m_hat = m2 / (1.0 - ADAM_B1 ** ADAM_STEP)
    v_hat = v2 / (1.0 - ADAM_B2 ** ADAM_STEP)
    return -ADAM_LR * (m_hat / (jnp.sqrt(v_hat) + ADAM_EPS) + ADAM_WD * w), m2, v2


def _small_allreduce_adamw(part, w, m, v):
    def body(p_ref, w_ref, m_ref, v_ref, g_ref, d_ref, m2_ref, v2_ref, buf, send_sems, recv_sems):
        x, y, c = _place()
        me = 4 * x + 2 * y + c
        cps = []
        for k in range(1, 8):
            peer = (1 - x if k & 4 else x, 1 - y if k & 2 else y, 1 - c if k & 1 else c)
            cps.append(pltpu.make_async_remote_copy(
                src_ref=p_ref, dst_ref=buf.at[me], send_sem=send_sems.at[k - 1], recv_sem=recv_sems.at[k - 1],
                device_id=peer, device_id_type=MESH))
        for cp in cps:
            cp.start()
        buf[me] = p_ref[...]
        for cp in cps:
            cp.wait()
        g = buf[0]
        for k in range(1, 8):
            g = g + buf[k]
        g_ref[...] = g
        d_ref[...], m2_ref[...], v2_ref[...] = _adam(w_ref[...], g, m_ref[...], v_ref[...])

    vm = pl.BlockSpec(memory_space=pltpu.VMEM)
    shp = jax.ShapeDtypeStruct((SMALL_ROWS, LANES), F32)
    return _pcall(
        body, name="small_allreduce_adamw",
        out_shape=[shp, shp, shp, shp], in_specs=[vm, vm, vm, vm], out_specs=[vm, vm, vm, vm],
        scratch_shapes=[pltpu.VMEM((8, SMALL_ROWS, LANES), F32), pltpu.SemaphoreType.DMA((7,)),
                        pltpu.SemaphoreType.DMA((7,))],
    )(part, w, m, v)


def _adamw(w, g, m, v, name):
    rows, cols = w.shape
    tr = min(256, rows)

    def body(w_ref, g_ref, m_ref, v_ref, d_ref, m2_ref, v2_ref):
        d_ref[...], m2_ref[...], v2_ref[...] = _adam(w_ref[...], g_ref[...], m_ref[...], v_ref[...])

    spec = _rows(tr, cols)
    shp = jax.ShapeDtypeStruct((rows, cols), F32)
    return _pcall(
        body, name=name, grid=(rows // tr,), in_specs=[spec] * 4, out_specs=[spec] * 3, out_shape=[shp] * 3,
        compiler_params=_params(("parallel",)),
    )(w, g, m, v)


def _pad_lanes(v):
    return jnp.pad(v, (0, LANES - v.shape[0]))


def _pad_head_rows(w_t, heads):
    n = w_t.shape[1]
    return jnp.pad(w_t.reshape(heads, HEAD_DIM, n), ((0, 0), (0, LANES - HEAD_DIM), (0, 0))).reshape(heads * LANES, n)


def _unpad_head_rows(w_t, heads):
    n = w_t.shape[1]
    return w_t.reshape(heads, LANES, n)[:, :HEAD_DIM].reshape(heads * HEAD_DIM, n)


def _in_rows_pad(w_in_t):
    qa, ka, va, qb, kb, vb, f = jnp.split(w_in_t, [512, 640, 768, 1280, 1792, 2304], axis=0)
    f = jnp.pad(f, ((0, 2 * LANES - B_HEADS), (0, 0)))
    return jnp.concatenate([_pad_head_rows(qa, 8), _pad_head_rows(ka, 2), _pad_head_rows(qb, 8),
                            _pad_head_rows(kb, 8), _pad_head_rows(va, 2), _pad_head_rows(vb, 8), f], axis=0)


def _in_rows_unpad(d):
    qa = _unpad_head_rows(d[G_QA * LANES:G_KA * LANES], 8)
    ka = _unpad_head_rows(d[G_KA * LANES:G_QB * LANES], 2)
    qb = _unpad_head_rows(d[G_QB * LANES:G_KB * LANES], 8)
    kb = _unpad_head_rows(d[G_KB * LANES:G_VA * LANES], 8)
    va = _unpad_head_rows(d[G_VA * LANES:G_VB * LANES], 2)
    vb = _unpad_head_rows(d[G_VB * LANES:G_F * LANES], 8)
    f = d[G_F * LANES:G_F * LANES + B_HEADS]
    return jnp.concatenate([qa, ka, va, qb, kb, vb, f], axis=0)


def _pack_small(g1, bf, qa, ka, sk, qb, kb, g2, loss_row):
    rows = [g1.reshape(8, LANES), g2.reshape(8, LANES)]
    rows += [_pad_lanes(t)[None] for t in (qa, ka, qb, kb, bf, sk)]
    rows += [loss_row, jnp.zeros((1, LANES), F32)]
    return jnp.concatenate(rows, axis=0)


def _unpack_small(p):
    return (p[0:8].reshape(D_MODEL), p[20, :B_HEADS], p[16, :HEAD_DIM], p[17, :HEAD_DIM], p[21, :A_HEADS],
            p[18, :HEAD_DIM], p[19, :HEAD_DIM], p[8:16].reshape(D_MODEL))


def kernel(x, attn_norm_g, w_in, b_forget, q_norm_a, k_norm_a, sink_logits, q_norm_b, k_norm_b, w_out, mlp_norm_g, w_up, w_down, loss_target, m_attn_norm_g, m_w_in, m_b_forget, m_q_norm_a, m_k_norm_a, m_sink_logits, m_q_norm_b, m_k_norm_b, m_w_out, m_mlp_norm_g, m_w_up, m_w_down, v_attn_norm_g, v_w_in, v_b_forget, v_q_norm_a, v_k_norm_a, v_sink_logits, v_q_norm_b, v_k_norm_b, v_w_out, v_mlp_norm_g, v_w_up, v_w_down):
    nb, seq, _ = x.shape
    t_all = nb * seq
    c_idx = lax.axis_index("c")
    s_idx = 2 * lax.axis_index("x") + lax.axis_index("y")

    packed = jnp.concatenate([jnp.pad(w_in.T, ((0, IN_SHARD_P - IN_SHARD), (0, 0))), w_out, w_up, w_down], axis=0)
    mine = lax.dynamic_slice_in_dim(packed.astype(BF16).reshape(2, R_HALF, D_MODEL), c_idx, 1, axis=0)[0]
    gathered = _allgather_halves(mine).reshape(N_CHIPS, R_ALL, D_MODEL)
    w_in_t = gathered[:, :IN_SHARD].reshape(IN_WIDTH, D_MODEL)
    w_pad_t = _in_rows_pad(w_in_t)
    w_pad = w_pad_t.T
    wo_pad = _pad_head_rows(gathered[:, R_OUT:R_UP].reshape(D_MODEL, D_MODEL), A_HEADS + B_HEADS)
    wo_pad_t = wo_pad.T
    w_up_blocks = gathered[:, R_UP:R_DOWN]
    w_up_t = jnp.swapaxes(w_up_blocks, 1, 2).reshape(D_FF, D_MODEL)
    w_down_f = gathered[:, R_DOWN:].reshape(D_FF, D_MODEL)
    w_down_t = w_down_f.T

    ones = jnp.ones((LANES,), F32)
    gain_row = jnp.concatenate(
        [jnp.tile(_pad_lanes(q_norm_a), 8), jnp.tile(_pad_lanes(k_norm_a), 2), jnp.tile(_pad_lanes(q_norm_b), 8),
         jnp.tile(_pad_lanes(k_norm_b), 8), jnp.tile(ones, N_GROUPS - N_NORM_GROUPS)])[None]
    b_row = _pad_lanes(b_forget)[None]
    g1 = attn_norm_g[None]
    g2 = mlp_norm_g[None]
    slopes = jnp.exp2(-(8.0 / A_HEADS) * (jnp.arange(A_HEADS, dtype=F32) + 1.0))

    x2 = x.reshape(t_all, D_MODEL)
    tgt = loss_target.reshape(t_all, D_MODEL)

    xn, pre, qa, ka, va, qb, kb, vb, z = _inproj(x2, g1, w_pad, gain_row, b_row, seq)
    swa_bias = _swa_bias(slopes)
    oa, la = _swa_fwd(qa, ka, va, sink_logits, swa_bias, nb, seq)
    ob, lse = _fox_fwd(qb, kb, vb, nb, seq)
    h, hn = _outproj(x2, oa, ob, wo_pad, g2)
    ru, hid = _mlp_up(hn, w_up_blocks)
    dy, dyb, loss_acc = _mlp_down(hid, w_down_f, h, tgt)

    du = _mlp_dhid(dyb, w_down_t, ru)
    d_w_down = _wgrad(hid, dyb, "dw_down", D_FF // D_MODEL, 1)
    d_w_up = _wgrad(hn, du, "dw_up", 1, D_FF // D_MODEL)
    dh, dhb, d_g2 = _mlp_dhn(du, w_up_t, h, dy, g2)
    doa, dob = _dmixed(dhb, wo_pad_t)
    d_wo_a = _wgrad(oa, dhb, "dw_out_a", 1, 1)[0]
    d_wo_b = _wgrad(ob, dhb, "dw_out_b", 1, 1)[0]
    dqb, dkb, dvb = _fox_bwd(qb, kb, vb, ob, dob, lse, nb, seq)
    dqa, dka, dva, dsink = _swa_bwd(qa, ka, va, oa, doa, la, sink_logits, swa_bias, nb, seq)
    dproj, small = _dproj(pre, dqa, dka, dqb, dkb, dva, dvb, z, gain_row, seq)
    grad_x, d_g1 = _dx(dproj, w_pad_t, x2, dh, g1)
    d_w_in_t = _dwin(dproj, xn)

    d_w_out = jnp.concatenate([_unpad_head_rows(d_wo_a, 8), _unpad_head_rows(d_wo_b, 8)], axis=0)
    g_pack = jnp.concatenate([
        jnp.pad(_in_rows_unpad(d_w_in_t).reshape(N_CHIPS, IN_SHARD, D_MODEL),
                ((0, 0), (0, IN_SHARD_P - IN_SHARD), (0, 0))),
        d_w_out.reshape(N_CHIPS, D_MODEL // N_CHIPS, D_MODEL), d_w_up, d_w_down], axis=1)
    g4 = g_pack.reshape(N_CHIPS, 2, R_HALF, D_MODEL)
    pair, pair_bf = _rs_pair_add(g4, _rs_pair_exchange(g4), c_idx.reshape(1).astype(jnp.int32))
    r_half = _rs_chip_add(pair, _rs_chip_exchange(pair_bf), s_idx.reshape(1).astype(jnp.int32))
    red = _rs_pair_share(r_half).reshape(R_ALL, D_MODEL)
    g_w_in = red[:IN_SHARD].T
    g_w_out = red[R_OUT:R_UP]
    g_w_up = red[R_UP:R_DOWN]
    g_w_down = red[R_DOWN:]

    loss_row = loss_acc[0:1] * (0.5 / D_MODEL)
    d_sink = dsink[:, :A_GROUP, 0].reshape(nb, A_HEADS).sum(axis=0)
    part = _pack_small(d_g1[0], small[4, :B_HEADS], small[0, :HEAD_DIM], small[1, :HEAD_DIM], d_sink,
                       small[2, :HEAD_DIM], small[3, :HEAD_DIM], d_g2[0], loss_row)
    zero_row = jnp.zeros((1, LANES), F32)
    smalls = lambda t: _pack_small(*t, zero_row)
    w_small = smalls((attn_norm_g, b_forget, q_norm_a, k_norm_a, sink_logits, q_norm_b, k_norm_b, mlp_norm_g))
    m_small = smalls((m_attn_norm_g, m_b_forget, m_q_norm_a, m_k_norm_a, m_sink_logits, m_q_norm_b, m_k_norm_b,
                      m_mlp_norm_g))
    v_small = smalls((v_attn_norm_g, v_b_forget, v_q_norm_a, v_k_norm_a, v_sink_logits, v_q_norm_b, v_k_norm_b,
                      v_mlp_norm_g))
    g_s, d_s, m_s, v_s = _small_allreduce_adamw(part, w_small, m_small, v_small)
    loss = g_s[ROW_LOSS, 0]

    big = {}
    for name, w, g, m, v in (("adamw_w_in", w_in, g_w_in, m_w_in, v_w_in),
                             ("adamw_w_out", w_out, g_w_out, m_w_out, v_w_out),
                             ("adamw_w_up", w_up, g_w_up, m_w_up, v_w_up),
                             ("adamw_w_down", w_down, g_w_down, m_w_down, v_w_down)):
        big[name] = (g,) + tuple(_adamw(w, g, m, v, name))

    def assemble(k, small_pack):
        s = _unpack_small(small_pack)
        return (s[0], big["adamw_w_in"][k], s[1], s[2], s[3], s[4], s[5], s[6], big["adamw_w_out"][k], s[7],
                big["adamw_w_up"][k], big["adamw_w_down"][k])

    return (loss, grad_x.reshape(nb, seq, D_MODEL), *assemble(0, g_s), *assemble(1, d_s), *assemble(2, m_s),
            *assemble(3, v_s))
```

```python
import functools

import numpy as np
import jax
import jax.numpy as jnp
from jax import lax
from jax.experimental import pallas as pl
from jax.experimental.pallas import tpu as pltpu

F32 = jnp.float32
BF16 = jnp.bfloat16

D_MODEL = 1024
HEAD_DIM = 64
LANES = 128
A_HEADS = 8
A_KV_HEADS = 2
A_GROUP = A_HEADS // A_KV_HEADS
B_HEADS = 8
WINDOW = 128
D_FF = 4096
IN_WIDTH = 2312
EPS = 1e-6
SCALE = 0.125
NEG = -1e30

G_QA, G_KA, G_QB, G_KB, G_VA, G_VB, G_F = 0, 8, 10, 18, 26, 28, 36
N_NORM_GROUPS = 26
N_GROUPS = 38
NP = N_GROUPS * LANES
MIXED_P = (A_HEADS + B_HEADS) * LANES

N_CHIPS = 4
IN_SHARD = IN_WIDTH // N_CHIPS
IN_SHARD_P = 608
R_OUT = IN_SHARD_P
R_UP = R_OUT + D_MODEL // N_CHIPS
R_DOWN = R_UP + D_MODEL
R_ALL = R_DOWN + D_FF // N_CHIPS
R_HALF = R_ALL // 2

SMALL_ROWS = 24
ROW_LOSS = 22

ADAM_LR = 0.001
ADAM_B1 = 0.9
ADAM_B2 = 0.999
ADAM_EPS = 1e-08
ADAM_WD = 0.01
ADAM_STEP = 10

VMEM_LIMIT = 52 * 1024 * 1024
MESH = pl.DeviceIdType.MESH


def _pcall(body, **kw):
    return pl.pallas_call(body, **kw)


def _params(sem=None):
    return pltpu.CompilerParams(dimension_semantics=sem, vmem_limit_bytes=VMEM_LIMIT)


def _dot(a, b):
    return jnp.dot(a, b, preferred_element_type=F32)


def _dot_nt(a, b):
    return lax.dot_general(a, b, (((1,), (1,)), ((), ())), preferred_element_type=F32)


def _dot_tn(a, b):
    return lax.dot_general(a, b, (((0,), (0,)), ((), ())), preferred_element_type=F32)


def _split3(x):
    hi = x.astype(BF16)
    r1 = x - hi.astype(F32)
    mid = r1.astype(BF16)
    lo = (r1 - mid.astype(F32)).astype(BF16)
    return hi, mid, lo


def _dot_exact(mat, x):
    hi, mid, lo = _split3(x)
    return _dot(mat, lo) + _dot(mat, mid) + _dot(mat, hi)


def _const(shape):
    zeros = (0,) * len(shape)
    return pl.BlockSpec(shape, lambda *_: zeros)


def _rows(tm, n):
    return pl.BlockSpec((tm, n), lambda i: (i, 0))


def _aug_select():
    e = np.zeros((3 * LANES, 2 * B_HEADS * LANES), np.float32)
    for j in range(3):
        for h in range(B_HEADS):
            e[j * LANES + h, h * LANES + HEAD_DIM + j] = 1.0
            e[j * LANES + h, (B_HEADS + h) * LANES + HEAD_DIM + 3 + j] = -1.0
    return jnp.asarray(e, BF16)


def _dc_select():
    e = np.zeros((2 * B_HEADS * LANES, LANES), np.float32)
    for h in range(B_HEADS):
        e[h * LANES + HEAD_DIM, h] = 1.0
        e[(B_HEADS + h) * LANES + HEAD_DIM + 3, h] = -1.0
    return jnp.asarray(e, BF16)


def _tri(n, upper):
    t = np.tril(np.ones((n, n), np.float32))
    return jnp.asarray(t.T if upper else t, BF16)


def _inproj(x2, g1, w_pad, gain_row, b_row, seq):
    t_all = x2.shape[0]
    tm = min(256, seq)
    tiles_per_seq = seq // tm
    tri = _tri(tm, False)
    esel = _aug_select()

    def body(x_ref, g_ref, w_ref, gain_ref, b_ref, tri_ref, e_ref,
             xn_ref, pre_ref, qa_ref, ka_ref, va_ref, qb_ref, kb_ref, vb_ref, z_ref, carry_ref):
        i = pl.program_id(0)

        @pl.when(i % tiles_per_seq == 0)
        def _():
            carry_ref[...] = jnp.zeros_like(carry_ref)

        x = x_ref[...]
        r = lax.rsqrt(jnp.mean(x * x, axis=-1, keepdims=True) + EPS)
        xn = (x * r * g_ref[...]).astype(BF16)
        xn_ref[...] = xn
        proj = _dot(xn, w_ref[...])
        pre_ref[...] = proj[:, :N_NORM_GROUPS * LANES].astype(BF16)
        lane = lax.broadcasted_iota(jnp.int32, (tm, LANES), 1)

        z = proj[:, G_F * LANES:(G_F + 1) * LANES] + b_ref[...]
        z_ref[...] = z
        lf = jnp.minimum(z, 0.0) - jnp.log(1.0 + jnp.exp(-jnp.abs(z)))
        lf = jnp.where(lane < B_HEADS, lf, 0.0)
        c = _dot_exact(tri_ref[...], lf) + carry_ref[...]
        carry_ref[...] += jnp.sum(lf, axis=0, keepdims=True)
        aug = _dot(jnp.concatenate(_split3(c), axis=1), e_ref[...])

        def hnorm(g):
            p = proj[:, g * LANES:(g + 1) * LANES]
            rr = lax.rsqrt(jnp.sum(p * p, axis=-1, keepdims=True) * (1.0 / HEAD_DIM) + EPS)
            return p * rr * gain_ref[:, g * LANES:(g + 1) * LANES]

        ones_q = jnp.where((lane >= HEAD_DIM + 3) & (lane < HEAD_DIM + 6), 1.0, 0.0)
        ones_k = jnp.where((lane >= HEAD_DIM) & (lane < HEAD_DIM + 3), 1.0, 0.0)
        for h in range(A_HEADS):
            qa_ref[:, h * LANES:(h + 1) * LANES] = (hnorm(G_QA + h) * SCALE).astype(BF16)
        for h in range(A_KV_HEADS):
            ka_ref[:, h * LANES:(h + 1) * LANES] = hnorm(G_KA + h).astype(BF16)
        for h in range(B_HEADS):
            qb_ref[:, h * LANES:(h + 1) * LANES] = (
                hnorm(G_QB + h) * SCALE + aug[:, h * LANES:(h + 1) * LANES] + ones_q).astype(BF16)
            kb_ref[:, h * LANES:(h + 1) * LANES] = (
                hnorm(G_KB + h) + aug[:, (B_HEADS + h) * LANES:(B_HEADS + h + 1) * LANES] + ones_k).astype(BF16)
        va_ref[...] = proj[:, G_VA * LANES:G_VB * LANES].astype(BF16)
        vb_ref[...] = proj[:, G_VB * LANES:G_F * LANES].astype(BF16)

    widths = [(D_MODEL, BF16), (N_NORM_GROUPS * LANES, BF16), (A_HEADS * LANES, BF16), (A_KV_HEADS * LANES, BF16),
              (A_KV_HEADS * LANES, BF16), (B_HEADS * LANES, BF16), (B_HEADS * LANES, BF16), (B_HEADS * LANES, BF16),
              (LANES, F32)]
    return _pcall(
        body, name="inproj", grid=(t_all // tm,),
        in_specs=[_rows(tm, D_MODEL), _const((1, D_MODEL)), _const((D_MODEL, NP)), _const((1, NP)),
                  _const((1, LANES)), _const((tm, tm)), _const(esel.shape)],
        out_specs=[_rows(tm, w) for w, _ in widths],
        out_shape=[jax.ShapeDtypeStruct((t_all, w), dt) for w, dt in widths],
        scratch_shapes=[pltpu.VMEM((1, LANES), F32)],
        compiler_params=_params(("arbitrary",)),
    )(x2, g1, w_pad, gain_row, b_row, tri, esel)


def _fox_fwd(qb, kb, vb, nb, seq):
    t_all = qb.shape[0]
    tq = min(512, seq)
    nq = seq // tq

    def body(q_ref, k_ref, v_ref, o_ref, lse_ref):
        qi = pl.program_id(2)
        q = q_ref[...]

        def step(j, carry, masked):
            m, l, acc = carry
            off = pl.multiple_of(j * tq, tq)
            k = k_ref[pl.ds(off, tq), :]
            v = v_ref[pl.ds(off, tq), :]
            s = _dot_nt(q, k)
            if masked:
                row = lax.broadcasted_iota(jnp.int32, (tq, tq), 0)
                col = lax.broadcasted_iota(jnp.int32, (tq, tq), 1)
                s = jnp.where(row >= col, s, NEG)
            m_new = jnp.maximum(m, jnp.max(s, axis=-1, keepdims=True))
            alpha = jnp.exp(m - m_new)
            p = jnp.exp(s - m_new)
            l = alpha * l + jnp.sum(p, axis=-1, keepdims=True)
            acc = alpha * acc + _dot(p.astype(BF16), v)
            return m_new, l, acc

        init = (jnp.full((tq, 1), NEG, F32), jnp.zeros((tq, 1), F32), jnp.zeros((tq, LANES), F32))
        carry = lax.fori_loop(0, qi, lambda j, cr: step(j, cr, False), init)
        m, l, acc = step(qi, carry, True)
        o_ref[...] = (acc / l).astype(BF16)
        lse_ref[...] = jnp.broadcast_to(m + jnp.log(l), (tq, LANES))

    qspec = pl.BlockSpec((tq, LANES), lambda b, h, i: (b * nq + i, h))
    kspec = pl.BlockSpec((seq, LANES), lambda b, h, i: (b, h))
    return _pcall(
        body, name="fox_fwd", grid=(nb, B_HEADS, nq),
        in_specs=[qspec, kspec, kspec], out_specs=[qspec, qspec],
        out_shape=[jax.ShapeDtypeStruct((t_all, B_HEADS * LANES), BF16),
                   jax.ShapeDtypeStruct((t_all, B_HEADS * LANES), F32)],
        compiler_params=_params(("parallel", "parallel", "arbitrary")),
    )(qb, kb, vb)


def _swa_bias(slopes):
    row = jnp.arange(A_GROUP * WINDOW, dtype=jnp.int32)[:, None] % WINDOW
    col = jnp.arange(2 * WINDOW, dtype=jnp.int32)[None, :]
    slope_rows = jnp.repeat(slopes.reshape(A_KV_HEADS, A_GROUP), WINDOW, axis=1)[:, :, None]
    out = []
    for t_rel in (0, WINDOW):
        dist = t_rel + row - col
        valid = (dist >= 0) & (dist < WINDOW)
        out.append(jnp.where(valid[None], -slope_rows * dist.astype(F32)[None], NEG))
    return jnp.stack(out)


def _stack_heads(ref, rows):
    return jnp.concatenate([ref[rows, j * LANES:(j + 1) * LANES] for j in range(A_GROUP)], axis=0)


def _sink_column(sink_ref, g):
    return jnp.concatenate([jnp.full((WINDOW, 1), sink_ref[g * A_GROUP + j], F32) for j in range(A_GROUP)], axis=0)


def _swa_specs(nq, tq, seq):
    smem = pl.BlockSpec(memory_space=pltpu.SMEM)
    qspec = pl.BlockSpec((tq, A_GROUP * LANES), lambda b, g, i: (b * nq + i, g))
    kspec = pl.BlockSpec((seq, LANES), lambda b, g, i: (b, g))
    bias_first = pl.BlockSpec((None, None, A_GROUP * WINDOW, 2 * WINDOW),
                              lambda b, g, i: (jnp.minimum(i, 1), g, 0, 0))
    bias_rest = pl.BlockSpec((None, None, A_GROUP * WINDOW, 2 * WINDOW), lambda b, g, i: (1, g, 0, 0))
    return smem, qspec, kspec, bias_first, bias_rest


def _swa_fwd(qa, ka, va, sinks, bias, nb, seq):
    t_all = qa.shape[0]
    tq = min(512, seq)
    nq = seq // tq

    def body(sink_ref, q_ref, k_ref, v_ref, bias0_ref, bias_ref, o_ref, l_ref):
        qi = pl.program_id(2)
        sink = _sink_column(sink_ref, pl.program_id(1))
        for a in range(tq // WINDOW):
            t0 = qi * tq + a * WINDOW
            start = pl.multiple_of(jnp.maximum(t0 - WINDOW, 0), WINDOW)
            rows = slice(a * WINDOW, (a + 1) * WINDOW)
            k = k_ref[pl.ds(start, 2 * WINDOW), :]
            v = v_ref[pl.ds(start, 2 * WINDOW), :]
            s = _dot_nt(_stack_heads(q_ref, rows), k) + (bias0_ref if a == 0 else bias_ref)[...]
            m = jnp.maximum(jnp.max(s, axis=-1, keepdims=True), sink)
            p = jnp.exp(s - m)
            den = jnp.sum(p, axis=-1, keepdims=True) + jnp.exp(sink - m)
            o = _dot((p / den).astype(BF16), v).astype(BF16)
            lrow = jnp.broadcast_to(m + jnp.log(den), (A_GROUP * WINDOW, LANES))
            for j in range(A_GROUP):
                o_ref[rows, j * LANES:(j + 1) * LANES] = o[j * WINDOW:(j + 1) * WINDOW]
                l_ref[rows, j * LANES:(j + 1) * LANES] = lrow[j * WINDOW:(j + 1) * WINDOW]

    smem, qspec, kspec, bias_first, bias_rest = _swa_specs(nq, tq, seq)
    return _pcall(
        body, name="swa_fwd", grid=(nb, A_KV_HEADS, nq),
        in_specs=[smem, qspec, kspec, kspec, bias_first, bias_rest], out_specs=[qspec, qspec],
        out_shape=[jax.ShapeDtypeStruct((t_all, A_HEADS * LANES), BF16),
                   jax.ShapeDtypeStruct((t_all, A_HEADS * LANES), F32)],
        compiler_params=_params(("parallel", "parallel", "arbitrary")),
    )(sinks, qa, ka, va, bias, bias)


def _outproj(x2, oa, ob, wo_pad, g2):
    t_all = x2.shape[0]
    tm = min(512, t_all)
    half = A_HEADS * LANES

    def body(x_ref, oa_ref, ob_ref, w_ref, g_ref, h_ref, hn_ref):
        h = x_ref[...] + _dot(oa_ref[...], w_ref[:half, :]) + _dot(ob_ref[...], w_ref[half:, :])
        h_ref[...] = h
        r = lax.rsqrt(jnp.mean(h * h, axis=-1, keepdims=True) + EPS)
        hn_ref[...] = (h * r * g_ref[...]).astype(BF16)

    return _pcall(
        body, name="outproj", grid=(t_all // tm,),
        in_specs=[_rows(tm, D_MODEL), _rows(tm, half), _rows(tm, half), _const((MIXED_P, D_MODEL)),
                  _const((1, D_MODEL))],
        out_specs=[_rows(tm, D_MODEL), _rows(tm, D_MODEL)],
        out_shape=[jax.ShapeDtypeStruct((t_all, D_MODEL), F32), jax.ShapeDtypeStruct((t_all, D_MODEL), BF16)],
        compiler_params=_params(("parallel",)),
    )(x2, oa, ob, wo_pad, g2)


def _mlp_up(hn, w_up_blocks):
    t_all = hn.shape[0]
    tm = min(512, t_all)
    nj = D_FF // D_MODEL

    def body(a_ref, w_ref, ru_ref, hid_ref):
        ru = jnp.maximum(_dot(a_ref[...], w_ref[...]), 0.0)
        ru_ref[...] = ru.astype(BF16)
        hid_ref[...] = (ru * ru).astype(BF16)

    ospec = pl.BlockSpec((tm, D_MODEL), lambda j, i: (i, j))
    return _pcall(
        body, name="mlp_up", grid=(nj, t_all // tm),
        in_specs=[pl.BlockSpec((tm, D_MODEL), lambda j, i: (i, 0)),
                  pl.BlockSpec((None, D_MODEL, D_MODEL), lambda j, i: (j, 0, 0))],
        out_specs=[ospec, ospec],
        out_shape=[jax.ShapeDtypeStruct((t_all, D_FF), BF16), jax.ShapeDtypeStruct((t_all, D_FF), BF16)],
        compiler_params=_params(("parallel", "parallel")),
    )(hn, w_up_blocks)


def _mlp_down(hid, w_down, h, tgt):
    t_all = h.shape[0]
    tm = min(256, t_all)

    def body(a_ref, w_ref, h_ref, t_ref, dy_ref, dyb_ref, loss_ref):
        @pl.when(pl.program_id(0) == 0)
        def _():
            loss_ref[...] = jnp.zeros_like(loss_ref)

        y = h_ref[...] + _dot(a_ref[...], w_ref[...])
        err = y - t_ref[...]
        loss_ref[...] += jnp.sum(err * err)
        dy = err * (1.0 / D_MODEL)
        dy_ref[...] = dy
        dyb_ref[...] = dy.astype(BF16)

    return _pcall(
        body, name="mlp_down", grid=(t_all // tm,),
        in_specs=[_rows(tm, D_FF), _const((D_FF, D_MODEL)), _rows(tm, D_MODEL), _rows(tm, D_MODEL)],
        out_specs=[_rows(tm, D_MODEL), _rows(tm, D_MODEL), _const((8, LANES))],
        out_shape=[jax.ShapeDtypeStruct((t_all, D_MODEL), F32), jax.ShapeDtypeStruct((t_all, D_MODEL), BF16),
                   jax.ShapeDtypeStruct((8, LANES), F32)],
        compiler_params=_params(("arbitrary",)),
    )(hid, w_down, h, tgt)


def _mlp_dhid(dyb, w_down_t, ru):
    t_all = dyb.shape[0]
    tm = min(512, t_all)
    nj = D_FF // D_MODEL

    def body(a_ref, w_ref, ru_ref, du_ref):
        du_ref[...] = (_dot(a_ref[...], w_ref[...]) * (2.0 * ru_ref[...].astype(F32))).astype(BF16)

    ospec = pl.BlockSpec((tm, D_MODEL), lambda j, i: (i, j))
    return _pcall(
        body, name="mlp_dhid", grid=(nj, t_all // tm),
        in_specs=[pl.BlockSpec((tm, D_MODEL), lambda j, i: (i, 0)),
                  pl.BlockSpec((D_MODEL, D_MODEL), lambda j, i: (0, j)), ospec],
        out_specs=ospec,
        out_shape=jax.ShapeDtypeStruct((t_all, D_FF), BF16),
        compiler_params=_params(("parallel", "parallel")),
    )(dyb, w_down_t, ru)


def _wgrad(a, b, name, a_col_blocks, b_col_blocks):
    t_all = a.shape[0]
    tt = min(512, t_all)
    nj = max(a_col_blocks, b_col_blocks)
    wa = a.shape[1] // a_col_blocks
    wb = b.shape[1] // b_col_blocks

    def body(a_ref, b_ref, o_ref):
        @pl.when(pl.program_id(1) == 0)
        def _():
            o_ref[...] = jnp.zeros_like(o_ref)

        o_ref[...] += _dot_tn(a_ref[...], b_ref[...])

    return _pcall(
        body, name=name, grid=(nj, t_all // tt),
        in_specs=[pl.BlockSpec((tt, wa), lambda j, t: (t, j if a_col_blocks > 1 else 0)),
                  pl.BlockSpec((tt, wb), lambda j, t: (t, j if b_col_blocks > 1 else 0))],
        out_specs=pl.BlockSpec((None, wa, wb), lambda j, t: (j, 0, 0)),
        out_shape=jax.ShapeDtypeStruct((nj, wa, wb), F32),
        compiler_params=_params(("parallel", "arbitrary")),
    )(a, b)


def _mlp_dhn(du, w_up_t, h, dy, g2):
    t_all = h.shape[0]
    tm = min(256, t_all)

    def body(a_ref, w_ref, h_ref, dy_ref, g_ref, dh_ref, dhb_ref, dg_ref):
        @pl.when(pl.program_id(0) == 0)
        def _():
            dg_ref[...] = jnp.zeros_like(dg_ref)

        dhn = _dot(a_ref[...], w_ref[...])
        h = h_ref[...]
        r = lax.rsqrt(jnp.mean(h * h, axis=-1, keepdims=True) + EPS)
        hh = h * r
        dg_ref[...] += jnp.sum(dhn * hh, axis=0, keepdims=True)
        dz = dhn * g_ref[...]
        dh = dy_ref[...] + r * (dz - hh * jnp.mean(dz * hh, axis=-1, keepdims=True))
        dh_ref[...] = dh
        dhb_ref[...] = dh.astype(BF16)

    return _pcall(
        body, name="mlp_dhn", grid=(t_all // tm,),
        in_specs=[_rows(tm, D_FF), _const((D_FF, D_MODEL)), _rows(tm, D_MODEL), _rows(tm, D_MODEL),
                  _const((1, D_MODEL))],
        out_specs=[_rows(tm, D_MODEL), _rows(tm, D_MODEL), _const((1, D_MODEL))],
        out_shape=[jax.ShapeDtypeStruct((t_all, D_MODEL), F32), jax.ShapeDtypeStruct((t_all, D_MODEL), BF16),
                   jax.ShapeDtypeStruct((1, D_MODEL), F32)],
        compiler_params=_params(("arbitrary",)),
    )(du, w_up_t, h, dy, g2)


def _dmixed(dhb, wo_pad_t):
    t_all = dhb.shape[0]
    tm = min(512, t_all)
    half = A_HEADS * LANES

    def body(a_ref, w_ref, da_ref, db_ref):
        d = _dot(a_ref[...], w_ref[...])
        da_ref[...] = d[:, :half].astype(BF16)
        db_ref[...] = d[:, half:].astype(BF16)

    return _pcall(
        body, name="dmixed", grid=(t_all // tm,),
        in_specs=[_rows(tm, D_MODEL), _const((D_MODEL, MIXED_P))],
        out_specs=[_rows(tm, half), _rows(tm, half)],
        out_shape=[jax.ShapeDtypeStruct((t_all, half), BF16), jax.ShapeDtypeStruct((t_all, half), BF16)],
        compiler_params=_params(("parallel",)),
    )(dhb, wo_pad_t)


def _fox_bwd(qb, kb, vb, ob, dob, lse, nb, seq):
    t_all = qb.shape[0]
    tk = min(512, seq)
    nk = seq // tk

    def body(q_ref, k_ref, v_ref, o_ref, do_ref, lse_ref, dq_ref, dk_ref, dv_ref):
        kj = pl.program_id(2)

        @pl.when(kj == 0)
        def _():
            dq_ref[...] = jnp.zeros_like(dq_ref)

        k = k_ref[...]
        v = v_ref[...]

        def step(i, carry, masked):
            dk, dv = carry
            off = pl.multiple_of(i * tk, tk)
            q = q_ref[pl.ds(off, tk), :]
            do = do_ref[pl.ds(off, tk), :]
            o = o_ref[pl.ds(off, tk), :]
            lse_t = jnp.max(lse_ref[pl.ds(off, tk), :], axis=-1, keepdims=True)
            delta = jnp.sum(do.astype(F32) * o.astype(F32), axis=-1, keepdims=True)
            p = jnp.exp(_dot_nt(q, k) - lse_t)
            if masked:
                row = lax.broadcasted_iota(jnp.int32, (tk, tk), 0)
                col = lax.broadcasted_iota(jnp.int32, (tk, tk), 1)
                p = jnp.where(row >= col, p, 0.0)
            ds = (p * (_dot_nt(do, v) - delta)).astype(BF16)
            dv = dv + _dot_tn(p.astype(BF16), do)
            dk = dk + _dot_tn(ds, q)
            dq_ref[pl.ds(off, tk), :] += _dot(ds, k)
            return dk, dv

        zero = jnp.zeros((tk, LANES), F32)
        carry = step(kj, (zero, zero), True)
        dk, dv = lax.fori_loop(kj + 1, nk, lambda i, cr: step(i, cr, False), carry)
        dk_ref[...] = dk
        dv_ref[...] = dv

    full = pl.BlockSpec((seq, LANES), lambda b, h, j: (b, h))
    tile = pl.BlockSpec((tk, LANES), lambda b, h, j: (b * nk + j, h))
    shp = jax.ShapeDtypeStruct((t_all, B_HEADS * LANES), F32)
    return _pcall(
        body, name="fox_bwd", grid=(nb, B_HEADS, nk),
        in_specs=[full, tile, tile, full, full, full], out_specs=[full, tile, tile],
        out_shape=[shp, shp, shp],
        compiler_params=_params(("parallel", "parallel", "arbitrary")),
    )(qb, kb, vb, ob, dob, lse)


def _swa_bwd(qa, ka, va, oa, doa, lrow, sinks, bias, nb, seq):
    t_all = qa.shape[0]
    tq = min(512, seq)
    nq = seq // tq

    def body(sink_ref, q_ref, k_ref, v_ref, bias0_ref, bias_ref, o_ref, do_ref, l_ref,
             dq_ref, dk_ref, dv_ref, dsink_ref):
        qi = pl.program_id(2)
        sink = _sink_column(sink_ref, pl.program_id(1))

        @pl.when(qi == 0)
        def _():
            dk_ref[...] = jnp.zeros_like(dk_ref)
            dv_ref[...] = jnp.zeros_like(dv_ref)
            dsink_ref[...] = jnp.zeros_like(dsink_ref)

        for a in range(tq // WINDOW):
            t0 = qi * tq + a * WINDOW
            start = pl.multiple_of(jnp.maximum(t0 - WINDOW, 0), WINDOW)
            rows = slice(a * WINDOW, (a + 1) * WINDOW)
            win = pl.ds(start, 2 * WINDOW)
            q = _stack_heads(q_ref, rows)
            k = k_ref[win, :]
            v = v_ref[win, :]
            do = _stack_heads(do_ref, rows)
            lrow_t = jnp.max(_stack_heads(l_ref, rows), axis=-1, keepdims=True)
            p = jnp.exp(_dot_nt(q, k) + (bias0_ref if a == 0 else bias_ref)[...] - lrow_t)
            delta = jnp.sum(do.astype(F32) * _stack_heads(o_ref, rows).astype(F32), axis=-1, keepdims=True)
            ds = (p * (_dot_nt(do, v) - delta)).astype(BF16)
            dq = _dot(ds, k)
            dk_ref[win, :] += _dot_tn(ds, q)
            dv_ref[win, :] += _dot_tn(p.astype(BF16), do)
            sink_term = jnp.exp(sink - lrow_t) * delta
            for j in range(A_GROUP):
                part = slice(j * WINDOW, (j + 1) * WINDOW)
                dq_ref[rows, j * LANES:(j + 1) * LANES] = dq[part]
                dsink_ref[j:j + 1, :] -= jnp.broadcast_to(jnp.sum(sink_term[part], axis=0, keepdims=True), (1, LANES))

    smem, qspec, kspec, bias_first, bias_rest = _swa_specs(nq, tq, seq)
    return _pcall(
        body, name="swa_bwd", grid=(nb, A_KV_HEADS, nq),
        in_specs=[smem, qspec, kspec, kspec, bias_first, bias_rest, qspec, qspec, qspec],
        out_specs=[qspec, kspec, kspec, pl.BlockSpec((None, 8, LANES), lambda b, g, i: (b * A_KV_HEADS + g, 0, 0))],
        out_shape=[jax.ShapeDtypeStruct((t_all, A_HEADS * LANES), F32),
                   jax.ShapeDtypeStruct((t_all, A_KV_HEADS * LANES), F32),
                   jax.ShapeDtypeStruct((t_all, A_KV_HEADS * LANES), F32),
                   jax.ShapeDtypeStruct((nb * A_KV_HEADS, 8, LANES), F32)],
        compiler_params=_params(("parallel", "parallel", "arbitrary")),
    )(sinks, qa, ka, va, bias, bias, oa, doa, lrow)


def _dproj(pre, dqa, dka, dqb, dkb, dva, dvb, z, gain_row, seq):
    t_all = pre.shape[0]
    tm = min(256, seq)
    nt = t_all // tm
    tiles_per_seq = seq // tm
    triu = _tri(tm, True)
    sel = _dc_select()

    def body(pre_ref, dqa_ref, dka_ref, dqb_ref, dkb_ref, dva_ref, dvb_ref, z_ref, gain_ref, triu_ref, sel_ref,
             dproj_ref, small_ref, carry_ref):
        i = pl.program_id(0)

        @pl.when(i == 0)
        def _():
            small_ref[...] = jnp.zeros_like(small_ref)

        @pl.when(i % tiles_per_seq == 0)
        def _():
            carry_ref[...] = jnp.zeros_like(carry_ref)

        def norm_bwd(g, dhat):
            cols = slice(g * LANES, (g + 1) * LANES)
            p = pre_ref[:, cols].astype(F32)
            rr = lax.rsqrt(jnp.sum(p * p, axis=-1, keepdims=True) * (1.0 / HEAD_DIM) + EPS)
            n = p * rr
            dz = dhat * gain_ref[:, cols]
            dproj_ref[:, cols] = (rr * (dz - n * (jnp.sum(dz * n, axis=-1, keepdims=True) * (1.0 / HEAD_DIM)))
                                  ).astype(BF16)
            return jnp.sum(dhat * n, axis=0, keepdims=True)

        def group_sum(g0, d_ref, count, scale):
            acc = jnp.zeros((1, LANES), F32)
            for h in range(count):
                d = d_ref[:, h * LANES:(h + 1) * LANES]
                acc = acc + norm_bwd(g0 + h, d * scale if scale != 1.0 else d)
            return acc

        small_ref[0:1, :] += group_sum(G_QA, dqa_ref, A_HEADS, SCALE)
        small_ref[1:2, :] += group_sum(G_KA, dka_ref, A_KV_HEADS, 1.0)
        small_ref[2:3, :] += group_sum(G_QB, dqb_ref, B_HEADS, SCALE)
        small_ref[3:4, :] += group_sum(G_KB, dkb_ref, B_HEADS, 1.0)
        dproj_ref[:, G_VA * LANES:G_VB * LANES] = dva_ref[...].astype(BF16)
        dproj_ref[:, G_VB * LANES:G_F * LANES] = dvb_ref[...].astype(BF16)

        dc = jnp.zeros((tm, LANES), F32)
        for piece_q, piece_k in zip(_split3(dqb_ref[...]), _split3(dkb_ref[...])):
            dc = dc + _dot(jnp.concatenate([piece_q, piece_k], axis=1), sel_ref[...])
        dlf = _dot_exact(triu_ref[...], dc) + carry_ref[...]
        carry_ref[...] += jnp.sum(dc, axis=0, keepdims=True)
        dz = dlf / (1.0 + jnp.exp(z_ref[...]))
        small_ref[4:5, :] += jnp.sum(dz, axis=0, keepdims=True)
        dproj_ref[:, G_F * LANES:(G_F + 1) * LANES] = dz.astype(BF16)
        dproj_ref[:, (G_F + 1) * LANES:] = jnp.zeros((tm, LANES), BF16)

    def rev(n):
        return pl.BlockSpec((tm, n), lambda i: (nt - 1 - i, 0))

    return _pcall(
        body, name="dproj", grid=(nt,),
        in_specs=[rev(N_NORM_GROUPS * LANES), rev(A_HEADS * LANES), rev(A_KV_HEADS * LANES), rev(B_HEADS * LANES),
                  rev(B_HEADS * LANES), rev(A_KV_HEADS * LANES), rev(B_HEADS * LANES), rev(LANES),
                  _const((1, NP)), _const((tm, tm)), _const(sel.shape)],
        out_specs=[rev(NP), _const((8, LANES))],
        out_shape=[jax.ShapeDtypeStruct((t_all, NP), BF16), jax.ShapeDtypeStruct((8, LANES), F32)],
        scratch_shapes=[pltpu.VMEM((1, LANES), F32)],
        compiler_params=_params(("arbitrary",)),
    )(pre, dqa, dka, dqb, dkb, dva, dvb, z, gain_row, triu, sel)


def _dx(dproj, w_pad_t, x2, dh, g1):
    t_all = x2.shape[0]
    tm = min(256, t_all)

    def body(a_ref, w_ref, x_ref, dh_ref, g_ref, dx_ref, dg_ref):
        @pl.when(pl.program_id(0) == 0)
        def _():
            dg_ref[...] = jnp.zeros_like(dg_ref)

        dxn = _dot(a_ref[...], w_ref[...])
        x = x_ref[...]
        r = lax.rsqrt(jnp.mean(x * x, axis=-1, keepdims=True) + EPS)
        xh = x * r
        dg_ref[...] += jnp.sum(dxn * xh, axis=0, keepdims=True)
        dz = dxn * g_ref[...]
        dx_ref[...] = dh_ref[...] + r * (dz - xh * jnp.mean(dz * xh, axis=-1, keepdims=True))

    return _pcall(
        body, name="dx", grid=(t_all // tm,),
        in_specs=[_rows(tm, NP), _const((NP, D_MODEL)), _rows(tm, D_MODEL), _rows(tm, D_MODEL), _const((1, D_MODEL))],
        out_specs=[_rows(tm, D_MODEL), _const((1, D_MODEL))],
        out_shape=[jax.ShapeDtypeStruct((t_all, D_MODEL), F32), jax.ShapeDtypeStruct((1, D_MODEL), F32)],
        compiler_params=_params(("arbitrary",)),
    )(dproj, w_pad_t, x2, dh, g1)


def _dwin(dproj, xn):
    t_all = xn.shape[0]
    tt = min(512, t_all)
    half = NP // 2

    def body(a_ref, b_ref, o_ref):
        @pl.when(pl.program_id(1) == 0)
        def _():
            o_ref[...] = jnp.zeros_like(o_ref)

        o_ref[...] += _dot_tn(a_ref[...], b_ref[...])

    return _pcall(
        body, name="dwin", grid=(2, t_all // tt),
        in_specs=[pl.BlockSpec((tt, half), lambda j, t: (t, j)), pl.BlockSpec((tt, D_MODEL), lambda j, t: (t, 0))],
        out_specs=pl.BlockSpec((half, D_MODEL), lambda j, t: (j, 0)),
        out_shape=jax.ShapeDtypeStruct((NP, D_MODEL), F32),
        compiler_params=_params(("parallel", "arbitrary")),
    )(dproj, xn)


ANY = pl.BlockSpec(memory_space=pl.ANY)


def _place():
    return lax.axis_index("x"), lax.axis_index("y"), lax.axis_index("c")


def _allgather_halves(mine):
    m_per, n = mine.shape

    def body(x_ref, out_ref, send_sems, recv_sems, local_sem):
        x, y, c = _place()
        me, sibling = (x, y, c), (x, y, 1 - c)
        chips = [(1 - x, y), (x, 1 - y), (1 - x, 1 - y)]

        def rows(px, py, pc):
            return out_ref.at[pl.ds((4 * px + 2 * py + pc) * m_per, m_per), :]

        def copy(k, block, to, src=None):
            return pltpu.make_async_remote_copy(
                src_ref=rows(*block) if src is None else src, dst_ref=rows(*block),
                send_sem=send_sems.at[k], recv_sem=recv_sems.at[k], device_id=to, device_id_type=MESH)

        own = pltpu.make_async_copy(x_ref, rows(*me), local_sem)
        own.start()
        first = [copy(0, me, sibling, src=x_ref)]
        first += [copy(1 + j, me, (*chip, c), src=x_ref) for j, chip in enumerate(chips)]
        for cp in first:
            cp.start()
        passed = [copy(4 + j, (*chip, c), sibling) for j, chip in enumerate(chips)]
        for j, chip in enumerate(chips):
            copy(1 + j, (*chip, c), me).wait_recv()
            passed[j].start()
        copy(0, sibling, me).wait_recv()
        for j, chip in enumerate(chips):
            copy(4 + j, (*chip, 1 - c), me).wait_recv()
        for cp in first + passed:
            cp.wait_send()
        own.wait()

    return _pcall(
        body, name="allgather_weights",
        out_shape=jax.ShapeDtypeStruct((8 * m_per, n), mine.dtype),
        in_specs=[ANY], out_specs=ANY,
        scratch_shapes=[pltpu.SemaphoreType.DMA((7,)), pltpu.SemaphoreType.DMA((7,)), pltpu.SemaphoreType.DMA],
    )(mine)


def _rs_pair_exchange(g4):
    def body(g_ref, out_ref, send_sem, recv_sem):
        x, y, c = _place()
        cp = pltpu.make_async_remote_copy(
            src_ref=g_ref.at[:, 1 - c], dst_ref=out_ref, send_sem=send_sem, recv_sem=recv_sem,
            device_id=(x, y, 1 - c), device_id_type=MESH)
        cp.start()
        cp.wait()

    return _pcall(
        body, name="rs_pair_exchange",
        out_shape=jax.ShapeDtypeStruct((N_CHIPS, R_HALF, D_MODEL), F32),
        in_specs=[ANY], out_specs=ANY,
        scratch_shapes=[pltpu.SemaphoreType.DMA, pltpu.SemaphoreType.DMA],
    )(g4)


def _rs_pair_add(g4, got, c_idx):
    def body(c_ref, a_ref, b_ref, o_ref, ob_ref):
        pair = a_ref[...] + b_ref[...]
        o_ref[...] = pair
        ob_ref[...] = pair.astype(BF16)

    blk = pl.BlockSpec((None, R_HALF, D_MODEL), lambda s, c_ref: (s, 0, 0))
    return _pcall(
        body, name="rs_pair_add",
        grid_spec=pltpu.PrefetchScalarGridSpec(
            num_scalar_prefetch=1, grid=(N_CHIPS,),
            in_specs=[pl.BlockSpec((None, None, R_HALF, D_MODEL), lambda s, c_ref: (s, c_ref[0], 0, 0)), blk],
            out_specs=[blk, blk]),
        out_shape=[jax.ShapeDtypeStruct((N_CHIPS, R_HALF, D_MODEL), F32),
                   jax.ShapeDtypeStruct((N_CHIPS, R_HALF, D_MODEL), BF16)],
        compiler_params=_params(("parallel",)),
    )(c_idx, g4, got)


def _rs_chip_exchange(p4):
    def body(p_ref, out_ref, send_sems, recv_sems):
        x, y, c = _place()
        chips = [(1 - x, y), (x, 1 - y), (1 - x, 1 - y)]
        cps = [pltpu.make_async_remote_copy(
            src_ref=p_ref.at[2 * cx + cy], dst_ref=out_ref.at[j], send_sem=send_sems.at[j],
            recv_sem=recv_sems.at[j], device_id=(cx, cy, c), device_id_type=MESH)
            for j, (cx, cy) in enumerate(chips)]
        for cp in cps:
            cp.start()
        for cp in cps:
            cp.wait()

    return _pcall(
        body, name="rs_chip_exchange",
        out_shape=jax.ShapeDtypeStruct((3, R_HALF, D_MODEL), p4.dtype),
        in_specs=[ANY], out_specs=ANY,
        scratch_shapes=[pltpu.SemaphoreType.DMA((3,)), pltpu.SemaphoreType.DMA((3,))],
    )(p4)


def _rs_chip_add(p4, got, sc_idx):
    tr = R_HALF // 7

    def body(sc_ref, a_ref, b_ref, o_ref):
        o_ref[...] = ((a_ref[...] + b_ref[0].astype(F32)) + b_ref[1].astype(F32)) + b_ref[2].astype(F32)

    return _pcall(
        body, name="rs_chip_add",
        grid_spec=pltpu.PrefetchScalarGridSpec(
            num_scalar_prefetch=1, grid=(R_HALF // tr,),
            in_specs=[pl.BlockSpec((None, tr, D_MODEL), lambda i, sc_ref: (sc_ref[0], i, 0)),
                      pl.BlockSpec((3, tr, D_MODEL), lambda i, sc_ref: (0, i, 0))],
            out_specs=pl.BlockSpec((None, tr, D_MODEL), lambda i, sc_ref: (sc_ref[1], i, 0))),
        out_shape=jax.ShapeDtypeStruct((2, R_HALF, D_MODEL), F32),
        compiler_params=_params(("parallel",)),
    )(sc_idx, p4, got)


def _rs_pair_share(halves):
    def body(r_ref, out_ref, send_sem, recv_sem):
        x, y, c = _place()
        cp = pltpu.make_async_remote_copy(
            src_ref=r_ref.at[c], dst_ref=out_ref.at[c], send_sem=send_sem, recv_sem=recv_sem,
            device_id=(x, y, 1 - c), device_id_type=MESH)
        cp.start()
        cp.wait()

    return _pcall(
        body, name="rs_pair_share",
        out_shape=jax.ShapeDtypeStruct((2, R_HALF, D_MODEL), F32),
        in_specs=[ANY], out_specs=ANY, input_output_aliases={0: 0},
        scratch_shapes=[pltpu.SemaphoreType.DMA, pltpu.SemaphoreType.DMA],
    )(halves)


def _adam(w, g, m, v):
    m2 = ADAM_B1 * m + (1.0 - ADAM_B1) * g
    v2 = ADAM_B2 * v + (1.0 - ADAM_B2) * (g * g)
    m_hat = m2 / (1.0 - ADAM_B1 ** ADAM_STEP)
    v_hat = v2 / (1.0 - ADAM_B2 ** ADAM_STEP)
    return -ADAM_LR * (m_hat / (jnp.sqrt(v_hat) + ADAM_EPS) + ADAM_WD * w), m2, v2


def _small_allreduce_adamw(part, w, m, v):
    def body(p_ref, w_ref, m_ref, v_ref, g_ref, d_ref, m2_ref, v2_ref, buf, send_sems, recv_sems):
        x, y, c = _place()
        me = 4 * x + 2 * y + c
        cps = []
        for k in range(1, 8):
            peer = (1 - x if k & 4 else x, 1 - y if k & 2 else y, 1 - c if k & 1 else c)
            cps.append(pltpu.make_async_remote_copy(
                src_ref=p_ref, dst_ref=buf.at[me], send_sem=send_sems.at[k - 1], recv_sem=recv_sems.at[k - 1],
                device_id=peer, device_id_type=MESH))
        for cp in cps:
            cp.start()
        buf[me] = p_ref[...]
        for cp in cps:
            cp.wait()
        g = buf[0]
        for k in range(1, 8):
            g = g + buf[k]
        g_ref[...] = g
        d_ref[...], m2_ref[...], v2_ref[...] = _adam(w_ref[...], g, m_ref[...], v_ref[...])

    vm = pl.BlockSpec(memory_space=pltpu.VMEM)
    shp = jax.ShapeDtypeStruct((SMALL_ROWS, LANES), F32)
    return _pcall(
        body, name="small_allreduce_adamw",
        out_shape=[shp, shp, shp, shp], in_specs=[vm, vm, vm, vm], out_specs=[vm, vm, vm, vm],
        scratch_shapes=[pltpu.VMEM((8, SMALL_ROWS, LANES), F32), pltpu.SemaphoreType.DMA((7,)),
                        pltpu.SemaphoreType.DMA((7,))],
    )(part, w, m, v)


def _adamw(w, g, m, v, name):
    rows, cols = w.shape
    tr = min(256, rows)

    def body(w_ref, g_ref, m_ref, v_ref, d_ref, m2_ref, v2_ref):
        d_ref[...], m2_ref[...], v2_ref[...] = _adam(w_ref[...], g_ref[...], m_ref[...], v_ref[...])

    spec = _rows(tr, cols)
    shp = jax.ShapeDtypeStruct((rows, cols), F32)
    return _pcall(
        body, name=name, grid=(rows // tr,), in_specs=[spec] * 4, out_specs=[spec] * 3, out_shape=[shp] * 3,
        compiler_params=_params(("parallel",)),
    )(w, g, m, v)


def _pad_lanes(v):
    return jnp.pad(v, (0, LANES - v.shape[0]))


def _pad_head_rows(w_t, heads):
    n = w_t.shape[1]
    return jnp.pad(w_t.reshape(heads, HEAD_DIM, n), ((0, 0), (0, LANES - HEAD_DIM), (0, 0))).reshape(heads * LANES, n)


def _unpad_head_rows(w_t, heads):
    n = w_t.shape[1]
    return w_t.reshape(heads, LANES, n)[:, :HEAD_DIM].reshape(heads * HEAD_DIM, n)


def _in_rows_pad(w_in_t):
    qa, ka, va, qb, kb, vb, f = jnp.split(w_in_t, [512, 640, 768, 1280, 1792, 2304], axis=0)
    f = jnp.pad(f, ((0, 2 * LANES - B_HEADS), (0, 0)))
    return jnp.concatenate([_pad_head_rows(qa, 8), _pad_head_rows(ka, 2), _pad_head_rows(qb, 8),
                            _pad_head_rows(kb, 8), _pad_head_rows(va, 2), _pad_head_rows(vb, 8), f], axis=0)


def _in_rows_unpad(d):
    qa = _unpad_head_rows(d[G_QA * LANES:G_KA * LANES], 8)
    ka = _unpad_head_rows(d[G_KA * LANES:G_QB * LANES], 2)
    qb = _unpad_head_rows(d[G_QB * LANES:G_KB * LANES], 8)
    kb = _unpad_head_rows(d[G_KB * LANES:G_VA * LANES], 8)
    va = _unpad_head_rows(d[G_VA * LANES:G_VB * LANES], 2)
    vb = _unpad_head_rows(d[G_VB * LANES:G_F * LANES], 8)
    f = d[G_F * LANES:G_F * LANES + B_HEADS]
    return jnp.concatenate([qa, ka, va, qb, kb, vb, f], axis=0)


def _pack_small(g1, bf, qa, ka, sk, qb, kb, g2, loss_row):
    rows = [g1.reshape(8, LANES), g2.reshape(8, LANES)]
    rows += [_pad_lanes(t)[None] for t in (qa, ka, qb, kb, bf, sk)]
    rows += [loss_row, jnp.zeros((1, LANES), F32)]
    return jnp.concatenate(rows, axis=0)


def _unpack_small(p):
    return (p[0:8].reshape(D_MODEL), p[20, :B_HEADS], p[16, :HEAD_DIM], p[17, :HEAD_DIM], p[21, :A_HEADS],
            p[18, :HEAD_DIM], p[19, :HEAD_DIM], p[8:16].reshape(D_MODEL))


def kernel(x, attn_norm_g, w_in, b_forget, q_norm_a, k_norm_a, sink_logits, q_norm_b, k_norm_b, w_out, mlp_norm_g, w_up, w_down, loss_target, m_attn_norm_g, m_w_in, m_b_forget, m_q_norm_a, m_k_norm_a, m_sink_logits, m_q_norm_b, m_k_norm_b, m_w_out, m_mlp_norm_g, m_w_up, m_w_down, v_attn_norm_g, v_w_in, v_b_forget, v_q_norm_a, v_k_norm_a, v_sink_logits, v_q_norm_b, v_k_norm_b, v_w_out, v_mlp_norm_g, v_w_up, v_w_down):
    nb, seq, _ = x.shape
    t_all = nb * seq
    c_idx = lax.axis_index("c")
    s_idx = 2 * lax.axis_index("x") + lax.axis_index("y")

    packed = jnp.concatenate([jnp.pad(w_in.T, ((0, IN_SHARD_P - IN_SHARD), (0, 0))), w_out, w_up, w_down], axis=0)
    mine = lax.dynamic_slice_in_dim(packed.astype(BF16).reshape(2, R_HALF, D_MODEL), c_idx, 1, axis=0)[0]
    gathered = _allgather_halves(mine).reshape(N_CHIPS, R_ALL, D_MODEL)
    w_in_t = gathered[:, :IN_SHARD].reshape(IN_WIDTH, D_MODEL)
    w_pad_t = _in_rows_pad(w_in_t)
    w_pad = w_pad_t.T
    wo_pad = _pad_head_rows(gathered[:, R_OUT:R_UP].reshape(D_MODEL, D_MODEL), A_HEADS + B_HEADS)
    wo_pad_t = wo_pad.T
    w_up_blocks = gathered[:, R_UP:R_DOWN]
    w_up_t = jnp.swapaxes(w_up_blocks, 1, 2).reshape(D_FF, D_MODEL)
    w_down_f = gathered[:, R_DOWN:].reshape(D_FF, D_MODEL)
    w_down_t = w_down_f.T

    ones = jnp.ones((LANES,), F32)
    gain_row = jnp.concatenate(
        [jnp.tile(_pad_lanes(q_norm_a), 8), jnp.tile(_pad_lanes(k_norm_a), 2), jnp.tile(_pad_lanes(q_norm_b), 8),
         jnp.tile(_pad_lanes(k_norm_b), 8), jnp.tile(ones, N_GROUPS - N_NORM_GROUPS)])[None]
    b_row = _pad_lanes(b_forget)[None]
    g1 = attn_norm_g[None]
    g2 = mlp_norm_g[None]
    slopes = jnp.exp2(-(8.0 / A_HEADS) * (jnp.arange(A_HEADS, dtype=F32) + 1.0))

    x2 = x.reshape(t_all, D_MODEL)
    tgt = loss_target.reshape(t_all, D_MODEL)

    xn, pre, qa, ka, va, qb, kb, vb, z = _inproj(x2, g1, w_pad, gain_row, b_row, seq)
    swa_bias = _swa_bias(slopes)
    oa, la = _swa_fwd(qa, ka, va, sink_logits, swa_bias, nb, seq)
    ob, lse = _fox_fwd(qb, kb, vb, nb, seq)
    h, hn = _outproj(x2, oa, ob, wo_pad, g2)
    ru, hid = _mlp_up(hn, w_up_blocks)
    dy, dyb, loss_acc = _mlp_down(hid, w_down_f, h, tgt)

    du = _mlp_dhid(dyb, w_down_t, ru)
    d_w_down = _wgrad(hid, dyb, "dw_down", D_FF // D_MODEL, 1)
    d_w_up = _wgrad(hn, du, "dw_up", 1, D_FF // D_MODEL)
    dh, dhb, d_g2 = _mlp_dhn(du, w_up_t, h, dy, g2)
    doa, dob = _dmixed(dhb, wo_pad_t)
    d_wo_a = _wgrad(oa, dhb, "dw_out_a", 1, 1)[0]
    d_wo_b = _wgrad(ob, dhb, "dw_out_b", 1, 1)[0]
    dqb, dkb, dvb = _fox_bwd(qb, kb, vb, ob, dob, lse, nb, seq)
    dqa, dka, dva, dsink = _swa_bwd(qa, ka, va, oa, doa, la, sink_logits, swa_bias, nb, seq)
    dproj, small = _dproj(pre, dqa, dka, dqb, dkb, dva, dvb, z, gain_row, seq)
    grad_x, d_g1 = _dx(dproj, w_pad_t, x2, dh, g1)
    d_w_in_t = _dwin(dproj, xn)

    d_w_out = jnp.concatenate([_unpad_head_rows(d_wo_a, 8), _unpad_head_rows(d_wo_b, 8)], axis=0)
    g_pack = jnp.concatenate([
        jnp.pad(_in_rows_unpad(d_w_in_t).reshape(N_CHIPS, IN_SHARD, D_MODEL),
                ((0, 0), (0, IN_SHARD_P - IN_SHARD), (0, 0))),
        d_w_out.reshape(N_CHIPS, D_MODEL // N_CHIPS, D_MODEL), d_w_up, d_w_down], axis=1)
    g4 = g_pack.reshape(N_CHIPS, 2, R_HALF, D_MODEL)
    pair, pair_bf = _rs_pair_add(g4, _rs_pair_exchange(g4), c_idx.reshape(1).astype(jnp.int32))
    halves = _rs_chip_add(pair, _rs_chip_exchange(pair_bf), jnp.stack([s_idx, c_idx]).astype(jnp.int32))
    red = _rs_pair_share(halves).reshape(R_ALL, D_MODEL)
    g_w_in = red[:IN_SHARD].T
    g_w_out = red[R_OUT:R_UP]
    g_w_up = red[R_UP:R_DOWN]
    g_w_down = red[R_DOWN:]

    loss_row = loss_acc[0:1] * (0.5 / D_MODEL)
    d_sink = dsink[:, :A_GROUP, 0].reshape(nb, A_HEADS).sum(axis=0)
    part = _pack_small(d_g1[0], small[4, :B_HEADS], small[0, :HEAD_DIM], small[1, :HEAD_DIM], d_sink,
                       small[2, :HEAD_DIM], small[3, :HEAD_DIM], d_g2[0], loss_row)
    zero_row = jnp.zeros((1, LANES), F32)
    smalls = lambda t: _pack_small(*t, zero_row)
    w_small = smalls((attn_norm_g, b_forget, q_norm_a, k_norm_a, sink_logits, q_norm_b, k_norm_b, mlp_norm_g))
    m_small = smalls((m_attn_norm_g, m_b_forget, m_q_norm_a, m_k_norm_a, m_sink_logits, m_q_norm_b, m_k_norm_b,
                      m_mlp_norm_g))
    v_small = smalls((v_attn_norm_g, v_b_forget, v_q_norm_a, v_k_norm_a, v_sink_logits, v_q_norm_b, v_k_norm_b,
                      v_mlp_norm_g))
    g_s, d_s, m_s, v_s = _small_allreduce_adamw(part, w_small, m_small, v_small)
    loss = g_s[ROW_LOSS, 0]

    big = {}
    for name, w, g, m, v in (("adamw_w_in", w_in, g_w_in, m_w_in, v_w_in),
                             ("adamw_w_out", w_out, g_w_out, m_w_out, v_w_out),
                             ("adamw_w_up", w_up, g_w_up, m_w_up, v_w_up),
                             ("adamw_w_down", w_down, g_w_down, m_w_down, v_w_down)):
        big[name] = (g,) + tuple(_adamw(w, g, m, v, name))

    def assemble(k, small_pack):
        s = _unpack_small(small_pack)
        return (s[0], big["adamw_w_in"][k], s[1], s[2], s[3], s[4], s[5], s[6], big["adamw_w_out"][k], s[7],
                big["adamw_w_up"][k], big["adamw_w_down"][k])

    return (loss, grad_x.reshape(nb, seq, D_MODEL), *assemble(0, g_s), *assemble(1, d_s), *assemble(2, m_s),
            *assemble(3, v_s))
```

```python
import functools

import numpy as np
import jax
import jax.numpy as jnp
from jax import lax
from jax.experimental import pallas as pl
from jax.experimental.pallas import tpu as pltpu

F32 = jnp.float32
BF16 = jnp.bfloat16

D_MODEL = 1024
HEAD_DIM = 64
LANES = 128
A_HEADS = 8
A_KV_HEADS = 2
A_GROUP = A_HEADS // A_KV_HEADS
B_HEADS = 8
WINDOW = 128
D_FF = 4096
IN_WIDTH = 2312
EPS = 1e-6
SCALE = 0.125
LOG2E = 1.4426950408889634
LN2 = 0.6931471805599453
CHUNK = 32
NEG = -1e30

G_QA, G_KA, G_QB, G_KB, G_VA, G_VB, G_F = 0, 8, 10, 18, 26, 28, 36
N_NORM_GROUPS = 26
N_GROUPS = 38
NP = N_GROUPS * LANES
MIXED_P = (A_HEADS + B_HEADS) * LANES

N_CHIPS = 4
IN_SHARD = IN_WIDTH // N_CHIPS
IN_SHARD_P = 608
R_OUT = IN_SHARD_P
R_UP = R_OUT + D_MODEL // N_CHIPS
R_DOWN = R_UP + D_MODEL
R_ALL = R_DOWN + D_FF // N_CHIPS
R_HALF = R_ALL // 2

SMALL_ROWS = 24
ROW_LOSS = 22

ADAM_LR = 0.001
ADAM_B1 = 0.9
ADAM_B2 = 0.999
ADAM_EPS = 1e-08
ADAM_WD = 0.01
ADAM_STEP = 10

VMEM_LIMIT = 52 * 1024 * 1024
MESH = pl.DeviceIdType.MESH


def _pcall(body, **kw):
    return pl.pallas_call(body, **kw)


def _params(sem=None):
    return pltpu.CompilerParams(dimension_semantics=sem, vmem_limit_bytes=VMEM_LIMIT)


def _dot(a, b):
    return jnp.dot(a, b, preferred_element_type=F32)


def _dot_nt(a, b):
    return lax.dot_general(a, b, (((1,), (1,)), ((), ())), preferred_element_type=F32)


def _dot_tn(a, b):
    return lax.dot_general(a, b, (((0,), (0,)), ((), ())), preferred_element_type=F32)


def _split3(x):
    hi = x.astype(BF16)
    r1 = x - hi.astype(F32)
    mid = r1.astype(BF16)
    lo = (r1 - mid.astype(F32)).astype(BF16)
    return hi, mid, lo


def _dot_exact(mat, x):
    hi, mid, lo = _split3(x)
    return _dot(mat, lo) + _dot(mat, mid) + _dot(mat, hi)


def _const(shape):
    zeros = (0,) * len(shape)
    return pl.BlockSpec(shape, lambda *_: zeros)


def _rows(tm, n):
    return pl.BlockSpec((tm, n), lambda i: (i, 0))


def _aug_select():
    e = np.zeros((3 * LANES, 2 * B_HEADS * LANES), np.float32)
    for j in range(3):
        for h in range(B_HEADS):
            e[j * LANES + h, h * LANES + HEAD_DIM + j] = 1.0
            e[j * LANES + h, (B_HEADS + h) * LANES + HEAD_DIM + 3 + j] = -1.0
    return jnp.asarray(e, BF16)


def _dc_select():
    e = np.zeros((2 * B_HEADS * LANES, LANES), np.float32)
    for h in range(B_HEADS):
        e[h * LANES + HEAD_DIM, h] = 1.0
        e[(B_HEADS + h) * LANES + HEAD_DIM + 3, h] = -1.0
    return jnp.asarray(e, BF16)


def _tri(n, upper):
    t = np.tril(np.ones((n, n), np.float32))
    return jnp.asarray(t.T if upper else t, BF16)


def _inproj(x2, g1, w_pad, gain_row, b_row, seq):
    t_all = x2.shape[0]
    tm = min(256, seq)
    tiles_per_seq = seq // tm
    tri = _tri(tm, False)
    esel = _aug_select()

    def body(x_ref, g_ref, w_ref, gain_ref, b_ref, tri_ref, e_ref,
             xn_ref, pre_ref, qa_ref, ka_ref, va_ref, qb_ref, kb_ref, vb_ref, z_ref, carry_ref):
        i = pl.program_id(0)

        @pl.when(i % tiles_per_seq == 0)
        def _():
            carry_ref[...] = jnp.zeros_like(carry_ref)

        x = x_ref[...]
        r = lax.rsqrt(jnp.mean(x * x, axis=-1, keepdims=True) + EPS)
        xn = (x * r * g_ref[...]).astype(BF16)
        xn_ref[...] = xn
        proj = _dot(xn, w_ref[...])
        pre_ref[...] = proj[:, :N_NORM_GROUPS * LANES].astype(BF16)
        lane = lax.broadcasted_iota(jnp.int32, (tm, LANES), 1)

        z = proj[:, G_F * LANES:(G_F + 1) * LANES] + b_ref[...]
        z_ref[...] = z
        lf = jnp.minimum(z, 0.0) - jnp.log(1.0 + jnp.exp(-jnp.abs(z)))
        lf = jnp.where(lane < B_HEADS, lf, 0.0)
        c = _dot_exact(tri_ref[...], lf) + carry_ref[...]
        carry_ref[...] += jnp.sum(lf, axis=0, keepdims=True)
        aug = _dot(jnp.concatenate(_split3(c * LOG2E), axis=1), e_ref[...])

        def hnorm(g):
            p = proj[:, g * LANES:(g + 1) * LANES]
            rr = lax.rsqrt(jnp.sum(p * p, axis=-1, keepdims=True) * (1.0 / HEAD_DIM) + EPS)
            return p * rr * gain_ref[:, g * LANES:(g + 1) * LANES]

        ones_q = jnp.where((lane >= HEAD_DIM + 3) & (lane < HEAD_DIM + 6), 1.0, 0.0)
        ones_k = jnp.where((lane >= HEAD_DIM) & (lane < HEAD_DIM + 3), 1.0, 0.0)
        for h in range(A_HEADS):
            qa_ref[:, h * LANES:(h + 1) * LANES] = (hnorm(G_QA + h) * SCALE).astype(BF16)
        for h in range(A_KV_HEADS):
            ka_ref[:, h * LANES:(h + 1) * LANES] = hnorm(G_KA + h).astype(BF16)
        for h in range(B_HEADS):
            qb_ref[:, h * LANES:(h + 1) * LANES] = (
                hnorm(G_QB + h) * (SCALE * LOG2E) + aug[:, h * LANES:(h + 1) * LANES] + ones_q).astype(BF16)
            kb_ref[:, h * LANES:(h + 1) * LANES] = (
                hnorm(G_KB + h) + aug[:, (B_HEADS + h) * LANES:(B_HEADS + h + 1) * LANES] + ones_k).astype(BF16)
        va_ref[...] = proj[:, G_VA * LANES:G_VB * LANES].astype(BF16)
        one_v = jnp.where(lane == HEAD_DIM, 1.0, 0.0)
        for h in range(B_HEADS):
            cols = slice((G_VB + h) * LANES, (G_VB + h + 1) * LANES)
            vb_ref[:, h * LANES:(h + 1) * LANES] = (proj[:, cols] + one_v).astype(BF16)

    widths = [(D_MODEL, BF16), (N_NORM_GROUPS * LANES, BF16), (A_HEADS * LANES, BF16), (A_KV_HEADS * LANES, BF16),
              (A_KV_HEADS * LANES, BF16), (B_HEADS * LANES, BF16), (B_HEADS * LANES, BF16), (B_HEADS * LANES, BF16),
              (LANES, F32)]
    return _pcall(
        body, name="inproj", grid=(t_all // tm,),
        in_specs=[_rows(tm, D_MODEL), _const((1, D_MODEL)), _const((D_MODEL, NP)), _const((1, NP)),
                  _const((1, LANES)), _const((tm, tm)), _const(esel.shape)],
        out_specs=[_rows(tm, w) for w, _ in widths],
        out_shape=[jax.ShapeDtypeStruct((t_all, w), dt) for w, dt in widths],
        scratch_shapes=[pltpu.VMEM((1, LANES), F32)],
        compiler_params=_params(("arbitrary",)),
    )(x2, g1, w_pad, gain_row, b_row, tri, esel)


def _fox_fwd(qb, kb, vb, nb, seq):
    t_all = qb.shape[0]
    tq = min(512, seq)
    nq = seq // tq

    def body(q_ref, k_ref, v_ref, o_ref, lse_ref, s_ref, p_ref, m_ref, alpha_ref, acc_ref):
        qi = pl.program_id(2)
        q = q_ref[...]
        hq = tq // 2
        m_ref[...] = jnp.full((tq, LANES), NEG, F32)
        acc_ref[...] = jnp.zeros((tq, LANES), F32)

        def step(j, masked):
            off = pl.multiple_of(j * tq, tq)
            k = k_ref[pl.ds(off, tq), :]
            v = v_ref[pl.ds(off, tq), :]
            for hf in range(2):
                s_ref[hf] = _dot_nt(q[hf * hq:(hf + 1) * hq], k)
            for hf in range(2):
                for r in range(0, hq, CHUNK):
                    rows = slice(r, r + CHUNK)
                    grows = slice(hf * hq + r, hf * hq + r + CHUNK)
                    tiles = []
                    for jt in range(tq // LANES):
                        sc = s_ref[hf, rows, jt * LANES:(jt + 1) * LANES]
                        if masked:
                            row = hf * hq + r + lax.broadcasted_iota(jnp.int32, (CHUNK, LANES), 0)
                            col = jt * LANES + lax.broadcasted_iota(jnp.int32, (CHUNK, LANES), 1)
                            sc = jnp.where(row >= col, sc, NEG)
                        tiles.append(sc)
                    m_prev = m_ref[grows, :]
                    m_cur = functools.reduce(jnp.maximum, tiles)
                    m_new = jnp.maximum(m_prev, jnp.max(m_cur, axis=-1, keepdims=True))
                    m_ref[grows, :] = m_new
                    alpha_ref[grows, :] = jnp.exp2(m_prev - m_new)
                    for jt, sc in enumerate(tiles):
                        p_ref[hf, rows, jt * LANES:(jt + 1) * LANES] = jnp.exp2(sc - m_new).astype(BF16)
                hrows = slice(hf * hq, (hf + 1) * hq)
                acc_ref[hrows, :] = alpha_ref[hrows, :] * acc_ref[hrows, :] + _dot(p_ref[hf], v)

        def unmasked(j, carry):
            step(j, False)
            return carry

        lax.fori_loop(0, qi, unmasked, 0)
        step(qi, True)
        acc = acc_ref[...]
        lane = lax.broadcasted_iota(jnp.int32, (tq, LANES), 1)
        l = jnp.sum(jnp.where(lane == HEAD_DIM, acc, 0.0), axis=-1, keepdims=True)
        o_ref[...] = (acc / l).astype(BF16)
        lse_ref[...] = m_ref[...] + jnp.log2(l)

    qspec = pl.BlockSpec((tq, LANES), lambda b, h, i: (b * nq + i, h))
    kspec = pl.BlockSpec((seq, LANES), lambda b, h, i: (b, h))
    return _pcall(
        body, name="fox_fwd", grid=(nb, B_HEADS, nq),
        in_specs=[qspec, kspec, kspec], out_specs=[qspec, qspec],
        out_shape=[jax.ShapeDtypeStruct((t_all, B_HEADS * LANES), BF16),
                   jax.ShapeDtypeStruct((t_all, B_HEADS * LANES), F32)],
        scratch_shapes=[pltpu.VMEM((2, tq // 2, tq), F32), pltpu.VMEM((2, tq // 2, tq), BF16),
                        pltpu.VMEM((tq, LANES), F32),
                        pltpu.VMEM((tq, LANES), F32), pltpu.VMEM((tq, LANES), F32)],
        compiler_params=_params(("parallel", "parallel", "arbitrary")),
    )(qb, kb, vb)


def _swa_bias(slopes):
    row = jnp.arange(A_GROUP * WINDOW, dtype=jnp.int32)[:, None] % WINDOW
    col = jnp.arange(2 * WINDOW, dtype=jnp.int32)[None, :]
    slope_rows = jnp.repeat(slopes.reshape(A_KV_HEADS, A_GROUP), WINDOW, axis=1)[:, :, None]
    out = []
    for t_rel in (0, WINDOW):
        dist = t_rel + row - col
        valid = (dist >= 0) & (dist < WINDOW)
        out.append(jnp.where(valid[None], -slope_rows * dist.astype(F32)[None], NEG))
    return jnp.stack(out)


def _stack_heads(ref, rows):
    return jnp.concatenate([ref[rows, j * LANES:(j + 1) * LANES] for j in range(A_GROUP)], axis=0)


def _sink_column(sink_ref, g):
    return jnp.concatenate([jnp.full((WINDOW, 1), sink_ref[g * A_GROUP + j], F32) for j in range(A_GROUP)], axis=0)


def _swa_specs(nq, tq, seq):
    smem = pl.BlockSpec(memory_space=pltpu.SMEM)
    qspec = pl.BlockSpec((tq, A_GROUP * LANES), lambda b, g, i: (b * nq + i, g))
    kspec = pl.BlockSpec((seq, LANES), lambda b, g, i: (b, g))
    bias_first = pl.BlockSpec((None, None, A_GROUP * WINDOW, 2 * WINDOW),
                              lambda b, g, i: (jnp.minimum(i, 1), g, 0, 0))
    bias_rest = pl.BlockSpec((None, None, A_GROUP * WINDOW, 2 * WINDOW), lambda b, g, i: (1, g, 0, 0))
    return smem, qspec, kspec, bias_first, bias_rest


def _swa_fwd(qa, ka, va, sinks, bias, nb, seq):
    t_all = qa.shape[0]
    tq = min(512, seq)
    nq = seq // tq

    def body(sink_ref, q_ref, k_ref, v_ref, bias0_ref, bias_ref, o_ref, l_ref):
        qi = pl.program_id(2)
        sink = _sink_column(sink_ref, pl.program_id(1))
        for a in range(tq // WINDOW):
            t0 = qi * tq + a * WINDOW
            start = pl.multiple_of(jnp.maximum(t0 - WINDOW, 0), WINDOW)
            rows = slice(a * WINDOW, (a + 1) * WINDOW)
            k = k_ref[pl.ds(start, 2 * WINDOW), :]
            v = v_ref[pl.ds(start, 2 * WINDOW), :]
            s = _dot_nt(_stack_heads(q_ref, rows), k) + (bias0_ref if a == 0 else bias_ref)[...]
            m = jnp.maximum(jnp.max(s, axis=-1, keepdims=True), sink)
            p = jnp.exp(s - m)
            den = jnp.sum(p, axis=-1, keepdims=True) + jnp.exp(sink - m)
            o = _dot((p / den).astype(BF16), v).astype(BF16)
            lrow = jnp.broadcast_to(m + jnp.log(den), (A_GROUP * WINDOW, LANES))
            for j in range(A_GROUP):
                o_ref[rows, j * LANES:(j + 1) * LANES] = o[j * WINDOW:(j + 1) * WINDOW]
                l_ref[rows, j * LANES:(j + 1) * LANES] = lrow[j * WINDOW:(j + 1) * WINDOW]

    smem, qspec, kspec, bias_first, bias_rest = _swa_specs(nq, tq, seq)
    return _pcall(
        body, name="swa_fwd", grid=(nb, A_KV_HEADS, nq),
        in_specs=[smem, qspec, kspec, kspec, bias_first, bias_rest], out_specs=[qspec, qspec],
        out_shape=[jax.ShapeDtypeStruct((t_all, A_HEADS * LANES), BF16),
                   jax.ShapeDtypeStruct((t_all, A_HEADS * LANES), F32)],
        compiler_params=_params(("parallel", "parallel", "arbitrary")),
    )(sinks, qa, ka, va, bias, bias)


def _outproj(x2, oa, ob, wo_pad, g2):
    t_all = x2.shape[0]
    tm = min(512, t_all)
    half = A_HEADS * LANES

    def body(x_ref, oa_ref, ob_ref, w_ref, g_ref, h_ref, hn_ref):
        h = x_ref[...] + _dot(oa_ref[...], w_ref[:half, :]) + _dot(ob_ref[...], w_ref[half:, :])
        h_ref[...] = h
        r = lax.rsqrt(jnp.mean(h * h, axis=-1, keepdims=True) + EPS)
        hn_ref[...] = (h * r * g_ref[...]).astype(BF16)

    return _pcall(
        body, name="outproj", grid=(t_all // tm,),
        in_specs=[_rows(tm, D_MODEL), _rows(tm, half), _rows(tm, half), _const((MIXED_P, D_MODEL)),
                  _const((1, D_MODEL))],
        out_specs=[_rows(tm, D_MODEL), _rows(tm, D_MODEL)],
        out_shape=[jax.ShapeDtypeStruct((t_all, D_MODEL), F32), jax.ShapeDtypeStruct((t_all, D_MODEL), BF16)],
        compiler_params=_params(("parallel",)),
    )(x2, oa, ob, wo_pad, g2)


def _mlp_up(hn, w_up_blocks):
    t_all = hn.shape[0]
    tm = min(512, t_all)
    nj = D_FF // D_MODEL

    def body(a_ref, w_ref, ru_ref, hid_ref):
        ru = jnp.maximum(_dot(a_ref[...], w_ref[...]), 0.0)
        ru_ref[...] = ru.astype(BF16)
        hid_ref[...] = (ru * ru).astype(BF16)

    ospec = pl.BlockSpec((tm, D_MODEL), lambda j, i: (i, j))
    return _pcall(
        body, name="mlp_up", grid=(nj, t_all // tm),
        in_specs=[pl.BlockSpec((tm, D_MODEL), lambda j, i: (i, 0)),
                  pl.BlockSpec((None, D_MODEL, D_MODEL), lambda j, i: (j, 0, 0))],
        out_specs=[ospec, ospec],
        out_shape=[jax.ShapeDtypeStruct((t_all, D_FF), BF16), jax.ShapeDtypeStruct((t_all, D_FF), BF16)],
        compiler_params=_params(("parallel", "parallel")),
    )(hn, w_up_blocks)


def _mlp_down(hid, w_down, h, tgt):
    t_all = h.shape[0]
    tm = min(256, t_all)

    def body(a_ref, w_ref, h_ref, t_ref, dy_ref, dyb_ref, loss_ref):
        @pl.when(pl.program_id(0) == 0)
        def _():
            loss_ref[...] = jnp.zeros_like(loss_ref)

        y = h_ref[...] + _dot(a_ref[...], w_ref[...])
        err = y - t_ref[...]
        loss_ref[...] += jnp.sum(err * err)
        dy = err * (1.0 / D_MODEL)
        dy_ref[...] = dy
        dyb_ref[...] = dy.astype(BF16)

    return _pcall(
        body, name="mlp_down", grid=(t_all // tm,),
        in_specs=[_rows(tm, D_FF), _const((D_FF, D_MODEL)), _rows(tm, D_MODEL), _rows(tm, D_MODEL)],
        out_specs=[_rows(tm, D_MODEL), _rows(tm, D_MODEL), _const((8, LANES))],
        out_shape=[jax.ShapeDtypeStruct((t_all, D_MODEL), F32), jax.ShapeDtypeStruct((t_all, D_MODEL), BF16),
                   jax.ShapeDtypeStruct((8, LANES), F32)],
        compiler_params=_params(("arbitrary",)),
    )(hid, w_down, h, tgt)


def _mlp_dhid(dyb, w_down_t, ru):
    t_all = dyb.shape[0]
    tm = min(512, t_all)
    nj = D_FF // D_MODEL

    def body(a_ref, w_ref, ru_ref, du_ref):
        du_ref[...] = (_dot(a_ref[...], w_ref[...]) * (2.0 * ru_ref[...].astype(F32))).astype(BF16)

    ospec = pl.BlockSpec((tm, D_MODEL), lambda j, i: (i, j))
    return _pcall(
        body, name="mlp_dhid", grid=(nj, t_all // tm),
        in_specs=[pl.BlockSpec((tm, D_MODEL), lambda j, i: (i, 0)),
                  pl.BlockSpec((D_MODEL, D_MODEL), lambda j, i: (0, j)), ospec],
        out_specs=ospec,
        out_shape=jax.ShapeDtypeStruct((t_all, D_FF), BF16),
        compiler_params=_params(("parallel", "parallel")),
    )(dyb, w_down_t, ru)


def _wgrad(a, b, name, a_col_blocks, b_col_blocks):
    t_all = a.shape[0]
    tt = min(512, t_all)
    nj = max(a_col_blocks, b_col_blocks)
    wa = a.shape[1] // a_col_blocks
    wb = b.shape[1] // b_col_blocks

    def body(a_ref, b_ref, o_ref):
        @pl.when(pl.program_id(1) == 0)
        def _():
            o_ref[...] = jnp.zeros_like(o_ref)

        o_ref[...] += _dot_tn(a_ref[...], b_ref[...])

    return _pcall(
        body, name=name, grid=(nj, t_all // tt),
        in_specs=[pl.BlockSpec((tt, wa), lambda j, t: (t, j if a_col_blocks > 1 else 0)),
                  pl.BlockSpec((tt, wb), lambda j, t: (t, j if b_col_blocks > 1 else 0))],
        out_specs=pl.BlockSpec((None, wa, wb), lambda j, t: (j, 0, 0)),
        out_shape=jax.ShapeDtypeStruct((nj, wa, wb), F32),
        compiler_params=_params(("parallel", "arbitrary")),
    )(a, b)


def _mlp_dhn(du, w_up_t, h, dy, g2):
    t_all = h.shape[0]
    tm = min(256, t_all)

    def body(a_ref, w_ref, h_ref, dy_ref, g_ref, dh_ref, dhb_ref, dg_ref):
        @pl.when(pl.program_id(0) == 0)
        def _():
            dg_ref[...] = jnp.zeros_like(dg_ref)

        dhn = _dot(a_ref[...], w_ref[...])
        h = h_ref[...]
        r = lax.rsqrt(jnp.mean(h * h, axis=-1, keepdims=True) + EPS)
        hh = h * r
        dg_ref[...] += jnp.sum(dhn * hh, axis=0, keepdims=True)
        dz = dhn * g_ref[...]
        dh = dy_ref[...] + r * (dz - hh * jnp.mean(dz * hh, axis=-1, keepdims=True))
        dh_ref[...] = dh
        dhb_ref[...] = dh.astype(BF16)

    return _pcall(
        body, name="mlp_dhn", grid=(t_all // tm,),
        in_specs=[_rows(tm, D_FF), _const((D_FF, D_MODEL)), _rows(tm, D_MODEL), _rows(tm, D_MODEL),
                  _const((1, D_MODEL))],
        out_specs=[_rows(tm, D_MODEL), _rows(tm, D_MODEL), _const((1, D_MODEL))],
        out_shape=[jax.ShapeDtypeStruct((t_all, D_MODEL), F32), jax.ShapeDtypeStruct((t_all, D_MODEL), BF16),
                   jax.ShapeDtypeStruct((1, D_MODEL), F32)],
        compiler_params=_params(("arbitrary",)),
    )(du, w_up_t, h, dy, g2)


def _dmixed(dhb, wo_pad_t, ob):
    t_all = dhb.shape[0]
    tm = min(512, t_all)
    half = A_HEADS * LANES

    def body(a_ref, w_ref, ob_ref, da_ref, db_ref, delta_ref):
        d = _dot(a_ref[...], w_ref[...])
        da_ref[...] = d[:, :half].astype(BF16)
        db_ref[...] = d[:, half:].astype(BF16)
        for h in range(B_HEADS):
            cols = slice(h * LANES, (h + 1) * LANES)
            prod = d[:, half + h * LANES:half + (h + 1) * LANES] * ob_ref[:, cols].astype(F32)
            delta_ref[:, cols] = jnp.broadcast_to(jnp.sum(prod, axis=-1, keepdims=True), (tm, LANES))

    return _pcall(
        body, name="dmixed", grid=(t_all // tm,),
        in_specs=[_rows(tm, D_MODEL), _const((D_MODEL, MIXED_P)), _rows(tm, half)],
        out_specs=[_rows(tm, half), _rows(tm, half), _rows(tm, half)],
        out_shape=[jax.ShapeDtypeStruct((t_all, half), BF16), jax.ShapeDtypeStruct((t_all, half), BF16),
                   jax.ShapeDtypeStruct((t_all, half), F32)],
        compiler_params=_params(("parallel",)),
    )(dhb, wo_pad_t, ob)


def _fox_bwd(qb, kb, vb, dob, lse, delta, nb, seq):
    t_all = qb.shape[0]
    tk = min(512, seq)
    nk = seq // tk

    def body(q_ref, k_ref, v_ref, do_ref, lse_ref, delta_ref, dq_ref, dk_ref, dv_ref,
             s_ref, dp_ref, p_ref, ds_ref, dk_acc, dv_acc):
        kj = pl.program_id(2)

        @pl.when(kj == 0)
        def _():
            dq_ref[...] = jnp.zeros_like(dq_ref)

        dk_acc[...] = jnp.zeros_like(dk_acc)
        dv_acc[...] = jnp.zeros_like(dv_acc)
        k = k_ref[...]
        v = v_ref[...]

        def step(i, masked):
            off = pl.multiple_of(i * tk, tk)
            q = q_ref[pl.ds(off, tk), :]
            do = do_ref[pl.ds(off, tk), :]
            s_ref[...] = _dot_nt(q, k)
            dp_ref[...] = _dot_nt(do, v)
            for r in range(0, tk, CHUNK):
                rows = slice(r, r + CHUNK)
                chunk = pl.ds(pl.multiple_of(off + r, CHUNK), CHUNK)
                lse_c = lse_ref[chunk, :]
                delta_c = delta_ref[chunk, :]
                for jt in range(tk // LANES):
                    cols = slice(jt * LANES, (jt + 1) * LANES)
                    p = jnp.exp2(s_ref[rows, cols] - lse_c)
                    if masked:
                        row = r + lax.broadcasted_iota(jnp.int32, (CHUNK, LANES), 0)
                        col = jt * LANES + lax.broadcasted_iota(jnp.int32, (CHUNK, LANES), 1)
                        p = jnp.where(row >= col, p, 0.0)
                    p_ref[rows, cols] = p.astype(BF16)
                    ds_ref[rows, cols] = (p * (dp_ref[rows, cols] - delta_c)).astype(BF16)
            dv_acc[...] += _dot_tn(p_ref[...], do)
            dk_acc[...] += _dot_tn(ds_ref[...], q)
            dq_ref[pl.ds(off, tk), :] += _dot(ds_ref[...], k)

        def unmasked(i, carry):
            step(i, False)
            return carry

        step(kj, True)
        lax.fori_loop(kj + 1, nk, unmasked, 0)
        dk_ref[...] = dk_acc[...]
        dv_ref[...] = dv_acc[...]

    full = pl.BlockSpec((seq, LANES), lambda b, h, j: (b, h))
    tile = pl.BlockSpec((tk, LANES), lambda b, h, j: (b * nk + j, h))
    shp = jax.ShapeDtypeStruct((t_all, B_HEADS * LANES), F32)
    return _pcall(
        body, name="fox_bwd", grid=(nb, B_HEADS, nk),
        in_specs=[full, tile, tile, full, full, full], out_specs=[full, tile, tile],
        out_shape=[shp, shp, shp],
        scratch_shapes=[pltpu.VMEM((tk, tk), F32), pltpu.VMEM((tk, tk), F32), pltpu.VMEM((tk, tk), BF16),
                        pltpu.VMEM((tk, tk), BF16), pltpu.VMEM((tk, LANES), F32), pltpu.VMEM((tk, LANES), F32)],
        compiler_params=_params(("parallel", "parallel", "arbitrary")),
    )(qb, kb, vb, dob, lse, delta)


def _swa_bwd(qa, ka, va, oa, doa, lrow, sinks, bias, nb, seq):
    t_all = qa.shape[0]
    tq = min(512, seq)
    nq = seq // tq

    def body(sink_ref, q_ref, k_ref, v_ref, bias0_ref, bias_ref, o_ref, do_ref, l_ref,
             dq_ref, dk_ref, dv_ref, dsink_ref):
        qi = pl.program_id(2)
        sink = _sink_column(sink_ref, pl.program_id(1))

        @pl.when(qi == 0)
        def _():
            dk_ref[...] = jnp.zeros_like(dk_ref)
            dv_ref[...] = jnp.zeros_like(dv_ref)
            dsink_ref[...] = jnp.zeros_like(dsink_ref)

        for a in range(tq // WINDOW):
            t0 = qi * tq + a * WINDOW
            start = pl.multiple_of(jnp.maximum(t0 - WINDOW, 0), WINDOW)
            rows = slice(a * WINDOW, (a + 1) * WINDOW)
            win = pl.ds(start, 2 * WINDOW)
            q = _stack_heads(q_ref, rows)
            k = k_ref[win, :]
            v = v_ref[win, :]
            do = _stack_heads(do_ref, rows)
            lrow_t = jnp.max(_stack_heads(l_ref, rows), axis=-1, keepdims=True)
            p = jnp.exp(_dot_nt(q, k) + (bias0_ref if a == 0 else bias_ref)[...] - lrow_t)
            delta = jnp.sum(do.astype(F32) * _stack_heads(o_ref, rows).astype(F32), axis=-1, keepdims=True)
            ds = (p * (_dot_nt(do, v) - delta)).astype(BF16)
            dq = _dot(ds, k)
            dk_ref[win, :] += _dot_tn(ds, q)
            dv_ref[win, :] += _dot_tn(p.astype(BF16), do)
            sink_term = jnp.exp(sink - lrow_t) * delta
            for j in range(A_GROUP):
                part = slice(j * WINDOW, (j + 1) * WINDOW)
                dq_ref[rows, j * LANES:(j + 1) * LANES] = dq[part]
                dsink_ref[j:j + 1, :] -= jnp.broadcast_to(jnp.sum(sink_term[part], axis=0, keepdims=True), (1, LANES))

    smem, qspec, kspec, bias_first, bias_rest = _swa_specs(nq, tq, seq)
    return _pcall(
        body, name="swa_bwd", grid=(nb, A_KV_HEADS, nq),
        in_specs=[smem, qspec, kspec, kspec, bias_first, bias_rest, qspec, qspec, qspec],
        out_specs=[qspec, kspec, kspec, pl.BlockSpec((None, 8, LANES), lambda b, g, i: (b * A_KV_HEADS + g, 0, 0))],
        out_shape=[jax.ShapeDtypeStruct((t_all, A_HEADS * LANES), F32),
                   jax.ShapeDtypeStruct((t_all, A_KV_HEADS * LANES), F32),
                   jax.ShapeDtypeStruct((t_all, A_KV_HEADS * LANES), F32),
                   jax.ShapeDtypeStruct((nb * A_KV_HEADS, 8, LANES), F32)],
        compiler_params=_params(("parallel", "parallel", "arbitrary")),
    )(sinks, qa, ka, va, bias, bias, oa, doa, lrow)


def _dproj(pre, dqa, dka, dqb, dkb, dva, dvb, z, gain_row, seq):
    t_all = pre.shape[0]
    tm = min(256, seq)
    nt = t_all // tm
    tiles_per_seq = seq // tm
    triu = _tri(tm, True)
    sel = _dc_select()

    def body(pre_ref, dqa_ref, dka_ref, dqb_ref, dkb_ref, dva_ref, dvb_ref, z_ref, gain_ref, triu_ref, sel_ref,
             dproj_ref, small_ref, carry_ref):
        i = pl.program_id(0)

        @pl.when(i == 0)
        def _():
            small_ref[...] = jnp.zeros_like(small_ref)

        @pl.when(i % tiles_per_seq == 0)
        def _():
            carry_ref[...] = jnp.zeros_like(carry_ref)

        def norm_bwd(g, dhat):
            cols = slice(g * LANES, (g + 1) * LANES)
            p = pre_ref[:, cols].astype(F32)
            rr = lax.rsqrt(jnp.sum(p * p, axis=-1, keepdims=True) * (1.0 / HEAD_DIM) + EPS)
            n = p * rr
            dz = dhat * gain_ref[:, cols]
            dproj_ref[:, cols] = (rr * (dz - n * (jnp.sum(dz * n, axis=-1, keepdims=True) * (1.0 / HEAD_DIM)))
                                  ).astype(BF16)
            return jnp.sum(dhat * n, axis=0, keepdims=True)

        def group_sum(g0, d_ref, count, scale):
            acc = jnp.zeros((1, LANES), F32)
            for h in range(count):
                d = d_ref[:, h * LANES:(h + 1) * LANES]
                acc = acc + norm_bwd(g0 + h, d * scale if scale != 1.0 else d)
            return acc

        small_ref[0:1, :] += group_sum(G_QA, dqa_ref, A_HEADS, SCALE)
        small_ref[1:2, :] += group_sum(G_KA, dka_ref, A_KV_HEADS, 1.0)
        small_ref[2:3, :] += group_sum(G_QB, dqb_ref, B_HEADS, SCALE)
        small_ref[3:4, :] += group_sum(G_KB, dkb_ref, B_HEADS, LN2)
        dproj_ref[:, G_VA * LANES:G_VB * LANES] = dva_ref[...].astype(BF16)
        dproj_ref[:, G_VB * LANES:G_F * LANES] = dvb_ref[...].astype(BF16)

        dc = jnp.zeros((tm, LANES), F32)
        for piece_q, piece_k in zip(_split3(dqb_ref[...]), _split3(dkb_ref[...])):
            dc = dc + _dot(jnp.concatenate([piece_q, piece_k], axis=1), sel_ref[...])
        dlf = _dot_exact(triu_ref[...], dc) + carry_ref[...]
        carry_ref[...] += jnp.sum(dc, axis=0, keepdims=True)
        dz = dlf / (1.0 + jnp.exp(z_ref[...]))
        small_ref[4:5, :] += jnp.sum(dz, axis=0, keepdims=True)
        dproj_ref[:, G_F * LANES:(G_F + 1) * LANES] = dz.astype(BF16)
        dproj_ref[:, (G_F + 1) * LANES:] = jnp.zeros((tm, LANES), BF16)

    def rev(n):
        return pl.BlockSpec((tm, n), lambda i: (nt - 1 - i, 0))

    return _pcall(
        body, name="dproj", grid=(nt,),
        in_specs=[rev(N_NORM_GROUPS * LANES), rev(A_HEADS * LANES), rev(A_KV_HEADS * LANES), rev(B_HEADS * LANES),
                  rev(B_HEADS * LANES), rev(A_KV_HEADS * LANES), rev(B_HEADS * LANES), rev(LANES),
                  _const((1, NP)), _const((tm, tm)), _const(sel.shape)],
        out_specs=[rev(NP), _const((8, LANES))],
        out_shape=[jax.ShapeDtypeStruct((t_all, NP), BF16), jax.ShapeDtypeStruct((8, LANES), F32)],
        scratch_shapes=[pltpu.VMEM((1, LANES), F32)],
        compiler_params=_params(("arbitrary",)),
    )(pre, dqa, dka, dqb, dkb, dva, dvb, z, gain_row, triu, sel)


def _dx(dproj, w_pad_t, x2, dh, g1):
    t_all = x2.shape[0]
    tm = min(256, t_all)

    def body(a_ref, w_ref, x_ref, dh_ref, g_ref, dx_ref, dg_ref):
        @pl.when(pl.program_id(0) == 0)
        def _():
            dg_ref[...] = jnp.zeros_like(dg_ref)

        dxn = _dot(a_ref[...], w_ref[...])
        x = x_ref[...]
        r = lax.rsqrt(jnp.mean(x * x, axis=-1, keepdims=True) + EPS)
        xh = x * r
        dg_ref[...] += jnp.sum(dxn * xh, axis=0, keepdims=True)
        dz = dxn * g_ref[...]
        dx_ref[...] = dh_ref[...] + r * (dz - xh * jnp.mean(dz * xh, axis=-1, keepdims=True))

    return _pcall(
        body, name="dx", grid=(t_all // tm,),
        in_specs=[_rows(tm, NP), _const((NP, D_MODEL)), _rows(tm, D_MODEL), _rows(tm, D_MODEL), _const((1, D_MODEL))],
        out_specs=[_rows(tm, D_MODEL), _const((1, D_MODEL))],
        out_shape=[jax.ShapeDtypeStruct((t_all, D_MODEL), F32), jax.ShapeDtypeStruct((1, D_MODEL), F32)],
        compiler_params=_params(("arbitrary",)),
    )(dproj, w_pad_t, x2, dh, g1)


def _dwin(dproj, xn):
    t_all = xn.shape[0]
    tt = min(512, t_all)
    half = NP // 2

    def body(a_ref, b_ref, o_ref):
        @pl.when(pl.program_id(1) == 0)
        def _():
            o_ref[...] = jnp.zeros_like(o_ref)

        o_ref[...] += _dot_tn(a_ref[...], b_ref[...])

    return _pcall(
        body, name="dwin", grid=(2, t_all // tt),
        in_specs=[pl.BlockSpec((tt, half), lambda j, t: (t, j)), pl.BlockSpec((tt, D_MODEL), lambda j, t: (t, 0))],
        out_specs=pl.BlockSpec((half, D_MODEL), lambda j, t: (j, 0)),
        out_shape=jax.ShapeDtypeStruct((NP, D_MODEL), F32),
        compiler_params=_params(("parallel", "arbitrary")),
    )(dproj, xn)


ANY = pl.BlockSpec(memory_space=pl.ANY)


def _place():
    return lax.axis_index("x"), lax.axis_index("y"), lax.axis_index("c")


def _allgather_halves(mine):
    m_per, n = mine.shape

    def body(x_ref, out_ref, send_sems, recv_sems, local_sem):
        x, y, c = _place()
        me, sibling = (x, y, c), (x, y, 1 - c)
        chips = [(1 - x, y), (x, 1 - y), (1 - x, 1 - y)]

        def rows(px, py, pc):
            return out_ref.at[pl.ds((4 * px + 2 * py + pc) * m_per, m_per), :]

        def copy(k, block, to, src=None):
            return pltpu.make_async_remote_copy(
                src_ref=rows(*block) if src is None else src, dst_ref=rows(*block),
                send_sem=send_sems.at[k], recv_sem=recv_sems.at[k], device_id=to, device_id_type=MESH)

        own = pltpu.make_async_copy(x_ref, rows(*me), local_sem)
        own.start()
        first = [copy(0, me, sibling, src=x_ref)]
        first += [copy(1 + j, me, (*chip, c), src=x_ref) for j, chip in enumerate(chips)]
        for cp in first:
            cp.start()
        passed = [copy(4 + j, (*chip, c), sibling) for j, chip in enumerate(chips)]
        for j, chip in enumerate(chips):
            copy(1 + j, (*chip, c), me).wait_recv()
            passed[j].start()
        copy(0, sibling, me).wait_recv()
        for j, chip in enumerate(chips):
            copy(4 + j, (*chip, 1 - c), me).wait_recv()
        for cp in first + passed:
            cp.wait_send()
        own.wait()

    return _pcall(
        body, name="allgather_weights",
        out_shape=jax.ShapeDtypeStruct((8 * m_per, n), mine.dtype),
        in_specs=[ANY], out_specs=ANY,
        scratch_shapes=[pltpu.SemaphoreType.DMA((7,)), pltpu.SemaphoreType.DMA((7,)), pltpu.SemaphoreType.DMA],
    )(mine)


def _rs_pair_exchange(g4):
    def body(g_ref, out_ref, send_sem, recv_sem):
        x, y, c = _place()
        cp = pltpu.make_async_remote_copy(
            src_ref=g_ref.at[:, 1 - c], dst_ref=out_ref, send_sem=send_sem, recv_sem=recv_sem,
            device_id=(x, y, 1 - c), device_id_type=MESH)
        cp.start()
        cp.wait()

    return _pcall(
        body, name="rs_pair_exchange",
        out_shape=jax.ShapeDtypeStruct((N_CHIPS, R_HALF, D_MODEL), F32),
        in_specs=[ANY], out_specs=ANY,
        scratch_shapes=[pltpu.SemaphoreType.DMA, pltpu.SemaphoreType.DMA],
    )(g4)


def _rs_pair_add(g4, got, c_idx):
    def body(c_ref, a_ref, b_ref, o_ref, ob_ref):
        pair = a_ref[...] + b_ref[...]
        o_ref[...] = pair
        ob_ref[...] = pair.astype(BF16)

    blk = pl.BlockSpec((None, R_HALF, D_MODEL), lambda s, c_ref: (s, 0, 0))
    return _pcall(
        body, name="rs_pair_add",
        grid_spec=pltpu.PrefetchScalarGridSpec(
            num_scalar_prefetch=1, grid=(N_CHIPS,),
            in_specs=[pl.BlockSpec((None, None, R_HALF, D_MODEL), lambda s, c_ref: (s, c_ref[0], 0, 0)), blk],
            out_specs=[blk, blk]),
        out_shape=[jax.ShapeDtypeStruct((N_CHIPS, R_HALF, D_MODEL), F32),
                   jax.ShapeDtypeStruct((N_CHIPS, R_HALF, D_MODEL), BF16)],
        compiler_params=_params(("parallel",)),
    )(c_idx, g4, got)


def _rs_chip_exchange(p4):
    def body(p_ref, out_ref, send_sems, recv_sems):
        x, y, c = _place()
        chips = [(1 - x, y), (x, 1 - y), (1 - x, 1 - y)]
        cps = [pltpu.make_async_remote_copy(
            src_ref=p_ref.at[2 * cx + cy], dst_ref=out_ref.at[j], send_sem=send_sems.at[j],
            recv_sem=recv_sems.at[j], device_id=(cx, cy, c), device_id_type=MESH)
            for j, (cx, cy) in enumerate(chips)]
        for cp in cps:
            cp.start()
        for cp in cps:
            cp.wait()

    return _pcall(
        body, name="rs_chip_exchange",
        out_shape=jax.ShapeDtypeStruct((3, R_HALF, D_MODEL), p4.dtype),
        in_specs=[ANY], out_specs=ANY,
        scratch_shapes=[pltpu.SemaphoreType.DMA((3,)), pltpu.SemaphoreType.DMA((3,))],
    )(p4)


def _rs_chip_add(p4, got, sc_idx):
    tr = R_HALF // 7

    def body(sc_ref, a_ref, b_ref, o_ref):
        o_ref[...] = ((a_ref[...] + b_ref[0].astype(F32)) + b_ref[1].astype(F32)) + b_ref[2].astype(F32)

    return _pcall(
        body, name="rs_chip_add",
        grid_spec=pltpu.PrefetchScalarGridSpec(
            num_scalar_prefetch=1, grid=(R_HALF // tr,),
            in_specs=[pl.BlockSpec((None, tr, D_MODEL), lambda i, sc_ref: (sc_ref[0], i, 0)),
                      pl.BlockSpec((3, tr, D_MODEL), lambda i, sc_ref: (0, i, 0))],
            out_specs=pl.BlockSpec((None, tr, D_MODEL), lambda i, sc_ref: (sc_ref[1], i, 0))),
        out_shape=jax.ShapeDtypeStruct((2, R_HALF, D_MODEL), F32),
        compiler_params=_params(("parallel",)),
    )(sc_idx, p4, got)


def _rs_pair_share(halves):
    def body(r_ref, out_ref, send_sem, recv_sem):
        x, y, c = _place()
        cp = pltpu.make_async_remote_copy(
            src_ref=r_ref.at[c], dst_ref=out_ref.at[c], send_sem=send_sem, recv_sem=recv_sem,
            device_id=(x, y, 1 - c), device_id_type=MESH)
        cp.start()
        cp.wait()

    return _pcall(
        body, name="rs_pair_share",
        out_shape=jax.ShapeDtypeStruct((2, R_HALF, D_MODEL), F32),
        in_specs=[ANY], out_specs=ANY, input_output_aliases={0: 0},
        scratch_shapes=[pltpu.SemaphoreType.DMA, pltpu.SemaphoreType.DMA],
    )(halves)


def _adam(w, g, m, v):
    m2 = ADAM_B1 * m + (1.0 - ADAM_B1) * g
    v2 = ADAM_B2 * v + (1.0 - ADAM_B2) * (g * g)
    m_hat = m2 / (1.0 - ADAM_B1 ** ADAM_STEP)
    v_hat = v2 / (1.0 - ADAM_B2 ** ADAM_STEP)
    return -ADAM_LR * (m_hat / (jnp.sqrt(v_hat) + ADAM_EPS) + ADAM_WD * w), m2, v2


def _small_allreduce_adamw(part, w, m, v):
    def body(p_ref, w_ref, m_ref, v_ref, g_ref, d_ref, m2_ref, v2_ref, buf, send_sems, recv_sems):
        x, y, c = _place()
        me = 4 * x + 2 * y + c
        cps = []
        for k in range(1, 8):
            peer = (1 - x if k & 4 else x, 1 - y if k & 2 else y, 1 - c if k & 1 else c)
            cps.append(pltpu.make_async_remote_copy(
                src_ref=p_ref, dst_ref=buf.at[me], send_sem=send_sems.at[k - 1], recv_sem=recv_sems.at[k - 1],
                device_id=peer, device_id_type=MESH))
        for cp in cps:
            cp.start()
        buf[me] = p_ref[...]
        for cp in cps:
            cp.wait()
        g = buf[0]
        for k in range(1, 8):
            g = g + buf[k]
        g_ref[...] = g
        d_ref[...], m2_ref[...], v2_ref[...] = _adam(w_ref[...], g, m_ref[...], v_ref[...])

    vm = pl.BlockSpec(memory_space=pltpu.VMEM)
    shp = jax.ShapeDtypeStruct((SMALL_ROWS, LANES), F32)
    return _pcall(
        body, name="small_allreduce_adamw",
        out_shape=[shp, shp, shp, shp], in_specs=[vm, vm, vm, vm], out_specs=[vm, vm, vm, vm],
        scratch_shapes=[pltpu.VMEM((8, SMALL_ROWS, LANES), F32), pltpu.SemaphoreType.DMA((7,)),
                        pltpu.SemaphoreType.DMA((7,))],
    )(part, w, m, v)


def _adamw(w, g, m, v, name):
    rows, cols = w.shape
    tr = min(256, rows)

    def body(w_ref, g_ref, m_ref, v_ref, d_ref, m2_ref, v2_ref):
        d_ref[...], m2_ref[...], v2_ref[...] = _adam(w_ref[...], g_ref[...], m_ref[...], v_ref[...])

    spec = _rows(tr, cols)
    shp = jax.ShapeDtypeStruct((rows, cols), F32)
    return _pcall(
        body, name=name, grid=(rows // tr,), in_specs=[spec] * 4, out_specs=[spec] * 3, out_shape=[shp] * 3,
        compiler_params=_params(("parallel",)),
    )(w, g, m, v)


def _pad_lanes(v):
    return jnp.pad(v, (0, LANES - v.shape[0]))


def _pad_head_rows(w_t, heads):
    n = w_t.shape[1]
    return jnp.pad(w_t.reshape(heads, HEAD_DIM, n), ((0, 0), (0, LANES - HEAD_DIM), (0, 0))).reshape(heads * LANES, n)


def _unpad_head_rows(w_t, heads):
    n = w_t.shape[1]
    return w_t.reshape(heads, LANES, n)[:, :HEAD_DIM].reshape(heads * HEAD_DIM, n)


def _in_rows_pad(w_in_t):
    qa, ka, va, qb, kb, vb, f = jnp.split(w_in_t, [512, 640, 768, 1280, 1792, 2304], axis=0)
    f = jnp.pad(f, ((0, 2 * LANES - B_HEADS), (0, 0)))
    return jnp.concatenate([_pad_head_rows(qa, 8), _pad_head_rows(ka, 2), _pad_head_rows(qb, 8),
                            _pad_head_rows(kb, 8), _pad_head_rows(va, 2), _pad_head_rows(vb, 8), f], axis=0)


def _in_rows_unpad(d):
    qa = _unpad_head_rows(d[G_QA * LANES:G_KA * LANES], 8)
    ka = _unpad_head_rows(d[G_KA * LANES:G_QB * LANES], 2)
    qb = _unpad_head_rows(d[G_QB * LANES:G_KB * LANES], 8)
    kb = _unpad_head_rows(d[G_KB * LANES:G_VA * LANES], 8)
    va = _unpad_head_rows(d[G_VA * LANES:G_VB * LANES], 2)
    vb = _unpad_head_rows(d[G_VB * LANES:G_F * LANES], 8)
    f = d[G_F * LANES:G_F * LANES + B_HEADS]
    return jnp.concatenate([qa, ka, va, qb, kb, vb, f], axis=0)


def _pack_small(g1, bf, qa, ka, sk, qb, kb, g2, loss_row):
    rows = [g1.reshape(8, LANES), g2.reshape(8, LANES)]
    rows += [_pad_lanes(t)[None] for t in (qa, ka, qb, kb, bf, sk)]
    rows += [loss_row, jnp.zeros((1, LANES), F32)]
    return jnp.concatenate(rows, axis=0)


def _unpack_small(p):
    return (p[0:8].reshape(D_MODEL), p[20, :B_HEADS], p[16, :HEAD_DIM], p[17, :HEAD_DIM], p[21, :A_HEADS],
            p[18, :HEAD_DIM], p[19, :HEAD_DIM], p[8:16].reshape(D_MODEL))


def kernel(x, attn_norm_g, w_in, b_forget, q_norm_a, k_norm_a, sink_logits, q_norm_b, k_norm_b, w_out, mlp_norm_g, w_up, w_down, loss_target, m_attn_norm_g, m_w_in, m_b_forget, m_q_norm_a, m_k_norm_a, m_sink_logits, m_q_norm_b, m_k_norm_b, m_w_out, m_mlp_norm_g, m_w_up, m_w_down, v_attn_norm_g, v_w_in, v_b_forget, v_q_norm_a, v_k_norm_a, v_sink_logits, v_q_norm_b, v_k_norm_b, v_w_out, v_mlp_norm_g, v_w_up, v_w_down):
    nb, seq, _ = x.shape
    t_all = nb * seq
    c_idx = lax.axis_index("c")
    s_idx = 2 * lax.axis_index("x") + lax.axis_index("y")

    packed = jnp.concatenate([jnp.pad(w_in.T, ((0, IN_SHARD_P - IN_SHARD), (0, 0))), w_out, w_up, w_down], axis=0)
    mine = lax.dynamic_slice_in_dim(packed.astype(BF16).reshape(2, R_HALF, D_MODEL), c_idx, 1, axis=0)[0]
    gathered = _allgather_halves(mine).reshape(N_CHIPS, R_ALL, D_MODEL)
    w_in_t = gathered[:, :IN_SHARD].reshape(IN_WIDTH, D_MODEL)
    w_pad_t = _in_rows_pad(w_in_t)
    w_pad = w_pad_t.T
    wo_pad = _pad_head_rows(gathered[:, R_OUT:R_UP].reshape(D_MODEL, D_MODEL), A_HEADS + B_HEADS)
    wo_pad_t = wo_pad.T
    w_up_blocks = gathered[:, R_UP:R_DOWN]
    w_up_t = jnp.swapaxes(w_up_blocks, 1, 2).reshape(D_FF, D_MODEL)
    w_down_f = gathered[:, R_DOWN:].reshape(D_FF, D_MODEL)
    w_down_t = w_down_f.T

    ones = jnp.ones((LANES,), F32)
    gain_row = jnp.concatenate(
        [jnp.tile(_pad_lanes(q_norm_a), 8), jnp.tile(_pad_lanes(k_norm_a), 2), jnp.tile(_pad_lanes(q_norm_b), 8),
         jnp.tile(_pad_lanes(k_norm_b), 8), jnp.tile(ones, N_GROUPS - N_NORM_GROUPS)])[None]
    b_row = _pad_lanes(b_forget)[None]
    g1 = attn_norm_g[None]
    g2 = mlp_norm_g[None]
    slopes = jnp.exp2(-(8.0 / A_HEADS) * (jnp.arange(A_HEADS, dtype=F32) + 1.0))

    x2 = x.reshape(t_all, D_MODEL)
    tgt = loss_target.reshape(t_all, D_MODEL)

    xn, pre, qa, ka, va, qb, kb, vb, z = _inproj(x2, g1, w_pad, gain_row, b_row, seq)
    swa_bias = _swa_bias(slopes)
    oa, la = _swa_fwd(qa, ka, va, sink_logits, swa_bias, nb, seq)
    ob, lse = _fox_fwd(qb, kb, vb, nb, seq)
    h, hn = _outproj(x2, oa, ob, wo_pad, g2)
    ru, hid = _mlp_up(hn, w_up_blocks)
    dy, dyb, loss_acc = _mlp_down(hid, w_down_f, h, tgt)

    du = _mlp_dhid(dyb, w_down_t, ru)
    d_w_down = _wgrad(hid, dyb, "dw_down", D_FF // D_MODEL, 1)
    d_w_up = _wgrad(hn, du, "dw_up", 1, D_FF // D_MODEL)
    dh, dhb, d_g2 = _mlp_dhn(du, w_up_t, h, dy, g2)
    doa, dob, delta_b = _dmixed(dhb, wo_pad_t, ob)
    d_wo_a = _wgrad(oa, dhb, "dw_out_a", 1, 1)[0]
    d_wo_b = _wgrad(ob, dhb, "dw_out_b", 1, 1)[0]
    dqb, dkb, dvb = _fox_bwd(qb, kb, vb, dob, lse, delta_b, nb, seq)
    dqa, dka, dva, dsink = _swa_bwd(qa, ka, va, oa, doa, la, sink_logits, swa_bias, nb, seq)
    dproj, small = _dproj(pre, dqa, dka, dqb, dkb, dva, dvb, z, gain_row, seq)
    grad_x, d_g1 = _dx(dproj, w_pad_t, x2, dh, g1)
    d_w_in_t = _dwin(dproj, xn)

    d_w_out = jnp.concatenate([_unpad_head_rows(d_wo_a, 8), _unpad_head_rows(d_wo_b, 8)], axis=0)
    g_pack = jnp.concatenate([
        jnp.pad(_in_rows_unpad(d_w_in_t).reshape(N_CHIPS, IN_SHARD, D_MODEL),
                ((0, 0), (0, IN_SHARD_P - IN_SHARD), (0, 0))),
        d_w_out.reshape(N_CHIPS, D_MODEL // N_CHIPS, D_MODEL), d_w_up, d_w_down], axis=1)
    g4 = g_pack.reshape(N_CHIPS, 2, R_HALF, D_MODEL)
    pair, pair_bf = _rs_pair_add(g4, _rs_pair_exchange(g4), c_idx.reshape(1).astype(jnp.int32))
    halves = _rs_chip_add(pair, _rs_chip_exchange(pair_bf), jnp.stack([s_idx, c_idx]).astype(jnp.int32))
    red = _rs_pair_share(halves).reshape(R_ALL, D_MODEL)
    g_w_in = red[:IN_SHARD].T
    g_w_out = red[R_OUT:R_UP]
    g_w_up = red[R_UP:R_DOWN]
    g_w_down = red[R_DOWN:]

    loss_row = loss_acc[0:1] * (0.5 / D_MODEL)
    d_sink = dsink[:, :A_GROUP, 0].reshape(nb, A_HEADS).sum(axis=0)
    part = _pack_small(d_g1[0], small[4, :B_HEADS], small[0, :HEAD_DIM], small[1, :HEAD_DIM], d_sink,
                       small[2, :HEAD_DIM], small[3, :HEAD_DIM], d_g2[0], loss_row)
    zero_row = jnp.zeros((1, LANES), F32)
    smalls = lambda t: _pack_small(*t, zero_row)
    w_small = smalls((attn_norm_g, b_forget, q_norm_a, k_norm_a, sink_logits, q_norm_b, k_norm_b, mlp_norm_g))
    m_small = smalls((m_attn_norm_g, m_b_forget, m_q_norm_a, m_k_norm_a, m_sink_logits, m_q_norm_b, m_k_norm_b,
                      m_mlp_norm_g))
    v_small = smalls((v_attn_norm_g, v_b_forget, v_q_norm_a, v_k_norm_a, v_sink_logits, v_q_norm_b, v_k_norm_b,
                      v_mlp_norm_g))
    g_s, d_s, m_s, v_s = _small_allreduce_adamw(part, w_small, m_small, v_small)
    loss = g_s[ROW_LOSS, 0]

    big = {}
    for name, w, g, m, v in (("adamw_w_in", w_in, g_w_in, m_w_in, v_w_in),
                             ("adamw_w_out", w_out, g_w_out, m_w_out, v_w_out),
                             ("adamw_w_up", w_up, g_w_up, m_w_up, v_w_up),
                             ("adamw_w_down", w_down, g_w_down, m_w_down, v_w_down)):
        big[name] = (g,) + tuple(_adamw(w, g, m, v, name))

    def assemble(k, small_pack):
        s = _unpack_small(small_pack)
        return (s[0], big["adamw_w_in"][k], s[1], s[2], s[3], s[4], s[5], s[6], big["adamw_w_out"][k], s[7],
                big["adamw_w_up"][k], big["adamw_w_down"][k])

    return (loss, grad_x.reshape(nb, seq, D_MODEL), *assemble(0, g_s), *assemble(1, d_s), *assemble(2, m_s),
            *assemble(3, v_s))
```

```python
import functools

import numpy as np
import jax
import jax.numpy as jnp
from jax import lax
from jax.experimental import pallas as pl
from jax.experimental.pallas import tpu as pltpu

F32 = jnp.float32
BF16 = jnp.bfloat16

D_MODEL = 1024
HEAD_DIM = 64
LANES = 128
A_HEADS = 8
A_KV_HEADS = 2
A_GROUP = A_HEADS // A_KV_HEADS
B_HEADS = 8
WINDOW = 128
D_FF = 4096
IN_WIDTH = 2312
EPS = 1e-6
SCALE = 0.125
LOG2E = 1.4426950408889634
LN2 = 0.6931471805599453
CHUNK = 32
NEG = -1e30

G_QA, G_KA, G_QB, G_KB, G_VA, G_VB, G_F = 0, 8, 10, 18, 26, 28, 36
N_NORM_GROUPS = 26
N_GROUPS = 38
NP = N_GROUPS * LANES
MIXED_P = (A_HEADS + B_HEADS) * LANES

N_CHIPS = 4
IN_SHARD = IN_WIDTH // N_CHIPS
IN_SHARD_P = 608
R_OUT = IN_SHARD_P
R_UP = R_OUT + D_MODEL // N_CHIPS
R_DOWN = R_UP + D_MODEL
R_ALL = R_DOWN + D_FF // N_CHIPS
R_HALF = R_ALL // 2

SMALL_ROWS = 24
ROW_LOSS = 22

ADAM_LR = 0.001
ADAM_B1 = 0.9
ADAM_B2 = 0.999
ADAM_EPS = 1e-08
ADAM_WD = 0.01
ADAM_STEP = 10

VMEM_LIMIT = 52 * 1024 * 1024
MESH = pl.DeviceIdType.MESH


def _pcall(body, **kw):
    return pl.pallas_call(body, **kw)


def _params(sem=None):
    return pltpu.CompilerParams(dimension_semantics=sem, vmem_limit_bytes=VMEM_LIMIT)


def _dot(a, b):
    return jnp.dot(a, b, preferred_element_type=F32)


def _dot_nt(a, b):
    return lax.dot_general(a, b, (((1,), (1,)), ((), ())), preferred_element_type=F32)


def _dot_tn(a, b):
    return lax.dot_general(a, b, (((0,), (0,)), ((), ())), preferred_element_type=F32)


def _split3(x):
    hi = x.astype(BF16)
    r1 = x - hi.astype(F32)
    mid = r1.astype(BF16)
    lo = (r1 - mid.astype(F32)).astype(BF16)
    return hi, mid, lo


def _dot_exact(mat, x):
    hi, mid, lo = _split3(x)
    return _dot(mat, lo) + _dot(mat, mid) + _dot(mat, hi)


def _const(shape):
    zeros = (0,) * len(shape)
    return pl.BlockSpec(shape, lambda *_: zeros)


def _rows(tm, n):
    return pl.BlockSpec((tm, n), lambda i: (i, 0))


def _aug_select():
    e = np.zeros((3 * LANES, 2 * B_HEADS * LANES), np.float32)
    for j in range(3):
        for h in range(B_HEADS):
            e[j * LANES + h, h * LANES + HEAD_DIM + j] = 1.0
            e[j * LANES + h, (B_HEADS + h) * LANES + HEAD_DIM + 3 + j] = -1.0
    return jnp.asarray(e, BF16)


def _dc_select():
    e = np.zeros((2 * B_HEADS * LANES, LANES), np.float32)
    for h in range(B_HEADS):
        e[h * LANES + HEAD_DIM, h] = 1.0
        e[(B_HEADS + h) * LANES + HEAD_DIM + 3, h] = -1.0
    return jnp.asarray(e, BF16)


def _tri(n, upper):
    t = np.tril(np.ones((n, n), np.float32))
    return jnp.asarray(t.T if upper else t, BF16)


def _inproj(x2, g1, w_pad, gain_row, b_row, seq):
    t_all = x2.shape[0]
    tm = min(256, seq)
    tiles_per_seq = seq // tm
    tri = _tri(tm, False)
    esel = _aug_select()

    def body(x_ref, g_ref, w_ref, gain_ref, b_ref, tri_ref, e_ref,
             xn_ref, pre_ref, qa_ref, ka_ref, va_ref, qb_ref, kb_ref, vb_ref, z_ref, carry_ref):
        i = pl.program_id(0)

        @pl.when(i % tiles_per_seq == 0)
        def _():
            carry_ref[...] = jnp.zeros_like(carry_ref)

        x = x_ref[...]
        r = lax.rsqrt(jnp.mean(x * x, axis=-1, keepdims=True) + EPS)
        xn = (x * r * g_ref[...]).astype(BF16)
        xn_ref[...] = xn
        proj = _dot(xn, w_ref[...])
        pre_ref[...] = proj[:, :N_NORM_GROUPS * LANES].astype(BF16)
        lane = lax.broadcasted_iota(jnp.int32, (tm, LANES), 1)

        z = proj[:, G_F * LANES:(G_F + 1) * LANES] + b_ref[...]
        z_ref[...] = z
        lf = jnp.minimum(z, 0.0) - jnp.log(1.0 + jnp.exp(-jnp.abs(z)))
        lf = jnp.where(lane < B_HEADS, lf, 0.0)
        c = _dot_exact(tri_ref[...], lf) + carry_ref[...]
        carry_ref[...] += jnp.sum(lf, axis=0, keepdims=True)
        aug = _dot(jnp.concatenate(_split3(c * LOG2E), axis=1), e_ref[...])

        def hnorm(g):
            p = proj[:, g * LANES:(g + 1) * LANES]
            rr = lax.rsqrt(jnp.sum(p * p, axis=-1, keepdims=True) * (1.0 / HEAD_DIM) + EPS)
            return p * rr * gain_ref[:, g * LANES:(g + 1) * LANES]

        ones_q = jnp.where((lane >= HEAD_DIM + 3) & (lane < HEAD_DIM + 6), 1.0, 0.0)
        ones_k = jnp.where((lane >= HEAD_DIM) & (lane < HEAD_DIM + 3), 1.0, 0.0)
        for h in range(A_HEADS):
            qa_ref[:, h * LANES:(h + 1) * LANES] = (hnorm(G_QA + h) * SCALE).astype(BF16)
        for h in range(A_KV_HEADS):
            ka_ref[:, h * LANES:(h + 1) * LANES] = hnorm(G_KA + h).astype(BF16)
        for h in range(B_HEADS):
            qb_ref[:, h * LANES:(h + 1) * LANES] = (
                hnorm(G_QB + h) * (SCALE * LOG2E) + aug[:, h * LANES:(h + 1) * LANES] + ones_q).astype(BF16)
            kb_ref[:, h * LANES:(h + 1) * LANES] = (
                hnorm(G_KB + h) + aug[:, (B_HEADS + h) * LANES:(B_HEADS + h + 1) * LANES] + ones_k).astype(BF16)
        va_ref[...] = proj[:, G_VA * LANES:G_VB * LANES].astype(BF16)
        one_v = jnp.where(lane == HEAD_DIM, 1.0, 0.0)
        for h in range(B_HEADS):
            cols = slice((G_VB + h) * LANES, (G_VB + h + 1) * LANES)
            vb_ref[:, h * LANES:(h + 1) * LANES] = (proj[:, cols] + one_v).astype(BF16)

    widths = [(D_MODEL, BF16), (N_NORM_GROUPS * LANES, BF16), (A_HEADS * LANES, BF16), (A_KV_HEADS * LANES, BF16),
              (A_KV_HEADS * LANES, BF16), (B_HEADS * LANES, BF16), (B_HEADS * LANES, BF16), (B_HEADS * LANES, BF16),
              (LANES, F32)]
    return _pcall(
        body, name="inproj", grid=(t_all // tm,),
        in_specs=[_rows(tm, D_MODEL), _const((1, D_MODEL)), _const((D_MODEL, NP)), _const((1, NP)),
                  _const((1, LANES)), _const((tm, tm)), _const(esel.shape)],
        out_specs=[_rows(tm, w) for w, _ in widths],
        out_shape=[jax.ShapeDtypeStruct((t_all, w), dt) for w, dt in widths],
        scratch_shapes=[pltpu.VMEM((1, LANES), F32)],
        compiler_params=_params(("arbitrary",)),
    )(x2, g1, w_pad, gain_row, b_row, tri, esel)


def _fox_fwd(qb, kb, vb, nb, seq):
    t_all = qb.shape[0]
    tq = min(512, seq)
    nq = seq // tq

    def body(q_ref, k_ref, v_ref, o_ref, lse_ref, s_ref, p_ref, m_ref, alpha_ref, acc_ref):
        qi = pl.program_id(2)
        q = q_ref[...]
        hq = tq // 2
        m_ref[...] = jnp.full((tq, LANES), NEG, F32)
        acc_ref[...] = jnp.zeros((tq, LANES), F32)

        def step(j, masked):
            off = pl.multiple_of(j * tq, tq)
            k = k_ref[pl.ds(off, tq), :]
            v = v_ref[pl.ds(off, tq), :]
            for hf in range(2):
                s_ref[hf] = _dot_nt(q[hf * hq:(hf + 1) * hq], k)
            for hf in range(2):
                for r in range(0, hq, CHUNK):
                    rows = slice(r, r + CHUNK)
                    grows = slice(hf * hq + r, hf * hq + r + CHUNK)
                    tiles = []
                    for jt in range(tq // LANES):
                        sc = s_ref[hf, rows, jt * LANES:(jt + 1) * LANES]
                        if masked:
                            row = hf * hq + r + lax.broadcasted_iota(jnp.int32, (CHUNK, LANES), 0)
                            col = jt * LANES + lax.broadcasted_iota(jnp.int32, (CHUNK, LANES), 1)
                            sc = jnp.where(row >= col, sc, NEG)
                        tiles.append(sc)
                    m_prev = m_ref[grows, :]
                    m_cur = functools.reduce(jnp.maximum, tiles)
                    m_new = jnp.maximum(m_prev, jnp.max(m_cur, axis=-1, keepdims=True))
                    m_ref[grows, :] = m_new
                    alpha_ref[grows, :] = jnp.exp2(m_prev - m_new)
                    for jt, sc in enumerate(tiles):
                        p_ref[hf, rows, jt * LANES:(jt + 1) * LANES] = jnp.exp2(sc - m_new).astype(BF16)
                hrows = slice(hf * hq, (hf + 1) * hq)
                acc_ref[hrows, :] = alpha_ref[hrows, :] * acc_ref[hrows, :] + _dot(p_ref[hf], v)

        def unmasked(j, carry):
            step(j, False)
            return carry

        lax.fori_loop(0, qi, unmasked, 0)
        step(qi, True)
        acc = acc_ref[...]
        lane = lax.broadcasted_iota(jnp.int32, (tq, LANES), 1)
        l = jnp.sum(jnp.where(lane == HEAD_DIM, acc, 0.0), axis=-1, keepdims=True)
        o_ref[...] = (acc / l).astype(BF16)
        lse_ref[...] = m_ref[...] + jnp.log2(l)

    qspec = pl.BlockSpec((tq, LANES), lambda b, h, i: (b * nq + i, h))
    kspec = pl.BlockSpec((seq, LANES), lambda b, h, i: (b, h))
    return _pcall(
        body, name="fox_fwd", grid=(nb, B_HEADS, nq),
        in_specs=[qspec, kspec, kspec], out_specs=[qspec, qspec],
        out_shape=[jax.ShapeDtypeStruct((t_all, B_HEADS * LANES), BF16),
                   jax.ShapeDtypeStruct((t_all, B_HEADS * LANES), F32)],
        scratch_shapes=[pltpu.VMEM((2, tq // 2, tq), F32), pltpu.VMEM((2, tq // 2, tq), BF16),
                        pltpu.VMEM((tq, LANES), F32),
                        pltpu.VMEM((tq, LANES), F32), pltpu.VMEM((tq, LANES), F32)],
        compiler_params=_params(("parallel", "parallel", "arbitrary")),
    )(qb, kb, vb)


def _swa_bias(slopes):
    row = jnp.arange(A_GROUP * WINDOW, dtype=jnp.int32)[:, None] % WINDOW
    col = jnp.arange(2 * WINDOW, dtype=jnp.int32)[None, :]
    slope_rows = jnp.repeat(slopes.reshape(A_KV_HEADS, A_GROUP), WINDOW, axis=1)[:, :, None]
    out = []
    for t_rel in (0, WINDOW):
        dist = t_rel + row - col
        valid = (dist >= 0) & (dist < WINDOW)
        out.append(jnp.where(valid[None], -slope_rows * dist.astype(F32)[None], NEG))
    return jnp.stack(out)


def _stack_heads(ref, rows):
    return jnp.concatenate([ref[rows, j * LANES:(j + 1) * LANES] for j in range(A_GROUP)], axis=0)


def _sink_column(sink_ref, g):
    return jnp.concatenate([jnp.full((WINDOW, 1), sink_ref[g * A_GROUP + j], F32) for j in range(A_GROUP)], axis=0)


def _swa_specs(nq, tq, seq):
    smem = pl.BlockSpec(memory_space=pltpu.SMEM)
    qspec = pl.BlockSpec((tq, A_GROUP * LANES), lambda b, g, i: (b * nq + i, g))
    kspec = pl.BlockSpec((seq, LANES), lambda b, g, i: (b, g))
    bias_first = pl.BlockSpec((None, None, A_GROUP * WINDOW, 2 * WINDOW),
                              lambda b, g, i: (jnp.minimum(i, 1), g, 0, 0))
    bias_rest = pl.BlockSpec((None, None, A_GROUP * WINDOW, 2 * WINDOW), lambda b, g, i: (1, g, 0, 0))
    return smem, qspec, kspec, bias_first, bias_rest


def _swa_fwd(qa, ka, va, sinks, bias, nb, seq):
    t_all = qa.shape[0]
    tq = min(512, seq)
    nq = seq // tq

    def body(sink_ref, q_ref, k_ref, v_ref, bias0_ref, bias_ref, o_ref, l_ref):
        qi = pl.program_id(2)
        sink = _sink_column(sink_ref, pl.program_id(1))
        for a in range(tq // WINDOW):
            t0 = qi * tq + a * WINDOW
            start = pl.multiple_of(jnp.maximum(t0 - WINDOW, 0), WINDOW)
            rows = slice(a * WINDOW, (a + 1) * WINDOW)
            k = k_ref[pl.ds(start, 2 * WINDOW), :]
            v = v_ref[pl.ds(start, 2 * WINDOW), :]
            s = _dot_nt(_stack_heads(q_ref, rows), k) + (bias0_ref if a == 0 else bias_ref)[...]
            m = jnp.maximum(jnp.max(s, axis=-1, keepdims=True), sink)
            p = jnp.exp(s - m)
            den = jnp.sum(p, axis=-1, keepdims=True) + jnp.exp(sink - m)
            o = _dot((p / den).astype(BF16), v).astype(BF16)
            lrow = jnp.broadcast_to(m + jnp.log(den), (A_GROUP * WINDOW, LANES))
            for j in range(A_GROUP):
                o_ref[rows, j * LANES:(j + 1) * LANES] = o[j * WINDOW:(j + 1) * WINDOW]
                l_ref[rows, j * LANES:(j + 1) * LANES] = lrow[j * WINDOW:(j + 1) * WINDOW]

    smem, qspec, kspec, bias_first, bias_rest = _swa_specs(nq, tq, seq)
    return _pcall(
        body, name="swa_fwd", grid=(nb, A_KV_HEADS, nq),
        in_specs=[smem, qspec, kspec, kspec, bias_first, bias_rest], out_specs=[qspec, qspec],
        out_shape=[jax.ShapeDtypeStruct((t_all, A_HEADS * LANES), BF16),
                   jax.ShapeDtypeStruct((t_all, A_HEADS * LANES), F32)],
        compiler_params=_params(("parallel", "parallel", "arbitrary")),
    )(sinks, qa, ka, va, bias, bias)


def _outproj(x2, oa, ob, wo_pad, g2):
    t_all = x2.shape[0]
    tm = min(512, t_all)
    half = A_HEADS * LANES

    def body(x_ref, oa_ref, ob_ref, w_ref, g_ref, h_ref, hn_ref):
        h = x_ref[...] + _dot(oa_ref[...], w_ref[:half, :]) + _dot(ob_ref[...], w_ref[half:, :])
        h_ref[...] = h
        r = lax.rsqrt(jnp.mean(h * h, axis=-1, keepdims=True) + EPS)
        hn_ref[...] = (h * r * g_ref[...]).astype(BF16)

    return _pcall(
        body, name="outproj", grid=(t_all // tm,),
        in_specs=[_rows(tm, D_MODEL), _rows(tm, half), _rows(tm, half), _const((MIXED_P, D_MODEL)),
                  _const((1, D_MODEL))],
        out_specs=[_rows(tm, D_MODEL), _rows(tm, D_MODEL)],
        out_shape=[jax.ShapeDtypeStruct((t_all, D_MODEL), F32), jax.ShapeDtypeStruct((t_all, D_MODEL), BF16)],
        compiler_params=_params(("parallel",)),
    )(x2, oa, ob, wo_pad, g2)


def _mlp_fwd(hn, w_up_blocks, w_down, h, tgt):
    t_all = h.shape[0]
    tm = min(256, t_all)
    nj = D_FF // D_MODEL

    def body(a_ref, wu_ref, wd_ref, h_ref, t_ref, ru_ref, dy_ref, dyb_ref, loss_ref):
        @pl.when(pl.program_id(0) == 0)
        def _():
            loss_ref[...] = jnp.zeros_like(loss_ref)

        a = a_ref[...]
        y = h_ref[...]
        for j in range(nj):
            cols = slice(j * D_MODEL, (j + 1) * D_MODEL)
            ru = jnp.maximum(_dot(a, wu_ref[j]), 0.0)
            ru_ref[:, cols] = ru.astype(BF16)
            y = y + _dot((ru * ru).astype(BF16), wd_ref[cols, :])
        err = y - t_ref[...]
        loss_ref[...] += jnp.sum(err * err)
        dy = err * (1.0 / D_MODEL)
        dy_ref[...] = dy
        dyb_ref[...] = dy.astype(BF16)

    return _pcall(
        body, name="mlp_fwd", grid=(t_all // tm,),
        in_specs=[_rows(tm, D_MODEL), _const((nj, D_MODEL, D_MODEL)), _const((D_FF, D_MODEL)), _rows(tm, D_MODEL),
                  _rows(tm, D_MODEL)],
        out_specs=[_rows(tm, D_FF), _rows(tm, D_MODEL), _rows(tm, D_MODEL), _const((8, LANES))],
        out_shape=[jax.ShapeDtypeStruct((t_all, D_FF), BF16), jax.ShapeDtypeStruct((t_all, D_MODEL), F32),
                   jax.ShapeDtypeStruct((t_all, D_MODEL), BF16), jax.ShapeDtypeStruct((8, LANES), F32)],
        compiler_params=_params(("arbitrary",)),
    )(hn, w_up_blocks, w_down, h, tgt)


def _mlp_bwd_w(dyb, w_down_t, ru, hn):
    t_all = dyb.shape[0]
    tm = min(512, t_all)
    nj = D_FF // D_MODEL

    def body(dy_ref, w_ref, ru_ref, hn_ref, du_ref, dwd_ref, dwu_ref):
        @pl.when(pl.program_id(1) == 0)
        def _():
            dwd_ref[...] = jnp.zeros_like(dwd_ref)
            dwu_ref[...] = jnp.zeros_like(dwu_ref)

        dy = dy_ref[...]
        ru = ru_ref[...].astype(F32)
        du = (_dot(dy, w_ref[...]) * (2.0 * ru)).astype(BF16)
        du_ref[...] = du
        dwd_ref[...] += _dot_tn((ru * ru).astype(BF16), dy)
        dwu_ref[...] += _dot_tn(hn_ref[...], du)

    tok = pl.BlockSpec((tm, D_MODEL), lambda j, i: (i, 0))
    blk = pl.BlockSpec((tm, D_MODEL), lambda j, i: (i, j))
    wspec = pl.BlockSpec((None, D_MODEL, D_MODEL), lambda j, i: (j, 0, 0))
    wshape = jax.ShapeDtypeStruct((nj, D_MODEL, D_MODEL), F32)
    return _pcall(
        body, name="mlp_bwd_w", grid=(nj, t_all // tm),
        in_specs=[tok, pl.BlockSpec((D_MODEL, D_MODEL), lambda j, i: (0, j)), blk, tok],
        out_specs=[blk, wspec, wspec],
        out_shape=[jax.ShapeDtypeStruct((t_all, D_FF), BF16), wshape, wshape],
        compiler_params=_params(("parallel", "arbitrary")),
    )(dyb, w_down_t, ru, hn)


def _mlp_dhn(du, w_up_t, h, dy, g2):
    t_all = h.shape[0]
    tm = min(256, t_all)

    def body(a_ref, w_ref, h_ref, dy_ref, g_ref, dh_ref, dhb_ref, dg_ref):
        @pl.when(pl.program_id(0) == 0)
        def _():
            dg_ref[...] = jnp.zeros_like(dg_ref)

        dhn = _dot(a_ref[...], w_ref[...])
        h = h_ref[...]
        r = lax.rsqrt(jnp.mean(h * h, axis=-1, keepdims=True) + EPS)
        hh = h * r
        dg_ref[...] += jnp.sum(dhn * hh, axis=0, keepdims=True)
        dz = dhn * g_ref[...]
        dh = dy_ref[...] + r * (dz - hh * jnp.mean(dz * hh, axis=-1, keepdims=True))
        dh_ref[...] = dh
        dhb_ref[...] = dh.astype(BF16)

    return _pcall(
        body, name="mlp_dhn", grid=(t_all // tm,),
        in_specs=[_rows(tm, D_FF), _const((D_FF, D_MODEL)), _rows(tm, D_MODEL), _rows(tm, D_MODEL),
                  _const((1, D_MODEL))],
        out_specs=[_rows(tm, D_MODEL), _rows(tm, D_MODEL), _const((1, D_MODEL))],
        out_shape=[jax.ShapeDtypeStruct((t_all, D_MODEL), F32), jax.ShapeDtypeStruct((t_all, D_MODEL), BF16),
                   jax.ShapeDtypeStruct((1, D_MODEL), F32)],
        compiler_params=_params(("arbitrary",)),
    )(du, w_up_t, h, dy, g2)


def _dmixed(dhb, wo_pad_t, oa, ob):
    t_all = dhb.shape[0]
    tm = min(512, t_all)
    half = A_HEADS * LANES

    def body(a_ref, w_ref, oa_ref, ob_ref, da_ref, db_ref, delta_ref, dwo_ref):
        @pl.when(pl.program_id(0) == 0)
        def _():
            dwo_ref[...] = jnp.zeros_like(dwo_ref)

        a = a_ref[...]
        d = _dot(a, w_ref[...])
        da_ref[...] = d[:, :half].astype(BF16)
        db_ref[...] = d[:, half:].astype(BF16)
        for h in range(B_HEADS):
            cols = slice(h * LANES, (h + 1) * LANES)
            prod = d[:, half + h * LANES:half + (h + 1) * LANES] * ob_ref[:, cols].astype(F32)
            delta_ref[:, cols] = jnp.broadcast_to(jnp.sum(prod, axis=-1, keepdims=True), (tm, LANES))
        dwo_ref[:half, :] += _dot_tn(oa_ref[...], a)
        dwo_ref[half:, :] += _dot_tn(ob_ref[...], a)

    return _pcall(
        body, name="dmixed", grid=(t_all // tm,),
        in_specs=[_rows(tm, D_MODEL), _const((D_MODEL, MIXED_P)), _rows(tm, half), _rows(tm, half)],
        out_specs=[_rows(tm, half), _rows(tm, half), _rows(tm, half), _const((MIXED_P, D_MODEL))],
        out_shape=[jax.ShapeDtypeStruct((t_all, half), BF16), jax.ShapeDtypeStruct((t_all, half), BF16),
                   jax.ShapeDtypeStruct((t_all, half), F32), jax.ShapeDtypeStruct((MIXED_P, D_MODEL), F32)],
        compiler_params=_params(("arbitrary",)),
    )(dhb, wo_pad_t, oa, ob)


def _fox_bwd(qb, kb, vb, dob, lse, delta, nb, seq):
    t_all = qb.shape[0]
    tk = min(512, seq)
    nk = seq // tk

    def body(q_ref, k_ref, v_ref, do_ref, lse_ref, delta_ref, dq_ref, dk_ref, dv_ref,
             s_ref, dp_ref, p_ref, ds_ref, dk_acc, dv_acc):
        kj = pl.program_id(2)

        @pl.when(kj == 0)
        def _():
            dq_ref[...] = jnp.zeros_like(dq_ref)

        dk_acc[...] = jnp.zeros_like(dk_acc)
        dv_acc[...] = jnp.zeros_like(dv_acc)
        k = k_ref[...]
        v = v_ref[...]

        def step(i, masked):
            off = pl.multiple_of(i * tk, tk)
            q = q_ref[pl.ds(off, tk), :]
            do = do_ref[pl.ds(off, tk), :]
            s_ref[...] = _dot_nt(q, k)
            dp_ref[...] = _dot_nt(do, v)
            for r in range(0, tk, CHUNK):
                rows = slice(r, r + CHUNK)
                chunk = pl.ds(pl.multiple_of(off + r, CHUNK), CHUNK)
                lse_c = lse_ref[chunk, :]
                delta_c = delta_ref[chunk, :]
                for jt in range(tk // LANES):
                    cols = slice(jt * LANES, (jt + 1) * LANES)
                    p = jnp.exp2(s_ref[rows, cols] - lse_c)
                    if masked:
                        row = r + lax.broadcasted_iota(jnp.int32, (CHUNK, LANES), 0)
                        col = jt * LANES + lax.broadcasted_iota(jnp.int32, (CHUNK, LANES), 1)
                        p = jnp.where(row >= col, p, 0.0)
                    p_ref[rows, cols] = p.astype(BF16)
                    ds_ref[rows, cols] = (p * (dp_ref[rows, cols] - delta_c)).astype(BF16)
            dv_acc[...] += _dot_tn(p_ref[...], do)
            dk_acc[...] += _dot_tn(ds_ref[...], q)
            dq_ref[pl.ds(off, tk), :] += _dot(ds_ref[...], k)

        def unmasked(i, carry):
            step(i, False)
            return carry

        step(kj, True)
        lax.fori_loop(kj + 1, nk, unmasked, 0)
        dk_ref[...] = dk_acc[...]
        dv_ref[...] = dv_acc[...]

    full = pl.BlockSpec((seq, LANES), lambda b, h, j: (b, h))
    tile = pl.BlockSpec((tk, LANES), lambda b, h, j: (b * nk + j, h))
    shp = jax.ShapeDtypeStruct((t_all, B_HEADS * LANES), F32)
    return _pcall(
        body, name="fox_bwd", grid=(nb, B_HEADS, nk),
        in_specs=[full, tile, tile, full, full, full], out_specs=[full, tile, tile],
        out_shape=[shp, shp, shp],
        scratch_shapes=[pltpu.VMEM((tk, tk), F32), pltpu.VMEM((tk, tk), F32), pltpu.VMEM((tk, tk), BF16),
                        pltpu.VMEM((tk, tk), BF16), pltpu.VMEM((tk, LANES), F32), pltpu.VMEM((tk, LANES), F32)],
        compiler_params=_params(("parallel", "parallel", "arbitrary")),
    )(qb, kb, vb, dob, lse, delta)


def _swa_bwd(qa, ka, va, oa, doa, lrow, sinks, bias, nb, seq):
    t_all = qa.shape[0]
    tq = min(512, seq)
    nq = seq // tq

    def body(sink_ref, q_ref, k_ref, v_ref, bias0_ref, bias_ref, o_ref, do_ref, l_ref,
             dq_ref, dk_ref, dv_ref, dsink_ref):
        qi = pl.program_id(2)
        sink = _sink_column(sink_ref, pl.program_id(1))

        @pl.when(qi == 0)
        def _():
            dk_ref[...] = jnp.zeros_like(dk_ref)
            dv_ref[...] = jnp.zeros_like(dv_ref)
            dsink_ref[...] = jnp.zeros_like(dsink_ref)

        for a in range(tq // WINDOW):
            t0 = qi * tq + a * WINDOW
            start = pl.multiple_of(jnp.maximum(t0 - WINDOW, 0), WINDOW)
            rows = slice(a * WINDOW, (a + 1) * WINDOW)
            win = pl.ds(start, 2 * WINDOW)
            q = _stack_heads(q_ref, rows)
            k = k_ref[win, :]
            v = v_ref[win, :]
            do = _stack_heads(do_ref, rows)
            lrow_t = jnp.max(_stack_heads(l_ref, rows), axis=-1, keepdims=True)
            p = jnp.exp(_dot_nt(q, k) + (bias0_ref if a == 0 else bias_ref)[...] - lrow_t)
            delta = jnp.sum(do.astype(F32) * _stack_heads(o_ref, rows).astype(F32), axis=-1, keepdims=True)
            ds = (p * (_dot_nt(do, v) - delta)).astype(BF16)
            dq = _dot(ds, k)
            dk_ref[win, :] += _dot_tn(ds, q)
            dv_ref[win, :] += _dot_tn(p.astype(BF16), do)
            sink_term = jnp.exp(sink - lrow_t) * delta
            for j in range(A_GROUP):
                part = slice(j * WINDOW, (j + 1) * WINDOW)
                dq_ref[rows, j * LANES:(j + 1) * LANES] = dq[part]
                dsink_ref[j:j + 1, :] -= jnp.broadcast_to(jnp.sum(sink_term[part], axis=0, keepdims=True), (1, LANES))

    smem, qspec, kspec, bias_first, bias_rest = _swa_specs(nq, tq, seq)
    return _pcall(
        body, name="swa_bwd", grid=(nb, A_KV_HEADS, nq),
        in_specs=[smem, qspec, kspec, kspec, bias_first, bias_rest, qspec, qspec, qspec],
        out_specs=[qspec, kspec, kspec, pl.BlockSpec((None, 8, LANES), lambda b, g, i: (b * A_KV_HEADS + g, 0, 0))],
        out_shape=[jax.ShapeDtypeStruct((t_all, A_HEADS * LANES), F32),
                   jax.ShapeDtypeStruct((t_all, A_KV_HEADS * LANES), F32),
                   jax.ShapeDtypeStruct((t_all, A_KV_HEADS * LANES), F32),
                   jax.ShapeDtypeStruct((nb * A_KV_HEADS, 8, LANES), F32)],
        compiler_params=_params(("parallel", "parallel", "arbitrary")),
    )(sinks, qa, ka, va, bias, bias, oa, doa, lrow)


def _dproj(pre, dqa, dka, dqb, dkb, dva, dvb, z, gain_row, seq):
    t_all = pre.shape[0]
    tm = min(256, seq)
    nt = t_all // tm
    tiles_per_seq = seq // tm
    triu = _tri(tm, True)
    sel = _dc_select()

    def body(pre_ref, dqa_ref, dka_ref, dqb_ref, dkb_ref, dva_ref, dvb_ref, z_ref, gain_ref, triu_ref, sel_ref,
             dproj_ref, small_ref, carry_ref):
        i = pl.program_id(0)

        @pl.when(i == 0)
        def _():
            small_ref[...] = jnp.zeros_like(small_ref)

        @pl.when(i % tiles_per_seq == 0)
        def _():
            carry_ref[...] = jnp.zeros_like(carry_ref)

        def norm_bwd(g, dhat):
            cols = slice(g * LANES, (g + 1) * LANES)
            p = pre_ref[:, cols].astype(F32)
            rr = lax.rsqrt(jnp.sum(p * p, axis=-1, keepdims=True) * (1.0 / HEAD_DIM) + EPS)
            n = p * rr
            dz = dhat * gain_ref[:, cols]
            dproj_ref[:, cols] = (rr * (dz - n * (jnp.sum(dz * n, axis=-1, keepdims=True) * (1.0 / HEAD_DIM)))
                                  ).astype(BF16)
            return jnp.sum(dhat * n, axis=0, keepdims=True)

        def group_sum(g0, d_ref, count, scale):
            acc = jnp.zeros((1, LANES), F32)
            for h in range(count):
                d = d_ref[:, h * LANES:(h + 1) * LANES]
                acc = acc + norm_bwd(g0 + h, d * scale if scale != 1.0 else d)
            return acc

        small_ref[0:1, :] += group_sum(G_QA, dqa_ref, A_HEADS, SCALE)
        small_ref[1:2, :] += group_sum(G_KA, dka_ref, A_KV_HEADS, 1.0)
        small_ref[2:3, :] += group_sum(G_QB, dqb_ref, B_HEADS, SCALE)
        small_ref[3:4, :] += group_sum(G_KB, dkb_ref, B_HEADS, LN2)
        dproj_ref[:, G_VA * LANES:G_VB * LANES] = dva_ref[...].astype(BF16)
        dproj_ref[:, G_VB * LANES:G_F * LANES] = dvb_ref[...].astype(BF16)

        dc = jnp.zeros((tm, LANES), F32)
        for piece_q, piece_k in zip(_split3(dqb_ref[...]), _split3(dkb_ref[...])):
            dc = dc + _dot(jnp.concatenate([piece_q, piece_k], axis=1), sel_ref[...])
        dlf = _dot_exact(triu_ref[...], dc) + carry_ref[...]
        carry_ref[...] += jnp.sum(dc, axis=0, keepdims=True)
        dz = dlf / (1.0 + jnp.exp(z_ref[...]))
        small_ref[4:5, :] += jnp.sum(dz, axis=0, keepdims=True)
        dproj_ref[:, G_F * LANES:(G_F + 1) * LANES] = dz.astype(BF16)
        dproj_ref[:, (G_F + 1) * LANES:] = jnp.zeros((tm, LANES), BF16)

    def rev(n):
        return pl.BlockSpec((tm, n), lambda i: (nt - 1 - i, 0))

    return _pcall(
        body, name="dproj", grid=(nt,),
        in_specs=[rev(N_NORM_GROUPS * LANES), rev(A_HEADS * LANES), rev(A_KV_HEADS * LANES), rev(B_HEADS * LANES),
                  rev(B_HEADS * LANES), rev(A_KV_HEADS * LANES), rev(B_HEADS * LANES), rev(LANES),
                  _const((1, NP)), _const((tm, tm)), _const(sel.shape)],
        out_specs=[rev(NP), _const((8, LANES))],
        out_shape=[jax.ShapeDtypeStruct((t_all, NP), BF16), jax.ShapeDtypeStruct((8, LANES), F32)],
        scratch_shapes=[pltpu.VMEM((1, LANES), F32)],
        compiler_params=_params(("arbitrary",)),
    )(pre, dqa, dka, dqb, dkb, dva, dvb, z, gain_row, triu, sel)


def _dx(dproj, w_pad_t, x2, dh, g1):
    t_all = x2.shape[0]
    tm = min(256, t_all)

    def body(a_ref, w_ref, x_ref, dh_ref, g_ref, dx_ref, dg_ref):
        @pl.when(pl.program_id(0) == 0)
        def _():
            dg_ref[...] = jnp.zeros_like(dg_ref)

        dxn = _dot(a_ref[...], w_ref[...])
        x = x_ref[...]
        r = lax.rsqrt(jnp.mean(x * x, axis=-1, keepdims=True) + EPS)
        xh = x * r
        dg_ref[...] += jnp.sum(dxn * xh, axis=0, keepdims=True)
        dz = dxn * g_ref[...]
        dx_ref[...] = dh_ref[...] + r * (dz - xh * jnp.mean(dz * xh, axis=-1, keepdims=True))

    return _pcall(
        body, name="dx", grid=(t_all // tm,),
        in_specs=[_rows(tm, NP), _const((NP, D_MODEL)), _rows(tm, D_MODEL), _rows(tm, D_MODEL), _const((1, D_MODEL))],
        out_specs=[_rows(tm, D_MODEL), _const((1, D_MODEL))],
        out_shape=[jax.ShapeDtypeStruct((t_all, D_MODEL), F32), jax.ShapeDtypeStruct((1, D_MODEL), F32)],
        compiler_params=_params(("arbitrary",)),
    )(dproj, w_pad_t, x2, dh, g1)


def _dwin(dproj, xn):
    t_all = xn.shape[0]
    tt = min(512, t_all)
    half = NP // 2

    def body(a_ref, b_ref, o_ref):
        @pl.when(pl.program_id(1) == 0)
        def _():
            o_ref[...] = jnp.zeros_like(o_ref)

        o_ref[...] += _dot_tn(a_ref[...], b_ref[...])

    return _pcall(
        body, name="dwin", grid=(2, t_all // tt),
        in_specs=[pl.BlockSpec((tt, half), lambda j, t: (t, j)), pl.BlockSpec((tt, D_MODEL), lambda j, t: (t, 0))],
        out_specs=pl.BlockSpec((half, D_MODEL), lambda j, t: (j, 0)),
        out_shape=jax.ShapeDtypeStruct((NP, D_MODEL), F32),
        compiler_params=_params(("parallel", "arbitrary")),
    )(dproj, xn)


ANY = pl.BlockSpec(memory_space=pl.ANY)


def _place():
    return lax.axis_index("x"), lax.axis_index("y"), lax.axis_index("c")


def _allgather_halves(mine):
    m_per, n = mine.shape

    def body(x_ref, out_ref, send_sems, recv_sems, local_sem):
        x, y, c = _place()
        me, sibling = (x, y, c), (x, y, 1 - c)
        chips = [(1 - x, y), (x, 1 - y), (1 - x, 1 - y)]

        def rows(px, py, pc):
            return out_ref.at[pl.ds((4 * px + 2 * py + pc) * m_per, m_per), :]

        def copy(k, block, to, src=None):
            return pltpu.make_async_remote_copy(
                src_ref=rows(*block) if src is None else src, dst_ref=rows(*block),
                send_sem=send_sems.at[k], recv_sem=recv_sems.at[k], device_id=to, device_id_type=MESH)

        own = pltpu.make_async_copy(x_ref, rows(*me), local_sem)
        own.start()
        first = [copy(0, me, sibling, src=x_ref)]
        first += [copy(1 + j, me, (*chip, c), src=x_ref) for j, chip in enumerate(chips)]
        for cp in first:
            cp.start()
        passed = [copy(4 + j, (*chip, c), sibling) for j, chip in enumerate(chips)]
        for j, chip in enumerate(chips):
            copy(1 + j, (*chip, c), me).wait_recv()
            passed[j].start()
        copy(0, sibling, me).wait_recv()
        for j, chip in enumerate(chips):
            copy(4 + j, (*chip, 1 - c), me).wait_recv()
        for cp in first + passed:
            cp.wait_send()
        own.wait()

    return _pcall(
        body, name="allgather_weights",
        out_shape=jax.ShapeDtypeStruct((8 * m_per, n), mine.dtype),
        in_specs=[ANY], out_specs=ANY,
        scratch_shapes=[pltpu.SemaphoreType.DMA((7,)), pltpu.SemaphoreType.DMA((7,)), pltpu.SemaphoreType.DMA],
    )(mine)


def _rs_pair_exchange(g4):
    def body(g_ref, out_ref, send_sem, recv_sem):
        x, y, c = _place()
        cp = pltpu.make_async_remote_copy(
            src_ref=g_ref.at[:, 1 - c], dst_ref=out_ref, send_sem=send_sem, recv_sem=recv_sem,
            device_id=(x, y, 1 - c), device_id_type=MESH)
        cp.start()
        cp.wait()

    return _pcall(
        body, name="rs_pair_exchange",
        out_shape=jax.ShapeDtypeStruct((N_CHIPS, R_HALF, D_MODEL), F32),
        in_specs=[ANY], out_specs=ANY,
        scratch_shapes=[pltpu.SemaphoreType.DMA, pltpu.SemaphoreType.DMA],
    )(g4)


def _rs_pair_add(g4, got, c_idx):
    def body(c_ref, a_ref, b_ref, o_ref, ob_ref):
        pair = a_ref[...] + b_ref[...]
        o_ref[...] = pair
        ob_ref[...] = pair.astype(BF16)

    blk = pl.BlockSpec((None, R_HALF, D_MODEL), lambda s, c_ref: (s, 0, 0))
    return _pcall(
        body, name="rs_pair_add",
        grid_spec=pltpu.PrefetchScalarGridSpec(
            num_scalar_prefetch=1, grid=(N_CHIPS,),
            in_specs=[pl.BlockSpec((None, None, R_HALF, D_MODEL), lambda s, c_ref: (s, c_ref[0], 0, 0)), blk],
            out_specs=[blk, blk]),
        out_shape=[jax.ShapeDtypeStruct((N_CHIPS, R_HALF, D_MODEL), F32),
                   jax.ShapeDtypeStruct((N_CHIPS, R_HALF, D_MODEL), BF16)],
        compiler_params=_params(("parallel",)),
    )(c_idx, g4, got)


def _rs_chip_exchange(p4):
    def body(p_ref, out_ref, send_sems, recv_sems):
        x, y, c = _place()
        chips = [(1 - x, y), (x, 1 - y), (1 - x, 1 - y)]
        cps = [pltpu.make_async_remote_copy(
            src_ref=p_ref.at[2 * cx + cy], dst_ref=out_ref.at[j], send_sem=send_sems.at[j],
            recv_sem=recv_sems.at[j], device_id=(cx, cy, c), device_id_type=MESH)
            for j, (cx, cy) in enumerate(chips)]
        for cp in cps:
            cp.start()
        for cp in cps:
            cp.wait()

    return _pcall(
        body, name="rs_chip_exchange",
        out_shape=jax.ShapeDtypeStruct((3, R_HALF, D_MODEL), p4.dtype),
        in_specs=[ANY], out_specs=ANY,
        scratch_shapes=[pltpu.SemaphoreType.DMA((3,)), pltpu.SemaphoreType.DMA((3,))],
    )(p4)


def _rs_chip_add(p4, got, sc_idx):
    tr = R_HALF // 7

    def body(sc_ref, a_ref, b_ref, o_ref):
        o_ref[...] = ((a_ref[...] + b_ref[0].astype(F32)) + b_ref[1].astype(F32)) + b_ref[2].astype(F32)

    return _pcall(
        body, name="rs_chip_add",
        grid_spec=pltpu.PrefetchScalarGridSpec(
            num_scalar_prefetch=1, grid=(R_HALF // tr,),
            in_specs=[pl.BlockSpec((None, tr, D_MODEL), lambda i, sc_ref: (sc_ref[0], i, 0)),
                      pl.BlockSpec((3, tr, D_MODEL), lambda i, sc_ref: (0, i, 0))],
            out_specs=pl.BlockSpec((None, tr, D_MODEL), lambda i, sc_ref: (sc_ref[1], i, 0))),
        out_shape=jax.ShapeDtypeStruct((2, R_HALF, D_MODEL), F32),
        compiler_params=_params(("parallel",)),
    )(sc_idx, p4, got)


def _rs_pair_share(halves):
    def body(r_ref, out_ref, send_sem, recv_sem):
        x, y, c = _place()
        cp = pltpu.make_async_remote_copy(
            src_ref=r_ref.at[c], dst_ref=out_ref.at[c], send_sem=send_sem, recv_sem=recv_sem,
            device_id=(x, y, 1 - c), device_id_type=MESH)
        cp.start()
        cp.wait()

    return _pcall(
        body, name="rs_pair_share",
        out_shape=jax.ShapeDtypeStruct((2, R_HALF, D_MODEL), F32),
        in_specs=[ANY], out_specs=ANY, input_output_aliases={0: 0},
        scratch_shapes=[pltpu.SemaphoreType.DMA, pltpu.SemaphoreType.DMA],
    )(halves)


def _adam(w, g, m, v):
    m2 = ADAM_B1 * m + (1.0 - ADAM_B1) * g
    v2 = ADAM_B2 * v + (1.0 - ADAM_B2) * (g * g)
    m_hat = m2 / (1.0 - ADAM_B1 ** ADAM_STEP)
    v_hat = v2 / (1.0 - ADAM_B2 ** ADAM_STEP)
    return -ADAM_LR * (m_hat / (jnp.sqrt(v_hat) + ADAM_EPS) + ADAM_WD * w), m2, v2


def _small_allreduce_adamw(part, w, m, v):
    def body(p_ref, w_ref, m_ref, v_ref, g_ref, d_ref, m2_ref, v2_ref, buf, send_sems, recv_sems):
        x, y, c = _place()
        me = 4 * x + 2 * y + c
        cps = []
        for k in range(1, 8):
            peer = (1 - x if k & 4 else x, 1 - y if k & 2 else y, 1 - c if k & 1 else c)
            cps.append(pltpu.make_async_remote_copy(
                src_ref=p_ref, dst_ref=buf.at[me], send_sem=send_sems.at[k - 1], recv_sem=recv_sems.at[k - 1],
                device_id=peer, device_id_type=MESH))
        for cp in cps:
            cp.start()
        buf[me] = p_ref[...]
        for cp in cps:
            cp.wait()
        g = buf[0]
        for k in range(1, 8):
            g = g + buf[k]
        g_ref[...] = g
        d_ref[...], m2_ref[...], v2_ref[...] = _adam(w_ref[...], g, m_ref[...], v_ref[...])

    vm = pl.BlockSpec(memory_space=pltpu.VMEM)
    shp = jax.ShapeDtypeStruct((SMALL_ROWS, LANES), F32)
    return _pcall(
        body, name="small_allreduce_adamw",
        out_shape=[shp, shp, shp, shp], in_specs=[vm, vm, vm, vm], out_specs=[vm, vm, vm, vm],
        scratch_shapes=[pltpu.VMEM((8, SMALL_ROWS, LANES), F32), pltpu.SemaphoreType.DMA((7,)),
                        pltpu.SemaphoreType.DMA((7,))],
    )(part, w, m, v)


def _adamw(w, g, m, v, name):
    rows, cols = w.shape
    tr = min(256, rows)

    def body(w_ref, g_ref, m_ref, v_ref, d_ref, m2_ref, v2_ref):
        d_ref[...], m2_ref[...], v2_ref[...] = _adam(w_ref[...], g_ref[...], m_ref[...], v_ref[...])

    spec = _rows(tr, cols)
    shp = jax.ShapeDtypeStruct((rows, cols), F32)
    return _pcall(
        body, name=name, grid=(rows // tr,), in_specs=[spec] * 4, out_specs=[spec] * 3, out_shape=[shp] * 3,
        compiler_params=_params(("parallel",)),
    )(w, g, m, v)


def _pad_lanes(v):
    return jnp.pad(v, (0, LANES - v.shape[0]))


def _pad_head_rows(w_t, heads):
    n = w_t.shape[1]
    return jnp.pad(w_t.reshape(heads, HEAD_DIM, n), ((0, 0), (0, LANES - HEAD_DIM), (0, 0))).reshape(heads * LANES, n)


def _unpad_head_rows(w_t, heads):
    n = w_t.shape[1]
    return w_t.reshape(heads, LANES, n)[:, :HEAD_DIM].reshape(heads * HEAD_DIM, n)


def _in_rows_pad(w_in_t):
    qa, ka, va, qb, kb, vb, f = jnp.split(w_in_t, [512, 640, 768, 1280, 1792, 2304], axis=0)
    f = jnp.pad(f, ((0, 2 * LANES - B_HEADS), (0, 0)))
    return jnp.concatenate([_pad_head_rows(qa, 8), _pad_head_rows(ka, 2), _pad_head_rows(qb, 8),
                            _pad_head_rows(kb, 8), _pad_head_rows(va, 2), _pad_head_rows(vb, 8), f], axis=0)


def _in_rows_unpad(d):
    qa = _unpad_head_rows(d[G_QA * LANES:G_KA * LANES], 8)
    ka = _unpad_head_rows(d[G_KA * LANES:G_QB * LANES], 2)
    qb = _unpad_head_rows(d[G_QB * LANES:G_KB * LANES], 8)
    kb = _unpad_head_rows(d[G_KB * LANES:G_VA * LANES], 8)
    va = _unpad_head_rows(d[G_VA * LANES:G_VB * LANES], 2)
    vb = _unpad_head_rows(d[G_VB * LANES:G_F * LANES], 8)
    f = d[G_F * LANES:G_F * LANES + B_HEADS]
    return jnp.concatenate([qa, ka, va, qb, kb, vb, f], axis=0)


def _pack_small(g1, bf, qa, ka, sk, qb, kb, g2, loss_row):
    rows = [g1.reshape(8, LANES), g2.reshape(8, LANES)]
    rows += [_pad_lanes(t)[None] for t in (qa, ka, qb, kb, bf, sk)]
    rows += [loss_row, jnp.zeros((1, LANES), F32)]
    return jnp.concatenate(rows, axis=0)


def _unpack_small(p):
    return (p[0:8].reshape(D_MODEL), p[20, :B_HEADS], p[16, :HEAD_DIM], p[17, :HEAD_DIM], p[21, :A_HEADS],
            p[18, :HEAD_DIM], p[19, :HEAD_DIM], p[8:16].reshape(D_MODEL))


def kernel(x, attn_norm_g, w_in, b_forget, q_norm_a, k_norm_a, sink_logits, q_norm_b, k_norm_b, w_out, mlp_norm_g, w_up, w_down, loss_target, m_attn_norm_g, m_w_in, m_b_forget, m_q_norm_a, m_k_norm_a, m_sink_logits, m_q_norm_b, m_k_norm_b, m_w_out, m_mlp_norm_g, m_w_up, m_w_down, v_attn_norm_g, v_w_in, v_b_forget, v_q_norm_a, v_k_norm_a, v_sink_logits, v_q_norm_b, v_k_norm_b, v_w_out, v_mlp_norm_g, v_w_up, v_w_down):
    nb, seq, _ = x.shape
    t_all = nb * seq
    c_idx = lax.axis_index("c")
    s_idx = 2 * lax.axis_index("x") + lax.axis_index("y")

    packed = jnp.concatenate([jnp.pad(w_in.T, ((0, IN_SHARD_P - IN_SHARD), (0, 0))), w_out, w_up, w_down], axis=0)
    mine = lax.dynamic_slice_in_dim(packed.astype(BF16).reshape(2, R_HALF, D_MODEL), c_idx, 1, axis=0)[0]
    gathered = _allgather_halves(mine).reshape(N_CHIPS, R_ALL, D_MODEL)
    w_in_t = gathered[:, :IN_SHARD].reshape(IN_WIDTH, D_MODEL)
    w_pad_t = _in_rows_pad(w_in_t)
    w_pad = w_pad_t.T
    wo_pad = _pad_head_rows(gathered[:, R_OUT:R_UP].reshape(D_MODEL, D_MODEL), A_HEADS + B_HEADS)
    wo_pad_t = wo_pad.T
    w_up_blocks = gathered[:, R_UP:R_DOWN]
    w_up_t = jnp.swapaxes(w_up_blocks, 1, 2).reshape(D_FF, D_MODEL)
    w_down_f = gathered[:, R_DOWN:].reshape(D_FF, D_MODEL)
    w_down_t = w_down_f.T

    ones = jnp.ones((LANES,), F32)
    gain_row = jnp.concatenate(
        [jnp.tile(_pad_lanes(q_norm_a), 8), jnp.tile(_pad_lanes(k_norm_a), 2), jnp.tile(_pad_lanes(q_norm_b), 8),
         jnp.tile(_pad_lanes(k_norm_b), 8), jnp.tile(ones, N_GROUPS - N_NORM_GROUPS)])[None]
    b_row = _pad_lanes(b_forget)[None]
    g1 = attn_norm_g[None]
    g2 = mlp_norm_g[None]
    slopes = jnp.exp2(-(8.0 / A_HEADS) * (jnp.arange(A_HEADS, dtype=F32) + 1.0))

    x2 = x.reshape(t_all, D_MODEL)
    tgt = loss_target.reshape(t_all, D_MODEL)

    xn, pre, qa, ka, va, qb, kb, vb, z = _inproj(x2, g1, w_pad, gain_row, b_row, seq)
    swa_bias = _swa_bias(slopes)
    oa, la = _swa_fwd(qa, ka, va, sink_logits, swa_bias, nb, seq)
    ob, lse = _fox_fwd(qb, kb, vb, nb, seq)
    h, hn = _outproj(x2, oa, ob, wo_pad, g2)
    ru, dy, dyb, loss_acc = _mlp_fwd(hn, w_up_blocks, w_down_f, h, tgt)

    du, d_w_down, d_w_up = _mlp_bwd_w(dyb, w_down_t, ru, hn)
    dh, dhb, d_g2 = _mlp_dhn(du, w_up_t, h, dy, g2)
    doa, dob, delta_b, d_wo = _dmixed(dhb, wo_pad_t, oa, ob)
    dqb, dkb, dvb = _fox_bwd(qb, kb, vb, dob, lse, delta_b, nb, seq)
    dqa, dka, dva, dsink = _swa_bwd(qa, ka, va, oa, doa, la, sink_logits, swa_bias, nb, seq)
    dproj, small = _dproj(pre, dqa, dka, dqb, dkb, dva, dvb, z, gain_row, seq)
    grad_x, d_g1 = _dx(dproj, w_pad_t, x2, dh, g1)
    d_w_in_t = _dwin(dproj, xn)

    d_w_out = _unpad_head_rows(d_wo, A_HEADS + B_HEADS)
    g_pack = jnp.concatenate([
        jnp.pad(_in_rows_unpad(d_w_in_t).reshape(N_CHIPS, IN_SHARD, D_MODEL),
                ((0, 0), (0, IN_SHARD_P - IN_SHARD), (0, 0))),
        d_w_out.reshape(N_CHIPS, D_MODEL // N_CHIPS, D_MODEL), d_w_up, d_w_down], axis=1)
    g4 = g_pack.reshape(N_CHIPS, 2, R_HALF, D_MODEL)
    pair, pair_bf = _rs_pair_add(g4, _rs_pair_exchange(g4), c_idx.reshape(1).astype(jnp.int32))
    halves = _rs_chip_add(pair, _rs_chip_exchange(pair_bf), jnp.stack([s_idx, c_idx]).astype(jnp.int32))
    red = _rs_pair_share(halves).reshape(R_ALL, D_MODEL)
    g_w_in = red[:IN_SHARD].T
    g_w_out = red[R_OUT:R_UP]
    g_w_up = red[R_UP:R_DOWN]
    g_w_down = red[R_DOWN:]

    loss_row = loss_acc[0:1] * (0.5 / D_MODEL)
    d_sink = dsink[:, :A_GROUP, 0].reshape(nb, A_HEADS).sum(axis=0)
    part = _pack_small(d_g1[0], small[4, :B_HEADS], small[0, :HEAD_DIM], small[1, :HEAD_DIM], d_sink,
                       small[2, :HEAD_DIM], small[3, :HEAD_DIM], d_g2[0], loss_row)
    zero_row = jnp.zeros((1, LANES), F32)
    smalls = lambda t: _pack_small(*t, zero_row)
    w_small = smalls((attn_norm_g, b_forget, q_norm_a, k_norm_a, sink_logits, q_norm_b, k_norm_b, mlp_norm_g))
    m_small = smalls((m_attn_norm_g, m_b_forget, m_q_norm_a, m_k_norm_a, m_sink_logits, m_q_norm_b, m_k_norm_b,
                      m_mlp_norm_g))
    v_small = smalls((v_attn_norm_g, v_b_forget, v_q_norm_a, v_k_norm_a, v_sink_logits, v_q_norm_b, v_k_norm_b,
                      v_mlp_norm_g))
    g_s, d_s, m_s, v_s = _small_allreduce_adamw(part, w_small, m_small, v_small)
    loss = g_s[ROW_LOSS, 0]

    big = {}
    for name, w, g, m, v in (("adamw_w_in", w_in, g_w_in, m_w_in, v_w_in),
                             ("adamw_w_out", w_out, g_w_out, m_w_out, v_w_out),
                             ("adamw_w_up", w_up, g_w_up, m_w_up, v_w_up),
                             ("adamw_w_down", w_down, g_w_down, m_w_down, v_w_down)):
        big[name] = (g,) + tuple(_adamw(w, g, m, v, name))

    def assemble(k, small_pack):
        s = _unpack_small(small_pack)
        return (s[0], big["adamw_w_in"][k], s[1], s[2], s[3], s[4], s[5], s[6], big["adamw_w_out"][k], s[7],
                big["adamw_w_up"][k], big["adamw_w_down"][k])

    return (loss, grad_x.reshape(nb, seq, D_MODEL), *assemble(0, g_s), *assemble(1, d_s), *assemble(2, m_s),
            *assemble(3, v_s))
```

```python
import functools

import numpy as np
import jax
import jax.numpy as jnp
from jax import lax
from jax.experimental import pallas as pl
from jax.experimental.pallas import tpu as pltpu

F32 = jnp.float32
BF16 = jnp.bfloat16

D_MODEL = 1024
HEAD_DIM = 64
LANES = 128
A_HEADS = 8
A_KV_HEADS = 2
A_GROUP = A_HEADS // A_KV_HEADS
B_HEADS = 8
WINDOW = 128
D_FF = 4096
IN_WIDTH = 2312
EPS = 1e-6
SCALE = 0.125
LOG2E = 1.4426950408889634
LN2 = 0.6931471805599453
CHUNK = 32
NEG = -1e30

G_QA, G_KA, G_QB, G_KB, G_VA, G_VB, G_F = 0, 8, 10, 18, 26, 28, 36
N_NORM_GROUPS = 26
N_GROUPS = 38
NP = N_GROUPS * LANES
MIXED_P = (A_HEADS + B_HEADS) * LANES

N_CHIPS = 4
IN_SHARD = IN_WIDTH // N_CHIPS
IN_SHARD_P = 608
R_OUT = IN_SHARD_P
R_UP = R_OUT + D_MODEL // N_CHIPS
R_DOWN = R_UP + D_MODEL
R_ALL = R_DOWN + D_FF // N_CHIPS
R_HALF = R_ALL // 2

SMALL_ROWS = 24
ROW_LOSS = 22

ADAM_LR = 0.001
ADAM_B1 = 0.9
ADAM_B2 = 0.999
ADAM_EPS = 1e-08
ADAM_WD = 0.01
ADAM_STEP = 10

VMEM_LIMIT = 52 * 1024 * 1024
MESH = pl.DeviceIdType.MESH


def _pcall(body, **kw):
    return pl.pallas_call(body, **kw)


def _params(sem=None):
    return pltpu.CompilerParams(dimension_semantics=sem, vmem_limit_bytes=VMEM_LIMIT)


def _dot(a, b):
    return jnp.dot(a, b, preferred_element_type=F32)


def _dot_nt(a, b):
    return lax.dot_general(a, b, (((1,), (1,)), ((), ())), preferred_element_type=F32)


def _dot_tn(a, b):
    return lax.dot_general(a, b, (((0,), (0,)), ((), ())), preferred_element_type=F32)


def _split3(x):
    hi = x.astype(BF16)
    r1 = x - hi.astype(F32)
    mid = r1.astype(BF16)
    lo = (r1 - mid.astype(F32)).astype(BF16)
    return hi, mid, lo


def _dot_exact(mat, x):
    hi, mid, lo = _split3(x)
    return _dot(mat, lo) + _dot(mat, mid) + _dot(mat, hi)


def _const(shape):
    zeros = (0,) * len(shape)
    return pl.BlockSpec(shape, lambda *_: zeros)


def _rows(tm, n):
    return pl.BlockSpec((tm, n), lambda i: (i, 0))


def _aug_select():
    e = np.zeros((3 * LANES, 2 * B_HEADS * LANES), np.float32)
    for j in range(3):
        for h in range(B_HEADS):
            e[j * LANES + h, h * LANES + HEAD_DIM + j] = 1.0
            e[j * LANES + h, (B_HEADS + h) * LANES + HEAD_DIM + 3 + j] = -1.0
    return jnp.asarray(e, BF16)


def _dc_select():
    e = np.zeros((2 * B_HEADS * LANES, LANES), np.float32)
    for h in range(B_HEADS):
        e[h * LANES + HEAD_DIM, h] = 1.0
        e[(B_HEADS + h) * LANES + HEAD_DIM + 3, h] = -1.0
    return jnp.asarray(e, BF16)


def _tri(n, upper):
    t = np.tril(np.ones((n, n), np.float32))
    return jnp.asarray(t.T if upper else t, BF16)


def _inproj(x2, g1, w_pad_t, gain_row, b_row, seq):
    t_all = x2.shape[0]
    tm = min(256, seq)
    tiles_per_seq = seq // tm
    tri = _tri(tm, False)
    esel = _aug_select()

    def body(x_ref, g_ref, w_ref, gain_ref, b_ref, tri_ref, e_ref,
             xn_ref, pre_ref, qa_ref, ka_ref, va_ref, qb_ref, kb_ref, vb_ref, z_ref, carry_ref):
        i = pl.program_id(0)

        @pl.when(i % tiles_per_seq == 0)
        def _():
            carry_ref[...] = jnp.zeros_like(carry_ref)

        x = x_ref[...]
        r = lax.rsqrt(jnp.mean(x * x, axis=-1, keepdims=True) + EPS)
        xn = (x * r * g_ref[...]).astype(BF16)
        xn_ref[...] = xn
        proj = _dot_nt(xn, w_ref[...])
        pre_ref[...] = proj[:, :N_NORM_GROUPS * LANES].astype(BF16)
        lane = lax.broadcasted_iota(jnp.int32, (tm, LANES), 1)

        z = proj[:, G_F * LANES:(G_F + 1) * LANES] + b_ref[...]
        z_ref[...] = z
        lf = jnp.minimum(z, 0.0) - jnp.log(1.0 + jnp.exp(-jnp.abs(z)))
        lf = jnp.where(lane < B_HEADS, lf, 0.0)
        c = _dot_exact(tri_ref[...], lf) + carry_ref[...]
        carry_ref[...] += jnp.sum(lf, axis=0, keepdims=True)
        aug = _dot(jnp.concatenate(_split3(c * LOG2E), axis=1), e_ref[...])

        def hnorm(g):
            p = proj[:, g * LANES:(g + 1) * LANES]
            rr = lax.rsqrt(jnp.sum(p * p, axis=-1, keepdims=True) * (1.0 / HEAD_DIM) + EPS)
            return p * rr * gain_ref[:, g * LANES:(g + 1) * LANES]

        ones_q = jnp.where((lane >= HEAD_DIM + 3) & (lane < HEAD_DIM + 6), 1.0, 0.0)
        ones_k = jnp.where((lane >= HEAD_DIM) & (lane < HEAD_DIM + 3), 1.0, 0.0)
        for h in range(A_HEADS):
            qa_ref[:, h * LANES:(h + 1) * LANES] = (hnorm(G_QA + h) * SCALE).astype(BF16)
        for h in range(A_KV_HEADS):
            ka_ref[:, h * LANES:(h + 1) * LANES] = hnorm(G_KA + h).astype(BF16)
        for h in range(B_HEADS):
            qb_ref[:, h * LANES:(h + 1) * LANES] = (
                hnorm(G_QB + h) * (SCALE * LOG2E) + aug[:, h * LANES:(h + 1) * LANES] + ones_q).astype(BF16)
            kb_ref[:, h * LANES:(h + 1) * LANES] = (
                hnorm(G_KB + h) + aug[:, (B_HEADS + h) * LANES:(B_HEADS + h + 1) * LANES] + ones_k).astype(BF16)
        va_ref[...] = proj[:, G_VA * LANES:G_VB * LANES].astype(BF16)
        one_v = jnp.where(lane == HEAD_DIM, 1.0, 0.0)
        for h in range(B_HEADS):
            cols = slice((G_VB + h) * LANES, (G_VB + h + 1) * LANES)
            vb_ref[:, h * LANES:(h + 1) * LANES] = (proj[:, cols] + one_v).astype(BF16)

    widths = [(D_MODEL, BF16), (N_NORM_GROUPS * LANES, BF16), (A_HEADS * LANES, BF16), (A_KV_HEADS * LANES, BF16),
              (A_KV_HEADS * LANES, BF16), (B_HEADS * LANES, BF16), (B_HEADS * LANES, BF16), (B_HEADS * LANES, BF16),
              (LANES, F32)]
    return _pcall(
        body, name="inproj", grid=(t_all // tm,),
        in_specs=[_rows(tm, D_MODEL), _const((1, D_MODEL)), _const((NP, D_MODEL)), _const((1, NP)),
                  _const((1, LANES)), _const((tm, tm)), _const(esel.shape)],
        out_specs=[_rows(tm, w) for w, _ in widths],
        out_shape=[jax.ShapeDtypeStruct((t_all, w), dt) for w, dt in widths],
        scratch_shapes=[pltpu.VMEM((1, LANES), F32)],
        compiler_params=_params(("arbitrary",)),
    )(x2, g1, w_pad_t, gain_row, b_row, tri, esel)


def _fox_fwd(qb, kb, vb, nb, seq):
    t_all = qb.shape[0]
    tq = min(512, seq)
    nq = seq // tq

    def body(q_ref, k_ref, v_ref, o_ref, lse_ref, s_ref, p_ref, m_ref, alpha_ref, acc_ref):
        qi = pl.program_id(2)
        q = q_ref[...]
        hq = tq // 2
        m_ref[...] = jnp.full((tq, LANES), NEG, F32)
        acc_ref[...] = jnp.zeros((tq, LANES), F32)

        def step(j, masked):
            off = pl.multiple_of(j * tq, tq)
            k = k_ref[pl.ds(off, tq), :]
            v = v_ref[pl.ds(off, tq), :]
            for hf in range(2):
                s_ref[hf] = _dot_nt(q[hf * hq:(hf + 1) * hq], k)
            for hf in range(2):
                for r in range(0, hq, CHUNK):
                    rows = slice(r, r + CHUNK)
                    grows = slice(hf * hq + r, hf * hq + r + CHUNK)
                    tiles = []
                    for jt in range(tq // LANES):
                        sc = s_ref[hf, rows, jt * LANES:(jt + 1) * LANES]
                        if masked:
                            row = hf * hq + r + lax.broadcasted_iota(jnp.int32, (CHUNK, LANES), 0)
                            col = jt * LANES + lax.broadcasted_iota(jnp.int32, (CHUNK, LANES), 1)
                            sc = jnp.where(row >= col, sc, NEG)
                        tiles.append(sc)
                    m_prev = m_ref[grows, :]
                    m_cur = functools.reduce(jnp.maximum, tiles)
                    m_new = jnp.maximum(m_prev, jnp.max(m_cur, axis=-1, keepdims=True))
                    m_ref[grows, :] = m_new
                    alpha_ref[grows, :] = jnp.exp2(m_prev - m_new)
                    for jt, sc in enumerate(tiles):
                        p_ref[hf, rows, jt * LANES:(jt + 1) * LANES] = jnp.exp2(sc - m_new).astype(BF16)
                hrows = slice(hf * hq, (hf + 1) * hq)
                acc_ref[hrows, :] = alpha_ref[hrows, :] * acc_ref[hrows, :] + _dot(p_ref[hf], v)

        def unmasked(j, carry):
            step(j, False)
            return carry

        lax.fori_loop(0, qi, unmasked, 0)
        step(qi, True)
        acc = acc_ref[...]
        lane = lax.broadcasted_iota(jnp.int32, (tq, LANES), 1)
        l = jnp.sum(jnp.where(lane == HEAD_DIM, acc, 0.0), axis=-1, keepdims=True)
        o_ref[...] = (acc / l).astype(BF16)
        lse_ref[...] = m_ref[...] + jnp.log2(l)

    qspec = pl.BlockSpec((tq, LANES), lambda b, h, i: (b * nq + i, h))
    kspec = pl.BlockSpec((seq, LANES), lambda b, h, i: (b, h))
    return _pcall(
        body, name="fox_fwd", grid=(nb, B_HEADS, nq),
        in_specs=[qspec, kspec, kspec], out_specs=[qspec, qspec],
        out_shape=[jax.ShapeDtypeStruct((t_all, B_HEADS * LANES), BF16),
                   jax.ShapeDtypeStruct((t_all, B_HEADS * LANES), F32)],
        scratch_shapes=[pltpu.VMEM((2, tq // 2, tq), F32), pltpu.VMEM((2, tq // 2, tq), BF16),
                        pltpu.VMEM((tq, LANES), F32),
                        pltpu.VMEM((tq, LANES), F32), pltpu.VMEM((tq, LANES), F32)],
        compiler_params=_params(("parallel", "parallel", "arbitrary")),
    )(qb, kb, vb)


def _swa_bias(slopes):
    row = jnp.arange(A_GROUP * WINDOW, dtype=jnp.int32)[:, None] % WINDOW
    col = jnp.arange(2 * WINDOW, dtype=jnp.int32)[None, :]
    slope_rows = jnp.repeat(slopes.reshape(A_KV_HEADS, A_GROUP), WINDOW, axis=1)[:, :, None]
    out = []
    for t_rel in (0, WINDOW):
        dist = t_rel + row - col
        valid = (dist >= 0) & (dist < WINDOW)
        out.append(jnp.where(valid[None], -slope_rows * dist.astype(F32)[None], NEG))
    return jnp.stack(out)


def _stack_heads(ref, rows):
    return jnp.concatenate([ref[rows, j * LANES:(j + 1) * LANES] for j in range(A_GROUP)], axis=0)


def _sink_column(sink_ref, g):
    return jnp.concatenate([jnp.full((WINDOW, 1), sink_ref[g * A_GROUP + j], F32) for j in range(A_GROUP)], axis=0)


def _swa_specs(nq, tq, seq):
    smem = pl.BlockSpec(memory_space=pltpu.SMEM)
    qspec = pl.BlockSpec((tq, A_GROUP * LANES), lambda b, g, i: (b * nq + i, g))
    kspec = pl.BlockSpec((seq, LANES), lambda b, g, i: (b, g))
    bias_first = pl.BlockSpec((None, None, A_GROUP * WINDOW, 2 * WINDOW),
                              lambda b, g, i: (jnp.minimum(i, 1), g, 0, 0))
    bias_rest = pl.BlockSpec((None, None, A_GROUP * WINDOW, 2 * WINDOW), lambda b, g, i: (1, g, 0, 0))
    return smem, qspec, kspec, bias_first, bias_rest


def _swa_fwd(qa, ka, va, sinks, bias, nb, seq):
    t_all = qa.shape[0]
    tq = min(512, seq)
    nq = seq // tq

    def body(sink_ref, q_ref, k_ref, v_ref, bias0_ref, bias_ref, o_ref, l_ref):
        qi = pl.program_id(2)
        sink = _sink_column(sink_ref, pl.program_id(1))
        for a in range(tq // WINDOW):
            t0 = qi * tq + a * WINDOW
            start = pl.multiple_of(jnp.maximum(t0 - WINDOW, 0), WINDOW)
            rows = slice(a * WINDOW, (a + 1) * WINDOW)
            k = k_ref[pl.ds(start, 2 * WINDOW), :]
            v = v_ref[pl.ds(start, 2 * WINDOW), :]
            s = _dot_nt(_stack_heads(q_ref, rows), k) + (bias0_ref if a == 0 else bias_ref)[...]
            m = jnp.maximum(jnp.max(s, axis=-1, keepdims=True), sink)
            p = jnp.exp(s - m)
            den = jnp.sum(p, axis=-1, keepdims=True) + jnp.exp(sink - m)
            o = _dot((p / den).astype(BF16), v).astype(BF16)
            lrow = jnp.broadcast_to(m + jnp.log(den), (A_GROUP * WINDOW, LANES))
            for j in range(A_GROUP):
                o_ref[rows, j * LANES:(j + 1) * LANES] = o[j * WINDOW:(j + 1) * WINDOW]
                l_ref[rows, j * LANES:(j + 1) * LANES] = lrow[j * WINDOW:(j + 1) * WINDOW]

    smem, qspec, kspec, bias_first, bias_rest = _swa_specs(nq, tq, seq)
    return _pcall(
        body, name="swa_fwd", grid=(nb, A_KV_HEADS, nq),
        in_specs=[smem, qspec, kspec, kspec, bias_first, bias_rest], out_specs=[qspec, qspec],
        out_shape=[jax.ShapeDtypeStruct((t_all, A_HEADS * LANES), BF16),
                   jax.ShapeDtypeStruct((t_all, A_HEADS * LANES), F32)],
        compiler_params=_params(("parallel", "parallel", "arbitrary")),
    )(sinks, qa, ka, va, bias, bias)


def _outproj(x2, oa, ob, wo_pad, g2):
    t_all = x2.shape[0]
    tm = min(512, t_all)
    half = A_HEADS * LANES

    def body(x_ref, oa_ref, ob_ref, w_ref, g_ref, h_ref, hn_ref):
        h = x_ref[...] + _dot(oa_ref[...], w_ref[:half, :]) + _dot(ob_ref[...], w_ref[half:, :])
        h_ref[...] = h
        r = lax.rsqrt(jnp.mean(h * h, axis=-1, keepdims=True) + EPS)
        hn_ref[...] = (h * r * g_ref[...]).astype(BF16)

    return _pcall(
        body, name="outproj", grid=(t_all // tm,),
        in_specs=[_rows(tm, D_MODEL), _rows(tm, half), _rows(tm, half), _const((MIXED_P, D_MODEL)),
                  _const((1, D_MODEL))],
        out_specs=[_rows(tm, D_MODEL), _rows(tm, D_MODEL)],
        out_shape=[jax.ShapeDtypeStruct((t_all, D_MODEL), F32), jax.ShapeDtypeStruct((t_all, D_MODEL), BF16)],
        compiler_params=_params(("parallel",)),
    )(x2, oa, ob, wo_pad, g2)


def _mlp_fwd(hn, w_up_blocks, w_down, h, tgt):
    t_all = h.shape[0]
    tm = min(256, t_all)
    nj = D_FF // D_MODEL

    def body(a_ref, wu_ref, wd_ref, h_ref, t_ref, ru_ref, dy_ref, dyb_ref, loss_ref):
        @pl.when(pl.program_id(0) == 0)
        def _():
            loss_ref[...] = jnp.zeros_like(loss_ref)

        a = a_ref[...]
        y = h_ref[...]
        for j in range(nj):
            cols = slice(j * D_MODEL, (j + 1) * D_MODEL)
            ru = jnp.maximum(_dot(a, wu_ref[j]), 0.0)
            ru_ref[:, cols] = ru.astype(BF16)
            y = y + _dot((ru * ru).astype(BF16), wd_ref[cols, :])
        err = y - t_ref[...]
        loss_ref[...] += jnp.sum(err * err)
        dy = err * (1.0 / D_MODEL)
        dy_ref[...] = dy
        dyb_ref[...] = dy.astype(BF16)

    return _pcall(
        body, name="mlp_fwd", grid=(t_all // tm,),
        in_specs=[_rows(tm, D_MODEL), _const((nj, D_MODEL, D_MODEL)), _const((D_FF, D_MODEL)), _rows(tm, D_MODEL),
                  _rows(tm, D_MODEL)],
        out_specs=[_rows(tm, D_FF), _rows(tm, D_MODEL), _rows(tm, D_MODEL), _const((8, LANES))],
        out_shape=[jax.ShapeDtypeStruct((t_all, D_FF), BF16), jax.ShapeDtypeStruct((t_all, D_MODEL), F32),
                   jax.ShapeDtypeStruct((t_all, D_MODEL), BF16), jax.ShapeDtypeStruct((8, LANES), F32)],
        compiler_params=_params(("arbitrary",)),
    )(hn, w_up_blocks, w_down, h, tgt)


def _mlp_bwd_w(dyb, w_down, ru, hn):
    t_all = dyb.shape[0]
    tm = min(512, t_all)
    nj = D_FF // D_MODEL

    def body(dy_ref, w_ref, ru_ref, hn_ref, du_ref, dwd_ref, dwu_ref):
        @pl.when(pl.program_id(1) == 0)
        def _():
            dwd_ref[...] = jnp.zeros_like(dwd_ref)
            dwu_ref[...] = jnp.zeros_like(dwu_ref)

        dy = dy_ref[...]
        ru = ru_ref[...].astype(F32)
        du = (_dot_nt(dy, w_ref[...]) * (2.0 * ru)).astype(BF16)
        du_ref[...] = du
        dwd_ref[...] += _dot_tn((ru * ru).astype(BF16), dy)
        dwu_ref[...] += _dot_tn(hn_ref[...], du)

    tok = pl.BlockSpec((tm, D_MODEL), lambda j, i: (i, 0))
    blk = pl.BlockSpec((tm, D_MODEL), lambda j, i: (i, j))
    wspec = pl.BlockSpec((None, D_MODEL, D_MODEL), lambda j, i: (j, 0, 0))
    wshape = jax.ShapeDtypeStruct((nj, D_MODEL, D_MODEL), F32)
    return _pcall(
        body, name="mlp_bwd_w", grid=(nj, t_all // tm),
        in_specs=[tok, pl.BlockSpec((D_MODEL, D_MODEL), lambda j, i: (j, 0)), blk, tok],
        out_specs=[blk, wspec, wspec],
        out_shape=[jax.ShapeDtypeStruct((t_all, D_FF), BF16), wshape, wshape],
        compiler_params=_params(("parallel", "arbitrary")),
    )(dyb, w_down, ru, hn)


def _mlp_dhn(du, w_up_blocks, h, dy, g2):
    t_all = h.shape[0]
    tm = min(256, t_all)

    def body(a_ref, w_ref, h_ref, dy_ref, g_ref, dh_ref, dhb_ref, dg_ref):
        @pl.when(pl.program_id(0) == 0)
        def _():
            dg_ref[...] = jnp.zeros_like(dg_ref)

        dhn = _dot_nt(a_ref[:, :D_MODEL], w_ref[0])
        for j in range(1, D_FF // D_MODEL):
            dhn = dhn + _dot_nt(a_ref[:, j * D_MODEL:(j + 1) * D_MODEL], w_ref[j])
        h = h_ref[...]
        r = lax.rsqrt(jnp.mean(h * h, axis=-1, keepdims=True) + EPS)
        hh = h * r
        dg_ref[...] += jnp.sum(dhn * hh, axis=0, keepdims=True)
        dz = dhn * g_ref[...]
        dh = dy_ref[...] + r * (dz - hh * jnp.mean(dz * hh, axis=-1, keepdims=True))
        dh_ref[...] = dh
        dhb_ref[...] = dh.astype(BF16)

    return _pcall(
        body, name="mlp_dhn", grid=(t_all // tm,),
        in_specs=[_rows(tm, D_FF), _const((D_FF // D_MODEL, D_MODEL, D_MODEL)), _rows(tm, D_MODEL),
                  _rows(tm, D_MODEL), _const((1, D_MODEL))],
        out_specs=[_rows(tm, D_MODEL), _rows(tm, D_MODEL), _const((1, D_MODEL))],
        out_shape=[jax.ShapeDtypeStruct((t_all, D_MODEL), F32), jax.ShapeDtypeStruct((t_all, D_MODEL), BF16),
                   jax.ShapeDtypeStruct((1, D_MODEL), F32)],
        compiler_params=_params(("arbitrary",)),
    )(du, w_up_blocks, h, dy, g2)


def _dmixed(dhb, wo_pad, oa, ob):
    t_all = dhb.shape[0]
    tm = min(512, t_all)
    half = A_HEADS * LANES

    def body(a_ref, w_ref, oa_ref, ob_ref, da_ref, db_ref, delta_ref, dwo_ref):
        @pl.when(pl.program_id(0) == 0)
        def _():
            dwo_ref[...] = jnp.zeros_like(dwo_ref)

        a = a_ref[...]
        d = _dot_nt(a, w_ref[...])
        da_ref[...] = d[:, :half].astype(BF16)
        db_ref[...] = d[:, half:].astype(BF16)
        for h in range(B_HEADS):
            cols = slice(h * LANES, (h + 1) * LANES)
            prod = d[:, half + h * LANES:half + (h + 1) * LANES] * ob_ref[:, cols].astype(F32)
            delta_ref[:, cols] = jnp.broadcast_to(jnp.sum(prod, axis=-1, keepdims=True), (tm, LANES))
        dwo_ref[:half, :] += _dot_tn(oa_ref[...], a)
        dwo_ref[half:, :] += _dot_tn(ob_ref[...], a)

    return _pcall(
        body, name="dmixed", grid=(t_all // tm,),
        in_specs=[_rows(tm, D_MODEL), _const((MIXED_P, D_MODEL)), _rows(tm, half), _rows(tm, half)],
        out_specs=[_rows(tm, half), _rows(tm, half), _rows(tm, half), _const((MIXED_P, D_MODEL))],
        out_shape=[jax.ShapeDtypeStruct((t_all, half), BF16), jax.ShapeDtypeStruct((t_all, half), BF16),
                   jax.ShapeDtypeStruct((t_all, half), F32), jax.ShapeDtypeStruct((MIXED_P, D_MODEL), F32)],
        compiler_params=_params(("arbitrary",)),
    )(dhb, wo_pad, oa, ob)


def _fox_bwd(qb, kb, vb, dob, lse, delta, nb, seq):
    t_all = qb.shape[0]
    tk = min(512, seq)
    nk = seq // tk

    def body(q_ref, k_ref, v_ref, do_ref, lse_ref, delta_ref, dq_ref, dk_ref, dv_ref,
             s_ref, dp_ref, p_ref, ds_ref, dk_acc, dv_acc):
        kj = pl.program_id(2)

        @pl.when(kj == 0)
        def _():
            dq_ref[...] = jnp.zeros_like(dq_ref)

        dk_acc[...] = jnp.zeros_like(dk_acc)
        dv_acc[...] = jnp.zeros_like(dv_acc)
        k = k_ref[...]
        v = v_ref[...]

        def step(i, masked):
            off = pl.multiple_of(i * tk, tk)
            q = q_ref[pl.ds(off, tk), :]
            do = do_ref[pl.ds(off, tk), :]
            s_ref[...] = _dot_nt(q, k)
            dp_ref[...] = _dot_nt(do, v)
            for r in range(0, tk, CHUNK):
                rows = slice(r, r + CHUNK)
                chunk = pl.ds(pl.multiple_of(off + r, CHUNK), CHUNK)
                lse_c = lse_ref[chunk, :]
                delta_c = delta_ref[chunk, :]
                for jt in range(tk // LANES):
                    cols = slice(jt * LANES, (jt + 1) * LANES)
                    p = jnp.exp2(s_ref[rows, cols] - lse_c)
                    if masked:
                        row = r + lax.broadcasted_iota(jnp.int32, (CHUNK, LANES), 0)
                        col = jt * LANES + lax.broadcasted_iota(jnp.int32, (CHUNK, LANES), 1)
                        p = jnp.where(row >= col, p, 0.0)
                    p_ref[rows, cols] = p.astype(BF16)
                    ds_ref[rows, cols] = (p * (dp_ref[rows, cols] - delta_c)).astype(BF16)
            dv_acc[...] += _dot_tn(p_ref[...], do)
            dk_acc[...] += _dot_tn(ds_ref[...], q)
            dq_ref[pl.ds(off, tk), :] += _dot(ds_ref[...], k)

        def unmasked(i, carry):
            step(i, False)
            return carry

        step(kj, True)
        lax.fori_loop(kj + 1, nk, unmasked, 0)
        dk_ref[...] = dk_acc[...]
        dv_ref[...] = dv_acc[...]

    full = pl.BlockSpec((seq, LANES), lambda b, h, j: (b, h))
    tile = pl.BlockSpec((tk, LANES), lambda b, h, j: (b * nk + j, h))
    shp = jax.ShapeDtypeStruct((t_all, B_HEADS * LANES), F32)
    return _pcall(
        body, name="fox_bwd", grid=(nb, B_HEADS, nk),
        in_specs=[full, tile, tile, full, full, full], out_specs=[full, tile, tile],
        out_shape=[shp, shp, shp],
        scratch_shapes=[pltpu.VMEM((tk, tk), F32), pltpu.VMEM((tk, tk), F32), pltpu.VMEM((tk, tk), BF16),
                        pltpu.VMEM((tk, tk), BF16), pltpu.VMEM((tk, LANES), F32), pltpu.VMEM((tk, LANES), F32)],
        compiler_params=_params(("parallel", "parallel", "arbitrary")),
    )(qb, kb, vb, dob, lse, delta)


def _swa_bwd(qa, ka, va, oa, doa, lrow, sinks, bias, nb, seq):
    t_all = qa.shape[0]
    tq = min(512, seq)
    nq = seq // tq

    def body(sink_ref, q_ref, k_ref, v_ref, bias0_ref, bias_ref, o_ref, do_ref, l_ref,
             dq_ref, dk_ref, dv_ref, dsink_ref):
        qi = pl.program_id(2)
        sink = _sink_column(sink_ref, pl.program_id(1))

        @pl.when(qi == 0)
        def _():
            dk_ref[...] = jnp.zeros_like(dk_ref)
            dv_ref[...] = jnp.zeros_like(dv_ref)
            dsink_ref[...] = jnp.zeros_like(dsink_ref)

        for a in range(tq // WINDOW):
            t0 = qi * tq + a * WINDOW
            start = pl.multiple_of(jnp.maximum(t0 - WINDOW, 0), WINDOW)
            rows = slice(a * WINDOW, (a + 1) * WINDOW)
            win = pl.ds(start, 2 * WINDOW)
            q = _stack_heads(q_ref, rows)
            k = k_ref[win, :]
            v = v_ref[win, :]
            do = _stack_heads(do_ref, rows)
            lrow_t = jnp.max(_stack_heads(l_ref, rows), axis=-1, keepdims=True)
            p = jnp.exp(_dot_nt(q, k) + (bias0_ref if a == 0 else bias_ref)[...] - lrow_t)
            delta = jnp.sum(do.astype(F32) * _stack_heads(o_ref, rows).astype(F32), axis=-1, keepdims=True)
            ds = (p * (_dot_nt(do, v) - delta)).astype(BF16)
            dq = _dot(ds, k)
            dk_ref[win, :] += _dot_tn(ds, q)
            dv_ref[win, :] += _dot_tn(p.astype(BF16), do)
            sink_term = jnp.exp(sink - lrow_t) * delta
            for j in range(A_GROUP):
                part = slice(j * WINDOW, (j + 1) * WINDOW)
                dq_ref[rows, j * LANES:(j + 1) * LANES] = dq[part]
                dsink_ref[j:j + 1, :] -= jnp.broadcast_to(jnp.sum(sink_term[part], axis=0, keepdims=True), (1, LANES))

    smem, qspec, kspec, bias_first, bias_rest = _swa_specs(nq, tq, seq)
    return _pcall(
        body, name="swa_bwd", grid=(nb, A_KV_HEADS, nq),
        in_specs=[smem, qspec, kspec, kspec, bias_first, bias_rest, qspec, qspec, qspec],
        out_specs=[qspec, kspec, kspec, pl.BlockSpec((None, 8, LANES), lambda b, g, i: (b * A_KV_HEADS + g, 0, 0))],
        out_shape=[jax.ShapeDtypeStruct((t_all, A_HEADS * LANES), F32),
                   jax.ShapeDtypeStruct((t_all, A_KV_HEADS * LANES), F32),
                   jax.ShapeDtypeStruct((t_all, A_KV_HEADS * LANES), F32),
                   jax.ShapeDtypeStruct((nb * A_KV_HEADS, 8, LANES), F32)],
        compiler_params=_params(("parallel", "parallel", "arbitrary")),
    )(sinks, qa, ka, va, bias, bias, oa, doa, lrow)


def _dproj_dx(pre, dqa, dka, dqb, dkb, dva, dvb, z, x2, dh, gain_row, w_pad_t, g1, seq):
    t_all = pre.shape[0]
    tm = min(256, seq)
    nt = t_all // tm
    tiles_per_seq = seq // tm
    triu = _tri(tm, True)
    sel = _dc_select()

    def body(pre_ref, dqa_ref, dka_ref, dqb_ref, dkb_ref, dva_ref, dvb_ref, z_ref, x_ref, dh_ref, gain_ref, triu_ref,
             sel_ref, w_ref, g_ref, dproj_ref, small_ref, dx_ref, dg_ref, carry_ref):
        i = pl.program_id(0)

        @pl.when(i == 0)
        def _():
            small_ref[...] = jnp.zeros_like(small_ref)
            dg_ref[...] = jnp.zeros_like(dg_ref)

        @pl.when(i % tiles_per_seq == 0)
        def _():
            carry_ref[...] = jnp.zeros_like(carry_ref)

        def norm_bwd(g, dhat):
            cols = slice(g * LANES, (g + 1) * LANES)
            p = pre_ref[:, cols].astype(F32)
            rr = lax.rsqrt(jnp.sum(p * p, axis=-1, keepdims=True) * (1.0 / HEAD_DIM) + EPS)
            n = p * rr
            dz = dhat * gain_ref[:, cols]
            dproj_ref[:, cols] = (rr * (dz - n * (jnp.sum(dz * n, axis=-1, keepdims=True) * (1.0 / HEAD_DIM)))
                                  ).astype(BF16)
            return jnp.sum(dhat * n, axis=0, keepdims=True)

        def group_sum(g0, d_ref, count, scale):
            acc = jnp.zeros((1, LANES), F32)
            for h in range(count):
                d = d_ref[:, h * LANES:(h + 1) * LANES]
                acc = acc + norm_bwd(g0 + h, d * scale if scale != 1.0 else d)
            return acc

        small_ref[0:1, :] += group_sum(G_QA, dqa_ref, A_HEADS, SCALE)
        small_ref[1:2, :] += group_sum(G_KA, dka_ref, A_KV_HEADS, 1.0)
        small_ref[2:3, :] += group_sum(G_QB, dqb_ref, B_HEADS, SCALE)
        small_ref[3:4, :] += group_sum(G_KB, dkb_ref, B_HEADS, LN2)
        dproj_ref[:, G_VA * LANES:G_VB * LANES] = dva_ref[...].astype(BF16)
        dproj_ref[:, G_VB * LANES:G_F * LANES] = dvb_ref[...].astype(BF16)

        dc = jnp.zeros((tm, LANES), F32)
        for piece_q, piece_k in zip(_split3(dqb_ref[...]), _split3(dkb_ref[...])):
            dc = dc + _dot(jnp.concatenate([piece_q, piece_k], axis=1), sel_ref[...])
        dlf = _dot_exact(triu_ref[...], dc) + carry_ref[...]
        carry_ref[...] += jnp.sum(dc, axis=0, keepdims=True)
        dz = dlf / (1.0 + jnp.exp(z_ref[...]))
        small_ref[4:5, :] += jnp.sum(dz, axis=0, keepdims=True)
        dproj_ref[:, G_F * LANES:(G_F + 1) * LANES] = dz.astype(BF16)
        dproj_ref[:, (G_F + 1) * LANES:] = jnp.zeros((tm, LANES), BF16)

        dxn = _dot(dproj_ref[...], w_ref[...])
        x = x_ref[...]
        r = lax.rsqrt(jnp.mean(x * x, axis=-1, keepdims=True) + EPS)
        xh = x * r
        dg_ref[...] += jnp.sum(dxn * xh, axis=0, keepdims=True)
        dxz = dxn * g_ref[...]
        dx_ref[...] = dh_ref[...] + r * (dxz - xh * jnp.mean(dxz * xh, axis=-1, keepdims=True))

    def rev(n):
        return pl.BlockSpec((tm, n), lambda i: (nt - 1 - i, 0))

    return _pcall(
        body, name="dproj_dx", grid=(nt,),
        in_specs=[rev(N_NORM_GROUPS * LANES), rev(A_HEADS * LANES), rev(A_KV_HEADS * LANES), rev(B_HEADS * LANES),
                  rev(B_HEADS * LANES), rev(A_KV_HEADS * LANES), rev(B_HEADS * LANES), rev(LANES), rev(D_MODEL),
                  rev(D_MODEL), _const((1, NP)), _const((tm, tm)), _const(sel.shape), _const((NP, D_MODEL)),
                  _const((1, D_MODEL))],
        out_specs=[rev(NP), _const((8, LANES)), rev(D_MODEL), _const((1, D_MODEL))],
        out_shape=[jax.ShapeDtypeStruct((t_all, NP), BF16), jax.ShapeDtypeStruct((8, LANES), F32),
                   jax.ShapeDtypeStruct((t_all, D_MODEL), F32), jax.ShapeDtypeStruct((1, D_MODEL), F32)],
        scratch_shapes=[pltpu.VMEM((1, LANES), F32)],
        compiler_params=_params(("arbitrary",)),
    )(pre, dqa, dka, dqb, dkb, dva, dvb, z, x2, dh, gain_row, triu, sel, w_pad_t, g1)


def _dwin(dproj, xn):
    t_all = xn.shape[0]
    tt = min(512, t_all)
    half = NP // 2

    def body(a_ref, b_ref, o_ref):
        @pl.when(pl.program_id(1) == 0)
        def _():
            o_ref[...] = jnp.zeros_like(o_ref)

        o_ref[...] += _dot_tn(a_ref[...], b_ref[...])

    return _pcall(
        body, name="dwin", grid=(2, t_all // tt),
        in_specs=[pl.BlockSpec((tt, half), lambda j, t: (t, j)), pl.BlockSpec((tt, D_MODEL), lambda j, t: (t, 0))],
        out_specs=pl.BlockSpec((half, D_MODEL), lambda j, t: (j, 0)),
        out_shape=jax.ShapeDtypeStruct((NP, D_MODEL), F32),
        compiler_params=_params(("parallel", "arbitrary")),
    )(dproj, xn)


ANY = pl.BlockSpec(memory_space=pl.ANY)


def _place():
    return lax.axis_index("x"), lax.axis_index("y"), lax.axis_index("c")


def _allgather_halves(mine):
    m_per, n = mine.shape

    def body(x_ref, out_ref, send_sems, recv_sems, local_sem):
        x, y, c = _place()
        me, sibling = (x, y, c), (x, y, 1 - c)
        chips = [(1 - x, y), (x, 1 - y), (1 - x, 1 - y)]

        def rows(px, py, pc):
            return out_ref.at[pl.ds((4 * px + 2 * py + pc) * m_per, m_per), :]

        def copy(k, block, to, src=None):
            return pltpu.make_async_remote_copy(
                src_ref=rows(*block) if src is None else src, dst_ref=rows(*block),
                send_sem=send_sems.at[k], recv_sem=recv_sems.at[k], device_id=to, device_id_type=MESH)

        own = pltpu.make_async_copy(x_ref, rows(*me), local_sem)
        own.start()
        first = [copy(0, me, sibling, src=x_ref)]
        first += [copy(1 + j, me, (*chip, c), src=x_ref) for j, chip in enumerate(chips)]
        for cp in first:
            cp.start()
        passed = [copy(4 + j, (*chip, c), sibling) for j, chip in enumerate(chips)]
        for j, chip in enumerate(chips):
            copy(1 + j, (*chip, c), me).wait_recv()
            passed[j].start()
        copy(0, sibling, me).wait_recv()
        for j, chip in enumerate(chips):
            copy(4 + j, (*chip, 1 - c), me).wait_recv()
        for cp in first + passed:
            cp.wait_send()
        own.wait()

    return _pcall(
        body, name="allgather_weights",
        out_shape=jax.ShapeDtypeStruct((8 * m_per, n), mine.dtype),
        in_specs=[ANY], out_specs=ANY,
        scratch_shapes=[pltpu.SemaphoreType.DMA((7,)), pltpu.SemaphoreType.DMA((7,)), pltpu.SemaphoreType.DMA],
    )(mine)


def _rs_pair_exchange(g4):
    def body(g_ref, out_ref, send_sem, recv_sem):
        x, y, c = _place()
        cp = pltpu.make_async_remote_copy(
            src_ref=g_ref.at[:, 1 - c], dst_ref=out_ref, send_sem=send_sem, recv_sem=recv_sem,
            device_id=(x, y, 1 - c), device_id_type=MESH)
        cp.start()
        cp.wait()

    return _pcall(
        body, name="rs_pair_exchange",
        out_shape=jax.ShapeDtypeStruct((N_CHIPS, R_HALF, D_MODEL), F32),
        in_specs=[ANY], out_specs=ANY,
        scratch_shapes=[pltpu.SemaphoreType.DMA, pltpu.SemaphoreType.DMA],
    )(g4)


def _rs_pair_add(g4, got, c_idx):
    def body(c_ref, a_ref, b_ref, o_ref, ob_ref):
        pair = a_ref[...] + b_ref[...]
        o_ref[...] = pair
        ob_ref[...] = pair.astype(BF16)

    blk = pl.BlockSpec((None, R_HALF, D_MODEL), lambda s, c_ref: (s, 0, 0))
    return _pcall(
        body, name="rs_pair_add",
        grid_spec=pltpu.PrefetchScalarGridSpec(
            num_scalar_prefetch=1, grid=(N_CHIPS,),
            in_specs=[pl.BlockSpec((None, None, R_HALF, D_MODEL), lambda s, c_ref: (s, c_ref[0], 0, 0)), blk],
            out_specs=[blk, blk]),
        out_shape=[jax.ShapeDtypeStruct((N_CHIPS, R_HALF, D_MODEL), F32),
                   jax.ShapeDtypeStruct((N_CHIPS, R_HALF, D_MODEL), BF16)],
        compiler_params=_params(("parallel",)),
    )(c_idx, g4, got)


def _rs_chip_exchange(p4):
    def body(p_ref, out_ref, send_sems, recv_sems):
        x, y, c = _place()
        chips = [(1 - x, y), (x, 1 - y), (1 - x, 1 - y)]
        cps = [pltpu.make_async_remote_copy(
            src_ref=p_ref.at[2 * cx + cy], dst_ref=out_ref.at[j], send_sem=send_sems.at[j],
            recv_sem=recv_sems.at[j], device_id=(cx, cy, c), device_id_type=MESH)
            for j, (cx, cy) in enumerate(chips)]
        for cp in cps:
            cp.start()
        for cp in cps:
            cp.wait()

    return _pcall(
        body, name="rs_chip_exchange",
        out_shape=jax.ShapeDtypeStruct((3, R_HALF, D_MODEL), p4.dtype),
        in_specs=[ANY], out_specs=ANY,
        scratch_shapes=[pltpu.SemaphoreType.DMA((3,)), pltpu.SemaphoreType.DMA((3,))],
    )(p4)


def _rs_chip_add(p4, got, sc_idx):
    tr = R_HALF // 7

    def body(sc_ref, a_ref, b_ref, o_ref):
        o_ref[...] = ((a_ref[...] + b_ref[0].astype(F32)) + b_ref[1].astype(F32)) + b_ref[2].astype(F32)

    return _pcall(
        body, name="rs_chip_add",
        grid_spec=pltpu.PrefetchScalarGridSpec(
            num_scalar_prefetch=1, grid=(R_HALF // tr,),
            in_specs=[pl.BlockSpec((None, tr, D_MODEL), lambda i, sc_ref: (sc_ref[0], i, 0)),
                      pl.BlockSpec((3, tr, D_MODEL), lambda i, sc_ref: (0, i, 0))],
            out_specs=pl.BlockSpec((None, tr, D_MODEL), lambda i, sc_ref: (sc_ref[1], i, 0))),
        out_shape=jax.ShapeDtypeStruct((2, R_HALF, D_MODEL), F32),
        compiler_params=_params(("parallel",)),
    )(sc_idx, p4, got)


def _rs_pair_share(halves):
    def body(r_ref, out_ref, send_sem, recv_sem):
        x, y, c = _place()
        cp = pltpu.make_async_remote_copy(
            src_ref=r_ref.at[c], dst_ref=out_ref.at[c], send_sem=send_sem, recv_sem=recv_sem,
            device_id=(x, y, 1 - c), device_id_type=MESH)
        cp.start()
        cp.wait()

    return _pcall(
        body, name="rs_pair_share",
        out_shape=jax.ShapeDtypeStruct((2, R_HALF, D_MODEL), F32),
        in_specs=[ANY], out_specs=ANY, input_output_aliases={0: 0},
        scratch_shapes=[pltpu.SemaphoreType.DMA, pltpu.SemaphoreType.DMA],
    )(halves)


def _adam(w, g, m, v):
    m2 = ADAM_B1 * m + (1.0 - ADAM_B1) * g
    v2 = ADAM_B2 * v + (1.0 - ADAM_B2) * (g * g)
    m_hat = m2 / (1.0 - ADAM_B1 ** ADAM_STEP)
    v_hat = v2 / (1.0 - ADAM_B2 ** ADAM_STEP)
    return -ADAM_LR * (m_hat / (jnp.sqrt(v_hat) + ADAM_EPS) + ADAM_WD * w), m2, v2


def _small_allreduce_adamw(part, w, m, v):
    pieces = ((0, 8, LANES), (20, 1, B_HEADS), (16, 1, HEAD_DIM), (17, 1, HEAD_DIM), (21, 1, A_HEADS),
              (18, 1, HEAD_DIM), (19, 1, HEAD_DIM), (8, 8, LANES))

    def body(p_ref, w_ref, m_ref, v_ref, *rest):
        outs, (loss_ref, buf, stage, send_sems, recv_sems) = rest[:4 * len(pieces)], rest[4 * len(pieces):]
        x, y, c = _place()
        me = 4 * x + 2 * y + c
        cps = []
        for k in range(1, 8):
            peer = (1 - x if k & 4 else x, 1 - y if k & 2 else y, 1 - c if k & 1 else c)
            cps.append(pltpu.make_async_remote_copy(
                src_ref=p_ref, dst_ref=buf.at[me], send_sem=send_sems.at[k - 1], recv_sem=recv_sems.at[k - 1],
                device_id=peer, device_id_type=MESH))
        for cp in cps:
            cp.start()
        buf[me] = p_ref[...]
        for cp in cps:
            cp.wait()
        g = buf[0]
        for k in range(1, 8):
            g = g + buf[k]
        for kind, packed in enumerate((g,) + _adam(w_ref[...], g, m_ref[...], v_ref[...])):
            stage[...] = packed
            if kind == 0:
                loss_ref[...] = stage[ROW_LOSS:ROW_LOSS + 1, :]
            for i, (row, rows, lanes) in enumerate(pieces):
                outs[kind * len(pieces) + i][...] = stage[row:row + rows, 0:lanes]

    vm = pl.BlockSpec(memory_space=pltpu.VMEM)
    shapes = [jax.ShapeDtypeStruct((rows, lanes), F32) for _ in range(4) for _, rows, lanes in pieces]
    shapes.append(jax.ShapeDtypeStruct((1, LANES), F32))
    res = _pcall(
        body, name="small_allreduce_adamw",
        out_shape=shapes, in_specs=[vm, vm, vm, vm], out_specs=[vm] * len(shapes),
        scratch_shapes=[pltpu.VMEM((8, SMALL_ROWS, LANES), F32), pltpu.VMEM((SMALL_ROWS, LANES), F32),
                        pltpu.SemaphoreType.DMA((7,)), pltpu.SemaphoreType.DMA((7,))],
    )(part, w, m, v)
    flat = [r.reshape(r.size) for r in res[:-1]]
    n = len(pieces)
    return [flat[k * n:(k + 1) * n] for k in range(4)], res[-1][0, 0]


def _adamw(w, g, m, v, name):
    rows, cols = w.shape
    tr = min(256, rows)

    def body(w_ref, g_ref, m_ref, v_ref, d_ref, m2_ref, v2_ref):
        d_ref[...], m2_ref[...], v2_ref[...] = _adam(w_ref[...], g_ref[...], m_ref[...], v_ref[...])

    spec = _rows(tr, cols)
    shp = jax.ShapeDtypeStruct((rows, cols), F32)
    return _pcall(
        body, name=name, grid=(rows // tr,), in_specs=[spec] * 4, out_specs=[spec] * 3, out_shape=[shp] * 3,
        compiler_params=_params(("parallel",)),
    )(w, g, m, v)


def _pad_lanes(v):
    return jnp.pad(v, (0, LANES - v.shape[0]))


def _pad_head_rows(w_t, heads):
    n = w_t.shape[1]
    return jnp.pad(w_t.reshape(heads, HEAD_DIM, n), ((0, 0), (0, LANES - HEAD_DIM), (0, 0))).reshape(heads * LANES, n)


def _unpad_head_rows(w_t, heads):
    n = w_t.shape[1]
    return w_t.reshape(heads, LANES, n)[:, :HEAD_DIM].reshape(heads * HEAD_DIM, n)


def _in_rows_pad(w_in_t):
    qa, ka, va, qb, kb, vb, f = jnp.split(w_in_t, [512, 640, 768, 1280, 1792, 2304], axis=0)
    f = jnp.pad(f, ((0, 2 * LANES - B_HEADS), (0, 0)))
    return jnp.concatenate([_pad_head_rows(qa, 8), _pad_head_rows(ka, 2), _pad_head_rows(qb, 8),
                            _pad_head_rows(kb, 8), _pad_head_rows(va, 2), _pad_head_rows(vb, 8), f], axis=0)


def _in_rows_unpad(d):
    qa = _unpad_head_rows(d[G_QA * LANES:G_KA * LANES], 8)
    ka = _unpad_head_rows(d[G_KA * LANES:G_QB * LANES], 2)
    qb = _unpad_head_rows(d[G_QB * LANES:G_KB * LANES], 8)
    kb = _unpad_head_rows(d[G_KB * LANES:G_VA * LANES], 8)
    va = _unpad_head_rows(d[G_VA * LANES:G_VB * LANES], 2)
    vb = _unpad_head_rows(d[G_VB * LANES:G_F * LANES], 8)
    f = d[G_F * LANES:G_F * LANES + B_HEADS]
    return jnp.concatenate([qa, ka, va, qb, kb, vb, f], axis=0)


def _pack_small(g1, bf, qa, ka, sk, qb, kb, g2, loss_row):
    rows = [g1.reshape(8, LANES), g2.reshape(8, LANES)]
    rows += [_pad_lanes(t)[None] for t in (qa, ka, qb, kb, bf, sk)]
    rows += [loss_row, jnp.zeros((1, LANES), F32)]
    return jnp.concatenate(rows, axis=0)


def kernel(x, attn_norm_g, w_in, b_forget, q_norm_a, k_norm_a, sink_logits, q_norm_b, k_norm_b, w_out, mlp_norm_g, w_up, w_down, loss_target, m_attn_norm_g, m_w_in, m_b_forget, m_q_norm_a, m_k_norm_a, m_sink_logits, m_q_norm_b, m_k_norm_b, m_w_out, m_mlp_norm_g, m_w_up, m_w_down, v_attn_norm_g, v_w_in, v_b_forget, v_q_norm_a, v_k_norm_a, v_sink_logits, v_q_norm_b, v_k_norm_b, v_w_out, v_mlp_norm_g, v_w_up, v_w_down):
    nb, seq, _ = x.shape
    t_all = nb * seq
    c_idx = lax.axis_index("c")
    s_idx = 2 * lax.axis_index("x") + lax.axis_index("y")

    packed = jnp.concatenate([jnp.pad(w_in.T, ((0, IN_SHARD_P - IN_SHARD), (0, 0))), w_out, w_up, w_down], axis=0)
    mine = lax.dynamic_slice_in_dim(packed.astype(BF16).reshape(2, R_HALF, D_MODEL), c_idx, 1, axis=0)[0]
    gathered = _allgather_halves(mine).reshape(N_CHIPS, R_ALL, D_MODEL)
    w_in_t = gathered[:, :IN_SHARD].reshape(IN_WIDTH, D_MODEL)
    w_pad_t = _in_rows_pad(w_in_t)
    wo_pad = _pad_head_rows(gathered[:, R_OUT:R_UP].reshape(D_MODEL, D_MODEL), A_HEADS + B_HEADS)
    w_up_blocks = gathered[:, R_UP:R_DOWN]
    w_down_f = gathered[:, R_DOWN:].reshape(D_FF, D_MODEL)

    ones = jnp.ones((LANES,), F32)
    gain_row = jnp.concatenate(
        [jnp.tile(_pad_lanes(q_norm_a), 8), jnp.tile(_pad_lanes(k_norm_a), 2), jnp.tile(_pad_lanes(q_norm_b), 8),
         jnp.tile(_pad_lanes(k_norm_b), 8), jnp.tile(ones, N_GROUPS - N_NORM_GROUPS)])[None]
    b_row = _pad_lanes(b_forget)[None]
    g1 = attn_norm_g[None]
    g2 = mlp_norm_g[None]
    slopes = jnp.exp2(-(8.0 / A_HEADS) * (jnp.arange(A_HEADS, dtype=F32) + 1.0))

    x2 = x.reshape(t_all, D_MODEL)
    tgt = loss_target.reshape(t_all, D_MODEL)

    xn, pre, qa, ka, va, qb, kb, vb, z = _inproj(x2, g1, w_pad_t, gain_row, b_row, seq)
    swa_bias = _swa_bias(slopes)
    oa, la = _swa_fwd(qa, ka, va, sink_logits, swa_bias, nb, seq)
    ob, lse = _fox_fwd(qb, kb, vb, nb, seq)
    h, hn = _outproj(x2, oa, ob, wo_pad, g2)
    ru, dy, dyb, loss_acc = _mlp_fwd(hn, w_up_blocks, w_down_f, h, tgt)

    du, d_w_down, d_w_up = _mlp_bwd_w(dyb, w_down_f, ru, hn)
    dh, dhb, d_g2 = _mlp_dhn(du, w_up_blocks, h, dy, g2)
    doa, dob, delta_b, d_wo = _dmixed(dhb, wo_pad, oa, ob)
    dqb, dkb, dvb = _fox_bwd(qb, kb, vb, dob, lse, delta_b, nb, seq)
    dqa, dka, dva, dsink = _swa_bwd(qa, ka, va, oa, doa, la, sink_logits, swa_bias, nb, seq)
    dproj, small, grad_x, d_g1 = _dproj_dx(pre, dqa, dka, dqb, dkb, dva, dvb, z, x2, dh, gain_row, w_pad_t, g1, seq)
    d_w_in_t = _dwin(dproj, xn)

    d_w_out = _unpad_head_rows(d_wo, A_HEADS + B_HEADS)
    g_pack = jnp.concatenate([
        jnp.pad(_in_rows_unpad(d_w_in_t).reshape(N_CHIPS, IN_SHARD, D_MODEL),
                ((0, 0), (0, IN_SHARD_P - IN_SHARD), (0, 0))),
        d_w_out.reshape(N_CHIPS, D_MODEL // N_CHIPS, D_MODEL), d_w_up, d_w_down], axis=1)
    g4 = g_pack.reshape(N_CHIPS, 2, R_HALF, D_MODEL)
    pair, pair_bf = _rs_pair_add(g4, _rs_pair_exchange(g4), c_idx.reshape(1).astype(jnp.int32))
    halves = _rs_chip_add(pair, _rs_chip_exchange(pair_bf), jnp.stack([s_idx, c_idx]).astype(jnp.int32))
    red = _rs_pair_share(halves).reshape(R_ALL, D_MODEL)
    g_w_in = red[:IN_SHARD].T
    g_w_out = red[R_OUT:R_UP]
    g_w_up = red[R_UP:R_DOWN]
    g_w_down = red[R_DOWN:]

    loss_row = loss_acc[0:1] * (0.5 / D_MODEL)
    d_sink = dsink[:, :A_GROUP, 0].reshape(nb, A_HEADS).sum(axis=0)
    part = _pack_small(d_g1[0], small[4, :B_HEADS], small[0, :HEAD_DIM], small[1, :HEAD_DIM], d_sink,
                       small[2, :HEAD_DIM], small[3, :HEAD_DIM], d_g2[0], loss_row)
    zero_row = jnp.zeros((1, LANES), F32)
    smalls = lambda t: _pack_small(*t, zero_row)
    w_small = smalls((attn_norm_g, b_forget, q_norm_a, k_norm_a, sink_logits, q_norm_b, k_norm_b, mlp_norm_g))
    m_small = smalls((m_attn_norm_g, m_b_forget, m_q_norm_a, m_k_norm_a, m_sink_logits, m_q_norm_b, m_k_norm_b,
                      m_mlp_norm_g))
    v_small = smalls((v_attn_norm_g, v_b_forget, v_q_norm_a, v_k_norm_a, v_sink_logits, v_q_norm_b, v_k_norm_b,
                      v_mlp_norm_g))
    (g_s, d_s, m_s, v_s), loss = _small_allreduce_adamw(part, w_small, m_small, v_small)

    big = {}
    for name, w, g, m, v in (("adamw_w_in", w_in, g_w_in, m_w_in, v_w_in),
                             ("adamw_w_out", w_out, g_w_out, m_w_out, v_w_out),
                             ("adamw_w_up", w_up, g_w_up, m_w_up, v_w_up),
                             ("adamw_w_down", w_down, g_w_down, m_w_down, v_w_down)):
        big[name] = (g,) + tuple(_adamw(w, g, m, v, name))

    def assemble(k, s):
        return (s[0], big["adamw_w_in"][k], s[1], s[2], s[3], s[4], s[5], s[6], big["adamw_w_out"][k], s[7],
                big["adamw_w_up"][k], big["adamw_w_down"][k])

    return (loss, grad_x.reshape(nb, seq, D_MODEL), *assemble(0, g_s), *assemble(1, d_s), *assemble(2, m_s),
            *assemble(3, v_s))
```

```python
import functools

import numpy as np
import jax
import jax.numpy as jnp
from jax import lax
from jax.experimental import pallas as pl
from jax.experimental.pallas import tpu as pltpu

F32 = jnp.float32
BF16 = jnp.bfloat16

D_MODEL = 1024
HEAD_DIM = 64
LANES = 128
A_HEADS = 8
A_KV_HEADS = 2
A_GROUP = A_HEADS // A_KV_HEADS
B_HEADS = 8
WINDOW = 128
D_FF = 4096
IN_WIDTH = 2312
EPS = 1e-6
SCALE = 0.125
LOG2E = 1.4426950408889634
LN2 = 0.6931471805599453
CHUNK = 32
NEG = -1e30

G_QA, G_KA, G_QB, G_KB, G_VA, G_VB, G_F = 0, 8, 10, 18, 26, 28, 36
N_NORM_GROUPS = 26
N_GROUPS = 38
NP = N_GROUPS * LANES
MIXED_P = (A_HEADS + B_HEADS) * LANES

N_CHIPS = 4
IN_SHARD = IN_WIDTH // N_CHIPS
IN_SHARD_P = 608
R_OUT = IN_SHARD_P
R_UP = R_OUT + D_MODEL // N_CHIPS
R_DOWN = R_UP + D_MODEL
R_ALL = R_DOWN + D_FF // N_CHIPS
R_HALF = R_ALL // 2

SMALL_ROWS = 24
ROW_LOSS = 22

ADAM_LR = 0.001
ADAM_B1 = 0.9
ADAM_B2 = 0.999
ADAM_EPS = 1e-08
ADAM_WD = 0.01
ADAM_STEP = 10

VMEM_LIMIT = 52 * 1024 * 1024
MESH = pl.DeviceIdType.MESH


def _pcall(body, **kw):
    return pl.pallas_call(body, **kw)


def _params(sem=None):
    return pltpu.CompilerParams(dimension_semantics=sem, vmem_limit_bytes=VMEM_LIMIT)


def _dot(a, b):
    return jnp.dot(a, b, preferred_element_type=F32)


def _dot_nt(a, b):
    return lax.dot_general(a, b, (((1,), (1,)), ((), ())), preferred_element_type=F32)


def _dot_tn(a, b):
    return lax.dot_general(a, b, (((0,), (0,)), ((), ())), preferred_element_type=F32)


def _split3(x):
    hi = x.astype(BF16)
    r1 = x - hi.astype(F32)
    mid = r1.astype(BF16)
    lo = (r1 - mid.astype(F32)).astype(BF16)
    return hi, mid, lo


def _dot_exact(mat, x):
    hi, mid, lo = _split3(x)
    return _dot(mat, lo) + _dot(mat, mid) + _dot(mat, hi)


def _const(shape):
    zeros = (0,) * len(shape)
    return pl.BlockSpec(shape, lambda *_: zeros)


def _rows(tm, n):
    return pl.BlockSpec((tm, n), lambda i: (i, 0))


def _aug_select():
    e = np.zeros((3 * LANES, 2 * B_HEADS * LANES), np.float32)
    for j in range(3):
        for h in range(B_HEADS):
            e[j * LANES + h, h * LANES + HEAD_DIM + j] = 1.0
            e[j * LANES + h, (B_HEADS + h) * LANES + HEAD_DIM + 3 + j] = -1.0
    return jnp.asarray(e, BF16)


def _dc_select():
    e = np.zeros((2 * B_HEADS * LANES, LANES), np.float32)
    for h in range(B_HEADS):
        e[h * LANES + HEAD_DIM, h] = 1.0
        e[(B_HEADS + h) * LANES + HEAD_DIM + 3, h] = -1.0
    return jnp.asarray(e, BF16)


def _tri(n, upper):
    t = np.tril(np.ones((n, n), np.float32))
    return jnp.asarray(t.T if upper else t, BF16)


def _inproj(x2, g1, w_pad_t, gain_row, b_row, seq, later_weights):
    t_all = x2.shape[0]
    tm = min(256, seq)
    n_steps = t_all // tm
    forward_step = (2 * n_steps) // 3
    tiles_per_seq = seq // tm
    tri = _tri(tm, False)
    esel = _aug_select()
    n_later = len(later_weights)

    def body(x_ref, g_ref, w_ref, gain_ref, b_ref, tri_ref, e_ref, *rest):
        later_src, rest = rest[:n_later], rest[n_later:]
        xn_ref, pre_ref, qa_ref, ka_ref, va_ref, qb_ref, kb_ref, vb_ref, z_ref = rest[:9]
        later_out, (carry_ref, send_sems, recv_sems, local_sems) = rest[9:9 + n_later], rest[9 + n_later:]
        i = pl.program_id(0)
        gather = _Gather(later_src, later_out, send_sems, recv_sems, local_sems)

        @pl.when(i == 0)
        def _():
            gather.start()

        @pl.when(i == forward_step)
        def _():
            gather.forward()

        @pl.when(i % tiles_per_seq == 0)
        def _():
            carry_ref[...] = jnp.zeros_like(carry_ref)

        x = x_ref[...]
        r = lax.rsqrt(jnp.mean(x * x, axis=-1, keepdims=True) + EPS)
        xn = (x * r * g_ref[...]).astype(BF16)
        xn_ref[...] = xn
        proj = _dot_nt(xn, w_ref[...])
        pre_ref[...] = proj[:, :N_NORM_GROUPS * LANES].astype(BF16)
        lane = lax.broadcasted_iota(jnp.int32, (tm, LANES), 1)

        z = proj[:, G_F * LANES:(G_F + 1) * LANES] + b_ref[...]
        z_ref[...] = z
        lf = jnp.minimum(z, 0.0) - jnp.log(1.0 + jnp.exp(-jnp.abs(z)))
        lf = jnp.where(lane < B_HEADS, lf, 0.0)
        c = _dot_exact(tri_ref[...], lf) + carry_ref[...]
        carry_ref[...] += jnp.sum(lf, axis=0, keepdims=True)
        aug = _dot(jnp.concatenate(_split3(c * LOG2E), axis=1), e_ref[...])

        def hnorm(g):
            p = proj[:, g * LANES:(g + 1) * LANES]
            rr = lax.rsqrt(jnp.sum(p * p, axis=-1, keepdims=True) * (1.0 / HEAD_DIM) + EPS)
            return p * rr * gain_ref[:, g * LANES:(g + 1) * LANES]

        ones_q = jnp.where((lane >= HEAD_DIM + 3) & (lane < HEAD_DIM + 6), 1.0, 0.0)
        ones_k = jnp.where((lane >= HEAD_DIM) & (lane < HEAD_DIM + 3), 1.0, 0.0)
        for h in range(A_HEADS):
            qa_ref[:, h * LANES:(h + 1) * LANES] = (hnorm(G_QA + h) * SCALE).astype(BF16)
        for h in range(A_KV_HEADS):
            ka_ref[:, h * LANES:(h + 1) * LANES] = hnorm(G_KA + h).astype(BF16)
        for h in range(B_HEADS):
            qb_ref[:, h * LANES:(h + 1) * LANES] = (
                hnorm(G_QB + h) * (SCALE * LOG2E) + aug[:, h * LANES:(h + 1) * LANES] + ones_q).astype(BF16)
            kb_ref[:, h * LANES:(h + 1) * LANES] = (
                hnorm(G_KB + h) + aug[:, (B_HEADS + h) * LANES:(B_HEADS + h + 1) * LANES] + ones_k).astype(BF16)
        va_ref[...] = proj[:, G_VA * LANES:G_VB * LANES].astype(BF16)
        one_v = jnp.where(lane == HEAD_DIM, 1.0, 0.0)
        for h in range(B_HEADS):
            cols = slice((G_VB + h) * LANES, (G_VB + h + 1) * LANES)
            vb_ref[:, h * LANES:(h + 1) * LANES] = (proj[:, cols] + one_v).astype(BF16)

        @pl.when(i == n_steps - 1)
        def _():
            gather.finish()

    widths = [(D_MODEL, BF16), (N_NORM_GROUPS * LANES, BF16), (A_HEADS * LANES, BF16), (A_KV_HEADS * LANES, BF16),
              (A_KV_HEADS * LANES, BF16), (B_HEADS * LANES, BF16), (B_HEADS * LANES, BF16), (B_HEADS * LANES, BF16),
              (LANES, F32)]
    res = _pcall(
        body, name="inproj", grid=(n_steps,),
        in_specs=[_rows(tm, D_MODEL), _const((1, D_MODEL)), _const((NP, D_MODEL)), _const((1, NP)),
                  _const((1, LANES)), _const((tm, tm)), _const(esel.shape)] + [ANY] * n_later,
        out_specs=[_rows(tm, w) for w, _ in widths] + [ANY] * n_later,
        out_shape=[jax.ShapeDtypeStruct((t_all, w), dt) for w, dt in widths]
        + [jax.ShapeDtypeStruct((8 * w.shape[0], w.shape[1]), w.dtype) for w in later_weights],
        scratch_shapes=[pltpu.VMEM((1, LANES), F32)] + _gather_scratch(n_later),
        compiler_params=_params(("arbitrary",)),
    )(x2, g1, w_pad_t, gain_row, b_row, tri, esel, *later_weights)
    return res[:9], res[9:]


def _fox_fwd(qb, kb, vb, nb, seq):
    t_all = qb.shape[0]
    tq = min(512, seq)
    nq = seq // tq

    def body(q_ref, k_ref, v_ref, o_ref, lse_ref, s_ref, p_ref, m_ref, alpha_ref, acc_ref):
        qi = pl.program_id(2)
        q = q_ref[...]
        hq = tq // 2
        m_ref[...] = jnp.full((tq, LANES), NEG, F32)
        acc_ref[...] = jnp.zeros((tq, LANES), F32)

        def step(j, masked):
            off = pl.multiple_of(j * tq, tq)
            k = k_ref[pl.ds(off, tq), :]
            v = v_ref[pl.ds(off, tq), :]
            for hf in range(2):
                s_ref[hf] = _dot_nt(q[hf * hq:(hf + 1) * hq], k)
            for hf in range(2):
                for r in range(0, hq, CHUNK):
                    rows = slice(r, r + CHUNK)
                    grows = slice(hf * hq + r, hf * hq + r + CHUNK)
                    tiles = []
                    for jt in range(tq // LANES):
                        sc = s_ref[hf, rows, jt * LANES:(jt + 1) * LANES]
                        if masked:
                            row = hf * hq + r + lax.broadcasted_iota(jnp.int32, (CHUNK, LANES), 0)
                            col = jt * LANES + lax.broadcasted_iota(jnp.int32, (CHUNK, LANES), 1)
                            sc = jnp.where(row >= col, sc, NEG)
                        tiles.append(sc)
                    m_prev = m_ref[grows, :]
                    m_cur = functools.reduce(jnp.maximum, tiles)
                    m_new = jnp.maximum(m_prev, jnp.max(m_cur, axis=-1, keepdims=True))
                    m_ref[grows, :] = m_new
                    alpha_ref[grows, :] = jnp.exp2(m_prev - m_new)
                    for jt, sc in enumerate(tiles):
                        p_ref[hf, rows, jt * LANES:(jt + 1) * LANES] = jnp.exp2(sc - m_new).astype(BF16)
                hrows = slice(hf * hq, (hf + 1) * hq)
                acc_ref[hrows, :] = alpha_ref[hrows, :] * acc_ref[hrows, :] + _dot(p_ref[hf], v)

        def unmasked(j, carry):
            step(j, False)
            return carry

        lax.fori_loop(0, qi, unmasked, 0)
        step(qi, True)
        acc = acc_ref[...]
        lane = lax.broadcasted_iota(jnp.int32, (tq, LANES), 1)
        l = jnp.sum(jnp.where(lane == HEAD_DIM, acc, 0.0), axis=-1, keepdims=True)
        o_ref[...] = (acc / l).astype(BF16)
        lse_ref[...] = m_ref[...] + jnp.log2(l)

    qspec = pl.BlockSpec((tq, LANES), lambda b, h, i: (b * nq + i, h))
    kspec = pl.BlockSpec((seq, LANES), lambda b, h, i: (b, h))
    return _pcall(
        body, name="fox_fwd", grid=(nb, B_HEADS, nq),
        in_specs=[qspec, kspec, kspec], out_specs=[qspec, qspec],
        out_shape=[jax.ShapeDtypeStruct((t_all, B_HEADS * LANES), BF16),
                   jax.ShapeDtypeStruct((t_all, B_HEADS * LANES), F32)],
        scratch_shapes=[pltpu.VMEM((2, tq // 2, tq), F32), pltpu.VMEM((2, tq // 2, tq), BF16),
                        pltpu.VMEM((tq, LANES), F32),
                        pltpu.VMEM((tq, LANES), F32), pltpu.VMEM((tq, LANES), F32)],
        compiler_params=_params(("parallel", "parallel", "arbitrary")),
    )(qb, kb, vb)


def _swa_bias(slopes):
    row = jnp.arange(A_GROUP * WINDOW, dtype=jnp.int32)[:, None] % WINDOW
    col = jnp.arange(2 * WINDOW, dtype=jnp.int32)[None, :]
    slope_rows = jnp.repeat(slopes.reshape(A_KV_HEADS, A_GROUP), WINDOW, axis=1)[:, :, None]
    out = []
    for t_rel in (0, WINDOW):
        dist = t_rel + row - col
        valid = (dist >= 0) & (dist < WINDOW)
        out.append(jnp.where(valid[None], -slope_rows * dist.astype(F32)[None], NEG))
    return jnp.stack(out)


def _stack_heads(ref, rows):
    return jnp.concatenate([ref[rows, j * LANES:(j + 1) * LANES] for j in range(A_GROUP)], axis=0)


def _sink_column(sink_ref, g):
    return jnp.concatenate([jnp.full((WINDOW, 1), sink_ref[g * A_GROUP + j], F32) for j in range(A_GROUP)], axis=0)


def _swa_specs(nq, tq, seq):
    smem = pl.BlockSpec(memory_space=pltpu.SMEM)
    qspec = pl.BlockSpec((tq, A_GROUP * LANES), lambda b, g, i: (b * nq + i, g))
    kspec = pl.BlockSpec((seq, LANES), lambda b, g, i: (b, g))
    bias_first = pl.BlockSpec((None, None, A_GROUP * WINDOW, 2 * WINDOW),
                              lambda b, g, i: (jnp.minimum(i, 1), g, 0, 0))
    bias_rest = pl.BlockSpec((None, None, A_GROUP * WINDOW, 2 * WINDOW), lambda b, g, i: (1, g, 0, 0))
    return smem, qspec, kspec, bias_first, bias_rest


def _swa_fwd(qa, ka, va, sinks, bias, nb, seq):
    t_all = qa.shape[0]
    tq = min(512, seq)
    nq = seq // tq

    def body(sink_ref, q_ref, k_ref, v_ref, bias0_ref, bias_ref, o_ref, l_ref):
        qi = pl.program_id(2)
        sink = _sink_column(sink_ref, pl.program_id(1))
        for a in range(tq // WINDOW):
            t0 = qi * tq + a * WINDOW
            start = pl.multiple_of(jnp.maximum(t0 - WINDOW, 0), WINDOW)
            rows = slice(a * WINDOW, (a + 1) * WINDOW)
            k = k_ref[pl.ds(start, 2 * WINDOW), :]
            v = v_ref[pl.ds(start, 2 * WINDOW), :]
            s = _dot_nt(_stack_heads(q_ref, rows), k) + (bias0_ref if a == 0 else bias_ref)[...]
            m = jnp.maximum(jnp.max(s, axis=-1, keepdims=True), sink)
            p = jnp.exp(s - m)
            den = jnp.sum(p, axis=-1, keepdims=True) + jnp.exp(sink - m)
            o = _dot((p / den).astype(BF16), v).astype(BF16)
            lrow = jnp.broadcast_to(m + jnp.log(den), (A_GROUP * WINDOW, LANES))
            for j in range(A_GROUP):
                o_ref[rows, j * LANES:(j + 1) * LANES] = o[j * WINDOW:(j + 1) * WINDOW]
                l_ref[rows, j * LANES:(j + 1) * LANES] = lrow[j * WINDOW:(j + 1) * WINDOW]

    smem, qspec, kspec, bias_first, bias_rest = _swa_specs(nq, tq, seq)
    return _pcall(
        body, name="swa_fwd", grid=(nb, A_KV_HEADS, nq),
        in_specs=[smem, qspec, kspec, kspec, bias_first, bias_rest], out_specs=[qspec, qspec],
        out_shape=[jax.ShapeDtypeStruct((t_all, A_HEADS * LANES), BF16),
                   jax.ShapeDtypeStruct((t_all, A_HEADS * LANES), F32)],
        compiler_params=_params(("parallel", "parallel", "arbitrary")),
    )(sinks, qa, ka, va, bias, bias)


def _outproj(x2, oa, ob, wo_pad, g2):
    t_all = x2.shape[0]
    tm = min(512, t_all)
    half = A_HEADS * LANES

    def body(x_ref, oa_ref, ob_ref, w_ref, g_ref, h_ref, hn_ref):
        h = x_ref[...] + _dot(oa_ref[...], w_ref[:half, :]) + _dot(ob_ref[...], w_ref[half:, :])
        h_ref[...] = h
        r = lax.rsqrt(jnp.mean(h * h, axis=-1, keepdims=True) + EPS)
        hn_ref[...] = (h * r * g_ref[...]).astype(BF16)

    return _pcall(
        body, name="outproj", grid=(t_all // tm,),
        in_specs=[_rows(tm, D_MODEL), _rows(tm, half), _rows(tm, half), _const((MIXED_P, D_MODEL)),
                  _const((1, D_MODEL))],
        out_specs=[_rows(tm, D_MODEL), _rows(tm, D_MODEL)],
        out_shape=[jax.ShapeDtypeStruct((t_all, D_MODEL), F32), jax.ShapeDtypeStruct((t_all, D_MODEL), BF16)],
        compiler_params=_params(("parallel",)),
    )(x2, oa, ob, wo_pad, g2)


def _mlp_fwd(hn, w_up_blocks, w_down, h, tgt):
    t_all = h.shape[0]
    tm = min(256, t_all)
    nj = D_FF // D_MODEL

    def body(a_ref, wu_ref, wd_ref, h_ref, t_ref, ru_ref, dy_ref, dyb_ref, loss_ref):
        @pl.when(pl.program_id(0) == 0)
        def _():
            loss_ref[...] = jnp.zeros_like(loss_ref)

        a = a_ref[...]
        y = h_ref[...]
        for j in range(nj):
            cols = slice(j * D_MODEL, (j + 1) * D_MODEL)
            ru = jnp.maximum(_dot(a, wu_ref[j]), 0.0)
            ru_ref[:, cols] = ru.astype(BF16)
            y = y + _dot((ru * ru).astype(BF16), wd_ref[cols, :])
        err = y - t_ref[...]
        loss_ref[...] += jnp.sum(err * err)
        dy = err * (1.0 / D_MODEL)
        dy_ref[...] = dy
        dyb_ref[...] = dy.astype(BF16)

    return _pcall(
        body, name="mlp_fwd", grid=(t_all // tm,),
        in_specs=[_rows(tm, D_MODEL), _const((nj, D_MODEL, D_MODEL)), _const((D_FF, D_MODEL)), _rows(tm, D_MODEL),
                  _rows(tm, D_MODEL)],
        out_specs=[_rows(tm, D_FF), _rows(tm, D_MODEL), _rows(tm, D_MODEL), _const((8, LANES))],
        out_shape=[jax.ShapeDtypeStruct((t_all, D_FF), BF16), jax.ShapeDtypeStruct((t_all, D_MODEL), F32),
                   jax.ShapeDtypeStruct((t_all, D_MODEL), BF16), jax.ShapeDtypeStruct((8, LANES), F32)],
        compiler_params=_params(("arbitrary",)),
    )(hn, w_up_blocks, w_down, h, tgt)


def _mlp_bwd_w(dyb, w_down, ru, hn):
    t_all = dyb.shape[0]
    tm = min(512, t_all)
    nj = D_FF // D_MODEL

    def body(dy_ref, w_ref, ru_ref, hn_ref, du_ref, dwd_ref, dwu_ref):
        @pl.when(pl.program_id(1) == 0)
        def _():
            dwd_ref[...] = jnp.zeros_like(dwd_ref)
            dwu_ref[...] = jnp.zeros_like(dwu_ref)

        dy = dy_ref[...]
        ru = ru_ref[...].astype(F32)
        du = (_dot_nt(dy, w_ref[...]) * (2.0 * ru)).astype(BF16)
        du_ref[...] = du
        dwd_ref[...] += _dot_tn((ru * ru).astype(BF16), dy)
        dwu_ref[...] += _dot_tn(hn_ref[...], du)

    tok = pl.BlockSpec((tm, D_MODEL), lambda j, i: (i, 0))
    blk = pl.BlockSpec((tm, D_MODEL), lambda j, i: (i, j))
    wspec = pl.BlockSpec((None, D_MODEL, D_MODEL), lambda j, i: (j, 0, 0))
    wshape = jax.ShapeDtypeStruct((nj, D_MODEL, D_MODEL), F32)
    return _pcall(
        body, name="mlp_bwd_w", grid=(nj, t_all // tm),
        in_specs=[tok, pl.BlockSpec((D_MODEL, D_MODEL), lambda j, i: (j, 0)), blk, tok],
        out_specs=[blk, wspec, wspec],
        out_shape=[jax.ShapeDtypeStruct((t_all, D_FF), BF16), wshape, wshape],
        compiler_params=_params(("parallel", "arbitrary")),
    )(dyb, w_down, ru, hn)


def _mlp_dhn(du, w_up_blocks, h, dy, g2):
    t_all = h.shape[0]
    tm = min(256, t_all)

    def body(a_ref, w_ref, h_ref, dy_ref, g_ref, dh_ref, dhb_ref, dg_ref):
        @pl.when(pl.program_id(0) == 0)
        def _():
            dg_ref[...] = jnp.zeros_like(dg_ref)

        dhn = _dot_nt(a_ref[:, :D_MODEL], w_ref[0])
        for j in range(1, D_FF // D_MODEL):
            dhn = dhn + _dot_nt(a_ref[:, j * D_MODEL:(j + 1) * D_MODEL], w_ref[j])
        h = h_ref[...]
        r = lax.rsqrt(jnp.mean(h * h, axis=-1, keepdims=True) + EPS)
        hh = h * r
        dg_ref[...] += jnp.sum(dhn * hh, axis=0, keepdims=True)
        dz = dhn * g_ref[...]
        dh = dy_ref[...] + r * (dz - hh * jnp.mean(dz * hh, axis=-1, keepdims=True))
        dh_ref[...] = dh
        dhb_ref[...] = dh.astype(BF16)

    return _pcall(
        body, name="mlp_dhn", grid=(t_all // tm,),
        in_specs=[_rows(tm, D_FF), _const((D_FF // D_MODEL, D_MODEL, D_MODEL)), _rows(tm, D_MODEL),
                  _rows(tm, D_MODEL), _const((1, D_MODEL))],
        out_specs=[_rows(tm, D_MODEL), _rows(tm, D_MODEL), _const((1, D_MODEL))],
        out_shape=[jax.ShapeDtypeStruct((t_all, D_MODEL), F32), jax.ShapeDtypeStruct((t_all, D_MODEL), BF16),
                   jax.ShapeDtypeStruct((1, D_MODEL), F32)],
        compiler_params=_params(("arbitrary",)),
    )(du, w_up_blocks, h, dy, g2)


def _dmixed(dhb, wo_pad, oa, ob):
    t_all = dhb.shape[0]
    tm = min(512, t_all)
    half = A_HEADS * LANES

    def body(a_ref, w_ref, oa_ref, ob_ref, da_ref, db_ref, delta_ref, dwo_ref):
        @pl.when(pl.program_id(0) == 0)
        def _():
            dwo_ref[...] = jnp.zeros_like(dwo_ref)

        a = a_ref[...]
        d = _dot_nt(a, w_ref[...])
        da_ref[...] = d[:, :half].astype(BF16)
        db_ref[...] = d[:, half:].astype(BF16)
        for h in range(B_HEADS):
            cols = slice(h * LANES, (h + 1) * LANES)
            prod = d[:, half + h * LANES:half + (h + 1) * LANES] * ob_ref[:, cols].astype(F32)
            delta_ref[:, cols] = jnp.broadcast_to(jnp.sum(prod, axis=-1, keepdims=True), (tm, LANES))
        dwo_ref[:half, :] += _dot_tn(oa_ref[...], a)
        dwo_ref[half:, :] += _dot_tn(ob_ref[...], a)

    return _pcall(
        body, name="dmixed", grid=(t_all // tm,),
        in_specs=[_rows(tm, D_MODEL), _const((MIXED_P, D_MODEL)), _rows(tm, half), _rows(tm, half)],
        out_specs=[_rows(tm, half), _rows(tm, half), _rows(tm, half), _const((MIXED_P, D_MODEL))],
        out_shape=[jax.ShapeDtypeStruct((t_all, half), BF16), jax.ShapeDtypeStruct((t_all, half), BF16),
                   jax.ShapeDtypeStruct((t_all, half), F32), jax.ShapeDtypeStruct((MIXED_P, D_MODEL), F32)],
        compiler_params=_params(("arbitrary",)),
    )(dhb, wo_pad, oa, ob)


def _fox_bwd(qb, kb, vb, dob, lse, delta, nb, seq):
    t_all = qb.shape[0]
    tk = min(512, seq)
    nk = seq // tk

    def body(q_ref, k_ref, v_ref, do_ref, lse_ref, delta_ref, dq_ref, dk_ref, dv_ref,
             s_ref, dp_ref, p_ref, ds_ref, dk_acc, dv_acc):
        kj = pl.program_id(2)

        @pl.when(kj == 0)
        def _():
            dq_ref[...] = jnp.zeros_like(dq_ref)

        dk_acc[...] = jnp.zeros_like(dk_acc)
        dv_acc[...] = jnp.zeros_like(dv_acc)
        k = k_ref[...]
        v = v_ref[...]

        def step(i, masked):
            off = pl.multiple_of(i * tk, tk)
            q = q_ref[pl.ds(off, tk), :]
            do = do_ref[pl.ds(off, tk), :]
            s_ref[...] = _dot_nt(q, k)
            dp_ref[...] = _dot_nt(do, v)
            for r in range(0, tk, CHUNK):
                rows = slice(r, r + CHUNK)
                chunk = pl.ds(pl.multiple_of(off + r, CHUNK), CHUNK)
                lse_c = lse_ref[chunk, :]
                delta_c = delta_ref[chunk, :]
                for jt in range(tk // LANES):
                    cols = slice(jt * LANES, (jt + 1) * LANES)
                    p = jnp.exp2(s_ref[rows, cols] - lse_c)
                    if masked:
                        row = r + lax.broadcasted_iota(jnp.int32, (CHUNK, LANES), 0)
                        col = jt * LANES + lax.broadcasted_iota(jnp.int32, (CHUNK, LANES), 1)
                        p = jnp.where(row >= col, p, 0.0)
                    p_ref[rows, cols] = p.astype(BF16)
                    ds_ref[rows, cols] = (p * (dp_ref[rows, cols] - delta_c)).astype(BF16)
            dv_acc[...] += _dot_tn(p_ref[...], do)
            dk_acc[...] += _dot_tn(ds_ref[...], q)
            dq_ref[pl.ds(off, tk), :] += _dot(ds_ref[...], k)

        def unmasked(i, carry):
            step(i, False)
            return carry

        step(kj, True)
        lax.fori_loop(kj + 1, nk, unmasked, 0)
        dk_ref[...] = dk_acc[...]
        dv_ref[...] = dv_acc[...]

    full = pl.BlockSpec((seq, LANES), lambda b, h, j: (b, h))
    tile = pl.BlockSpec((tk, LANES), lambda b, h, j: (b * nk + j, h))
    shp = jax.ShapeDtypeStruct((t_all, B_HEADS * LANES), F32)
    return _pcall(
        body, name="fox_bwd", grid=(nb, B_HEADS, nk),
        in_specs=[full, tile, tile, full, full, full], out_specs=[full, tile, tile],
        out_shape=[shp, shp, shp],
        scratch_shapes=[pltpu.VMEM((tk, tk), F32), pltpu.VMEM((tk, tk), F32), pltpu.VMEM((tk, tk), BF16),
                        pltpu.VMEM((tk, tk), BF16), pltpu.VMEM((tk, LANES), F32), pltpu.VMEM((tk, LANES), F32)],
        compiler_params=_params(("parallel", "parallel", "arbitrary")),
    )(qb, kb, vb, dob, lse, delta)


def _swa_bwd(qa, ka, va, oa, doa, lrow, sinks, bias, nb, seq):
    t_all = qa.shape[0]
    tq = min(512, seq)
    nq = seq // tq

    def body(sink_ref, q_ref, k_ref, v_ref, bias0_ref, bias_ref, o_ref, do_ref, l_ref,
             dq_ref, dk_ref, dv_ref, dsink_ref):
        qi = pl.program_id(2)
        sink = _sink_column(sink_ref, pl.program_id(1))

        @pl.when(qi == 0)
        def _():
            dk_ref[...] = jnp.zeros_like(dk_ref)
            dv_ref[...] = jnp.zeros_like(dv_ref)
            dsink_ref[...] = jnp.zeros_like(dsink_ref)

        for a in range(tq // WINDOW):
            t0 = qi * tq + a * WINDOW
            start = pl.multiple_of(jnp.maximum(t0 - WINDOW, 0), WINDOW)
            rows = slice(a * WINDOW, (a + 1) * WINDOW)
            win = pl.ds(start, 2 * WINDOW)
            q = _stack_heads(q_ref, rows)
            k = k_ref[win, :]
            v = v_ref[win, :]
            do = _stack_heads(do_ref, rows)
            lrow_t = jnp.max(_stack_heads(l_ref, rows), axis=-1, keepdims=True)
            p = jnp.exp(_dot_nt(q, k) + (bias0_ref if a == 0 else bias_ref)[...] - lrow_t)
            delta = jnp.sum(do.astype(F32) * _stack_heads(o_ref, rows).astype(F32), axis=-1, keepdims=True)
            ds = (p * (_dot_nt(do, v) - delta)).astype(BF16)
            dq = _dot(ds, k)
            dk_ref[win, :] += _dot_tn(ds, q)
            dv_ref[win, :] += _dot_tn(p.astype(BF16), do)
            sink_term = jnp.exp(sink - lrow_t) * delta
            for j in range(A_GROUP):
                part = slice(j * WINDOW, (j + 1) * WINDOW)
                dq_ref[rows, j * LANES:(j + 1) * LANES] = dq[part]
                dsink_ref[j:j + 1, :] -= jnp.broadcast_to(jnp.sum(sink_term[part], axis=0, keepdims=True), (1, LANES))

    smem, qspec, kspec, bias_first, bias_rest = _swa_specs(nq, tq, seq)
    return _pcall(
        body, name="swa_bwd", grid=(nb, A_KV_HEADS, nq),
        in_specs=[smem, qspec, kspec, kspec, bias_first, bias_rest, qspec, qspec, qspec],
        out_specs=[qspec, kspec, kspec, pl.BlockSpec((None, 8, LANES), lambda b, g, i: (b * A_KV_HEADS + g, 0, 0))],
        out_shape=[jax.ShapeDtypeStruct((t_all, A_HEADS * LANES), F32),
                   jax.ShapeDtypeStruct((t_all, A_KV_HEADS * LANES), F32),
                   jax.ShapeDtypeStruct((t_all, A_KV_HEADS * LANES), F32),
                   jax.ShapeDtypeStruct((nb * A_KV_HEADS, 8, LANES), F32)],
        compiler_params=_params(("parallel", "parallel", "arbitrary")),
    )(sinks, qa, ka, va, bias, bias, oa, doa, lrow)


def _dproj_dx(pre, dqa, dka, dqb, dkb, dva, dvb, z, x2, dh, gain_row, w_pad_t, g1, seq):
    t_all = pre.shape[0]
    tm = min(256, seq)
    nt = t_all // tm
    tiles_per_seq = seq // tm
    triu = _tri(tm, True)
    sel = _dc_select()

    def body(pre_ref, dqa_ref, dka_ref, dqb_ref, dkb_ref, dva_ref, dvb_ref, z_ref, x_ref, dh_ref, gain_ref, triu_ref,
             sel_ref, w_ref, g_ref, dproj_ref, small_ref, dx_ref, dg_ref, carry_ref):
        i = pl.program_id(0)

        @pl.when(i == 0)
        def _():
            small_ref[...] = jnp.zeros_like(small_ref)
            dg_ref[...] = jnp.zeros_like(dg_ref)

        @pl.when(i % tiles_per_seq == 0)
        def _():
            carry_ref[...] = jnp.zeros_like(carry_ref)

        def norm_bwd(g, dhat):
            cols = slice(g * LANES, (g + 1) * LANES)
            p = pre_ref[:, cols].astype(F32)
            rr = lax.rsqrt(jnp.sum(p * p, axis=-1, keepdims=True) * (1.0 / HEAD_DIM) + EPS)
            n = p * rr
            dz = dhat * gain_ref[:, cols]
            dproj_ref[:, cols] = (rr * (dz - n * (jnp.sum(dz * n, axis=-1, keepdims=True) * (1.0 / HEAD_DIM)))
                                  ).astype(BF16)
            return jnp.sum(dhat * n, axis=0, keepdims=True)

        def group_sum(g0, d_ref, count, scale):
            acc = jnp.zeros((1, LANES), F32)
            for h in range(count):
                d = d_ref[:, h * LANES:(h + 1) * LANES]
                acc = acc + norm_bwd(g0 + h, d * scale if scale != 1.0 else d)
            return acc

        small_ref[0:1, :] += group_sum(G_QA, dqa_ref, A_HEADS, SCALE)
        small_ref[1:2, :] += group_sum(G_KA, dka_ref, A_KV_HEADS, 1.0)
        small_ref[2:3, :] += group_sum(G_QB, dqb_ref, B_HEADS, SCALE)
        small_ref[3:4, :] += group_sum(G_KB, dkb_ref, B_HEADS, LN2)
        dproj_ref[:, G_VA * LANES:G_VB * LANES] = dva_ref[...].astype(BF16)
        dproj_ref[:, G_VB * LANES:G_F * LANES] = dvb_ref[...].astype(BF16)

        dc = jnp.zeros((tm, LANES), F32)
        for piece_q, piece_k in zip(_split3(dqb_ref[...]), _split3(dkb_ref[...])):
            dc = dc + _dot(jnp.concatenate([piece_q, piece_k], axis=1), sel_ref[...])
        dlf = _dot_exact(triu_ref[...], dc) + carry_ref[...]
        carry_ref[...] += jnp.sum(dc, axis=0, keepdims=True)
        dz = dlf / (1.0 + jnp.exp(z_ref[...]))
        small_ref[4:5, :] += jnp.sum(dz, axis=0, keepdims=True)
        dproj_ref[:, G_F * LANES:(G_F + 1) * LANES] = dz.astype(BF16)
        dproj_ref[:, (G_F + 1) * LANES:] = jnp.zeros((tm, LANES), BF16)

        dxn = _dot(dproj_ref[...], w_ref[...])
        x = x_ref[...]
        r = lax.rsqrt(jnp.mean(x * x, axis=-1, keepdims=True) + EPS)
        xh = x * r
        dg_ref[...] += jnp.sum(dxn * xh, axis=0, keepdims=True)
        dxz = dxn * g_ref[...]
        dx_ref[...] = dh_ref[...] + r * (dxz - xh * jnp.mean(dxz * xh, axis=-1, keepdims=True))

    def rev(n):
        return pl.BlockSpec((tm, n), lambda i: (nt - 1 - i, 0))

    return _pcall(
        body, name="dproj_dx", grid=(nt,),
        in_specs=[rev(N_NORM_GROUPS * LANES), rev(A_HEADS * LANES), rev(A_KV_HEADS * LANES), rev(B_HEADS * LANES),
                  rev(B_HEADS * LANES), rev(A_KV_HEADS * LANES), rev(B_HEADS * LANES), rev(LANES), rev(D_MODEL),
                  rev(D_MODEL), _const((1, NP)), _const((tm, tm)), _const(sel.shape), _const((NP, D_MODEL)),
                  _const((1, D_MODEL))],
        out_specs=[rev(NP), _const((8, LANES)), rev(D_MODEL), _const((1, D_MODEL))],
        out_shape=[jax.ShapeDtypeStruct((t_all, NP), BF16), jax.ShapeDtypeStruct((8, LANES), F32),
                   jax.ShapeDtypeStruct((t_all, D_MODEL), F32), jax.ShapeDtypeStruct((1, D_MODEL), F32)],
        scratch_shapes=[pltpu.VMEM((1, LANES), F32)],
        compiler_params=_params(("arbitrary",)),
    )(pre, dqa, dka, dqb, dkb, dva, dvb, z, x2, dh, gain_row, triu, sel, w_pad_t, g1)


def _dwin(dproj, xn):
    t_all = xn.shape[0]
    tt = min(512, t_all)
    half = NP // 2

    def body(a_ref, b_ref, o_ref):
        @pl.when(pl.program_id(1) == 0)
        def _():
            o_ref[...] = jnp.zeros_like(o_ref)

        o_ref[...] += _dot_tn(a_ref[...], b_ref[...])

    return _pcall(
        body, name="dwin", grid=(2, t_all // tt),
        in_specs=[pl.BlockSpec((tt, half), lambda j, t: (t, j)), pl.BlockSpec((tt, D_MODEL), lambda j, t: (t, 0))],
        out_specs=pl.BlockSpec((half, D_MODEL), lambda j, t: (j, 0)),
        out_shape=jax.ShapeDtypeStruct((NP, D_MODEL), F32),
        compiler_params=_params(("parallel", "arbitrary")),
    )(dproj, xn)


ANY = pl.BlockSpec(memory_space=pl.ANY)


def _place():
    return lax.axis_index("x"), lax.axis_index("y"), lax.axis_index("c")


class _Gather:
    def __init__(self, srcs, outs, send_sems, recv_sems, local_sems):
        self.srcs, self.outs = srcs, outs
        self.send_sems, self.recv_sems, self.local_sems = send_sems, recv_sems, local_sems
        x, y, c = _place()
        self.c = c
        self.me, self.sibling = (x, y, c), (x, y, 1 - c)
        self.chips = [(1 - x, y), (x, 1 - y), (1 - x, 1 - y)]

    def _rows(self, a, px, py, pc):
        m = self.srcs[a].shape[0]
        return self.outs[a].at[pl.ds((4 * px + 2 * py + pc) * m, m), :]

    def _copy(self, a, k, block, to, from_src=False):
        return pltpu.make_async_remote_copy(
            src_ref=self.srcs[a] if from_src else self._rows(a, *block), dst_ref=self._rows(a, *block),
            send_sem=self.send_sems.at[k, a], recv_sem=self.recv_sems.at[k, a], device_id=to, device_id_type=MESH)

    def _own(self, a):
        return pltpu.make_async_copy(self.srcs[a], self._rows(a, *self.me), self.local_sems.at[a])

    def start(self):
        for a in range(len(self.srcs)):
            self._own(a).start()
            self._copy(a, 0, self.me, self.sibling, from_src=True).start()
            for j, chip in enumerate(self.chips):
                self._copy(a, 1 + j, self.me, (*chip, self.c), from_src=True).start()

    def forward(self):
        for a in range(len(self.srcs)):
            for j, chip in enumerate(self.chips):
                self._copy(a, 1 + j, (*chip, self.c), self.me).wait_recv()
                self._copy(a, 4 + j, (*chip, self.c), self.sibling).start()

    def finish(self):
        for a in range(len(self.srcs)):
            self._copy(a, 0, self.sibling, self.me).wait_recv()
            for j, chip in enumerate(self.chips):
                self._copy(a, 4 + j, (*chip, 1 - self.c), self.me).wait_recv()
            self._copy(a, 0, self.me, self.sibling, from_src=True).wait_send()
            for j, chip in enumerate(self.chips):
                self._copy(a, 1 + j, self.me, (*chip, self.c), from_src=True).wait_send()
                self._copy(a, 4 + j, (*chip, self.c), self.sibling).wait_send()
            self._own(a).wait()


def _gather_scratch(n_arrays):
    return [pltpu.SemaphoreType.DMA((7, n_arrays)), pltpu.SemaphoreType.DMA((7, n_arrays)),
            pltpu.SemaphoreType.DMA((n_arrays,))]


def _allgather_halves(mine):
    m_per, n = mine.shape

    def body(x_ref, out_ref, send_sems, recv_sems, local_sems):
        gather = _Gather((x_ref,), (out_ref,), send_sems, recv_sems, local_sems)
        gather.start()
        gather.forward()
        gather.finish()

    return _pcall(
        body, name="allgather_w_in",
        out_shape=jax.ShapeDtypeStruct((8 * m_per, n), mine.dtype),
        in_specs=[ANY], out_specs=ANY, scratch_shapes=_gather_scratch(1),
    )(mine)


def _rs_pair_exchange(g4):
    def body(g_ref, out_ref, send_sem, recv_sem):
        x, y, c = _place()
        cp = pltpu.make_async_remote_copy(
            src_ref=g_ref.at[:, 1 - c], dst_ref=out_ref, send_sem=send_sem, recv_sem=recv_sem,
            device_id=(x, y, 1 - c), device_id_type=MESH)
        cp.start()
        cp.wait()

    return _pcall(
        body, name="rs_pair_exchange",
        out_shape=jax.ShapeDtypeStruct((N_CHIPS, R_HALF, D_MODEL), F32),
        in_specs=[ANY], out_specs=ANY,
        scratch_shapes=[pltpu.SemaphoreType.DMA, pltpu.SemaphoreType.DMA],
    )(g4)


def _rs_pair_add(g4, got, c_idx):
    def body(c_ref, a_ref, b_ref, o_ref, ob_ref):
        pair = a_ref[...] + b_ref[...]
        o_ref[...] = pair
        ob_ref[...] = pair.astype(BF16)

    blk = pl.BlockSpec((None, R_HALF, D_MODEL), lambda s, c_ref: (s, 0, 0))
    return _pcall(
        body, name="rs_pair_add",
        grid_spec=pltpu.PrefetchScalarGridSpec(
            num_scalar_prefetch=1, grid=(N_CHIPS,),
            in_specs=[pl.BlockSpec((None, None, R_HALF, D_MODEL), lambda s, c_ref: (s, c_ref[0], 0, 0)), blk],
            out_specs=[blk, blk]),
        out_shape=[jax.ShapeDtypeStruct((N_CHIPS, R_HALF, D_MODEL), F32),
                   jax.ShapeDtypeStruct((N_CHIPS, R_HALF, D_MODEL), BF16)],
        compiler_params=_params(("parallel",)),
    )(c_idx, g4, got)


def _rs_chip_exchange(p4):
    def body(p_ref, out_ref, send_sems, recv_sems):
        x, y, c = _place()
        chips = [(1 - x, y), (x, 1 - y), (1 - x, 1 - y)]
        cps = [pltpu.make_async_remote_copy(
            src_ref=p_ref.at[2 * cx + cy], dst_ref=out_ref.at[j], send_sem=send_sems.at[j],
            recv_sem=recv_sems.at[j], device_id=(cx, cy, c), device_id_type=MESH)
            for j, (cx, cy) in enumerate(chips)]
        for cp in cps:
            cp.start()
        for cp in cps:
            cp.wait()

    return _pcall(
        body, name="rs_chip_exchange",
        out_shape=jax.ShapeDtypeStruct((3, R_HALF, D_MODEL), p4.dtype),
        in_specs=[ANY], out_specs=ANY,
        scratch_shapes=[pltpu.SemaphoreType.DMA((3,)), pltpu.SemaphoreType.DMA((3,))],
    )(p4)


def _rs_chip_add(p4, got, sc_idx):
    tr = R_HALF // 7

    def body(sc_ref, a_ref, b_ref, o_ref):
        o_ref[...] = ((a_ref[...] + b_ref[0].astype(F32)) + b_ref[1].astype(F32)) + b_ref[2].astype(F32)

    return _pcall(
        body, name="rs_chip_add",
        grid_spec=pltpu.PrefetchScalarGridSpec(
            num_scalar_prefetch=1, grid=(R_HALF // tr,),
            in_specs=[pl.BlockSpec((None, tr, D_MODEL), lambda i, sc_ref: (sc_ref[0], i, 0)),
                      pl.BlockSpec((3, tr, D_MODEL), lambda i, sc_ref: (0, i, 0))],
            out_specs=pl.BlockSpec((None, tr, D_MODEL), lambda i, sc_ref: (sc_ref[1], i, 0))),
        out_shape=jax.ShapeDtypeStruct((2, R_HALF, D_MODEL), F32),
        compiler_params=_params(("parallel",)),
    )(sc_idx, p4, got)


def _rs_pair_share(halves):
    def body(r_ref, out_ref, send_sem, recv_sem):
        x, y, c = _place()
        cp = pltpu.make_async_remote_copy(
            src_ref=r_ref.at[c], dst_ref=out_ref.at[c], send_sem=send_sem, recv_sem=recv_sem,
            device_id=(x, y, 1 - c), device_id_type=MESH)
        cp.start()
        cp.wait()

    return _pcall(
        body, name="rs_pair_share",
        out_shape=jax.ShapeDtypeStruct((2, R_HALF, D_MODEL), F32),
        in_specs=[ANY], out_specs=ANY, input_output_aliases={0: 0},
        scratch_shapes=[pltpu.SemaphoreType.DMA, pltpu.SemaphoreType.DMA],
    )(halves)


def _adam(w, g, m, v):
    m2 = ADAM_B1 * m + (1.0 - ADAM_B1) * g
    v2 = ADAM_B2 * v + (1.0 - ADAM_B2) * (g * g)
    m_hat = m2 / (1.0 - ADAM_B1 ** ADAM_STEP)
    v_hat = v2 / (1.0 - ADAM_B2 ** ADAM_STEP)
    return -ADAM_LR * (m_hat / (jnp.sqrt(v_hat) + ADAM_EPS) + ADAM_WD * w), m2, v2


def _small_allreduce_adamw(part, w, m, v):
    pieces = ((0, 8, LANES), (20, 1, B_HEADS), (16, 1, HEAD_DIM), (17, 1, HEAD_DIM), (21, 1, A_HEADS),
              (18, 1, HEAD_DIM), (19, 1, HEAD_DIM), (8, 8, LANES))

    def body(p_ref, w_ref, m_ref, v_ref, *rest):
        outs, (loss_ref, buf, stage, send_sems, recv_sems) = rest[:4 * len(pieces)], rest[4 * len(pieces):]
        x, y, c = _place()
        me = 4 * x + 2 * y + c
        cps = []
        for k in range(1, 8):
            peer = (1 - x if k & 4 else x, 1 - y if k & 2 else y, 1 - c if k & 1 else c)
            cps.append(pltpu.make_async_remote_copy(
                src_ref=p_ref, dst_ref=buf.at[me], send_sem=send_sems.at[k - 1], recv_sem=recv_sems.at[k - 1],
                device_id=peer, device_id_type=MESH))
        for cp in cps:
            cp.start()
        buf[me] = p_ref[...]
        for cp in cps:
            cp.wait()
        g = buf[0]
        for k in range(1, 8):
            g = g + buf[k]
        for kind, packed in enumerate((g,) + _adam(w_ref[...], g, m_ref[...], v_ref[...])):
            stage[...] = packed
            if kind == 0:
                loss_ref[...] = stage[ROW_LOSS:ROW_LOSS + 1, :]
            for i, (row, rows, lanes) in enumerate(pieces):
                outs[kind * len(pieces) + i][...] = stage[row:row + rows, 0:lanes]

    vm = pl.BlockSpec(memory_space=pltpu.VMEM)
    shapes = [jax.ShapeDtypeStruct((rows, lanes), F32) for _ in range(4) for _, rows, lanes in pieces]
    shapes.append(jax.ShapeDtypeStruct((1, LANES), F32))
    res = _pcall(
        body, name="small_allreduce_adamw",
        out_shape=shapes, in_specs=[vm, vm, vm, vm], out_specs=[vm] * len(shapes),
        scratch_shapes=[pltpu.VMEM((8, SMALL_ROWS, LANES), F32), pltpu.VMEM((SMALL_ROWS, LANES), F32),
                        pltpu.SemaphoreType.DMA((7,)), pltpu.SemaphoreType.DMA((7,))],
    )(part, w, m, v)
    flat = [r.reshape(r.size) for r in res[:-1]]
    n = len(pieces)
    return [flat[k * n:(k + 1) * n] for k in range(4)], res[-1][0, 0]


def _adamw(w, g, m, v, name):
    rows, cols = w.shape
    tr = min(256, rows)

    def body(w_ref, g_ref, m_ref, v_ref, d_ref, m2_ref, v2_ref):
        d_ref[...], m2_ref[...], v2_ref[...] = _adam(w_ref[...], g_ref[...], m_ref[...], v_ref[...])

    spec = _rows(tr, cols)
    shp = jax.ShapeDtypeStruct((rows, cols), F32)
    return _pcall(
        body, name=name, grid=(rows // tr,), in_specs=[spec] * 4, out_specs=[spec] * 3, out_shape=[shp] * 3,
        compiler_params=_params(("parallel",)),
    )(w, g, m, v)


def _pad_lanes(v):
    return jnp.pad(v, (0, LANES - v.shape[0]))


def _pad_head_rows(w_t, heads):
    n = w_t.shape[1]
    return jnp.pad(w_t.reshape(heads, HEAD_DIM, n), ((0, 0), (0, LANES - HEAD_DIM), (0, 0))).reshape(heads * LANES, n)


def _unpad_head_rows(w_t, heads):
    n = w_t.shape[1]
    return w_t.reshape(heads, LANES, n)[:, :HEAD_DIM].reshape(heads * HEAD_DIM, n)


def _in_rows_pad(w_in_t):
    qa, ka, va, qb, kb, vb, f = jnp.split(w_in_t, [512, 640, 768, 1280, 1792, 2304], axis=0)
    f = jnp.pad(f, ((0, 2 * LANES - B_HEADS), (0, 0)))
    return jnp.concatenate([_pad_head_rows(qa, 8), _pad_head_rows(ka, 2), _pad_head_rows(qb, 8),
                            _pad_head_rows(kb, 8), _pad_head_rows(va, 2), _pad_head_rows(vb, 8), f], axis=0)


def _in_rows_unpad(d):
    qa = _unpad_head_rows(d[G_QA * LANES:G_KA * LANES], 8)
    ka = _unpad_head_rows(d[G_KA * LANES:G_QB * LANES], 2)
    qb = _unpad_head_rows(d[G_QB * LANES:G_KB * LANES], 8)
    kb = _unpad_head_rows(d[G_KB * LANES:G_VA * LANES], 8)
    va = _unpad_head_rows(d[G_VA * LANES:G_VB * LANES], 2)
    vb = _unpad_head_rows(d[G_VB * LANES:G_F * LANES], 8)
    f = d[G_F * LANES:G_F * LANES + B_HEADS]
    return jnp.concatenate([qa, ka, va, qb, kb, vb, f], axis=0)


def _pack_small(g1, bf, qa, ka, sk, qb, kb, g2, loss_row):
    rows = [g1.reshape(8, LANES), g2.reshape(8, LANES)]
    rows += [_pad_lanes(t)[None] for t in (qa, ka, qb, kb, bf, sk)]
    rows += [loss_row, jnp.zeros((1, LANES), F32)]
    return jnp.concatenate(rows, axis=0)


def kernel(x, attn_norm_g, w_in, b_forget, q_norm_a, k_norm_a, sink_logits, q_norm_b, k_norm_b, w_out, mlp_norm_g, w_up, w_down, loss_target, m_attn_norm_g, m_w_in, m_b_forget, m_q_norm_a, m_k_norm_a, m_sink_logits, m_q_norm_b, m_k_norm_b, m_w_out, m_mlp_norm_g, m_w_up, m_w_down, v_attn_norm_g, v_w_in, v_b_forget, v_q_norm_a, v_k_norm_a, v_sink_logits, v_q_norm_b, v_k_norm_b, v_w_out, v_mlp_norm_g, v_w_up, v_w_down):
    nb, seq, _ = x.shape
    t_all = nb * seq
    c_idx = lax.axis_index("c")
    s_idx = 2 * lax.axis_index("x") + lax.axis_index("y")

    def my_half(a):
        halves = a.astype(BF16).reshape(2, a.shape[0] // 2, a.shape[1])
        return lax.dynamic_slice_in_dim(halves, c_idx, 1, axis=0)[0]

    w_in_shard_t = jnp.pad(w_in.T, ((0, IN_SHARD_P - IN_SHARD), (0, 0)))
    gathered_in = _allgather_halves(my_half(w_in_shard_t)).reshape(N_CHIPS, IN_SHARD_P, D_MODEL)
    w_pad_t = _in_rows_pad(gathered_in[:, :IN_SHARD].reshape(IN_WIDTH, D_MODEL))

    ones = jnp.ones((LANES,), F32)
    gain_row = jnp.concatenate(
        [jnp.tile(_pad_lanes(q_norm_a), 8), jnp.tile(_pad_lanes(k_norm_a), 2), jnp.tile(_pad_lanes(q_norm_b), 8),
         jnp.tile(_pad_lanes(k_norm_b), 8), jnp.tile(ones, N_GROUPS - N_NORM_GROUPS)])[None]
    b_row = _pad_lanes(b_forget)[None]
    g1 = attn_norm_g[None]
    g2 = mlp_norm_g[None]
    slopes = jnp.exp2(-(8.0 / A_HEADS) * (jnp.arange(A_HEADS, dtype=F32) + 1.0))

    x2 = x.reshape(t_all, D_MODEL)
    tgt = loss_target.reshape(t_all, D_MODEL)

    (xn, pre, qa, ka, va, qb, kb, vb, z), (w_out_g, w_up_g, w_down_f) = _inproj(
        x2, g1, w_pad_t, gain_row, b_row, seq, [my_half(w_out), my_half(w_up), my_half(w_down)])
    wo_pad = _pad_head_rows(w_out_g, A_HEADS + B_HEADS)
    w_up_blocks = w_up_g.reshape(N_CHIPS, D_MODEL, D_MODEL)
    swa_bias = _swa_bias(slopes)
    oa, la = _swa_fwd(qa, ka, va, sink_logits, swa_bias, nb, seq)
    ob, lse = _fox_fwd(qb, kb, vb, nb, seq)
    h, hn = _outproj(x2, oa, ob, wo_pad, g2)
    ru, dy, dyb, loss_acc = _mlp_fwd(hn, w_up_blocks, w_down_f, h, tgt)

    du, d_w_down, d_w_up = _mlp_bwd_w(dyb, w_down_f, ru, hn)
    dh, dhb, d_g2 = _mlp_dhn(du, w_up_blocks, h, dy, g2)
    doa, dob, delta_b, d_wo = _dmixed(dhb, wo_pad, oa, ob)
    dqb, dkb, dvb = _fox_bwd(qb, kb, vb, dob, lse, delta_b, nb, seq)
    dqa, dka, dva, dsink = _swa_bwd(qa, ka, va, oa, doa, la, sink_logits, swa_bias, nb, seq)
    dproj, small, grad_x, d_g1 = _dproj_dx(pre, dqa, dka, dqb, dkb, dva, dvb, z, x2, dh, gain_row, w_pad_t, g1, seq)
    d_w_in_t = _dwin(dproj, xn)

    d_w_out = _unpad_head_rows(d_wo, A_HEADS + B_HEADS)
    g_pack = jnp.concatenate([
        jnp.pad(_in_rows_unpad(d_w_in_t).reshape(N_CHIPS, IN_SHARD, D_MODEL),
                ((0, 0), (0, IN_SHARD_P - IN_SHARD), (0, 0))),
        d_w_out.reshape(N_CHIPS, D_MODEL // N_CHIPS, D_MODEL), d_w_up, d_w_down], axis=1)
    g4 = g_pack.reshape(N_CHIPS, 2, R_HALF, D_MODEL)
    pair, pair_bf = _rs_pair_add(g4, _rs_pair_exchange(g4), c_idx.reshape(1).astype(jnp.int32))
    halves = _rs_chip_add(pair, _rs_chip_exchange(pair_bf), jnp.stack([s_idx, c_idx]).astype(jnp.int32))
    red = _rs_pair_share(halves).reshape(R_ALL, D_MODEL)
    g_w_in = red[:IN_SHARD].T
    g_w_out = red[R_OUT:R_UP]
    g_w_up = red[R_UP:R_DOWN]
    g_w_down = red[R_DOWN:]

    loss_row = loss_acc[0:1] * (0.5 / D_MODEL)
    d_sink = dsink[:, :A_GROUP, 0].reshape(nb, A_HEADS).sum(axis=0)
    part = _pack_small(d_g1[0], small[4, :B_HEADS], small[0, :HEAD_DIM], small[1, :HEAD_DIM], d_sink,
                       small[2, :HEAD_DIM], small[3, :HEAD_DIM], d_g2[0], loss_row)
    zero_row = jnp.zeros((1, LANES), F32)
    smalls = lambda t: _pack_small(*t, zero_row)
    w_small = smalls((attn_norm_g, b_forget, q_norm_a, k_norm_a, sink_logits, q_norm_b, k_norm_b, mlp_norm_g))
    m_small = smalls((m_attn_norm_g, m_b_forget, m_q_norm_a, m_k_norm_a, m_sink_logits, m_q_norm_b, m_k_norm_b,
                      m_mlp_norm_g))
    v_small = smalls((v_attn_norm_g, v_b_forget, v_q_norm_a, v_k_norm_a, v_sink_logits, v_q_norm_b, v_k_norm_b,
                      v_mlp_norm_g))
    (g_s, d_s, m_s, v_s), loss = _small_allreduce_adamw(part, w_small, m_small, v_small)

    big = {}
    for name, w, g, m, v in (("adamw_w_in", w_in, g_w_in, m_w_in, v_w_in),
                             ("adamw_w_out", w_out, g_w_out, m_w_out, v_w_out),
                             ("adamw_w_up", w_up, g_w_up, m_w_up, v_w_up),
                             ("adamw_w_down", w_down, g_w_down, m_w_down, v_w_down)):
        big[name] = (g,) + tuple(_adamw(w, g, m, v, name))

    def assemble(k, s):
        return (s[0], big["adamw_w_in"][k], s[1], s[2], s[3], s[4], s[5], s[6], big["adamw_w_out"][k], s[7],
                big["adamw_w_up"][k], big["adamw_w_down"][k])

    return (loss, grad_x.reshape(nb, seq, D_MODEL), *assemble(0, g_s), *assemble(1, d_s), *assemble(2, m_s),
            *assemble(3, v_s))
```

```python
import functools

import numpy as np
import jax
import jax.numpy as jnp
from jax import lax
from jax.experimental import pallas as pl
from jax.experimental.pallas import tpu as pltpu

F32 = jnp.float32
BF16 = jnp.bfloat16

D_MODEL = 1024
HEAD_DIM = 64
LANES = 128
A_HEADS = 8
A_KV_HEADS = 2
A_GROUP = A_HEADS // A_KV_HEADS
B_HEADS = 8
WINDOW = 128
D_FF = 4096
IN_WIDTH = 2312
EPS = 1e-6
SCALE = 0.125
LOG2E = 1.4426950408889634
LN2 = 0.6931471805599453
CHUNK = 32
NEG = -1e30

G_QA, G_KA, G_QB, G_KB, G_VA, G_VB, G_F = 0, 8, 10, 18, 26, 28, 36
N_NORM_GROUPS = 26
N_GROUPS = 38
NP = N_GROUPS * LANES
MIXED_P = (A_HEADS + B_HEADS) * LANES

N_CHIPS = 4
IN_SHARD = IN_WIDTH // N_CHIPS
IN_SHARD_P = 608
R_ATT = IN_SHARD_P + D_MODEL // N_CHIPS

SMALL_ROWS = 24
ROW_LOSS = 22

ADAM_LR = 0.001
ADAM_B1 = 0.9
ADAM_B2 = 0.999
ADAM_EPS = 1e-08
ADAM_WD = 0.01
ADAM_STEP = 10

VMEM_LIMIT = 52 * 1024 * 1024
MESH = pl.DeviceIdType.MESH


def _pcall(body, **kw):
    return pl.pallas_call(body, **kw)


def _params(sem=None):
    return pltpu.CompilerParams(dimension_semantics=sem, vmem_limit_bytes=VMEM_LIMIT)


def _dot(a, b):
    return jnp.dot(a, b, preferred_element_type=F32)


def _dot_nt(a, b):
    return lax.dot_general(a, b, (((1,), (1,)), ((), ())), preferred_element_type=F32)


def _dot_tn(a, b):
    return lax.dot_general(a, b, (((0,), (0,)), ((), ())), preferred_element_type=F32)


def _split3(x):
    hi = x.astype(BF16)
    r1 = x - hi.astype(F32)
    mid = r1.astype(BF16)
    lo = (r1 - mid.astype(F32)).astype(BF16)
    return hi, mid, lo


def _dot_exact(mat, x):
    hi, mid, lo = _split3(x)
    return _dot(mat, lo) + _dot(mat, mid) + _dot(mat, hi)


def _const(shape):
    zeros = (0,) * len(shape)
    return pl.BlockSpec(shape, lambda *_: zeros)


def _rows(tm, n):
    return pl.BlockSpec((tm, n), lambda i: (i, 0))


def _aug_select():
    e = np.zeros((3 * LANES, 2 * B_HEADS * LANES), np.float32)
    for j in range(3):
        for h in range(B_HEADS):
            e[j * LANES + h, h * LANES + HEAD_DIM + j] = 1.0
            e[j * LANES + h, (B_HEADS + h) * LANES + HEAD_DIM + 3 + j] = -1.0
    return jnp.asarray(e, BF16)


def _dc_select():
    e = np.zeros((2 * B_HEADS * LANES, LANES), np.float32)
    for h in range(B_HEADS):
        e[h * LANES + HEAD_DIM, h] = 1.0
        e[(B_HEADS + h) * LANES + HEAD_DIM + 3, h] = -1.0
    return jnp.asarray(e, BF16)


def _tri(n, upper):
    t = np.tril(np.ones((n, n), np.float32))
    return jnp.asarray(t.T if upper else t, BF16)


def _inproj(x2, g1, w_pad_t, gain_row, b_row, seq, later_weights):
    t_all = x2.shape[0]
    tm = min(256, seq)
    n_steps = t_all // tm
    forward_step = max(n_steps - 2, 0)
    tiles_per_seq = seq // tm
    tri = _tri(tm, False)
    esel = _aug_select()
    n_later = len(later_weights)

    def body(x_ref, g_ref, w_ref, gain_ref, b_ref, tri_ref, e_ref, *rest):
        later_src, rest = rest[:n_later], rest[n_later:]
        xn_ref, pre_ref, qa_ref, ka_ref, va_ref, qb_ref, kb_ref, vb_ref, z_ref = rest[:9]
        later_out, (carry_ref, send_sems, recv_sems, local_sems) = rest[9:9 + n_later], rest[9 + n_later:]
        i = pl.program_id(0)
        gather = _Gather(later_src, later_out, send_sems, recv_sems, local_sems)

        @pl.when(i == 0)
        def _():
            gather.start()

        @pl.when(i == forward_step)
        def _():
            gather.forward()

        @pl.when(i % tiles_per_seq == 0)
        def _():
            carry_ref[...] = jnp.zeros_like(carry_ref)

        x = x_ref[...]
        r = lax.rsqrt(jnp.mean(x * x, axis=-1, keepdims=True) + EPS)
        xn = (x * r * g_ref[...]).astype(BF16)
        xn_ref[...] = xn
        proj = _dot_nt(xn, w_ref[...])
        pre_ref[...] = proj[:, :N_NORM_GROUPS * LANES].astype(BF16)
        lane = lax.broadcasted_iota(jnp.int32, (tm, LANES), 1)

        z = proj[:, G_F * LANES:(G_F + 1) * LANES] + b_ref[...]
        z_ref[...] = z
        lf = jnp.minimum(z, 0.0) - jnp.log(1.0 + jnp.exp(-jnp.abs(z)))
        lf = jnp.where(lane < B_HEADS, lf, 0.0)
        c = _dot_exact(tri_ref[...], lf) + carry_ref[...]
        carry_ref[...] += jnp.sum(lf, axis=0, keepdims=True)
        aug = _dot(jnp.concatenate(_split3(c * LOG2E), axis=1), e_ref[...])

        def hnorm(g):
            p = proj[:, g * LANES:(g + 1) * LANES]
            rr = lax.rsqrt(jnp.sum(p * p, axis=-1, keepdims=True) * (1.0 / HEAD_DIM) + EPS)
            return p * rr * gain_ref[:, g * LANES:(g + 1) * LANES]

        ones_q = jnp.where((lane >= HEAD_DIM + 3) & (lane < HEAD_DIM + 6), 1.0, 0.0)
        ones_k = jnp.where((lane >= HEAD_DIM) & (lane < HEAD_DIM + 3), 1.0, 0.0)
        for h in range(A_HEADS):
            qa_ref[:, h * LANES:(h + 1) * LANES] = (hnorm(G_QA + h) * SCALE).astype(BF16)
        for h in range(A_KV_HEADS):
            ka_ref[:, h * LANES:(h + 1) * LANES] = hnorm(G_KA + h).astype(BF16)
        for h in range(B_HEADS):
            qb_ref[:, h * LANES:(h + 1) * LANES] = (
                hnorm(G_QB + h) * (SCALE * LOG2E) + aug[:, h * LANES:(h + 1) * LANES] + ones_q).astype(BF16)
            kb_ref[:, h * LANES:(h + 1) * LANES] = (
                hnorm(G_KB + h) + aug[:, (B_HEADS + h) * LANES:(B_HEADS + h + 1) * LANES] + ones_k).astype(BF16)
        va_ref[...] = proj[:, G_VA * LANES:G_VB * LANES].astype(BF16)
        one_v = jnp.where(lane == HEAD_DIM, 1.0, 0.0)
        for h in range(B_HEADS):
            cols = slice((G_VB + h) * LANES, (G_VB + h + 1) * LANES)
            vb_ref[:, h * LANES:(h + 1) * LANES] = (proj[:, cols] + one_v).astype(BF16)

        @pl.when(i == n_steps - 1)
        def _():
            gather.finish()

    widths = [(D_MODEL, BF16), (N_NORM_GROUPS * LANES, BF16), (A_HEADS * LANES, BF16), (A_KV_HEADS * LANES, BF16),
              (A_KV_HEADS * LANES, BF16), (B_HEADS * LANES, BF16), (B_HEADS * LANES, BF16), (B_HEADS * LANES, BF16),
              (LANES, F32)]
    res = _pcall(
        body, name="inproj", grid=(n_steps,),
        in_specs=[_rows(tm, D_MODEL), _const((1, D_MODEL)), _const((NP, D_MODEL)), _const((1, NP)),
                  _const((1, LANES)), _const((tm, tm)), _const(esel.shape)] + [ANY] * n_later,
        out_specs=[_rows(tm, w) for w, _ in widths] + [ANY] * n_later,
        out_shape=[jax.ShapeDtypeStruct((t_all, w), dt) for w, dt in widths]
        + [jax.ShapeDtypeStruct((8 * w.shape[0], w.shape[1]), w.dtype) for w in later_weights],
        scratch_shapes=[pltpu.VMEM((1, LANES), F32)] + _gather_scratch(n_later),
        compiler_params=_params(("arbitrary",)),
    )(x2, g1, w_pad_t, gain_row, b_row, tri, esel, *later_weights)
    return res[:9], res[9:]


def _fox_fwd(qb, kb, vb, nb, seq):
    t_all = qb.shape[0]
    tq = min(512, seq)
    nq = seq // tq

    def body(q_ref, k_ref, v_ref, o_ref, lse_ref, s_ref, p_ref, m_ref, alpha_ref, acc_ref):
        qi = pl.program_id(2)
        q = q_ref[...]
        hq = tq // 2
        m_ref[...] = jnp.full((tq, LANES), NEG, F32)
        acc_ref[...] = jnp.zeros((tq, LANES), F32)

        def step(j, masked):
            off = pl.multiple_of(j * tq, tq)
            k = k_ref[pl.ds(off, tq), :]
            v = v_ref[pl.ds(off, tq), :]
            for hf in range(2):
                s_ref[hf] = _dot_nt(q[hf * hq:(hf + 1) * hq], k)
            for hf in range(2):
                for r in range(0, hq, CHUNK):
                    rows = slice(r, r + CHUNK)
                    grows = slice(hf * hq + r, hf * hq + r + CHUNK)
                    tiles = []
                    for jt in range(tq // LANES):
                        sc = s_ref[hf, rows, jt * LANES:(jt + 1) * LANES]
                        if masked:
                            row = hf * hq + r + lax.broadcasted_iota(jnp.int32, (CHUNK, LANES), 0)
                            col = jt * LANES + lax.broadcasted_iota(jnp.int32, (CHUNK, LANES), 1)
                            sc = jnp.where(row >= col, sc, NEG)
                        tiles.append(sc)
                    m_prev = m_ref[grows, :]
                    m_cur = functools.reduce(jnp.maximum, tiles)
                    m_new = jnp.maximum(m_prev, jnp.max(m_cur, axis=-1, keepdims=True))
                    m_ref[grows, :] = m_new
                    alpha_ref[grows, :] = jnp.exp2(m_prev - m_new)
                    for jt, sc in enumerate(tiles):
                        p_ref[hf, rows, jt * LANES:(jt + 1) * LANES] = jnp.exp2(sc - m_new).astype(BF16)
                hrows = slice(hf * hq, (hf + 1) * hq)
                acc_ref[hrows, :] = alpha_ref[hrows, :] * acc_ref[hrows, :] + _dot(p_ref[hf], v)

        def unmasked(j, carry):
            step(j, False)
            return carry

        lax.fori_loop(0, qi, unmasked, 0)
        step(qi, True)
        acc = acc_ref[...]
        lane = lax.broadcasted_iota(jnp.int32, (tq, LANES), 1)
        l = jnp.sum(jnp.where(lane == HEAD_DIM, acc, 0.0), axis=-1, keepdims=True)
        o_ref[...] = (acc / l).astype(BF16)
        lse_ref[...] = m_ref[...] + jnp.log2(l)

    qspec = pl.BlockSpec((tq, LANES), lambda b, h, i: (b * nq + i, h))
    kspec = pl.BlockSpec((seq, LANES), lambda b, h, i: (b, h))
    return _pcall(
        body, name="fox_fwd", grid=(nb, B_HEADS, nq),
        in_specs=[qspec, kspec, kspec], out_specs=[qspec, qspec],
        out_shape=[jax.ShapeDtypeStruct((t_all, B_HEADS * LANES), BF16),
                   jax.ShapeDtypeStruct((t_all, B_HEADS * LANES), F32)],
        scratch_shapes=[pltpu.VMEM((2, tq // 2, tq), F32), pltpu.VMEM((2, tq // 2, tq), BF16),
                        pltpu.VMEM((tq, LANES), F32),
                        pltpu.VMEM((tq, LANES), F32), pltpu.VMEM((tq, LANES), F32)],
        compiler_params=_params(("parallel", "parallel", "arbitrary")),
    )(qb, kb, vb)


def _swa_bias(slopes):
    row = jnp.arange(A_GROUP * WINDOW, dtype=jnp.int32)[:, None] % WINDOW
    col = jnp.arange(2 * WINDOW, dtype=jnp.int32)[None, :]
    slope_rows = jnp.repeat(slopes.reshape(A_KV_HEADS, A_GROUP), WINDOW, axis=1)[:, :, None]
    out = []
    for t_rel in (0, WINDOW):
        dist = t_rel + row - col
        valid = (dist >= 0) & (dist < WINDOW)
        out.append(jnp.where(valid[None], -slope_rows * dist.astype(F32)[None], NEG))
    return jnp.stack(out)


def _stack_heads(ref, rows):
    return jnp.concatenate([ref[rows, j * LANES:(j + 1) * LANES] for j in range(A_GROUP)], axis=0)


def _sink_column(sink_ref, g):
    return jnp.concatenate([jnp.full((WINDOW, 1), sink_ref[g * A_GROUP + j], F32) for j in range(A_GROUP)], axis=0)


def _swa_specs(nq, tq, seq):
    smem = pl.BlockSpec(memory_space=pltpu.SMEM)
    qspec = pl.BlockSpec((tq, A_GROUP * LANES), lambda b, g, i: (b * nq + i, g))
    kspec = pl.BlockSpec((seq, LANES), lambda b, g, i: (b, g))
    bias_first = pl.BlockSpec((None, None, A_GROUP * WINDOW, 2 * WINDOW),
                              lambda b, g, i: (jnp.minimum(i, 1), g, 0, 0))
    bias_rest = pl.BlockSpec((None, None, A_GROUP * WINDOW, 2 * WINDOW), lambda b, g, i: (1, g, 0, 0))
    return smem, qspec, kspec, bias_first, bias_rest


def _swa_fwd(qa, ka, va, sinks, bias, nb, seq):
    t_all = qa.shape[0]
    tq = min(512, seq)
    nq = seq // tq

    def body(sink_ref, q_ref, k_ref, v_ref, bias0_ref, bias_ref, o_ref, l_ref):
        qi = pl.program_id(2)
        sink = _sink_column(sink_ref, pl.program_id(1))
        for a in range(tq // WINDOW):
            t0 = qi * tq + a * WINDOW
            start = pl.multiple_of(jnp.maximum(t0 - WINDOW, 0), WINDOW)
            rows = slice(a * WINDOW, (a + 1) * WINDOW)
            k = k_ref[pl.ds(start, 2 * WINDOW), :]
            v = v_ref[pl.ds(start, 2 * WINDOW), :]
            s = _dot_nt(_stack_heads(q_ref, rows), k) + (bias0_ref if a == 0 else bias_ref)[...]
            m = jnp.maximum(jnp.max(s, axis=-1, keepdims=True), sink)
            p = jnp.exp(s - m)
            den = jnp.sum(p, axis=-1, keepdims=True) + jnp.exp(sink - m)
            o = _dot((p / den).astype(BF16), v).astype(BF16)
            lrow = jnp.broadcast_to(m + jnp.log(den), (A_GROUP * WINDOW, LANES))
            for j in range(A_GROUP):
                o_ref[rows, j * LANES:(j + 1) * LANES] = o[j * WINDOW:(j + 1) * WINDOW]
                l_ref[rows, j * LANES:(j + 1) * LANES] = lrow[j * WINDOW:(j + 1) * WINDOW]

    smem, qspec, kspec, bias_first, bias_rest = _swa_specs(nq, tq, seq)
    return _pcall(
        body, name="swa_fwd", grid=(nb, A_KV_HEADS, nq),
        in_specs=[smem, qspec, kspec, kspec, bias_first, bias_rest], out_specs=[qspec, qspec],
        out_shape=[jax.ShapeDtypeStruct((t_all, A_HEADS * LANES), BF16),
                   jax.ShapeDtypeStruct((t_all, A_HEADS * LANES), F32)],
        compiler_params=_params(("parallel", "parallel", "arbitrary")),
    )(sinks, qa, ka, va, bias, bias)


def _outproj(x2, oa, ob, wo_pad, g2):
    t_all = x2.shape[0]
    tm = min(512, t_all)
    half = A_HEADS * LANES

    def body(x_ref, oa_ref, ob_ref, w_ref, g_ref, h_ref, hn_ref):
        h = x_ref[...] + _dot(oa_ref[...], w_ref[:half, :]) + _dot(ob_ref[...], w_ref[half:, :])
        h_ref[...] = h
        r = lax.rsqrt(jnp.mean(h * h, axis=-1, keepdims=True) + EPS)
        hn_ref[...] = (h * r * g_ref[...]).astype(BF16)

    return _pcall(
        body, name="outproj", grid=(t_all // tm,),
        in_specs=[_rows(tm, D_MODEL), _rows(tm, half), _rows(tm, half), _const((MIXED_P, D_MODEL)),
                  _const((1, D_MODEL))],
        out_specs=[_rows(tm, D_MODEL), _rows(tm, D_MODEL)],
        out_shape=[jax.ShapeDtypeStruct((t_all, D_MODEL), F32), jax.ShapeDtypeStruct((t_all, D_MODEL), BF16)],
        compiler_params=_params(("parallel",)),
    )(x2, oa, ob, wo_pad, g2)


def _mlp_fwd(hn, w_up_blocks, w_down, h, tgt):
    t_all = h.shape[0]
    tm = min(256, t_all)
    nj = D_FF // D_MODEL

    def body(a_ref, wu_ref, wd_ref, h_ref, t_ref, ru_ref, dy_ref, dyb_ref, loss_ref):
        @pl.when(pl.program_id(0) == 0)
        def _():
            loss_ref[...] = jnp.zeros_like(loss_ref)

        a = a_ref[...]
        y = h_ref[...]
        for j in range(nj):
            cols = slice(j * D_MODEL, (j + 1) * D_MODEL)
            ru = jnp.maximum(_dot(a, wu_ref[j]), 0.0)
            ru_ref[:, cols] = ru.astype(BF16)
            y = y + _dot((ru * ru).astype(BF16), wd_ref[cols, :])
        err = y - t_ref[...]
        loss_ref[...] += jnp.sum(err * err)
        dy = err * (1.0 / D_MODEL)
        dy_ref[...] = dy
        dyb_ref[...] = dy.astype(BF16)

    return _pcall(
        body, name="mlp_fwd", grid=(t_all // tm,),
        in_specs=[_rows(tm, D_MODEL), _const((nj, D_MODEL, D_MODEL)), _const((D_FF, D_MODEL)), _rows(tm, D_MODEL),
                  _rows(tm, D_MODEL)],
        out_specs=[_rows(tm, D_FF), _rows(tm, D_MODEL), _rows(tm, D_MODEL), _const((8, LANES))],
        out_shape=[jax.ShapeDtypeStruct((t_all, D_FF), BF16), jax.ShapeDtypeStruct((t_all, D_MODEL), F32),
                   jax.ShapeDtypeStruct((t_all, D_MODEL), BF16), jax.ShapeDtypeStruct((8, LANES), F32)],
        compiler_params=_params(("arbitrary",)),
    )(hn, w_up_blocks, w_down, h, tgt)


def _mlp_bwd_w(dyb, w_down, ru, hn):
    t_all = dyb.shape[0]
    tm = min(512, t_all)
    nj = D_FF // D_MODEL

    def body(dy_ref, w_ref, ru_ref, hn_ref, du_ref, dw_ref):
        @pl.when(pl.program_id(1) == 0)
        def _():
            dw_ref[...] = jnp.zeros_like(dw_ref)

        dy = dy_ref[...]
        ru = ru_ref[...].astype(F32)
        du = (_dot_nt(dy, w_ref[...]) * (2.0 * ru)).astype(BF16)
        du_ref[...] = du
        dw_ref[0] += _dot_tn(hn_ref[...], du)
        dw_ref[1] += _dot_tn((ru * ru).astype(BF16), dy)

    tok = pl.BlockSpec((tm, D_MODEL), lambda j, i: (i, 0))
    blk = pl.BlockSpec((tm, D_MODEL), lambda j, i: (i, j))
    wspec = pl.BlockSpec((2, None, D_MODEL, D_MODEL), lambda j, i: (0, j, 0, 0))
    return _pcall(
        body, name="mlp_bwd_w", grid=(nj, t_all // tm),
        in_specs=[tok, pl.BlockSpec((D_MODEL, D_MODEL), lambda j, i: (j, 0)), blk, tok],
        out_specs=[blk, wspec],
        out_shape=[jax.ShapeDtypeStruct((t_all, D_FF), BF16), jax.ShapeDtypeStruct((2, nj, D_MODEL, D_MODEL), F32)],
        compiler_params=_params(("parallel", "arbitrary")),
    )(dyb, w_down, ru, hn)


def _mlp_dhn(du, w_up_blocks, h, dy, g2):
    t_all = h.shape[0]
    tm = min(256, t_all)

    def body(a_ref, w_ref, h_ref, dy_ref, g_ref, dh_ref, dhb_ref, dg_ref):
        @pl.when(pl.program_id(0) == 0)
        def _():
            dg_ref[...] = jnp.zeros_like(dg_ref)

        dhn = _dot_nt(a_ref[:, :D_MODEL], w_ref[0])
        for j in range(1, D_FF // D_MODEL):
            dhn = dhn + _dot_nt(a_ref[:, j * D_MODEL:(j + 1) * D_MODEL], w_ref[j])
        h = h_ref[...]
        r = lax.rsqrt(jnp.mean(h * h, axis=-1, keepdims=True) + EPS)
        hh = h * r
        dg_ref[...] += jnp.sum(dhn * hh, axis=0, keepdims=True)
        dz = dhn * g_ref[...]
        dh = dy_ref[...] + r * (dz - hh * jnp.mean(dz * hh, axis=-1, keepdims=True))
        dh_ref[...] = dh
        dhb_ref[...] = dh.astype(BF16)

    return _pcall(
        body, name="mlp_dhn", grid=(t_all // tm,),
        in_specs=[_rows(tm, D_FF), _const((D_FF // D_MODEL, D_MODEL, D_MODEL)), _rows(tm, D_MODEL),
                  _rows(tm, D_MODEL), _const((1, D_MODEL))],
        out_specs=[_rows(tm, D_MODEL), _rows(tm, D_MODEL), _const((1, D_MODEL))],
        out_shape=[jax.ShapeDtypeStruct((t_all, D_MODEL), F32), jax.ShapeDtypeStruct((t_all, D_MODEL), BF16),
                   jax.ShapeDtypeStruct((1, D_MODEL), F32)],
        compiler_params=_params(("arbitrary",)),
    )(du, w_up_blocks, h, dy, g2)


def _dmixed(dhb, wo_pad, oa, ob):
    t_all = dhb.shape[0]
    tm = min(512, t_all)
    half = A_HEADS * LANES

    def body(a_ref, w_ref, oa_ref, ob_ref, da_ref, db_ref, delta_ref, dwo_ref):
        @pl.when(pl.program_id(0) == 0)
        def _():
            dwo_ref[...] = jnp.zeros_like(dwo_ref)

        a = a_ref[...]
        d = _dot_nt(a, w_ref[...])
        da_ref[...] = d[:, :half].astype(BF16)
        db_ref[...] = d[:, half:].astype(BF16)
        for h in range(B_HEADS):
            cols = slice(h * LANES, (h + 1) * LANES)
            prod = d[:, half + h * LANES:half + (h + 1) * LANES] * ob_ref[:, cols].astype(F32)
            delta_ref[:, cols] = jnp.broadcast_to(jnp.sum(prod, axis=-1, keepdims=True), (tm, LANES))
        dwo_ref[:half, :] += _dot_tn(oa_ref[...], a)
        dwo_ref[half:, :] += _dot_tn(ob_ref[...], a)

    return _pcall(
        body, name="dmixed", grid=(t_all // tm,),
        in_specs=[_rows(tm, D_MODEL), _const((MIXED_P, D_MODEL)), _rows(tm, half), _rows(tm, half)],
        out_specs=[_rows(tm, half), _rows(tm, half), _rows(tm, half), _const((MIXED_P, D_MODEL))],
        out_shape=[jax.ShapeDtypeStruct((t_all, half), BF16), jax.ShapeDtypeStruct((t_all, half), BF16),
                   jax.ShapeDtypeStruct((t_all, half), F32), jax.ShapeDtypeStruct((MIXED_P, D_MODEL), F32)],
        compiler_params=_params(("arbitrary",)),
    )(dhb, wo_pad, oa, ob)


def _fox_bwd(qb, kb, vb, dob, lse, delta, nb, seq, pair_sums):
    t_all = qb.shape[0]
    tk = min(512, seq)
    nk = seq // tk

    def body(q_ref, k_ref, v_ref, do_ref, lse_ref, delta_ref, pair_ref, dq_ref, dk_ref, dv_ref, got_ref,
             s_ref, dp_ref, p_ref, ds_ref, dk_acc, dv_acc, send_sems, recv_sems):
        kj = pl.program_id(2)
        bh = pl.program_id(0) * B_HEADS + pl.program_id(1)

        @pl.when((bh == 0) & (kj == 0))
        def _():
            for cp in _chip_exchange_copies(pair_ref, got_ref, send_sems, recv_sems):
                cp.start()

        @pl.when(kj == 0)
        def _():
            dq_ref[...] = jnp.zeros_like(dq_ref)

        dk_acc[...] = jnp.zeros_like(dk_acc)
        dv_acc[...] = jnp.zeros_like(dv_acc)
        k = k_ref[...]
        v = v_ref[...]

        def step(i, masked):
            off = pl.multiple_of(i * tk, tk)
            q = q_ref[pl.ds(off, tk), :]
            do = do_ref[pl.ds(off, tk), :]
            s_ref[...] = _dot_nt(q, k)
            dp_ref[...] = _dot_nt(do, v)
            for r in range(0, tk, CHUNK):
                rows = slice(r, r + CHUNK)
                chunk = pl.ds(pl.multiple_of(off + r, CHUNK), CHUNK)
                lse_c = lse_ref[chunk, :]
                delta_c = delta_ref[chunk, :]
                for jt in range(tk // LANES):
                    cols = slice(jt * LANES, (jt + 1) * LANES)
                    p = jnp.exp2(s_ref[rows, cols] - lse_c)
                    if masked:
                        row = r + lax.broadcasted_iota(jnp.int32, (CHUNK, LANES), 0)
                        col = jt * LANES + lax.broadcasted_iota(jnp.int32, (CHUNK, LANES), 1)
                        p = jnp.where(row >= col, p, 0.0)
                    p_ref[rows, cols] = p.astype(BF16)
                    ds_ref[rows, cols] = (p * (dp_ref[rows, cols] - delta_c)).astype(BF16)
            dv_acc[...] += _dot_tn(p_ref[...], do)
            dk_acc[...] += _dot_tn(ds_ref[...], q)
            dq_ref[pl.ds(off, tk), :] += _dot(ds_ref[...], k)

        def unmasked(i, carry):
            step(i, False)
            return carry

        step(kj, True)
        lax.fori_loop(kj + 1, nk, unmasked, 0)
        dk_ref[...] = dk_acc[...]
        dv_ref[...] = dv_acc[...]

        @pl.when((bh == nb * B_HEADS - 1) & (kj == nk - 1))
        def _():
            for cp in _chip_exchange_copies(pair_ref, got_ref, send_sems, recv_sems):
                cp.wait()

    full = pl.BlockSpec((seq, LANES), lambda b, h, j: (b, h))
    tile = pl.BlockSpec((tk, LANES), lambda b, h, j: (b * nk + j, h))
    shp = jax.ShapeDtypeStruct((t_all, B_HEADS * LANES), F32)
    return _pcall(
        body, name="fox_bwd", grid=(nb, B_HEADS, nk),
        in_specs=[full, tile, tile, full, full, full, ANY], out_specs=[full, tile, tile, ANY],
        out_shape=[shp, shp, shp, jax.ShapeDtypeStruct((3,) + pair_sums.shape[1:], pair_sums.dtype)],
        scratch_shapes=[pltpu.VMEM((tk, tk), F32), pltpu.VMEM((tk, tk), F32), pltpu.VMEM((tk, tk), BF16),
                        pltpu.VMEM((tk, tk), BF16), pltpu.VMEM((tk, LANES), F32), pltpu.VMEM((tk, LANES), F32),
                        pltpu.SemaphoreType.DMA((3,)), pltpu.SemaphoreType.DMA((3,))],
        compiler_params=_params(("arbitrary", "arbitrary", "arbitrary")),
    )(qb, kb, vb, dob, lse, delta, pair_sums)


def _swa_bwd(qa, ka, va, oa, doa, lrow, sinks, bias, nb, seq):
    t_all = qa.shape[0]
    tq = min(512, seq)
    nq = seq // tq

    def body(sink_ref, q_ref, k_ref, v_ref, bias0_ref, bias_ref, o_ref, do_ref, l_ref,
             dq_ref, dk_ref, dv_ref, dsink_ref):
        qi = pl.program_id(2)
        sink = _sink_column(sink_ref, pl.program_id(1))

        @pl.when(qi == 0)
        def _():
            dk_ref[...] = jnp.zeros_like(dk_ref)
            dv_ref[...] = jnp.zeros_like(dv_ref)
            dsink_ref[...] = jnp.zeros_like(dsink_ref)

        for a in range(tq // WINDOW):
            t0 = qi * tq + a * WINDOW
            start = pl.multiple_of(jnp.maximum(t0 - WINDOW, 0), WINDOW)
            rows = slice(a * WINDOW, (a + 1) * WINDOW)
            win = pl.ds(start, 2 * WINDOW)
            q = _stack_heads(q_ref, rows)
            k = k_ref[win, :]
            v = v_ref[win, :]
            do = _stack_heads(do_ref, rows)
            lrow_t = jnp.max(_stack_heads(l_ref, rows), axis=-1, keepdims=True)
            p = jnp.exp(_dot_nt(q, k) + (bias0_ref if a == 0 else bias_ref)[...] - lrow_t)
            delta = jnp.sum(do.astype(F32) * _stack_heads(o_ref, rows).astype(F32), axis=-1, keepdims=True)
            ds = (p * (_dot_nt(do, v) - delta)).astype(BF16)
            dq = _dot(ds, k)
            dk_ref[win, :] += _dot_tn(ds, q)
            dv_ref[win, :] += _dot_tn(p.astype(BF16), do)
            sink_term = jnp.exp(sink - lrow_t) * delta
            for j in range(A_GROUP):
                part = slice(j * WINDOW, (j + 1) * WINDOW)
                dq_ref[rows, j * LANES:(j + 1) * LANES] = dq[part]
                dsink_ref[j:j + 1, :] -= jnp.broadcast_to(jnp.sum(sink_term[part], axis=0, keepdims=True), (1, LANES))

    smem, qspec, kspec, bias_first, bias_rest = _swa_specs(nq, tq, seq)
    return _pcall(
        body, name="swa_bwd", grid=(nb, A_KV_HEADS, nq),
        in_specs=[smem, qspec, kspec, kspec, bias_first, bias_rest, qspec, qspec, qspec],
        out_specs=[qspec, kspec, kspec, pl.BlockSpec((None, 8, LANES), lambda b, g, i: (b * A_KV_HEADS + g, 0, 0))],
        out_shape=[jax.ShapeDtypeStruct((t_all, A_HEADS * LANES), F32),
                   jax.ShapeDtypeStruct((t_all, A_KV_HEADS * LANES), F32),
                   jax.ShapeDtypeStruct((t_all, A_KV_HEADS * LANES), F32),
                   jax.ShapeDtypeStruct((nb * A_KV_HEADS, 8, LANES), F32)],
        compiler_params=_params(("parallel", "parallel", "arbitrary")),
    )(sinks, qa, ka, va, bias, bias, oa, doa, lrow)


def _dproj_dx(pre, dqa, dka, dqb, dkb, dva, dvb, z, x2, dh, gain_row, w_pad_t, g1, seq):
    t_all = pre.shape[0]
    tm = min(256, seq)
    nt = t_all // tm
    tiles_per_seq = seq // tm
    triu = _tri(tm, True)
    sel = _dc_select()

    def body(pre_ref, dqa_ref, dka_ref, dqb_ref, dkb_ref, dva_ref, dvb_ref, z_ref, x_ref, dh_ref, gain_ref, triu_ref,
             sel_ref, w_ref, g_ref, dproj_ref, small_ref, dx_ref, dg_ref, carry_ref):
        i = pl.program_id(0)

        @pl.when(i == 0)
        def _():
            small_ref[...] = jnp.zeros_like(small_ref)
            dg_ref[...] = jnp.zeros_like(dg_ref)

        @pl.when(i % tiles_per_seq == 0)
        def _():
            carry_ref[...] = jnp.zeros_like(carry_ref)

        def norm_bwd(g, dhat):
            cols = slice(g * LANES, (g + 1) * LANES)
            p = pre_ref[:, cols].astype(F32)
            rr = lax.rsqrt(jnp.sum(p * p, axis=-1, keepdims=True) * (1.0 / HEAD_DIM) + EPS)
            n = p * rr
            dz = dhat * gain_ref[:, cols]
            dproj_ref[:, cols] = (rr * (dz - n * (jnp.sum(dz * n, axis=-1, keepdims=True) * (1.0 / HEAD_DIM)))
                                  ).astype(BF16)
            return jnp.sum(dhat * n, axis=0, keepdims=True)

        def group_sum(g0, d_ref, count, scale):
            acc = jnp.zeros((1, LANES), F32)
            for h in range(count):
                d = d_ref[:, h * LANES:(h + 1) * LANES]
                acc = acc + norm_bwd(g0 + h, d * scale if scale != 1.0 else d)
            return acc

        small_ref[0:1, :] += group_sum(G_QA, dqa_ref, A_HEADS, SCALE)
        small_ref[1:2, :] += group_sum(G_KA, dka_ref, A_KV_HEADS, 1.0)
        small_ref[2:3, :] += group_sum(G_QB, dqb_ref, B_HEADS, SCALE)
        small_ref[3:4, :] += group_sum(G_KB, dkb_ref, B_HEADS, LN2)
        dproj_ref[:, G_VA * LANES:G_VB * LANES] = dva_ref[...].astype(BF16)
        dproj_ref[:, G_VB * LANES:G_F * LANES] = dvb_ref[...].astype(BF16)

        dc = jnp.zeros((tm, LANES), F32)
        for piece_q, piece_k in zip(_split3(dqb_ref[...]), _split3(dkb_ref[...])):
            dc = dc + _dot(jnp.concatenate([piece_q, piece_k], axis=1), sel_ref[...])
        dlf = _dot_exact(triu_ref[...], dc) + carry_ref[...]
        carry_ref[...] += jnp.sum(dc, axis=0, keepdims=True)
        dz = dlf / (1.0 + jnp.exp(z_ref[...]))
        small_ref[4:5, :] += jnp.sum(dz, axis=0, keepdims=True)
        dproj_ref[:, G_F * LANES:(G_F + 1) * LANES] = dz.astype(BF16)
        dproj_ref[:, (G_F + 1) * LANES:] = jnp.zeros((tm, LANES), BF16)

        dxn = _dot(dproj_ref[...], w_ref[...])
        x = x_ref[...]
        r = lax.rsqrt(jnp.mean(x * x, axis=-1, keepdims=True) + EPS)
        xh = x * r
        dg_ref[...] += jnp.sum(dxn * xh, axis=0, keepdims=True)
        dxz = dxn * g_ref[...]
        dx_ref[...] = dh_ref[...] + r * (dxz - xh * jnp.mean(dxz * xh, axis=-1, keepdims=True))

    def rev(n):
        return pl.BlockSpec((tm, n), lambda i: (nt - 1 - i, 0))

    return _pcall(
        body, name="dproj_dx", grid=(nt,),
        in_specs=[rev(N_NORM_GROUPS * LANES), rev(A_HEADS * LANES), rev(A_KV_HEADS * LANES), rev(B_HEADS * LANES),
                  rev(B_HEADS * LANES), rev(A_KV_HEADS * LANES), rev(B_HEADS * LANES), rev(LANES), rev(D_MODEL),
                  rev(D_MODEL), _const((1, NP)), _const((tm, tm)), _const(sel.shape), _const((NP, D_MODEL)),
                  _const((1, D_MODEL))],
        out_specs=[rev(NP), _const((8, LANES)), rev(D_MODEL), _const((1, D_MODEL))],
        out_shape=[jax.ShapeDtypeStruct((t_all, NP), BF16), jax.ShapeDtypeStruct((8, LANES), F32),
                   jax.ShapeDtypeStruct((t_all, D_MODEL), F32), jax.ShapeDtypeStruct((1, D_MODEL), F32)],
        scratch_shapes=[pltpu.VMEM((1, LANES), F32)],
        compiler_params=_params(("arbitrary",)),
    )(pre, dqa, dka, dqb, dkb, dva, dvb, z, x2, dh, gain_row, triu, sel, w_pad_t, g1)


def _dwin(dproj, xn):
    t_all = xn.shape[0]
    tt = min(512, t_all)
    half = NP // 2

    def body(a_ref, b_ref, o_ref):
        @pl.when(pl.program_id(1) == 0)
        def _():
            o_ref[...] = jnp.zeros_like(o_ref)

        o_ref[...] += _dot_tn(a_ref[...], b_ref[...])

    return _pcall(
        body, name="dwin", grid=(2, t_all // tt),
        in_specs=[pl.BlockSpec((tt, half), lambda j, t: (t, j)), pl.BlockSpec((tt, D_MODEL), lambda j, t: (t, 0))],
        out_specs=pl.BlockSpec((half, D_MODEL), lambda j, t: (j, 0)),
        out_shape=jax.ShapeDtypeStruct((NP, D_MODEL), F32),
        compiler_params=_params(("parallel", "arbitrary")),
    )(dproj, xn)


ANY = pl.BlockSpec(memory_space=pl.ANY)


def _place():
    return lax.axis_index("x"), lax.axis_index("y"), lax.axis_index("c")


class _Gather:
    def __init__(self, srcs, outs, send_sems, recv_sems, local_sems):
        self.srcs, self.outs = srcs, outs
        self.send_sems, self.recv_sems, self.local_sems = send_sems, recv_sems, local_sems
        x, y, c = _place()
        self.c = c
        self.me, self.sibling = (x, y, c), (x, y, 1 - c)
        self.chips = [(1 - x, y), (x, 1 - y), (1 - x, 1 - y)]

    def _rows(self, a, px, py, pc):
        m = self.srcs[a].shape[0]
        return self.outs[a].at[pl.ds((4 * px + 2 * py + pc) * m, m), :]

    def _copy(self, a, k, block, to, from_src=False):
        return pltpu.make_async_remote_copy(
            src_ref=self.srcs[a] if from_src else self._rows(a, *block), dst_ref=self._rows(a, *block),
            send_sem=self.send_sems.at[k, a], recv_sem=self.recv_sems.at[k, a], device_id=to, device_id_type=MESH)

    def _own(self, a):
        return pltpu.make_async_copy(self.srcs[a], self._rows(a, *self.me), self.local_sems.at[a])

    def start(self):
        for a in range(len(self.srcs)):
            self._own(a).start()
            self._copy(a, 0, self.me, self.sibling, from_src=True).start()
            for j, chip in enumerate(self.chips):
                self._copy(a, 1 + j, self.me, (*chip, self.c), from_src=True).start()

    def forward(self):
        for a in range(len(self.srcs)):
            for j, chip in enumerate(self.chips):
                self._copy(a, 1 + j, (*chip, self.c), self.me).wait_recv()
                self._copy(a, 4 + j, (*chip, self.c), self.sibling).start()

    def finish(self):
        for a in range(len(self.srcs)):
            self._copy(a, 0, self.sibling, self.me).wait_recv()
            for j, chip in enumerate(self.chips):
                self._copy(a, 4 + j, (*chip, 1 - self.c), self.me).wait_recv()
            self._copy(a, 0, self.me, self.sibling, from_src=True).wait_send()
            for j, chip in enumerate(self.chips):
                self._copy(a, 1 + j, self.me, (*chip, self.c), from_src=True).wait_send()
                self._copy(a, 4 + j, (*chip, self.c), self.sibling).wait_send()
            self._own(a).wait()


def _gather_scratch(n_arrays):
    return [pltpu.SemaphoreType.DMA((7, n_arrays)), pltpu.SemaphoreType.DMA((7, n_arrays)),
            pltpu.SemaphoreType.DMA((n_arrays,))]


def _allgather_halves(mine):
    m_per, n = mine.shape

    def body(x_ref, out_ref, send_sems, recv_sems, local_sems):
        gather = _Gather((x_ref,), (out_ref,), send_sems, recv_sems, local_sems)
        gather.start()
        gather.forward()
        gather.finish()

    return _pcall(
        body, name="allgather_w_in",
        out_shape=jax.ShapeDtypeStruct((8 * m_per, n), mine.dtype),
        in_specs=[ANY], out_specs=ANY, scratch_shapes=_gather_scratch(1),
    )(mine)


def _rs_pair_exchange(g, name):
    def body(g_ref, out_ref, send_sem, recv_sem):
        x, y, c = _place()
        cp = pltpu.make_async_remote_copy(
            src_ref=g_ref.at[1 - c], dst_ref=out_ref, send_sem=send_sem, recv_sem=recv_sem,
            device_id=(x, y, 1 - c), device_id_type=MESH)
        cp.start()
        cp.wait()

    return _pcall(
        body, name=name, out_shape=jax.ShapeDtypeStruct(g.shape[1:], F32),
        in_specs=[ANY], out_specs=ANY, scratch_shapes=[pltpu.SemaphoreType.DMA, pltpu.SemaphoreType.DMA],
    )(g)


def _rs_pair_add(g, got, c_idx, name):
    rows = g.shape[2]

    def body(c_ref, a_ref, b_ref, o_ref, ob_ref):
        pair = a_ref[...] + b_ref[...]
        o_ref[...] = pair
        ob_ref[...] = pair.astype(BF16)

    blk = pl.BlockSpec((None, rows, D_MODEL), lambda s, c_ref: (s, 0, 0))
    return _pcall(
        body, name=name,
        grid_spec=pltpu.PrefetchScalarGridSpec(
            num_scalar_prefetch=1, grid=(N_CHIPS,),
            in_specs=[pl.BlockSpec((None, None, rows, D_MODEL), lambda s, c_ref: (c_ref[0], s, 0, 0)), blk],
            out_specs=[blk, blk]),
        out_shape=[jax.ShapeDtypeStruct((N_CHIPS, rows, D_MODEL), F32),
                   jax.ShapeDtypeStruct((N_CHIPS, rows, D_MODEL), BF16)],
        compiler_params=_params(("parallel",)),
    )(c_idx, g, got)


def _chip_exchange_copies(p_ref, out_ref, send_sems, recv_sems):
    x, y, c = _place()
    chips = [(1 - x, y), (x, 1 - y), (1 - x, 1 - y)]
    return [pltpu.make_async_remote_copy(
        src_ref=p_ref.at[2 * cx + cy], dst_ref=out_ref.at[j], send_sem=send_sems.at[j], recv_sem=recv_sems.at[j],
        device_id=(cx, cy, c), device_id_type=MESH) for j, (cx, cy) in enumerate(chips)]


def _rs_chip_exchange(p4, name):
    def body(p_ref, out_ref, send_sems, recv_sems):
        cps = _chip_exchange_copies(p_ref, out_ref, send_sems, recv_sems)
        for cp in cps:
            cp.start()
        for cp in cps:
            cp.wait()

    return _pcall(
        body, name=name, out_shape=jax.ShapeDtypeStruct((3,) + p4.shape[1:], p4.dtype),
        in_specs=[ANY], out_specs=ANY,
        scratch_shapes=[pltpu.SemaphoreType.DMA((3,)), pltpu.SemaphoreType.DMA((3,))],
    )(p4)


def _rs_chip_add(p4, got, sc_idx, name):
    rows = p4.shape[1]
    tr = next(rows // n for n in (8, 7, 6, 5, 4, 3, 2, 1) if rows % n == 0 and (rows // n) % 16 == 0)

    def body(sc_ref, a_ref, b_ref, o_ref):
        o_ref[...] = ((a_ref[...] + b_ref[0].astype(F32)) + b_ref[1].astype(F32)) + b_ref[2].astype(F32)

    return _pcall(
        body, name=name,
        grid_spec=pltpu.PrefetchScalarGridSpec(
            num_scalar_prefetch=1, grid=(rows // tr,),
            in_specs=[pl.BlockSpec((None, tr, D_MODEL), lambda i, sc_ref: (sc_ref[0], i, 0)),
                      pl.BlockSpec((3, tr, D_MODEL), lambda i, sc_ref: (0, i, 0))],
            out_specs=pl.BlockSpec((None, tr, D_MODEL), lambda i, sc_ref: (sc_ref[1], i, 0))),
        out_shape=jax.ShapeDtypeStruct((2, rows, D_MODEL), F32),
        compiler_params=_params(("parallel",)),
    )(sc_idx, p4, got)


def _rs_pair_share(halves, name):
    def body(r_ref, out_ref, send_sem, recv_sem):
        x, y, c = _place()
        cp = pltpu.make_async_remote_copy(
            src_ref=r_ref.at[c], dst_ref=out_ref.at[c], send_sem=send_sem, recv_sem=recv_sem,
            device_id=(x, y, 1 - c), device_id_type=MESH)
        cp.start()
        cp.wait()

    return _pcall(
        body, name=name, out_shape=jax.ShapeDtypeStruct(halves.shape, F32),
        in_specs=[ANY], out_specs=ANY, input_output_aliases={0: 0},
        scratch_shapes=[pltpu.SemaphoreType.DMA, pltpu.SemaphoreType.DMA],
    )(halves)


def _adam(w, g, m, v):
    m2 = ADAM_B1 * m + (1.0 - ADAM_B1) * g
    v2 = ADAM_B2 * v + (1.0 - ADAM_B2) * (g * g)
    m_hat = m2 / (1.0 - ADAM_B1 ** ADAM_STEP)
    v_hat = v2 / (1.0 - ADAM_B2 ** ADAM_STEP)
    return -ADAM_LR * (m_hat / (jnp.sqrt(v_hat) + ADAM_EPS) + ADAM_WD * w), m2, v2


def _small_allreduce_adamw(part, w, m, v):
    pieces = ((0, 8, LANES), (20, 1, B_HEADS), (16, 1, HEAD_DIM), (17, 1, HEAD_DIM), (21, 1, A_HEADS),
              (18, 1, HEAD_DIM), (19, 1, HEAD_DIM), (8, 8, LANES))

    def body(p_ref, w_ref, m_ref, v_ref, *rest):
        outs, (loss_ref, buf, stage, send_sems, recv_sems) = rest[:4 * len(pieces)], rest[4 * len(pieces):]
        x, y, c = _place()
        me = 4 * x + 2 * y + c
        cps = []
        for k in range(1, 8):
            peer = (1 - x if k & 4 else x, 1 - y if k & 2 else y, 1 - c if k & 1 else c)
            cps.append(pltpu.make_async_remote_copy(
                src_ref=p_ref, dst_ref=buf.at[me], send_sem=send_sems.at[k - 1], recv_sem=recv_sems.at[k - 1],
                device_id=peer, device_id_type=MESH))
        for cp in cps:
            cp.start()
        buf[me] = p_ref[...]
        for cp in cps:
            cp.wait()
        g = buf[0]
        for k in range(1, 8):
            g = g + buf[k]
        for kind, packed in enumerate((g,) + _adam(w_ref[...], g, m_ref[...], v_ref[...])):
            stage[...] = packed
            if kind == 0:
                loss_ref[...] = stage[ROW_LOSS:ROW_LOSS + 1, :]
            for i, (row, rows, lanes) in enumerate(pieces):
                outs[kind * len(pieces) + i][...] = stage[row:row + rows, 0:lanes]

    vm = pl.BlockSpec(memory_space=pltpu.VMEM)
    shapes = [jax.ShapeDtypeStruct((rows, lanes), F32) for _ in range(4) for _, rows, lanes in pieces]
    shapes.append(jax.ShapeDtypeStruct((1, LANES), F32))
    res = _pcall(
        body, name="small_allreduce_adamw",
        out_shape=shapes, in_specs=[vm, vm, vm, vm], out_specs=[vm] * len(shapes),
        scratch_shapes=[pltpu.VMEM((8, SMALL_ROWS, LANES), F32), pltpu.VMEM((SMALL_ROWS, LANES), F32),
                        pltpu.SemaphoreType.DMA((7,)), pltpu.SemaphoreType.DMA((7,))],
    )(part, w, m, v)
    flat = [r.reshape(r.size) for r in res[:-1]]
    n = len(pieces)
    return [flat[k * n:(k + 1) * n] for k in range(4)], res[-1][0, 0]


def _adamw(w, g, m, v, name):
    rows, cols = w.shape
    tr = min(256, rows)

    def body(w_ref, g_ref, m_ref, v_ref, d_ref, m2_ref, v2_ref):
        d_ref[...], m2_ref[...], v2_ref[...] = _adam(w_ref[...], g_ref[...], m_ref[...], v_ref[...])

    spec = _rows(tr, cols)
    shp = jax.ShapeDtypeStruct((rows, cols), F32)
    return _pcall(
        body, name=name, grid=(rows // tr,), in_specs=[spec] * 4, out_specs=[spec] * 3, out_shape=[shp] * 3,
        compiler_params=_params(("parallel",)),
    )(w, g, m, v)


def _pad_lanes(v):
    return jnp.pad(v, (0, LANES - v.shape[0]))


def _pad_head_rows(w_t, heads):
    n = w_t.shape[1]
    return jnp.pad(w_t.reshape(heads, HEAD_DIM, n), ((0, 0), (0, LANES - HEAD_DIM), (0, 0))).reshape(heads * LANES, n)


def _unpad_head_rows(w_t, heads):
    n = w_t.shape[1]
    return w_t.reshape(heads, LANES, n)[:, :HEAD_DIM].reshape(heads * HEAD_DIM, n)


def _in_rows_pad(w_in_t):
    qa, ka, va, qb, kb, vb, f = jnp.split(w_in_t, [512, 640, 768, 1280, 1792, 2304], axis=0)
    f = jnp.pad(f, ((0, 2 * LANES - B_HEADS), (0, 0)))
    return jnp.concatenate([_pad_head_rows(qa, 8), _pad_head_rows(ka, 2), _pad_head_rows(qb, 8),
                            _pad_head_rows(kb, 8), _pad_head_rows(va, 2), _pad_head_rows(vb, 8), f], axis=0)


def _in_rows_unpad(d):
    qa = _unpad_head_rows(d[G_QA * LANES:G_KA * LANES], 8)
    ka = _unpad_head_rows(d[G_KA * LANES:G_QB * LANES], 2)
    qb = _unpad_head_rows(d[G_QB * LANES:G_KB * LANES], 8)
    kb = _unpad_head_rows(d[G_KB * LANES:G_VA * LANES], 8)
    va = _unpad_head_rows(d[G_VA * LANES:G_VB * LANES], 2)
    vb = _unpad_head_rows(d[G_VB * LANES:G_F * LANES], 8)
    f = d[G_F * LANES:G_F * LANES + B_HEADS]
    return jnp.concatenate([qa, ka, va, qb, kb, vb, f], axis=0)


def _pack_small(g1, bf, qa, ka, sk, qb, kb, g2, loss_row):
    rows = [g1.reshape(8, LANES), g2.reshape(8, LANES)]
    rows += [_pad_lanes(t)[None] for t in (qa, ka, qb, kb, bf, sk)]
    rows += [loss_row, jnp.zeros((1, LANES), F32)]
    return jnp.concatenate(rows, axis=0)


def kernel(x, attn_norm_g, w_in, b_forget, q_norm_a, k_norm_a, sink_logits, q_norm_b, k_norm_b, w_out, mlp_norm_g, w_up, w_down, loss_target, m_attn_norm_g, m_w_in, m_b_forget, m_q_norm_a, m_k_norm_a, m_sink_logits, m_q_norm_b, m_k_norm_b, m_w_out, m_mlp_norm_g, m_w_up, m_w_down, v_attn_norm_g, v_w_in, v_b_forget, v_q_norm_a, v_k_norm_a, v_sink_logits, v_q_norm_b, v_k_norm_b, v_w_out, v_mlp_norm_g, v_w_up, v_w_down):
    nb, seq, _ = x.shape
    t_all = nb * seq
    c_idx = lax.axis_index("c")
    s_idx = 2 * lax.axis_index("x") + lax.axis_index("y")

    def my_half(a):
        halves = a.astype(BF16).reshape(2, a.shape[0] // 2, a.shape[1])
        return lax.dynamic_slice_in_dim(halves, c_idx, 1, axis=0)[0]

    w_in_shard_t = jnp.pad(w_in.T, ((0, IN_SHARD_P - IN_SHARD), (0, 0)))
    gathered_in = _allgather_halves(my_half(w_in_shard_t)).reshape(N_CHIPS, IN_SHARD_P, D_MODEL)
    w_pad_t = _in_rows_pad(gathered_in[:, :IN_SHARD].reshape(IN_WIDTH, D_MODEL))

    ones = jnp.ones((LANES,), F32)
    gain_row = jnp.concatenate(
        [jnp.tile(_pad_lanes(q_norm_a), 8), jnp.tile(_pad_lanes(k_norm_a), 2), jnp.tile(_pad_lanes(q_norm_b), 8),
         jnp.tile(_pad_lanes(k_norm_b), 8), jnp.tile(ones, N_GROUPS - N_NORM_GROUPS)])[None]
    b_row = _pad_lanes(b_forget)[None]
    g1 = attn_norm_g[None]
    g2 = mlp_norm_g[None]
    slopes = jnp.exp2(-(8.0 / A_HEADS) * (jnp.arange(A_HEADS, dtype=F32) + 1.0))

    x2 = x.reshape(t_all, D_MODEL)
    tgt = loss_target.reshape(t_all, D_MODEL)

    (xn, pre, qa, ka, va, qb, kb, vb, z), (w_out_g, w_up_g, w_down_f) = _inproj(
        x2, g1, w_pad_t, gain_row, b_row, seq, [my_half(w_out), my_half(w_up), my_half(w_down)])
    wo_pad = _pad_head_rows(w_out_g, A_HEADS + B_HEADS)
    w_up_blocks = w_up_g.reshape(N_CHIPS, D_MODEL, D_MODEL)
    swa_bias = _swa_bias(slopes)
    oa, la = _swa_fwd(qa, ka, va, sink_logits, swa_bias, nb, seq)
    ob, lse = _fox_fwd(qb, kb, vb, nb, seq)
    h, hn = _outproj(x2, oa, ob, wo_pad, g2)
    ru, dy, dyb, loss_acc = _mlp_fwd(hn, w_up_blocks, w_down_f, h, tgt)

    du, d_w_mlp = _mlp_bwd_w(dyb, w_down_f, ru, hn)
    c_arg = c_idx.reshape(1).astype(jnp.int32)
    sc_arg = jnp.stack([s_idx, c_idx]).astype(jnp.int32)
    pair_m, pair_m_bf = _rs_pair_add(d_w_mlp, _rs_pair_exchange(d_w_mlp, "rs_pair_exchange_mlp"), c_arg,
                                     "rs_pair_add_mlp")
    dh, dhb, d_g2 = _mlp_dhn(du, w_up_blocks, h, dy, g2)
    doa, dob, delta_b, d_wo = _dmixed(dhb, wo_pad, oa, ob)
    dqb, dkb, dvb, got_m = _fox_bwd(qb, kb, vb, dob, lse, delta_b, nb, seq, pair_m_bf)
    red_m = _rs_pair_share(_rs_chip_add(pair_m, got_m, sc_arg, "rs_chip_add_mlp"), "rs_pair_share_mlp")
    g_w_up, g_w_down = red_m[0], red_m[1]
    dqa, dka, dva, dsink = _swa_bwd(qa, ka, va, oa, doa, la, sink_logits, swa_bias, nb, seq)
    dproj, small, grad_x, d_g1 = _dproj_dx(pre, dqa, dka, dqb, dkb, dva, dvb, z, x2, dh, gain_row, w_pad_t, g1, seq)
    d_w_in_t = _dwin(dproj, xn)

    d_w_out = _unpad_head_rows(d_wo, A_HEADS + B_HEADS)
    g_att = jnp.concatenate([
        jnp.pad(_in_rows_unpad(d_w_in_t).reshape(N_CHIPS, IN_SHARD, D_MODEL),
                ((0, 0), (0, IN_SHARD_P - IN_SHARD), (0, 0))),
        d_w_out.reshape(N_CHIPS, D_MODEL // N_CHIPS, D_MODEL)], axis=1)
    g_att = jnp.stack([g_att[:, :R_ATT // 2], g_att[:, R_ATT // 2:]])
    pair_a, pair_a_bf = _rs_pair_add(g_att, _rs_pair_exchange(g_att, "rs_pair_exchange_att"), c_arg, "rs_pair_add_att")
    got_a = _rs_chip_exchange(pair_a_bf, "rs_chip_exchange_att")
    red_a = _rs_pair_share(_rs_chip_add(pair_a, got_a, sc_arg, "rs_chip_add_att"), "rs_pair_share_att")
    red_a = red_a.reshape(R_ATT, D_MODEL)
    g_w_in = red_a[:IN_SHARD].T
    g_w_out = red_a[IN_SHARD_P:]

    loss_row = loss_acc[0:1] * (0.5 / D_MODEL)
    d_sink = dsink[:, :A_GROUP, 0].reshape(nb, A_HEADS).sum(axis=0)
    part = _pack_small(d_g1[0], small[4, :B_HEADS], small[0, :HEAD_DIM], small[1, :HEAD_DIM], d_sink,
                       small[2, :HEAD_DIM], small[3, :HEAD_DIM], d_g2[0], loss_row)
    zero_row = jnp.zeros((1, LANES), F32)
    smalls = lambda t: _pack_small(*t, zero_row)
    w_small = smalls((attn_norm_g, b_forget, q_norm_a, k_norm_a, sink_logits, q_norm_b, k_norm_b, mlp_norm_g))
    m_small = smalls((m_attn_norm_g, m_b_forget, m_q_norm_a, m_k_norm_a, m_sink_logits, m_q_norm_b, m_k_norm_b,
                      m_mlp_norm_g))
    v_small = smalls((v_attn_norm_g, v_b_forget, v_q_norm_a, v_k_norm_a, v_sink_logits, v_q_norm_b, v_k_norm_b,
                      v_mlp_norm_g))
    (g_s, d_s, m_s, v_s), loss = _small_allreduce_adamw(part, w_small, m_small, v_small)

    big = {}
    for name, w, g, m, v in (("adamw_w_in", w_in, g_w_in, m_w_in, v_w_in),
                             ("adamw_w_out", w_out, g_w_out, m_w_out, v_w_out),
                             ("adamw_w_up", w_up, g_w_up, m_w_up, v_w_up),
                             ("adamw_w_down", w_down, g_w_down, m_w_down, v_w_down)):
        big[name] = (g,) + tuple(_adamw(w, g, m, v, name))

    def assemble(k, s):
        return (s[0], big["adamw_w_in"][k], s[1], s[2], s[3], s[4], s[5], s[6], big["adamw_w_out"][k], s[7],
                big["adamw_w_up"][k], big["adamw_w_down"][k])

    return (loss, grad_x.reshape(nb, seq, D_MODEL), *assemble(0, g_s), *assemble(1, d_s), *assemble(2, m_s),
            *assemble(3, v_s))
```

```python
import functools

import numpy as np
import jax
import jax.numpy as jnp
from jax import lax
from jax.experimental import pallas as pl
from jax.experimental.pallas import tpu as pltpu

F32 = jnp.float32
BF16 = jnp.bfloat16

D_MODEL = 1024
HEAD_DIM = 64
LANES = 128
A_HEADS = 8
A_KV_HEADS = 2
A_GROUP = A_HEADS // A_KV_HEADS
B_HEADS = 8
WINDOW = 128
D_FF = 4096
IN_WIDTH = 2312
EPS = 1e-6
SCALE = 0.125
LOG2E = 1.4426950408889634
LN2 = 0.6931471805599453
CHUNK = 32
NEG = -1e30

G_QA, G_KA, G_QB, G_KB, G_VA, G_VB, G_F = 0, 8, 10, 18, 26, 28, 36
N_NORM_GROUPS = 26
N_GROUPS = 38
NP = N_GROUPS * LANES
MIXED_P = (A_HEADS + B_HEADS) * LANES

N_CHIPS = 4
IN_SHARD = IN_WIDTH // N_CHIPS
IN_SHARD_P = 608
R_ATT = IN_SHARD_P + D_MODEL // N_CHIPS

SMALL_ROWS = 24
ROW_LOSS = 22

ADAM_LR = 0.001
ADAM_B1 = 0.9
ADAM_B2 = 0.999
ADAM_EPS = 1e-08
ADAM_WD = 0.01
ADAM_STEP = 10

VMEM_LIMIT = 52 * 1024 * 1024
MESH = pl.DeviceIdType.MESH


def _pcall(body, **kw):
    return pl.pallas_call(body, **kw)


def _params(sem=None):
    return pltpu.CompilerParams(dimension_semantics=sem, vmem_limit_bytes=VMEM_LIMIT)


def _dot(a, b):
    return jnp.dot(a, b, preferred_element_type=F32)


def _dot_nt(a, b):
    return lax.dot_general(a, b, (((1,), (1,)), ((), ())), preferred_element_type=F32)


def _dot_tn(a, b):
    return lax.dot_general(a, b, (((0,), (0,)), ((), ())), preferred_element_type=F32)


def _split3(x):
    hi = x.astype(BF16)
    r1 = x - hi.astype(F32)
    mid = r1.astype(BF16)
    lo = (r1 - mid.astype(F32)).astype(BF16)
    return hi, mid, lo


def _dot_exact(mat, x):
    hi, mid, lo = _split3(x)
    return _dot(mat, lo) + _dot(mat, mid) + _dot(mat, hi)


def _const(shape):
    zeros = (0,) * len(shape)
    return pl.BlockSpec(shape, lambda *_: zeros)


def _rows(tm, n):
    return pl.BlockSpec((tm, n), lambda i: (i, 0))


def _aug_select():
    e = np.zeros((3 * LANES, 2 * B_HEADS * LANES), np.float32)
    for j in range(3):
        for h in range(B_HEADS):
            e[j * LANES + h, h * LANES + HEAD_DIM + j] = 1.0
            e[j * LANES + h, (B_HEADS + h) * LANES + HEAD_DIM + 3 + j] = -1.0
    return jnp.asarray(e, BF16)


def _dc_select():
    e = np.zeros((2 * B_HEADS * LANES, LANES), np.float32)
    for h in range(B_HEADS):
        e[h * LANES + HEAD_DIM, h] = 1.0
        e[(B_HEADS + h) * LANES + HEAD_DIM + 3, h] = -1.0
    return jnp.asarray(e, BF16)


def _tri(n, upper):
    t = np.tril(np.ones((n, n), np.float32))
    return jnp.asarray(t.T if upper else t, BF16)


def _inproj(x2, g1, w_pad_t, gain_row, b_row, seq, later_weights):
    t_all = x2.shape[0]
    tm = min(256, seq)
    n_steps = t_all // tm
    forward_step = max(n_steps - 2, 0)
    tiles_per_seq = seq // tm
    tri = _tri(tm, False)
    esel = _aug_select()
    n_later = len(later_weights)

    def body(x_ref, g_ref, w_ref, gain_ref, b_ref, tri_ref, e_ref, *rest):
        later_src, rest = rest[:n_later], rest[n_later:]
        xn_ref, pre_ref, qa_ref, ka_ref, va_ref, qb_ref, kb_ref, vb_ref, z_ref = rest[:9]
        later_out, (carry_ref, send_sems, recv_sems, local_sems) = rest[9:9 + n_later], rest[9 + n_later:]
        i = pl.program_id(0)
        gather = _Gather(later_src, later_out, send_sems, recv_sems, local_sems)

        @pl.when(i == 0)
        def _():
            gather.start()

        @pl.when(i == forward_step)
        def _():
            gather.forward()

        @pl.when(i % tiles_per_seq == 0)
        def _():
            carry_ref[...] = jnp.zeros_like(carry_ref)

        x = x_ref[...]
        r = lax.rsqrt(jnp.mean(x * x, axis=-1, keepdims=True) + EPS)
        xn = (x * r * g_ref[...]).astype(BF16)
        xn_ref[...] = xn
        proj = _dot_nt(xn, w_ref[...])
        pre_ref[...] = proj[:, :N_NORM_GROUPS * LANES].astype(BF16)
        lane = lax.broadcasted_iota(jnp.int32, (tm, LANES), 1)

        z = proj[:, G_F * LANES:(G_F + 1) * LANES] + b_ref[...]
        z_ref[...] = z
        lf = jnp.minimum(z, 0.0) - jnp.log(1.0 + jnp.exp(-jnp.abs(z)))
        lf = jnp.where(lane < B_HEADS, lf, 0.0)
        c = _dot_exact(tri_ref[...], lf) + carry_ref[...]
        carry_ref[...] += jnp.sum(lf, axis=0, keepdims=True)
        aug = _dot(jnp.concatenate(_split3(c * LOG2E), axis=1), e_ref[...])

        def hnorm(g):
            p = proj[:, g * LANES:(g + 1) * LANES]
            rr = lax.rsqrt(jnp.sum(p * p, axis=-1, keepdims=True) * (1.0 / HEAD_DIM) + EPS)
            return p * rr * gain_ref[:, g * LANES:(g + 1) * LANES]

        ones_q = jnp.where((lane >= HEAD_DIM + 3) & (lane < HEAD_DIM + 6), 1.0, 0.0)
        ones_k = jnp.where((lane >= HEAD_DIM) & (lane < HEAD_DIM + 3), 1.0, 0.0)
        for h in range(A_HEADS):
            qa_ref[:, h * LANES:(h + 1) * LANES] = (hnorm(G_QA + h) * SCALE).astype(BF16)
        for h in range(A_KV_HEADS):
            ka_ref[:, h * LANES:(h + 1) * LANES] = hnorm(G_KA + h).astype(BF16)
        for h in range(B_HEADS):
            qb_ref[:, h * LANES:(h + 1) * LANES] = (
                hnorm(G_QB + h) * (SCALE * LOG2E) + aug[:, h * LANES:(h + 1) * LANES] + ones_q).astype(BF16)
            kb_ref[:, h * LANES:(h + 1) * LANES] = (
                hnorm(G_KB + h) + aug[:, (B_HEADS + h) * LANES:(B_HEADS + h + 1) * LANES] + ones_k).astype(BF16)
        va_ref[...] = proj[:, G_VA * LANES:G_VB * LANES].astype(BF16)
        one_v = jnp.where(lane == HEAD_DIM, 1.0, 0.0)
        for h in range(B_HEADS):
            cols = slice((G_VB + h) * LANES, (G_VB + h + 1) * LANES)
            vb_ref[:, h * LANES:(h + 1) * LANES] = (proj[:, cols] + one_v).astype(BF16)

        @pl.when(i == n_steps - 1)
        def _():
            gather.finish()

    widths = [(D_MODEL, BF16), (N_NORM_GROUPS * LANES, BF16), (A_HEADS * LANES, BF16), (A_KV_HEADS * LANES, BF16),
              (A_KV_HEADS * LANES, BF16), (B_HEADS * LANES, BF16), (B_HEADS * LANES, BF16), (B_HEADS * LANES, BF16),
              (LANES, F32)]
    res = _pcall(
        body, name="inproj", grid=(n_steps,),
        in_specs=[_rows(tm, D_MODEL), _const((1, D_MODEL)), _const((NP, D_MODEL)), _const((1, NP)),
                  _const((1, LANES)), _const((tm, tm)), _const(esel.shape)] + [ANY] * n_later,
        out_specs=[_rows(tm, w) for w, _ in widths] + [ANY] * n_later,
        out_shape=[jax.ShapeDtypeStruct((t_all, w), dt) for w, dt in widths]
        + [jax.ShapeDtypeStruct((8 * w.shape[0], w.shape[1]), w.dtype) for w in later_weights],
        scratch_shapes=[pltpu.VMEM((1, LANES), F32)] + _gather_scratch(n_later),
        compiler_params=_params(("arbitrary",)),
    )(x2, g1, w_pad_t, gain_row, b_row, tri, esel, *later_weights)
    return res[:9], res[9:]


def _fox_fwd(qb, kb, vb, nb, seq):
    t_all = qb.shape[0]
    tq = min(512, seq)
    nq = seq // tq

    def body(q_ref, k_ref, v_ref, o_ref, lse_ref, s_ref, p_ref, m_ref, alpha_ref, acc_ref):
        qi = pl.program_id(2)
        q = q_ref[...]
        hq = tq // 2
        m_ref[...] = jnp.full((tq, LANES), NEG, F32)
        acc_ref[...] = jnp.zeros((tq, LANES), F32)

        def step(j, masked):
            off = pl.multiple_of(j * tq, tq)
            k = k_ref[pl.ds(off, tq), :]
            v = v_ref[pl.ds(off, tq), :]
            ncols = [hq if (masked and hf == 0) else tq for hf in range(2)]
            for hf in range(2):
                s_ref[hf, :, :ncols[hf]] = _dot_nt(q[hf * hq:(hf + 1) * hq], k[:ncols[hf]])
            for hf in range(2):
                for r in range(0, hq, CHUNK):
                    rows = slice(r, r + CHUNK)
                    grows = slice(hf * hq + r, hf * hq + r + CHUNK)
                    tiles = []
                    for jt in range(ncols[hf] // LANES):
                        sc = s_ref[hf, rows, jt * LANES:(jt + 1) * LANES]
                        if masked:
                            row = hf * hq + r + lax.broadcasted_iota(jnp.int32, (CHUNK, LANES), 0)
                            col = jt * LANES + lax.broadcasted_iota(jnp.int32, (CHUNK, LANES), 1)
                            sc = jnp.where(row >= col, sc, NEG)
                        tiles.append(sc)
                    m_prev = m_ref[grows, :]
                    m_cur = functools.reduce(jnp.maximum, tiles)
                    m_new = jnp.maximum(m_prev, jnp.max(m_cur, axis=-1, keepdims=True))
                    m_ref[grows, :] = m_new
                    alpha_ref[grows, :] = jnp.exp2(m_prev - m_new)
                    for jt, sc in enumerate(tiles):
                        p_ref[hf, rows, jt * LANES:(jt + 1) * LANES] = jnp.exp2(sc - m_new).astype(BF16)
                hrows = slice(hf * hq, (hf + 1) * hq)
                acc_ref[hrows, :] = (alpha_ref[hrows, :] * acc_ref[hrows, :]
                                     + _dot(p_ref[hf, :, :ncols[hf]], v[:ncols[hf]]))

        def unmasked(j, carry):
            step(j, False)
            return carry

        lax.fori_loop(0, qi, unmasked, 0)
        step(qi, True)
        acc = acc_ref[...]
        lane = lax.broadcasted_iota(jnp.int32, (tq, LANES), 1)
        l = jnp.sum(jnp.where(lane == HEAD_DIM, acc, 0.0), axis=-1, keepdims=True)
        o_ref[...] = (acc / l).astype(BF16)
        lse_ref[...] = m_ref[...] + jnp.log2(l)

    qspec = pl.BlockSpec((tq, LANES), lambda b, h, i: (b * nq + i, h))
    kspec = pl.BlockSpec((seq, LANES), lambda b, h, i: (b, h))
    return _pcall(
        body, name="fox_fwd", grid=(nb, B_HEADS, nq),
        in_specs=[qspec, kspec, kspec], out_specs=[qspec, qspec],
        out_shape=[jax.ShapeDtypeStruct((t_all, B_HEADS * LANES), BF16),
                   jax.ShapeDtypeStruct((t_all, B_HEADS * LANES), F32)],
        scratch_shapes=[pltpu.VMEM((2, tq // 2, tq), F32), pltpu.VMEM((2, tq // 2, tq), BF16),
                        pltpu.VMEM((tq, LANES), F32),
                        pltpu.VMEM((tq, LANES), F32), pltpu.VMEM((tq, LANES), F32)],
        compiler_params=_params(("parallel", "parallel", "arbitrary")),
    )(qb, kb, vb)


def _swa_bias(slopes):
    row = jnp.arange(A_GROUP * WINDOW, dtype=jnp.int32)[:, None] % WINDOW
    col = jnp.arange(2 * WINDOW, dtype=jnp.int32)[None, :]
    slope_rows = jnp.repeat(slopes.reshape(A_KV_HEADS, A_GROUP), WINDOW, axis=1)[:, :, None]
    out = []
    for t_rel in (0, WINDOW):
        dist = t_rel + row - col
        valid = (dist >= 0) & (dist < WINDOW)
        out.append(jnp.where(valid[None], -slope_rows * dist.astype(F32)[None], NEG))
    return jnp.stack(out)


def _stack_heads(ref, rows):
    return jnp.concatenate([ref[rows, j * LANES:(j + 1) * LANES] for j in range(A_GROUP)], axis=0)


def _sink_column(sink_ref, g):
    return jnp.concatenate([jnp.full((WINDOW, 1), sink_ref[g * A_GROUP + j], F32) for j in range(A_GROUP)], axis=0)


def _swa_specs(nq, tq, seq):
    smem = pl.BlockSpec(memory_space=pltpu.SMEM)
    qspec = pl.BlockSpec((tq, A_GROUP * LANES), lambda b, g, i: (b * nq + i, g))
    kspec = pl.BlockSpec((seq, LANES), lambda b, g, i: (b, g))
    bias_first = pl.BlockSpec((None, None, A_GROUP * WINDOW, 2 * WINDOW),
                              lambda b, g, i: (jnp.minimum(i, 1), g, 0, 0))
    bias_rest = pl.BlockSpec((None, None, A_GROUP * WINDOW, 2 * WINDOW), lambda b, g, i: (1, g, 0, 0))
    return smem, qspec, kspec, bias_first, bias_rest


def _swa_fwd(qa, ka, va, sinks, bias, nb, seq):
    t_all = qa.shape[0]
    tq = min(512, seq)
    nq = seq // tq

    def body(sink_ref, q_ref, k_ref, v_ref, bias0_ref, bias_ref, o_ref, l_ref):
        qi = pl.program_id(2)
        sink = _sink_column(sink_ref, pl.program_id(1))
        for a in range(tq // WINDOW):
            t0 = qi * tq + a * WINDOW
            start = pl.multiple_of(jnp.maximum(t0 - WINDOW, 0), WINDOW)
            rows = slice(a * WINDOW, (a + 1) * WINDOW)
            k = k_ref[pl.ds(start, 2 * WINDOW), :]
            v = v_ref[pl.ds(start, 2 * WINDOW), :]
            s = _dot_nt(_stack_heads(q_ref, rows), k) + (bias0_ref if a == 0 else bias_ref)[...]
            m = jnp.maximum(jnp.max(s, axis=-1, keepdims=True), sink)
            p = jnp.exp(s - m)
            den = jnp.sum(p, axis=-1, keepdims=True) + jnp.exp(sink - m)
            o = _dot((p * (1.0 / den)).astype(BF16), v).astype(BF16)
            lrow = jnp.broadcast_to(m + jnp.log(den), (A_GROUP * WINDOW, LANES))
            for j in range(A_GROUP):
                o_ref[rows, j * LANES:(j + 1) * LANES] = o[j * WINDOW:(j + 1) * WINDOW]
                l_ref[rows, j * LANES:(j + 1) * LANES] = lrow[j * WINDOW:(j + 1) * WINDOW]

    smem, qspec, kspec, bias_first, bias_rest = _swa_specs(nq, tq, seq)
    return _pcall(
        body, name="swa_fwd", grid=(nb, A_KV_HEADS, nq),
        in_specs=[smem, qspec, kspec, kspec, bias_first, bias_rest], out_specs=[qspec, qspec],
        out_shape=[jax.ShapeDtypeStruct((t_all, A_HEADS * LANES), BF16),
                   jax.ShapeDtypeStruct((t_all, A_HEADS * LANES), F32)],
        compiler_params=_params(("parallel", "parallel", "arbitrary")),
    )(sinks, qa, ka, va, bias, bias)


def _outproj(x2, oa, ob, wo_pad, g2):
    t_all = x2.shape[0]
    tm = min(512, t_all)
    half = A_HEADS * LANES

    def body(x_ref, oa_ref, ob_ref, w_ref, g_ref, h_ref, hn_ref):
        h = x_ref[...] + _dot(oa_ref[...], w_ref[:half, :]) + _dot(ob_ref[...], w_ref[half:, :])
        h_ref[...] = h
        r = lax.rsqrt(jnp.mean(h * h, axis=-1, keepdims=True) + EPS)
        hn_ref[...] = (h * r * g_ref[...]).astype(BF16)

    return _pcall(
        body, name="outproj", grid=(t_all // tm,),
        in_specs=[_rows(tm, D_MODEL), _rows(tm, half), _rows(tm, half), _const((MIXED_P, D_MODEL)),
                  _const((1, D_MODEL))],
        out_specs=[_rows(tm, D_MODEL), _rows(tm, D_MODEL)],
        out_shape=[jax.ShapeDtypeStruct((t_all, D_MODEL), F32), jax.ShapeDtypeStruct((t_all, D_MODEL), BF16)],
        compiler_params=_params(("parallel",)),
    )(x2, oa, ob, wo_pad, g2)


def _mlp_fwd(hn, w_up_blocks, w_down, h, tgt):
    t_all = h.shape[0]
    tm = min(256, t_all)
    nj = D_FF // D_MODEL

    def body(a_ref, wu_ref, wd_ref, h_ref, t_ref, ru_ref, dy_ref, dyb_ref, loss_ref):
        @pl.when(pl.program_id(0) == 0)
        def _():
            loss_ref[...] = jnp.zeros_like(loss_ref)

        a = a_ref[...]
        y = h_ref[...]
        for j in range(nj):
            cols = slice(j * D_MODEL, (j + 1) * D_MODEL)
            ru = jnp.maximum(_dot(a, wu_ref[j]), 0.0)
            ru_ref[:, cols] = ru.astype(BF16)
            y = y + _dot((ru * ru).astype(BF16), wd_ref[cols, :])
        err = y - t_ref[...]
        loss_ref[...] += jnp.sum(err * err)
        dy = err * (1.0 / D_MODEL)
        dy_ref[...] = dy
        dyb_ref[...] = dy.astype(BF16)

    return _pcall(
        body, name="mlp_fwd", grid=(t_all // tm,),
        in_specs=[_rows(tm, D_MODEL), _const((nj, D_MODEL, D_MODEL)), _const((D_FF, D_MODEL)), _rows(tm, D_MODEL),
                  _rows(tm, D_MODEL)],
        out_specs=[_rows(tm, D_FF), _rows(tm, D_MODEL), _rows(tm, D_MODEL), _const((8, LANES))],
        out_shape=[jax.ShapeDtypeStruct((t_all, D_FF), BF16), jax.ShapeDtypeStruct((t_all, D_MODEL), F32),
                   jax.ShapeDtypeStruct((t_all, D_MODEL), BF16), jax.ShapeDtypeStruct((8, LANES), F32)],
        compiler_params=_params(("arbitrary",)),
    )(hn, w_up_blocks, w_down, h, tgt)


def _mlp_bwd_w(dyb, w_down, ru, hn):
    t_all = dyb.shape[0]
    tm = min(512, t_all)
    nj = D_FF // D_MODEL

    def body(dy_ref, w_ref, ru_ref, hn_ref, du_ref, dw_ref):
        @pl.when(pl.program_id(1) == 0)
        def _():
            dw_ref[...] = jnp.zeros_like(dw_ref)

        dy = dy_ref[...]
        ru = ru_ref[...].astype(F32)
        du = (_dot_nt(dy, w_ref[...]) * (2.0 * ru)).astype(BF16)
        du_ref[...] = du
        dw_ref[0] += _dot_tn(hn_ref[...], du)
        dw_ref[1] += _dot_tn((ru * ru).astype(BF16), dy)

    tok = pl.BlockSpec((tm, D_MODEL), lambda j, i: (i, 0))
    blk = pl.BlockSpec((tm, D_MODEL), lambda j, i: (i, j))
    wspec = pl.BlockSpec((2, None, D_MODEL, D_MODEL), lambda j, i: (0, j, 0, 0))
    return _pcall(
        body, name="mlp_bwd_w", grid=(nj, t_all // tm),
        in_specs=[tok, pl.BlockSpec((D_MODEL, D_MODEL), lambda j, i: (j, 0)), blk, tok],
        out_specs=[blk, wspec],
        out_shape=[jax.ShapeDtypeStruct((t_all, D_FF), BF16), jax.ShapeDtypeStruct((2, nj, D_MODEL, D_MODEL), F32)],
        compiler_params=_params(("parallel", "arbitrary")),
    )(dyb, w_down, ru, hn)


def _pair_exchange_copy(g_ref, out_ref, send_sem, recv_sem):
    x, y, c = _place()
    return pltpu.make_async_remote_copy(
        src_ref=g_ref.at[1 - c], dst_ref=out_ref, send_sem=send_sem, recv_sem=recv_sem,
        device_id=(x, y, 1 - c), device_id_type=MESH)


def _mlp_dhn(du, w_up_blocks, h, dy, g2, d_w_mlp):
    t_all = h.shape[0]
    tm = min(256, t_all)
    n_steps = t_all // tm

    def body(a_ref, w_ref, h_ref, dy_ref, g_ref, dw_ref, dh_ref, dhb_ref, dg_ref, got_ref, send_sem, recv_sem):
        @pl.when(pl.program_id(0) == 0)
        def _():
            dg_ref[...] = jnp.zeros_like(dg_ref)
            _pair_exchange_copy(dw_ref, got_ref, send_sem, recv_sem).start()

        dhn = _dot_nt(a_ref[:, :D_MODEL], w_ref[0])
        for j in range(1, D_FF // D_MODEL):
            dhn = dhn + _dot_nt(a_ref[:, j * D_MODEL:(j + 1) * D_MODEL], w_ref[j])
        h = h_ref[...]
        r = lax.rsqrt(jnp.mean(h * h, axis=-1, keepdims=True) + EPS)
        hh = h * r
        dg_ref[...] += jnp.sum(dhn * hh, axis=0, keepdims=True)
        dz = dhn * g_ref[...]
        dh = dy_ref[...] + r * (dz - hh * jnp.mean(dz * hh, axis=-1, keepdims=True))
        dh_ref[...] = dh
        dhb_ref[...] = dh.astype(BF16)

        @pl.when(pl.program_id(0) == n_steps - 1)
        def _():
            _pair_exchange_copy(dw_ref, got_ref, send_sem, recv_sem).wait()

    return _pcall(
        body, name="mlp_dhn", grid=(n_steps,),
        in_specs=[_rows(tm, D_FF), _const((D_FF // D_MODEL, D_MODEL, D_MODEL)), _rows(tm, D_MODEL),
                  _rows(tm, D_MODEL), _const((1, D_MODEL)), ANY],
        out_specs=[_rows(tm, D_MODEL), _rows(tm, D_MODEL), _const((1, D_MODEL)), ANY],
        out_shape=[jax.ShapeDtypeStruct((t_all, D_MODEL), F32), jax.ShapeDtypeStruct((t_all, D_MODEL), BF16),
                   jax.ShapeDtypeStruct((1, D_MODEL), F32), jax.ShapeDtypeStruct(d_w_mlp.shape[1:], F32)],
        scratch_shapes=[pltpu.SemaphoreType.DMA, pltpu.SemaphoreType.DMA],
        compiler_params=_params(("arbitrary",)),
    )(du, w_up_blocks, h, dy, g2, d_w_mlp)


def _dmixed(dhb, wo_pad, oa, ob):
    t_all = dhb.shape[0]
    tm = min(512, t_all)
    half = A_HEADS * LANES

    def body(a_ref, w_ref, oa_ref, ob_ref, da_ref, db_ref, delta_ref, dwo_ref):
        @pl.when(pl.program_id(0) == 0)
        def _():
            dwo_ref[...] = jnp.zeros_like(dwo_ref)

        a = a_ref[...]
        d = _dot_nt(a, w_ref[...])
        da_ref[...] = d[:, :half].astype(BF16)
        db_ref[...] = d[:, half:].astype(BF16)
        for h in range(B_HEADS):
            cols = slice(h * LANES, (h + 1) * LANES)
            prod = d[:, half + h * LANES:half + (h + 1) * LANES] * ob_ref[:, cols].astype(F32)
            delta_ref[:, cols] = jnp.broadcast_to(jnp.sum(prod, axis=-1, keepdims=True), (tm, LANES))
        dwo_ref[:half, :] += _dot_tn(oa_ref[...], a)
        dwo_ref[half:, :] += _dot_tn(ob_ref[...], a)

    return _pcall(
        body, name="dmixed", grid=(t_all // tm,),
        in_specs=[_rows(tm, D_MODEL), _const((MIXED_P, D_MODEL)), _rows(tm, half), _rows(tm, half)],
        out_specs=[_rows(tm, half), _rows(tm, half), _rows(tm, half), _const((MIXED_P, D_MODEL))],
        out_shape=[jax.ShapeDtypeStruct((t_all, half), BF16), jax.ShapeDtypeStruct((t_all, half), BF16),
                   jax.ShapeDtypeStruct((t_all, half), F32), jax.ShapeDtypeStruct((MIXED_P, D_MODEL), F32)],
        compiler_params=_params(("arbitrary",)),
    )(dhb, wo_pad, oa, ob)


def _fox_bwd(qb, kb, vb, dob, lse, delta, nb, seq, pair_sums):
    t_all = qb.shape[0]
    tk = min(512, seq)
    nk = seq // tk

    def body(q_ref, k_ref, v_ref, do_ref, lse_ref, delta_ref, pair_ref, dq_ref, dk_ref, dv_ref, got_ref,
             s_ref, dp_ref, p_ref, ds_ref, dk_acc, dv_acc, send_sems, recv_sems):
        kj = pl.program_id(2)
        bh = pl.program_id(0) * B_HEADS + pl.program_id(1)

        @pl.when((bh == 0) & (kj == 0))
        def _():
            for cp in _chip_exchange_copies(pair_ref, got_ref, send_sems, recv_sems):
                cp.start()

        @pl.when(kj == 0)
        def _():
            dq_ref[...] = jnp.zeros_like(dq_ref)

        dk_acc[...] = jnp.zeros_like(dk_acc)
        dv_acc[...] = jnp.zeros_like(dv_acc)
        k = k_ref[...]
        v = v_ref[...]

        def block(off, r0, r1, c1, masked):
            qrows = pl.ds(pl.multiple_of(off + r0, CHUNK), r1 - r0)
            q = q_ref[qrows, :]
            do = do_ref[qrows, :]
            s_ref[r0:r1, :c1] = _dot_nt(q, k[:c1])
            dp_ref[r0:r1, :c1] = _dot_nt(do, v[:c1])
            for r in range(r0, r1, CHUNK):
                rows = slice(r, r + CHUNK)
                chunk = pl.ds(pl.multiple_of(off + r, CHUNK), CHUNK)
                lse_c = lse_ref[chunk, :]
                delta_c = delta_ref[chunk, :]
                for jt in range(c1 // LANES):
                    cols = slice(jt * LANES, (jt + 1) * LANES)
                    p = jnp.exp2(s_ref[rows, cols] - lse_c)
                    if masked:
                        row = r + lax.broadcasted_iota(jnp.int32, (CHUNK, LANES), 0)
                        col = jt * LANES + lax.broadcasted_iota(jnp.int32, (CHUNK, LANES), 1)
                        p = jnp.where(row >= col, p, 0.0)
                    p_ref[rows, cols] = p.astype(BF16)
                    ds_ref[rows, cols] = (p * (dp_ref[rows, cols] - delta_c)).astype(BF16)
            dv_acc[:c1, :] += _dot_tn(p_ref[r0:r1, :c1], do)
            dk_acc[:c1, :] += _dot_tn(ds_ref[r0:r1, :c1], q)
            dq_ref[qrows, :] += _dot(ds_ref[r0:r1, :c1], k[:c1])

        def step(i, masked):
            off = pl.multiple_of(i * tk, tk)
            if masked:
                block(off, 0, tk // 2, tk // 2, True)
                block(off, tk // 2, tk, tk, True)
            else:
                block(off, 0, tk, tk, False)

        def unmasked(i, carry):
            step(i, False)
            return carry

        step(kj, True)
        lax.fori_loop(kj + 1, nk, unmasked, 0)
        dk_ref[...] = dk_acc[...]
        dv_ref[...] = dv_acc[...]

        @pl.when((bh == nb * B_HEADS - 1) & (kj == nk - 1))
        def _():
            for cp in _chip_exchange_copies(pair_ref, got_ref, send_sems, recv_sems):
                cp.wait()

    full = pl.BlockSpec((seq, LANES), lambda b, h, j: (b, h))
    tile = pl.BlockSpec((tk, LANES), lambda b, h, j: (b * nk + j, h))
    shp = jax.ShapeDtypeStruct((t_all, B_HEADS * LANES), F32)
    return _pcall(
        body, name="fox_bwd", grid=(nb, B_HEADS, nk),
        in_specs=[full, tile, tile, full, full, full, ANY], out_specs=[full, tile, tile, ANY],
        out_shape=[shp, shp, shp, jax.ShapeDtypeStruct((3,) + pair_sums.shape[1:], pair_sums.dtype)],
        scratch_shapes=[pltpu.VMEM((tk, tk), F32), pltpu.VMEM((tk, tk), F32), pltpu.VMEM((tk, tk), BF16),
                        pltpu.VMEM((tk, tk), BF16), pltpu.VMEM((tk, LANES), F32), pltpu.VMEM((tk, LANES), F32),
                        pltpu.SemaphoreType.DMA((3,)), pltpu.SemaphoreType.DMA((3,))],
        compiler_params=_params(("arbitrary", "arbitrary", "arbitrary")),
    )(qb, kb, vb, dob, lse, delta, pair_sums)


def _swa_bwd(qa, ka, va, oa, doa, lrow, sinks, bias, nb, seq):
    t_all = qa.shape[0]
    tq = min(512, seq)
    nq = seq // tq

    def body(sink_ref, q_ref, k_ref, v_ref, bias0_ref, bias_ref, o_ref, do_ref, l_ref,
             dq_ref, dk_ref, dv_ref, dsink_ref):
        qi = pl.program_id(2)
        sink = _sink_column(sink_ref, pl.program_id(1))

        @pl.when(qi == 0)
        def _():
            dk_ref[...] = jnp.zeros_like(dk_ref)
            dv_ref[...] = jnp.zeros_like(dv_ref)
            dsink_ref[...] = jnp.zeros_like(dsink_ref)

        for a in range(tq // WINDOW):
            t0 = qi * tq + a * WINDOW
            start = pl.multiple_of(jnp.maximum(t0 - WINDOW, 0), WINDOW)
            rows = slice(a * WINDOW, (a + 1) * WINDOW)
            win = pl.ds(start, 2 * WINDOW)
            q = _stack_heads(q_ref, rows)
            k = k_ref[win, :]
            v = v_ref[win, :]
            do = _stack_heads(do_ref, rows)
            lrow_t = jnp.max(_stack_heads(l_ref, rows), axis=-1, keepdims=True)
            p = jnp.exp(_dot_nt(q, k) + (bias0_ref if a == 0 else bias_ref)[...] - lrow_t)
            delta = jnp.sum(do.astype(F32) * _stack_heads(o_ref, rows).astype(F32), axis=-1, keepdims=True)
            ds = (p * (_dot_nt(do, v) - delta)).astype(BF16)
            dq = _dot(ds, k)
            dk_ref[win, :] += _dot_tn(ds, q)
            dv_ref[win, :] += _dot_tn(p.astype(BF16), do)
            sink_term = jnp.exp(sink - lrow_t) * delta
            for j in range(A_GROUP):
                part = slice(j * WINDOW, (j + 1) * WINDOW)
                dq_ref[rows, j * LANES:(j + 1) * LANES] = dq[part]
                dsink_ref[j:j + 1, :] -= jnp.broadcast_to(jnp.sum(sink_term[part], axis=0, keepdims=True), (1, LANES))

    smem, qspec, kspec, bias_first, bias_rest = _swa_specs(nq, tq, seq)
    return _pcall(
        body, name="swa_bwd", grid=(nb, A_KV_HEADS, nq),
        in_specs=[smem, qspec, kspec, kspec, bias_first, bias_rest, qspec, qspec, qspec],
        out_specs=[qspec, kspec, kspec, pl.BlockSpec((None, 8, LANES), lambda b, g, i: (b * A_KV_HEADS + g, 0, 0))],
        out_shape=[jax.ShapeDtypeStruct((t_all, A_HEADS * LANES), F32),
                   jax.ShapeDtypeStruct((t_all, A_KV_HEADS * LANES), F32),
                   jax.ShapeDtypeStruct((t_all, A_KV_HEADS * LANES), F32),
                   jax.ShapeDtypeStruct((nb * A_KV_HEADS, 8, LANES), F32)],
        compiler_params=_params(("parallel", "parallel", "arbitrary")),
    )(sinks, qa, ka, va, bias, bias, oa, doa, lrow)


def _dproj_dx(pre, dqa, dka, dqb, dkb, dva, dvb, z, x2, dh, gain_row, w_pad_t, g1, seq):
    t_all = pre.shape[0]
    tm = min(256, seq)
    nt = t_all // tm
    tiles_per_seq = seq // tm
    triu = _tri(tm, True)
    sel = _dc_select()

    def body(pre_ref, dqa_ref, dka_ref, dqb_ref, dkb_ref, dva_ref, dvb_ref, z_ref, x_ref, dh_ref, gain_ref, triu_ref,
             sel_ref, w_ref, g_ref, dproj_ref, small_ref, dx_ref, dg_ref, carry_ref):
        i = pl.program_id(0)

        @pl.when(i == 0)
        def _():
            small_ref[...] = jnp.zeros_like(small_ref)
            dg_ref[...] = jnp.zeros_like(dg_ref)

        @pl.when(i % tiles_per_seq == 0)
        def _():
            carry_ref[...] = jnp.zeros_like(carry_ref)

        def norm_bwd(g, dhat):
            cols = slice(g * LANES, (g + 1) * LANES)
            p = pre_ref[:, cols].astype(F32)
            rr = lax.rsqrt(jnp.sum(p * p, axis=-1, keepdims=True) * (1.0 / HEAD_DIM) + EPS)
            n = p * rr
            dz = dhat * gain_ref[:, cols]
            dproj_ref[:, cols] = (rr * (dz - n * (jnp.sum(dz * n, axis=-1, keepdims=True) * (1.0 / HEAD_DIM)))
                                  ).astype(BF16)
            return jnp.sum(dhat * n, axis=0, keepdims=True)

        def group_sum(g0, d_ref, count, scale):
            acc = jnp.zeros((1, LANES), F32)
            for h in range(count):
                d = d_ref[:, h * LANES:(h + 1) * LANES]
                acc = acc + norm_bwd(g0 + h, d * scale if scale != 1.0 else d)
            return acc

        small_ref[0:1, :] += group_sum(G_QA, dqa_ref, A_HEADS, SCALE)
        small_ref[1:2, :] += group_sum(G_KA, dka_ref, A_KV_HEADS, 1.0)
        small_ref[2:3, :] += group_sum(G_QB, dqb_ref, B_HEADS, SCALE)
        small_ref[3:4, :] += group_sum(G_KB, dkb_ref, B_HEADS, LN2)
        dproj_ref[:, G_VA * LANES:G_VB * LANES] = dva_ref[...].astype(BF16)
        dproj_ref[:, G_VB * LANES:G_F * LANES] = dvb_ref[...].astype(BF16)

        dc = jnp.zeros((tm, LANES), F32)
        for piece_q, piece_k in zip(_split3(dqb_ref[...]), _split3(dkb_ref[...])):
            dc = dc + _dot(jnp.concatenate([piece_q, piece_k], axis=1), sel_ref[...])
        dlf = _dot_exact(triu_ref[...], dc) + carry_ref[...]
        carry_ref[...] += jnp.sum(dc, axis=0, keepdims=True)
        dz = dlf / (1.0 + jnp.exp(z_ref[...]))
        small_ref[4:5, :] += jnp.sum(dz, axis=0, keepdims=True)
        dproj_ref[:, G_F * LANES:(G_F + 1) * LANES] = dz.astype(BF16)
        dproj_ref[:, (G_F + 1) * LANES:] = jnp.zeros((tm, LANES), BF16)

        dxn = _dot(dproj_ref[...], w_ref[...])
        x = x_ref[...]
        r = lax.rsqrt(jnp.mean(x * x, axis=-1, keepdims=True) + EPS)
        xh = x * r
        dg_ref[...] += jnp.sum(dxn * xh, axis=0, keepdims=True)
        dxz = dxn * g_ref[...]
        dx_ref[...] = dh_ref[...] + r * (dxz - xh * jnp.mean(dxz * xh, axis=-1, keepdims=True))

    def rev(n):
        return pl.BlockSpec((tm, n), lambda i: (nt - 1 - i, 0))

    return _pcall(
        body, name="dproj_dx", grid=(nt,),
        in_specs=[rev(N_NORM_GROUPS * LANES), rev(A_HEADS * LANES), rev(A_KV_HEADS * LANES), rev(B_HEADS * LANES),
                  rev(B_HEADS * LANES), rev(A_KV_HEADS * LANES), rev(B_HEADS * LANES), rev(LANES), rev(D_MODEL),
                  rev(D_MODEL), _const((1, NP)), _const((tm, tm)), _const(sel.shape), _const((NP, D_MODEL)),
                  _const((1, D_MODEL))],
        out_specs=[rev(NP), _const((8, LANES)), rev(D_MODEL), _const((1, D_MODEL))],
        out_shape=[jax.ShapeDtypeStruct((t_all, NP), BF16), jax.ShapeDtypeStruct((8, LANES), F32),
                   jax.ShapeDtypeStruct((t_all, D_MODEL), F32), jax.ShapeDtypeStruct((1, D_MODEL), F32)],
        scratch_shapes=[pltpu.VMEM((1, LANES), F32)],
        compiler_params=_params(("arbitrary",)),
    )(pre, dqa, dka, dqb, dkb, dva, dvb, z, x2, dh, gain_row, triu, sel, w_pad_t, g1)


def _dwin(dproj, xn):
    t_all = xn.shape[0]
    tt = min(512, t_all)
    half = NP // 2

    def body(a_ref, b_ref, o_ref):
        @pl.when(pl.program_id(1) == 0)
        def _():
            o_ref[...] = jnp.zeros_like(o_ref)

        o_ref[...] += _dot_tn(a_ref[...], b_ref[...])

    return _pcall(
        body, name="dwin", grid=(2, t_all // tt),
        in_specs=[pl.BlockSpec((tt, half), lambda j, t: (t, j)), pl.BlockSpec((tt, D_MODEL), lambda j, t: (t, 0))],
        out_specs=pl.BlockSpec((half, D_MODEL), lambda j, t: (j, 0)),
        out_shape=jax.ShapeDtypeStruct((NP, D_MODEL), F32),
        compiler_params=_params(("parallel", "arbitrary")),
    )(dproj, xn)


ANY = pl.BlockSpec(memory_space=pl.ANY)


def _place():
    return lax.axis_index("x"), lax.axis_index("y"), lax.axis_index("c")


class _Gather:
    def __init__(self, srcs, outs, send_sems, recv_sems, local_sems):
        self.srcs, self.outs = srcs, outs
        self.send_sems, self.recv_sems, self.local_sems = send_sems, recv_sems, local_sems
        x, y, c = _place()
        self.c = c
        self.me, self.sibling = (x, y, c), (x, y, 1 - c)
        self.chips = [(1 - x, y), (x, 1 - y), (1 - x, 1 - y)]

    def _rows(self, a, px, py, pc):
        m = self.srcs[a].shape[0]
        return self.outs[a].at[pl.ds((4 * px + 2 * py + pc) * m, m), :]

    def _copy(self, a, k, block, to, from_src=False):
        return pltpu.make_async_remote_copy(
            src_ref=self.srcs[a] if from_src else self._rows(a, *block), dst_ref=self._rows(a, *block),
            send_sem=self.send_sems.at[k, a], recv_sem=self.recv_sems.at[k, a], device_id=to, device_id_type=MESH)

    def _own(self, a):
        return pltpu.make_async_copy(self.srcs[a], self._rows(a, *self.me), self.local_sems.at[a])

    def start(self):
        for a in range(len(self.srcs)):
            self._own(a).start()
            self._copy(a, 0, self.me, self.sibling, from_src=True).start()
            for j, chip in enumerate(self.chips):
                self._copy(a, 1 + j, self.me, (*chip, self.c), from_src=True).start()

    def forward(self):
        for a in range(len(self.srcs)):
            for j, chip in enumerate(self.chips):
                self._copy(a, 1 + j, (*chip, self.c), self.me).wait_recv()
                self._copy(a, 4 + j, (*chip, self.c), self.sibling).start()

    def finish(self):
        for a in range(len(self.srcs)):
            self._copy(a, 0, self.sibling, self.me).wait_recv()
            for j, chip in enumerate(self.chips):
                self._copy(a, 4 + j, (*chip, 1 - self.c), self.me).wait_recv()
            self._copy(a, 0, self.me, self.sibling, from_src=True).wait_send()
            for j, chip in enumerate(self.chips):
                self._copy(a, 1 + j, self.me, (*chip, self.c), from_src=True).wait_send()
                self._copy(a, 4 + j, (*chip, self.c), self.sibling).wait_send()
            self._own(a).wait()


def _gather_scratch(n_arrays):
    return [pltpu.SemaphoreType.DMA((7, n_arrays)), pltpu.SemaphoreType.DMA((7, n_arrays)),
            pltpu.SemaphoreType.DMA((n_arrays,))]


def _allgather_halves(mine):
    m_per, n = mine.shape

    def body(x_ref, out_ref, send_sems, recv_sems, local_sems):
        gather = _Gather((x_ref,), (out_ref,), send_sems, recv_sems, local_sems)
        gather.start()
        gather.forward()
        gather.finish()

    return _pcall(
        body, name="allgather_w_in",
        out_shape=jax.ShapeDtypeStruct((8 * m_per, n), mine.dtype),
        in_specs=[ANY], out_specs=ANY, scratch_shapes=_gather_scratch(1),
    )(mine)


def _rs_pair_exchange(g, name):
    def body(g_ref, out_ref, send_sem, recv_sem):
        x, y, c = _place()
        cp = pltpu.make_async_remote_copy(
            src_ref=g_ref.at[1 - c], dst_ref=out_ref, send_sem=send_sem, recv_sem=recv_sem,
            device_id=(x, y, 1 - c), device_id_type=MESH)
        cp.start()
        cp.wait()

    return _pcall(
        body, name=name, out_shape=jax.ShapeDtypeStruct(g.shape[1:], F32),
        in_specs=[ANY], out_specs=ANY, scratch_shapes=[pltpu.SemaphoreType.DMA, pltpu.SemaphoreType.DMA],
    )(g)


def _rs_pair_add(g, got, c_idx, name):
    rows = g.shape[2]

    def body(c_ref, a_ref, b_ref, o_ref, ob_ref):
        pair = a_ref[...] + b_ref[...]
        o_ref[...] = pair
        ob_ref[...] = pair.astype(BF16)

    blk = pl.BlockSpec((None, rows, D_MODEL), lambda s, c_ref: (s, 0, 0))
    return _pcall(
        body, name=name,
        grid_spec=pltpu.PrefetchScalarGridSpec(
            num_scalar_prefetch=1, grid=(N_CHIPS,),
            in_specs=[pl.BlockSpec((None, None, rows, D_MODEL), lambda s, c_ref: (c_ref[0], s, 0, 0)), blk],
            out_specs=[blk, blk]),
        out_shape=[jax.ShapeDtypeStruct((N_CHIPS, rows, D_MODEL), F32),
                   jax.ShapeDtypeStruct((N_CHIPS, rows, D_MODEL), BF16)],
        compiler_params=_params(("parallel",)),
    )(c_idx, g, got)


def _chip_exchange_copies(p_ref, out_ref, send_sems, recv_sems):
    x, y, c = _place()
    chips = [(1 - x, y), (x, 1 - y), (1 - x, 1 - y)]
    return [pltpu.make_async_remote_copy(
        src_ref=p_ref.at[2 * cx + cy], dst_ref=out_ref.at[j], send_sem=send_sems.at[j], recv_sem=recv_sems.at[j],
        device_id=(cx, cy, c), device_id_type=MESH) for j, (cx, cy) in enumerate(chips)]


def _rs_chip_exchange(p4, name):
    def body(p_ref, out_ref, send_sems, recv_sems):
        cps = _chip_exchange_copies(p_ref, out_ref, send_sems, recv_sems)
        for cp in cps:
            cp.start()
        for cp in cps:
            cp.wait()

    return _pcall(
        body, name=name, out_shape=jax.ShapeDtypeStruct((3,) + p4.shape[1:], p4.dtype),
        in_specs=[ANY], out_specs=ANY,
        scratch_shapes=[pltpu.SemaphoreType.DMA((3,)), pltpu.SemaphoreType.DMA((3,))],
    )(p4)


def _rs_chip_add(p4, got, sc_idx, name):
    rows = p4.shape[1]
    tr = next(rows // n for n in (8, 7, 6, 5, 4, 3, 2, 1) if rows % n == 0 and (rows // n) % 16 == 0)

    def body(sc_ref, a_ref, b_ref, o_ref):
        o_ref[...] = ((a_ref[...] + b_ref[0].astype(F32)) + b_ref[1].astype(F32)) + b_ref[2].astype(F32)

    return _pcall(
        body, name=name,
        grid_spec=pltpu.PrefetchScalarGridSpec(
            num_scalar_prefetch=1, grid=(rows // tr,),
            in_specs=[pl.BlockSpec((None, tr, D_MODEL), lambda i, sc_ref: (sc_ref[0], i, 0)),
                      pl.BlockSpec((3, tr, D_MODEL), lambda i, sc_ref: (0, i, 0))],
            out_specs=pl.BlockSpec((None, tr, D_MODEL), lambda i, sc_ref: (sc_ref[1], i, 0))),
        out_shape=jax.ShapeDtypeStruct((2, rows, D_MODEL), F32),
        compiler_params=_params(("parallel",)),
    )(sc_idx, p4, got)


def _rs_pair_share(halves, name):
    def body(r_ref, out_ref, send_sem, recv_sem):
        x, y, c = _place()
        cp = pltpu.make_async_remote_copy(
            src_ref=r_ref.at[c], dst_ref=out_ref.at[c], send_sem=send_sem, recv_sem=recv_sem,
            device_id=(x, y, 1 - c), device_id_type=MESH)
        cp.start()
        cp.wait()

    return _pcall(
        body, name=name, out_shape=jax.ShapeDtypeStruct(halves.shape, F32),
        in_specs=[ANY], out_specs=ANY, input_output_aliases={0: 0},
        scratch_shapes=[pltpu.SemaphoreType.DMA, pltpu.SemaphoreType.DMA],
    )(halves)


def _adam(w, g, m, v):
    m2 = ADAM_B1 * m + (1.0 - ADAM_B1) * g
    v2 = ADAM_B2 * v + (1.0 - ADAM_B2) * (g * g)
    m_hat = m2 / (1.0 - ADAM_B1 ** ADAM_STEP)
    v_hat = v2 / (1.0 - ADAM_B2 ** ADAM_STEP)
    return -ADAM_LR * (m_hat / (jnp.sqrt(v_hat) + ADAM_EPS) + ADAM_WD * w), m2, v2


def _small_allreduce_adamw(part, w, m, v):
    pieces = ((0, 8, LANES), (20, 1, B_HEADS), (16, 1, HEAD_DIM), (17, 1, HEAD_DIM), (21, 1, A_HEADS),
              (18, 1, HEAD_DIM), (19, 1, HEAD_DIM), (8, 8, LANES))

    def body(p_ref, w_ref, m_ref, v_ref, *rest):
        outs, (loss_ref, buf, stage, send_sems, recv_sems) = rest[:4 * len(pieces)], rest[4 * len(pieces):]
        x, y, c = _place()
        me = 4 * x + 2 * y + c
        cps = []
        for k in range(1, 8):
            peer = (1 - x if k & 4 else x, 1 - y if k & 2 else y, 1 - c if k & 1 else c)
            cps.append(pltpu.make_async_remote_copy(
                src_ref=p_ref, dst_ref=buf.at[me], send_sem=send_sems.at[k - 1], recv_sem=recv_sems.at[k - 1],
                device_id=peer, device_id_type=MESH))
        for cp in cps:
            cp.start()
        buf[me] = p_ref[...]
        for cp in cps:
            cp.wait()
        g = buf[0]
        for k in range(1, 8):
            g = g + buf[k]
        for kind, packed in enumerate((g,) + _adam(w_ref[...], g, m_ref[...], v_ref[...])):
            stage[...] = packed
            if kind == 0:
                loss_ref[...] = stage[ROW_LOSS:ROW_LOSS + 1, :]
            for i, (row, rows, lanes) in enumerate(pieces):
                outs[kind * len(pieces) + i][...] = stage[row:row + rows, 0:lanes]

    vm = pl.BlockSpec(memory_space=pltpu.VMEM)
    shapes = [jax.ShapeDtypeStruct((rows, lanes), F32) for _ in range(4) for _, rows, lanes in pieces]
    shapes.append(jax.ShapeDtypeStruct((1, LANES), F32))
    res = _pcall(
        body, name="small_allreduce_adamw",
        out_shape=shapes, in_specs=[vm, vm, vm, vm], out_specs=[vm] * len(shapes),
        scratch_shapes=[pltpu.VMEM((8, SMALL_ROWS, LANES), F32), pltpu.VMEM((SMALL_ROWS, LANES), F32),
                        pltpu.SemaphoreType.DMA((7,)), pltpu.SemaphoreType.DMA((7,))],
    )(part, w, m, v)
    flat = [r.reshape(r.size) for r in res[:-1]]
    n = len(pieces)
    return [flat[k * n:(k + 1) * n] for k in range(4)], res[-1][0, 0]


def _adamw(w, g, m, v, name):
    rows, cols = w.shape
    tr = min(256, rows)

    def body(w_ref, g_ref, m_ref, v_ref, d_ref, m2_ref, v2_ref):
        d_ref[...], m2_ref[...], v2_ref[...] = _adam(w_ref[...], g_ref[...], m_ref[...], v_ref[...])

    spec = _rows(tr, cols)
    shp = jax.ShapeDtypeStruct((rows, cols), F32)
    return _pcall(
        body, name=name, grid=(rows // tr,), in_specs=[spec] * 4, out_specs=[spec] * 3, out_shape=[shp] * 3,
        compiler_params=_params(("parallel",)),
    )(w, g, m, v)


def _pad_lanes(v):
    return jnp.pad(v, (0, LANES - v.shape[0]))


def _pad_head_rows(w_t, heads):
    n = w_t.shape[1]
    return jnp.pad(w_t.reshape(heads, HEAD_DIM, n), ((0, 0), (0, LANES - HEAD_DIM), (0, 0))).reshape(heads * LANES, n)


def _unpad_head_rows(w_t, heads):
    n = w_t.shape[1]
    return w_t.reshape(heads, LANES, n)[:, :HEAD_DIM].reshape(heads * HEAD_DIM, n)


def _in_rows_pad(w_in_t):
    qa, ka, va, qb, kb, vb, f = jnp.split(w_in_t, [512, 640, 768, 1280, 1792, 2304], axis=0)
    f = jnp.pad(f, ((0, 2 * LANES - B_HEADS), (0, 0)))
    return jnp.concatenate([_pad_head_rows(qa, 8), _pad_head_rows(ka, 2), _pad_head_rows(qb, 8),
                            _pad_head_rows(kb, 8), _pad_head_rows(va, 2), _pad_head_rows(vb, 8), f], axis=0)


def _in_rows_unpad(d):
    qa = _unpad_head_rows(d[G_QA * LANES:G_KA * LANES], 8)
    ka = _unpad_head_rows(d[G_KA * LANES:G_QB * LANES], 2)
    qb = _unpad_head_rows(d[G_QB * LANES:G_KB * LANES], 8)
    kb = _unpad_head_rows(d[G_KB * LANES:G_VA * LANES], 8)
    va = _unpad_head_rows(d[G_VA * LANES:G_VB * LANES], 2)
    vb = _unpad_head_rows(d[G_VB * LANES:G_F * LANES], 8)
    f = d[G_F * LANES:G_F * LANES + B_HEADS]
    return jnp.concatenate([qa, ka, va, qb, kb, vb, f], axis=0)


def _pack_small(g1, bf, qa, ka, sk, qb, kb, g2, loss_row):
    rows = [g1.reshape(8, LANES), g2.reshape(8, LANES)]
    rows += [_pad_lanes(t)[None] for t in (qa, ka, qb, kb, bf, sk)]
    rows += [loss_row, jnp.zeros((1, LANES), F32)]
    return jnp.concatenate(rows, axis=0)


def kernel(x, attn_norm_g, w_in, b_forget, q_norm_a, k_norm_a, sink_logits, q_norm_b, k_norm_b, w_out, mlp_norm_g, w_up, w_down, loss_target, m_attn_norm_g, m_w_in, m_b_forget, m_q_norm_a, m_k_norm_a, m_sink_logits, m_q_norm_b, m_k_norm_b, m_w_out, m_mlp_norm_g, m_w_up, m_w_down, v_attn_norm_g, v_w_in, v_b_forget, v_q_norm_a, v_k_norm_a, v_sink_logits, v_q_norm_b, v_k_norm_b, v_w_out, v_mlp_norm_g, v_w_up, v_w_down):
    nb, seq, _ = x.shape
    t_all = nb * seq
    c_idx = lax.axis_index("c")
    s_idx = 2 * lax.axis_index("x") + lax.axis_index("y")

    def my_half(a):
        halves = a.astype(BF16).reshape(2, a.shape[0] // 2, a.shape[1])
        return lax.dynamic_slice_in_dim(halves, c_idx, 1, axis=0)[0]

    w_in_shard_t = jnp.pad(w_in.T, ((0, IN_SHARD_P - IN_SHARD), (0, 0)))
    gathered_in = _allgather_halves(my_half(w_in_shard_t)).reshape(N_CHIPS, IN_SHARD_P, D_MODEL)
    w_pad_t = _in_rows_pad(gathered_in[:, :IN_SHARD].reshape(IN_WIDTH, D_MODEL))

    ones = jnp.ones((LANES,), F32)
    gain_row = jnp.concatenate(
        [jnp.tile(_pad_lanes(q_norm_a), 8), jnp.tile(_pad_lanes(k_norm_a), 2), jnp.tile(_pad_lanes(q_norm_b), 8),
         jnp.tile(_pad_lanes(k_norm_b), 8), jnp.tile(ones, N_GROUPS - N_NORM_GROUPS)])[None]
    b_row = _pad_lanes(b_forget)[None]
    g1 = attn_norm_g[None]
    g2 = mlp_norm_g[None]
    slopes = jnp.exp2(-(8.0 / A_HEADS) * (jnp.arange(A_HEADS, dtype=F32) + 1.0))

    x2 = x.reshape(t_all, D_MODEL)
    tgt = loss_target.reshape(t_all, D_MODEL)

    (xn, pre, qa, ka, va, qb, kb, vb, z), (w_out_g, w_up_g, w_down_f) = _inproj(
        x2, g1, w_pad_t, gain_row, b_row, seq, [my_half(w_out), my_half(w_up), my_half(w_down)])
    wo_pad = _pad_head_rows(w_out_g, A_HEADS + B_HEADS)
    w_up_blocks = w_up_g.reshape(N_CHIPS, D_MODEL, D_MODEL)
    swa_bias = _swa_bias(slopes)
    oa, la = _swa_fwd(qa, ka, va, sink_logits, swa_bias, nb, seq)
    ob, lse = _fox_fwd(qb, kb, vb, nb, seq)
    h, hn = _outproj(x2, oa, ob, wo_pad, g2)
    ru, dy, dyb, loss_acc = _mlp_fwd(hn, w_up_blocks, w_down_f, h, tgt)

    du, d_w_mlp = _mlp_bwd_w(dyb, w_down_f, ru, hn)
    c_arg = c_idx.reshape(1).astype(jnp.int32)
    sc_arg = jnp.stack([s_idx, c_idx]).astype(jnp.int32)
    dh, dhb, d_g2, sibling_w_mlp = _mlp_dhn(du, w_up_blocks, h, dy, g2, d_w_mlp)
    pair_m, pair_m_bf = _rs_pair_add(d_w_mlp, sibling_w_mlp, c_arg, "rs_pair_add_mlp")
    doa, dob, delta_b, d_wo = _dmixed(dhb, wo_pad, oa, ob)
    dqb, dkb, dvb, got_m = _fox_bwd(qb, kb, vb, dob, lse, delta_b, nb, seq, pair_m_bf)
    red_m = _rs_pair_share(_rs_chip_add(pair_m, got_m, sc_arg, "rs_chip_add_mlp"), "rs_pair_share_mlp")
    g_w_up, g_w_down = red_m[0], red_m[1]
    dqa, dka, dva, dsink = _swa_bwd(qa, ka, va, oa, doa, la, sink_logits, swa_bias, nb, seq)
    dproj, small, grad_x, d_g1 = _dproj_dx(pre, dqa, dka, dqb, dkb, dva, dvb, z, x2, dh, gain_row, w_pad_t, g1, seq)
    d_w_in_t = _dwin(dproj, xn)

    d_w_out = _unpad_head_rows(d_wo, A_HEADS + B_HEADS)
    g_att = jnp.concatenate([
        jnp.pad(_in_rows_unpad(d_w_in_t).reshape(N_CHIPS, IN_SHARD, D_MODEL),
                ((0, 0), (0, IN_SHARD_P - IN_SHARD), (0, 0))),
        d_w_out.reshape(N_CHIPS, D_MODEL // N_CHIPS, D_MODEL)], axis=1)
    g_att = jnp.stack([g_att[:, :R_ATT // 2], g_att[:, R_ATT // 2:]])
    pair_a, pair_a_bf = _rs_pair_add(g_att, _rs_pair_exchange(g_att, "rs_pair_exchange_att"), c_arg, "rs_pair_add_att")
    got_a = _rs_chip_exchange(pair_a_bf, "rs_chip_exchange_att")
    red_a = _rs_pair_share(_rs_chip_add(pair_a, got_a, sc_arg, "rs_chip_add_att"), "rs_pair_share_att")
    red_a = red_a.reshape(R_ATT, D_MODEL)
    g_w_in = red_a[:IN_SHARD].T
    g_w_out = red_a[IN_SHARD_P:]

    loss_row = loss_acc[0:1] * (0.5 / D_MODEL)
    d_sink = dsink[:, :A_GROUP, 0].reshape(nb, A_HEADS).sum(axis=0)
    part = _pack_small(d_g1[0], small[4, :B_HEADS], small[0, :HEAD_DIM], small[1, :HEAD_DIM], d_sink,
                       small[2, :HEAD_DIM], small[3, :HEAD_DIM], d_g2[0], loss_row)
    zero_row = jnp.zeros((1, LANES), F32)
    smalls = lambda t: _pack_small(*t, zero_row)
    w_small = smalls((attn_norm_g, b_forget, q_norm_a, k_norm_a, sink_logits, q_norm_b, k_norm_b, mlp_norm_g))
    m_small = smalls((m_attn_norm_g, m_b_forget, m_q_norm_a, m_k_norm_a, m_sink_logits, m_q_norm_b, m_k_norm_b,
                      m_mlp_norm_g))
    v_small = smalls((v_attn_norm_g, v_b_forget, v_q_norm_a, v_k_norm_a, v_sink_logits, v_q_norm_b, v_k_norm_b,
                      v_mlp_norm_g))
    (g_s, d_s, m_s, v_s), loss = _small_allreduce_adamw(part, w_small, m_small, v_small)

    big = {}
    for name, w, g, m, v in (("adamw_w_in", w_in, g_w_in, m_w_in, v_w_in),
                             ("adamw_w_out", w_out, g_w_out, m_w_out, v_w_out),
                             ("adamw_w_up", w_up, g_w_up, m_w_up, v_w_up),
                             ("adamw_w_down", w_down, g_w_down, m_w_down, v_w_down)):
        big[name] = (g,) + tuple(_adamw(w, g, m, v, name))

    def assemble(k, s):
        return (s[0], big["adamw_w_in"][k], s[1], s[2], s[3], s[4], s[5], s[6], big["adamw_w_out"][k], s[7],
                big["adamw_w_up"][k], big["adamw_w_down"][k])

    return (loss, grad_x.reshape(nb, seq, D_MODEL), *assemble(0, g_s), *assemble(1, d_s), *assemble(2, m_s),
            *assemble(3, v_s))
```

```python
import functools

import numpy as np
import jax
import jax.numpy as jnp
from jax import lax
from jax.experimental import pallas as pl
from jax.experimental.pallas import tpu as pltpu

F32 = jnp.float32
BF16 = jnp.bfloat16

D_MODEL = 1024
HEAD_DIM = 64
LANES = 128
A_HEADS = 8
A_KV_HEADS = 2
A_GROUP = A_HEADS // A_KV_HEADS
B_HEADS = 8
WINDOW = 128
D_FF = 4096
IN_WIDTH = 2312
EPS = 1e-6
SCALE = 0.125
LOG2E = 1.4426950408889634
LN2 = 0.6931471805599453
CHUNK = 32
NEG = -1e30

G_QA, G_KA, G_QB, G_KB, G_VA, G_VB, G_F = 0, 8, 10, 18, 26, 28, 36
N_NORM_GROUPS = 26
N_GROUPS = 38
NP = N_GROUPS * LANES
MIXED_P = (A_HEADS + B_HEADS) * LANES

N_CHIPS = 4
IN_SHARD = IN_WIDTH // N_CHIPS
IN_SHARD_P = 608
R_ATT = IN_SHARD_P + D_MODEL // N_CHIPS

SMALL_ROWS = 24
ROW_LOSS = 22

ADAM_LR = 0.001
ADAM_B1 = 0.9
ADAM_B2 = 0.999
ADAM_EPS = 1e-08
ADAM_WD = 0.01
ADAM_STEP = 10

VMEM_LIMIT = 52 * 1024 * 1024
MESH = pl.DeviceIdType.MESH


def _pcall(body, **kw):
    return pl.pallas_call(body, **kw)


def _params(sem=None):
    return pltpu.CompilerParams(dimension_semantics=sem, vmem_limit_bytes=VMEM_LIMIT)


def _dot(a, b):
    return jnp.dot(a, b, preferred_element_type=F32)


def _dot_nt(a, b):
    return lax.dot_general(a, b, (((1,), (1,)), ((), ())), preferred_element_type=F32)


def _dot_tn(a, b):
    return lax.dot_general(a, b, (((0,), (0,)), ((), ())), preferred_element_type=F32)


def _split3(x):
    hi = x.astype(BF16)
    r1 = x - hi.astype(F32)
    mid = r1.astype(BF16)
    lo = (r1 - mid.astype(F32)).astype(BF16)
    return hi, mid, lo


def _dot_exact(mat, x):
    hi, mid, lo = _split3(x)
    return _dot(mat, lo) + _dot(mat, mid) + _dot(mat, hi)


def _const(shape):
    zeros = (0,) * len(shape)
    return pl.BlockSpec(shape, lambda *_: zeros)


def _rows(tm, n):
    return pl.BlockSpec((tm, n), lambda i: (i, 0))


def _aug_select():
    e = np.zeros((3 * LANES, 2 * B_HEADS * LANES), np.float32)
    for j in range(3):
        for h in range(B_HEADS):
            e[j * LANES + h, h * LANES + HEAD_DIM + j] = 1.0
            e[j * LANES + h, (B_HEADS + h) * LANES + HEAD_DIM + 3 + j] = -1.0
    return jnp.asarray(e, BF16)


def _dc_select():
    e = np.zeros((2 * B_HEADS * LANES, LANES), np.float32)
    for h in range(B_HEADS):
        e[h * LANES + HEAD_DIM, h] = 1.0
        e[(B_HEADS + h) * LANES + HEAD_DIM + 3, h] = -1.0
    return jnp.asarray(e, BF16)


def _tri(n, upper):
    t = np.tril(np.ones((n, n), np.float32))
    return jnp.asarray(t.T if upper else t, BF16)


def _inproj(x2, g1, w_pad_t, gain_row, b_row, seq, later_weights):
    t_all = x2.shape[0]
    tm = min(256, seq)
    n_steps = t_all // tm
    forward_step = max(n_steps - 2, 0)
    tiles_per_seq = seq // tm
    tri = _tri(tm, False)
    esel = _aug_select()
    n_later = len(later_weights)

    def body(x_ref, g_ref, w_ref, gain_ref, b_ref, tri_ref, e_ref, *rest):
        later_src, rest = rest[:n_later], rest[n_later:]
        xn_ref, pre_ref, qa_ref, ka_ref, va_ref, qb_ref, kb_ref, vb_ref, z_ref = rest[:9]
        later_out, (carry_ref, send_sems, recv_sems, local_sems) = rest[9:9 + n_later], rest[9 + n_later:]
        i = pl.program_id(0)
        gather = _Gather(later_src, later_out, send_sems, recv_sems, local_sems)

        @pl.when(i == 0)
        def _():
            gather.start()

        @pl.when(i == forward_step)
        def _():
            gather.forward()

        @pl.when(i % tiles_per_seq == 0)
        def _():
            carry_ref[...] = jnp.zeros_like(carry_ref)

        x = x_ref[...]
        r = lax.rsqrt(jnp.mean(x * x, axis=-1, keepdims=True) + EPS)
        xn = (x * r * g_ref[...]).astype(BF16)
        xn_ref[...] = xn
        proj = _dot_nt(xn, w_ref[...])
        pre_ref[...] = proj[:, :N_NORM_GROUPS * LANES].astype(BF16)
        lane = lax.broadcasted_iota(jnp.int32, (tm, LANES), 1)

        z = proj[:, G_F * LANES:(G_F + 1) * LANES] + b_ref[...]
        z_ref[...] = z
        lf = jnp.minimum(z, 0.0) - jnp.log(1.0 + jnp.exp(-jnp.abs(z)))
        lf = jnp.where(lane < B_HEADS, lf, 0.0)
        c = _dot_exact(tri_ref[...], lf) + carry_ref[...]
        carry_ref[...] += jnp.sum(lf, axis=0, keepdims=True)
        aug = _dot(jnp.concatenate(_split3(c * LOG2E), axis=1), e_ref[...])

        def hnorm(g):
            p = proj[:, g * LANES:(g + 1) * LANES]
            rr = lax.rsqrt(jnp.sum(p * p, axis=-1, keepdims=True) * (1.0 / HEAD_DIM) + EPS)
            return p * rr * gain_ref[:, g * LANES:(g + 1) * LANES]

        ones_q = jnp.where((lane >= HEAD_DIM + 3) & (lane < HEAD_DIM + 6), 1.0, 0.0)
        ones_k = jnp.where((lane >= HEAD_DIM) & (lane < HEAD_DIM + 3), 1.0, 0.0)
        for h in range(A_HEADS):
            qa_ref[:, h * LANES:(h + 1) * LANES] = (hnorm(G_QA + h) * SCALE).astype(BF16)
        for h in range(A_KV_HEADS):
            ka_ref[:, h * LANES:(h + 1) * LANES] = hnorm(G_KA + h).astype(BF16)
        for h in range(B_HEADS):
            qb_ref[:, h * LANES:(h + 1) * LANES] = (
                hnorm(G_QB + h) * (SCALE * LOG2E) + aug[:, h * LANES:(h + 1) * LANES] + ones_q).astype(BF16)
            kb_ref[:, h * LANES:(h + 1) * LANES] = (
                hnorm(G_KB + h) + aug[:, (B_HEADS + h) * LANES:(B_HEADS + h + 1) * LANES] + ones_k).astype(BF16)
        va_ref[...] = proj[:, G_VA * LANES:G_VB * LANES].astype(BF16)
        one_v = jnp.where(lane == HEAD_DIM, 1.0, 0.0)
        for h in range(B_HEADS):
            cols = slice((G_VB + h) * LANES, (G_VB + h + 1) * LANES)
            vb_ref[:, h * LANES:(h + 1) * LANES] = (proj[:, cols] + one_v).astype(BF16)

        @pl.when(i == n_steps - 1)
        def _():
            gather.finish()

    widths = [(D_MODEL, BF16), (N_NORM_GROUPS * LANES, BF16), (A_HEADS * LANES, BF16), (A_KV_HEADS * LANES, BF16),
              (A_KV_HEADS * LANES, BF16), (B_HEADS * LANES, BF16), (B_HEADS * LANES, BF16), (B_HEADS * LANES, BF16),
              (LANES, F32)]
    res = _pcall(
        body, name="inproj", grid=(n_steps,),
        in_specs=[_rows(tm, D_MODEL), _const((1, D_MODEL)), _const((NP, D_MODEL)), _const((1, NP)),
                  _const((1, LANES)), _const((tm, tm)), _const(esel.shape)] + [ANY] * n_later,
        out_specs=[_rows(tm, w) for w, _ in widths] + [ANY] * n_later,
        out_shape=[jax.ShapeDtypeStruct((t_all, w), dt) for w, dt in widths]
        + [jax.ShapeDtypeStruct((8 * w.shape[0], w.shape[1]), w.dtype) for w in later_weights],
        scratch_shapes=[pltpu.VMEM((1, LANES), F32)] + _gather_scratch(n_later),
        compiler_params=_params(("arbitrary",)),
    )(x2, g1, w_pad_t, gain_row, b_row, tri, esel, *later_weights)
    return res[:9], res[9:]


def _fox_fwd(qb, kb, vb, nb, seq):
    t_all = qb.shape[0]
    tq = min(1024, seq)
    tk = tq // 2
    nq = seq // tq

    def body(q_ref, k_ref, v_ref, o_ref, lse_ref, s_ref, p_ref, m_ref, alpha_ref, acc_ref):
        qi = pl.program_id(2)
        q = q_ref[...]
        m_ref[...] = jnp.full((tq, LANES), NEG, F32)
        acc_ref[...] = jnp.zeros((tq, LANES), F32)

        def step(j, modes):
            off = pl.multiple_of(j * tk, tk)
            k = k_ref[pl.ds(off, tk), :]
            v = v_ref[pl.ds(off, tk), :]
            live = [hf for hf in range(2) if modes[hf] is not None]
            for hf in live:
                s_ref[hf] = _dot_nt(q[hf * tk:(hf + 1) * tk], k)
            for hf in live:
                for r in range(0, tk, CHUNK):
                    rows = slice(r, r + CHUNK)
                    grows = slice(hf * tk + r, hf * tk + r + CHUNK)
                    tiles = []
                    for jt in range(tk // LANES):
                        sc = s_ref[hf, rows, jt * LANES:(jt + 1) * LANES]
                        if modes[hf] == "diag":
                            row = r + lax.broadcasted_iota(jnp.int32, (CHUNK, LANES), 0)
                            col = jt * LANES + lax.broadcasted_iota(jnp.int32, (CHUNK, LANES), 1)
                            sc = jnp.where(row >= col, sc, NEG)
                        tiles.append(sc)
                    m_prev = m_ref[grows, :]
                    m_cur = functools.reduce(jnp.maximum, tiles)
                    m_new = jnp.maximum(m_prev, jnp.max(m_cur, axis=-1, keepdims=True))
                    m_ref[grows, :] = m_new
                    alpha_ref[grows, :] = jnp.exp2(m_prev - m_new)
                    for jt, sc in enumerate(tiles):
                        p_ref[hf, rows, jt * LANES:(jt + 1) * LANES] = jnp.exp2(sc - m_new).astype(BF16)
                hrows = slice(hf * tk, (hf + 1) * tk)
                acc_ref[hrows, :] = alpha_ref[hrows, :] * acc_ref[hrows, :] + _dot(p_ref[hf], v)

        def past(j, carry):
            step(j, ("full", "full"))
            return carry

        lax.fori_loop(0, 2 * qi, past, 0)
        step(2 * qi, ("diag", "full"))
        step(2 * qi + 1, (None, "diag"))
        acc = acc_ref[...]
        lane = lax.broadcasted_iota(jnp.int32, (tq, LANES), 1)
        l = jnp.sum(jnp.where(lane == HEAD_DIM, acc, 0.0), axis=-1, keepdims=True)
        o_ref[...] = (acc / l).astype(BF16)
        lse_ref[...] = m_ref[...] + jnp.log2(l)

    qspec = pl.BlockSpec((tq, LANES), lambda b, h, i: (b * nq + i, h))
    kspec = pl.BlockSpec((seq, LANES), lambda b, h, i: (b, h))
    return _pcall(
        body, name="fox_fwd", grid=(nb, B_HEADS, nq),
        in_specs=[qspec, kspec, kspec], out_specs=[qspec, qspec],
        out_shape=[jax.ShapeDtypeStruct((t_all, B_HEADS * LANES), BF16),
                   jax.ShapeDtypeStruct((t_all, B_HEADS * LANES), F32)],
        scratch_shapes=[pltpu.VMEM((2, tk, tk), F32), pltpu.VMEM((2, tk, tk), BF16), pltpu.VMEM((tq, LANES), F32),
                        pltpu.VMEM((tq, LANES), F32), pltpu.VMEM((tq, LANES), F32)],
        compiler_params=_params(("parallel", "parallel", "arbitrary")),
    )(qb, kb, vb)


def _swa_bias(slopes):
    row = jnp.arange(A_GROUP * WINDOW, dtype=jnp.int32)[:, None] % WINDOW
    col = jnp.arange(2 * WINDOW, dtype=jnp.int32)[None, :]
    slope_rows = jnp.repeat(slopes.reshape(A_KV_HEADS, A_GROUP), WINDOW, axis=1)[:, :, None]
    out = []
    for t_rel in (0, WINDOW):
        dist = t_rel + row - col
        valid = (dist >= 0) & (dist < WINDOW)
        out.append(jnp.where(valid[None], -slope_rows * dist.astype(F32)[None], NEG))
    return jnp.stack(out)


def _stack_heads(ref, rows):
    return jnp.concatenate([ref[rows, j * LANES:(j + 1) * LANES] for j in range(A_GROUP)], axis=0)


def _sink_column(sink_ref, g):
    return jnp.concatenate([jnp.full((WINDOW, 1), sink_ref[g * A_GROUP + j], F32) for j in range(A_GROUP)], axis=0)


def _swa_specs(nq, tq, seq):
    smem = pl.BlockSpec(memory_space=pltpu.SMEM)
    qspec = pl.BlockSpec((tq, A_GROUP * LANES), lambda b, g, i: (b * nq + i, g))
    kspec = pl.BlockSpec((seq, LANES), lambda b, g, i: (b, g))
    bias_first = pl.BlockSpec((None, None, A_GROUP * WINDOW, 2 * WINDOW),
                              lambda b, g, i: (jnp.minimum(i, 1), g, 0, 0))
    bias_rest = pl.BlockSpec((None, None, A_GROUP * WINDOW, 2 * WINDOW), lambda b, g, i: (1, g, 0, 0))
    return smem, qspec, kspec, bias_first, bias_rest


def _swa_fwd(qa, ka, va, sinks, bias, nb, seq):
    t_all = qa.shape[0]
    tq = min(512, seq)
    nq = seq // tq

    def body(sink_ref, q_ref, k_ref, v_ref, bias0_ref, bias_ref, o_ref, l_ref):
        qi = pl.program_id(2)
        sink = _sink_column(sink_ref, pl.program_id(1))
        for a in range(tq // WINDOW):
            t0 = qi * tq + a * WINDOW
            start = pl.multiple_of(jnp.maximum(t0 - WINDOW, 0), WINDOW)
            rows = slice(a * WINDOW, (a + 1) * WINDOW)
            k = k_ref[pl.ds(start, 2 * WINDOW), :]
            v = v_ref[pl.ds(start, 2 * WINDOW), :]
            s = _dot_nt(_stack_heads(q_ref, rows), k) + (bias0_ref if a == 0 else bias_ref)[...]
            m = jnp.maximum(jnp.max(s, axis=-1, keepdims=True), sink)
            p = jnp.exp(s - m)
            den = jnp.sum(p, axis=-1, keepdims=True) + jnp.exp(sink - m)
            o = _dot((p * (1.0 / den)).astype(BF16), v).astype(BF16)
            lrow = jnp.broadcast_to(m + jnp.log(den), (A_GROUP * WINDOW, LANES))
            for j in range(A_GROUP):
                o_ref[rows, j * LANES:(j + 1) * LANES] = o[j * WINDOW:(j + 1) * WINDOW]
                l_ref[rows, j * LANES:(j + 1) * LANES] = lrow[j * WINDOW:(j + 1) * WINDOW]

    smem, qspec, kspec, bias_first, bias_rest = _swa_specs(nq, tq, seq)
    return _pcall(
        body, name="swa_fwd", grid=(nb, A_KV_HEADS, nq),
        in_specs=[smem, qspec, kspec, kspec, bias_first, bias_rest], out_specs=[qspec, qspec],
        out_shape=[jax.ShapeDtypeStruct((t_all, A_HEADS * LANES), BF16),
                   jax.ShapeDtypeStruct((t_all, A_HEADS * LANES), F32)],
        compiler_params=_params(("parallel", "parallel", "arbitrary")),
    )(sinks, qa, ka, va, bias, bias)


def _outproj(x2, oa, ob, wo_pad, g2):
    t_all = x2.shape[0]
    tm = min(512, t_all)
    half = A_HEADS * LANES

    def body(x_ref, oa_ref, ob_ref, w_ref, g_ref, h_ref, hn_ref):
        h = x_ref[...] + _dot(oa_ref[...], w_ref[:half, :]) + _dot(ob_ref[...], w_ref[half:, :])
        h_ref[...] = h
        r = lax.rsqrt(jnp.mean(h * h, axis=-1, keepdims=True) + EPS)
        hn_ref[...] = (h * r * g_ref[...]).astype(BF16)

    return _pcall(
        body, name="outproj", grid=(t_all // tm,),
        in_specs=[_rows(tm, D_MODEL), _rows(tm, half), _rows(tm, half), _const((MIXED_P, D_MODEL)),
                  _const((1, D_MODEL))],
        out_specs=[_rows(tm, D_MODEL), _rows(tm, D_MODEL)],
        out_shape=[jax.ShapeDtypeStruct((t_all, D_MODEL), F32), jax.ShapeDtypeStruct((t_all, D_MODEL), BF16)],
        compiler_params=_params(("parallel",)),
    )(x2, oa, ob, wo_pad, g2)


def _mlp_fwd(hn, w_up_blocks, w_down, h, tgt):
    t_all = h.shape[0]
    tm = min(256, t_all)
    nj = D_FF // D_MODEL

    def body(a_ref, wu_ref, wd_ref, h_ref, t_ref, ru_ref, dy_ref, dyb_ref, loss_ref):
        @pl.when(pl.program_id(0) == 0)
        def _():
            loss_ref[...] = jnp.zeros_like(loss_ref)

        a = a_ref[...]
        y = h_ref[...]
        for j in range(nj):
            cols = slice(j * D_MODEL, (j + 1) * D_MODEL)
            ru = jnp.maximum(_dot(a, wu_ref[j]), 0.0)
            ru_ref[:, cols] = ru.astype(BF16)
            y = y + _dot((ru * ru).astype(BF16), wd_ref[cols, :])
        err = y - t_ref[...]
        loss_ref[...] += jnp.sum(err * err)
        dy = err * (1.0 / D_MODEL)
        dy_ref[...] = dy
        dyb_ref[...] = dy.astype(BF16)

    return _pcall(
        body, name="mlp_fwd", grid=(t_all // tm,),
        in_specs=[_rows(tm, D_MODEL), _const((nj, D_MODEL, D_MODEL)), _const((D_FF, D_MODEL)), _rows(tm, D_MODEL),
                  _rows(tm, D_MODEL)],
        out_specs=[_rows(tm, D_FF), _rows(tm, D_MODEL), _rows(tm, D_MODEL), _const((8, LANES))],
        out_shape=[jax.ShapeDtypeStruct((t_all, D_FF), BF16), jax.ShapeDtypeStruct((t_all, D_MODEL), F32),
                   jax.ShapeDtypeStruct((t_all, D_MODEL), BF16), jax.ShapeDtypeStruct((8, LANES), F32)],
        compiler_params=_params(("arbitrary",)),
    )(hn, w_up_blocks, w_down, h, tgt)


def _mlp_bwd_w(dyb, w_down, ru, hn):
    t_all = dyb.shape[0]
    tm = min(512, t_all)
    nj = D_FF // D_MODEL

    def body(dy_ref, w_ref, ru_ref, hn_ref, du_ref, dw_ref):
        @pl.when(pl.program_id(1) == 0)
        def _():
            dw_ref[...] = jnp.zeros_like(dw_ref)

        dy = dy_ref[...]
        ru = ru_ref[...].astype(F32)
        du = (_dot_nt(dy, w_ref[...]) * (2.0 * ru)).astype(BF16)
        du_ref[...] = du
        dw_ref[0] += _dot_tn(hn_ref[...], du)
        dw_ref[1] += _dot_tn((ru * ru).astype(BF16), dy)

    tok = pl.BlockSpec((tm, D_MODEL), lambda j, i: (i, 0))
    blk = pl.BlockSpec((tm, D_MODEL), lambda j, i: (i, j))
    wspec = pl.BlockSpec((2, None, D_MODEL, D_MODEL), lambda j, i: (0, j, 0, 0))
    return _pcall(
        body, name="mlp_bwd_w", grid=(nj, t_all // tm),
        in_specs=[tok, pl.BlockSpec((D_MODEL, D_MODEL), lambda j, i: (j, 0)), blk, tok],
        out_specs=[blk, wspec],
        out_shape=[jax.ShapeDtypeStruct((t_all, D_FF), BF16), jax.ShapeDtypeStruct((2, nj, D_MODEL, D_MODEL), F32)],
        compiler_params=_params(("parallel", "arbitrary")),
    )(dyb, w_down, ru, hn)


def _pair_exchange_copy(g_ref, out_ref, send_sem, recv_sem):
    x, y, c = _place()
    return pltpu.make_async_remote_copy(
        src_ref=g_ref.at[1 - c], dst_ref=out_ref, send_sem=send_sem, recv_sem=recv_sem,
        device_id=(x, y, 1 - c), device_id_type=MESH)


def _mlp_dhn(du, w_up_blocks, h, dy, g2, d_w_mlp):
    t_all = h.shape[0]
    tm = min(256, t_all)
    n_steps = t_all // tm

    def body(a_ref, w_ref, h_ref, dy_ref, g_ref, dw_ref, dh_ref, dhb_ref, dg_ref, got_ref, send_sem, recv_sem):
        @pl.when(pl.program_id(0) == 0)
        def _():
            dg_ref[...] = jnp.zeros_like(dg_ref)
            _pair_exchange_copy(dw_ref, got_ref, send_sem, recv_sem).start()

        dhn = _dot_nt(a_ref[:, :D_MODEL], w_ref[0])
        for j in range(1, D_FF // D_MODEL):
            dhn = dhn + _dot_nt(a_ref[:, j * D_MODEL:(j + 1) * D_MODEL], w_ref[j])
        h = h_ref[...]
        r = lax.rsqrt(jnp.mean(h * h, axis=-1, keepdims=True) + EPS)
        hh = h * r
        dg_ref[...] += jnp.sum(dhn * hh, axis=0, keepdims=True)
        dz = dhn * g_ref[...]
        dh = dy_ref[...] + r * (dz - hh * jnp.mean(dz * hh, axis=-1, keepdims=True))
        dh_ref[...] = dh
        dhb_ref[...] = dh.astype(BF16)

        @pl.when(pl.program_id(0) == n_steps - 1)
        def _():
            _pair_exchange_copy(dw_ref, got_ref, send_sem, recv_sem).wait()

    return _pcall(
        body, name="mlp_dhn", grid=(n_steps,),
        in_specs=[_rows(tm, D_FF), _const((D_FF // D_MODEL, D_MODEL, D_MODEL)), _rows(tm, D_MODEL),
                  _rows(tm, D_MODEL), _const((1, D_MODEL)), ANY],
        out_specs=[_rows(tm, D_MODEL), _rows(tm, D_MODEL), _const((1, D_MODEL)), ANY],
        out_shape=[jax.ShapeDtypeStruct((t_all, D_MODEL), F32), jax.ShapeDtypeStruct((t_all, D_MODEL), BF16),
                   jax.ShapeDtypeStruct((1, D_MODEL), F32), jax.ShapeDtypeStruct(d_w_mlp.shape[1:], F32)],
        scratch_shapes=[pltpu.SemaphoreType.DMA, pltpu.SemaphoreType.DMA],
        compiler_params=_params(("arbitrary",)),
    )(du, w_up_blocks, h, dy, g2, d_w_mlp)


def _dmixed(dhb, wo_pad, oa, ob):
    t_all = dhb.shape[0]
    tm = min(512, t_all)
    half = A_HEADS * LANES

    def body(a_ref, w_ref, oa_ref, ob_ref, da_ref, db_ref, delta_ref, dwo_ref):
        @pl.when(pl.program_id(0) == 0)
        def _():
            dwo_ref[...] = jnp.zeros_like(dwo_ref)

        a = a_ref[...]
        d = _dot_nt(a, w_ref[...])
        da_ref[...] = d[:, :half].astype(BF16)
        db_ref[...] = d[:, half:].astype(BF16)
        for h in range(B_HEADS):
            cols = slice(h * LANES, (h + 1) * LANES)
            prod = d[:, half + h * LANES:half + (h + 1) * LANES] * ob_ref[:, cols].astype(F32)
            delta_ref[:, cols] = jnp.broadcast_to(jnp.sum(prod, axis=-1, keepdims=True), (tm, LANES))
        dwo_ref[:half, :] += _dot_tn(oa_ref[...], a)
        dwo_ref[half:, :] += _dot_tn(ob_ref[...], a)

    return _pcall(
        body, name="dmixed", grid=(t_all // tm,),
        in_specs=[_rows(tm, D_MODEL), _const((MIXED_P, D_MODEL)), _rows(tm, half), _rows(tm, half)],
        out_specs=[_rows(tm, half), _rows(tm, half), _rows(tm, half), _const((MIXED_P, D_MODEL))],
        out_shape=[jax.ShapeDtypeStruct((t_all, half), BF16), jax.ShapeDtypeStruct((t_all, half), BF16),
                   jax.ShapeDtypeStruct((t_all, half), F32), jax.ShapeDtypeStruct((MIXED_P, D_MODEL), F32)],
        compiler_params=_params(("arbitrary",)),
    )(dhb, wo_pad, oa, ob)


def _fox_bwd(qb, kb, vb, dob, lse, delta, nb, seq, pair_sums):
    t_all = qb.shape[0]
    tq = min(1024, seq)
    tk = tq // 2
    nk = seq // tk

    def body(q_ref, k_ref, v_ref, do_ref, lse_ref, delta_ref, pair_ref, dq_ref, dk_ref, dv_ref, got_ref,
             s_ref, dp_ref, p_ref, ds_ref, dk_acc, dv_acc, send_sems, recv_sems):
        kj = pl.program_id(2)
        bh = pl.program_id(0) * B_HEADS + pl.program_id(1)

        @pl.when((bh == 0) & (kj == 0))
        def _():
            for cp in _chip_exchange_copies(pair_ref, got_ref, send_sems, recv_sems):
                cp.start()

        @pl.when(kj == 0)
        def _():
            dq_ref[...] = jnp.zeros_like(dq_ref)

        dk_acc[...] = jnp.zeros_like(dk_acc)
        dv_acc[...] = jnp.zeros_like(dv_acc)
        k = k_ref[...]
        v = v_ref[...]

        def block(off, r0, r1, masked):
            qrows = pl.ds(pl.multiple_of(off + r0, CHUNK), r1 - r0)
            q = q_ref[qrows, :]
            do = do_ref[qrows, :]
            s_ref[r0:r1, :] = _dot_nt(q, k)
            dp_ref[r0:r1, :] = _dot_nt(do, v)
            for r in range(r0, r1, CHUNK):
                rows = slice(r, r + CHUNK)
                chunk = pl.ds(pl.multiple_of(off + r, CHUNK), CHUNK)
                lse_c = lse_ref[chunk, :]
                delta_c = delta_ref[chunk, :]
                for jt in range(tk // LANES):
                    cols = slice(jt * LANES, (jt + 1) * LANES)
                    p = jnp.exp2(s_ref[rows, cols] - lse_c)
                    if masked:
                        row = r - r0 + lax.broadcasted_iota(jnp.int32, (CHUNK, LANES), 0)
                        col = jt * LANES + lax.broadcasted_iota(jnp.int32, (CHUNK, LANES), 1)
                        p = jnp.where(row >= col, p, 0.0)
                    p_ref[rows, cols] = p.astype(BF16)
                    ds_ref[rows, cols] = (p * (dp_ref[rows, cols] - delta_c)).astype(BF16)
            dv_acc[...] += _dot_tn(p_ref[r0:r1, :], do)
            dk_acc[...] += _dot_tn(ds_ref[r0:r1, :], q)
            dq_ref[qrows, :] += _dot(ds_ref[r0:r1, :], k)

        first = kj // 2
        off_first = pl.multiple_of(first * tq, tq)

        @pl.when(kj % 2 == 0)
        def _():
            block(off_first, 0, tk, True)
            block(off_first, tk, tq, False)

        @pl.when(kj % 2 == 1)
        def _():
            block(off_first, tk, tq, True)

        def later(i, carry):
            block(pl.multiple_of(i * tq, tq), 0, tq, False)
            return carry

        lax.fori_loop(first + 1, seq // tq, later, 0)
        dk_ref[...] = dk_acc[...]
        dv_ref[...] = dv_acc[...]

        @pl.when((bh == nb * B_HEADS - 1) & (kj == nk - 1))
        def _():
            for cp in _chip_exchange_copies(pair_ref, got_ref, send_sems, recv_sems):
                cp.wait()

    full = pl.BlockSpec((seq, LANES), lambda b, h, j: (b, h))
    tile = pl.BlockSpec((tk, LANES), lambda b, h, j: (b * nk + j, h))
    shp = jax.ShapeDtypeStruct((t_all, B_HEADS * LANES), F32)
    return _pcall(
        body, name="fox_bwd", grid=(nb, B_HEADS, nk),
        in_specs=[full, tile, tile, full, full, full, ANY], out_specs=[full, tile, tile, ANY],
        out_shape=[shp, shp, shp, jax.ShapeDtypeStruct((3,) + pair_sums.shape[1:], pair_sums.dtype)],
        scratch_shapes=[pltpu.VMEM((tq, tk), F32), pltpu.VMEM((tq, tk), F32), pltpu.VMEM((tq, tk), BF16),
                        pltpu.VMEM((tq, tk), BF16), pltpu.VMEM((tk, LANES), F32), pltpu.VMEM((tk, LANES), F32),
                        pltpu.SemaphoreType.DMA((3,)), pltpu.SemaphoreType.DMA((3,))],
        compiler_params=_params(("arbitrary", "arbitrary", "arbitrary")),
    )(qb, kb, vb, dob, lse, delta, pair_sums)


def _swa_bwd(qa, ka, va, oa, doa, lrow, sinks, bias, nb, seq):
    t_all = qa.shape[0]
    tq = min(512, seq)
    nq = seq // tq

    def body(sink_ref, q_ref, k_ref, v_ref, bias0_ref, bias_ref, o_ref, do_ref, l_ref,
             dq_ref, dk_ref, dv_ref, dsink_ref):
        qi = pl.program_id(2)
        sink = _sink_column(sink_ref, pl.program_id(1))

        @pl.when(qi == 0)
        def _():
            dk_ref[...] = jnp.zeros_like(dk_ref)
            dv_ref[...] = jnp.zeros_like(dv_ref)
            dsink_ref[...] = jnp.zeros_like(dsink_ref)

        for a in range(tq // WINDOW):
            t0 = qi * tq + a * WINDOW
            start = pl.multiple_of(jnp.maximum(t0 - WINDOW, 0), WINDOW)
            rows = slice(a * WINDOW, (a + 1) * WINDOW)
            win = pl.ds(start, 2 * WINDOW)
            q = _stack_heads(q_ref, rows)
            k = k_ref[win, :]
            v = v_ref[win, :]
            do = _stack_heads(do_ref, rows)
            lrow_t = jnp.max(_stack_heads(l_ref, rows), axis=-1, keepdims=True)
            p = jnp.exp(_dot_nt(q, k) + (bias0_ref if a == 0 else bias_ref)[...] - lrow_t)
            delta = jnp.sum(do.astype(F32) * _stack_heads(o_ref, rows).astype(F32), axis=-1, keepdims=True)
            ds = (p * (_dot_nt(do, v) - delta)).astype(BF16)
            dq = _dot(ds, k)
            dk_ref[win, :] += _dot_tn(ds, q)
            dv_ref[win, :] += _dot_tn(p.astype(BF16), do)
            sink_term = jnp.exp(sink - lrow_t) * delta
            for j in range(A_GROUP):
                part = slice(j * WINDOW, (j + 1) * WINDOW)
                dq_ref[rows, j * LANES:(j + 1) * LANES] = dq[part]
                dsink_ref[j:j + 1, :] -= jnp.broadcast_to(jnp.sum(sink_term[part], axis=0, keepdims=True), (1, LANES))

    smem, qspec, kspec, bias_first, bias_rest = _swa_specs(nq, tq, seq)
    return _pcall(
        body, name="swa_bwd", grid=(nb, A_KV_HEADS, nq),
        in_specs=[smem, qspec, kspec, kspec, bias_first, bias_rest, qspec, qspec, qspec],
        out_specs=[qspec, kspec, kspec, pl.BlockSpec((None, 8, LANES), lambda b, g, i: (b * A_KV_HEADS + g, 0, 0))],
        out_shape=[jax.ShapeDtypeStruct((t_all, A_HEADS * LANES), F32),
                   jax.ShapeDtypeStruct((t_all, A_KV_HEADS * LANES), F32),
                   jax.ShapeDtypeStruct((t_all, A_KV_HEADS * LANES), F32),
                   jax.ShapeDtypeStruct((nb * A_KV_HEADS, 8, LANES), F32)],
        compiler_params=_params(("parallel", "parallel", "arbitrary")),
    )(sinks, qa, ka, va, bias, bias, oa, doa, lrow)


def _dproj_dx(pre, dqa, dka, dqb, dkb, dva, dvb, z, x2, dh, gain_row, w_pad_t, g1, seq):
    t_all = pre.shape[0]
    tm = min(256, seq)
    nt = t_all // tm
    tiles_per_seq = seq // tm
    triu = _tri(tm, True)
    sel = _dc_select()

    def body(pre_ref, dqa_ref, dka_ref, dqb_ref, dkb_ref, dva_ref, dvb_ref, z_ref, x_ref, dh_ref, gain_ref, triu_ref,
             sel_ref, w_ref, g_ref, dproj_ref, small_ref, dx_ref, dg_ref, carry_ref):
        i = pl.program_id(0)

        @pl.when(i == 0)
        def _():
            small_ref[...] = jnp.zeros_like(small_ref)
            dg_ref[...] = jnp.zeros_like(dg_ref)

        @pl.when(i % tiles_per_seq == 0)
        def _():
            carry_ref[...] = jnp.zeros_like(carry_ref)

        def norm_bwd(g, dhat):
            cols = slice(g * LANES, (g + 1) * LANES)
            p = pre_ref[:, cols].astype(F32)
            rr = lax.rsqrt(jnp.sum(p * p, axis=-1, keepdims=True) * (1.0 / HEAD_DIM) + EPS)
            n = p * rr
            dz = dhat * gain_ref[:, cols]
            dproj_ref[:, cols] = (rr * (dz - n * (jnp.sum(dz * n, axis=-1, keepdims=True) * (1.0 / HEAD_DIM)))
                                  ).astype(BF16)
            return jnp.sum(dhat * n, axis=0, keepdims=True)

        def group_sum(g0, d_ref, count, scale):
            acc = jnp.zeros((1, LANES), F32)
            for h in range(count):
                d = d_ref[:, h * LANES:(h + 1) * LANES]
                acc = acc + norm_bwd(g0 + h, d * scale if scale != 1.0 else d)
            return acc

        small_ref[0:1, :] += group_sum(G_QA, dqa_ref, A_HEADS, SCALE)
        small_ref[1:2, :] += group_sum(G_KA, dka_ref, A_KV_HEADS, 1.0)
        small_ref[2:3, :] += group_sum(G_QB, dqb_ref, B_HEADS, SCALE)
        small_ref[3:4, :] += group_sum(G_KB, dkb_ref, B_HEADS, LN2)
        dproj_ref[:, G_VA * LANES:G_VB * LANES] = dva_ref[...].astype(BF16)
        dproj_ref[:, G_VB * LANES:G_F * LANES] = dvb_ref[...].astype(BF16)

        dc = jnp.zeros((tm, LANES), F32)
        for piece_q, piece_k in zip(_split3(dqb_ref[...]), _split3(dkb_ref[...])):
            dc = dc + _dot(jnp.concatenate([piece_q, piece_k], axis=1), sel_ref[...])
        dlf = _dot_exact(triu_ref[...], dc) + carry_ref[...]
        carry_ref[...] += jnp.sum(dc, axis=0, keepdims=True)
        dz = dlf / (1.0 + jnp.exp(z_ref[...]))
        small_ref[4:5, :] += jnp.sum(dz, axis=0, keepdims=True)
        dproj_ref[:, G_F * LANES:(G_F + 1) * LANES] = dz.astype(BF16)
        dproj_ref[:, (G_F + 1) * LANES:] = jnp.zeros((tm, LANES), BF16)

        dxn = _dot(dproj_ref[...], w_ref[...])
        x = x_ref[...]
        r = lax.rsqrt(jnp.mean(x * x, axis=-1, keepdims=True) + EPS)
        xh = x * r
        dg_ref[...] += jnp.sum(dxn * xh, axis=0, keepdims=True)
        dxz = dxn * g_ref[...]
        dx_ref[...] = dh_ref[...] + r * (dxz - xh * jnp.mean(dxz * xh, axis=-1, keepdims=True))

    def rev(n):
        return pl.BlockSpec((tm, n), lambda i: (nt - 1 - i, 0))

    return _pcall(
        body, name="dproj_dx", grid=(nt,),
        in_specs=[rev(N_NORM_GROUPS * LANES), rev(A_HEADS * LANES), rev(A_KV_HEADS * LANES), rev(B_HEADS * LANES),
                  rev(B_HEADS * LANES), rev(A_KV_HEADS * LANES), rev(B_HEADS * LANES), rev(LANES), rev(D_MODEL),
                  rev(D_MODEL), _const((1, NP)), _const((tm, tm)), _const(sel.shape), _const((NP, D_MODEL)),
                  _const((1, D_MODEL))],
        out_specs=[rev(NP), _const((8, LANES)), rev(D_MODEL), _const((1, D_MODEL))],
        out_shape=[jax.ShapeDtypeStruct((t_all, NP), BF16), jax.ShapeDtypeStruct((8, LANES), F32),
                   jax.ShapeDtypeStruct((t_all, D_MODEL), F32), jax.ShapeDtypeStruct((1, D_MODEL), F32)],
        scratch_shapes=[pltpu.VMEM((1, LANES), F32)],
        compiler_params=_params(("arbitrary",)),
    )(pre, dqa, dka, dqb, dkb, dva, dvb, z, x2, dh, gain_row, triu, sel, w_pad_t, g1)


def _dwin(dproj, xn):
    t_all = xn.shape[0]
    tt = min(512, t_all)
    half = NP // 2

    def body(a_ref, b_ref, o_ref):
        @pl.when(pl.program_id(1) == 0)
        def _():
            o_ref[...] = jnp.zeros_like(o_ref)

        o_ref[...] += _dot_tn(a_ref[...], b_ref[...])

    return _pcall(
        body, name="dwin", grid=(2, t_all // tt),
        in_specs=[pl.BlockSpec((tt, half), lambda j, t: (t, j)), pl.BlockSpec((tt, D_MODEL), lambda j, t: (t, 0))],
        out_specs=pl.BlockSpec((half, D_MODEL), lambda j, t: (j, 0)),
        out_shape=jax.ShapeDtypeStruct((NP, D_MODEL), F32),
        compiler_params=_params(("parallel", "arbitrary")),
    )(dproj, xn)


ANY = pl.BlockSpec(memory_space=pl.ANY)


def _place():
    return lax.axis_index("x"), lax.axis_index("y"), lax.axis_index("c")


class _Gather:
    def __init__(self, srcs, outs, send_sems, recv_sems, local_sems):
        self.srcs, self.outs = srcs, outs
        self.send_sems, self.recv_sems, self.local_sems = send_sems, recv_sems, local_sems
        x, y, c = _place()
        self.c = c
        self.me, self.sibling = (x, y, c), (x, y, 1 - c)
        self.chips = [(1 - x, y), (x, 1 - y), (1 - x, 1 - y)]

    def _rows(self, a, px, py, pc):
        m = self.srcs[a].shape[0]
        return self.outs[a].at[pl.ds((4 * px + 2 * py + pc) * m, m), :]

    def _copy(self, a, k, block, to, from_src=False):
        return pltpu.make_async_remote_copy(
            src_ref=self.srcs[a] if from_src else self._rows(a, *block), dst_ref=self._rows(a, *block),
            send_sem=self.send_sems.at[k, a], recv_sem=self.recv_sems.at[k, a], device_id=to, device_id_type=MESH)

    def _own(self, a):
        return pltpu.make_async_copy(self.srcs[a], self._rows(a, *self.me), self.local_sems.at[a])

    def start(self):
        for a in range(len(self.srcs)):
            self._own(a).start()
            self._copy(a, 0, self.me, self.sibling, from_src=True).start()
            for j, chip in enumerate(self.chips):
                self._copy(a, 1 + j, self.me, (*chip, self.c), from_src=True).start()

    def forward(self):
        for a in range(len(self.srcs)):
            for j, chip in enumerate(self.chips):
                self._copy(a, 1 + j, (*chip, self.c), self.me).wait_recv()
                self._copy(a, 4 + j, (*chip, self.c), self.sibling).start()

    def finish(self):
        for a in range(len(self.srcs)):
            self._copy(a, 0, self.sibling, self.me).wait_recv()
            for j, chip in enumerate(self.chips):
                self._copy(a, 4 + j, (*chip, 1 - self.c), self.me).wait_recv()
            self._copy(a, 0, self.me, self.sibling, from_src=True).wait_send()
            for j, chip in enumerate(self.chips):
                self._copy(a, 1 + j, self.me, (*chip, self.c), from_src=True).wait_send()
                self._copy(a, 4 + j, (*chip, self.c), self.sibling).wait_send()
            self._own(a).wait()


def _gather_scratch(n_arrays):
    return [pltpu.SemaphoreType.DMA((7, n_arrays)), pltpu.SemaphoreType.DMA((7, n_arrays)),
            pltpu.SemaphoreType.DMA((n_arrays,))]


def _allgather_halves(mine):
    m_per, n = mine.shape

    def body(x_ref, out_ref, send_sems, recv_sems, local_sems):
        gather = _Gather((x_ref,), (out_ref,), send_sems, recv_sems, local_sems)
        gather.start()
        gather.forward()
        gather.finish()

    return _pcall(
        body, name="allgather_w_in",
        out_shape=jax.ShapeDtypeStruct((8 * m_per, n), mine.dtype),
        in_specs=[ANY], out_specs=ANY, scratch_shapes=_gather_scratch(1),
    )(mine)


def _rs_pair_exchange(g, name):
    def body(g_ref, out_ref, send_sem, recv_sem):
        x, y, c = _place()
        cp = pltpu.make_async_remote_copy(
            src_ref=g_ref.at[1 - c], dst_ref=out_ref, send_sem=send_sem, recv_sem=recv_sem,
            device_id=(x, y, 1 - c), device_id_type=MESH)
        cp.start()
        cp.wait()

    return _pcall(
        body, name=name, out_shape=jax.ShapeDtypeStruct(g.shape[1:], F32),
        in_specs=[ANY], out_specs=ANY, scratch_shapes=[pltpu.SemaphoreType.DMA, pltpu.SemaphoreType.DMA],
    )(g)


def _rs_pair_add(g, got, c_idx, name):
    rows = g.shape[2]

    def body(c_ref, a_ref, b_ref, o_ref, ob_ref):
        pair = a_ref[...] + b_ref[...]
        o_ref[...] = pair
        ob_ref[...] = pair.astype(BF16)

    blk = pl.BlockSpec((None, rows, D_MODEL), lambda s, c_ref: (s, 0, 0))
    return _pcall(
        body, name=name,
        grid_spec=pltpu.PrefetchScalarGridSpec(
            num_scalar_prefetch=1, grid=(N_CHIPS,),
            in_specs=[pl.BlockSpec((None, None, rows, D_MODEL), lambda s, c_ref: (c_ref[0], s, 0, 0)), blk],
            out_specs=[blk, blk]),
        out_shape=[jax.ShapeDtypeStruct((N_CHIPS, rows, D_MODEL), F32),
                   jax.ShapeDtypeStruct((N_CHIPS, rows, D_MODEL), BF16)],
        compiler_params=_params(("parallel",)),
    )(c_idx, g, got)


def _chip_exchange_copies(p_ref, out_ref, send_sems, recv_sems):
    x, y, c = _place()
    chips = [(1 - x, y), (x, 1 - y), (1 - x, 1 - y)]
    return [pltpu.make_async_remote_copy(
        src_ref=p_ref.at[2 * cx + cy], dst_ref=out_ref.at[j], send_sem=send_sems.at[j], recv_sem=recv_sems.at[j],
        device_id=(cx, cy, c), device_id_type=MESH) for j, (cx, cy) in enumerate(chips)]


def _rs_chip_exchange(p4, name):
    def body(p_ref, out_ref, send_sems, recv_sems):
        cps = _chip_exchange_copies(p_ref, out_ref, send_sems, recv_sems)
        for cp in cps:
            cp.start()
        for cp in cps:
            cp.wait()

    return _pcall(
        body, name=name, out_shape=jax.ShapeDtypeStruct((3,) + p4.shape[1:], p4.dtype),
        in_specs=[ANY], out_specs=ANY,
        scratch_shapes=[pltpu.SemaphoreType.DMA((3,)), pltpu.SemaphoreType.DMA((3,))],
    )(p4)


def _rs_chip_add(p4, got, sc_idx, name):
    rows = p4.shape[1]
    tr = next(rows // n for n in (8, 7, 6, 5, 4, 3, 2, 1) if rows % n == 0 and (rows // n) % 16 == 0)

    def body(sc_ref, a_ref, b_ref, o_ref):
        o_ref[...] = ((a_ref[...] + b_ref[0].astype(F32)) + b_ref[1].astype(F32)) + b_ref[2].astype(F32)

    return _pcall(
        body, name=name,
        grid_spec=pltpu.PrefetchScalarGridSpec(
            num_scalar_prefetch=1, grid=(rows // tr,),
            in_specs=[pl.BlockSpec((None, tr, D_MODEL), lambda i, sc_ref: (sc_ref[0], i, 0)),
                      pl.BlockSpec((3, tr, D_MODEL), lambda i, sc_ref: (0, i, 0))],
            out_specs=pl.BlockSpec((None, tr, D_MODEL), lambda i, sc_ref: (sc_ref[1], i, 0))),
        out_shape=jax.ShapeDtypeStruct((2, rows, D_MODEL), F32),
        compiler_params=_params(("parallel",)),
    )(sc_idx, p4, got)


def _rs_pair_share(halves, name):
    def body(r_ref, out_ref, send_sem, recv_sem):
        x, y, c = _place()
        cp = pltpu.make_async_remote_copy(
            src_ref=r_ref.at[c], dst_ref=out_ref.at[c], send_sem=send_sem, recv_sem=recv_sem,
            device_id=(x, y, 1 - c), device_id_type=MESH)
        cp.start()
        cp.wait()

    return _pcall(
        body, name=name, out_shape=jax.ShapeDtypeStruct(halves.shape, F32),
        in_specs=[ANY], out_specs=ANY, input_output_aliases={0: 0},
        scratch_shapes=[pltpu.SemaphoreType.DMA, pltpu.SemaphoreType.DMA],
    )(halves)


def _adam(w, g, m, v):
    m2 = ADAM_B1 * m + (1.0 - ADAM_B1) * g
    v2 = ADAM_B2 * v + (1.0 - ADAM_B2) * (g * g)
    m_hat = m2 / (1.0 - ADAM_B1 ** ADAM_STEP)
    v_hat = v2 / (1.0 - ADAM_B2 ** ADAM_STEP)
    return -ADAM_LR * (m_hat / (jnp.sqrt(v_hat) + ADAM_EPS) + ADAM_WD * w), m2, v2


def _small_allreduce_adamw(part, w, m, v):
    pieces = ((0, 8, LANES), (20, 1, B_HEADS), (16, 1, HEAD_DIM), (17, 1, HEAD_DIM), (21, 1, A_HEADS),
              (18, 1, HEAD_DIM), (19, 1, HEAD_DIM), (8, 8, LANES))

    def body(p_ref, w_ref, m_ref, v_ref, *rest):
        outs, (loss_ref, buf, stage, send_sems, recv_sems) = rest[:4 * len(pieces)], rest[4 * len(pieces):]
        x, y, c = _place()
        me = 4 * x + 2 * y + c
        cps = []
        for k in range(1, 8):
            peer = (1 - x if k & 4 else x, 1 - y if k & 2 else y, 1 - c if k & 1 else c)
            cps.append(pltpu.make_async_remote_copy(
                src_ref=p_ref, dst_ref=buf.at[me], send_sem=send_sems.at[k - 1], recv_sem=recv_sems.at[k - 1],
                device_id=peer, device_id_type=MESH))
        for cp in cps:
            cp.start()
        buf[me] = p_ref[...]
        for cp in cps:
            cp.wait()
        g = buf[0]
        for k in range(1, 8):
            g = g + buf[k]
        for kind, packed in enumerate((g,) + _adam(w_ref[...], g, m_ref[...], v_ref[...])):
            stage[...] = packed
            if kind == 0:
                loss_ref[...] = stage[ROW_LOSS:ROW_LOSS + 1, :]
            for i, (row, rows, lanes) in enumerate(pieces):
                outs[kind * len(pieces) + i][...] = stage[row:row + rows, 0:lanes]

    vm = pl.BlockSpec(memory_space=pltpu.VMEM)
    shapes = [jax.ShapeDtypeStruct((rows, lanes), F32) for _ in range(4) for _, rows, lanes in pieces]
    shapes.append(jax.ShapeDtypeStruct((1, LANES), F32))
    res = _pcall(
        body, name="small_allreduce_adamw",
        out_shape=shapes, in_specs=[vm, vm, vm, vm], out_specs=[vm] * len(shapes),
        scratch_shapes=[pltpu.VMEM((8, SMALL_ROWS, LANES), F32), pltpu.VMEM((SMALL_ROWS, LANES), F32),
                        pltpu.SemaphoreType.DMA((7,)), pltpu.SemaphoreType.DMA((7,))],
    )(part, w, m, v)
    flat = [r.reshape(r.size) for r in res[:-1]]
    n = len(pieces)
    return [flat[k * n:(k + 1) * n] for k in range(4)], res[-1][0, 0]


def _adamw(w, g, m, v, name):
    rows, cols = w.shape
    tr = min(256, rows)

    def body(w_ref, g_ref, m_ref, v_ref, d_ref, m2_ref, v2_ref):
        d_ref[...], m2_ref[...], v2_ref[...] = _adam(w_ref[...], g_ref[...], m_ref[...], v_ref[...])

    spec = _rows(tr, cols)
    shp = jax.ShapeDtypeStruct((rows, cols), F32)
    return _pcall(
        body, name=name, grid=(rows // tr,), in_specs=[spec] * 4, out_specs=[spec] * 3, out_shape=[shp] * 3,
        compiler_params=_params(("parallel",)),
    )(w, g, m, v)


def _pad_lanes(v):
    return jnp.pad(v, (0, LANES - v.shape[0]))


def _pad_head_rows(w_t, heads):
    n = w_t.shape[1]
    return jnp.pad(w_t.reshape(heads, HEAD_DIM, n), ((0, 0), (0, LANES - HEAD_DIM), (0, 0))).reshape(heads * LANES, n)


def _unpad_head_rows(w_t, heads):
    n = w_t.shape[1]
    return w_t.reshape(heads, LANES, n)[:, :HEAD_DIM].reshape(heads * HEAD_DIM, n)


def _in_rows_pad(w_in_t):
    qa, ka, va, qb, kb, vb, f = jnp.split(w_in_t, [512, 640, 768, 1280, 1792, 2304], axis=0)
    f = jnp.pad(f, ((0, 2 * LANES - B_HEADS), (0, 0)))
    return jnp.concatenate([_pad_head_rows(qa, 8), _pad_head_rows(ka, 2), _pad_head_rows(qb, 8),
                            _pad_head_rows(kb, 8), _pad_head_rows(va, 2), _pad_head_rows(vb, 8), f], axis=0)


def _in_rows_unpad(d):
    qa = _unpad_head_rows(d[G_QA * LANES:G_KA * LANES], 8)
    ka = _unpad_head_rows(d[G_KA * LANES:G_QB * LANES], 2)
    qb = _unpad_head_rows(d[G_QB * LANES:G_KB * LANES], 8)
    kb = _unpad_head_rows(d[G_KB * LANES:G_VA * LANES], 8)
    va = _unpad_head_rows(d[G_VA * LANES:G_VB * LANES], 2)
    vb = _unpad_head_rows(d[G_VB * LANES:G_F * LANES], 8)
    f = d[G_F * LANES:G_F * LANES + B_HEADS]
    return jnp.concatenate([qa, ka, va, qb, kb, vb, f], axis=0)


def _pack_small(g1, bf, qa, ka, sk, qb, kb, g2, loss_row):
    rows = [g1.reshape(8, LANES), g2.reshape(8, LANES)]
    rows += [_pad_lanes(t)[None] for t in (qa, ka, qb, kb, bf, sk)]
    rows += [loss_row, jnp.zeros((1, LANES), F32)]
    return jnp.concatenate(rows, axis=0)


def kernel(x, attn_norm_g, w_in, b_forget, q_norm_a, k_norm_a, sink_logits, q_norm_b, k_norm_b, w_out, mlp_norm_g, w_up, w_down, loss_target, m_attn_norm_g, m_w_in, m_b_forget, m_q_norm_a, m_k_norm_a, m_sink_logits, m_q_norm_b, m_k_norm_b, m_w_out, m_mlp_norm_g, m_w_up, m_w_down, v_attn_norm_g, v_w_in, v_b_forget, v_q_norm_a, v_k_norm_a, v_sink_logits, v_q_norm_b, v_k_norm_b, v_w_out, v_mlp_norm_g, v_w_up, v_w_down):
    nb, seq, _ = x.shape
    t_all = nb * seq
    c_idx = lax.axis_index("c")
    s_idx = 2 * lax.axis_index("x") + lax.axis_index("y")

    def my_half(a):
        halves = a.astype(BF16).reshape(2, a.shape[0] // 2, a.shape[1])
        return lax.dynamic_slice_in_dim(halves, c_idx, 1, axis=0)[0]

    w_in_shard_t = jnp.pad(w_in.T, ((0, IN_SHARD_P - IN_SHARD), (0, 0)))
    gathered_in = _allgather_halves(my_half(w_in_shard_t)).reshape(N_CHIPS, IN_SHARD_P, D_MODEL)
    w_pad_t = _in_rows_pad(gathered_in[:, :IN_SHARD].reshape(IN_WIDTH, D_MODEL))

    ones = jnp.ones((LANES,), F32)
    gain_row = jnp.concatenate(
        [jnp.tile(_pad_lanes(q_norm_a), 8), jnp.tile(_pad_lanes(k_norm_a), 2), jnp.tile(_pad_lanes(q_norm_b), 8),
         jnp.tile(_pad_lanes(k_norm_b), 8), jnp.tile(ones, N_GROUPS - N_NORM_GROUPS)])[None]
    b_row = _pad_lanes(b_forget)[None]
    g1 = attn_norm_g[None]
    g2 = mlp_norm_g[None]
    slopes = jnp.exp2(-(8.0 / A_HEADS) * (jnp.arange(A_HEADS, dtype=F32) + 1.0))

    x2 = x.reshape(t_all, D_MODEL)
    tgt = loss_target.reshape(t_all, D_MODEL)

    (xn, pre, qa, ka, va, qb, kb, vb, z), (w_out_g, w_up_g, w_down_f) = _inproj(
        x2, g1, w_pad_t, gain_row, b_row, seq, [my_half(w_out), my_half(w_up), my_half(w_down)])
    wo_pad = _pad_head_rows(w_out_g, A_HEADS + B_HEADS)
    w_up_blocks = w_up_g.reshape(N_CHIPS, D_MODEL, D_MODEL)
    swa_bias = _swa_bias(slopes)
    oa, la = _swa_fwd(qa, ka, va, sink_logits, swa_bias, nb, seq)
    ob, lse = _fox_fwd(qb, kb, vb, nb, seq)
    h, hn = _outproj(x2, oa, ob, wo_pad, g2)
    ru, dy, dyb, loss_acc = _mlp_fwd(hn, w_up_blocks, w_down_f, h, tgt)

    du, d_w_mlp = _mlp_bwd_w(dyb, w_down_f, ru, hn)
    c_arg = c_idx.reshape(1).astype(jnp.int32)
    sc_arg = jnp.stack([s_idx, c_idx]).astype(jnp.int32)
    dh, dhb, d_g2, sibling_w_mlp = _mlp_dhn(du, w_up_blocks, h, dy, g2, d_w_mlp)
    pair_m, pair_m_bf = _rs_pair_add(d_w_mlp, sibling_w_mlp, c_arg, "rs_pair_add_mlp")
    doa, dob, delta_b, d_wo = _dmixed(dhb, wo_pad, oa, ob)
    dqb, dkb, dvb, got_m = _fox_bwd(qb, kb, vb, dob, lse, delta_b, nb, seq, pair_m_bf)
    red_m = _rs_pair_share(_rs_chip_add(pair_m, got_m, sc_arg, "rs_chip_add_mlp"), "rs_pair_share_mlp")
    g_w_up, g_w_down = red_m[0], red_m[1]
    dqa, dka, dva, dsink = _swa_bwd(qa, ka, va, oa, doa, la, sink_logits, swa_bias, nb, seq)
    dproj, small, grad_x, d_g1 = _dproj_dx(pre, dqa, dka, dqb, dkb, dva, dvb, z, x2, dh, gain_row, w_pad_t, g1, seq)
    d_w_in_t = _dwin(dproj, xn)

    d_w_out = _unpad_head_rows(d_wo, A_HEADS + B_HEADS)
    g_att = jnp.concatenate([
        jnp.pad(_in_rows_unpad(d_w_in_t).reshape(N_CHIPS, IN_SHARD, D_MODEL),
                ((0, 0), (0, IN_SHARD_P - IN_SHARD), (0, 0))),
        d_w_out.reshape(N_CHIPS, D_MODEL // N_CHIPS, D_MODEL)], axis=1)
    g_att = jnp.stack([g_att[:, :R_ATT // 2], g_att[:, R_ATT // 2:]])
    pair_a, pair_a_bf = _rs_pair_add(g_att, _rs_pair_exchange(g_att, "rs_pair_exchange_att"), c_arg, "rs_pair_add_att")
    got_a = _rs_chip_exchange(pair_a_bf, "rs_chip_exchange_att")
    red_a = _rs_pair_share(_rs_chip_add(pair_a, got_a, sc_arg, "rs_chip_add_att"), "rs_pair_share_att")
    red_a = red_a.reshape(R_ATT, D_MODEL)
    g_w_in = red_a[:IN_SHARD].T
    g_w_out = red_a[IN_SHARD_P:]

    loss_row = loss_acc[0:1] * (0.5 / D_MODEL)
    d_sink = dsink[:, :A_GROUP, 0].reshape(nb, A_HEADS).sum(axis=0)
    part = _pack_small(d_g1[0], small[4, :B_HEADS], small[0, :HEAD_DIM], small[1, :HEAD_DIM], d_sink,
                       small[2, :HEAD_DIM], small[3, :HEAD_DIM], d_g2[0], loss_row)
    zero_row = jnp.zeros((1, LANES), F32)
    smalls = lambda t: _pack_small(*t, zero_row)
    w_small = smalls((attn_norm_g, b_forget, q_norm_a, k_norm_a, sink_logits, q_norm_b, k_norm_b, mlp_norm_g))
    m_small = smalls((m_attn_norm_g, m_b_forget, m_q_norm_a, m_k_norm_a, m_sink_logits, m_q_norm_b, m_k_norm_b,
                      m_mlp_norm_g))
    v_small = smalls((v_attn_norm_g, v_b_forget, v_q_norm_a, v_k_norm_a, v_sink_logits, v_q_norm_b, v_k_norm_b,
                      v_mlp_norm_g))
    (g_s, d_s, m_s, v_s), loss = _small_allreduce_adamw(part, w_small, m_small, v_small)

    big = {}
    for name, w, g, m, v in (("adamw_w_in", w_in, g_w_in, m_w_in, v_w_in),
                             ("adamw_w_out", w_out, g_w_out, m_w_out, v_w_out),
                             ("adamw_w_up", w_up, g_w_up, m_w_up, v_w_up),
                             ("adamw_w_down", w_down, g_w_down, m_w_down, v_w_down)):
        big[name] = (g,) + tuple(_adamw(w, g, m, v, name))

    def assemble(k, s):
        return (s[0], big["adamw_w_in"][k], s[1], s[2], s[3], s[4], s[5], s[6], big["adamw_w_out"][k], s[7],
                big["adamw_w_up"][k], big["adamw_w_down"][k])

    return (loss, grad_x.reshape(nb, seq, D_MODEL), *assemble(0, g_s), *assemble(1, d_s), *assemble(2, m_s),
            *assemble(3, v_s))
```

```python
import functools

import numpy as np
import jax
import jax.numpy as jnp
from jax import lax
from jax.experimental import pallas as pl
from jax.experimental.pallas import tpu as pltpu

F32 = jnp.float32
BF16 = jnp.bfloat16

D_MODEL = 1024
HEAD_DIM = 64
LANES = 128
A_HEADS = 8
A_KV_HEADS = 2
A_GROUP = A_HEADS // A_KV_HEADS
B_HEADS = 8
WINDOW = 128
D_FF = 4096
IN_WIDTH = 2312
EPS = 1e-6
SCALE = 0.125
LOG2E = 1.4426950408889634
LN2 = 0.6931471805599453
CHUNK = 32
FOX_TK = 512
FOX_PARTS = 4
NEG = -1e30

G_QA, G_KA, G_QB, G_KB, G_VA, G_VB, G_F = 0, 8, 10, 18, 26, 28, 36
N_NORM_GROUPS = 26
N_GROUPS = 38
NP = N_GROUPS * LANES
MIXED_P = (A_HEADS + B_HEADS) * LANES

N_CHIPS = 4
IN_SHARD = IN_WIDTH // N_CHIPS
IN_SHARD_P = 608
R_ATT = IN_SHARD_P + D_MODEL // N_CHIPS

SMALL_ROWS = 24
ROW_LOSS = 22

ADAM_LR = 0.001
ADAM_B1 = 0.9
ADAM_B2 = 0.999
ADAM_EPS = 1e-08
ADAM_WD = 0.01
ADAM_STEP = 10

VMEM_LIMIT = 52 * 1024 * 1024
MESH = pl.DeviceIdType.MESH


def _pcall(body, **kw):
    return pl.pallas_call(body, **kw)


def _params(sem=None):
    return pltpu.CompilerParams(dimension_semantics=sem, vmem_limit_bytes=VMEM_LIMIT)


def _dot(a, b):
    return jnp.dot(a, b, preferred_element_type=F32)


def _dot_nt(a, b):
    return lax.dot_general(a, b, (((1,), (1,)), ((), ())), preferred_element_type=F32)


def _dot_tn(a, b):
    return lax.dot_general(a, b, (((0,), (0,)), ((), ())), preferred_element_type=F32)


def _split3(x):
    hi = x.astype(BF16)
    r1 = x - hi.astype(F32)
    mid = r1.astype(BF16)
    lo = (r1 - mid.astype(F32)).astype(BF16)
    return hi, mid, lo


def _dot_exact(mat, x):
    hi, mid, lo = _split3(x)
    return _dot(mat, lo) + _dot(mat, mid) + _dot(mat, hi)


def _const(shape):
    zeros = (0,) * len(shape)
    return pl.BlockSpec(shape, lambda *_: zeros)


def _rows(tm, n):
    return pl.BlockSpec((tm, n), lambda i: (i, 0))


def _aug_select():
    e = np.zeros((3 * LANES, 2 * B_HEADS * LANES), np.float32)
    for j in range(3):
        for h in range(B_HEADS):
            e[j * LANES + h, h * LANES + HEAD_DIM + j] = 1.0
            e[j * LANES + h, (B_HEADS + h) * LANES + HEAD_DIM + 3 + j] = -1.0
    return jnp.asarray(e, BF16)


def _dc_select():
    e = np.zeros((2 * B_HEADS * LANES, LANES), np.float32)
    for h in range(B_HEADS):
        e[h * LANES + HEAD_DIM, h] = 1.0
        e[(B_HEADS + h) * LANES + HEAD_DIM + 3, h] = -1.0
    return jnp.asarray(e, BF16)


def _tri(n, upper):
    t = np.tril(np.ones((n, n), np.float32))
    return jnp.asarray(t.T if upper else t, BF16)


def _inproj(x2, g1, w_pad_t, gain_row, b_row, seq, later_weights):
    t_all = x2.shape[0]
    tm = min(256, seq)
    n_steps = t_all // tm
    forward_step = max(n_steps - 2, 0)
    tiles_per_seq = seq // tm
    tri = _tri(tm, False)
    esel = _aug_select()
    n_later = len(later_weights)

    def body(x_ref, g_ref, w_ref, gain_ref, b_ref, tri_ref, e_ref, *rest):
        later_src, rest = rest[:n_later], rest[n_later:]
        xn_ref, pre_ref, qa_ref, ka_ref, va_ref, qb_ref, kb_ref, vb_ref, z_ref = rest[:9]
        later_out, (carry_ref, send_sems, recv_sems, local_sems) = rest[9:9 + n_later], rest[9 + n_later:]
        i = pl.program_id(0)
        gather = _Gather(later_src, later_out, send_sems, recv_sems, local_sems)

        @pl.when(i == 0)
        def _():
            gather.start()

        @pl.when(i == forward_step)
        def _():
            gather.forward()

        @pl.when(i % tiles_per_seq == 0)
        def _():
            carry_ref[...] = jnp.zeros_like(carry_ref)

        x = x_ref[...]
        r = lax.rsqrt(jnp.mean(x * x, axis=-1, keepdims=True) + EPS)
        xn = (x * r * g_ref[...]).astype(BF16)
        xn_ref[...] = xn
        proj = _dot_nt(xn, w_ref[...])
        pre_ref[...] = proj[:, :N_NORM_GROUPS * LANES].astype(BF16)
        lane = lax.broadcasted_iota(jnp.int32, (tm, LANES), 1)

        z = proj[:, G_F * LANES:(G_F + 1) * LANES] + b_ref[...]
        z_ref[...] = z
        lf = jnp.minimum(z, 0.0) - jnp.log(1.0 + jnp.exp(-jnp.abs(z)))
        lf = jnp.where(lane < B_HEADS, lf, 0.0)
        c = _dot_exact(tri_ref[...], lf) + carry_ref[...]
        carry_ref[...] += jnp.sum(lf, axis=0, keepdims=True)
        aug = _dot(jnp.concatenate(_split3(c * LOG2E), axis=1), e_ref[...])

        def hnorm(g):
            p = proj[:, g * LANES:(g + 1) * LANES]
            rr = lax.rsqrt(jnp.sum(p * p, axis=-1, keepdims=True) * (1.0 / HEAD_DIM) + EPS)
            return p * rr * gain_ref[:, g * LANES:(g + 1) * LANES]

        ones_q = jnp.where((lane >= HEAD_DIM + 3) & (lane < HEAD_DIM + 6), 1.0, 0.0)
        ones_k = jnp.where((lane >= HEAD_DIM) & (lane < HEAD_DIM + 3), 1.0, 0.0)
        for h in range(A_HEADS):
            qa_ref[:, h * LANES:(h + 1) * LANES] = (hnorm(G_QA + h) * SCALE).astype(BF16)
        for h in range(A_KV_HEADS):
            ka_ref[:, h * LANES:(h + 1) * LANES] = hnorm(G_KA + h).astype(BF16)
        for h in range(B_HEADS):
            qb_ref[:, h * LANES:(h + 1) * LANES] = (
                hnorm(G_QB + h) * (SCALE * LOG2E) + aug[:, h * LANES:(h + 1) * LANES] + ones_q).astype(BF16)
            kb_ref[:, h * LANES:(h + 1) * LANES] = (
                hnorm(G_KB + h) + aug[:, (B_HEADS + h) * LANES:(B_HEADS + h + 1) * LANES] + ones_k).astype(BF16)
        va_ref[...] = proj[:, G_VA * LANES:G_VB * LANES].astype(BF16)
        one_v = jnp.where(lane == HEAD_DIM, 1.0, 0.0)
        for h in range(B_HEADS):
            cols = slice((G_VB + h) * LANES, (G_VB + h + 1) * LANES)
            vb_ref[:, h * LANES:(h + 1) * LANES] = (proj[:, cols] + one_v).astype(BF16)

        @pl.when(i == n_steps - 1)
        def _():
            gather.finish()

    widths = [(D_MODEL, BF16), (N_NORM_GROUPS * LANES, BF16), (A_HEADS * LANES, BF16), (A_KV_HEADS * LANES, BF16),
              (A_KV_HEADS * LANES, BF16), (B_HEADS * LANES, BF16), (B_HEADS * LANES, BF16), (B_HEADS * LANES, BF16),
              (LANES, F32)]
    res = _pcall(
        body, name="inproj", grid=(n_steps,),
        in_specs=[_rows(tm, D_MODEL), _const((1, D_MODEL)), _const((NP, D_MODEL)), _const((1, NP)),
                  _const((1, LANES)), _const((tm, tm)), _const(esel.shape)] + [ANY] * n_later,
        out_specs=[_rows(tm, w) for w, _ in widths] + [ANY] * n_later,
        out_shape=[jax.ShapeDtypeStruct((t_all, w), dt) for w, dt in widths]
        + [jax.ShapeDtypeStruct((8 * w.shape[0], w.shape[1]), w.dtype) for w in later_weights],
        scratch_shapes=[pltpu.VMEM((1, LANES), F32)] + _gather_scratch(n_later),
        compiler_params=_params(("arbitrary",)),
    )(x2, g1, w_pad_t, gain_row, b_row, tri, esel, *later_weights)
    return res[:9], res[9:]


def _fox_fwd(qb, kb, vb, nb, seq):
    t_all = qb.shape[0]
    tk = min(FOX_TK, seq // FOX_PARTS)
    tq = FOX_PARTS * tk
    nq = seq // tq

    def body(q_ref, k_ref, v_ref, o_ref, lse_ref, s_ref, p_ref, m_ref, alpha_ref, acc_ref):
        qi = pl.program_id(2)
        q = q_ref[...]
        m_ref[...] = jnp.full((tq, LANES), NEG, F32)
        acc_ref[...] = jnp.zeros((tq, LANES), F32)

        def step(j, modes):
            off = pl.multiple_of(j * tk, tk)
            k = k_ref[pl.ds(off, tk), :]
            v = v_ref[pl.ds(off, tk), :]
            live = [hf for hf in range(FOX_PARTS) if modes[hf] is not None]
            for hf in live:
                s_ref[hf] = _dot_nt(q[hf * tk:(hf + 1) * tk], k)
            for hf in live:
                for r in range(0, tk, CHUNK):
                    rows = slice(r, r + CHUNK)
                    grows = slice(hf * tk + r, hf * tk + r + CHUNK)
                    tiles = []
                    for jt in range(tk // LANES):
                        sc = s_ref[hf, rows, jt * LANES:(jt + 1) * LANES]
                        if modes[hf] == "diag":
                            row = r + lax.broadcasted_iota(jnp.int32, (CHUNK, LANES), 0)
                            col = jt * LANES + lax.broadcasted_iota(jnp.int32, (CHUNK, LANES), 1)
                            sc = jnp.where(row >= col, sc, NEG)
                        tiles.append(sc)
                    m_prev = m_ref[grows, :]
                    m_cur = functools.reduce(jnp.maximum, tiles)
                    m_new = jnp.maximum(m_prev, jnp.max(m_cur, axis=-1, keepdims=True))
                    m_ref[grows, :] = m_new
                    alpha_ref[grows, :] = jnp.exp2(m_prev - m_new)
                    for jt, sc in enumerate(tiles):
                        p_ref[hf, rows, jt * LANES:(jt + 1) * LANES] = jnp.exp2(sc - m_new).astype(BF16)
                hrows = slice(hf * tk, (hf + 1) * tk)
                acc_ref[hrows, :] = alpha_ref[hrows, :] * acc_ref[hrows, :] + _dot(p_ref[hf], v)

        def past(j, carry):
            step(j, ("full",) * FOX_PARTS)
            return carry

        lax.fori_loop(0, FOX_PARTS * qi, past, 0)
        for d in range(FOX_PARTS):
            step(FOX_PARTS * qi + d, (None,) * d + ("diag",) + ("full",) * (FOX_PARTS - 1 - d))
        acc = acc_ref[...]
        lane = lax.broadcasted_iota(jnp.int32, (tq, LANES), 1)
        l = jnp.sum(jnp.where(lane == HEAD_DIM, acc, 0.0), axis=-1, keepdims=True)
        o_ref[...] = (acc / l).astype(BF16)
        lse_ref[...] = m_ref[...] + jnp.log2(l)

    qspec = pl.BlockSpec((tq, LANES), lambda b, h, i: (b * nq + i, h))
    kspec = pl.BlockSpec((seq, LANES), lambda b, h, i: (b, h))
    return _pcall(
        body, name="fox_fwd", grid=(nb, B_HEADS, nq),
        in_specs=[qspec, kspec, kspec], out_specs=[qspec, qspec],
        out_shape=[jax.ShapeDtypeStruct((t_all, B_HEADS * LANES), BF16),
                   jax.ShapeDtypeStruct((t_all, B_HEADS * LANES), F32)],
        scratch_shapes=[pltpu.VMEM((FOX_PARTS, tk, tk), F32), pltpu.VMEM((FOX_PARTS, tk, tk), BF16),
                        pltpu.VMEM((tq, LANES), F32),
                        pltpu.VMEM((tq, LANES), F32), pltpu.VMEM((tq, LANES), F32)],
        compiler_params=_params(("parallel", "parallel", "arbitrary")),
    )(qb, kb, vb)


def _swa_bias(slopes):
    row = jnp.arange(A_GROUP * WINDOW, dtype=jnp.int32)[:, None] % WINDOW
    col = jnp.arange(2 * WINDOW, dtype=jnp.int32)[None, :]
    slope_rows = jnp.repeat(slopes.reshape(A_KV_HEADS, A_GROUP), WINDOW, axis=1)[:, :, None]
    out = []
    for t_rel in (0, WINDOW):
        dist = t_rel + row - col
        valid = (dist >= 0) & (dist < WINDOW)
        out.append(jnp.where(valid[None], -slope_rows * dist.astype(F32)[None], NEG))
    return jnp.stack(out)


def _stack_heads(ref, rows):
    return jnp.concatenate([ref[rows, j * LANES:(j + 1) * LANES] for j in range(A_GROUP)], axis=0)


def _sink_column(sink_ref, g):
    return jnp.concatenate([jnp.full((WINDOW, 1), sink_ref[g * A_GROUP + j], F32) for j in range(A_GROUP)], axis=0)


def _swa_specs(nq, tq, seq):
    smem = pl.BlockSpec(memory_space=pltpu.SMEM)
    qspec = pl.BlockSpec((tq, A_GROUP * LANES), lambda b, g, i: (b * nq + i, g))
    kspec = pl.BlockSpec((seq, LANES), lambda b, g, i: (b, g))
    bias_first = pl.BlockSpec((None, None, A_GROUP * WINDOW, 2 * WINDOW),
                              lambda b, g, i: (jnp.minimum(i, 1), g, 0, 0))
    bias_rest = pl.BlockSpec((None, None, A_GROUP * WINDOW, 2 * WINDOW), lambda b, g, i: (1, g, 0, 0))
    return smem, qspec, kspec, bias_first, bias_rest


def _swa_fwd(qa, ka, va, sinks, bias, nb, seq):
    t_all = qa.shape[0]
    tq = min(512, seq)
    nq = seq // tq

    def body(sink_ref, q_ref, k_ref, v_ref, bias0_ref, bias_ref, o_ref, l_ref):
        qi = pl.program_id(2)
        sink = _sink_column(sink_ref, pl.program_id(1))
        for a in range(tq // WINDOW):
            t0 = qi * tq + a * WINDOW
            start = pl.multiple_of(jnp.maximum(t0 - WINDOW, 0), WINDOW)
            rows = slice(a * WINDOW, (a + 1) * WINDOW)
            k = k_ref[pl.ds(start, 2 * WINDOW), :]
            v = v_ref[pl.ds(start, 2 * WINDOW), :]
            s = _dot_nt(_stack_heads(q_ref, rows), k) + (bias0_ref if a == 0 else bias_ref)[...]
            m = jnp.maximum(jnp.max(s, axis=-1, keepdims=True), sink)
            p = jnp.exp(s - m)
            den = jnp.sum(p, axis=-1, keepdims=True) + jnp.exp(sink - m)
            o = _dot((p * (1.0 / den)).astype(BF16), v).astype(BF16)
            lrow = jnp.broadcast_to(m + jnp.log(den), (A_GROUP * WINDOW, LANES))
            for j in range(A_GROUP):
                o_ref[rows, j * LANES:(j + 1) * LANES] = o[j * WINDOW:(j + 1) * WINDOW]
                l_ref[rows, j * LANES:(j + 1) * LANES] = lrow[j * WINDOW:(j + 1) * WINDOW]

    smem, qspec, kspec, bias_first, bias_rest = _swa_specs(nq, tq, seq)
    return _pcall(
        body, name="swa_fwd", grid=(nb, A_KV_HEADS, nq),
        in_specs=[smem, qspec, kspec, kspec, bias_first, bias_rest], out_specs=[qspec, qspec],
        out_shape=[jax.ShapeDtypeStruct((t_all, A_HEADS * LANES), BF16),
                   jax.ShapeDtypeStruct((t_all, A_HEADS * LANES), F32)],
        compiler_params=_params(("parallel", "parallel", "arbitrary")),
    )(sinks, qa, ka, va, bias, bias)


def _outproj(x2, oa, ob, wo_pad, g2):
    t_all = x2.shape[0]
    tm = min(512, t_all)
    half = A_HEADS * LANES

    def body(x_ref, oa_ref, ob_ref, w_ref, g_ref, h_ref, hn_ref):
        h = x_ref[...] + _dot(oa_ref[...], w_ref[:half, :]) + _dot(ob_ref[...], w_ref[half:, :])
        h_ref[...] = h
        r = lax.rsqrt(jnp.mean(h * h, axis=-1, keepdims=True) + EPS)
        hn_ref[...] = (h * r * g_ref[...]).astype(BF16)

    return _pcall(
        body, name="outproj", grid=(t_all // tm,),
        in_specs=[_rows(tm, D_MODEL), _rows(tm, half), _rows(tm, half), _const((MIXED_P, D_MODEL)),
                  _const((1, D_MODEL))],
        out_specs=[_rows(tm, D_MODEL), _rows(tm, D_MODEL)],
        out_shape=[jax.ShapeDtypeStruct((t_all, D_MODEL), F32), jax.ShapeDtypeStruct((t_all, D_MODEL), BF16)],
        compiler_params=_params(("parallel",)),
    )(x2, oa, ob, wo_pad, g2)


def _mlp_fwd(hn, w_up_blocks, w_down, h, tgt):
    t_all = h.shape[0]
    tm = min(256, t_all)
    nj = D_FF // D_MODEL

    def body(a_ref, wu_ref, wd_ref, h_ref, t_ref, ru_ref, dy_ref, dyb_ref, loss_ref):
        @pl.when(pl.program_id(0) == 0)
        def _():
            loss_ref[...] = jnp.zeros_like(loss_ref)

        a = a_ref[...]
        y = h_ref[...]
        for j in range(nj):
            cols = slice(j * D_MODEL, (j + 1) * D_MODEL)
            ru = jnp.maximum(_dot(a, wu_ref[j]), 0.0)
            ru_ref[:, cols] = ru.astype(BF16)
            y = y + _dot((ru * ru).astype(BF16), wd_ref[cols, :])
        err = y - t_ref[...]
        loss_ref[...] += jnp.sum(err * err)
        dy = err * (1.0 / D_MODEL)
        dy_ref[...] = dy
        dyb_ref[...] = dy.astype(BF16)

    return _pcall(
        body, name="mlp_fwd", grid=(t_all // tm,),
        in_specs=[_rows(tm, D_MODEL), _const((nj, D_MODEL, D_MODEL)), _const((D_FF, D_MODEL)), _rows(tm, D_MODEL),
                  _rows(tm, D_MODEL)],
        out_specs=[_rows(tm, D_FF), _rows(tm, D_MODEL), _rows(tm, D_MODEL), _const((8, LANES))],
        out_shape=[jax.ShapeDtypeStruct((t_all, D_FF), BF16), jax.ShapeDtypeStruct((t_all, D_MODEL), F32),
                   jax.ShapeDtypeStruct((t_all, D_MODEL), BF16), jax.ShapeDtypeStruct((8, LANES), F32)],
        compiler_params=_params(("arbitrary",)),
    )(hn, w_up_blocks, w_down, h, tgt)


def _mlp_bwd_w(dyb, w_down, ru, hn):
    t_all = dyb.shape[0]
    tm = min(512, t_all)
    nj = D_FF // D_MODEL

    def body(dy_ref, w_ref, ru_ref, hn_ref, du_ref, dw_ref):
        @pl.when(pl.program_id(1) == 0)
        def _():
            dw_ref[...] = jnp.zeros_like(dw_ref)

        dy = dy_ref[...]
        ru = ru_ref[...].astype(F32)
        du = (_dot_nt(dy, w_ref[...]) * (2.0 * ru)).astype(BF16)
        du_ref[...] = du
        dw_ref[0] += _dot_tn(hn_ref[...], du)
        dw_ref[1] += _dot_tn((ru * ru).astype(BF16), dy)

    tok = pl.BlockSpec((tm, D_MODEL), lambda j, i: (i, 0))
    blk = pl.BlockSpec((tm, D_MODEL), lambda j, i: (i, j))
    wspec = pl.BlockSpec((2, None, D_MODEL, D_MODEL), lambda j, i: (0, j, 0, 0))
    return _pcall(
        body, name="mlp_bwd_w", grid=(nj, t_all // tm),
        in_specs=[tok, pl.BlockSpec((D_MODEL, D_MODEL), lambda j, i: (j, 0)), blk, tok],
        out_specs=[blk, wspec],
        out_shape=[jax.ShapeDtypeStruct((t_all, D_FF), BF16), jax.ShapeDtypeStruct((2, nj, D_MODEL, D_MODEL), F32)],
        compiler_params=_params(("parallel", "arbitrary")),
    )(dyb, w_down, ru, hn)


def _pair_exchange_copy(g_ref, out_ref, send_sem, recv_sem):
    x, y, c = _place()
    return pltpu.make_async_remote_copy(
        src_ref=g_ref.at[1 - c], dst_ref=out_ref, send_sem=send_sem, recv_sem=recv_sem,
        device_id=(x, y, 1 - c), device_id_type=MESH)


def _mlp_dhn(du, w_up_blocks, h, dy, g2, d_w_mlp):
    t_all = h.shape[0]
    tm = min(256, t_all)
    n_steps = t_all // tm

    def body(a_ref, w_ref, h_ref, dy_ref, g_ref, dw_ref, dh_ref, dhb_ref, dg_ref, got_ref, send_sem, recv_sem):
        @pl.when(pl.program_id(0) == 0)
        def _():
            dg_ref[...] = jnp.zeros_like(dg_ref)
            _pair_exchange_copy(dw_ref, got_ref, send_sem, recv_sem).start()

        dhn = _dot_nt(a_ref[:, :D_MODEL], w_ref[0])
        for j in range(1, D_FF // D_MODEL):
            dhn = dhn + _dot_nt(a_ref[:, j * D_MODEL:(j + 1) * D_MODEL], w_ref[j])
        h = h_ref[...]
        r = lax.rsqrt(jnp.mean(h * h, axis=-1, keepdims=True) + EPS)
        hh = h * r
        dg_ref[...] += jnp.sum(dhn * hh, axis=0, keepdims=True)
        dz = dhn * g_ref[...]
        dh = dy_ref[...] + r * (dz - hh * jnp.mean(dz * hh, axis=-1, keepdims=True))
        dh_ref[...] = dh
        dhb_ref[...] = dh.astype(BF16)

        @pl.when(pl.program_id(0) == n_steps - 1)
        def _():
            _pair_exchange_copy(dw_ref, got_ref, send_sem, recv_sem).wait()

    return _pcall(
        body, name="mlp_dhn", grid=(n_steps,),
        in_specs=[_rows(tm, D_FF), _const((D_FF // D_MODEL, D_MODEL, D_MODEL)), _rows(tm, D_MODEL),
                  _rows(tm, D_MODEL), _const((1, D_MODEL)), ANY],
        out_specs=[_rows(tm, D_MODEL), _rows(tm, D_MODEL), _const((1, D_MODEL)), ANY],
        out_shape=[jax.ShapeDtypeStruct((t_all, D_MODEL), F32), jax.ShapeDtypeStruct((t_all, D_MODEL), BF16),
                   jax.ShapeDtypeStruct((1, D_MODEL), F32), jax.ShapeDtypeStruct(d_w_mlp.shape[1:], F32)],
        scratch_shapes=[pltpu.SemaphoreType.DMA, pltpu.SemaphoreType.DMA],
        compiler_params=_params(("arbitrary",)),
    )(du, w_up_blocks, h, dy, g2, d_w_mlp)


def _dmixed(dhb, wo_pad, oa, ob):
    t_all = dhb.shape[0]
    tm = min(512, t_all)
    half = A_HEADS * LANES

    def body(a_ref, w_ref, oa_ref, ob_ref, da_ref, db_ref, delta_ref, dwo_ref):
        @pl.when(pl.program_id(0) == 0)
        def _():
            dwo_ref[...] = jnp.zeros_like(dwo_ref)

        a = a_ref[...]
        d = _dot_nt(a, w_ref[...])
        da_ref[...] = d[:, :half].astype(BF16)
        db_ref[...] = d[:, half:].astype(BF16)
        for h in range(B_HEADS):
            cols = slice(h * LANES, (h + 1) * LANES)
            prod = d[:, half + h * LANES:half + (h + 1) * LANES] * ob_ref[:, cols].astype(F32)
            delta_ref[:, cols] = jnp.broadcast_to(jnp.sum(prod, axis=-1, keepdims=True), (tm, LANES))
        dwo_ref[:half, :] += _dot_tn(oa_ref[...], a)
        dwo_ref[half:, :] += _dot_tn(ob_ref[...], a)

    return _pcall(
        body, name="dmixed", grid=(t_all // tm,),
        in_specs=[_rows(tm, D_MODEL), _const((MIXED_P, D_MODEL)), _rows(tm, half), _rows(tm, half)],
        out_specs=[_rows(tm, half), _rows(tm, half), _rows(tm, half), _const((MIXED_P, D_MODEL))],
        out_shape=[jax.ShapeDtypeStruct((t_all, half), BF16), jax.ShapeDtypeStruct((t_all, half), BF16),
                   jax.ShapeDtypeStruct((t_all, half), F32), jax.ShapeDtypeStruct((MIXED_P, D_MODEL), F32)],
        compiler_params=_params(("arbitrary",)),
    )(dhb, wo_pad, oa, ob)


def _fox_bwd(qb, kb, vb, dob, lse, delta, nb, seq, pair_sums):
    t_all = qb.shape[0]
    tk = min(FOX_TK, seq // FOX_PARTS)
    tq = FOX_PARTS * tk
    nk = seq // tk

    def body(q_ref, k_ref, v_ref, do_ref, lse_ref, delta_ref, pair_ref, dq_ref, dk_ref, dv_ref, got_ref,
             s_ref, dp_ref, p_ref, ds_ref, dk_acc, dv_acc, send_sems, recv_sems):
        kj = pl.program_id(2)
        bh = pl.program_id(0) * B_HEADS + pl.program_id(1)

        @pl.when((bh == 0) & (kj == 0))
        def _():
            for cp in _chip_exchange_copies(pair_ref, got_ref, send_sems, recv_sems):
                cp.start()

        @pl.when(kj == 0)
        def _():
            dq_ref[...] = jnp.zeros_like(dq_ref)

        dk_acc[...] = jnp.zeros_like(dk_acc)
        dv_acc[...] = jnp.zeros_like(dv_acc)
        k = k_ref[...]
        v = v_ref[...]

        def block(off, r0, r1, masked):
            qrows = pl.ds(pl.multiple_of(off + r0, CHUNK), r1 - r0)
            q = q_ref[qrows, :]
            do = do_ref[qrows, :]
            s_ref[r0:r1, :] = _dot_nt(q, k)
            dp_ref[r0:r1, :] = _dot_nt(do, v)
            for r in range(r0, r1, CHUNK):
                rows = slice(r, r + CHUNK)
                chunk = pl.ds(pl.multiple_of(off + r, CHUNK), CHUNK)
                lse_c = lse_ref[chunk, :]
                delta_c = delta_ref[chunk, :]
                for jt in range(tk // LANES):
                    cols = slice(jt * LANES, (jt + 1) * LANES)
                    p = jnp.exp2(s_ref[rows, cols] - lse_c)
                    if masked:
                        row = r - r0 + lax.broadcasted_iota(jnp.int32, (CHUNK, LANES), 0)
                        col = jt * LANES + lax.broadcasted_iota(jnp.int32, (CHUNK, LANES), 1)
                        p = jnp.where(row >= col, p, 0.0)
                    p_ref[rows, cols] = p.astype(BF16)
                    ds_ref[rows, cols] = (p * (dp_ref[rows, cols] - delta_c)).astype(BF16)
            dv_acc[...] += _dot_tn(p_ref[r0:r1, :], do)
            dk_acc[...] += _dot_tn(ds_ref[r0:r1, :], q)
            dq_ref[qrows, :] += _dot(ds_ref[r0:r1, :], k)

        first = kj // FOX_PARTS
        off_first = pl.multiple_of(first * tq, tq)
        for d in range(FOX_PARTS):
            @pl.when(kj % FOX_PARTS == d)
            def _(d=d):
                block(off_first, d * tk, (d + 1) * tk, True)
                if d < FOX_PARTS - 1:
                    block(off_first, (d + 1) * tk, tq, False)

        def later(i, carry):
            block(pl.multiple_of(i * tq, tq), 0, tq, False)
            return carry

        lax.fori_loop(first + 1, seq // tq, later, 0)
        dk_ref[...] = dk_acc[...]
        dv_ref[...] = dv_acc[...]

        @pl.when((bh == nb * B_HEADS - 1) & (kj == nk - 1))
        def _():
            for cp in _chip_exchange_copies(pair_ref, got_ref, send_sems, recv_sems):
                cp.wait()

    full = pl.BlockSpec((seq, LANES), lambda b, h, j: (b, h))
    tile = pl.BlockSpec((tk, LANES), lambda b, h, j: (b * nk + j, h))
    shp = jax.ShapeDtypeStruct((t_all, B_HEADS * LANES), F32)
    return _pcall(
        body, name="fox_bwd", grid=(nb, B_HEADS, nk),
        in_specs=[full, tile, tile, full, full, full, ANY], out_specs=[full, tile, tile, ANY],
        out_shape=[shp, shp, shp, jax.ShapeDtypeStruct((3,) + pair_sums.shape[1:], pair_sums.dtype)],
        scratch_shapes=[pltpu.VMEM((tq, tk), F32), pltpu.VMEM((tq, tk), F32), pltpu.VMEM((tq, tk), BF16),
                        pltpu.VMEM((tq, tk), BF16), pltpu.VMEM((tk, LANES), F32), pltpu.VMEM((tk, LANES), F32),
                        pltpu.SemaphoreType.DMA((3,)), pltpu.SemaphoreType.DMA((3,))],
        compiler_params=_params(("arbitrary", "arbitrary", "arbitrary")),
    )(qb, kb, vb, dob, lse, delta, pair_sums)


def _swa_bwd(qa, ka, va, oa, doa, lrow, sinks, bias, nb, seq):
    t_all = qa.shape[0]
    tq = min(512, seq)
    nq = seq // tq

    def body(sink_ref, q_ref, k_ref, v_ref, bias0_ref, bias_ref, o_ref, do_ref, l_ref,
             dq_ref, dk_ref, dv_ref, dsink_ref):
        qi = pl.program_id(2)
        sink = _sink_column(sink_ref, pl.program_id(1))

        @pl.when(qi == 0)
        def _():
            dk_ref[...] = jnp.zeros_like(dk_ref)
            dv_ref[...] = jnp.zeros_like(dv_ref)
            dsink_ref[...] = jnp.zeros_like(dsink_ref)

        for a in range(tq // WINDOW):
            t0 = qi * tq + a * WINDOW
            start = pl.multiple_of(jnp.maximum(t0 - WINDOW, 0), WINDOW)
            rows = slice(a * WINDOW, (a + 1) * WINDOW)
            win = pl.ds(start, 2 * WINDOW)
            q = _stack_heads(q_ref, rows)
            k = k_ref[win, :]
            v = v_ref[win, :]
            do = _stack_heads(do_ref, rows)
            lrow_t = jnp.max(_stack_heads(l_ref, rows), axis=-1, keepdims=True)
            p = jnp.exp(_dot_nt(q, k) + (bias0_ref if a == 0 else bias_ref)[...] - lrow_t)
            delta = jnp.sum(do.astype(F32) * _stack_heads(o_ref, rows).astype(F32), axis=-1, keepdims=True)
            ds = (p * (_dot_nt(do, v) - delta)).astype(BF16)
            dq = _dot(ds, k)
            dk_ref[win, :] += _dot_tn(ds, q)
            dv_ref[win, :] += _dot_tn(p.astype(BF16), do)
            sink_term = jnp.exp(sink - lrow_t) * delta
            for j in range(A_GROUP):
                part = slice(j * WINDOW, (j + 1) * WINDOW)
                dq_ref[rows, j * LANES:(j + 1) * LANES] = dq[part]
                dsink_ref[j:j + 1, :] -= jnp.broadcast_to(jnp.sum(sink_term[part], axis=0, keepdims=True), (1, LANES))

    smem, qspec, kspec, bias_first, bias_rest = _swa_specs(nq, tq, seq)
    return _pcall(
        body, name="swa_bwd", grid=(nb, A_KV_HEADS, nq),
        in_specs=[smem, qspec, kspec, kspec, bias_first, bias_rest, qspec, qspec, qspec],
        out_specs=[qspec, kspec, kspec, pl.BlockSpec((None, 8, LANES), lambda b, g, i: (b * A_KV_HEADS + g, 0, 0))],
        out_shape=[jax.ShapeDtypeStruct((t_all, A_HEADS * LANES), F32),
                   jax.ShapeDtypeStruct((t_all, A_KV_HEADS * LANES), F32),
                   jax.ShapeDtypeStruct((t_all, A_KV_HEADS * LANES), F32),
                   jax.ShapeDtypeStruct((nb * A_KV_HEADS, 8, LANES), F32)],
        compiler_params=_params(("parallel", "parallel", "arbitrary")),
    )(sinks, qa, ka, va, bias, bias, oa, doa, lrow)


def _dproj_dx(pre, dqa, dka, dqb, dkb, dva, dvb, z, x2, dh, gain_row, w_pad_t, g1, seq):
    t_all = pre.shape[0]
    tm = min(256, seq)
    nt = t_all // tm
    tiles_per_seq = seq // tm
    triu = _tri(tm, True)
    sel = _dc_select()

    def body(pre_ref, dqa_ref, dka_ref, dqb_ref, dkb_ref, dva_ref, dvb_ref, z_ref, x_ref, dh_ref, gain_ref, triu_ref,
             sel_ref, w_ref, g_ref, dproj_ref, small_ref, dx_ref, dg_ref, carry_ref):
        i = pl.program_id(0)

        @pl.when(i == 0)
        def _():
            small_ref[...] = jnp.zeros_like(small_ref)
            dg_ref[...] = jnp.zeros_like(dg_ref)

        @pl.when(i % tiles_per_seq == 0)
        def _():
            carry_ref[...] = jnp.zeros_like(carry_ref)

        def norm_bwd(g, dhat):
            cols = slice(g * LANES, (g + 1) * LANES)
            p = pre_ref[:, cols].astype(F32)
            rr = lax.rsqrt(jnp.sum(p * p, axis=-1, keepdims=True) * (1.0 / HEAD_DIM) + EPS)
            n = p * rr
            dz = dhat * gain_ref[:, cols]
            dproj_ref[:, cols] = (rr * (dz - n * (jnp.sum(dz * n, axis=-1, keepdims=True) * (1.0 / HEAD_DIM)))
                                  ).astype(BF16)
            return jnp.sum(dhat * n, axis=0, keepdims=True)

        def group_sum(g0, d_ref, count, scale):
            acc = jnp.zeros((1, LANES), F32)
            for h in range(count):
                d = d_ref[:, h * LANES:(h + 1) * LANES]
                acc = acc + norm_bwd(g0 + h, d * scale if scale != 1.0 else d)
            return acc

        small_ref[0:1, :] += group_sum(G_QA, dqa_ref, A_HEADS, SCALE)
        small_ref[1:2, :] += group_sum(G_KA, dka_ref, A_KV_HEADS, 1.0)
        small_ref[2:3, :] += group_sum(G_QB, dqb_ref, B_HEADS, SCALE)
        small_ref[3:4, :] += group_sum(G_KB, dkb_ref, B_HEADS, LN2)
        dproj_ref[:, G_VA * LANES:G_VB * LANES] = dva_ref[...].astype(BF16)
        dproj_ref[:, G_VB * LANES:G_F * LANES] = dvb_ref[...].astype(BF16)

        dc = jnp.zeros((tm, LANES), F32)
        for piece_q, piece_k in zip(_split3(dqb_ref[...]), _split3(dkb_ref[...])):
            dc = dc + _dot(jnp.concatenate([piece_q, piece_k], axis=1), sel_ref[...])
        dlf = _dot_exact(triu_ref[...], dc) + carry_ref[...]
        carry_ref[...] += jnp.sum(dc, axis=0, keepdims=True)
        dz = dlf / (1.0 + jnp.exp(z_ref[...]))
        small_ref[4:5, :] += jnp.sum(dz, axis=0, keepdims=True)
        dproj_ref[:, G_F * LANES:(G_F + 1) * LANES] = dz.astype(BF16)
        dproj_ref[:, (G_F + 1) * LANES:] = jnp.zeros((tm, LANES), BF16)

        dxn = _dot(dproj_ref[...], w_ref[...])
        x = x_ref[...]
        r = lax.rsqrt(jnp.mean(x * x, axis=-1, keepdims=True) + EPS)
        xh = x * r
        dg_ref[...] += jnp.sum(dxn * xh, axis=0, keepdims=True)
        dxz = dxn * g_ref[...]
        dx_ref[...] = dh_ref[...] + r * (dxz - xh * jnp.mean(dxz * xh, axis=-1, keepdims=True))

    def rev(n):
        return pl.BlockSpec((tm, n), lambda i: (nt - 1 - i, 0))

    return _pcall(
        body, name="dproj_dx", grid=(nt,),
        in_specs=[rev(N_NORM_GROUPS * LANES), rev(A_HEADS * LANES), rev(A_KV_HEADS * LANES), rev(B_HEADS * LANES),
                  rev(B_HEADS * LANES), rev(A_KV_HEADS * LANES), rev(B_HEADS * LANES), rev(LANES), rev(D_MODEL),
                  rev(D_MODEL), _const((1, NP)), _const((tm, tm)), _const(sel.shape), _const((NP, D_MODEL)),
                  _const((1, D_MODEL))],
        out_specs=[rev(NP), _const((8, LANES)), rev(D_MODEL), _const((1, D_MODEL))],
        out_shape=[jax.ShapeDtypeStruct((t_all, NP), BF16), jax.ShapeDtypeStruct((8, LANES), F32),
                   jax.ShapeDtypeStruct((t_all, D_MODEL), F32), jax.ShapeDtypeStruct((1, D_MODEL), F32)],
        scratch_shapes=[pltpu.VMEM((1, LANES), F32)],
        compiler_params=_params(("arbitrary",)),
    )(pre, dqa, dka, dqb, dkb, dva, dvb, z, x2, dh, gain_row, triu, sel, w_pad_t, g1)


def _dwin(dproj, xn):
    t_all = xn.shape[0]
    tt = min(512, t_all)
    half = NP // 2

    def body(a_ref, b_ref, o_ref):
        @pl.when(pl.program_id(1) == 0)
        def _():
            o_ref[...] = jnp.zeros_like(o_ref)

        o_ref[...] += _dot_tn(a_ref[...], b_ref[...])

    return _pcall(
        body, name="dwin", grid=(2, t_all // tt),
        in_specs=[pl.BlockSpec((tt, half), lambda j, t: (t, j)), pl.BlockSpec((tt, D_MODEL), lambda j, t: (t, 0))],
        out_specs=pl.BlockSpec((half, D_MODEL), lambda j, t: (j, 0)),
        out_shape=jax.ShapeDtypeStruct((NP, D_MODEL), F32),
        compiler_params=_params(("parallel", "arbitrary")),
    )(dproj, xn)


ANY = pl.BlockSpec(memory_space=pl.ANY)


def _place():
    return lax.axis_index("x"), lax.axis_index("y"), lax.axis_index("c")


class _Gather:
    def __init__(self, srcs, outs, send_sems, recv_sems, local_sems):
        self.srcs, self.outs = srcs, outs
        self.send_sems, self.recv_sems, self.local_sems = send_sems, recv_sems, local_sems
        x, y, c = _place()
        self.c = c
        self.me, self.sibling = (x, y, c), (x, y, 1 - c)
        self.chips = [(1 - x, y), (x, 1 - y), (1 - x, 1 - y)]

    def _rows(self, a, px, py, pc):
        m = self.srcs[a].shape[0]
        return self.outs[a].at[pl.ds((4 * px + 2 * py + pc) * m, m), :]

    def _copy(self, a, k, block, to, from_src=False):
        return pltpu.make_async_remote_copy(
            src_ref=self.srcs[a] if from_src else self._rows(a, *block), dst_ref=self._rows(a, *block),
            send_sem=self.send_sems.at[k, a], recv_sem=self.recv_sems.at[k, a], device_id=to, device_id_type=MESH)

    def _own(self, a):
        return pltpu.make_async_copy(self.srcs[a], self._rows(a, *self.me), self.local_sems.at[a])

    def start(self):
        for a in range(len(self.srcs)):
            self._own(a).start()
            self._copy(a, 0, self.me, self.sibling, from_src=True).start()
            for j, chip in enumerate(self.chips):
                self._copy(a, 1 + j, self.me, (*chip, self.c), from_src=True).start()

    def forward(self):
        for a in range(len(self.srcs)):
            for j, chip in enumerate(self.chips):
                self._copy(a, 1 + j, (*chip, self.c), self.me).wait_recv()
                self._copy(a, 4 + j, (*chip, self.c), self.sibling).start()

    def finish(self):
        for a in range(len(self.srcs)):
            self._copy(a, 0, self.sibling, self.me).wait_recv()
            for j, chip in enumerate(self.chips):
                self._copy(a, 4 + j, (*chip, 1 - self.c), self.me).wait_recv()
            self._copy(a, 0, self.me, self.sibling, from_src=True).wait_send()
            for j, chip in enumerate(self.chips):
                self._copy(a, 1 + j, self.me, (*chip, self.c), from_src=True).wait_send()
                self._copy(a, 4 + j, (*chip, self.c), self.sibling).wait_send()
            self._own(a).wait()


def _gather_scratch(n_arrays):
    return [pltpu.SemaphoreType.DMA((7, n_arrays)), pltpu.SemaphoreType.DMA((7, n_arrays)),
            pltpu.SemaphoreType.DMA((n_arrays,))]


def _allgather_halves(mine):
    m_per, n = mine.shape

    def body(x_ref, out_ref, send_sems, recv_sems, local_sems):
        gather = _Gather((x_ref,), (out_ref,), send_sems, recv_sems, local_sems)
        gather.start()
        gather.forward()
        gather.finish()

    return _pcall(
        body, name="allgather_w_in",
        out_shape=jax.ShapeDtypeStruct((8 * m_per, n), mine.dtype),
        in_specs=[ANY], out_specs=ANY, scratch_shapes=_gather_scratch(1),
    )(mine)


def _rs_pair_exchange(g, name):
    def body(g_ref, out_ref, send_sem, recv_sem):
        x, y, c = _place()
        cp = pltpu.make_async_remote_copy(
            src_ref=g_ref.at[1 - c], dst_ref=out_ref, send_sem=send_sem, recv_sem=recv_sem,
            device_id=(x, y, 1 - c), device_id_type=MESH)
        cp.start()
        cp.wait()

    return _pcall(
        body, name=name, out_shape=jax.ShapeDtypeStruct(g.shape[1:], F32),
        in_specs=[ANY], out_specs=ANY, scratch_shapes=[pltpu.SemaphoreType.DMA, pltpu.SemaphoreType.DMA],
    )(g)


def _rs_pair_add(g, got, c_idx, name):
    rows = g.shape[2]

    def body(c_ref, a_ref, b_ref, o_ref, ob_ref):
        pair = a_ref[...] + b_ref[...]
        o_ref[...] = pair
        ob_ref[...] = pair.astype(BF16)

    blk = pl.BlockSpec((None, rows, D_MODEL), lambda s, c_ref: (s, 0, 0))
    return _pcall(
        body, name=name,
        grid_spec=pltpu.PrefetchScalarGridSpec(
            num_scalar_prefetch=1, grid=(N_CHIPS,),
            in_specs=[pl.BlockSpec((None, None, rows, D_MODEL), lambda s, c_ref: (c_ref[0], s, 0, 0)), blk],
            out_specs=[blk, blk]),
        out_shape=[jax.ShapeDtypeStruct((N_CHIPS, rows, D_MODEL), F32),
                   jax.ShapeDtypeStruct((N_CHIPS, rows, D_MODEL), BF16)],
        compiler_params=_params(("parallel",)),
    )(c_idx, g, got)


def _chip_exchange_copies(p_ref, out_ref, send_sems, recv_sems):
    x, y, c = _place()
    chips = [(1 - x, y), (x, 1 - y), (1 - x, 1 - y)]
    return [pltpu.make_async_remote_copy(
        src_ref=p_ref.at[2 * cx + cy], dst_ref=out_ref.at[j], send_sem=send_sems.at[j], recv_sem=recv_sems.at[j],
        device_id=(cx, cy, c), device_id_type=MESH) for j, (cx, cy) in enumerate(chips)]


def _rs_chip_exchange(p4, name):
    def body(p_ref, out_ref, send_sems, recv_sems):
        cps = _chip_exchange_copies(p_ref, out_ref, send_sems, recv_sems)
        for cp in cps:
            cp.start()
        for cp in cps:
            cp.wait()

    return _pcall(
        body, name=name, out_shape=jax.ShapeDtypeStruct((3,) + p4.shape[1:], p4.dtype),
        in_specs=[ANY], out_specs=ANY,
        scratch_shapes=[pltpu.SemaphoreType.DMA((3,)), pltpu.SemaphoreType.DMA((3,))],
    )(p4)


def _rs_chip_add(p4, got, sc_idx, name):
    rows = p4.shape[1]
    tr = next(rows // n for n in (8, 7, 6, 5, 4, 3, 2, 1) if rows % n == 0 and (rows // n) % 16 == 0)

    def body(sc_ref, a_ref, b_ref, o_ref):
        o_ref[...] = ((a_ref[...] + b_ref[0].astype(F32)) + b_ref[1].astype(F32)) + b_ref[2].astype(F32)

    return _pcall(
        body, name=name,
        grid_spec=pltpu.PrefetchScalarGridSpec(
            num_scalar_prefetch=1, grid=(rows // tr,),
            in_specs=[pl.BlockSpec((None, tr, D_MODEL), lambda i, sc_ref: (sc_ref[0], i, 0)),
                      pl.BlockSpec((3, tr, D_MODEL), lambda i, sc_ref: (0, i, 0))],
            out_specs=pl.BlockSpec((None, tr, D_MODEL), lambda i, sc_ref: (sc_ref[1], i, 0))),
        out_shape=jax.ShapeDtypeStruct((2, rows, D_MODEL), F32),
        compiler_params=_params(("parallel",)),
    )(sc_idx, p4, got)


def _rs_pair_share(halves, name):
    def body(r_ref, out_ref, send_sem, recv_sem):
        x, y, c = _place()
        cp = pltpu.make_async_remote_copy(
            src_ref=r_ref.at[c], dst_ref=out_ref.at[c], send_sem=send_sem, recv_sem=recv_sem,
            device_id=(x, y, 1 - c), device_id_type=MESH)
        cp.start()
        cp.wait()

    return _pcall(
        body, name=name, out_shape=jax.ShapeDtypeStruct(halves.shape, F32),
        in_specs=[ANY], out_specs=ANY, input_output_aliases={0: 0},
        scratch_shapes=[pltpu.SemaphoreType.DMA, pltpu.SemaphoreType.DMA],
    )(halves)


def _adam(w, g, m, v):
    m2 = ADAM_B1 * m + (1.0 - ADAM_B1) * g
    v2 = ADAM_B2 * v + (1.0 - ADAM_B2) * (g * g)
    m_hat = m2 / (1.0 - ADAM_B1 ** ADAM_STEP)
    v_hat = v2 / (1.0 - ADAM_B2 ** ADAM_STEP)
    return -ADAM_LR * (m_hat / (jnp.sqrt(v_hat) + ADAM_EPS) + ADAM_WD * w), m2, v2


def _small_allreduce_adamw(part, w, m, v):
    pieces = ((0, 8, LANES), (20, 1, B_HEADS), (16, 1, HEAD_DIM), (17, 1, HEAD_DIM), (21, 1, A_HEADS),
              (18, 1, HEAD_DIM), (19, 1, HEAD_DIM), (8, 8, LANES))

    def body(p_ref, w_ref, m_ref, v_ref, *rest):
        outs, (loss_ref, buf, stage, send_sems, recv_sems) = rest[:4 * len(pieces)], rest[4 * len(pieces):]
        x, y, c = _place()
        me = 4 * x + 2 * y + c
        cps = []
        for k in range(1, 8):
            peer = (1 - x if k & 4 else x, 1 - y if k & 2 else y, 1 - c if k & 1 else c)
            cps.append(pltpu.make_async_remote_copy(
                src_ref=p_ref, dst_ref=buf.at[me], send_sem=send_sems.at[k - 1], recv_sem=recv_sems.at[k - 1],
                device_id=peer, device_id_type=MESH))
        for cp in cps:
            cp.start()
        buf[me] = p_ref[...]
        for cp in cps:
            cp.wait()
        g = buf[0]
        for k in range(1, 8):
            g = g + buf[k]
        for kind, packed in enumerate((g,) + _adam(w_ref[...], g, m_ref[...], v_ref[...])):
            stage[...] = packed
            if kind == 0:
                loss_ref[...] = stage[ROW_LOSS:ROW_LOSS + 1, :]
            for i, (row, rows, lanes) in enumerate(pieces):
                outs[kind * len(pieces) + i][...] = stage[row:row + rows, 0:lanes]

    vm = pl.BlockSpec(memory_space=pltpu.VMEM)
    shapes = [jax.ShapeDtypeStruct((rows, lanes), F32) for _ in range(4) for _, rows, lanes in pieces]
    shapes.append(jax.ShapeDtypeStruct((1, LANES), F32))
    res = _pcall(
        body, name="small_allreduce_adamw",
        out_shape=shapes, in_specs=[vm, vm, vm, vm], out_specs=[vm] * len(shapes),
        scratch_shapes=[pltpu.VMEM((8, SMALL_ROWS, LANES), F32), pltpu.VMEM((SMALL_ROWS, LANES), F32),
                        pltpu.SemaphoreType.DMA((7,)), pltpu.SemaphoreType.DMA((7,))],
    )(part, w, m, v)
    flat = [r.reshape(r.size) for r in res[:-1]]
    n = len(pieces)
    return [flat[k * n:(k + 1) * n] for k in range(4)], res[-1][0, 0]


def _adamw(w, g, m, v, name):
    rows, cols = w.shape
    tr = min(256, rows)

    def body(w_ref, g_ref, m_ref, v_ref, d_ref, m2_ref, v2_ref):
        d_ref[...], m2_ref[...], v2_ref[...] = _adam(w_ref[...], g_ref[...], m_ref[...], v_ref[...])

    spec = _rows(tr, cols)
    shp = jax.ShapeDtypeStruct((rows, cols), F32)
    return _pcall(
        body, name=name, grid=(rows // tr,), in_specs=[spec] * 4, out_specs=[spec] * 3, out_shape=[shp] * 3,
        compiler_params=_params(("parallel",)),
    )(w, g, m, v)


def _pad_lanes(v):
    return jnp.pad(v, (0, LANES - v.shape[0]))


def _pad_head_rows(w_t, heads):
    n = w_t.shape[1]
    return jnp.pad(w_t.reshape(heads, HEAD_DIM, n), ((0, 0), (0, LANES - HEAD_DIM), (0, 0))).reshape(heads * LANES, n)


def _unpad_head_rows(w_t, heads):
    n = w_t.shape[1]
    return w_t.reshape(heads, LANES, n)[:, :HEAD_DIM].reshape(heads * HEAD_DIM, n)


def _in_rows_pad(w_in_t):
    qa, ka, va, qb, kb, vb, f = jnp.split(w_in_t, [512, 640, 768, 1280, 1792, 2304], axis=0)
    f = jnp.pad(f, ((0, 2 * LANES - B_HEADS), (0, 0)))
    return jnp.concatenate([_pad_head_rows(qa, 8), _pad_head_rows(ka, 2), _pad_head_rows(qb, 8),
                            _pad_head_rows(kb, 8), _pad_head_rows(va, 2), _pad_head_rows(vb, 8), f], axis=0)


def _in_rows_unpad(d):
    qa = _unpad_head_rows(d[G_QA * LANES:G_KA * LANES], 8)
    ka = _unpad_head_rows(d[G_KA * LANES:G_QB * LANES], 2)
    qb = _unpad_head_rows(d[G_QB * LANES:G_KB * LANES], 8)
    kb = _unpad_head_rows(d[G_KB * LANES:G_VA * LANES], 8)
    va = _unpad_head_rows(d[G_VA * LANES:G_VB * LANES], 2)
    vb = _unpad_head_rows(d[G_VB * LANES:G_F * LANES], 8)
    f = d[G_F * LANES:G_F * LANES + B_HEADS]
    return jnp.concatenate([qa, ka, va, qb, kb, vb, f], axis=0)


def _pack_small(g1, bf, qa, ka, sk, qb, kb, g2, loss_row):
    rows = [g1.reshape(8, LANES), g2.reshape(8, LANES)]
    rows += [_pad_lanes(t)[None] for t in (qa, ka, qb, kb, bf, sk)]
    rows += [loss_row, jnp.zeros((1, LANES), F32)]
    return jnp.concatenate(rows, axis=0)


def kernel(x, attn_norm_g, w_in, b_forget, q_norm_a, k_norm_a, sink_logits, q_norm_b, k_norm_b, w_out, mlp_norm_g, w_up, w_down, loss_target, m_attn_norm_g, m_w_in, m_b_forget, m_q_norm_a, m_k_norm_a, m_sink_logits, m_q_norm_b, m_k_norm_b, m_w_out, m_mlp_norm_g, m_w_up, m_w_down, v_attn_norm_g, v_w_in, v_b_forget, v_q_norm_a, v_k_norm_a, v_sink_logits, v_q_norm_b, v_k_norm_b, v_w_out, v_mlp_norm_g, v_w_up, v_w_down):
    nb, seq, _ = x.shape
    t_all = nb * seq
    c_idx = lax.axis_index("c")
    s_idx = 2 * lax.axis_index("x") + lax.axis_index("y")

    def my_half(a):
        halves = a.astype(BF16).reshape(2, a.shape[0] // 2, a.shape[1])
        return lax.dynamic_slice_in_dim(halves, c_idx, 1, axis=0)[0]

    w_in_shard_t = jnp.pad(w_in.T, ((0, IN_SHARD_P - IN_SHARD), (0, 0)))
    gathered_in = _allgather_halves(my_half(w_in_shard_t)).reshape(N_CHIPS, IN_SHARD_P, D_MODEL)
    w_pad_t = _in_rows_pad(gathered_in[:, :IN_SHARD].reshape(IN_WIDTH, D_MODEL))

    ones = jnp.ones((LANES,), F32)
    gain_row = jnp.concatenate(
        [jnp.tile(_pad_lanes(q_norm_a), 8), jnp.tile(_pad_lanes(k_norm_a), 2), jnp.tile(_pad_lanes(q_norm_b), 8),
         jnp.tile(_pad_lanes(k_norm_b), 8), jnp.tile(ones, N_GROUPS - N_NORM_GROUPS)])[None]
    b_row = _pad_lanes(b_forget)[None]
    g1 = attn_norm_g[None]
    g2 = mlp_norm_g[None]
    slopes = jnp.exp2(-(8.0 / A_HEADS) * (jnp.arange(A_HEADS, dtype=F32) + 1.0))

    x2 = x.reshape(t_all, D_MODEL)
    tgt = loss_target.reshape(t_all, D_MODEL)

    (xn, pre, qa, ka, va, qb, kb, vb, z), (w_out_g, w_up_g, w_down_f) = _inproj(
        x2, g1, w_pad_t, gain_row, b_row, seq, [my_half(w_out), my_half(w_up), my_half(w_down)])
    wo_pad = _pad_head_rows(w_out_g, A_HEADS + B_HEADS)
    w_up_blocks = w_up_g.reshape(N_CHIPS, D_MODEL, D_MODEL)
    swa_bias = _swa_bias(slopes)
    oa, la = _swa_fwd(qa, ka, va, sink_logits, swa_bias, nb, seq)
    ob, lse = _fox_fwd(qb, kb, vb, nb, seq)
    h, hn = _outproj(x2, oa, ob, wo_pad, g2)
    ru, dy, dyb, loss_acc = _mlp_fwd(hn, w_up_blocks, w_down_f, h, tgt)

    du, d_w_mlp = _mlp_bwd_w(dyb, w_down_f, ru, hn)
    c_arg = c_idx.reshape(1).astype(jnp.int32)
    sc_arg = jnp.stack([s_idx, c_idx]).astype(jnp.int32)
    dh, dhb, d_g2, sibling_w_mlp = _mlp_dhn(du, w_up_blocks, h, dy, g2, d_w_mlp)
    pair_m, pair_m_bf = _rs_pair_add(d_w_mlp, sibling_w_mlp, c_arg, "rs_pair_add_mlp")
    doa, dob, delta_b, d_wo = _dmixed(dhb, wo_pad, oa, ob)
    dqb, dkb, dvb, got_m = _fox_bwd(qb, kb, vb, dob, lse, delta_b, nb, seq, pair_m_bf)
    red_m = _rs_pair_share(_rs_chip_add(pair_m, got_m, sc_arg, "rs_chip_add_mlp"), "rs_pair_share_mlp")
    g_w_up, g_w_down = red_m[0], red_m[1]
    dqa, dka, dva, dsink = _swa_bwd(qa, ka, va, oa, doa, la, sink_logits, swa_bias, nb, seq)
    dproj, small, grad_x, d_g1 = _dproj_dx(pre, dqa, dka, dqb, dkb, dva, dvb, z, x2, dh, gain_row, w_pad_t, g1, seq)
    d_w_in_t = _dwin(dproj, xn)

    d_w_out = _unpad_head_rows(d_wo, A_HEADS + B_HEADS)
    g_att = jnp.concatenate([
        jnp.pad(_in_rows_unpad(d_w_in_t).reshape(N_CHIPS, IN_SHARD, D_MODEL),
                ((0, 0), (0, IN_SHARD_P - IN_SHARD), (0, 0))),
        d_w_out.reshape(N_CHIPS, D_MODEL // N_CHIPS, D_MODEL)], axis=1)
    g_att = jnp.stack([g_att[:, :R_ATT // 2], g_att[:, R_ATT // 2:]])
    pair_a, pair_a_bf = _rs_pair_add(g_att, _rs_pair_exchange(g_att, "rs_pair_exchange_att"), c_arg, "rs_pair_add_att")
    got_a = _rs_chip_exchange(pair_a_bf, "rs_chip_exchange_att")
    red_a = _rs_pair_share(_rs_chip_add(pair_a, got_a, sc_arg, "rs_chip_add_att"), "rs_pair_share_att")
    red_a = red_a.reshape(R_ATT, D_MODEL)
    g_w_in = red_a[:IN_SHARD].T
    g_w_out = red_a[IN_SHARD_P:]

    loss_row = loss_acc[0:1] * (0.5 / D_MODEL)
    d_sink = dsink[:, :A_GROUP, 0].reshape(nb, A_HEADS).sum(axis=0)
    part = _pack_small(d_g1[0], small[4, :B_HEADS], small[0, :HEAD_DIM], small[1, :HEAD_DIM], d_sink,
                       small[2, :HEAD_DIM], small[3, :HEAD_DIM], d_g2[0], loss_row)
    zero_row = jnp.zeros((1, LANES), F32)
    smalls = lambda t: _pack_small(*t, zero_row)
    w_small = smalls((attn_norm_g, b_forget, q_norm_a, k_norm_a, sink_logits, q_norm_b, k_norm_b, mlp_norm_g))
    m_small = smalls((m_attn_norm_g, m_b_forget, m_q_norm_a, m_k_norm_a, m_sink_logits, m_q_norm_b, m_k_norm_b,
                      m_mlp_norm_g))
    v_small = smalls((v_attn_norm_g, v_b_forget, v_q_norm_a, v_k_norm_a, v_sink_logits, v_q_norm_b, v_k_norm_b,
                      v_mlp_norm_g))
    (g_s, d_s, m_s, v_s), loss = _small_allreduce_adamw(part, w_small, m_small, v_small)

    big = {}
    for name, w, g, m, v in (("adamw_w_in", w_in, g_w_in, m_w_in, v_w_in),
                             ("adamw_w_out", w_out, g_w_out, m_w_out, v_w_out),
                             ("adamw_w_up", w_up, g_w_up, m_w_up, v_w_up),
                             ("adamw_w_down", w_down, g_w_down, m_w_down, v_w_down)):
        big[name] = (g,) + tuple(_adamw(w, g, m, v, name))

    def assemble(k, s):
        return (s[0], big["adamw_w_in"][k], s[1], s[2], s[3], s[4], s[5], s[6], big["adamw_w_out"][k], s[7],
                big["adamw_w_up"][k], big["adamw_w_down"][k])

    return (loss, grad_x.reshape(nb, seq, D_MODEL), *assemble(0, g_s), *assemble(1, d_s), *assemble(2, m_s),
            *assemble(3, v_s))
```

```python
import functools

import numpy as np
import jax
import jax.numpy as jnp
from jax import lax
from jax.experimental import pallas as pl
from jax.experimental.pallas import tpu as pltpu

F32 = jnp.float32
BF16 = jnp.bfloat16

D_MODEL = 1024
HEAD_DIM = 64
LANES = 128
A_HEADS = 8
A_KV_HEADS = 2
A_GROUP = A_HEADS // A_KV_HEADS
B_HEADS = 8
WINDOW = 128
D_FF = 4096
IN_WIDTH = 2312
EPS = 1e-6
SCALE = 0.125
LOG2E = 1.4426950408889634
LN2 = 0.6931471805599453
CHUNK = 32
FOX_TK = 512
FOX_PARTS = 8
NEG = -1e30

G_QA, G_KA, G_QB, G_KB, G_VA, G_VB, G_F = 0, 8, 10, 18, 26, 28, 36
N_NORM_GROUPS = 26
N_GROUPS = 38
NP = N_GROUPS * LANES
MIXED_P = (A_HEADS + B_HEADS) * LANES

N_CHIPS = 4
IN_SHARD = IN_WIDTH // N_CHIPS
IN_SHARD_P = 608
R_ATT = IN_SHARD_P + D_MODEL // N_CHIPS

SMALL_ROWS = 24
ROW_LOSS = 22

ADAM_LR = 0.001
ADAM_B1 = 0.9
ADAM_B2 = 0.999
ADAM_EPS = 1e-08
ADAM_WD = 0.01
ADAM_STEP = 10

VMEM_LIMIT = 52 * 1024 * 1024
MESH = pl.DeviceIdType.MESH


def _pcall(body, **kw):
    return pl.pallas_call(body, **kw)


def _params(sem=None):
    return pltpu.CompilerParams(dimension_semantics=sem, vmem_limit_bytes=VMEM_LIMIT)


def _dot(a, b):
    return jnp.dot(a, b, preferred_element_type=F32)


def _dot_nt(a, b):
    return lax.dot_general(a, b, (((1,), (1,)), ((), ())), preferred_element_type=F32)


def _dot_tn(a, b):
    return lax.dot_general(a, b, (((0,), (0,)), ((), ())), preferred_element_type=F32)


def _split3(x):
    hi = x.astype(BF16)
    r1 = x - hi.astype(F32)
    mid = r1.astype(BF16)
    lo = (r1 - mid.astype(F32)).astype(BF16)
    return hi, mid, lo


def _dot_exact(mat, x):
    hi, mid, lo = _split3(x)
    return _dot(mat, lo) + _dot(mat, mid) + _dot(mat, hi)


def _const(shape):
    zeros = (0,) * len(shape)
    return pl.BlockSpec(shape, lambda *_: zeros)


def _rows(tm, n):
    return pl.BlockSpec((tm, n), lambda i: (i, 0))


def _aug_select():
    e = np.zeros((3 * LANES, 2 * B_HEADS * LANES), np.float32)
    for j in range(3):
        for h in range(B_HEADS):
            e[j * LANES + h, h * LANES + HEAD_DIM + j] = 1.0
            e[j * LANES + h, (B_HEADS + h) * LANES + HEAD_DIM + 3 + j] = -1.0
    return jnp.asarray(e, BF16)


def _dc_select():
    e = np.zeros((2 * B_HEADS * LANES, LANES), np.float32)
    for h in range(B_HEADS):
        e[h * LANES + HEAD_DIM, h] = 1.0
        e[(B_HEADS + h) * LANES + HEAD_DIM + 3, h] = -1.0
    return jnp.asarray(e, BF16)


def _tri(n, upper):
    t = np.tril(np.ones((n, n), np.float32))
    return jnp.asarray(t.T if upper else t, BF16)


def _inproj(x2, g1, w_pad_t, gain_row, b_row, seq, later_weights):
    t_all = x2.shape[0]
    tm = min(256, seq)
    n_steps = t_all // tm
    forward_step = max(n_steps - 2, 0)
    tiles_per_seq = seq // tm
    tri = _tri(tm, False)
    esel = _aug_select()
    n_later = len(later_weights)

    def body(x_ref, g_ref, w_ref, gain_ref, b_ref, tri_ref, e_ref, *rest):
        later_src, rest = rest[:n_later], rest[n_later:]
        xn_ref, pre_ref, qa_ref, ka_ref, va_ref, qb_ref, kb_ref, vb_ref, z_ref = rest[:9]
        later_out, (carry_ref, send_sems, recv_sems, local_sems) = rest[9:9 + n_later], rest[9 + n_later:]
        i = pl.program_id(0)
        gather = _Gather(later_src, later_out, send_sems, recv_sems, local_sems)

        @pl.when(i == 0)
        def _():
            gather.start()

        @pl.when(i == forward_step)
        def _():
            gather.forward()

        @pl.when(i % tiles_per_seq == 0)
        def _():
            carry_ref[...] = jnp.zeros_like(carry_ref)

        x = x_ref[...]
        r = lax.rsqrt(jnp.mean(x * x, axis=-1, keepdims=True) + EPS)
        xn = (x * r * g_ref[...]).astype(BF16)
        xn_ref[...] = xn
        proj = _dot_nt(xn, w_ref[...])
        pre_ref[...] = proj[:, :N_NORM_GROUPS * LANES].astype(BF16)
        lane = lax.broadcasted_iota(jnp.int32, (tm, LANES), 1)

        z = proj[:, G_F * LANES:(G_F + 1) * LANES] + b_ref[...]
        z_ref[...] = z
        lf = jnp.minimum(z, 0.0) - jnp.log(1.0 + jnp.exp(-jnp.abs(z)))
        lf = jnp.where(lane < B_HEADS, lf, 0.0)
        c = _dot_exact(tri_ref[...], lf) + carry_ref[...]
        carry_ref[...] += jnp.sum(lf, axis=0, keepdims=True)
        aug = _dot(jnp.concatenate(_split3(c * LOG2E), axis=1), e_ref[...])

        def hnorm(g):
            p = proj[:, g * LANES:(g + 1) * LANES]
            rr = lax.rsqrt(jnp.sum(p * p, axis=-1, keepdims=True) * (1.0 / HEAD_DIM) + EPS)
            return p * rr * gain_ref[:, g * LANES:(g + 1) * LANES]

        ones_q = jnp.where((lane >= HEAD_DIM + 3) & (lane < HEAD_DIM + 6), 1.0, 0.0)
        ones_k = jnp.where((lane >= HEAD_DIM) & (lane < HEAD_DIM + 3), 1.0, 0.0)
        for h in range(A_HEADS):
            qa_ref[:, h * LANES:(h + 1) * LANES] = (hnorm(G_QA + h) * SCALE).astype(BF16)
        for h in range(A_KV_HEADS):
            ka_ref[:, h * LANES:(h + 1) * LANES] = hnorm(G_KA + h).astype(BF16)
        for h in range(B_HEADS):
            qb_ref[:, h * LANES:(h + 1) * LANES] = (
                hnorm(G_QB + h) * (SCALE * LOG2E) + aug[:, h * LANES:(h + 1) * LANES] + ones_q).astype(BF16)
            kb_ref[:, h * LANES:(h + 1) * LANES] = (
                hnorm(G_KB + h) + aug[:, (B_HEADS + h) * LANES:(B_HEADS + h + 1) * LANES] + ones_k).astype(BF16)
        va_ref[...] = proj[:, G_VA * LANES:G_VB * LANES].astype(BF16)
        one_v = jnp.where(lane == HEAD_DIM, 1.0, 0.0)
        for h in range(B_HEADS):
            cols = slice((G_VB + h) * LANES, (G_VB + h + 1) * LANES)
            vb_ref[:, h * LANES:(h + 1) * LANES] = (proj[:, cols] + one_v).astype(BF16)

        @pl.when(i == n_steps - 1)
        def _():
            gather.finish()

    widths = [(D_MODEL, BF16), (N_NORM_GROUPS * LANES, BF16), (A_HEADS * LANES, BF16), (A_KV_HEADS * LANES, BF16),
              (A_KV_HEADS * LANES, BF16), (B_HEADS * LANES, BF16), (B_HEADS * LANES, BF16), (B_HEADS * LANES, BF16),
              (LANES, F32)]
    res = _pcall(
        body, name="inproj", grid=(n_steps,),
        in_specs=[_rows(tm, D_MODEL), _const((1, D_MODEL)), _const((NP, D_MODEL)), _const((1, NP)),
                  _const((1, LANES)), _const((tm, tm)), _const(esel.shape)] + [ANY] * n_later,
        out_specs=[_rows(tm, w) for w, _ in widths] + [ANY] * n_later,
        out_shape=[jax.ShapeDtypeStruct((t_all, w), dt) for w, dt in widths]
        + [jax.ShapeDtypeStruct((8 * w.shape[0], w.shape[1]), w.dtype) for w in later_weights],
        scratch_shapes=[pltpu.VMEM((1, LANES), F32)] + _gather_scratch(n_later),
        compiler_params=_params(("arbitrary",)),
    )(x2, g1, w_pad_t, gain_row, b_row, tri, esel, *later_weights)
    return res[:9], res[9:]


def _fox_fwd(qb, kb, vb, nb, seq):
    t_all = qb.shape[0]
    tk = min(FOX_TK, seq // FOX_PARTS)
    tq = FOX_PARTS * tk
    nq = seq // tq

    def body(q_ref, k_ref, v_ref, o_ref, lse_ref, s_ref, p_ref, m_ref, alpha_ref, acc_ref):
        qi = pl.program_id(2)
        q = q_ref[...]
        m_ref[...] = jnp.full((tq, LANES), NEG, F32)
        acc_ref[...] = jnp.zeros((tq, LANES), F32)

        def step(j, modes):
            off = pl.multiple_of(j * tk, tk)
            k = k_ref[pl.ds(off, tk), :]
            v = v_ref[pl.ds(off, tk), :]
            live = [hf for hf in range(FOX_PARTS) if modes[hf] is not None]
            for hf in live:
                s_ref[hf] = _dot_nt(q[hf * tk:(hf + 1) * tk], k)
            for hf in live:
                for r in range(0, tk, CHUNK):
                    rows = slice(r, r + CHUNK)
                    grows = slice(hf * tk + r, hf * tk + r + CHUNK)
                    tiles = []
                    for jt in range(tk // LANES):
                        sc = s_ref[hf, rows, jt * LANES:(jt + 1) * LANES]
                        if modes[hf] == "diag":
                            row = r + lax.broadcasted_iota(jnp.int32, (CHUNK, LANES), 0)
                            col = jt * LANES + lax.broadcasted_iota(jnp.int32, (CHUNK, LANES), 1)
                            sc = jnp.where(row >= col, sc, NEG)
                        tiles.append(sc)
                    m_prev = m_ref[grows, :]
                    m_cur = functools.reduce(jnp.maximum, tiles)
                    m_new = jnp.maximum(m_prev, jnp.max(m_cur, axis=-1, keepdims=True))
                    m_ref[grows, :] = m_new
                    alpha_ref[grows, :] = jnp.exp2(m_prev - m_new)
                    for jt, sc in enumerate(tiles):
                        p_ref[hf, rows, jt * LANES:(jt + 1) * LANES] = jnp.exp2(sc - m_new).astype(BF16)
                hrows = slice(hf * tk, (hf + 1) * tk)
                acc_ref[hrows, :] = alpha_ref[hrows, :] * acc_ref[hrows, :] + _dot(p_ref[hf], v)

        def past(j, carry):
            step(j, ("full",) * FOX_PARTS)
            return carry

        lax.fori_loop(0, FOX_PARTS * qi, past, 0)
        for d in range(FOX_PARTS):
            step(FOX_PARTS * qi + d, (None,) * d + ("diag",) + ("full",) * (FOX_PARTS - 1 - d))
        acc = acc_ref[...]
        lane = lax.broadcasted_iota(jnp.int32, (tq, LANES), 1)
        l = jnp.sum(jnp.where(lane == HEAD_DIM, acc, 0.0), axis=-1, keepdims=True)
        o_ref[...] = (acc / l).astype(BF16)
        lse_ref[...] = m_ref[...] + jnp.log2(l)

    qspec = pl.BlockSpec((tq, LANES), lambda b, h, i: (b * nq + i, h))
    kspec = pl.BlockSpec((seq, LANES), lambda b, h, i: (b, h))
    return _pcall(
        body, name="fox_fwd", grid=(nb, B_HEADS, nq),
        in_specs=[qspec, kspec, kspec], out_specs=[qspec, qspec],
        out_shape=[jax.ShapeDtypeStruct((t_all, B_HEADS * LANES), BF16),
                   jax.ShapeDtypeStruct((t_all, B_HEADS * LANES), F32)],
        scratch_shapes=[pltpu.VMEM((FOX_PARTS, tk, tk), F32), pltpu.VMEM((FOX_PARTS, tk, tk), BF16),
                        pltpu.VMEM((tq, LANES), F32),
                        pltpu.VMEM((tq, LANES), F32), pltpu.VMEM((tq, LANES), F32)],
        compiler_params=_params(("parallel", "parallel", "arbitrary")),
    )(qb, kb, vb)


def _swa_bias(slopes):
    row = jnp.arange(A_GROUP * WINDOW, dtype=jnp.int32)[:, None] % WINDOW
    col = jnp.arange(2 * WINDOW, dtype=jnp.int32)[None, :]
    slope_rows = jnp.repeat(slopes.reshape(A_KV_HEADS, A_GROUP), WINDOW, axis=1)[:, :, None]
    out = []
    for t_rel in (0, WINDOW):
        dist = t_rel + row - col
        valid = (dist >= 0) & (dist < WINDOW)
        out.append(jnp.where(valid[None], -slope_rows * dist.astype(F32)[None], NEG))
    return jnp.stack(out)


def _stack_heads(ref, rows):
    return jnp.concatenate([ref[rows, j * LANES:(j + 1) * LANES] for j in range(A_GROUP)], axis=0)


def _sink_rows(sink_ref, g):
    return jnp.concatenate([jnp.full((WINDOW, LANES), sink_ref[g * A_GROUP + j], F32) for j in range(A_GROUP)], axis=0)


def _rep(col):
    return jnp.broadcast_to(col, (col.shape[0], LANES))


def _swa_specs(nq, tq, seq):
    smem = pl.BlockSpec(memory_space=pltpu.SMEM)
    qspec = pl.BlockSpec((tq, A_GROUP * LANES), lambda b, g, i: (b * nq + i, g))
    kspec = pl.BlockSpec((seq, LANES), lambda b, g, i: (b, g))
    bias_first = pl.BlockSpec((None, None, A_GROUP * WINDOW, 2 * WINDOW),
                              lambda b, g, i: (jnp.minimum(i, 1), g, 0, 0))
    bias_rest = pl.BlockSpec((None, None, A_GROUP * WINDOW, 2 * WINDOW), lambda b, g, i: (1, g, 0, 0))
    return smem, qspec, kspec, bias_first, bias_rest


def _swa_fwd(qa, ka, va, sinks, bias, nb, seq):
    t_all = qa.shape[0]
    tq = min(512, seq)
    nq = seq // tq

    def body(sink_ref, q_ref, k_ref, v_ref, bias0_ref, bias_ref, o_ref, l_ref):
        qi = pl.program_id(2)
        sink = _sink_rows(sink_ref, pl.program_id(1))
        for a in range(tq // WINDOW):
            t0 = qi * tq + a * WINDOW
            start = pl.multiple_of(jnp.maximum(t0 - WINDOW, 0), WINDOW)
            rows = slice(a * WINDOW, (a + 1) * WINDOW)
            k = k_ref[pl.ds(start, 2 * WINDOW), :]
            v = v_ref[pl.ds(start, 2 * WINDOW), :]
            s = _dot_nt(_stack_heads(q_ref, rows), k) + (bias0_ref if a == 0 else bias_ref)[...]
            s0, s1 = s[:, :LANES], s[:, LANES:]
            m = jnp.maximum(_rep(jnp.max(jnp.maximum(s0, s1), axis=-1, keepdims=True)), sink)
            p0, p1 = jnp.exp(s0 - m), jnp.exp(s1 - m)
            den = _rep(jnp.sum(p0 + p1, axis=-1, keepdims=True)) + jnp.exp(sink - m)
            inv = 1.0 / den
            o = _dot(jnp.concatenate([(p0 * inv).astype(BF16), (p1 * inv).astype(BF16)], axis=1), v).astype(BF16)
            lrow = m + jnp.log(den)
            for j in range(A_GROUP):
                o_ref[rows, j * LANES:(j + 1) * LANES] = o[j * WINDOW:(j + 1) * WINDOW]
                l_ref[rows, j * LANES:(j + 1) * LANES] = lrow[j * WINDOW:(j + 1) * WINDOW]

    smem, qspec, kspec, bias_first, bias_rest = _swa_specs(nq, tq, seq)
    return _pcall(
        body, name="swa_fwd", grid=(nb, A_KV_HEADS, nq),
        in_specs=[smem, qspec, kspec, kspec, bias_first, bias_rest], out_specs=[qspec, qspec],
        out_shape=[jax.ShapeDtypeStruct((t_all, A_HEADS * LANES), BF16),
                   jax.ShapeDtypeStruct((t_all, A_HEADS * LANES), F32)],
        compiler_params=_params(("parallel", "parallel", "arbitrary")),
    )(sinks, qa, ka, va, bias, bias)


def _outproj(x2, oa, ob, wo_pad, g2):
    t_all = x2.shape[0]
    tm = min(512, t_all)
    half = A_HEADS * LANES

    def body(x_ref, oa_ref, ob_ref, w_ref, g_ref, h_ref, hn_ref):
        h = x_ref[...] + _dot(oa_ref[...], w_ref[:half, :]) + _dot(ob_ref[...], w_ref[half:, :])
        h_ref[...] = h
        r = lax.rsqrt(jnp.mean(h * h, axis=-1, keepdims=True) + EPS)
        hn_ref[...] = (h * r * g_ref[...]).astype(BF16)

    return _pcall(
        body, name="outproj", grid=(t_all // tm,),
        in_specs=[_rows(tm, D_MODEL), _rows(tm, half), _rows(tm, half), _const((MIXED_P, D_MODEL)),
                  _const((1, D_MODEL))],
        out_specs=[_rows(tm, D_MODEL), _rows(tm, D_MODEL)],
        out_shape=[jax.ShapeDtypeStruct((t_all, D_MODEL), F32), jax.ShapeDtypeStruct((t_all, D_MODEL), BF16)],
        compiler_params=_params(("parallel",)),
    )(x2, oa, ob, wo_pad, g2)


def _mlp_fwd(hn, w_up_blocks, w_down, h, tgt):
    t_all = h.shape[0]
    tm = min(256, t_all)
    nj = D_FF // D_MODEL

    def body(a_ref, wu_ref, wd_ref, h_ref, t_ref, ru_ref, dy_ref, dyb_ref, loss_ref):
        @pl.when(pl.program_id(0) == 0)
        def _():
            loss_ref[...] = jnp.zeros_like(loss_ref)

        a = a_ref[...]
        y = h_ref[...]
        for j in range(nj):
            cols = slice(j * D_MODEL, (j + 1) * D_MODEL)
            ru = jnp.maximum(_dot(a, wu_ref[j]), 0.0)
            ru_ref[:, cols] = ru.astype(BF16)
            y = y + _dot((ru * ru).astype(BF16), wd_ref[cols, :])
        err = y - t_ref[...]
        loss_ref[...] += jnp.sum(err * err)
        dy = err * (1.0 / D_MODEL)
        dy_ref[...] = dy
        dyb_ref[...] = dy.astype(BF16)

    return _pcall(
        body, name="mlp_fwd", grid=(t_all // tm,),
        in_specs=[_rows(tm, D_MODEL), _const((nj, D_MODEL, D_MODEL)), _const((D_FF, D_MODEL)), _rows(tm, D_MODEL),
                  _rows(tm, D_MODEL)],
        out_specs=[_rows(tm, D_FF), _rows(tm, D_MODEL), _rows(tm, D_MODEL), _const((8, LANES))],
        out_shape=[jax.ShapeDtypeStruct((t_all, D_FF), BF16), jax.ShapeDtypeStruct((t_all, D_MODEL), F32),
                   jax.ShapeDtypeStruct((t_all, D_MODEL), BF16), jax.ShapeDtypeStruct((8, LANES), F32)],
        compiler_params=_params(("arbitrary",)),
    )(hn, w_up_blocks, w_down, h, tgt)


def _mlp_bwd_w(dyb, w_down, ru, hn):
    t_all = dyb.shape[0]
    tm = min(512, t_all)
    nj = D_FF // D_MODEL

    def body(dy_ref, w_ref, ru_ref, hn_ref, du_ref, dw_ref):
        @pl.when(pl.program_id(1) == 0)
        def _():
            dw_ref[...] = jnp.zeros_like(dw_ref)

        dy = dy_ref[...]
        ru = ru_ref[...].astype(F32)
        du = (_dot_nt(dy, w_ref[...]) * (2.0 * ru)).astype(BF16)
        du_ref[...] = du
        dw_ref[0] += _dot_tn(hn_ref[...], du)
        dw_ref[1] += _dot_tn((ru * ru).astype(BF16), dy)

    tok = pl.BlockSpec((tm, D_MODEL), lambda j, i: (i, 0))
    blk = pl.BlockSpec((tm, D_MODEL), lambda j, i: (i, j))
    wspec = pl.BlockSpec((2, None, D_MODEL, D_MODEL), lambda j, i: (0, j, 0, 0))
    return _pcall(
        body, name="mlp_bwd_w", grid=(nj, t_all // tm),
        in_specs=[tok, pl.BlockSpec((D_MODEL, D_MODEL), lambda j, i: (j, 0)), blk, tok],
        out_specs=[blk, wspec],
        out_shape=[jax.ShapeDtypeStruct((t_all, D_FF), BF16), jax.ShapeDtypeStruct((2, nj, D_MODEL, D_MODEL), F32)],
        compiler_params=_params(("parallel", "arbitrary")),
    )(dyb, w_down, ru, hn)


def _pair_exchange_copy(g_ref, out_ref, send_sem, recv_sem):
    x, y, c = _place()
    return pltpu.make_async_remote_copy(
        src_ref=g_ref.at[1 - c], dst_ref=out_ref, send_sem=send_sem, recv_sem=recv_sem,
        device_id=(x, y, 1 - c), device_id_type=MESH)


def _mlp_dhn(du, w_up_blocks, h, dy, g2, d_w_mlp):
    t_all = h.shape[0]
    tm = min(256, t_all)
    n_steps = t_all // tm

    def body(a_ref, w_ref, h_ref, dy_ref, g_ref, dw_ref, dh_ref, dhb_ref, dg_ref, got_ref, send_sem, recv_sem):
        @pl.when(pl.program_id(0) == 0)
        def _():
            dg_ref[...] = jnp.zeros_like(dg_ref)
            _pair_exchange_copy(dw_ref, got_ref, send_sem, recv_sem).start()

        dhn = _dot_nt(a_ref[:, :D_MODEL], w_ref[0])
        for j in range(1, D_FF // D_MODEL):
            dhn = dhn + _dot_nt(a_ref[:, j * D_MODEL:(j + 1) * D_MODEL], w_ref[j])
        h = h_ref[...]
        r = lax.rsqrt(jnp.mean(h * h, axis=-1, keepdims=True) + EPS)
        hh = h * r
        dg_ref[...] += jnp.sum(dhn * hh, axis=0, keepdims=True)
        dz = dhn * g_ref[...]
        dh = dy_ref[...] + r * (dz - hh * jnp.mean(dz * hh, axis=-1, keepdims=True))
        dh_ref[...] = dh
        dhb_ref[...] = dh.astype(BF16)

        @pl.when(pl.program_id(0) == n_steps - 1)
        def _():
            _pair_exchange_copy(dw_ref, got_ref, send_sem, recv_sem).wait()

    return _pcall(
        body, name="mlp_dhn", grid=(n_steps,),
        in_specs=[_rows(tm, D_FF), _const((D_FF // D_MODEL, D_MODEL, D_MODEL)), _rows(tm, D_MODEL),
                  _rows(tm, D_MODEL), _const((1, D_MODEL)), ANY],
        out_specs=[_rows(tm, D_MODEL), _rows(tm, D_MODEL), _const((1, D_MODEL)), ANY],
        out_shape=[jax.ShapeDtypeStruct((t_all, D_MODEL), F32), jax.ShapeDtypeStruct((t_all, D_MODEL), BF16),
                   jax.ShapeDtypeStruct((1, D_MODEL), F32), jax.ShapeDtypeStruct(d_w_mlp.shape[1:], F32)],
        scratch_shapes=[pltpu.SemaphoreType.DMA, pltpu.SemaphoreType.DMA],
        compiler_params=_params(("arbitrary",)),
    )(du, w_up_blocks, h, dy, g2, d_w_mlp)


def _dmixed(dhb, wo_pad, oa, ob):
    t_all = dhb.shape[0]
    tm = min(512, t_all)
    half = A_HEADS * LANES

    def body(a_ref, w_ref, oa_ref, ob_ref, da_ref, db_ref, delta_ref, dwo_ref):
        @pl.when(pl.program_id(0) == 0)
        def _():
            dwo_ref[...] = jnp.zeros_like(dwo_ref)

        a = a_ref[...]
        d = _dot_nt(a, w_ref[...])
        da_ref[...] = d[:, :half].astype(BF16)
        db_ref[...] = d[:, half:].astype(BF16)
        for h in range(B_HEADS):
            cols = slice(h * LANES, (h + 1) * LANES)
            prod = d[:, half + h * LANES:half + (h + 1) * LANES] * ob_ref[:, cols].astype(F32)
            delta_ref[:, cols] = jnp.broadcast_to(jnp.sum(prod, axis=-1, keepdims=True), (tm, LANES))
        dwo_ref[:half, :] += _dot_tn(oa_ref[...], a)
        dwo_ref[half:, :] += _dot_tn(ob_ref[...], a)

    return _pcall(
        body, name="dmixed", grid=(t_all // tm,),
        in_specs=[_rows(tm, D_MODEL), _const((MIXED_P, D_MODEL)), _rows(tm, half), _rows(tm, half)],
        out_specs=[_rows(tm, half), _rows(tm, half), _rows(tm, half), _const((MIXED_P, D_MODEL))],
        out_shape=[jax.ShapeDtypeStruct((t_all, half), BF16), jax.ShapeDtypeStruct((t_all, half), BF16),
                   jax.ShapeDtypeStruct((t_all, half), F32), jax.ShapeDtypeStruct((MIXED_P, D_MODEL), F32)],
        compiler_params=_params(("arbitrary",)),
    )(dhb, wo_pad, oa, ob)


def _fox_bwd(qb, kb, vb, dob, lse, delta, nb, seq, pair_sums):
    t_all = qb.shape[0]
    tk = min(FOX_TK, seq // FOX_PARTS)
    tq = FOX_PARTS * tk
    nk = seq // tk

    def body(q_ref, k_ref, v_ref, do_ref, lse_ref, delta_ref, pair_ref, dq_ref, dk_ref, dv_ref, got_ref,
             s_ref, dp_ref, p_ref, ds_ref, dk_acc, dv_acc, send_sems, recv_sems):
        kj = pl.program_id(2)
        bh = pl.program_id(0) * B_HEADS + pl.program_id(1)

        @pl.when((bh == 0) & (kj == 0))
        def _():
            for cp in _chip_exchange_copies(pair_ref, got_ref, send_sems, recv_sems):
                cp.start()

        @pl.when(kj == 0)
        def _():
            dq_ref[...] = jnp.zeros_like(dq_ref)

        dk_acc[...] = jnp.zeros_like(dk_acc)
        dv_acc[...] = jnp.zeros_like(dv_acc)
        k = k_ref[...]
        v = v_ref[...]

        def block(off, r0, r1, masked):
            qrows = pl.ds(pl.multiple_of(off + r0, CHUNK), r1 - r0)
            q = q_ref[qrows, :]
            do = do_ref[qrows, :]
            s_ref[r0:r1, :] = _dot_nt(q, k)
            dp_ref[r0:r1, :] = _dot_nt(do, v)
            for r in range(r0, r1, CHUNK):
                rows = slice(r, r + CHUNK)
                chunk = pl.ds(pl.multiple_of(off + r, CHUNK), CHUNK)
                lse_c = lse_ref[chunk, :]
                delta_c = delta_ref[chunk, :]
                for jt in range(tk // LANES):
                    cols = slice(jt * LANES, (jt + 1) * LANES)
                    p = jnp.exp2(s_ref[rows, cols] - lse_c)
                    if masked:
                        row = r - r0 + lax.broadcasted_iota(jnp.int32, (CHUNK, LANES), 0)
                        col = jt * LANES + lax.broadcasted_iota(jnp.int32, (CHUNK, LANES), 1)
                        p = jnp.where(row >= col, p, 0.0)
                    p_ref[rows, cols] = p.astype(BF16)
                    ds_ref[rows, cols] = (p * (dp_ref[rows, cols] - delta_c)).astype(BF16)
            dv_acc[...] += _dot_tn(p_ref[r0:r1, :], do)
            dk_acc[...] += _dot_tn(ds_ref[r0:r1, :], q)
            dq_ref[qrows, :] += _dot(ds_ref[r0:r1, :], k)

        first = kj // FOX_PARTS
        off_first = pl.multiple_of(first * tq, tq)
        for d in range(FOX_PARTS):
            @pl.when(kj % FOX_PARTS == d)
            def _(d=d):
                block(off_first, d * tk, (d + 1) * tk, True)
                if d < FOX_PARTS - 1:
                    block(off_first, (d + 1) * tk, tq, False)

        def later(i, carry):
            block(pl.multiple_of(i * tq, tq), 0, tq, False)
            return carry

        lax.fori_loop(first + 1, seq // tq, later, 0)
        dk_ref[...] = dk_acc[...]
        dv_ref[...] = dv_acc[...]

        @pl.when((bh == nb * B_HEADS - 1) & (kj == nk - 1))
        def _():
            for cp in _chip_exchange_copies(pair_ref, got_ref, send_sems, recv_sems):
                cp.wait()

    full = pl.BlockSpec((seq, LANES), lambda b, h, j: (b, h))
    tile = pl.BlockSpec((tk, LANES), lambda b, h, j: (b * nk + j, h))
    shp = jax.ShapeDtypeStruct((t_all, B_HEADS * LANES), F32)
    return _pcall(
        body, name="fox_bwd", grid=(nb, B_HEADS, nk),
        in_specs=[full, tile, tile, full, full, full, ANY], out_specs=[full, tile, tile, ANY],
        out_shape=[shp, shp, shp, jax.ShapeDtypeStruct((3,) + pair_sums.shape[1:], pair_sums.dtype)],
        scratch_shapes=[pltpu.VMEM((tq, tk), F32), pltpu.VMEM((tq, tk), F32), pltpu.VMEM((tq, tk), BF16),
                        pltpu.VMEM((tq, tk), BF16), pltpu.VMEM((tk, LANES), F32), pltpu.VMEM((tk, LANES), F32),
                        pltpu.SemaphoreType.DMA((3,)), pltpu.SemaphoreType.DMA((3,))],
        compiler_params=_params(("arbitrary", "arbitrary", "arbitrary")),
    )(qb, kb, vb, dob, lse, delta, pair_sums)


def _swa_bwd(qa, ka, va, oa, doa, lrow, sinks, bias, nb, seq):
    t_all = qa.shape[0]
    tq = min(512, seq)
    nq = seq // tq

    def body(sink_ref, q_ref, k_ref, v_ref, bias0_ref, bias_ref, o_ref, do_ref, l_ref,
             dq_ref, dk_ref, dv_ref, dsink_ref):
        qi = pl.program_id(2)
        sink = _sink_rows(sink_ref, pl.program_id(1))

        @pl.when(qi == 0)
        def _():
            dk_ref[...] = jnp.zeros_like(dk_ref)
            dv_ref[...] = jnp.zeros_like(dv_ref)
            dsink_ref[...] = jnp.zeros_like(dsink_ref)

        for a in range(tq // WINDOW):
            t0 = qi * tq + a * WINDOW
            start = pl.multiple_of(jnp.maximum(t0 - WINDOW, 0), WINDOW)
            rows = slice(a * WINDOW, (a + 1) * WINDOW)
            win = pl.ds(start, 2 * WINDOW)
            q = _stack_heads(q_ref, rows)
            k = k_ref[win, :]
            v = v_ref[win, :]
            do = _stack_heads(do_ref, rows)
            lrow = _stack_heads(l_ref, rows)
            s = _dot_nt(q, k) + (bias0_ref if a == 0 else bias_ref)[...]
            dp = _dot_nt(do, v)
            delta = _rep(jnp.sum(do.astype(F32) * _stack_heads(o_ref, rows).astype(F32), axis=-1, keepdims=True))
            p = [jnp.exp(s[:, t * LANES:(t + 1) * LANES] - lrow) for t in range(2)]
            ds = jnp.concatenate([(p[t] * (dp[:, t * LANES:(t + 1) * LANES] - delta)).astype(BF16) for t in range(2)],
                                 axis=1)
            dq = _dot(ds, k)
            dk_ref[win, :] += _dot_tn(ds, q)
            dv_ref[win, :] += _dot_tn(jnp.concatenate([p[0].astype(BF16), p[1].astype(BF16)], axis=1), do)
            sink_term = jnp.exp(sink - lrow) * delta
            for j in range(A_GROUP):
                part = slice(j * WINDOW, (j + 1) * WINDOW)
                dq_ref[rows, j * LANES:(j + 1) * LANES] = dq[part]
                dsink_ref[j:j + 1, :] -= jnp.sum(sink_term[part], axis=0, keepdims=True)

    smem, qspec, kspec, bias_first, bias_rest = _swa_specs(nq, tq, seq)
    return _pcall(
        body, name="swa_bwd", grid=(nb, A_KV_HEADS, nq),
        in_specs=[smem, qspec, kspec, kspec, bias_first, bias_rest, qspec, qspec, qspec],
        out_specs=[qspec, kspec, kspec, pl.BlockSpec((None, 8, LANES), lambda b, g, i: (b * A_KV_HEADS + g, 0, 0))],
        out_shape=[jax.ShapeDtypeStruct((t_all, A_HEADS * LANES), F32),
                   jax.ShapeDtypeStruct((t_all, A_KV_HEADS * LANES), F32),
                   jax.ShapeDtypeStruct((t_all, A_KV_HEADS * LANES), F32),
                   jax.ShapeDtypeStruct((nb * A_KV_HEADS, 8, LANES), F32)],
        compiler_params=_params(("parallel", "parallel", "arbitrary")),
    )(sinks, qa, ka, va, bias, bias, oa, doa, lrow)


def _dproj_dx(pre, dqa, dka, dqb, dkb, dva, dvb, z, x2, dh, gain_row, w_pad_t, g1, seq):
    t_all = pre.shape[0]
    tm = min(256, seq)
    nt = t_all // tm
    tiles_per_seq = seq // tm
    triu = _tri(tm, True)
    sel = _dc_select()

    def body(pre_ref, dqa_ref, dka_ref, dqb_ref, dkb_ref, dva_ref, dvb_ref, z_ref, x_ref, dh_ref, gain_ref, triu_ref,
             sel_ref, w_ref, g_ref, dproj_ref, small_ref, dx_ref, dg_ref, carry_ref):
        i = pl.program_id(0)

        @pl.when(i == 0)
        def _():
            small_ref[...] = jnp.zeros_like(small_ref)
            dg_ref[...] = jnp.zeros_like(dg_ref)

        @pl.when(i % tiles_per_seq == 0)
        def _():
            carry_ref[...] = jnp.zeros_like(carry_ref)

        def norm_bwd(g, dhat):
            cols = slice(g * LANES, (g + 1) * LANES)
            p = pre_ref[:, cols].astype(F32)
            rr = lax.rsqrt(jnp.sum(p * p, axis=-1, keepdims=True) * (1.0 / HEAD_DIM) + EPS)
            n = p * rr
            dz = dhat * gain_ref[:, cols]
            dproj_ref[:, cols] = (rr * (dz - n * (jnp.sum(dz * n, axis=-1, keepdims=True) * (1.0 / HEAD_DIM)))
                                  ).astype(BF16)
            return jnp.sum(dhat * n, axis=0, keepdims=True)

        def group_sum(g0, d_ref, count, scale):
            acc = jnp.zeros((1, LANES), F32)
            for h in range(count):
                d = d_ref[:, h * LANES:(h + 1) * LANES]
                acc = acc + norm_bwd(g0 + h, d * scale if scale != 1.0 else d)
            return acc

        small_ref[0:1, :] += group_sum(G_QA, dqa_ref, A_HEADS, SCALE)
        small_ref[1:2, :] += group_sum(G_KA, dka_ref, A_KV_HEADS, 1.0)
        small_ref[2:3, :] += group_sum(G_QB, dqb_ref, B_HEADS, SCALE)
        small_ref[3:4, :] += group_sum(G_KB, dkb_ref, B_HEADS, LN2)
        dproj_ref[:, G_VA * LANES:G_VB * LANES] = dva_ref[...].astype(BF16)
        dproj_ref[:, G_VB * LANES:G_F * LANES] = dvb_ref[...].astype(BF16)

        dc = jnp.zeros((tm, LANES), F32)
        for piece_q, piece_k in zip(_split3(dqb_ref[...]), _split3(dkb_ref[...])):
            dc = dc + _dot(jnp.concatenate([piece_q, piece_k], axis=1), sel_ref[...])
        dlf = _dot_exact(triu_ref[...], dc) + carry_ref[...]
        carry_ref[...] += jnp.sum(dc, axis=0, keepdims=True)
        dz = dlf / (1.0 + jnp.exp(z_ref[...]))
        small_ref[4:5, :] += jnp.sum(dz, axis=0, keepdims=True)
        dproj_ref[:, G_F * LANES:(G_F + 1) * LANES] = dz.astype(BF16)
        dproj_ref[:, (G_F + 1) * LANES:] = jnp.zeros((tm, LANES), BF16)

        dxn = _dot(dproj_ref[...], w_ref[...])
        x = x_ref[...]
        r = lax.rsqrt(jnp.mean(x * x, axis=-1, keepdims=True) + EPS)
        xh = x * r
        dg_ref[...] += jnp.sum(dxn * xh, axis=0, keepdims=True)
        dxz = dxn * g_ref[...]
        dx_ref[...] = dh_ref[...] + r * (dxz - xh * jnp.mean(dxz * xh, axis=-1, keepdims=True))

    def rev(n):
        return pl.BlockSpec((tm, n), lambda i: (nt - 1 - i, 0))

    return _pcall(
        body, name="dproj_dx", grid=(nt,),
        in_specs=[rev(N_NORM_GROUPS * LANES), rev(A_HEADS * LANES), rev(A_KV_HEADS * LANES), rev(B_HEADS * LANES),
                  rev(B_HEADS * LANES), rev(A_KV_HEADS * LANES), rev(B_HEADS * LANES), rev(LANES), rev(D_MODEL),
                  rev(D_MODEL), _const((1, NP)), _const((tm, tm)), _const(sel.shape), _const((NP, D_MODEL)),
                  _const((1, D_MODEL))],
        out_specs=[rev(NP), _const((8, LANES)), rev(D_MODEL), _const((1, D_MODEL))],
        out_shape=[jax.ShapeDtypeStruct((t_all, NP), BF16), jax.ShapeDtypeStruct((8, LANES), F32),
                   jax.ShapeDtypeStruct((t_all, D_MODEL), F32), jax.ShapeDtypeStruct((1, D_MODEL), F32)],
        scratch_shapes=[pltpu.VMEM((1, LANES), F32)],
        compiler_params=_params(("arbitrary",)),
    )(pre, dqa, dka, dqb, dkb, dva, dvb, z, x2, dh, gain_row, triu, sel, w_pad_t, g1)


def _dwin(dproj, xn):
    t_all = xn.shape[0]
    tt = min(512, t_all)
    half = NP // 2

    def body(a_ref, b_ref, o_ref):
        @pl.when(pl.program_id(1) == 0)
        def _():
            o_ref[...] = jnp.zeros_like(o_ref)

        o_ref[...] += _dot_tn(a_ref[...], b_ref[...])

    return _pcall(
        body, name="dwin", grid=(2, t_all // tt),
        in_specs=[pl.BlockSpec((tt, half), lambda j, t: (t, j)), pl.BlockSpec((tt, D_MODEL), lambda j, t: (t, 0))],
        out_specs=pl.BlockSpec((half, D_MODEL), lambda j, t: (j, 0)),
        out_shape=jax.ShapeDtypeStruct((NP, D_MODEL), F32),
        compiler_params=_params(("parallel", "arbitrary")),
    )(dproj, xn)


ANY = pl.BlockSpec(memory_space=pl.ANY)


def _place():
    return lax.axis_index("x"), lax.axis_index("y"), lax.axis_index("c")


class _Gather:
    def __init__(self, srcs, outs, send_sems, recv_sems, local_sems):
        self.srcs, self.outs = srcs, outs
        self.send_sems, self.recv_sems, self.local_sems = send_sems, recv_sems, local_sems
        x, y, c = _place()
        self.c = c
        self.me, self.sibling = (x, y, c), (x, y, 1 - c)
        self.chips = [(1 - x, y), (x, 1 - y), (1 - x, 1 - y)]

    def _rows(self, a, px, py, pc):
        m = self.srcs[a].shape[0]
        return self.outs[a].at[pl.ds((4 * px + 2 * py + pc) * m, m), :]

    def _copy(self, a, k, block, to, from_src=False):
        return pltpu.make_async_remote_copy(
            src_ref=self.srcs[a] if from_src else self._rows(a, *block), dst_ref=self._rows(a, *block),
            send_sem=self.send_sems.at[k, a], recv_sem=self.recv_sems.at[k, a], device_id=to, device_id_type=MESH)

    def _own(self, a):
        return pltpu.make_async_copy(self.srcs[a], self._rows(a, *self.me), self.local_sems.at[a])

    def start(self):
        for a in range(len(self.srcs)):
            self._own(a).start()
            self._copy(a, 0, self.me, self.sibling, from_src=True).start()
            for j, chip in enumerate(self.chips):
                self._copy(a, 1 + j, self.me, (*chip, self.c), from_src=True).start()

    def forward(self):
        for a in range(len(self.srcs)):
            for j, chip in enumerate(self.chips):
                self._copy(a, 1 + j, (*chip, self.c), self.me).wait_recv()
                self._copy(a, 4 + j, (*chip, self.c), self.sibling).start()

    def finish(self):
        for a in range(len(self.srcs)):
            self._copy(a, 0, self.sibling, self.me).wait_recv()
            for j, chip in enumerate(self.chips):
                self._copy(a, 4 + j, (*chip, 1 - self.c), self.me).wait_recv()
            self._copy(a, 0, self.me, self.sibling, from_src=True).wait_send()
            for j, chip in enumerate(self.chips):
                self._copy(a, 1 + j, self.me, (*chip, self.c), from_src=True).wait_send()
                self._copy(a, 4 + j, (*chip, self.c), self.sibling).wait_send()
            self._own(a).wait()


def _gather_scratch(n_arrays):
    return [pltpu.SemaphoreType.DMA((7, n_arrays)), pltpu.SemaphoreType.DMA((7, n_arrays)),
            pltpu.SemaphoreType.DMA((n_arrays,))]


def _allgather_halves(mine):
    m_per, n = mine.shape

    def body(x_ref, out_ref, send_sems, recv_sems, local_sems):
        gather = _Gather((x_ref,), (out_ref,), send_sems, recv_sems, local_sems)
        gather.start()
        gather.forward()
        gather.finish()

    return _pcall(
        body, name="allgather_w_in",
        out_shape=jax.ShapeDtypeStruct((8 * m_per, n), mine.dtype),
        in_specs=[ANY], out_specs=ANY, scratch_shapes=_gather_scratch(1),
    )(mine)


def _rs_pair_exchange(g, name):
    def body(g_ref, out_ref, send_sem, recv_sem):
        x, y, c = _place()
        cp = pltpu.make_async_remote_copy(
            src_ref=g_ref.at[1 - c], dst_ref=out_ref, send_sem=send_sem, recv_sem=recv_sem,
            device_id=(x, y, 1 - c), device_id_type=MESH)
        cp.start()
        cp.wait()

    return _pcall(
        body, name=name, out_shape=jax.ShapeDtypeStruct(g.shape[1:], F32),
        in_specs=[ANY], out_specs=ANY, scratch_shapes=[pltpu.SemaphoreType.DMA, pltpu.SemaphoreType.DMA],
    )(g)


def _rs_pair_add(g, got, c_idx, name):
    rows = g.shape[2]

    def body(c_ref, a_ref, b_ref, o_ref, ob_ref):
        pair = a_ref[...] + b_ref[...]
        o_ref[...] = pair
        ob_ref[...] = pair.astype(BF16)

    blk = pl.BlockSpec((None, rows, D_MODEL), lambda s, c_ref: (s, 0, 0))
    return _pcall(
        body, name=name,
        grid_spec=pltpu.PrefetchScalarGridSpec(
            num_scalar_prefetch=1, grid=(N_CHIPS,),
            in_specs=[pl.BlockSpec((None, None, rows, D_MODEL), lambda s, c_ref: (c_ref[0], s, 0, 0)), blk],
            out_specs=[blk, blk]),
        out_shape=[jax.ShapeDtypeStruct((N_CHIPS, rows, D_MODEL), F32),
                   jax.ShapeDtypeStruct((N_CHIPS, rows, D_MODEL), BF16)],
        compiler_params=_params(("parallel",)),
    )(c_idx, g, got)


def _chip_exchange_copies(p_ref, out_ref, send_sems, recv_sems):
    x, y, c = _place()
    chips = [(1 - x, y), (x, 1 - y), (1 - x, 1 - y)]
    return [pltpu.make_async_remote_copy(
        src_ref=p_ref.at[2 * cx + cy], dst_ref=out_ref.at[j], send_sem=send_sems.at[j], recv_sem=recv_sems.at[j],
        device_id=(cx, cy, c), device_id_type=MESH) for j, (cx, cy) in enumerate(chips)]


def _rs_chip_exchange(p4, name):
    def body(p_ref, out_ref, send_sems, recv_sems):
        cps = _chip_exchange_copies(p_ref, out_ref, send_sems, recv_sems)
        for cp in cps:
            cp.start()
        for cp in cps:
            cp.wait()

    return _pcall(
        body, name=name, out_shape=jax.ShapeDtypeStruct((3,) + p4.shape[1:], p4.dtype),
        in_specs=[ANY], out_specs=ANY,
        scratch_shapes=[pltpu.SemaphoreType.DMA((3,)), pltpu.SemaphoreType.DMA((3,))],
    )(p4)


def _rs_chip_add(p4, got, sc_idx, name):
    rows = p4.shape[1]
    tr = next(rows // n for n in (8, 7, 6, 5, 4, 3, 2, 1) if rows % n == 0 and (rows // n) % 16 == 0)

    def body(sc_ref, a_ref, b_ref, o_ref):
        o_ref[...] = ((a_ref[...] + b_ref[0].astype(F32)) + b_ref[1].astype(F32)) + b_ref[2].astype(F32)

    return _pcall(
        body, name=name,
        grid_spec=pltpu.PrefetchScalarGridSpec(
            num_scalar_prefetch=1, grid=(rows // tr,),
            in_specs=[pl.BlockSpec((None, tr, D_MODEL), lambda i, sc_ref: (sc_ref[0], i, 0)),
                      pl.BlockSpec((3, tr, D_MODEL), lambda i, sc_ref: (0, i, 0))],
            out_specs=pl.BlockSpec((None, tr, D_MODEL), lambda i, sc_ref: (sc_ref[1], i, 0))),
        out_shape=jax.ShapeDtypeStruct((2, rows, D_MODEL), F32),
        compiler_params=_params(("parallel",)),
    )(sc_idx, p4, got)


def _rs_pair_share(halves, name):
    def body(r_ref, out_ref, send_sem, recv_sem):
        x, y, c = _place()
        cp = pltpu.make_async_remote_copy(
            src_ref=r_ref.at[c], dst_ref=out_ref.at[c], send_sem=send_sem, recv_sem=recv_sem,
            device_id=(x, y, 1 - c), device_id_type=MESH)
        cp.start()
        cp.wait()

    return _pcall(
        body, name=name, out_shape=jax.ShapeDtypeStruct(halves.shape, F32),
        in_specs=[ANY], out_specs=ANY, input_output_aliases={0: 0},
        scratch_shapes=[pltpu.SemaphoreType.DMA, pltpu.SemaphoreType.DMA],
    )(halves)


def _adam(w, g, m, v):
    m2 = ADAM_B1 * m + (1.0 - ADAM_B1) * g
    v2 = ADAM_B2 * v + (1.0 - ADAM_B2) * (g * g)
    m_hat = m2 / (1.0 - ADAM_B1 ** ADAM_STEP)
    v_hat = v2 / (1.0 - ADAM_B2 ** ADAM_STEP)
    return -ADAM_LR * (m_hat / (jnp.sqrt(v_hat) + ADAM_EPS) + ADAM_WD * w), m2, v2


def _small_allreduce_adamw(part, w, m, v):
    pieces = ((0, 8, LANES), (20, 1, B_HEADS), (16, 1, HEAD_DIM), (17, 1, HEAD_DIM), (21, 1, A_HEADS),
              (18, 1, HEAD_DIM), (19, 1, HEAD_DIM), (8, 8, LANES))

    def body(p_ref, w_ref, m_ref, v_ref, *rest):
        outs, (loss_ref, buf, stage, send_sems, recv_sems) = rest[:4 * len(pieces)], rest[4 * len(pieces):]
        x, y, c = _place()
        me = 4 * x + 2 * y + c
        cps = []
        for k in range(1, 8):
            peer = (1 - x if k & 4 else x, 1 - y if k & 2 else y, 1 - c if k & 1 else c)
            cps.append(pltpu.make_async_remote_copy(
                src_ref=p_ref, dst_ref=buf.at[me], send_sem=send_sems.at[k - 1], recv_sem=recv_sems.at[k - 1],
                device_id=peer, device_id_type=MESH))
        for cp in cps:
            cp.start()
        buf[me] = p_ref[...]
        for cp in cps:
            cp.wait()
        g = buf[0]
        for k in range(1, 8):
            g = g + buf[k]
        for kind, packed in enumerate((g,) + _adam(w_ref[...], g, m_ref[...], v_ref[...])):
            stage[...] = packed
            if kind == 0:
                loss_ref[...] = stage[ROW_LOSS:ROW_LOSS + 1, :]
            for i, (row, rows, lanes) in enumerate(pieces):
                outs[kind * len(pieces) + i][...] = stage[row:row + rows, 0:lanes]

    vm = pl.BlockSpec(memory_space=pltpu.VMEM)
    shapes = [jax.ShapeDtypeStruct((rows, lanes), F32) for _ in range(4) for _, rows, lanes in pieces]
    shapes.append(jax.ShapeDtypeStruct((1, LANES), F32))
    res = _pcall(
        body, name="small_allreduce_adamw",
        out_shape=shapes, in_specs=[vm, vm, vm, vm], out_specs=[vm] * len(shapes),
        scratch_shapes=[pltpu.VMEM((8, SMALL_ROWS, LANES), F32), pltpu.VMEM((SMALL_ROWS, LANES), F32),
                        pltpu.SemaphoreType.DMA((7,)), pltpu.SemaphoreType.DMA((7,))],
    )(part, w, m, v)
    flat = [r.reshape(r.size) for r in res[:-1]]
    n = len(pieces)
    return [flat[k * n:(k + 1) * n] for k in range(4)], res[-1][0, 0]


def _adamw(w, g, m, v, name):
    rows, cols = w.shape
    tr = min(256, rows)

    def body(w_ref, g_ref, m_ref, v_ref, d_ref, m2_ref, v2_ref):
        d_ref[...], m2_ref[...], v2_ref[...] = _adam(w_ref[...], g_ref[...], m_ref[...], v_ref[...])

    spec = _rows(tr, cols)
    shp = jax.ShapeDtypeStruct((rows, cols), F32)
    return _pcall(
        body, name=name, grid=(rows // tr,), in_specs=[spec] * 4, out_specs=[spec] * 3, out_shape=[shp] * 3,
        compiler_params=_params(("parallel",)),
    )(w, g, m, v)


def _pad_lanes(v):
    return jnp.pad(v, (0, LANES - v.shape[0]))


def _pad_head_rows(w_t, heads):
    n = w_t.shape[1]
    return jnp.pad(w_t.reshape(heads, HEAD_DIM, n), ((0, 0), (0, LANES - HEAD_DIM), (0, 0))).reshape(heads * LANES, n)


def _unpad_head_rows(w_t, heads):
    n = w_t.shape[1]
    return w_t.reshape(heads, LANES, n)[:, :HEAD_DIM].reshape(heads * HEAD_DIM, n)


def _in_rows_pad(w_in_t):
    qa, ka, va, qb, kb, vb, f = jnp.split(w_in_t, [512, 640, 768, 1280, 1792, 2304], axis=0)
    f = jnp.pad(f, ((0, 2 * LANES - B_HEADS), (0, 0)))
    return jnp.concatenate([_pad_head_rows(qa, 8), _pad_head_rows(ka, 2), _pad_head_rows(qb, 8),
                            _pad_head_rows(kb, 8), _pad_head_rows(va, 2), _pad_head_rows(vb, 8), f], axis=0)


def _in_rows_unpad(d):
    qa = _unpad_head_rows(d[G_QA * LANES:G_KA * LANES], 8)
    ka = _unpad_head_rows(d[G_KA * LANES:G_QB * LANES], 2)
    qb = _unpad_head_rows(d[G_QB * LANES:G_KB * LANES], 8)
    kb = _unpad_head_rows(d[G_KB * LANES:G_VA * LANES], 8)
    va = _unpad_head_rows(d[G_VA * LANES:G_VB * LANES], 2)
    vb = _unpad_head_rows(d[G_VB * LANES:G_F * LANES], 8)
    f = d[G_F * LANES:G_F * LANES + B_HEADS]
    return jnp.concatenate([qa, ka, va, qb, kb, vb, f], axis=0)


def _pack_small(g1, bf, qa, ka, sk, qb, kb, g2, loss_row):
    rows = [g1.reshape(8, LANES), g2.reshape(8, LANES)]
    rows += [_pad_lanes(t)[None] for t in (qa, ka, qb, kb, bf, sk)]
    rows += [loss_row, jnp.zeros((1, LANES), F32)]
    return jnp.concatenate(rows, axis=0)


def kernel(x, attn_norm_g, w_in, b_forget, q_norm_a, k_norm_a, sink_logits, q_norm_b, k_norm_b, w_out, mlp_norm_g, w_up, w_down, loss_target, m_attn_norm_g, m_w_in, m_b_forget, m_q_norm_a, m_k_norm_a, m_sink_logits, m_q_norm_b, m_k_norm_b, m_w_out, m_mlp_norm_g, m_w_up, m_w_down, v_attn_norm_g, v_w_in, v_b_forget, v_q_norm_a, v_k_norm_a, v_sink_logits, v_q_norm_b, v_k_norm_b, v_w_out, v_mlp_norm_g, v_w_up, v_w_down):
    nb, seq, _ = x.shape
    t_all = nb * seq
    c_idx = lax.axis_index("c")
    s_idx = 2 * lax.axis_index("x") + lax.axis_index("y")

    def my_half(a):
        halves = a.astype(BF16).reshape(2, a.shape[0] // 2, a.shape[1])
        return lax.dynamic_slice_in_dim(halves, c_idx, 1, axis=0)[0]

    w_in_shard_t = jnp.pad(w_in.T, ((0, IN_SHARD_P - IN_SHARD), (0, 0)))
    gathered_in = _allgather_halves(my_half(w_in_shard_t)).reshape(N_CHIPS, IN_SHARD_P, D_MODEL)
    w_pad_t = _in_rows_pad(gathered_in[:, :IN_SHARD].reshape(IN_WIDTH, D_MODEL))

    ones = jnp.ones((LANES,), F32)
    gain_row = jnp.concatenate(
        [jnp.tile(_pad_lanes(q_norm_a), 8), jnp.tile(_pad_lanes(k_norm_a), 2), jnp.tile(_pad_lanes(q_norm_b), 8),
         jnp.tile(_pad_lanes(k_norm_b), 8), jnp.tile(ones, N_GROUPS - N_NORM_GROUPS)])[None]
    b_row = _pad_lanes(b_forget)[None]
    g1 = attn_norm_g[None]
    g2 = mlp_norm_g[None]
    slopes = jnp.exp2(-(8.0 / A_HEADS) * (jnp.arange(A_HEADS, dtype=F32) + 1.0))

    x2 = x.reshape(t_all, D_MODEL)
    tgt = loss_target.reshape(t_all, D_MODEL)

    (xn, pre, qa, ka, va, qb, kb, vb, z), (w_out_g, w_up_g, w_down_f) = _inproj(
        x2, g1, w_pad_t, gain_row, b_row, seq, [my_half(w_out), my_half(w_up), my_half(w_down)])
    wo_pad = _pad_head_rows(w_out_g, A_HEADS + B_HEADS)
    w_up_blocks = w_up_g.reshape(N_CHIPS, D_MODEL, D_MODEL)
    swa_bias = _swa_bias(slopes)
    oa, la = _swa_fwd(qa, ka, va, sink_logits, swa_bias, nb, seq)
    ob, lse = _fox_fwd(qb, kb, vb, nb, seq)
    h, hn = _outproj(x2, oa, ob, wo_pad, g2)
    ru, dy, dyb, loss_acc = _mlp_fwd(hn, w_up_blocks, w_down_f, h, tgt)

    du, d_w_mlp = _mlp_bwd_w(dyb, w_down_f, ru, hn)
    c_arg = c_idx.reshape(1).astype(jnp.int32)
    sc_arg = jnp.stack([s_idx, c_idx]).astype(jnp.int32)
    dh, dhb, d_g2, sibling_w_mlp = _mlp_dhn(du, w_up_blocks, h, dy, g2, d_w_mlp)
    pair_m, pair_m_bf = _rs_pair_add(d_w_mlp, sibling_w_mlp, c_arg, "rs_pair_add_mlp")
    doa, dob, delta_b, d_wo = _dmixed(dhb, wo_pad, oa, ob)
    dqb, dkb, dvb, got_m = _fox_bwd(qb, kb, vb, dob, lse, delta_b, nb, seq, pair_m_bf)
    red_m = _rs_pair_share(_rs_chip_add(pair_m, got_m, sc_arg, "rs_chip_add_mlp"), "rs_pair_share_mlp")
    g_w_up, g_w_down = red_m[0], red_m[1]
    dqa, dka, dva, dsink = _swa_bwd(qa, ka, va, oa, doa, la, sink_logits, swa_bias, nb, seq)
    dproj, small, grad_x, d_g1 = _dproj_dx(pre, dqa, dka, dqb, dkb, dva, dvb, z, x2, dh, gain_row, w_pad_t, g1, seq)
    d_w_in_t = _dwin(dproj, xn)

    d_w_out = _unpad_head_rows(d_wo, A_HEADS + B_HEADS)
    g_att = jnp.concatenate([
        jnp.pad(_in_rows_unpad(d_w_in_t).reshape(N_CHIPS, IN_SHARD, D_MODEL),
                ((0, 0), (0, IN_SHARD_P - IN_SHARD), (0, 0))),
        d_w_out.reshape(N_CHIPS, D_MODEL // N_CHIPS, D_MODEL)], axis=1)
    g_att = jnp.stack([g_att[:, :R_ATT // 2], g_att[:, R_ATT // 2:]])
    pair_a, pair_a_bf = _rs_pair_add(g_att, _rs_pair_exchange(g_att, "rs_pair_exchange_att"), c_arg, "rs_pair_add_att")
    got_a = _rs_chip_exchange(pair_a_bf, "rs_chip_exchange_att")
    red_a = _rs_pair_share(_rs_chip_add(pair_a, got_a, sc_arg, "rs_chip_add_att"), "rs_pair_share_att")
    red_a = red_a.reshape(R_ATT, D_MODEL)
    g_w_in = red_a[:IN_SHARD].T
    g_w_out = red_a[IN_SHARD_P:]

    loss_row = loss_acc[0:1] * (0.5 / D_MODEL)
    d_sink = dsink[:, :A_GROUP, 0].reshape(nb, A_HEADS).sum(axis=0)
    part = _pack_small(d_g1[0], small[4, :B_HEADS], small[0, :HEAD_DIM], small[1, :HEAD_DIM], d_sink,
                       small[2, :HEAD_DIM], small[3, :HEAD_DIM], d_g2[0], loss_row)
    zero_row = jnp.zeros((1, LANES), F32)
    smalls = lambda t: _pack_small(*t, zero_row)
    w_small = smalls((attn_norm_g, b_forget, q_norm_a, k_norm_a, sink_logits, q_norm_b, k_norm_b, mlp_norm_g))
    m_small = smalls((m_attn_norm_g, m_b_forget, m_q_norm_a, m_k_norm_a, m_sink_logits, m_q_norm_b, m_k_norm_b,
                      m_mlp_norm_g))
    v_small = smalls((v_attn_norm_g, v_b_forget, v_q_norm_a, v_k_norm_a, v_sink_logits, v_q_norm_b, v_k_norm_b,
                      v_mlp_norm_g))
    (g_s, d_s, m_s, v_s), loss = _small_allreduce_adamw(part, w_small, m_small, v_small)

    big = {}
    for name, w, g, m, v in (("adamw_w_in", w_in, g_w_in, m_w_in, v_w_in),
                             ("adamw_w_out", w_out, g_w_out, m_w_out, v_w_out),
                             ("adamw_w_up", w_up, g_w_up, m_w_up, v_w_up),
                             ("adamw_w_down", w_down, g_w_down, m_w_down, v_w_down)):
        big[name] = (g,) + tuple(_adamw(w, g, m, v, name))

    def assemble(k, s):
        return (s[0], big["adamw_w_in"][k], s[1], s[2], s[3], s[4], s[5], s[6], big["adamw_w_out"][k], s[7],
                big["adamw_w_up"][k], big["adamw_w_down"][k])

    return (loss, grad_x.reshape(nb, seq, D_MODEL), *assemble(0, g_s), *assemble(1, d_s), *assemble(2, m_s),
            *assemble(3, v_s))
```

```python
import functools

import numpy as np
import jax
import jax.numpy as jnp
from jax import lax
from jax.experimental import pallas as pl
from jax.experimental.pallas import tpu as pltpu

F32 = jnp.float32
BF16 = jnp.bfloat16

D_MODEL = 1024
HEAD_DIM = 64
LANES = 128
A_HEADS = 8
A_KV_HEADS = 2
A_GROUP = A_HEADS // A_KV_HEADS
B_HEADS = 8
WINDOW = 128
D_FF = 4096
IN_WIDTH = 2312
EPS = 1e-6
SCALE = 0.125
LOG2E = 1.4426950408889634
LN2 = 0.6931471805599453
CHUNK = 32
FOX_TK = 512
FOX_PARTS = 8
FOX_PARTS_BWD = 4
NEG = -1e30

G_QA, G_KA, G_QB, G_KB, G_VA, G_VB, G_F = 0, 8, 10, 18, 26, 28, 36
N_NORM_GROUPS = 26
N_GROUPS = 38
NP = N_GROUPS * LANES
MIXED_P = (A_HEADS + B_HEADS) * LANES

N_CHIPS = 4
IN_SHARD = IN_WIDTH // N_CHIPS
IN_SHARD_P = 608
R_ATT = IN_SHARD_P + D_MODEL // N_CHIPS

SMALL_ROWS = 24
ROW_LOSS = 22

ADAM_LR = 0.001
ADAM_B1 = 0.9
ADAM_B2 = 0.999
ADAM_EPS = 1e-08
ADAM_WD = 0.01
ADAM_STEP = 10

VMEM_LIMIT = 52 * 1024 * 1024
MESH = pl.DeviceIdType.MESH


def _pcall(body, **kw):
    return pl.pallas_call(body, **kw)


def _params(sem=None):
    return pltpu.CompilerParams(dimension_semantics=sem, vmem_limit_bytes=VMEM_LIMIT)


def _dot(a, b):
    return jnp.dot(a, b, preferred_element_type=F32)


def _dot_nt(a, b):
    return lax.dot_general(a, b, (((1,), (1,)), ((), ())), preferred_element_type=F32)


def _dot_tn(a, b):
    return lax.dot_general(a, b, (((0,), (0,)), ((), ())), preferred_element_type=F32)


def _split3(x):
    hi = x.astype(BF16)
    r1 = x - hi.astype(F32)
    mid = r1.astype(BF16)
    lo = (r1 - mid.astype(F32)).astype(BF16)
    return hi, mid, lo


def _dot_exact(mat, x):
    hi, mid, lo = _split3(x)
    return _dot(mat, lo) + _dot(mat, mid) + _dot(mat, hi)


def _const(shape):
    zeros = (0,) * len(shape)
    return pl.BlockSpec(shape, lambda *_: zeros)


def _rows(tm, n):
    return pl.BlockSpec((tm, n), lambda i: (i, 0))


def _aug_select():
    e = np.zeros((3 * LANES, 2 * B_HEADS * LANES), np.float32)
    for j in range(3):
        for h in range(B_HEADS):
            e[j * LANES + h, h * LANES + HEAD_DIM + j] = 1.0
            e[j * LANES + h, (B_HEADS + h) * LANES + HEAD_DIM + 3 + j] = -1.0
    return jnp.asarray(e, BF16)


def _dc_select():
    e = np.zeros((2 * B_HEADS * LANES, LANES), np.float32)
    for h in range(B_HEADS):
        e[h * LANES + HEAD_DIM, h] = 1.0
        e[(B_HEADS + h) * LANES + HEAD_DIM + 3, h] = -1.0
    return jnp.asarray(e, BF16)


def _tri(n, upper):
    t = np.tril(np.ones((n, n), np.float32))
    return jnp.asarray(t.T if upper else t, BF16)


def _inproj(x2, g1, w_pad_t, gain_row, b_row, seq, later_weights):
    t_all = x2.shape[0]
    tm = min(256, seq)
    n_steps = t_all // tm
    forward_step = max(n_steps - 2, 0)
    tiles_per_seq = seq // tm
    tri = _tri(tm, False)
    esel = _aug_select()
    n_later = len(later_weights)

    def body(x_ref, g_ref, w_ref, gain_ref, b_ref, tri_ref, e_ref, *rest):
        later_src, rest = rest[:n_later], rest[n_later:]
        xn_ref, pre_ref, qa_ref, ka_ref, va_ref, qb_ref, kb_ref, vb_ref, z_ref = rest[:9]
        later_out, (carry_ref, send_sems, recv_sems, local_sems) = rest[9:9 + n_later], rest[9 + n_later:]
        i = pl.program_id(0)
        gather = _Gather(later_src, later_out, send_sems, recv_sems, local_sems)

        @pl.when(i == 0)
        def _():
            gather.start()

        @pl.when(i == forward_step)
        def _():
            gather.forward()

        @pl.when(i % tiles_per_seq == 0)
        def _():
            carry_ref[...] = jnp.zeros_like(carry_ref)

        x = x_ref[...]
        r = lax.rsqrt(jnp.mean(x * x, axis=-1, keepdims=True) + EPS)
        xn = (x * r * g_ref[...]).astype(BF16)
        xn_ref[...] = xn
        proj = _dot_nt(xn, w_ref[...])
        pre_ref[...] = proj[:, :N_NORM_GROUPS * LANES].astype(BF16)
        lane = lax.broadcasted_iota(jnp.int32, (tm, LANES), 1)

        z = proj[:, G_F * LANES:(G_F + 1) * LANES] + b_ref[...]
        z_ref[...] = z
        lf = jnp.minimum(z, 0.0) - jnp.log(1.0 + jnp.exp(-jnp.abs(z)))
        lf = jnp.where(lane < B_HEADS, lf, 0.0)
        c = _dot_exact(tri_ref[...], lf) + carry_ref[...]
        carry_ref[...] += jnp.sum(lf, axis=0, keepdims=True)
        aug = _dot(jnp.concatenate(_split3(c * LOG2E), axis=1), e_ref[...])

        def hnorm(g):
            p = proj[:, g * LANES:(g + 1) * LANES]
            rr = lax.rsqrt(jnp.sum(p * p, axis=-1, keepdims=True) * (1.0 / HEAD_DIM) + EPS)
            return p * rr * gain_ref[:, g * LANES:(g + 1) * LANES]

        ones_q = jnp.where((lane >= HEAD_DIM + 3) & (lane < HEAD_DIM + 6), 1.0, 0.0)
        ones_k = jnp.where((lane >= HEAD_DIM) & (lane < HEAD_DIM + 3), 1.0, 0.0)
        for h in range(A_HEADS):
            qa_ref[:, h * LANES:(h + 1) * LANES] = (hnorm(G_QA + h) * SCALE).astype(BF16)
        for h in range(A_KV_HEADS):
            ka_ref[:, h * LANES:(h + 1) * LANES] = hnorm(G_KA + h).astype(BF16)
        for h in range(B_HEADS):
            qb_ref[:, h * LANES:(h + 1) * LANES] = (
                hnorm(G_QB + h) * (SCALE * LOG2E) + aug[:, h * LANES:(h + 1) * LANES] + ones_q).astype(BF16)
            kb_ref[:, h * LANES:(h + 1) * LANES] = (
                hnorm(G_KB + h) + aug[:, (B_HEADS + h) * LANES:(B_HEADS + h + 1) * LANES] + ones_k).astype(BF16)
        va_ref[...] = proj[:, G_VA * LANES:G_VB * LANES].astype(BF16)
        one_v = jnp.where(lane == HEAD_DIM, 1.0, 0.0)
        for h in range(B_HEADS):
            cols = slice((G_VB + h) * LANES, (G_VB + h + 1) * LANES)
            vb_ref[:, h * LANES:(h + 1) * LANES] = (proj[:, cols] + one_v).astype(BF16)

        @pl.when(i == n_steps - 1)
        def _():
            gather.finish()

    widths = [(D_MODEL, BF16), (N_NORM_GROUPS * LANES, BF16), (A_HEADS * LANES, BF16), (A_KV_HEADS * LANES, BF16),
              (A_KV_HEADS * LANES, BF16), (B_HEADS * LANES, BF16), (B_HEADS * LANES, BF16), (B_HEADS * LANES, BF16),
              (LANES, F32)]
    res = _pcall(
        body, name="inproj", grid=(n_steps,),
        in_specs=[_rows(tm, D_MODEL), _const((1, D_MODEL)), _const((NP, D_MODEL)), _const((1, NP)),
                  _const((1, LANES)), _const((tm, tm)), _const(esel.shape)] + [ANY] * n_later,
        out_specs=[_rows(tm, w) for w, _ in widths] + [ANY] * n_later,
        out_shape=[jax.ShapeDtypeStruct((t_all, w), dt) for w, dt in widths]
        + [jax.ShapeDtypeStruct((8 * w.shape[0], w.shape[1]), w.dtype) for w in later_weights],
        scratch_shapes=[pltpu.VMEM((1, LANES), F32)] + _gather_scratch(n_later),
        compiler_params=_params(("arbitrary",)),
    )(x2, g1, w_pad_t, gain_row, b_row, tri, esel, *later_weights)
    return res[:9], res[9:]


def _fox_fwd(qb, kb, vb, nb, seq):
    t_all = qb.shape[0]
    tk = min(FOX_TK, seq // FOX_PARTS)
    tq = FOX_PARTS * tk
    nq = seq // tq

    def body(q_ref, k_ref, v_ref, o_ref, lse_ref, s_ref, p_ref, m_ref, alpha_ref, acc_ref):
        qi = pl.program_id(2)
        q = q_ref[...]
        m_ref[...] = jnp.full((tq, LANES), NEG, F32)
        acc_ref[...] = jnp.zeros((tq, LANES), F32)

        def step(j, modes):
            off = pl.multiple_of(j * tk, tk)
            k = k_ref[pl.ds(off, tk), :]
            v = v_ref[pl.ds(off, tk), :]
            live = [hf for hf in range(FOX_PARTS) if modes[hf] is not None]
            for hf in live:
                s_ref[hf] = _dot_nt(q[hf * tk:(hf + 1) * tk], k)
            for hf in live:
                for r in range(0, tk, CHUNK):
                    rows = slice(r, r + CHUNK)
                    grows = slice(hf * tk + r, hf * tk + r + CHUNK)
                    tiles = []
                    for jt in range(tk // LANES):
                        sc = s_ref[hf, rows, jt * LANES:(jt + 1) * LANES]
                        if modes[hf] == "diag":
                            row = r + lax.broadcasted_iota(jnp.int32, (CHUNK, LANES), 0)
                            col = jt * LANES + lax.broadcasted_iota(jnp.int32, (CHUNK, LANES), 1)
                            sc = jnp.where(row >= col, sc, NEG)
                        tiles.append(sc)
                    m_prev = m_ref[grows, :]
                    m_cur = functools.reduce(jnp.maximum, tiles)
                    m_new = jnp.maximum(m_prev, jnp.max(m_cur, axis=-1, keepdims=True))
                    m_ref[grows, :] = m_new
                    alpha_ref[grows, :] = jnp.exp2(m_prev - m_new)
                    for jt, sc in enumerate(tiles):
                        p_ref[hf, rows, jt * LANES:(jt + 1) * LANES] = jnp.exp2(sc - m_new).astype(BF16)
                hrows = slice(hf * tk, (hf + 1) * tk)
                acc_ref[hrows, :] = alpha_ref[hrows, :] * acc_ref[hrows, :] + _dot(p_ref[hf], v)

        def past(j, carry):
            step(j, ("full",) * FOX_PARTS)
            return carry

        lax.fori_loop(0, FOX_PARTS * qi, past, 0)
        for d in range(FOX_PARTS):
            step(FOX_PARTS * qi + d, (None,) * d + ("diag",) + ("full",) * (FOX_PARTS - 1 - d))
        acc = acc_ref[...]
        lane = lax.broadcasted_iota(jnp.int32, (tq, LANES), 1)
        l = jnp.sum(jnp.where(lane == HEAD_DIM, acc, 0.0), axis=-1, keepdims=True)
        o_ref[...] = (acc / l).astype(BF16)
        lse_ref[...] = m_ref[...] + jnp.log2(l)

    qspec = pl.BlockSpec((tq, LANES), lambda b, h, i: (b * nq + i, h))
    kspec = pl.BlockSpec((seq, LANES), lambda b, h, i: (b, h))
    return _pcall(
        body, name="fox_fwd", grid=(nb, B_HEADS, nq),
        in_specs=[qspec, kspec, kspec], out_specs=[qspec, qspec],
        out_shape=[jax.ShapeDtypeStruct((t_all, B_HEADS * LANES), BF16),
                   jax.ShapeDtypeStruct((t_all, B_HEADS * LANES), F32)],
        scratch_shapes=[pltpu.VMEM((FOX_PARTS, tk, tk), F32), pltpu.VMEM((FOX_PARTS, tk, tk), BF16),
                        pltpu.VMEM((tq, LANES), F32),
                        pltpu.VMEM((tq, LANES), F32), pltpu.VMEM((tq, LANES), F32)],
        compiler_params=_params(("parallel", "parallel", "arbitrary")),
    )(qb, kb, vb)


def _swa_bias(slopes):
    row = jnp.arange(A_GROUP * WINDOW, dtype=jnp.int32)[:, None] % WINDOW
    col = jnp.arange(2 * WINDOW, dtype=jnp.int32)[None, :]
    slope_rows = jnp.repeat(slopes.reshape(A_KV_HEADS, A_GROUP), WINDOW, axis=1)[:, :, None]
    out = []
    for t_rel in (0, WINDOW):
        dist = t_rel + row - col
        valid = (dist >= 0) & (dist < WINDOW)
        out.append(jnp.where(valid[None], -slope_rows * dist.astype(F32)[None], NEG))
    return jnp.stack(out)


def _stack_heads(ref, rows):
    return jnp.concatenate([ref[rows, j * LANES:(j + 1) * LANES] for j in range(A_GROUP)], axis=0)


def _sink_rows(sink_ref, g):
    return jnp.concatenate([jnp.full((WINDOW, LANES), sink_ref[g * A_GROUP + j], F32) for j in range(A_GROUP)], axis=0)


def _rep(col):
    return jnp.broadcast_to(col, (col.shape[0], LANES))


def _swa_specs(nq, tq, seq):
    smem = pl.BlockSpec(memory_space=pltpu.SMEM)
    qspec = pl.BlockSpec((tq, A_GROUP * LANES), lambda b, g, i: (b * nq + i, g))
    kspec = pl.BlockSpec((seq, LANES), lambda b, g, i: (b, g))
    bias_first = pl.BlockSpec((None, None, A_GROUP * WINDOW, 2 * WINDOW),
                              lambda b, g, i: (jnp.minimum(i, 1), g, 0, 0))
    bias_rest = pl.BlockSpec((None, None, A_GROUP * WINDOW, 2 * WINDOW), lambda b, g, i: (1, g, 0, 0))
    return smem, qspec, kspec, bias_first, bias_rest


def _swa_fwd(qa, ka, va, sinks, bias, nb, seq):
    t_all = qa.shape[0]
    tq = min(512, seq)
    nq = seq // tq

    def body(sink_ref, q_ref, k_ref, v_ref, bias0_ref, bias_ref, o_ref, l_ref):
        qi = pl.program_id(2)
        sink = _sink_rows(sink_ref, pl.program_id(1))
        for a in range(tq // WINDOW):
            t0 = qi * tq + a * WINDOW
            start = pl.multiple_of(jnp.maximum(t0 - WINDOW, 0), WINDOW)
            rows = slice(a * WINDOW, (a + 1) * WINDOW)
            k = k_ref[pl.ds(start, 2 * WINDOW), :]
            v = v_ref[pl.ds(start, 2 * WINDOW), :]
            s = _dot_nt(_stack_heads(q_ref, rows), k) + (bias0_ref if a == 0 else bias_ref)[...]
            s0, s1 = s[:, :LANES], s[:, LANES:]
            m = jnp.maximum(_rep(jnp.max(jnp.maximum(s0, s1), axis=-1, keepdims=True)), sink)
            p0, p1 = jnp.exp(s0 - m), jnp.exp(s1 - m)
            den = _rep(jnp.sum(p0 + p1, axis=-1, keepdims=True)) + jnp.exp(sink - m)
            inv = 1.0 / den
            o = _dot(jnp.concatenate([(p0 * inv).astype(BF16), (p1 * inv).astype(BF16)], axis=1), v).astype(BF16)
            lrow = m + jnp.log(den)
            for j in range(A_GROUP):
                o_ref[rows, j * LANES:(j + 1) * LANES] = o[j * WINDOW:(j + 1) * WINDOW]
                l_ref[rows, j * LANES:(j + 1) * LANES] = lrow[j * WINDOW:(j + 1) * WINDOW]

    smem, qspec, kspec, bias_first, bias_rest = _swa_specs(nq, tq, seq)
    return _pcall(
        body, name="swa_fwd", grid=(nb, A_KV_HEADS, nq),
        in_specs=[smem, qspec, kspec, kspec, bias_first, bias_rest], out_specs=[qspec, qspec],
        out_shape=[jax.ShapeDtypeStruct((t_all, A_HEADS * LANES), BF16),
                   jax.ShapeDtypeStruct((t_all, A_HEADS * LANES), F32)],
        compiler_params=_params(("parallel", "parallel", "arbitrary")),
    )(sinks, qa, ka, va, bias, bias)


def _outproj(x2, oa, ob, wo_pad, g2):
    t_all = x2.shape[0]
    tm = min(512, t_all)
    half = A_HEADS * LANES

    def body(x_ref, oa_ref, ob_ref, w_ref, g_ref, h_ref, hn_ref):
        h = x_ref[...] + _dot(oa_ref[...], w_ref[:half, :]) + _dot(ob_ref[...], w_ref[half:, :])
        h_ref[...] = h
        r = lax.rsqrt(jnp.mean(h * h, axis=-1, keepdims=True) + EPS)
        hn_ref[...] = (h * r * g_ref[...]).astype(BF16)

    return _pcall(
        body, name="outproj", grid=(t_all // tm,),
        in_specs=[_rows(tm, D_MODEL), _rows(tm, half), _rows(tm, half), _const((MIXED_P, D_MODEL)),
                  _const((1, D_MODEL))],
        out_specs=[_rows(tm, D_MODEL), _rows(tm, D_MODEL)],
        out_shape=[jax.ShapeDtypeStruct((t_all, D_MODEL), F32), jax.ShapeDtypeStruct((t_all, D_MODEL), BF16)],
        compiler_params=_params(("parallel",)),
    )(x2, oa, ob, wo_pad, g2)


def _mlp_fwd(hn, w_up_blocks, w_down, h, tgt):
    t_all = h.shape[0]
    tm = min(256, t_all)
    nj = D_FF // D_MODEL

    def body(a_ref, wu_ref, wd_ref, h_ref, t_ref, ru_ref, dy_ref, dyb_ref, loss_ref):
        @pl.when(pl.program_id(0) == 0)
        def _():
            loss_ref[...] = jnp.zeros_like(loss_ref)

        a = a_ref[...]
        y = h_ref[...]
        for j in range(nj):
            cols = slice(j * D_MODEL, (j + 1) * D_MODEL)
            ru = jnp.maximum(_dot(a, wu_ref[j]), 0.0)
            ru_ref[:, cols] = ru.astype(BF16)
            y = y + _dot((ru * ru).astype(BF16), wd_ref[cols, :])
        err = y - t_ref[...]
        loss_ref[...] += jnp.sum(err * err)
        dy = err * (1.0 / D_MODEL)
        dy_ref[...] = dy
        dyb_ref[...] = dy.astype(BF16)

    return _pcall(
        body, name="mlp_fwd", grid=(t_all // tm,),
        in_specs=[_rows(tm, D_MODEL), _const((nj, D_MODEL, D_MODEL)), _const((D_FF, D_MODEL)), _rows(tm, D_MODEL),
                  _rows(tm, D_MODEL)],
        out_specs=[_rows(tm, D_FF), _rows(tm, D_MODEL), _rows(tm, D_MODEL), _const((8, LANES))],
        out_shape=[jax.ShapeDtypeStruct((t_all, D_FF), BF16), jax.ShapeDtypeStruct((t_all, D_MODEL), F32),
                   jax.ShapeDtypeStruct((t_all, D_MODEL), BF16), jax.ShapeDtypeStruct((8, LANES), F32)],
        compiler_params=_params(("arbitrary",)),
    )(hn, w_up_blocks, w_down, h, tgt)


def _mlp_bwd_w(dyb, w_down, ru, hn):
    t_all = dyb.shape[0]
    tm = min(512, t_all)
    nj = D_FF // D_MODEL

    def body(dy_ref, w_ref, ru_ref, hn_ref, du_ref, dw_ref):
        @pl.when(pl.program_id(1) == 0)
        def _():
            dw_ref[...] = jnp.zeros_like(dw_ref)

        dy = dy_ref[...]
        ru = ru_ref[...].astype(F32)
        du = (_dot_nt(dy, w_ref[...]) * (2.0 * ru)).astype(BF16)
        du_ref[...] = du
        dw_ref[0] += _dot_tn(hn_ref[...], du)
        dw_ref[1] += _dot_tn((ru * ru).astype(BF16), dy)

    tok = pl.BlockSpec((tm, D_MODEL), lambda j, i: (i, 0))
    blk = pl.BlockSpec((tm, D_MODEL), lambda j, i: (i, j))
    wspec = pl.BlockSpec((2, None, D_MODEL, D_MODEL), lambda j, i: (0, j, 0, 0))
    return _pcall(
        body, name="mlp_bwd_w", grid=(nj, t_all // tm),
        in_specs=[tok, pl.BlockSpec((D_MODEL, D_MODEL), lambda j, i: (j, 0)), blk, tok],
        out_specs=[blk, wspec],
        out_shape=[jax.ShapeDtypeStruct((t_all, D_FF), BF16), jax.ShapeDtypeStruct((2, nj, D_MODEL, D_MODEL), F32)],
        compiler_params=_params(("parallel", "arbitrary")),
    )(dyb, w_down, ru, hn)


def _pair_exchange_copy(g_ref, out_ref, send_sem, recv_sem):
    x, y, c = _place()
    return pltpu.make_async_remote_copy(
        src_ref=g_ref.at[1 - c], dst_ref=out_ref, send_sem=send_sem, recv_sem=recv_sem,
        device_id=(x, y, 1 - c), device_id_type=MESH)


def _mlp_dhn(du, w_up_blocks, h, dy, g2, d_w_mlp):
    t_all = h.shape[0]
    tm = min(256, t_all)
    n_steps = t_all // tm

    def body(a_ref, w_ref, h_ref, dy_ref, g_ref, dw_ref, dh_ref, dhb_ref, dg_ref, got_ref, send_sem, recv_sem):
        @pl.when(pl.program_id(0) == 0)
        def _():
            dg_ref[...] = jnp.zeros_like(dg_ref)
            _pair_exchange_copy(dw_ref, got_ref, send_sem, recv_sem).start()

        dhn = _dot_nt(a_ref[:, :D_MODEL], w_ref[0])
        for j in range(1, D_FF // D_MODEL):
            dhn = dhn + _dot_nt(a_ref[:, j * D_MODEL:(j + 1) * D_MODEL], w_ref[j])
        h = h_ref[...]
        r = lax.rsqrt(jnp.mean(h * h, axis=-1, keepdims=True) + EPS)
        hh = h * r
        dg_ref[...] += jnp.sum(dhn * hh, axis=0, keepdims=True)
        dz = dhn * g_ref[...]
        dh = dy_ref[...] + r * (dz - hh * jnp.mean(dz * hh, axis=-1, keepdims=True))
        dh_ref[...] = dh
        dhb_ref[...] = dh.astype(BF16)

        @pl.when(pl.program_id(0) == n_steps - 1)
        def _():
            _pair_exchange_copy(dw_ref, got_ref, send_sem, recv_sem).wait()

    return _pcall(
        body, name="mlp_dhn", grid=(n_steps,),
        in_specs=[_rows(tm, D_FF), _const((D_FF // D_MODEL, D_MODEL, D_MODEL)), _rows(tm, D_MODEL),
                  _rows(tm, D_MODEL), _const((1, D_MODEL)), ANY],
        out_specs=[_rows(tm, D_MODEL), _rows(tm, D_MODEL), _const((1, D_MODEL)), ANY],
        out_shape=[jax.ShapeDtypeStruct((t_all, D_MODEL), F32), jax.ShapeDtypeStruct((t_all, D_MODEL), BF16),
                   jax.ShapeDtypeStruct((1, D_MODEL), F32), jax.ShapeDtypeStruct(d_w_mlp.shape[1:], F32)],
        scratch_shapes=[pltpu.SemaphoreType.DMA, pltpu.SemaphoreType.DMA],
        compiler_params=_params(("arbitrary",)),
    )(du, w_up_blocks, h, dy, g2, d_w_mlp)


def _dmixed(dhb, wo_pad, oa, ob):
    t_all = dhb.shape[0]
    tm = min(512, t_all)
    half = A_HEADS * LANES

    def body(a_ref, w_ref, oa_ref, ob_ref, da_ref, db_ref, delta_ref, dwo_ref):
        @pl.when(pl.program_id(0) == 0)
        def _():
            dwo_ref[...] = jnp.zeros_like(dwo_ref)

        a = a_ref[...]
        d = _dot_nt(a, w_ref[...])
        da_ref[...] = d[:, :half].astype(BF16)
        db_ref[...] = d[:, half:].astype(BF16)
        for h in range(B_HEADS):
            cols = slice(h * LANES, (h + 1) * LANES)
            prod = d[:, half + h * LANES:half + (h + 1) * LANES] * ob_ref[:, cols].astype(F32)
            delta_ref[:, cols] = jnp.broadcast_to(jnp.sum(prod, axis=-1, keepdims=True), (tm, LANES))
        dwo_ref[:half, :] += _dot_tn(oa_ref[...], a)
        dwo_ref[half:, :] += _dot_tn(ob_ref[...], a)

    return _pcall(
        body, name="dmixed", grid=(t_all // tm,),
        in_specs=[_rows(tm, D_MODEL), _const((MIXED_P, D_MODEL)), _rows(tm, half), _rows(tm, half)],
        out_specs=[_rows(tm, half), _rows(tm, half), _rows(tm, half), _const((MIXED_P, D_MODEL))],
        out_shape=[jax.ShapeDtypeStruct((t_all, half), BF16), jax.ShapeDtypeStruct((t_all, half), BF16),
                   jax.ShapeDtypeStruct((t_all, half), F32), jax.ShapeDtypeStruct((MIXED_P, D_MODEL), F32)],
        compiler_params=_params(("arbitrary",)),
    )(dhb, wo_pad, oa, ob)


def _fox_bwd(qb, kb, vb, dob, lse, delta, nb, seq, pair_sums):
    t_all = qb.shape[0]
    tk = min(FOX_TK, seq // FOX_PARTS_BWD)
    tq = FOX_PARTS_BWD * tk
    nk = seq // tk

    def body(q_ref, k_ref, v_ref, do_ref, lse_ref, delta_ref, pair_ref, dq_ref, dk_ref, dv_ref, got_ref,
             s_ref, dp_ref, p_ref, ds_ref, dk_acc, dv_acc, send_sems, recv_sems):
        kj = pl.program_id(2)
        bh = pl.program_id(0) * B_HEADS + pl.program_id(1)

        @pl.when((bh == 0) & (kj == 0))
        def _():
            for cp in _chip_exchange_copies(pair_ref, got_ref, send_sems, recv_sems):
                cp.start()

        @pl.when(kj == 0)
        def _():
            dq_ref[...] = jnp.zeros_like(dq_ref)

        dk_acc[...] = jnp.zeros_like(dk_acc)
        dv_acc[...] = jnp.zeros_like(dv_acc)
        k = k_ref[...]
        v = v_ref[...]

        def block(off, r0, r1, masked):
            qrows = pl.ds(pl.multiple_of(off + r0, CHUNK), r1 - r0)
            q = q_ref[qrows, :]
            do = do_ref[qrows, :]
            s_ref[r0:r1, :] = _dot_nt(q, k)
            dp_ref[r0:r1, :] = _dot_nt(do, v)
            for r in range(r0, r1, CHUNK):
                rows = slice(r, r + CHUNK)
                chunk = pl.ds(pl.multiple_of(off + r, CHUNK), CHUNK)
                lse_c = lse_ref[chunk, :]
                delta_c = delta_ref[chunk, :]
                for jt in range(tk // LANES):
                    cols = slice(jt * LANES, (jt + 1) * LANES)
                    p = jnp.exp2(s_ref[rows, cols] - lse_c)
                    if masked:
                        row = r - r0 + lax.broadcasted_iota(jnp.int32, (CHUNK, LANES), 0)
                        col = jt * LANES + lax.broadcasted_iota(jnp.int32, (CHUNK, LANES), 1)
                        p = jnp.where(row >= col, p, 0.0)
                    p_ref[rows, cols] = p.astype(BF16)
                    ds_ref[rows, cols] = (p * (dp_ref[rows, cols] - delta_c)).astype(BF16)
            dv_acc[...] += _dot_tn(p_ref[r0:r1, :], do)
            dk_acc[...] += _dot_tn(ds_ref[r0:r1, :], q)
            dq_ref[qrows, :] += _dot(ds_ref[r0:r1, :], k)

        first = kj // FOX_PARTS_BWD
        off_first = pl.multiple_of(first * tq, tq)
        for d in range(FOX_PARTS_BWD):
            @pl.when(kj % FOX_PARTS_BWD == d)
            def _(d=d):
                block(off_first, d * tk, (d + 1) * tk, True)
                if d < FOX_PARTS_BWD - 1:
                    block(off_first, (d + 1) * tk, tq, False)

        def later(i, carry):
            block(pl.multiple_of(i * tq, tq), 0, tq, False)
            return carry

        lax.fori_loop(first + 1, seq // tq, later, 0)
        dk_ref[...] = dk_acc[...]
        dv_ref[...] = dv_acc[...]

        @pl.when((bh == nb * B_HEADS - 1) & (kj == nk - 1))
        def _():
            for cp in _chip_exchange_copies(pair_ref, got_ref, send_sems, recv_sems):
                cp.wait()

    full = pl.BlockSpec((seq, LANES), lambda b, h, j: (b, h))
    tile = pl.BlockSpec((tk, LANES), lambda b, h, j: (b * nk + j, h))
    shp = jax.ShapeDtypeStruct((t_all, B_HEADS * LANES), F32)
    return _pcall(
        body, name="fox_bwd", grid=(nb, B_HEADS, nk),
        in_specs=[full, tile, tile, full, full, full, ANY], out_specs=[full, tile, tile, ANY],
        out_shape=[shp, shp, shp, jax.ShapeDtypeStruct((3,) + pair_sums.shape[1:], pair_sums.dtype)],
        scratch_shapes=[pltpu.VMEM((tq, tk), F32), pltpu.VMEM((tq, tk), F32), pltpu.VMEM((tq, tk), BF16),
                        pltpu.VMEM((tq, tk), BF16), pltpu.VMEM((tk, LANES), F32), pltpu.VMEM((tk, LANES), F32),
                        pltpu.SemaphoreType.DMA((3,)), pltpu.SemaphoreType.DMA((3,))],
        compiler_params=_params(("arbitrary", "arbitrary", "arbitrary")),
    )(qb, kb, vb, dob, lse, delta, pair_sums)


def _swa_bwd(qa, ka, va, oa, doa, lrow, sinks, bias, nb, seq):
    t_all = qa.shape[0]
    tq = min(512, seq)
    nq = seq // tq

    def body(sink_ref, q_ref, k_ref, v_ref, bias0_ref, bias_ref, o_ref, do_ref, l_ref,
             dq_ref, dk_ref, dv_ref, dsink_ref):
        qi = pl.program_id(2)
        sink = _sink_rows(sink_ref, pl.program_id(1))

        @pl.when(qi == 0)
        def _():
            dk_ref[...] = jnp.zeros_like(dk_ref)
            dv_ref[...] = jnp.zeros_like(dv_ref)
            dsink_ref[...] = jnp.zeros_like(dsink_ref)

        for a in range(tq // WINDOW):
            t0 = qi * tq + a * WINDOW
            start = pl.multiple_of(jnp.maximum(t0 - WINDOW, 0), WINDOW)
            rows = slice(a * WINDOW, (a + 1) * WINDOW)
            win = pl.ds(start, 2 * WINDOW)
            q = _stack_heads(q_ref, rows)
            k = k_ref[win, :]
            v = v_ref[win, :]
            do = _stack_heads(do_ref, rows)
            lrow = _stack_heads(l_ref, rows)
            s = _dot_nt(q, k) + (bias0_ref if a == 0 else bias_ref)[...]
            dp = _dot_nt(do, v)
            delta = _rep(jnp.sum(do.astype(F32) * _stack_heads(o_ref, rows).astype(F32), axis=-1, keepdims=True))
            p = [jnp.exp(s[:, t * LANES:(t + 1) * LANES] - lrow) for t in range(2)]
            ds = jnp.concatenate([(p[t] * (dp[:, t * LANES:(t + 1) * LANES] - delta)).astype(BF16) for t in range(2)],
                                 axis=1)
            dq = _dot(ds, k)
            dk_ref[win, :] += _dot_tn(ds, q)
            dv_ref[win, :] += _dot_tn(jnp.concatenate([p[0].astype(BF16), p[1].astype(BF16)], axis=1), do)
            sink_term = jnp.exp(sink - lrow) * delta
            for j in range(A_GROUP):
                part = slice(j * WINDOW, (j + 1) * WINDOW)
                dq_ref[rows, j * LANES:(j + 1) * LANES] = dq[part]
                dsink_ref[j:j + 1, :] -= jnp.sum(sink_term[part], axis=0, keepdims=True)

    smem, qspec, kspec, bias_first, bias_rest = _swa_specs(nq, tq, seq)
    return _pcall(
        body, name="swa_bwd", grid=(nb, A_KV_HEADS, nq),
        in_specs=[smem, qspec, kspec, kspec, bias_first, bias_rest, qspec, qspec, qspec],
        out_specs=[qspec, kspec, kspec, pl.BlockSpec((None, 8, LANES), lambda b, g, i: (b * A_KV_HEADS + g, 0, 0))],
        out_shape=[jax.ShapeDtypeStruct((t_all, A_HEADS * LANES), F32),
                   jax.ShapeDtypeStruct((t_all, A_KV_HEADS * LANES), F32),
                   jax.ShapeDtypeStruct((t_all, A_KV_HEADS * LANES), F32),
                   jax.ShapeDtypeStruct((nb * A_KV_HEADS, 8, LANES), F32)],
        compiler_params=_params(("parallel", "parallel", "arbitrary")),
    )(sinks, qa, ka, va, bias, bias, oa, doa, lrow)


def _dproj_dx(pre, dqa, dka, dqb, dkb, dva, dvb, z, x2, dh, gain_row, w_pad_t, g1, seq):
    t_all = pre.shape[0]
    tm = min(256, seq)
    nt = t_all // tm
    tiles_per_seq = seq // tm
    triu = _tri(tm, True)
    sel = _dc_select()

    def body(pre_ref, dqa_ref, dka_ref, dqb_ref, dkb_ref, dva_ref, dvb_ref, z_ref, x_ref, dh_ref, gain_ref, triu_ref,
             sel_ref, w_ref, g_ref, dproj_ref, small_ref, dx_ref, dg_ref, carry_ref):
        i = pl.program_id(0)

        @pl.when(i == 0)
        def _():
            small_ref[...] = jnp.zeros_like(small_ref)
            dg_ref[...] = jnp.zeros_like(dg_ref)

        @pl.when(i % tiles_per_seq == 0)
        def _():
            carry_ref[...] = jnp.zeros_like(carry_ref)

        def norm_bwd(g, dhat):
            cols = slice(g * LANES, (g + 1) * LANES)
            p = pre_ref[:, cols].astype(F32)
            rr = lax.rsqrt(jnp.sum(p * p, axis=-1, keepdims=True) * (1.0 / HEAD_DIM) + EPS)
            n = p * rr
            dz = dhat * gain_ref[:, cols]
            dproj_ref[:, cols] = (rr * (dz - n * (jnp.sum(dz * n, axis=-1, keepdims=True) * (1.0 / HEAD_DIM)))
                                  ).astype(BF16)
            return jnp.sum(dhat * n, axis=0, keepdims=True)

        def group_sum(g0, d_ref, count, scale):
            acc = jnp.zeros((1, LANES), F32)
            for h in range(count):
                d = d_ref[:, h * LANES:(h + 1) * LANES]
                acc = acc + norm_bwd(g0 + h, d * scale if scale != 1.0 else d)
            return acc

        small_ref[0:1, :] += group_sum(G_QA, dqa_ref, A_HEADS, SCALE)
        small_ref[1:2, :] += group_sum(G_KA, dka_ref, A_KV_HEADS, 1.0)
        small_ref[2:3, :] += group_sum(G_QB, dqb_ref, B_HEADS, SCALE)
        small_ref[3:4, :] += group_sum(G_KB, dkb_ref, B_HEADS, LN2)
        dproj_ref[:, G_VA * LANES:G_VB * LANES] = dva_ref[...].astype(BF16)
        dproj_ref[:, G_VB * LANES:G_F * LANES] = dvb_ref[...].astype(BF16)

        dc = jnp.zeros((tm, LANES), F32)
        for piece_q, piece_k in zip(_split3(dqb_ref[...]), _split3(dkb_ref[...])):
            dc = dc + _dot(jnp.concatenate([piece_q, piece_k], axis=1), sel_ref[...])
        dlf = _dot_exact(triu_ref[...], dc) + carry_ref[...]
        carry_ref[...] += jnp.sum(dc, axis=0, keepdims=True)
        dz = dlf / (1.0 + jnp.exp(z_ref[...]))
        small_ref[4:5, :] += jnp.sum(dz, axis=0, keepdims=True)
        dproj_ref[:, G_F * LANES:(G_F + 1) * LANES] = dz.astype(BF16)
        dproj_ref[:, (G_F + 1) * LANES:] = jnp.zeros((tm, LANES), BF16)

        dxn = _dot(dproj_ref[...], w_ref[...])
        x = x_ref[...]
        r = lax.rsqrt(jnp.mean(x * x, axis=-1, keepdims=True) + EPS)
        xh = x * r
        dg_ref[...] += jnp.sum(dxn * xh, axis=0, keepdims=True)
        dxz = dxn * g_ref[...]
        dx_ref[...] = dh_ref[...] + r * (dxz - xh * jnp.mean(dxz * xh, axis=-1, keepdims=True))

    def rev(n):
        return pl.BlockSpec((tm, n), lambda i: (nt - 1 - i, 0))

    return _pcall(
        body, name="dproj_dx", grid=(nt,),
        in_specs=[rev(N_NORM_GROUPS * LANES), rev(A_HEADS * LANES), rev(A_KV_HEADS * LANES), rev(B_HEADS * LANES),
                  rev(B_HEADS * LANES), rev(A_KV_HEADS * LANES), rev(B_HEADS * LANES), rev(LANES), rev(D_MODEL),
                  rev(D_MODEL), _const((1, NP)), _const((tm, tm)), _const(sel.shape), _const((NP, D_MODEL)),
                  _const((1, D_MODEL))],
        out_specs=[rev(NP), _const((8, LANES)), rev(D_MODEL), _const((1, D_MODEL))],
        out_shape=[jax.ShapeDtypeStruct((t_all, NP), BF16), jax.ShapeDtypeStruct((8, LANES), F32),
                   jax.ShapeDtypeStruct((t_all, D_MODEL), F32), jax.ShapeDtypeStruct((1, D_MODEL), F32)],
        scratch_shapes=[pltpu.VMEM((1, LANES), F32)],
        compiler_params=_params(("arbitrary",)),
    )(pre, dqa, dka, dqb, dkb, dva, dvb, z, x2, dh, gain_row, triu, sel, w_pad_t, g1)


def _dwin(dproj, xn):
    t_all = xn.shape[0]
    tt = min(512, t_all)
    half = NP // 2

    def body(a_ref, b_ref, o_ref):
        @pl.when(pl.program_id(1) == 0)
        def _():
            o_ref[...] = jnp.zeros_like(o_ref)

        o_ref[...] += _dot_tn(a_ref[...], b_ref[...])

    return _pcall(
        body, name="dwin", grid=(2, t_all // tt),
        in_specs=[pl.BlockSpec((tt, half), lambda j, t: (t, j)), pl.BlockSpec((tt, D_MODEL), lambda j, t: (t, 0))],
        out_specs=pl.BlockSpec((half, D_MODEL), lambda j, t: (j, 0)),
        out_shape=jax.ShapeDtypeStruct((NP, D_MODEL), F32),
        compiler_params=_params(("parallel", "arbitrary")),
    )(dproj, xn)


ANY = pl.BlockSpec(memory_space=pl.ANY)


def _place():
    return lax.axis_index("x"), lax.axis_index("y"), lax.axis_index("c")


class _Gather:
    def __init__(self, srcs, outs, send_sems, recv_sems, local_sems):
        self.srcs, self.outs = srcs, outs
        self.send_sems, self.recv_sems, self.local_sems = send_sems, recv_sems, local_sems
        x, y, c = _place()
        self.c = c
        self.me, self.sibling = (x, y, c), (x, y, 1 - c)
        self.chips = [(1 - x, y), (x, 1 - y), (1 - x, 1 - y)]

    def _rows(self, a, px, py, pc):
        m = self.srcs[a].shape[0]
        return self.outs[a].at[pl.ds((4 * px + 2 * py + pc) * m, m), :]

    def _copy(self, a, k, block, to, from_src=False):
        return pltpu.make_async_remote_copy(
            src_ref=self.srcs[a] if from_src else self._rows(a, *block), dst_ref=self._rows(a, *block),
            send_sem=self.send_sems.at[k, a], recv_sem=self.recv_sems.at[k, a], device_id=to, device_id_type=MESH)

    def _own(self, a):
        return pltpu.make_async_copy(self.srcs[a], self._rows(a, *self.me), self.local_sems.at[a])

    def start(self):
        for a in range(len(self.srcs)):
            self._own(a).start()
            self._copy(a, 0, self.me, self.sibling, from_src=True).start()
            for j, chip in enumerate(self.chips):
                self._copy(a, 1 + j, self.me, (*chip, self.c), from_src=True).start()

    def forward(self):
        for a in range(len(self.srcs)):
            for j, chip in enumerate(self.chips):
                self._copy(a, 1 + j, (*chip, self.c), self.me).wait_recv()
                self._copy(a, 4 + j, (*chip, self.c), self.sibling).start()

    def finish(self):
        for a in range(len(self.srcs)):
            self._copy(a, 0, self.sibling, self.me).wait_recv()
            for j, chip in enumerate(self.chips):
                self._copy(a, 4 + j, (*chip, 1 - self.c), self.me).wait_recv()
            self._copy(a, 0, self.me, self.sibling, from_src=True).wait_send()
            for j, chip in enumerate(self.chips):
                self._copy(a, 1 + j, self.me, (*chip, self.c), from_src=True).wait_send()
                self._copy(a, 4 + j, (*chip, self.c), self.sibling).wait_send()
            self._own(a).wait()


def _gather_scratch(n_arrays):
    return [pltpu.SemaphoreType.DMA((7, n_arrays)), pltpu.SemaphoreType.DMA((7, n_arrays)),
            pltpu.SemaphoreType.DMA((n_arrays,))]


def _allgather_halves(mine):
    m_per, n = mine.shape

    def body(x_ref, out_ref, send_sems, recv_sems, local_sems):
        gather = _Gather((x_ref,), (out_ref,), send_sems, recv_sems, local_sems)
        gather.start()
        gather.forward()
        gather.finish()

    return _pcall(
        body, name="allgather_w_in",
        out_shape=jax.ShapeDtypeStruct((8 * m_per, n), mine.dtype),
        in_specs=[ANY], out_specs=ANY, scratch_shapes=_gather_scratch(1),
    )(mine)


def _rs_pair_exchange(g, name):
    def body(g_ref, out_ref, send_sem, recv_sem):
        x, y, c = _place()
        cp = pltpu.make_async_remote_copy(
            src_ref=g_ref.at[1 - c], dst_ref=out_ref, send_sem=send_sem, recv_sem=recv_sem,
            device_id=(x, y, 1 - c), device_id_type=MESH)
        cp.start()
        cp.wait()

    return _pcall(
        body, name=name, out_shape=jax.ShapeDtypeStruct(g.shape[1:], F32),
        in_specs=[ANY], out_specs=ANY, scratch_shapes=[pltpu.SemaphoreType.DMA, pltpu.SemaphoreType.DMA],
    )(g)


def _rs_pair_add(g, got, c_idx, name):
    rows = g.shape[2]

    def body(c_ref, a_ref, b_ref, o_ref, ob_ref):
        pair = a_ref[...] + b_ref[...]
        o_ref[...] = pair
        ob_ref[...] = pair.astype(BF16)

    blk = pl.BlockSpec((None, rows, D_MODEL), lambda s, c_ref: (s, 0, 0))
    return _pcall(
        body, name=name,
        grid_spec=pltpu.PrefetchScalarGridSpec(
            num_scalar_prefetch=1, grid=(N_CHIPS,),
            in_specs=[pl.BlockSpec((None, None, rows, D_MODEL), lambda s, c_ref: (c_ref[0], s, 0, 0)), blk],
            out_specs=[blk, blk]),
        out_shape=[jax.ShapeDtypeStruct((N_CHIPS, rows, D_MODEL), F32),
                   jax.ShapeDtypeStruct((N_CHIPS, rows, D_MODEL), BF16)],
        compiler_params=_params(("parallel",)),
    )(c_idx, g, got)


def _chip_exchange_copies(p_ref, out_ref, send_sems, recv_sems):
    x, y, c = _place()
    chips = [(1 - x, y), (x, 1 - y), (1 - x, 1 - y)]
    return [pltpu.make_async_remote_copy(
        src_ref=p_ref.at[2 * cx + cy], dst_ref=out_ref.at[j], send_sem=send_sems.at[j], recv_sem=recv_sems.at[j],
        device_id=(cx, cy, c), device_id_type=MESH) for j, (cx, cy) in enumerate(chips)]


def _rs_chip_exchange(p4, name):
    def body(p_ref, out_ref, send_sems, recv_sems):
        cps = _chip_exchange_copies(p_ref, out_ref, send_sems, recv_sems)
        for cp in cps:
            cp.start()
        for cp in cps:
            cp.wait()

    return _pcall(
        body, name=name, out_shape=jax.ShapeDtypeStruct((3,) + p4.shape[1:], p4.dtype),
        in_specs=[ANY], out_specs=ANY,
        scratch_shapes=[pltpu.SemaphoreType.DMA((3,)), pltpu.SemaphoreType.DMA((3,))],
    )(p4)


def _rs_chip_add(p4, got, sc_idx, name):
    rows = p4.shape[1]
    tr = next(rows // n for n in (8, 7, 6, 5, 4, 3, 2, 1) if rows % n == 0 and (rows // n) % 16 == 0)

    def body(sc_ref, a_ref, b_ref, o_ref):
        o_ref[...] = ((a_ref[...] + b_ref[0].astype(F32)) + b_ref[1].astype(F32)) + b_ref[2].astype(F32)

    return _pcall(
        body, name=name,
        grid_spec=pltpu.PrefetchScalarGridSpec(
            num_scalar_prefetch=1, grid=(rows // tr,),
            in_specs=[pl.BlockSpec((None, tr, D_MODEL), lambda i, sc_ref: (sc_ref[0], i, 0)),
                      pl.BlockSpec((3, tr, D_MODEL), lambda i, sc_ref: (0, i, 0))],
            out_specs=pl.BlockSpec((None, tr, D_MODEL), lambda i, sc_ref: (sc_ref[1], i, 0))),
        out_shape=jax.ShapeDtypeStruct((2, rows, D_MODEL), F32),
        compiler_params=_params(("parallel",)),
    )(sc_idx, p4, got)


def _rs_pair_share(halves, name):
    def body(r_ref, out_ref, send_sem, recv_sem):
        x, y, c = _place()
        cp = pltpu.make_async_remote_copy(
            src_ref=r_ref.at[c], dst_ref=out_ref.at[c], send_sem=send_sem, recv_sem=recv_sem,
            device_id=(x, y, 1 - c), device_id_type=MESH)
        cp.start()
        cp.wait()

    return _pcall(
        body, name=name, out_shape=jax.ShapeDtypeStruct(halves.shape, F32),
        in_specs=[ANY], out_specs=ANY, input_output_aliases={0: 0},
        scratch_shapes=[pltpu.SemaphoreType.DMA, pltpu.SemaphoreType.DMA],
    )(halves)


def _adam(w, g, m, v):
    m2 = ADAM_B1 * m + (1.0 - ADAM_B1) * g
    v2 = ADAM_B2 * v + (1.0 - ADAM_B2) * (g * g)
    m_hat = m2 / (1.0 - ADAM_B1 ** ADAM_STEP)
    v_hat = v2 / (1.0 - ADAM_B2 ** ADAM_STEP)
    return -ADAM_LR * (m_hat / (jnp.sqrt(v_hat) + ADAM_EPS) + ADAM_WD * w), m2, v2


def _small_allreduce_adamw(part, w, m, v):
    pieces = ((0, 8, LANES), (20, 1, B_HEADS), (16, 1, HEAD_DIM), (17, 1, HEAD_DIM), (21, 1, A_HEADS),
              (18, 1, HEAD_DIM), (19, 1, HEAD_DIM), (8, 8, LANES))

    def body(p_ref, w_ref, m_ref, v_ref, *rest):
        outs, (loss_ref, buf, stage, send_sems, recv_sems) = rest[:4 * len(pieces)], rest[4 * len(pieces):]
        x, y, c = _place()
        me = 4 * x + 2 * y + c
        cps = []
        for k in range(1, 8):
            peer = (1 - x if k & 4 else x, 1 - y if k & 2 else y, 1 - c if k & 1 else c)
            cps.append(pltpu.make_async_remote_copy(
                src_ref=p_ref, dst_ref=buf.at[me], send_sem=send_sems.at[k - 1], recv_sem=recv_sems.at[k - 1],
                device_id=peer, device_id_type=MESH))
        for cp in cps:
            cp.start()
        buf[me] = p_ref[...]
        for cp in cps:
            cp.wait()
        g = buf[0]
        for k in range(1, 8):
            g = g + buf[k]
        for kind, packed in enumerate((g,) + _adam(w_ref[...], g, m_ref[...], v_ref[...])):
            stage[...] = packed
            if kind == 0:
                loss_ref[...] = stage[ROW_LOSS:ROW_LOSS + 1, :]
            for i, (row, rows, lanes) in enumerate(pieces):
                outs[kind * len(pieces) + i][...] = stage[row:row + rows, 0:lanes]

    vm = pl.BlockSpec(memory_space=pltpu.VMEM)
    shapes = [jax.ShapeDtypeStruct((rows, lanes), F32) for _ in range(4) for _, rows, lanes in pieces]
    shapes.append(jax.ShapeDtypeStruct((1, LANES), F32))
    res = _pcall(
        body, name="small_allreduce_adamw",
        out_shape=shapes, in_specs=[vm, vm, vm, vm], out_specs=[vm] * len(shapes),
        scratch_shapes=[pltpu.VMEM((8, SMALL_ROWS, LANES), F32), pltpu.VMEM((SMALL_ROWS, LANES), F32),
                        pltpu.SemaphoreType.DMA((7,)), pltpu.SemaphoreType.DMA((7,))],
    )(part, w, m, v)
    flat = [r.reshape(r.size) for r in res[:-1]]
    n = len(pieces)
    return [flat[k * n:(k + 1) * n] for k in range(4)], res[-1][0, 0]


def _adamw(w, g, m, v, name):
    rows, cols = w.shape
    tr = min(256, rows)

    def body(w_ref, g_ref, m_ref, v_ref, d_ref, m2_ref, v2_ref):
        d_ref[...], m2_ref[...], v2_ref[...] = _adam(w_ref[...], g_ref[...], m_ref[...], v_ref[...])

    spec = _rows(tr, cols)
    shp = jax.ShapeDtypeStruct((rows, cols), F32)
    return _pcall(
        body, name=name, grid=(rows // tr,), in_specs=[spec] * 4, out_specs=[spec] * 3, out_shape=[shp] * 3,
        compiler_params=_params(("parallel",)),
    )(w, g, m, v)


def _pad_lanes(v):
    return jnp.pad(v, (0, LANES - v.shape[0]))


def _pad_head_rows(w_t, heads):
    n = w_t.shape[1]
    return jnp.pad(w_t.reshape(heads, HEAD_DIM, n), ((0, 0), (0, LANES - HEAD_DIM), (0, 0))).reshape(heads * LANES, n)


def _unpad_head_rows(w_t, heads):
    n = w_t.shape[1]
    return w_t.reshape(heads, LANES, n)[:, :HEAD_DIM].reshape(heads * HEAD_DIM, n)


def _in_rows_pad(w_in_t):
    qa, ka, va, qb, kb, vb, f = jnp.split(w_in_t, [512, 640, 768, 1280, 1792, 2304], axis=0)
    f = jnp.pad(f, ((0, 2 * LANES - B_HEADS), (0, 0)))
    return jnp.concatenate([_pad_head_rows(qa, 8), _pad_head_rows(ka, 2), _pad_head_rows(qb, 8),
                            _pad_head_rows(kb, 8), _pad_head_rows(va, 2), _pad_head_rows(vb, 8), f], axis=0)


def _in_rows_unpad(d):
    qa = _unpad_head_rows(d[G_QA * LANES:G_KA * LANES], 8)
    ka = _unpad_head_rows(d[G_KA * LANES:G_QB * LANES], 2)
    qb = _unpad_head_rows(d[G_QB * LANES:G_KB * LANES], 8)
    kb = _unpad_head_rows(d[G_KB * LANES:G_VA * LANES], 8)
    va = _unpad_head_rows(d[G_VA * LANES:G_VB * LANES], 2)
    vb = _unpad_head_rows(d[G_VB * LANES:G_F * LANES], 8)
    f = d[G_F * LANES:G_F * LANES + B_HEADS]
    return jnp.concatenate([qa, ka, va, qb, kb, vb, f], axis=0)


def _pack_small(g1, bf, qa, ka, sk, qb, kb, g2, loss_row):
    rows = [g1.reshape(8, LANES), g2.reshape(8, LANES)]
    rows += [_pad_lanes(t)[None] for t in (qa, ka, qb, kb, bf, sk)]
    rows += [loss_row, jnp.zeros((1, LANES), F32)]
    return jnp.concatenate(rows, axis=0)


def kernel(x, attn_norm_g, w_in, b_forget, q_norm_a, k_norm_a, sink_logits, q_norm_b, k_norm_b, w_out, mlp_norm_g, w_up, w_down, loss_target, m_attn_norm_g, m_w_in, m_b_forget, m_q_norm_a, m_k_norm_a, m_sink_logits, m_q_norm_b, m_k_norm_b, m_w_out, m_mlp_norm_g, m_w_up, m_w_down, v_attn_norm_g, v_w_in, v_b_forget, v_q_norm_a, v_k_norm_a, v_sink_logits, v_q_norm_b, v_k_norm_b, v_w_out, v_mlp_norm_g, v_w_up, v_w_down):
    nb, seq, _ = x.shape
    t_all = nb * seq
    c_idx = lax.axis_index("c")
    s_idx = 2 * lax.axis_index("x") + lax.axis_index("y")

    def my_half(a):
        halves = a.astype(BF16).reshape(2, a.shape[0] // 2, a.shape[1])
        return lax.dynamic_slice_in_dim(halves, c_idx, 1, axis=0)[0]

    w_in_shard_t = jnp.pad(w_in.T, ((0, IN_SHARD_P - IN_SHARD), (0, 0)))
    gathered_in = _allgather_halves(my_half(w_in_shard_t)).reshape(N_CHIPS, IN_SHARD_P, D_MODEL)
    w_pad_t = _in_rows_pad(gathered_in[:, :IN_SHARD].reshape(IN_WIDTH, D_MODEL))

    ones = jnp.ones((LANES,), F32)
    gain_row = jnp.concatenate(
        [jnp.tile(_pad_lanes(q_norm_a), 8), jnp.tile(_pad_lanes(k_norm_a), 2), jnp.tile(_pad_lanes(q_norm_b), 8),
         jnp.tile(_pad_lanes(k_norm_b), 8), jnp.tile(ones, N_GROUPS - N_NORM_GROUPS)])[None]
    b_row = _pad_lanes(b_forget)[None]
    g1 = attn_norm_g[None]
    g2 = mlp_norm_g[None]
    slopes = jnp.exp2(-(8.0 / A_HEADS) * (jnp.arange(A_HEADS, dtype=F32) + 1.0))

    x2 = x.reshape(t_all, D_MODEL)
    tgt = loss_target.reshape(t_all, D_MODEL)

    (xn, pre, qa, ka, va, qb, kb, vb, z), (w_out_g, w_up_g, w_down_f) = _inproj(
        x2, g1, w_pad_t, gain_row, b_row, seq, [my_half(w_out), my_half(w_up), my_half(w_down)])
    wo_pad = _pad_head_rows(w_out_g, A_HEADS + B_HEADS)
    w_up_blocks = w_up_g.reshape(N_CHIPS, D_MODEL, D_MODEL)
    swa_bias = _swa_bias(slopes)
    oa, la = _swa_fwd(qa, ka, va, sink_logits, swa_bias, nb, seq)
    ob, lse = _fox_fwd(qb, kb, vb, nb, seq)
    h, hn = _outproj(x2, oa, ob, wo_pad, g2)
    ru, dy, dyb, loss_acc = _mlp_fwd(hn, w_up_blocks, w_down_f, h, tgt)

    du, d_w_mlp = _mlp_bwd_w(dyb, w_down_f, ru, hn)
    c_arg = c_idx.reshape(1).astype(jnp.int32)
    sc_arg = jnp.stack([s_idx, c_idx]).astype(jnp.int32)
    dh, dhb, d_g2, sibling_w_mlp = _mlp_dhn(du, w_up_blocks, h, dy, g2, d_w_mlp)
    pair_m, pair_m_bf = _rs_pair_add(d_w_mlp, sibling_w_mlp, c_arg, "rs_pair_add_mlp")
    doa, dob, delta_b, d_wo = _dmixed(dhb, wo_pad, oa, ob)
    dqb, dkb, dvb, got_m = _fox_bwd(qb, kb, vb, dob, lse, delta_b, nb, seq, pair_m_bf)
    red_m = _rs_pair_share(_rs_chip_add(pair_m, got_m, sc_arg, "rs_chip_add_mlp"), "rs_pair_share_mlp")
    g_w_up, g_w_down = red_m[0], red_m[1]
    dqa, dka, dva, dsink = _swa_bwd(qa, ka, va, oa, doa, la, sink_logits, swa_bias, nb, seq)
    dproj, small, grad_x, d_g1 = _dproj_dx(pre, dqa, dka, dqb, dkb, dva, dvb, z, x2, dh, gain_row, w_pad_t, g1, seq)
    d_w_in_t = _dwin(dproj, xn)

    d_w_out = _unpad_head_rows(d_wo, A_HEADS + B_HEADS)
    g_att = jnp.concatenate([
        jnp.pad(_in_rows_unpad(d_w_in_t).reshape(N_CHIPS, IN_SHARD, D_MODEL),
                ((0, 0), (0, IN_SHARD_P - IN_SHARD), (0, 0))),
        d_w_out.reshape(N_CHIPS, D_MODEL // N_CHIPS, D_MODEL)], axis=1)
    g_att = jnp.stack([g_att[:, :R_ATT // 2], g_att[:, R_ATT // 2:]])
    pair_a, pair_a_bf = _rs_pair_add(g_att, _rs_pair_exchange(g_att, "rs_pair_exchange_att"), c_arg, "rs_pair_add_att")
    got_a = _rs_chip_exchange(pair_a_bf, "rs_chip_exchange_att")
    red_a = _rs_pair_share(_rs_chip_add(pair_a, got_a, sc_arg, "rs_chip_add_att"), "rs_pair_share_att")
    red_a = red_a.reshape(R_ATT, D_MODEL)
    g_w_in = red_a[:IN_SHARD].T
    g_w_out = red_a[IN_SHARD_P:]

    loss_row = loss_acc[0:1] * (0.5 / D_MODEL)
    d_sink = dsink[:, :A_GROUP, 0].reshape(nb, A_HEADS).sum(axis=0)
    part = _pack_small(d_g1[0], small[4, :B_HEADS], small[0, :HEAD_DIM], small[1, :HEAD_DIM], d_sink,
                       small[2, :HEAD_DIM], small[3, :HEAD_DIM], d_g2[0], loss_row)
    zero_row = jnp.zeros((1, LANES), F32)
    smalls = lambda t: _pack_small(*t, zero_row)
    w_small = smalls((attn_norm_g, b_forget, q_norm_a, k_norm_a, sink_logits, q_norm_b, k_norm_b, mlp_norm_g))
    m_small = smalls((m_attn_norm_g, m_b_forget, m_q_norm_a, m_k_norm_a, m_sink_logits, m_q_norm_b, m_k_norm_b,
                      m_mlp_norm_g))
    v_small = smalls((v_attn_norm_g, v_b_forget, v_q_norm_a, v_k_norm_a, v_sink_logits, v_q_norm_b, v_k_norm_b,
                      v_mlp_norm_g))
    (g_s, d_s, m_s, v_s), loss = _small_allreduce_adamw(part, w_small, m_small, v_small)

    big = {}
    for name, w, g, m, v in (("adamw_w_in", w_in, g_w_in, m_w_in, v_w_in),
                             ("adamw_w_out", w_out, g_w_out, m_w_out, v_w_out),
                             ("adamw_w_up", w_up, g_w_up, m_w_up, v_w_up),
                             ("adamw_w_down", w_down, g_w_down, m_w_down, v_w_down)):
        big[name] = (g,) + tuple(_adamw(w, g, m, v, name))

    def assemble(k, s):
        return (s[0], big["adamw_w_in"][k], s[1], s[2], s[3], s[4], s[5], s[6], big["adamw_w_out"][k], s[7],
                big["adamw_w_up"][k], big["adamw_w_down"][k])

    return (loss, grad_x.reshape(nb, seq, D_MODEL), *assemble(0, g_s), *assemble(1, d_s), *assemble(2, m_s),
            *assemble(3, v_s))
```

```python
import functools

import numpy as np
import jax
import jax.numpy as jnp
from jax import lax
from jax.experimental import pallas as pl
from jax.experimental.pallas import tpu as pltpu

F32 = jnp.float32
BF16 = jnp.bfloat16

D_MODEL = 1024
HEAD_DIM = 64
LANES = 128
A_HEADS = 8
A_KV_HEADS = 2
A_GROUP = A_HEADS // A_KV_HEADS
B_HEADS = 8
WINDOW = 128
D_FF = 4096
IN_WIDTH = 2312
EPS = 1e-6
SCALE = 0.125
LOG2E = 1.4426950408889634
LN2 = 0.6931471805599453
CHUNK = 32
FOX_TK = 512
FOX_PARTS = 8
FOX_PARTS_BWD = 4
NEG = -1e30

G_QA, G_KA, G_QB, G_KB, G_VA, G_VB, G_F = 0, 8, 10, 18, 26, 28, 36
N_NORM_GROUPS = 26
N_GROUPS = 38
NP = N_GROUPS * LANES
MIXED_P = (A_HEADS + B_HEADS) * LANES

N_CHIPS = 4
IN_SHARD = IN_WIDTH // N_CHIPS
IN_SHARD_P = 608
R_ATT = IN_SHARD_P + D_MODEL // N_CHIPS

SMALL_ROWS = 24
ROW_LOSS = 22

ADAM_LR = 0.001
ADAM_B1 = 0.9
ADAM_B2 = 0.999
ADAM_EPS = 1e-08
ADAM_WD = 0.01
ADAM_STEP = 10

VMEM_LIMIT = 52 * 1024 * 1024
MESH = pl.DeviceIdType.MESH


def _pcall(body, **kw):
    return pl.pallas_call(body, **kw)


def _params(sem=None):
    return pltpu.CompilerParams(dimension_semantics=sem, vmem_limit_bytes=VMEM_LIMIT)


def _dot(a, b):
    return jnp.dot(a, b, preferred_element_type=F32)


def _dot_nt(a, b):
    return lax.dot_general(a, b, (((1,), (1,)), ((), ())), preferred_element_type=F32)


def _dot_tn(a, b):
    return lax.dot_general(a, b, (((0,), (0,)), ((), ())), preferred_element_type=F32)


def _split3(x):
    hi = x.astype(BF16)
    r1 = x - hi.astype(F32)
    mid = r1.astype(BF16)
    lo = (r1 - mid.astype(F32)).astype(BF16)
    return hi, mid, lo


def _dot_exact(mat, x):
    hi, mid, lo = _split3(x)
    return _dot(mat, lo) + _dot(mat, mid) + _dot(mat, hi)


def _const(shape):
    zeros = (0,) * len(shape)
    return pl.BlockSpec(shape, lambda *_: zeros)


def _rows(tm, n):
    return pl.BlockSpec((tm, n), lambda i: (i, 0))


def _aug_select():
    e = np.zeros((3 * LANES, 2 * B_HEADS * LANES), np.float32)
    for j in range(3):
        for h in range(B_HEADS):
            e[j * LANES + h, h * LANES + HEAD_DIM + j] = 1.0
            e[j * LANES + h, (B_HEADS + h) * LANES + HEAD_DIM + 3 + j] = -1.0
    return jnp.asarray(e, BF16)


def _dc_select():
    e = np.zeros((2 * B_HEADS * LANES, LANES), np.float32)
    for h in range(B_HEADS):
        e[h * LANES + HEAD_DIM, h] = 1.0
        e[(B_HEADS + h) * LANES + HEAD_DIM + 3, h] = -1.0
    return jnp.asarray(e, BF16)


def _tri(n, upper):
    t = np.tril(np.ones((n, n), np.float32))
    return jnp.asarray(t.T if upper else t, BF16)


def _inproj(x2, g1, w_pad_t, gain_row, b_row, seq, later_weights):
    t_all = x2.shape[0]
    tm = min(256, seq)
    n_steps = t_all // tm
    forward_step = max(n_steps - 2, 0)
    tiles_per_seq = seq // tm
    tri = _tri(tm, False)
    esel = _aug_select()
    n_later = len(later_weights)

    def body(x_ref, g_ref, w_ref, gain_ref, b_ref, tri_ref, e_ref, *rest):
        later_src, rest = rest[:n_later], rest[n_later:]
        xn_ref, pre_ref, qa_ref, ka_ref, va_ref, qb_ref, kb_ref, vb_ref, z_ref = rest[:9]
        later_out, (carry_ref, send_sems, recv_sems, local_sems) = rest[9:9 + n_later], rest[9 + n_later:]
        i = pl.program_id(0)
        gather = _Gather(later_src, later_out, send_sems, recv_sems, local_sems)

        @pl.when(i == 0)
        def _():
            gather.start()

        @pl.when(i == forward_step)
        def _():
            gather.forward()

        @pl.when(i % tiles_per_seq == 0)
        def _():
            carry_ref[...] = jnp.zeros_like(carry_ref)

        x = x_ref[...]
        r = lax.rsqrt(jnp.mean(x * x, axis=-1, keepdims=True) + EPS)
        xn = (x * r * g_ref[...]).astype(BF16)
        xn_ref[...] = xn
        proj = _dot_nt(xn, w_ref[...])
        pre_ref[...] = proj[:, :N_NORM_GROUPS * LANES].astype(BF16)
        lane = lax.broadcasted_iota(jnp.int32, (tm, LANES), 1)

        z = proj[:, G_F * LANES:(G_F + 1) * LANES] + b_ref[...]
        z_ref[...] = z
        lf = jnp.minimum(z, 0.0) - jnp.log(1.0 + jnp.exp(-jnp.abs(z)))
        lf = jnp.where(lane < B_HEADS, lf, 0.0)
        c = _dot_exact(tri_ref[...], lf) + carry_ref[...]
        carry_ref[...] += jnp.sum(lf, axis=0, keepdims=True)
        aug = _dot(jnp.concatenate(_split3(c * LOG2E), axis=1), e_ref[...])

        def hnorm(g):
            p = proj[:, g * LANES:(g + 1) * LANES]
            rr = lax.rsqrt(jnp.sum(p * p, axis=-1, keepdims=True) * (1.0 / HEAD_DIM) + EPS)
            return p * rr * gain_ref[:, g * LANES:(g + 1) * LANES]

        ones_q = jnp.where((lane >= HEAD_DIM + 3) & (lane < HEAD_DIM + 6), 1.0, 0.0)
        ones_k = jnp.where((lane >= HEAD_DIM) & (lane < HEAD_DIM + 3), 1.0, 0.0)
        for h in range(A_HEADS):
            qa_ref[:, h * LANES:(h + 1) * LANES] = (hnorm(G_QA + h) * SCALE).astype(BF16)
        for h in range(A_KV_HEADS):
            ka_ref[:, h * LANES:(h + 1) * LANES] = hnorm(G_KA + h).astype(BF16)
        for h in range(B_HEADS):
            qb_ref[:, h * LANES:(h + 1) * LANES] = (
                hnorm(G_QB + h) * (SCALE * LOG2E) + aug[:, h * LANES:(h + 1) * LANES] + ones_q).astype(BF16)
            kb_ref[:, h * LANES:(h + 1) * LANES] = (
                hnorm(G_KB + h) + aug[:, (B_HEADS + h) * LANES:(B_HEADS + h + 1) * LANES] + ones_k).astype(BF16)
        va_ref[...] = proj[:, G_VA * LANES:G_VB * LANES].astype(BF16)
        one_v = jnp.where(lane == HEAD_DIM, 1.0, 0.0)
        for h in range(B_HEADS):
            cols = slice((G_VB + h) * LANES, (G_VB + h + 1) * LANES)
            vb_ref[:, h * LANES:(h + 1) * LANES] = (proj[:, cols] + one_v).astype(BF16)

        @pl.when(i == n_steps - 1)
        def _():
            gather.finish()

    widths = [(D_MODEL, BF16), (N_NORM_GROUPS * LANES, BF16), (A_HEADS * LANES, BF16), (A_KV_HEADS * LANES, BF16),
              (A_KV_HEADS * LANES, BF16), (B_HEADS * LANES, BF16), (B_HEADS * LANES, BF16), (B_HEADS * LANES, BF16),
              (LANES, F32)]
    res = _pcall(
        body, name="inproj", grid=(n_steps,),
        in_specs=[_rows(tm, D_MODEL), _const((1, D_MODEL)), _const((NP, D_MODEL)), _const((1, NP)),
                  _const((1, LANES)), _const((tm, tm)), _const(esel.shape)] + [ANY] * n_later,
        out_specs=[_rows(tm, w) for w, _ in widths] + [ANY] * n_later,
        out_shape=[jax.ShapeDtypeStruct((t_all, w), dt) for w, dt in widths]
        + [jax.ShapeDtypeStruct((8 * w.shape[0], w.shape[1]), w.dtype) for w in later_weights],
        scratch_shapes=[pltpu.VMEM((1, LANES), F32)] + _gather_scratch(n_later),
        compiler_params=_params(("arbitrary",)),
    )(x2, g1, w_pad_t, gain_row, b_row, tri, esel, *later_weights)
    return res[:9], res[9:]


def _fox_fwd(qb, kb, vb, nb, seq):
    t_all = qb.shape[0]
    tk = min(FOX_TK, seq // FOX_PARTS)
    tq = FOX_PARTS * tk
    nq = seq // tq

    def body(q_ref, k_ref, v_ref, o_ref, lse_ref, s_ref, p_ref, m_ref, alpha_ref, acc_ref):
        qi = pl.program_id(2)
        q = q_ref[...]
        m_ref[...] = jnp.full((tq, LANES), NEG, F32)
        acc_ref[...] = jnp.zeros((tq, LANES), F32)

        def step(j, modes):
            off = pl.multiple_of(j * tk, tk)
            k = k_ref[pl.ds(off, tk), :]
            v = v_ref[pl.ds(off, tk), :]
            live = [hf for hf in range(FOX_PARTS) if modes[hf] is not None]
            for hf in live:
                s_ref[hf] = _dot_nt(q[hf * tk:(hf + 1) * tk], k)
            for hf in live:
                for r in range(0, tk, CHUNK):
                    rows = slice(r, r + CHUNK)
                    grows = slice(hf * tk + r, hf * tk + r + CHUNK)
                    tiles = []
                    for jt in range(tk // LANES):
                        sc = s_ref[hf, rows, jt * LANES:(jt + 1) * LANES]
                        if modes[hf] == "diag":
                            row = r + lax.broadcasted_iota(jnp.int32, (CHUNK, LANES), 0)
                            col = jt * LANES + lax.broadcasted_iota(jnp.int32, (CHUNK, LANES), 1)
                            sc = jnp.where(row >= col, sc, NEG)
                        tiles.append(sc)
                    m_prev = m_ref[grows, :]
                    m_cur = functools.reduce(jnp.maximum, tiles)
                    m_new = jnp.maximum(m_prev, jnp.max(m_cur, axis=-1, keepdims=True))
                    m_ref[grows, :] = m_new
                    alpha_ref[grows, :] = jnp.exp2(m_prev - m_new)
                    for jt, sc in enumerate(tiles):
                        p_ref[hf, rows, jt * LANES:(jt + 1) * LANES] = jnp.exp2(sc - m_new).astype(BF16)
                hrows = slice(hf * tk, (hf + 1) * tk)
                acc_ref[hrows, :] = alpha_ref[hrows, :] * acc_ref[hrows, :] + _dot(p_ref[hf], v)

        def past(j, carry):
            step(j, ("full",) * FOX_PARTS)
            return carry

        lax.fori_loop(0, FOX_PARTS * qi, past, 0)
        for d in range(FOX_PARTS):
            step(FOX_PARTS * qi + d, (None,) * d + ("diag",) + ("full",) * (FOX_PARTS - 1 - d))
        acc = acc_ref[...]
        lane = lax.broadcasted_iota(jnp.int32, (tq, LANES), 1)
        l = jnp.sum(jnp.where(lane == HEAD_DIM, acc, 0.0), axis=-1, keepdims=True)
        o_ref[...] = (acc / l).astype(BF16)
        lse_ref[...] = m_ref[...] + jnp.log2(l)

    qspec = pl.BlockSpec((tq, LANES), lambda b, h, i: (b * nq + i, h))
    kspec = pl.BlockSpec((seq, LANES), lambda b, h, i: (b, h))
    return _pcall(
        body, name="fox_fwd", grid=(nb, B_HEADS, nq),
        in_specs=[qspec, kspec, kspec], out_specs=[qspec, qspec],
        out_shape=[jax.ShapeDtypeStruct((t_all, B_HEADS * LANES), BF16),
                   jax.ShapeDtypeStruct((t_all, B_HEADS * LANES), F32)],
        scratch_shapes=[pltpu.VMEM((FOX_PARTS, tk, tk), F32), pltpu.VMEM((FOX_PARTS, tk, tk), BF16),
                        pltpu.VMEM((tq, LANES), F32),
                        pltpu.VMEM((tq, LANES), F32), pltpu.VMEM((tq, LANES), F32)],
        compiler_params=_params(("parallel", "parallel", "arbitrary")),
    )(qb, kb, vb)


def _swa_bias(slopes):
    row = jnp.arange(A_GROUP * WINDOW, dtype=jnp.int32)[:, None] % WINDOW
    col = jnp.arange(2 * WINDOW, dtype=jnp.int32)[None, :]
    slope_rows = jnp.repeat(slopes.reshape(A_KV_HEADS, A_GROUP), WINDOW, axis=1)[:, :, None]
    out = []
    for t_rel in (0, WINDOW):
        dist = t_rel + row - col
        valid = (dist >= 0) & (dist < WINDOW)
        out.append(jnp.where(valid[None], -slope_rows * dist.astype(F32)[None], NEG))
    return jnp.stack(out)


def _stack_heads(ref, rows):
    return jnp.concatenate([ref[rows, j * LANES:(j + 1) * LANES] for j in range(A_GROUP)], axis=0)


def _sink_rows(sink_ref, g):
    return jnp.concatenate([jnp.full((WINDOW, LANES), sink_ref[g * A_GROUP + j], F32) for j in range(A_GROUP)], axis=0)


def _rep(col):
    return jnp.broadcast_to(col, (col.shape[0], LANES))


def _swa_specs(nq, tq, seq):
    smem = pl.BlockSpec(memory_space=pltpu.SMEM)
    qspec = pl.BlockSpec((tq, A_GROUP * LANES), lambda b, g, i: (b * nq + i, g))
    kspec = pl.BlockSpec((seq, LANES), lambda b, g, i: (b, g))
    bias_first = pl.BlockSpec((None, None, A_GROUP * WINDOW, 2 * WINDOW),
                              lambda b, g, i: (jnp.minimum(i, 1), g, 0, 0))
    bias_rest = pl.BlockSpec((None, None, A_GROUP * WINDOW, 2 * WINDOW), lambda b, g, i: (1, g, 0, 0))
    return smem, qspec, kspec, bias_first, bias_rest


def _swa_fwd(qa, ka, va, sinks, bias, nb, seq):
    t_all = qa.shape[0]
    tq = min(512, seq)
    nq = seq // tq

    def body(sink_ref, q_ref, k_ref, v_ref, bias0_ref, bias_ref, o_ref, l_ref):
        qi = pl.program_id(2)
        sink = _sink_rows(sink_ref, pl.program_id(1))
        for a in range(tq // WINDOW):
            t0 = qi * tq + a * WINDOW
            start = pl.multiple_of(jnp.maximum(t0 - WINDOW, 0), WINDOW)
            rows = slice(a * WINDOW, (a + 1) * WINDOW)
            k = k_ref[pl.ds(start, 2 * WINDOW), :]
            v = v_ref[pl.ds(start, 2 * WINDOW), :]
            s = _dot_nt(_stack_heads(q_ref, rows), k) + (bias0_ref if a == 0 else bias_ref)[...]
            s0, s1 = s[:, :LANES], s[:, LANES:]
            m = jnp.maximum(_rep(jnp.max(jnp.maximum(s0, s1), axis=-1, keepdims=True)), sink)
            p0, p1 = jnp.exp(s0 - m), jnp.exp(s1 - m)
            den = _rep(jnp.sum(p0 + p1, axis=-1, keepdims=True)) + jnp.exp(sink - m)
            inv = 1.0 / den
            o = _dot(jnp.concatenate([(p0 * inv).astype(BF16), (p1 * inv).astype(BF16)], axis=1), v).astype(BF16)
            lrow = m + jnp.log(den)
            for j in range(A_GROUP):
                o_ref[rows, j * LANES:(j + 1) * LANES] = o[j * WINDOW:(j + 1) * WINDOW]
                l_ref[rows, j * LANES:(j + 1) * LANES] = lrow[j * WINDOW:(j + 1) * WINDOW]

    smem, qspec, kspec, bias_first, bias_rest = _swa_specs(nq, tq, seq)
    return _pcall(
        body, name="swa_fwd", grid=(nb, A_KV_HEADS, nq),
        in_specs=[smem, qspec, kspec, kspec, bias_first, bias_rest], out_specs=[qspec, qspec],
        out_shape=[jax.ShapeDtypeStruct((t_all, A_HEADS * LANES), BF16),
                   jax.ShapeDtypeStruct((t_all, A_HEADS * LANES), F32)],
        compiler_params=_params(("parallel", "parallel", "arbitrary")),
    )(sinks, qa, ka, va, bias, bias)


def _outproj(x2, oa, ob, wo_pad, g2):
    t_all = x2.shape[0]
    tm = min(512, t_all)
    half = A_HEADS * LANES

    def body(x_ref, oa_ref, ob_ref, w_ref, g_ref, h_ref, hn_ref):
        h = x_ref[...] + _dot(oa_ref[...], w_ref[:half, :]) + _dot(ob_ref[...], w_ref[half:, :])
        h_ref[...] = h
        r = lax.rsqrt(jnp.mean(h * h, axis=-1, keepdims=True) + EPS)
        hn_ref[...] = (h * r * g_ref[...]).astype(BF16)

    return _pcall(
        body, name="outproj", grid=(t_all // tm,),
        in_specs=[_rows(tm, D_MODEL), _rows(tm, half), _rows(tm, half), _const((MIXED_P, D_MODEL)),
                  _const((1, D_MODEL))],
        out_specs=[_rows(tm, D_MODEL), _rows(tm, D_MODEL)],
        out_shape=[jax.ShapeDtypeStruct((t_all, D_MODEL), F32), jax.ShapeDtypeStruct((t_all, D_MODEL), BF16)],
        compiler_params=_params(("parallel",)),
    )(x2, oa, ob, wo_pad, g2)


def _mlp_fwd(hn, w_up_blocks, w_down, h, tgt):
    t_all = h.shape[0]
    tm = min(256, t_all)
    nj = D_FF // D_MODEL

    def body(a_ref, wu_ref, wd_ref, h_ref, t_ref, ru_ref, dy_ref, dyb_ref, loss_ref):
        @pl.when(pl.program_id(0) == 0)
        def _():
            loss_ref[...] = jnp.zeros_like(loss_ref)

        a = a_ref[...]
        y = h_ref[...]
        for j in range(nj):
            cols = slice(j * D_MODEL, (j + 1) * D_MODEL)
            ru = jnp.maximum(_dot(a, wu_ref[j]), 0.0)
            ru_ref[:, cols] = ru.astype(BF16)
            y = y + _dot((ru * ru).astype(BF16), wd_ref[cols, :])
        err = y - t_ref[...]
        loss_ref[...] += jnp.sum(err * err)
        dy = err * (1.0 / D_MODEL)
        dy_ref[...] = dy
        dyb_ref[...] = dy.astype(BF16)

    return _pcall(
        body, name="mlp_fwd", grid=(t_all // tm,),
        in_specs=[_rows(tm, D_MODEL), _const((nj, D_MODEL, D_MODEL)), _const((D_FF, D_MODEL)), _rows(tm, D_MODEL),
                  _rows(tm, D_MODEL)],
        out_specs=[_rows(tm, D_FF), _rows(tm, D_MODEL), _rows(tm, D_MODEL), _const((8, LANES))],
        out_shape=[jax.ShapeDtypeStruct((t_all, D_FF), BF16), jax.ShapeDtypeStruct((t_all, D_MODEL), F32),
                   jax.ShapeDtypeStruct((t_all, D_MODEL), BF16), jax.ShapeDtypeStruct((8, LANES), F32)],
        compiler_params=_params(("arbitrary",)),
    )(hn, w_up_blocks, w_down, h, tgt)


def _mlp_bwd_w(dyb, w_down, ru, hn):
    t_all = dyb.shape[0]
    tm = min(512, t_all)
    nj = D_FF // D_MODEL

    def body(dy_ref, w_ref, ru_ref, hn_ref, du_ref, dw_ref):
        @pl.when(pl.program_id(1) == 0)
        def _():
            dw_ref[...] = jnp.zeros_like(dw_ref)

        dy = dy_ref[...]
        ru = ru_ref[...].astype(F32)
        du = (_dot_nt(dy, w_ref[...]) * (2.0 * ru)).astype(BF16)
        du_ref[...] = du
        dw_ref[0] += _dot_tn(hn_ref[...], du)
        dw_ref[1] += _dot_tn((ru * ru).astype(BF16), dy)

    tok = pl.BlockSpec((tm, D_MODEL), lambda j, i: (i, 0))
    blk = pl.BlockSpec((tm, D_MODEL), lambda j, i: (i, j))
    wspec = pl.BlockSpec((2, None, D_MODEL, D_MODEL), lambda j, i: (0, j, 0, 0))
    return _pcall(
        body, name="mlp_bwd_w", grid=(nj, t_all // tm),
        in_specs=[tok, pl.BlockSpec((D_MODEL, D_MODEL), lambda j, i: (j, 0)), blk, tok],
        out_specs=[blk, wspec],
        out_shape=[jax.ShapeDtypeStruct((t_all, D_FF), BF16), jax.ShapeDtypeStruct((2, nj, D_MODEL, D_MODEL), F32)],
        compiler_params=_params(("parallel", "arbitrary")),
    )(dyb, w_down, ru, hn)


def _pair_exchange_copy(g_ref, out_ref, send_sem, recv_sem):
    x, y, c = _place()
    return pltpu.make_async_remote_copy(
        src_ref=g_ref.at[1 - c], dst_ref=out_ref, send_sem=send_sem, recv_sem=recv_sem,
        device_id=(x, y, 1 - c), device_id_type=MESH)


def _mlp_dhn(du, w_up_blocks, h, dy, g2, d_w_mlp):
    t_all = h.shape[0]
    tm = min(256, t_all)
    n_steps = t_all // tm

    def body(a_ref, w_ref, h_ref, dy_ref, g_ref, dw_ref, dh_ref, dhb_ref, dg_ref, got_ref, send_sem, recv_sem):
        @pl.when(pl.program_id(0) == 0)
        def _():
            dg_ref[...] = jnp.zeros_like(dg_ref)
            _pair_exchange_copy(dw_ref, got_ref, send_sem, recv_sem).start()

        dhn = _dot_nt(a_ref[:, :D_MODEL], w_ref[0])
        for j in range(1, D_FF // D_MODEL):
            dhn = dhn + _dot_nt(a_ref[:, j * D_MODEL:(j + 1) * D_MODEL], w_ref[j])
        h = h_ref[...]
        r = lax.rsqrt(jnp.mean(h * h, axis=-1, keepdims=True) + EPS)
        hh = h * r
        dg_ref[...] += jnp.sum(dhn * hh, axis=0, keepdims=True)
        dz = dhn * g_ref[...]
        dh = dy_ref[...] + r * (dz - hh * jnp.mean(dz * hh, axis=-1, keepdims=True))
        dh_ref[...] = dh
        dhb_ref[...] = dh.astype(BF16)

        @pl.when(pl.program_id(0) == n_steps - 1)
        def _():
            _pair_exchange_copy(dw_ref, got_ref, send_sem, recv_sem).wait()

    return _pcall(
        body, name="mlp_dhn", grid=(n_steps,),
        in_specs=[_rows(tm, D_FF), _const((D_FF // D_MODEL, D_MODEL, D_MODEL)), _rows(tm, D_MODEL),
                  _rows(tm, D_MODEL), _const((1, D_MODEL)), ANY],
        out_specs=[_rows(tm, D_MODEL), _rows(tm, D_MODEL), _const((1, D_MODEL)), ANY],
        out_shape=[jax.ShapeDtypeStruct((t_all, D_MODEL), F32), jax.ShapeDtypeStruct((t_all, D_MODEL), BF16),
                   jax.ShapeDtypeStruct((1, D_MODEL), F32), jax.ShapeDtypeStruct(d_w_mlp.shape[1:], F32)],
        scratch_shapes=[pltpu.SemaphoreType.DMA, pltpu.SemaphoreType.DMA],
        compiler_params=_params(("arbitrary",)),
    )(du, w_up_blocks, h, dy, g2, d_w_mlp)


def _dmixed(dhb, wo_pad, oa, ob):
    t_all = dhb.shape[0]
    tm = min(512, t_all)
    half = A_HEADS * LANES

    def body(a_ref, w_ref, oa_ref, ob_ref, da_ref, db_ref, delta_ref, dwo_ref):
        @pl.when(pl.program_id(0) == 0)
        def _():
            dwo_ref[...] = jnp.zeros_like(dwo_ref)

        a = a_ref[...]
        d = _dot_nt(a, w_ref[...])
        da_ref[...] = d[:, :half].astype(BF16)
        db_ref[...] = d[:, half:].astype(BF16)
        for h in range(B_HEADS):
            cols = slice(h * LANES, (h + 1) * LANES)
            prod = d[:, half + h * LANES:half + (h + 1) * LANES] * ob_ref[:, cols].astype(F32)
            delta_ref[:, cols] = jnp.broadcast_to(jnp.sum(prod, axis=-1, keepdims=True), (tm, LANES))
        dwo_ref[:half, :] += _dot_tn(oa_ref[...], a)
        dwo_ref[half:, :] += _dot_tn(ob_ref[...], a)

    return _pcall(
        body, name="dmixed", grid=(t_all // tm,),
        in_specs=[_rows(tm, D_MODEL), _const((MIXED_P, D_MODEL)), _rows(tm, half), _rows(tm, half)],
        out_specs=[_rows(tm, half), _rows(tm, half), _rows(tm, half), _const((MIXED_P, D_MODEL))],
        out_shape=[jax.ShapeDtypeStruct((t_all, half), BF16), jax.ShapeDtypeStruct((t_all, half), BF16),
                   jax.ShapeDtypeStruct((t_all, half), F32), jax.ShapeDtypeStruct((MIXED_P, D_MODEL), F32)],
        compiler_params=_params(("arbitrary",)),
    )(dhb, wo_pad, oa, ob)


def _fox_bwd(qb, kb, vb, dob, lse, delta, nb, seq, pair_sums):
    t_all = qb.shape[0]
    tk = min(FOX_TK, seq // FOX_PARTS_BWD)
    tq = FOX_PARTS_BWD * tk
    nk = seq // tk

    def body(q_ref, k_ref, v_ref, do_ref, lse_ref, delta_ref, pair_ref, dq_ref, dk_ref, dv_ref, got_ref,
             s_ref, dp_ref, p_ref, ds_ref, dk_acc, dv_acc, send_sems, recv_sems):
        kj = pl.program_id(2)
        bh = pl.program_id(0) * B_HEADS + pl.program_id(1)

        @pl.when((bh == 0) & (kj == 0))
        def _():
            for cp in _chip_exchange_copies(pair_ref, got_ref, send_sems, recv_sems):
                cp.start()

        @pl.when(kj == 0)
        def _():
            dq_ref[...] = jnp.zeros_like(dq_ref)

        dk_acc[...] = jnp.zeros_like(dk_acc)
        dv_acc[...] = jnp.zeros_like(dv_acc)
        k = k_ref[...]
        v = v_ref[...]

        def block(off, r0, r1, masked):
            qrows = pl.ds(pl.multiple_of(off + r0, CHUNK), r1 - r0)
            q = q_ref[qrows, :]
            do = do_ref[qrows, :]
            s_ref[r0:r1, :] = _dot_nt(q, k)
            dp_ref[r0:r1, :] = _dot_nt(do, v)
            for r in range(r0, r1, CHUNK):
                rows = slice(r, r + CHUNK)
                chunk = pl.ds(pl.multiple_of(off + r, CHUNK), CHUNK)
                lse_c = lse_ref[chunk, :]
                delta_c = delta_ref[chunk, :]
                for jt in range(tk // LANES):
                    cols = slice(jt * LANES, (jt + 1) * LANES)
                    p = jnp.exp2(s_ref[rows, cols] - lse_c)
                    if masked:
                        row = r - r0 + lax.broadcasted_iota(jnp.int32, (CHUNK, LANES), 0)
                        col = jt * LANES + lax.broadcasted_iota(jnp.int32, (CHUNK, LANES), 1)
                        p = jnp.where(row >= col, p, 0.0)
                    p_ref[rows, cols] = p.astype(BF16)
                    ds_ref[rows, cols] = (p * (dp_ref[rows, cols] - delta_c)).astype(BF16)
            dv_acc[...] += _dot_tn(p_ref[r0:r1, :], do)
            dk_acc[...] += _dot_tn(ds_ref[r0:r1, :], q)
            dq_ref[qrows, :] += _dot(ds_ref[r0:r1, :], k)

        first = kj // FOX_PARTS_BWD
        off_first = pl.multiple_of(first * tq, tq)
        for d in range(FOX_PARTS_BWD):
            @pl.when(kj % FOX_PARTS_BWD == d)
            def _(d=d):
                block(off_first, d * tk, (d + 1) * tk, True)
                if d < FOX_PARTS_BWD - 1:
                    block(off_first, (d + 1) * tk, tq, False)

        def later(i, carry):
            block(pl.multiple_of(i * tq, tq), 0, tq, False)
            return carry

        lax.fori_loop(first + 1, seq // tq, later, 0)
        dk_ref[...] = dk_acc[...]
        dv_ref[...] = dv_acc[...]

        @pl.when((bh == nb * B_HEADS - 1) & (kj == nk - 1))
        def _():
            for cp in _chip_exchange_copies(pair_ref, got_ref, send_sems, recv_sems):
                cp.wait()

    full = pl.BlockSpec((seq, LANES), lambda b, h, j: (b, h))
    tile = pl.BlockSpec((tk, LANES), lambda b, h, j: (b * nk + j, h))
    shp = jax.ShapeDtypeStruct((t_all, B_HEADS * LANES), F32)
    return _pcall(
        body, name="fox_bwd", grid=(nb, B_HEADS, nk),
        in_specs=[full, tile, tile, full, full, full, ANY], out_specs=[full, tile, tile, ANY],
        out_shape=[shp, shp, shp, jax.ShapeDtypeStruct((3,) + pair_sums.shape[1:], pair_sums.dtype)],
        scratch_shapes=[pltpu.VMEM((tq, tk), F32), pltpu.VMEM((tq, tk), F32), pltpu.VMEM((tq, tk), BF16),
                        pltpu.VMEM((tq, tk), BF16), pltpu.VMEM((tk, LANES), F32), pltpu.VMEM((tk, LANES), F32),
                        pltpu.SemaphoreType.DMA((3,)), pltpu.SemaphoreType.DMA((3,))],
        compiler_params=_params(("arbitrary", "arbitrary", "arbitrary")),
    )(qb, kb, vb, dob, lse, delta, pair_sums)


def _swa_bwd(qa, ka, va, oa, doa, lrow, sinks, bias, nb, seq):
    t_all = qa.shape[0]
    tq = min(512, seq)
    nq = seq // tq

    def body(sink_ref, q_ref, k_ref, v_ref, bias0_ref, bias_ref, o_ref, do_ref, l_ref,
             dq_ref, dk_ref, dv_ref, dsink_ref):
        qi = pl.program_id(2)
        sink = _sink_rows(sink_ref, pl.program_id(1))

        @pl.when(qi == 0)
        def _():
            dk_ref[...] = jnp.zeros_like(dk_ref)
            dv_ref[...] = jnp.zeros_like(dv_ref)
            dsink_ref[...] = jnp.zeros_like(dsink_ref)

        for a in range(tq // WINDOW):
            t0 = qi * tq + a * WINDOW
            start = pl.multiple_of(jnp.maximum(t0 - WINDOW, 0), WINDOW)
            rows = slice(a * WINDOW, (a + 1) * WINDOW)
            win = pl.ds(start, 2 * WINDOW)
            q = _stack_heads(q_ref, rows)
            k = k_ref[win, :]
            v = v_ref[win, :]
            do = _stack_heads(do_ref, rows)
            lrow = _stack_heads(l_ref, rows)
            s = _dot_nt(q, k) + (bias0_ref if a == 0 else bias_ref)[...]
            dp = _dot_nt(do, v)
            delta = _rep(jnp.sum(do.astype(F32) * _stack_heads(o_ref, rows).astype(F32), axis=-1, keepdims=True))
            p = [jnp.exp(s[:, t * LANES:(t + 1) * LANES] - lrow) for t in range(2)]
            ds = jnp.concatenate([(p[t] * (dp[:, t * LANES:(t + 1) * LANES] - delta)).astype(BF16) for t in range(2)],
                                 axis=1)
            dq = _dot(ds, k)
            dk_ref[win, :] += _dot_tn(ds, q)
            dv_ref[win, :] += _dot_tn(jnp.concatenate([p[0].astype(BF16), p[1].astype(BF16)], axis=1), do)
            sink_term = jnp.exp(sink - lrow) * delta
            for j in range(A_GROUP):
                part = slice(j * WINDOW, (j + 1) * WINDOW)
                dq_ref[rows, j * LANES:(j + 1) * LANES] = dq[part]
                dsink_ref[j:j + 1, :] -= jnp.sum(sink_term[part], axis=0, keepdims=True)

    smem, qspec, kspec, bias_first, bias_rest = _swa_specs(nq, tq, seq)
    return _pcall(
        body, name="swa_bwd", grid=(nb, A_KV_HEADS, nq),
        in_specs=[smem, qspec, kspec, kspec, bias_first, bias_rest, qspec, qspec, qspec],
        out_specs=[qspec, kspec, kspec, pl.BlockSpec((None, 8, LANES), lambda b, g, i: (b * A_KV_HEADS + g, 0, 0))],
        out_shape=[jax.ShapeDtypeStruct((t_all, A_HEADS * LANES), F32),
                   jax.ShapeDtypeStruct((t_all, A_KV_HEADS * LANES), F32),
                   jax.ShapeDtypeStruct((t_all, A_KV_HEADS * LANES), F32),
                   jax.ShapeDtypeStruct((nb * A_KV_HEADS, 8, LANES), F32)],
        compiler_params=_params(("parallel", "parallel", "arbitrary")),
    )(sinks, qa, ka, va, bias, bias, oa, doa, lrow)


def _dproj_dx(pre, dqa, dka, dqb, dkb, dva, dvb, z, x2, dh, gain_row, w_pad_t, g1, seq):
    t_all = pre.shape[0]
    tm = min(256, seq)
    nt = t_all // tm
    tiles_per_seq = seq // tm
    triu = _tri(tm, True)
    sel = _dc_select()

    def body(pre_ref, dqa_ref, dka_ref, dqb_ref, dkb_ref, dva_ref, dvb_ref, z_ref, x_ref, dh_ref, gain_ref, triu_ref,
             sel_ref, w_ref, g_ref, dproj_ref, small_ref, dx_ref, dg_ref, carry_ref):
        i = pl.program_id(0)

        @pl.when(i == 0)
        def _():
            small_ref[...] = jnp.zeros_like(small_ref)
            dg_ref[...] = jnp.zeros_like(dg_ref)

        @pl.when(i % tiles_per_seq == 0)
        def _():
            carry_ref[...] = jnp.zeros_like(carry_ref)

        def norm_bwd(g, dhat):
            cols = slice(g * LANES, (g + 1) * LANES)
            p = pre_ref[:, cols].astype(F32)
            rr = lax.rsqrt(jnp.sum(p * p, axis=-1, keepdims=True) * (1.0 / HEAD_DIM) + EPS)
            n = p * rr
            dz = dhat * gain_ref[:, cols]
            dproj_ref[:, cols] = (rr * (dz - n * (jnp.sum(dz * n, axis=-1, keepdims=True) * (1.0 / HEAD_DIM)))
                                  ).astype(BF16)
            return jnp.sum(dhat * n, axis=0, keepdims=True)

        def group_sum(g0, d_ref, count, scale):
            acc = jnp.zeros((1, LANES), F32)
            for h in range(count):
                d = d_ref[:, h * LANES:(h + 1) * LANES]
                acc = acc + norm_bwd(g0 + h, d * scale if scale != 1.0 else d)
            return acc

        small_ref[0:1, :] += group_sum(G_QA, dqa_ref, A_HEADS, SCALE)
        small_ref[1:2, :] += group_sum(G_KA, dka_ref, A_KV_HEADS, 1.0)
        small_ref[2:3, :] += group_sum(G_QB, dqb_ref, B_HEADS, SCALE)
        small_ref[3:4, :] += group_sum(G_KB, dkb_ref, B_HEADS, LN2)
        dproj_ref[:, G_VA * LANES:G_VB * LANES] = dva_ref[...].astype(BF16)
        dproj_ref[:, G_VB * LANES:G_F * LANES] = dvb_ref[...].astype(BF16)

        dc = jnp.zeros((tm, LANES), F32)
        for piece_q, piece_k in zip(_split3(dqb_ref[...]), _split3(dkb_ref[...])):
            dc = dc + _dot(jnp.concatenate([piece_q, piece_k], axis=1), sel_ref[...])
        dlf = _dot_exact(triu_ref[...], dc) + carry_ref[...]
        carry_ref[...] += jnp.sum(dc, axis=0, keepdims=True)
        dz = dlf / (1.0 + jnp.exp(z_ref[...]))
        small_ref[4:5, :] += jnp.sum(dz, axis=0, keepdims=True)
        dproj_ref[:, G_F * LANES:(G_F + 1) * LANES] = dz.astype(BF16)
        dproj_ref[:, (G_F + 1) * LANES:] = jnp.zeros((tm, LANES), BF16)

        dxn = _dot(dproj_ref[...], w_ref[...])
        x = x_ref[...]
        r = lax.rsqrt(jnp.mean(x * x, axis=-1, keepdims=True) + EPS)
        xh = x * r
        dg_ref[...] += jnp.sum(dxn * xh, axis=0, keepdims=True)
        dxz = dxn * g_ref[...]
        dx_ref[...] = dh_ref[...] + r * (dxz - xh * jnp.mean(dxz * xh, axis=-1, keepdims=True))

    def rev(n):
        return pl.BlockSpec((tm, n), lambda i: (nt - 1 - i, 0))

    return _pcall(
        body, name="dproj_dx", grid=(nt,),
        in_specs=[rev(N_NORM_GROUPS * LANES), rev(A_HEADS * LANES), rev(A_KV_HEADS * LANES), rev(B_HEADS * LANES),
                  rev(B_HEADS * LANES), rev(A_KV_HEADS * LANES), rev(B_HEADS * LANES), rev(LANES), rev(D_MODEL),
                  rev(D_MODEL), _const((1, NP)), _const((tm, tm)), _const(sel.shape), _const((NP, D_MODEL)),
                  _const((1, D_MODEL))],
        out_specs=[rev(NP), _const((8, LANES)), rev(D_MODEL), _const((1, D_MODEL))],
        out_shape=[jax.ShapeDtypeStruct((t_all, NP), BF16), jax.ShapeDtypeStruct((8, LANES), F32),
                   jax.ShapeDtypeStruct((t_all, D_MODEL), F32), jax.ShapeDtypeStruct((1, D_MODEL), F32)],
        scratch_shapes=[pltpu.VMEM((1, LANES), F32)],
        compiler_params=_params(("arbitrary",)),
    )(pre, dqa, dka, dqb, dkb, dva, dvb, z, x2, dh, gain_row, triu, sel, w_pad_t, g1)


def _dwin(dproj, xn):
    t_all = xn.shape[0]
    tt = min(512, t_all)
    half = NP // 2

    def body(a_ref, b_ref, o_ref):
        @pl.when(pl.program_id(1) == 0)
        def _():
            o_ref[...] = jnp.zeros_like(o_ref)

        o_ref[...] += _dot_tn(a_ref[...], b_ref[...])

    return _pcall(
        body, name="dwin", grid=(2, t_all // tt),
        in_specs=[pl.BlockSpec((tt, half), lambda j, t: (t, j)), pl.BlockSpec((tt, D_MODEL), lambda j, t: (t, 0))],
        out_specs=pl.BlockSpec((half, D_MODEL), lambda j, t: (j, 0)),
        out_shape=jax.ShapeDtypeStruct((NP, D_MODEL), F32),
        compiler_params=_params(("parallel", "arbitrary")),
    )(dproj, xn)


ANY = pl.BlockSpec(memory_space=pl.ANY)


def _place():
    return lax.axis_index("x"), lax.axis_index("y"), lax.axis_index("c")


class _Gather:
    def __init__(self, srcs, outs, send_sems, recv_sems, local_sems):
        self.srcs, self.outs = srcs, outs
        self.send_sems, self.recv_sems, self.local_sems = send_sems, recv_sems, local_sems
        x, y, c = _place()
        self.c = c
        self.me, self.sibling = (x, y, c), (x, y, 1 - c)
        self.chips = [(1 - x, y), (x, 1 - y), (1 - x, 1 - y)]

    def _rows(self, a, px, py, pc):
        m = self.srcs[a].shape[0]
        return self.outs[a].at[pl.ds((4 * px + 2 * py + pc) * m, m), :]

    def _copy(self, a, k, block, to, from_src=False):
        return pltpu.make_async_remote_copy(
            src_ref=self.srcs[a] if from_src else self._rows(a, *block), dst_ref=self._rows(a, *block),
            send_sem=self.send_sems.at[k, a], recv_sem=self.recv_sems.at[k, a], device_id=to, device_id_type=MESH)

    def _own(self, a):
        return pltpu.make_async_copy(self.srcs[a], self._rows(a, *self.me), self.local_sems.at[a])

    def start(self):
        for a in range(len(self.srcs)):
            self._own(a).start()
            self._copy(a, 0, self.me, self.sibling, from_src=True).start()
            for j, chip in enumerate(self.chips):
                self._copy(a, 1 + j, self.me, (*chip, self.c), from_src=True).start()

    def forward(self):
        for a in range(len(self.srcs)):
            for j, chip in enumerate(self.chips):
                self._copy(a, 1 + j, (*chip, self.c), self.me).wait_recv()
                self._copy(a, 4 + j, (*chip, self.c), self.sibling).start()

    def finish(self):
        for a in range(len(self.srcs)):
            self._copy(a, 0, self.sibling, self.me).wait_recv()
            for j, chip in enumerate(self.chips):
                self._copy(a, 4 + j, (*chip, 1 - self.c), self.me).wait_recv()
            self._copy(a, 0, self.me, self.sibling, from_src=True).wait_send()
            for j, chip in enumerate(self.chips):
                self._copy(a, 1 + j, self.me, (*chip, self.c), from_src=True).wait_send()
                self._copy(a, 4 + j, (*chip, self.c), self.sibling).wait_send()
            self._own(a).wait()


def _gather_scratch(n_arrays):
    return [pltpu.SemaphoreType.DMA((7, n_arrays)), pltpu.SemaphoreType.DMA((7, n_arrays)),
            pltpu.SemaphoreType.DMA((n_arrays,))]


def _allgather_halves(mine):
    m_per, n = mine.shape

    def body(x_ref, out_ref, send_sems, recv_sems, local_sems):
        gather = _Gather((x_ref,), (out_ref,), send_sems, recv_sems, local_sems)
        gather.start()
        gather.forward()
        gather.finish()

    return _pcall(
        body, name="allgather_w_in",
        out_shape=jax.ShapeDtypeStruct((8 * m_per, n), mine.dtype),
        in_specs=[ANY], out_specs=ANY, scratch_shapes=_gather_scratch(1),
    )(mine)


def _rs_pair_exchange(g, name):
    def body(g_ref, out_ref, send_sem, recv_sem):
        x, y, c = _place()
        cp = pltpu.make_async_remote_copy(
            src_ref=g_ref.at[1 - c], dst_ref=out_ref, send_sem=send_sem, recv_sem=recv_sem,
            device_id=(x, y, 1 - c), device_id_type=MESH)
        cp.start()
        cp.wait()

    return _pcall(
        body, name=name, out_shape=jax.ShapeDtypeStruct(g.shape[1:], F32),
        in_specs=[ANY], out_specs=ANY, scratch_shapes=[pltpu.SemaphoreType.DMA, pltpu.SemaphoreType.DMA],
    )(g)


def _rs_pair_add(g, got, c_idx, name):
    rows = g.shape[2]

    def body(c_ref, a_ref, b_ref, o_ref, ob_ref):
        pair = a_ref[...] + b_ref[...]
        o_ref[...] = pair
        ob_ref[...] = pair.astype(BF16)

    blk = pl.BlockSpec((None, rows, D_MODEL), lambda s, c_ref: (s, 0, 0))
    return _pcall(
        body, name=name,
        grid_spec=pltpu.PrefetchScalarGridSpec(
            num_scalar_prefetch=1, grid=(N_CHIPS,),
            in_specs=[pl.BlockSpec((None, None, rows, D_MODEL), lambda s, c_ref: (c_ref[0], s, 0, 0)), blk],
            out_specs=[blk, blk]),
        out_shape=[jax.ShapeDtypeStruct((N_CHIPS, rows, D_MODEL), F32),
                   jax.ShapeDtypeStruct((N_CHIPS, rows, D_MODEL), BF16)],
        compiler_params=_params(("parallel",)),
    )(c_idx, g, got)


def _chip_exchange_copies(p_ref, out_ref, send_sems, recv_sems):
    x, y, c = _place()
    chips = [(1 - x, y), (x, 1 - y), (1 - x, 1 - y)]
    return [pltpu.make_async_remote_copy(
        src_ref=p_ref.at[2 * cx + cy], dst_ref=out_ref.at[j], send_sem=send_sems.at[j], recv_sem=recv_sems.at[j],
        device_id=(cx, cy, c), device_id_type=MESH) for j, (cx, cy) in enumerate(chips)]


def _rs_chip_exchange(p4, name):
    def body(p_ref, out_ref, send_sems, recv_sems):
        cps = _chip_exchange_copies(p_ref, out_ref, send_sems, recv_sems)
        for cp in cps:
            cp.start()
        for cp in cps:
            cp.wait()

    return _pcall(
        body, name=name, out_shape=jax.ShapeDtypeStruct((3,) + p4.shape[1:], p4.dtype),
        in_specs=[ANY], out_specs=ANY,
        scratch_shapes=[pltpu.SemaphoreType.DMA((3,)), pltpu.SemaphoreType.DMA((3,))],
    )(p4)


def _rs_chip_add(p4, got, sc_idx, name):
    rows = p4.shape[1]
    tr = next(rows // n for n in (8, 7, 6, 5, 4, 3, 2, 1) if rows % n == 0 and (rows // n) % 16 == 0)

    def body(sc_ref, a_ref, b_ref, o_ref):
        o_ref[...] = ((a_ref[...] + b_ref[0].astype(F32)) + b_ref[1].astype(F32)) + b_ref[2].astype(F32)

    return _pcall(
        body, name=name,
        grid_spec=pltpu.PrefetchScalarGridSpec(
            num_scalar_prefetch=1, grid=(rows // tr,),
            in_specs=[pl.BlockSpec((None, tr, D_MODEL), lambda i, sc_ref: (sc_ref[0], i, 0)),
                      pl.BlockSpec((3, tr, D_MODEL), lambda i, sc_ref: (0, i, 0))],
            out_specs=pl.BlockSpec((None, tr, D_MODEL), lambda i, sc_ref: (sc_ref[1], i, 0))),
        out_shape=jax.ShapeDtypeStruct((2, rows, D_MODEL), F32),
        compiler_params=_params(("parallel",)),
    )(sc_idx, p4, got)


def _rs_pair_share(halves, name):
    def body(r_ref, out_ref, send_sem, recv_sem):
        x, y, c = _place()
        cp = pltpu.make_async_remote_copy(
            src_ref=r_ref.at[c], dst_ref=out_ref.at[c], send_sem=send_sem, recv_sem=recv_sem,
            device_id=(x, y, 1 - c), device_id_type=MESH)
        cp.start()
        cp.wait()

    return _pcall(
        body, name=name, out_shape=jax.ShapeDtypeStruct(halves.shape, F32),
        in_specs=[ANY], out_specs=ANY, input_output_aliases={0: 0},
        scratch_shapes=[pltpu.SemaphoreType.DMA, pltpu.SemaphoreType.DMA],
    )(halves)


def _adam(w, g, m, v):
    m2 = ADAM_B1 * m + (1.0 - ADAM_B1) * g
    v2 = ADAM_B2 * v + (1.0 - ADAM_B2) * (g * g)
    m_hat = m2 / (1.0 - ADAM_B1 ** ADAM_STEP)
    v_hat = v2 / (1.0 - ADAM_B2 ** ADAM_STEP)
    return -ADAM_LR * (m_hat / (jnp.sqrt(v_hat) + ADAM_EPS) + ADAM_WD * w), m2, v2


def _small_allreduce(part):
    def body(p_ref, g_ref, buf, send_sems, recv_sems):
        x, y, c = _place()
        me = 4 * x + 2 * y + c
        cps = []
        for k in range(1, 8):
            peer = (1 - x if k & 4 else x, 1 - y if k & 2 else y, 1 - c if k & 1 else c)
            cps.append(pltpu.make_async_remote_copy(
                src_ref=p_ref, dst_ref=buf.at[me], send_sem=send_sems.at[k - 1], recv_sem=recv_sems.at[k - 1],
                device_id=peer, device_id_type=MESH))
        for cp in cps:
            cp.start()
        buf[me] = p_ref[...]
        for cp in cps:
            cp.wait()
        g = buf[0]
        for k in range(1, 8):
            g = g + buf[k]
        g_ref[...] = g

    vm = pl.BlockSpec(memory_space=pltpu.VMEM)
    return _pcall(
        body, name="small_allreduce",
        out_shape=jax.ShapeDtypeStruct((SMALL_ROWS, LANES), F32), in_specs=[vm], out_specs=vm,
        scratch_shapes=[pltpu.VMEM((8, SMALL_ROWS, LANES), F32), pltpu.SemaphoreType.DMA((7,)),
                        pltpu.SemaphoreType.DMA((7,))],
    )(part)


def _small_adamw(g, w, m, v):
    pieces = ((0, 8, LANES), (20, 1, B_HEADS), (16, 1, HEAD_DIM), (17, 1, HEAD_DIM), (21, 1, A_HEADS),
              (18, 1, HEAD_DIM), (19, 1, HEAD_DIM), (8, 8, LANES))

    def body(g_ref, w_ref, m_ref, v_ref, *rest):
        outs, (loss_ref, stage) = rest[:4 * len(pieces)], rest[4 * len(pieces):]
        g = g_ref[...]
        loss_ref[...] = g_ref[ROW_LOSS:ROW_LOSS + 1, :]
        for kind, packed in enumerate((g,) + _adam(w_ref[...], g, m_ref[...], v_ref[...])):
            stage[...] = packed
            for i, (row, rows, lanes) in enumerate(pieces):
                outs[kind * len(pieces) + i][...] = stage[row:row + rows, 0:lanes]

    vm = pl.BlockSpec(memory_space=pltpu.VMEM)
    shapes = [jax.ShapeDtypeStruct((rows, lanes), F32) for _ in range(4) for _, rows, lanes in pieces]
    shapes.append(jax.ShapeDtypeStruct((1, LANES), F32))
    res = _pcall(
        body, name="small_adamw", out_shape=shapes, in_specs=[vm, vm, vm, vm], out_specs=[vm] * len(shapes),
        scratch_shapes=[pltpu.VMEM((SMALL_ROWS, LANES), F32)],
    )(g, w, m, v)
    flat = [r.reshape(r.size) for r in res[:-1]]
    n = len(pieces)
    return [flat[k * n:(k + 1) * n] for k in range(4)], res[-1][0, 0]


def _adamw(w, g, m, v, name):
    rows, cols = w.shape
    tr = min(256, rows)

    def body(w_ref, g_ref, m_ref, v_ref, d_ref, m2_ref, v2_ref):
        d_ref[...], m2_ref[...], v2_ref[...] = _adam(w_ref[...], g_ref[...], m_ref[...], v_ref[...])

    spec = _rows(tr, cols)
    shp = jax.ShapeDtypeStruct((rows, cols), F32)
    return _pcall(
        body, name=name, grid=(rows // tr,), in_specs=[spec] * 4, out_specs=[spec] * 3, out_shape=[shp] * 3,
        compiler_params=_params(("parallel",)),
    )(w, g, m, v)


def _pad_lanes(v):
    return jnp.pad(v, (0, LANES - v.shape[0]))


def _pad_head_rows(w_t, heads):
    n = w_t.shape[1]
    return jnp.pad(w_t.reshape(heads, HEAD_DIM, n), ((0, 0), (0, LANES - HEAD_DIM), (0, 0))).reshape(heads * LANES, n)


def _unpad_head_rows(w_t, heads):
    n = w_t.shape[1]
    return w_t.reshape(heads, LANES, n)[:, :HEAD_DIM].reshape(heads * HEAD_DIM, n)


def _in_rows_pad(w_in_t):
    qa, ka, va, qb, kb, vb, f = jnp.split(w_in_t, [512, 640, 768, 1280, 1792, 2304], axis=0)
    f = jnp.pad(f, ((0, 2 * LANES - B_HEADS), (0, 0)))
    return jnp.concatenate([_pad_head_rows(qa, 8), _pad_head_rows(ka, 2), _pad_head_rows(qb, 8),
                            _pad_head_rows(kb, 8), _pad_head_rows(va, 2), _pad_head_rows(vb, 8), f], axis=0)


def _in_rows_unpad(d):
    qa = _unpad_head_rows(d[G_QA * LANES:G_KA * LANES], 8)
    ka = _unpad_head_rows(d[G_KA * LANES:G_QB * LANES], 2)
    qb = _unpad_head_rows(d[G_QB * LANES:G_KB * LANES], 8)
    kb = _unpad_head_rows(d[G_KB * LANES:G_VA * LANES], 8)
    va = _unpad_head_rows(d[G_VA * LANES:G_VB * LANES], 2)
    vb = _unpad_head_rows(d[G_VB * LANES:G_F * LANES], 8)
    f = d[G_F * LANES:G_F * LANES + B_HEADS]
    return jnp.concatenate([qa, ka, va, qb, kb, vb, f], axis=0)


def _pack_small(g1, bf, qa, ka, sk, qb, kb, g2, loss_row):
    rows = [g1.reshape(8, LANES), g2.reshape(8, LANES)]
    rows += [_pad_lanes(t)[None] for t in (qa, ka, qb, kb, bf, sk)]
    rows += [loss_row, jnp.zeros((1, LANES), F32)]
    return jnp.concatenate(rows, axis=0)


def kernel(x, attn_norm_g, w_in, b_forget, q_norm_a, k_norm_a, sink_logits, q_norm_b, k_norm_b, w_out, mlp_norm_g, w_up, w_down, loss_target, m_attn_norm_g, m_w_in, m_b_forget, m_q_norm_a, m_k_norm_a, m_sink_logits, m_q_norm_b, m_k_norm_b, m_w_out, m_mlp_norm_g, m_w_up, m_w_down, v_attn_norm_g, v_w_in, v_b_forget, v_q_norm_a, v_k_norm_a, v_sink_logits, v_q_norm_b, v_k_norm_b, v_w_out, v_mlp_norm_g, v_w_up, v_w_down):
    nb, seq, _ = x.shape
    t_all = nb * seq
    c_idx = lax.axis_index("c")
    s_idx = 2 * lax.axis_index("x") + lax.axis_index("y")

    def my_half(a):
        halves = a.astype(BF16).reshape(2, a.shape[0] // 2, a.shape[1])
        return lax.dynamic_slice_in_dim(halves, c_idx, 1, axis=0)[0]

    w_in_shard_t = jnp.pad(w_in.T, ((0, IN_SHARD_P - IN_SHARD), (0, 0)))
    gathered_in = _allgather_halves(my_half(w_in_shard_t)).reshape(N_CHIPS, IN_SHARD_P, D_MODEL)
    w_pad_t = _in_rows_pad(gathered_in[:, :IN_SHARD].reshape(IN_WIDTH, D_MODEL))

    ones = jnp.ones((LANES,), F32)
    gain_row = jnp.concatenate(
        [jnp.tile(_pad_lanes(q_norm_a), 8), jnp.tile(_pad_lanes(k_norm_a), 2), jnp.tile(_pad_lanes(q_norm_b), 8),
         jnp.tile(_pad_lanes(k_norm_b), 8), jnp.tile(ones, N_GROUPS - N_NORM_GROUPS)])[None]
    b_row = _pad_lanes(b_forget)[None]
    g1 = attn_norm_g[None]
    g2 = mlp_norm_g[None]
    slopes = jnp.exp2(-(8.0 / A_HEADS) * (jnp.arange(A_HEADS, dtype=F32) + 1.0))

    x2 = x.reshape(t_all, D_MODEL)
    tgt = loss_target.reshape(t_all, D_MODEL)

    (xn, pre, qa, ka, va, qb, kb, vb, z), (w_out_g, w_up_g, w_down_f) = _inproj(
        x2, g1, w_pad_t, gain_row, b_row, seq, [my_half(w_out), my_half(w_up), my_half(w_down)])
    wo_pad = _pad_head_rows(w_out_g, A_HEADS + B_HEADS)
    w_up_blocks = w_up_g.reshape(N_CHIPS, D_MODEL, D_MODEL)
    swa_bias = _swa_bias(slopes)
    oa, la = _swa_fwd(qa, ka, va, sink_logits, swa_bias, nb, seq)
    ob, lse = _fox_fwd(qb, kb, vb, nb, seq)
    h, hn = _outproj(x2, oa, ob, wo_pad, g2)
    ru, dy, dyb, loss_acc = _mlp_fwd(hn, w_up_blocks, w_down_f, h, tgt)

    du, d_w_mlp = _mlp_bwd_w(dyb, w_down_f, ru, hn)
    c_arg = c_idx.reshape(1).astype(jnp.int32)
    sc_arg = jnp.stack([s_idx, c_idx]).astype(jnp.int32)
    dh, dhb, d_g2, sibling_w_mlp = _mlp_dhn(du, w_up_blocks, h, dy, g2, d_w_mlp)
    pair_m, pair_m_bf = _rs_pair_add(d_w_mlp, sibling_w_mlp, c_arg, "rs_pair_add_mlp")
    doa, dob, delta_b, d_wo = _dmixed(dhb, wo_pad, oa, ob)
    dqb, dkb, dvb, got_m = _fox_bwd(qb, kb, vb, dob, lse, delta_b, nb, seq, pair_m_bf)
    red_m = _rs_pair_share(_rs_chip_add(pair_m, got_m, sc_arg, "rs_chip_add_mlp"), "rs_pair_share_mlp")
    g_w_up, g_w_down = red_m[0], red_m[1]
    dqa, dka, dva, dsink = _swa_bwd(qa, ka, va, oa, doa, la, sink_logits, swa_bias, nb, seq)
    dproj, small, grad_x, d_g1 = _dproj_dx(pre, dqa, dka, dqb, dkb, dva, dvb, z, x2, dh, gain_row, w_pad_t, g1, seq)
    d_w_in_t = _dwin(dproj, xn)

    d_w_out = _unpad_head_rows(d_wo, A_HEADS + B_HEADS)
    g_att = jnp.concatenate([
        jnp.pad(_in_rows_unpad(d_w_in_t).reshape(N_CHIPS, IN_SHARD, D_MODEL),
                ((0, 0), (0, IN_SHARD_P - IN_SHARD), (0, 0))),
        d_w_out.reshape(N_CHIPS, D_MODEL // N_CHIPS, D_MODEL)], axis=1)
    g_att = jnp.stack([g_att[:, :R_ATT // 2], g_att[:, R_ATT // 2:]])
    pair_a, pair_a_bf = _rs_pair_add(g_att, _rs_pair_exchange(g_att, "rs_pair_exchange_att"), c_arg, "rs_pair_add_att")
    got_a = _rs_chip_exchange(pair_a_bf, "rs_chip_exchange_att")
    red_a = _rs_pair_share(_rs_chip_add(pair_a, got_a, sc_arg, "rs_chip_add_att"), "rs_pair_share_att")
    red_a = red_a.reshape(R_ATT, D_MODEL)
    g_w_in = red_a[:IN_SHARD].T
    g_w_out = red_a[IN_SHARD_P:]

    loss_row = loss_acc[0:1] * (0.5 / D_MODEL)
    d_sink = dsink[:, :A_GROUP, 0].reshape(nb, A_HEADS).sum(axis=0)
    part = _pack_small(d_g1[0], small[4, :B_HEADS], small[0, :HEAD_DIM], small[1, :HEAD_DIM], d_sink,
                       small[2, :HEAD_DIM], small[3, :HEAD_DIM], d_g2[0], loss_row)
    zero_row = jnp.zeros((1, LANES), F32)
    smalls = lambda t: _pack_small(*t, zero_row)
    w_small = smalls((attn_norm_g, b_forget, q_norm_a, k_norm_a, sink_logits, q_norm_b, k_norm_b, mlp_norm_g))
    m_small = smalls((m_attn_norm_g, m_b_forget, m_q_norm_a, m_k_norm_a, m_sink_logits, m_q_norm_b, m_k_norm_b,
                      m_mlp_norm_g))
    v_small = smalls((v_attn_norm_g, v_b_forget, v_q_norm_a, v_k_norm_a, v_sink_logits, v_q_norm_b, v_k_norm_b,
                      v_mlp_norm_g))
    (g_s, d_s, m_s, v_s), loss = _small_adamw(_small_allreduce(part), w_small, m_small, v_small)

    big = {}
    for name, w, g, m, v in (("adamw_w_in", w_in, g_w_in, m_w_in, v_w_in),
                             ("adamw_w_out", w_out, g_w_out, m_w_out, v_w_out),
                             ("adamw_w_up", w_up, g_w_up, m_w_up, v_w_up),
                             ("adamw_w_down", w_down, g_w_down, m_w_down, v_w_down)):
        big[name] = (g,) + tuple(_adamw(w, g, m, v, name))

    def assemble(k, s):
        return (s[0], big["adamw_w_in"][k], s[1], s[2], s[3], s[4], s[5], s[6], big["adamw_w_out"][k], s[7],
                big["adamw_w_up"][k], big["adamw_w_down"][k])

    return (loss, grad_x.reshape(nb, seq, D_MODEL), *assemble(0, g_s), *assemble(1, d_s), *assemble(2, m_s),
            *assemble(3, v_s))
```

```python
import functools

import numpy as np
import jax
import jax.numpy as jnp
from jax import lax
from jax.experimental import pallas as pl
from jax.experimental.pallas import tpu as pltpu

F32 = jnp.float32
BF16 = jnp.bfloat16

D_MODEL = 1024
HEAD_DIM = 64
LANES = 128
A_HEADS = 8
A_KV_HEADS = 2
A_GROUP = A_HEADS // A_KV_HEADS
B_HEADS = 8
WINDOW = 128
D_FF = 4096
IN_WIDTH = 2312
EPS = 1e-6
SCALE = 0.125
LOG2E = 1.4426950408889634
LN2 = 0.6931471805599453
CHUNK = 32
FOX_TK = 512
FOX_PARTS = 8
FOX_PARTS_BWD = 4
NEG = -1e30

G_QA, G_KA, G_QB, G_KB, G_VA, G_VB, G_F = 0, 8, 10, 18, 26, 28, 36
N_NORM_GROUPS = 26
N_GROUPS = 38
NP = N_GROUPS * LANES
MIXED_P = (A_HEADS + B_HEADS) * LANES

N_CHIPS = 4
IN_SHARD = IN_WIDTH // N_CHIPS
IN_SHARD_P = 608
R_ATT = IN_SHARD_P + D_MODEL // N_CHIPS

SMALL_ROWS = 24
ROW_LOSS = 22

ADAM_LR = 0.001
ADAM_B1 = 0.9
ADAM_B2 = 0.999
ADAM_EPS = 1e-08
ADAM_WD = 0.01
ADAM_STEP = 10

VMEM_LIMIT = 52 * 1024 * 1024
MESH = pl.DeviceIdType.MESH


def _pcall(body, **kw):
    return pl.pallas_call(body, **kw)


def _params(sem=None):
    return pltpu.CompilerParams(dimension_semantics=sem, vmem_limit_bytes=VMEM_LIMIT)


def _dot(a, b):
    return jnp.dot(a, b, preferred_element_type=F32)


def _dot_nt(a, b):
    return lax.dot_general(a, b, (((1,), (1,)), ((), ())), preferred_element_type=F32)


def _dot_tn(a, b):
    return lax.dot_general(a, b, (((0,), (0,)), ((), ())), preferred_element_type=F32)


def _split3(x):
    hi = x.astype(BF16)
    r1 = x - hi.astype(F32)
    mid = r1.astype(BF16)
    lo = (r1 - mid.astype(F32)).astype(BF16)
    return hi, mid, lo


def _dot_exact(mat, x):
    hi, mid, lo = _split3(x)
    return _dot(mat, lo) + _dot(mat, mid) + _dot(mat, hi)


def _const(shape):
    zeros = (0,) * len(shape)
    return pl.BlockSpec(shape, lambda *_: zeros)


def _rows(tm, n):
    return pl.BlockSpec((tm, n), lambda i: (i, 0))


def _aug_select():
    e = np.zeros((3 * LANES, 2 * B_HEADS * LANES), np.float32)
    for j in range(3):
        for h in range(B_HEADS):
            e[j * LANES + h, h * LANES + HEAD_DIM + j] = 1.0
            e[j * LANES + h, (B_HEADS + h) * LANES + HEAD_DIM + 3 + j] = -1.0
    return jnp.asarray(e, BF16)


def _dc_select():
    e = np.zeros((2 * B_HEADS * LANES, LANES), np.float32)
    for h in range(B_HEADS):
        e[h * LANES + HEAD_DIM, h] = 1.0
        e[(B_HEADS + h) * LANES + HEAD_DIM + 3, h] = -1.0
    return jnp.asarray(e, BF16)


def _tri(n, upper):
    t = np.tril(np.ones((n, n), np.float32))
    return jnp.asarray(t.T if upper else t, BF16)


def _inproj(x2, g1, w_pad_t, gain_row, b_row, seq, later_weights):
    t_all = x2.shape[0]
    tm = min(256, seq)
    n_steps = t_all // tm
    forward_step = max(n_steps - 2, 0)
    tiles_per_seq = seq // tm
    tri = _tri(tm, False)
    esel = _aug_select()
    n_later = len(later_weights)

    def body(x_ref, g_ref, w_ref, gain_ref, b_ref, tri_ref, e_ref, *rest):
        later_src, rest = rest[:n_later], rest[n_later:]
        xn_ref, pre_ref, qa_ref, ka_ref, va_ref, qb_ref, kb_ref, vb_ref, z_ref = rest[:9]
        later_out, (carry_ref, send_sems, recv_sems, local_sems) = rest[9:9 + n_later], rest[9 + n_later:]
        i = pl.program_id(0)
        gather = _Gather(later_src, later_out, send_sems, recv_sems, local_sems)

        @pl.when(i == 0)
        def _():
            gather.start()

        @pl.when(i == forward_step)
        def _():
            gather.forward()

        @pl.when(i % tiles_per_seq == 0)
        def _():
            carry_ref[...] = jnp.zeros_like(carry_ref)

        x = x_ref[...]
        r = lax.rsqrt(jnp.mean(x * x, axis=-1, keepdims=True) + EPS)
        xn = (x * r * g_ref[...]).astype(BF16)
        xn_ref[...] = xn
        proj = _dot_nt(xn, w_ref[...])
        pre_ref[...] = proj[:, :N_NORM_GROUPS * LANES].astype(BF16)
        lane = lax.broadcasted_iota(jnp.int32, (tm, LANES), 1)

        z = proj[:, G_F * LANES:(G_F + 1) * LANES] + b_ref[...]
        z_ref[...] = z
        lf = jnp.minimum(z, 0.0) - jnp.log(1.0 + jnp.exp(-jnp.abs(z)))
        lf = jnp.where(lane < B_HEADS, lf, 0.0)
        c = _dot_exact(tri_ref[...], lf) + carry_ref[...]
        carry_ref[...] += jnp.sum(lf, axis=0, keepdims=True)
        aug = _dot(jnp.concatenate(_split3(c * LOG2E), axis=1), e_ref[...])

        def hnorm(g):
            p = proj[:, g * LANES:(g + 1) * LANES]
            rr = lax.rsqrt(jnp.sum(p * p, axis=-1, keepdims=True) * (1.0 / HEAD_DIM) + EPS)
            return p * rr * gain_ref[:, g * LANES:(g + 1) * LANES]

        ones_q = jnp.where((lane >= HEAD_DIM + 3) & (lane < HEAD_DIM + 6), 1.0, 0.0)
        ones_k = jnp.where((lane >= HEAD_DIM) & (lane < HEAD_DIM + 3), 1.0, 0.0)
        for h in range(A_HEADS):
            qa_ref[:, h * LANES:(h + 1) * LANES] = (hnorm(G_QA + h) * SCALE).astype(BF16)
        for h in range(A_KV_HEADS):
            ka_ref[:, h * LANES:(h + 1) * LANES] = hnorm(G_KA + h).astype(BF16)
        for h in range(B_HEADS):
            qb_ref[:, h * LANES:(h + 1) * LANES] = (
                hnorm(G_QB + h) * (SCALE * LOG2E) + aug[:, h * LANES:(h + 1) * LANES] + ones_q).astype(BF16)
            kb_ref[:, h * LANES:(h + 1) * LANES] = (
                hnorm(G_KB + h) + aug[:, (B_HEADS + h) * LANES:(B_HEADS + h + 1) * LANES] + ones_k).astype(BF16)
        va_ref[...] = proj[:, G_VA * LANES:G_VB * LANES].astype(BF16)
        one_v = jnp.where(lane == HEAD_DIM, 1.0, 0.0)
        for h in range(B_HEADS):
            cols = slice((G_VB + h) * LANES, (G_VB + h + 1) * LANES)
            vb_ref[:, h * LANES:(h + 1) * LANES] = (proj[:, cols] + one_v).astype(BF16)

        @pl.when(i == n_steps - 1)
        def _():
            gather.finish()

    widths = [(D_MODEL, BF16), (N_NORM_GROUPS * LANES, BF16), (A_HEADS * LANES, BF16), (A_KV_HEADS * LANES, BF16),
              (A_KV_HEADS * LANES, BF16), (B_HEADS * LANES, BF16), (B_HEADS * LANES, BF16), (B_HEADS * LANES, BF16),
              (LANES, F32)]
    res = _pcall(
        body, name="inproj", grid=(n_steps,),
        in_specs=[_rows(tm, D_MODEL), _const((1, D_MODEL)), _const((NP, D_MODEL)), _const((1, NP)),
                  _const((1, LANES)), _const((tm, tm)), _const(esel.shape)] + [ANY] * n_later,
        out_specs=[_rows(tm, w) for w, _ in widths] + [ANY] * n_later,
        out_shape=[jax.ShapeDtypeStruct((t_all, w), dt) for w, dt in widths]
        + [jax.ShapeDtypeStruct((8 * w.shape[0], w.shape[1]), w.dtype) for w in later_weights],
        scratch_shapes=[pltpu.VMEM((1, LANES), F32)] + _gather_scratch(n_later),
        compiler_params=_params(("arbitrary",)),
    )(x2, g1, w_pad_t, gain_row, b_row, tri, esel, *later_weights)
    return res[:9], res[9:]


def _fox_fwd(qb, kb, vb, nb, seq):
    t_all = qb.shape[0]
    tk = min(FOX_TK, seq // FOX_PARTS)
    tq = FOX_PARTS * tk
    nq = seq // tq

    def body(q_ref, k_ref, v_ref, o_ref, lse_ref, s_ref, p_ref, m_ref, alpha_ref, acc_ref):
        qi = pl.program_id(2)
        q = q_ref[...]
        m_ref[...] = jnp.full((tq, LANES), NEG, F32)
        acc_ref[...] = jnp.zeros((tq, LANES), F32)

        def step(j, modes):
            off = pl.multiple_of(j * tk, tk)
            k = k_ref[pl.ds(off, tk), :]
            v = v_ref[pl.ds(off, tk), :]
            live = [hf for hf in range(FOX_PARTS) if modes[hf] is not None]
            for hf in live:
                s_ref[hf] = _dot_nt(q[hf * tk:(hf + 1) * tk], k)
            for hf in live:
                for r in range(0, tk, CHUNK):
                    rows = slice(r, r + CHUNK)
                    grows = slice(hf * tk + r, hf * tk + r + CHUNK)
                    tiles = []
                    for jt in range(tk // LANES):
                        sc = s_ref[hf, rows, jt * LANES:(jt + 1) * LANES]
                        if modes[hf] == "diag":
                            row = r + lax.broadcasted_iota(jnp.int32, (CHUNK, LANES), 0)
                            col = jt * LANES + lax.broadcasted_iota(jnp.int32, (CHUNK, LANES), 1)
                            sc = jnp.where(row >= col, sc, NEG)
                        tiles.append(sc)
                    m_prev = m_ref[grows, :]
                    m_cur = functools.reduce(jnp.maximum, tiles)
                    m_new = jnp.maximum(m_prev, jnp.max(m_cur, axis=-1, keepdims=True))
                    m_ref[grows, :] = m_new
                    alpha_ref[grows, :] = jnp.exp2(m_prev - m_new)
                    for jt, sc in enumerate(tiles):
                        p_ref[hf, rows, jt * LANES:(jt + 1) * LANES] = jnp.exp2(sc - m_new).astype(BF16)
                hrows = slice(hf * tk, (hf + 1) * tk)
                acc_ref[hrows, :] = alpha_ref[hrows, :] * acc_ref[hrows, :] + _dot(p_ref[hf], v)

        def past(j, carry):
            step(j, ("full",) * FOX_PARTS)
            return carry

        lax.fori_loop(0, FOX_PARTS * qi, past, 0)
        for d in range(FOX_PARTS):
            step(FOX_PARTS * qi + d, (None,) * d + ("diag",) + ("full",) * (FOX_PARTS - 1 - d))
        acc = acc_ref[...]
        lane = lax.broadcasted_iota(jnp.int32, (tq, LANES), 1)
        l = jnp.sum(jnp.where(lane == HEAD_DIM, acc, 0.0), axis=-1, keepdims=True)
        o_ref[...] = (acc / l).astype(BF16)
        lse_ref[...] = m_ref[...] + jnp.log2(l)

    qspec = pl.BlockSpec((tq, LANES), lambda b, h, i: (b * nq + i, h))
    kspec = pl.BlockSpec((seq, LANES), lambda b, h, i: (b, h))
    return _pcall(
        body, name="fox_fwd", grid=(nb, B_HEADS, nq),
        in_specs=[qspec, kspec, kspec], out_specs=[qspec, qspec],
        out_shape=[jax.ShapeDtypeStruct((t_all, B_HEADS * LANES), BF16),
                   jax.ShapeDtypeStruct((t_all, B_HEADS * LANES), F32)],
        scratch_shapes=[pltpu.VMEM((FOX_PARTS, tk, tk), F32), pltpu.VMEM((FOX_PARTS, tk, tk), BF16),
                        pltpu.VMEM((tq, LANES), F32),
                        pltpu.VMEM((tq, LANES), F32), pltpu.VMEM((tq, LANES), F32)],
        compiler_params=_params(("parallel", "parallel", "arbitrary")),
    )(qb, kb, vb)


def _swa_bias(slopes):
    row = jnp.arange(A_GROUP * WINDOW, dtype=jnp.int32)[:, None] % WINDOW
    col = jnp.arange(2 * WINDOW, dtype=jnp.int32)[None, :]
    slope_rows = jnp.repeat(slopes.reshape(A_KV_HEADS, A_GROUP), WINDOW, axis=1)[:, :, None]
    out = []
    for t_rel in (0, WINDOW):
        dist = t_rel + row - col
        valid = (dist >= 0) & (dist < WINDOW)
        out.append(jnp.where(valid[None], -slope_rows * dist.astype(F32)[None], NEG))
    return jnp.stack(out)


def _stack_heads(ref, rows):
    return jnp.concatenate([ref[rows, j * LANES:(j + 1) * LANES] for j in range(A_GROUP)], axis=0)


def _sink_rows(sink_ref, g):
    return jnp.concatenate([jnp.full((WINDOW, LANES), sink_ref[g * A_GROUP + j], F32) for j in range(A_GROUP)], axis=0)


def _rep(col):
    return jnp.broadcast_to(col, (col.shape[0], LANES))


def _swa_specs(nq, tq, seq):
    smem = pl.BlockSpec(memory_space=pltpu.SMEM)
    qspec = pl.BlockSpec((tq, A_GROUP * LANES), lambda b, g, i: (b * nq + i, g))
    kspec = pl.BlockSpec((seq, LANES), lambda b, g, i: (b, g))
    bias_first = pl.BlockSpec((None, None, A_GROUP * WINDOW, 2 * WINDOW),
                              lambda b, g, i: (jnp.minimum(i, 1), g, 0, 0))
    bias_rest = pl.BlockSpec((None, None, A_GROUP * WINDOW, 2 * WINDOW), lambda b, g, i: (1, g, 0, 0))
    return smem, qspec, kspec, bias_first, bias_rest


def _swa_fwd(qa, ka, va, sinks, bias, nb, seq):
    t_all = qa.shape[0]
    tq = min(512, seq)
    nq = seq // tq

    def body(sink_ref, q_ref, k_ref, v_ref, bias0_ref, bias_ref, o_ref, l_ref):
        qi = pl.program_id(2)
        sink = _sink_rows(sink_ref, pl.program_id(1))
        for a in range(tq // WINDOW):
            t0 = qi * tq + a * WINDOW
            start = pl.multiple_of(jnp.maximum(t0 - WINDOW, 0), WINDOW)
            rows = slice(a * WINDOW, (a + 1) * WINDOW)
            k = k_ref[pl.ds(start, 2 * WINDOW), :]
            v = v_ref[pl.ds(start, 2 * WINDOW), :]
            s = _dot_nt(_stack_heads(q_ref, rows), k) + (bias0_ref if a == 0 else bias_ref)[...]
            s0, s1 = s[:, :LANES], s[:, LANES:]
            m = jnp.maximum(_rep(jnp.max(jnp.maximum(s0, s1), axis=-1, keepdims=True)), sink)
            p0, p1 = jnp.exp(s0 - m), jnp.exp(s1 - m)
            den = _rep(jnp.sum(p0 + p1, axis=-1, keepdims=True)) + jnp.exp(sink - m)
            inv = 1.0 / den
            o = _dot(jnp.concatenate([(p0 * inv).astype(BF16), (p1 * inv).astype(BF16)], axis=1), v).astype(BF16)
            lrow = m + jnp.log(den)
            for j in range(A_GROUP):
                o_ref[rows, j * LANES:(j + 1) * LANES] = o[j * WINDOW:(j + 1) * WINDOW]
                l_ref[rows, j * LANES:(j + 1) * LANES] = lrow[j * WINDOW:(j + 1) * WINDOW]

    smem, qspec, kspec, bias_first, bias_rest = _swa_specs(nq, tq, seq)
    return _pcall(
        body, name="swa_fwd", grid=(nb, A_KV_HEADS, nq),
        in_specs=[smem, qspec, kspec, kspec, bias_first, bias_rest], out_specs=[qspec, qspec],
        out_shape=[jax.ShapeDtypeStruct((t_all, A_HEADS * LANES), BF16),
                   jax.ShapeDtypeStruct((t_all, A_HEADS * LANES), F32)],
        compiler_params=_params(("parallel", "parallel", "arbitrary")),
    )(sinks, qa, ka, va, bias, bias)


def _outproj(x2, oa, ob, wo_pad, g2):
    t_all = x2.shape[0]
    tm = min(512, t_all)
    half = A_HEADS * LANES

    def body(x_ref, oa_ref, ob_ref, w_ref, g_ref, h_ref, hn_ref):
        h = x_ref[...] + _dot(oa_ref[...], w_ref[:half, :]) + _dot(ob_ref[...], w_ref[half:, :])
        h_ref[...] = h
        r = lax.rsqrt(jnp.mean(h * h, axis=-1, keepdims=True) + EPS)
        hn_ref[...] = (h * r * g_ref[...]).astype(BF16)

    return _pcall(
        body, name="outproj", grid=(t_all // tm,),
        in_specs=[_rows(tm, D_MODEL), _rows(tm, half), _rows(tm, half), _const((MIXED_P, D_MODEL)),
                  _const((1, D_MODEL))],
        out_specs=[_rows(tm, D_MODEL), _rows(tm, D_MODEL)],
        out_shape=[jax.ShapeDtypeStruct((t_all, D_MODEL), F32), jax.ShapeDtypeStruct((t_all, D_MODEL), BF16)],
        compiler_params=_params(("parallel",)),
    )(x2, oa, ob, wo_pad, g2)


def _mlp_fwd(hn, w_up_blocks, w_down, h, tgt):
    t_all = h.shape[0]
    tm = min(256, t_all)
    nj = D_FF // D_MODEL

    def body(a_ref, wu_ref, wd_ref, h_ref, t_ref, ru_ref, dy_ref, dyb_ref, loss_ref):
        @pl.when(pl.program_id(0) == 0)
        def _():
            loss_ref[...] = jnp.zeros_like(loss_ref)

        a = a_ref[...]
        y = h_ref[...]
        for j in range(nj):
            cols = slice(j * D_MODEL, (j + 1) * D_MODEL)
            ru = jnp.maximum(_dot(a, wu_ref[j]), 0.0)
            ru_ref[:, cols] = ru.astype(BF16)
            y = y + _dot((ru * ru).astype(BF16), wd_ref[cols, :])
        err = y - t_ref[...]
        loss_ref[...] += jnp.sum(err * err)
        dy = err * (1.0 / D_MODEL)
        dy_ref[...] = dy
        dyb_ref[...] = dy.astype(BF16)

    return _pcall(
        body, name="mlp_fwd", grid=(t_all // tm,),
        in_specs=[_rows(tm, D_MODEL), _const((nj, D_MODEL, D_MODEL)), _const((D_FF, D_MODEL)), _rows(tm, D_MODEL),
                  _rows(tm, D_MODEL)],
        out_specs=[_rows(tm, D_FF), _rows(tm, D_MODEL), _rows(tm, D_MODEL), _const((8, LANES))],
        out_shape=[jax.ShapeDtypeStruct((t_all, D_FF), BF16), jax.ShapeDtypeStruct((t_all, D_MODEL), F32),
                   jax.ShapeDtypeStruct((t_all, D_MODEL), BF16), jax.ShapeDtypeStruct((8, LANES), F32)],
        compiler_params=_params(("arbitrary",)),
    )(hn, w_up_blocks, w_down, h, tgt)


def _mlp_bwd_w(dyb, w_down, ru, hn):
    t_all = dyb.shape[0]
    tm = min(512, t_all)
    nj = D_FF // D_MODEL

    def body(dy_ref, w_ref, ru_ref, hn_ref, du_ref, dw_ref):
        @pl.when(pl.program_id(1) == 0)
        def _():
            dw_ref[...] = jnp.zeros_like(dw_ref)

        dy = dy_ref[...]
        ru = ru_ref[...].astype(F32)
        du = (_dot_nt(dy, w_ref[...]) * (2.0 * ru)).astype(BF16)
        du_ref[...] = du
        dw_ref[0] += _dot_tn(hn_ref[...], du)
        dw_ref[1] += _dot_tn((ru * ru).astype(BF16), dy)

    tok = pl.BlockSpec((tm, D_MODEL), lambda j, i: (i, 0))
    blk = pl.BlockSpec((tm, D_MODEL), lambda j, i: (i, j))
    wspec = pl.BlockSpec((2, None, D_MODEL, D_MODEL), lambda j, i: (0, j, 0, 0))
    return _pcall(
        body, name="mlp_bwd_w", grid=(nj, t_all // tm),
        in_specs=[tok, pl.BlockSpec((D_MODEL, D_MODEL), lambda j, i: (j, 0)), blk, tok],
        out_specs=[blk, wspec],
        out_shape=[jax.ShapeDtypeStruct((t_all, D_FF), BF16), jax.ShapeDtypeStruct((2, nj, D_MODEL, D_MODEL), F32)],
        compiler_params=_params(("parallel", "arbitrary")),
    )(dyb, w_down, ru, hn)


def _pair_exchange_copy(g_ref, out_ref, send_sem, recv_sem):
    x, y, c = _place()
    return pltpu.make_async_remote_copy(
        src_ref=g_ref.at[1 - c], dst_ref=out_ref, send_sem=send_sem, recv_sem=recv_sem,
        device_id=(x, y, 1 - c), device_id_type=MESH)


def _mlp_dhn(du, w_up_blocks, h, dy, g2, d_w_mlp):
    t_all = h.shape[0]
    tm = min(256, t_all)
    n_steps = t_all // tm

    def body(a_ref, w_ref, h_ref, dy_ref, g_ref, dw_ref, dh_ref, dhb_ref, dg_ref, got_ref, send_sem, recv_sem):
        @pl.when(pl.program_id(0) == 0)
        def _():
            dg_ref[...] = jnp.zeros_like(dg_ref)
            _pair_exchange_copy(dw_ref, got_ref, send_sem, recv_sem).start()

        dhn = _dot_nt(a_ref[:, :D_MODEL], w_ref[0])
        for j in range(1, D_FF // D_MODEL):
            dhn = dhn + _dot_nt(a_ref[:, j * D_MODEL:(j + 1) * D_MODEL], w_ref[j])
        h = h_ref[...]
        r = lax.rsqrt(jnp.mean(h * h, axis=-1, keepdims=True) + EPS)
        hh = h * r
        dg_ref[...] += jnp.sum(dhn * hh, axis=0, keepdims=True)
        dz = dhn * g_ref[...]
        dh = dy_ref[...] + r * (dz - hh * jnp.mean(dz * hh, axis=-1, keepdims=True))
        dh_ref[...] = dh
        dhb_ref[...] = dh.astype(BF16)

        @pl.when(pl.program_id(0) == n_steps - 1)
        def _():
            _pair_exchange_copy(dw_ref, got_ref, send_sem, recv_sem).wait()

    return _pcall(
        body, name="mlp_dhn", grid=(n_steps,),
        in_specs=[_rows(tm, D_FF), _const((D_FF // D_MODEL, D_MODEL, D_MODEL)), _rows(tm, D_MODEL),
                  _rows(tm, D_MODEL), _const((1, D_MODEL)), ANY],
        out_specs=[_rows(tm, D_MODEL), _rows(tm, D_MODEL), _const((1, D_MODEL)), ANY],
        out_shape=[jax.ShapeDtypeStruct((t_all, D_MODEL), F32), jax.ShapeDtypeStruct((t_all, D_MODEL), BF16),
                   jax.ShapeDtypeStruct((1, D_MODEL), F32), jax.ShapeDtypeStruct(d_w_mlp.shape[1:], F32)],
        scratch_shapes=[pltpu.SemaphoreType.DMA, pltpu.SemaphoreType.DMA],
        compiler_params=_params(("arbitrary",)),
    )(du, w_up_blocks, h, dy, g2, d_w_mlp)


def _dmixed(dhb, wo_pad, oa, ob):
    t_all = dhb.shape[0]
    tm = min(512, t_all)
    half = A_HEADS * LANES

    def body(a_ref, w_ref, oa_ref, ob_ref, da_ref, db_ref, delta_ref, dwo_ref):
        @pl.when(pl.program_id(0) == 0)
        def _():
            dwo_ref[...] = jnp.zeros_like(dwo_ref)

        a = a_ref[...]
        d = _dot_nt(a, w_ref[...])
        da_ref[...] = d[:, :half].astype(BF16)
        db_ref[...] = d[:, half:].astype(BF16)
        for h in range(B_HEADS):
            cols = slice(h * LANES, (h + 1) * LANES)
            prod = d[:, half + h * LANES:half + (h + 1) * LANES] * ob_ref[:, cols].astype(F32)
            delta_ref[:, cols] = jnp.broadcast_to(jnp.sum(prod, axis=-1, keepdims=True), (tm, LANES))
        dwo_ref[:half, :] += _dot_tn(oa_ref[...], a)
        dwo_ref[half:, :] += _dot_tn(ob_ref[...], a)

    return _pcall(
        body, name="dmixed", grid=(t_all // tm,),
        in_specs=[_rows(tm, D_MODEL), _const((MIXED_P, D_MODEL)), _rows(tm, half), _rows(tm, half)],
        out_specs=[_rows(tm, half), _rows(tm, half), _rows(tm, half), _const((MIXED_P, D_MODEL))],
        out_shape=[jax.ShapeDtypeStruct((t_all, half), BF16), jax.ShapeDtypeStruct((t_all, half), BF16),
                   jax.ShapeDtypeStruct((t_all, half), F32), jax.ShapeDtypeStruct((MIXED_P, D_MODEL), F32)],
        compiler_params=_params(("arbitrary",)),
    )(dhb, wo_pad, oa, ob)


def _fox_bwd(qb, kb, vb, dob, lse, delta, nb, seq, pair_sums):
    t_all = qb.shape[0]
    tk = min(FOX_TK, seq // FOX_PARTS_BWD)
    tq = FOX_PARTS_BWD * tk
    nk = seq // tk

    def body(q_ref, k_ref, v_ref, do_ref, lse_ref, delta_ref, pair_ref, dq_ref, dk_ref, dv_ref, got_ref,
             s_ref, dp_ref, p_ref, ds_ref, dk_acc, dv_acc, send_sems, recv_sems):
        kj = pl.program_id(2)
        bh = pl.program_id(0) * B_HEADS + pl.program_id(1)

        @pl.when((bh == 0) & (kj == 0))
        def _():
            for cp in _chip_exchange_copies(pair_ref, got_ref, send_sems, recv_sems):
                cp.start()

        @pl.when(kj == 0)
        def _():
            dq_ref[...] = jnp.zeros_like(dq_ref)

        dk_acc[...] = jnp.zeros_like(dk_acc)
        dv_acc[...] = jnp.zeros_like(dv_acc)
        k = k_ref[...]
        v = v_ref[...]

        def block(off, r0, r1, masked):
            qrows = pl.ds(pl.multiple_of(off + r0, CHUNK), r1 - r0)
            q = q_ref[qrows, :]
            do = do_ref[qrows, :]
            s_ref[r0:r1, :] = _dot_nt(q, k)
            dp_ref[r0:r1, :] = _dot_nt(do, v)
            for r in range(r0, r1, CHUNK):
                rows = slice(r, r + CHUNK)
                chunk = pl.ds(pl.multiple_of(off + r, CHUNK), CHUNK)
                lse_c = lse_ref[chunk, :]
                delta_c = delta_ref[chunk, :]
                for jt in range(tk // LANES):
                    cols = slice(jt * LANES, (jt + 1) * LANES)
                    p = jnp.exp2(s_ref[rows, cols] - lse_c)
                    if masked:
                        row = r - r0 + lax.broadcasted_iota(jnp.int32, (CHUNK, LANES), 0)
                        col = jt * LANES + lax.broadcasted_iota(jnp.int32, (CHUNK, LANES), 1)
                        p = jnp.where(row >= col, p, 0.0)
                    p_ref[rows, cols] = p.astype(BF16)
                    ds_ref[rows, cols] = (p * (dp_ref[rows, cols] - delta_c)).astype(BF16)
            dv_acc[...] += _dot_tn(p_ref[r0:r1, :], do)
            dk_acc[...] += _dot_tn(ds_ref[r0:r1, :], q)
            dq_ref[qrows, :] += _dot(ds_ref[r0:r1, :], k)

        first = kj // FOX_PARTS_BWD
        off_first = pl.multiple_of(first * tq, tq)
        for d in range(FOX_PARTS_BWD):
            @pl.when(kj % FOX_PARTS_BWD == d)
            def _(d=d):
                block(off_first, d * tk, (d + 1) * tk, True)
                if d < FOX_PARTS_BWD - 1:
                    block(off_first, (d + 1) * tk, tq, False)

        def later(i, carry):
            block(pl.multiple_of(i * tq, tq), 0, tq, False)
            return carry

        lax.fori_loop(first + 1, seq // tq, later, 0)
        dk_ref[...] = dk_acc[...]
        dv_ref[...] = dv_acc[...]

        @pl.when((bh == nb * B_HEADS - 1) & (kj == nk - 1))
        def _():
            for cp in _chip_exchange_copies(pair_ref, got_ref, send_sems, recv_sems):
                cp.wait()

    full = pl.BlockSpec((seq, LANES), lambda b, h, j: (b, h))
    tile = pl.BlockSpec((tk, LANES), lambda b, h, j: (b * nk + j, h))
    shp = jax.ShapeDtypeStruct((t_all, B_HEADS * LANES), F32)
    return _pcall(
        body, name="fox_bwd", grid=(nb, B_HEADS, nk),
        in_specs=[full, tile, tile, full, full, full, ANY], out_specs=[full, tile, tile, ANY],
        out_shape=[shp, shp, shp, jax.ShapeDtypeStruct((3,) + pair_sums.shape[1:], pair_sums.dtype)],
        scratch_shapes=[pltpu.VMEM((tq, tk), F32), pltpu.VMEM((tq, tk), F32), pltpu.VMEM((tq, tk), BF16),
                        pltpu.VMEM((tq, tk), BF16), pltpu.VMEM((tk, LANES), F32), pltpu.VMEM((tk, LANES), F32),
                        pltpu.SemaphoreType.DMA((3,)), pltpu.SemaphoreType.DMA((3,))],
        compiler_params=_params(("arbitrary", "arbitrary", "arbitrary")),
    )(qb, kb, vb, dob, lse, delta, pair_sums)


def _swa_bwd(qa, ka, va, oa, doa, lrow, sinks, bias, nb, seq):
    t_all = qa.shape[0]
    tq = min(512, seq)
    nq = seq // tq

    def body(sink_ref, q_ref, k_ref, v_ref, bias0_ref, bias_ref, o_ref, do_ref, l_ref,
             dq_ref, dk_ref, dv_ref, dsink_ref):
        qi = pl.program_id(2)
        sink = _sink_rows(sink_ref, pl.program_id(1))

        @pl.when(qi == 0)
        def _():
            dk_ref[...] = jnp.zeros_like(dk_ref)
            dv_ref[...] = jnp.zeros_like(dv_ref)
            dsink_ref[...] = jnp.zeros_like(dsink_ref)

        for a in range(tq // WINDOW):
            t0 = qi * tq + a * WINDOW
            start = pl.multiple_of(jnp.maximum(t0 - WINDOW, 0), WINDOW)
            rows = slice(a * WINDOW, (a + 1) * WINDOW)
            win = pl.ds(start, 2 * WINDOW)
            q = _stack_heads(q_ref, rows)
            k = k_ref[win, :]
            v = v_ref[win, :]
            do = _stack_heads(do_ref, rows)
            lrow = _stack_heads(l_ref, rows)
            s = _dot_nt(q, k) + (bias0_ref if a == 0 else bias_ref)[...]
            dp = _dot_nt(do, v)
            delta = _rep(jnp.sum(do.astype(F32) * _stack_heads(o_ref, rows).astype(F32), axis=-1, keepdims=True))
            p = [jnp.exp(s[:, t * LANES:(t + 1) * LANES] - lrow) for t in range(2)]
            ds = jnp.concatenate([(p[t] * (dp[:, t * LANES:(t + 1) * LANES] - delta)).astype(BF16) for t in range(2)],
                                 axis=1)
            dq = _dot(ds, k)
            dk_ref[win, :] += _dot_tn(ds, q)
            dv_ref[win, :] += _dot_tn(jnp.concatenate([p[0].astype(BF16), p[1].astype(BF16)], axis=1), do)
            sink_term = jnp.exp(sink - lrow) * delta
            for j in range(A_GROUP):
                part = slice(j * WINDOW, (j + 1) * WINDOW)
                dq_ref[rows, j * LANES:(j + 1) * LANES] = dq[part]
                dsink_ref[j:j + 1, :] -= jnp.sum(sink_term[part], axis=0, keepdims=True)

    smem, qspec, kspec, bias_first, bias_rest = _swa_specs(nq, tq, seq)
    return _pcall(
        body, name="swa_bwd", grid=(nb, A_KV_HEADS, nq),
        in_specs=[smem, qspec, kspec, kspec, bias_first, bias_rest, qspec, qspec, qspec],
        out_specs=[qspec, kspec, kspec, pl.BlockSpec((None, 8, LANES), lambda b, g, i: (b * A_KV_HEADS + g, 0, 0))],
        out_shape=[jax.ShapeDtypeStruct((t_all, A_HEADS * LANES), F32),
                   jax.ShapeDtypeStruct((t_all, A_KV_HEADS * LANES), F32),
                   jax.ShapeDtypeStruct((t_all, A_KV_HEADS * LANES), F32),
                   jax.ShapeDtypeStruct((nb * A_KV_HEADS, 8, LANES), F32)],
        compiler_params=_params(("parallel", "parallel", "arbitrary")),
    )(sinks, qa, ka, va, bias, bias, oa, doa, lrow)


def _dproj_dx(pre, dqa, dka, dqb, dkb, dva, dvb, z, x2, dh, gain_row, w_pad_t, g1, seq):
    t_all = pre.shape[0]
    tm = min(256, seq)
    nt = t_all // tm
    tiles_per_seq = seq // tm
    triu = _tri(tm, True)
    sel = _dc_select()

    def body(pre_ref, dqa_ref, dka_ref, dqb_ref, dkb_ref, dva_ref, dvb_ref, z_ref, x_ref, dh_ref, gain_ref, triu_ref,
             sel_ref, w_ref, g_ref, dproj_ref, small_ref, dx_ref, dg_ref, carry_ref):
        i = pl.program_id(0)

        @pl.when(i == 0)
        def _():
            small_ref[...] = jnp.zeros_like(small_ref)
            dg_ref[...] = jnp.zeros_like(dg_ref)

        @pl.when(i % tiles_per_seq == 0)
        def _():
            carry_ref[...] = jnp.zeros_like(carry_ref)

        def norm_bwd(g, dhat):
            cols = slice(g * LANES, (g + 1) * LANES)
            p = pre_ref[:, cols].astype(F32)
            rr = lax.rsqrt(jnp.sum(p * p, axis=-1, keepdims=True) * (1.0 / HEAD_DIM) + EPS)
            n = p * rr
            dz = dhat * gain_ref[:, cols]
            dproj_ref[:, cols] = (rr * (dz - n * (jnp.sum(dz * n, axis=-1, keepdims=True) * (1.0 / HEAD_DIM)))
                                  ).astype(BF16)
            return jnp.sum(dhat * n, axis=0, keepdims=True)

        def group_sum(g0, d_ref, count, scale):
            acc = jnp.zeros((1, LANES), F32)
            for h in range(count):
                d = d_ref[:, h * LANES:(h + 1) * LANES]
                acc = acc + norm_bwd(g0 + h, d * scale if scale != 1.0 else d)
            return acc

        small_ref[0:1, :] += group_sum(G_QA, dqa_ref, A_HEADS, SCALE)
        small_ref[1:2, :] += group_sum(G_KA, dka_ref, A_KV_HEADS, 1.0)
        small_ref[2:3, :] += group_sum(G_QB, dqb_ref, B_HEADS, SCALE)
        small_ref[3:4, :] += group_sum(G_KB, dkb_ref, B_HEADS, LN2)
        dproj_ref[:, G_VA * LANES:G_VB * LANES] = dva_ref[...].astype(BF16)
        dproj_ref[:, G_VB * LANES:G_F * LANES] = dvb_ref[...].astype(BF16)

        dc = jnp.zeros((tm, LANES), F32)
        for piece_q, piece_k in zip(_split3(dqb_ref[...]), _split3(dkb_ref[...])):
            dc = dc + _dot(jnp.concatenate([piece_q, piece_k], axis=1), sel_ref[...])
        dlf = _dot_exact(triu_ref[...], dc) + carry_ref[...]
        carry_ref[...] += jnp.sum(dc, axis=0, keepdims=True)
        dz = dlf / (1.0 + jnp.exp(z_ref[...]))
        small_ref[4:5, :] += jnp.sum(dz, axis=0, keepdims=True)
        dproj_ref[:, G_F * LANES:(G_F + 1) * LANES] = dz.astype(BF16)
        dproj_ref[:, (G_F + 1) * LANES:] = jnp.zeros((tm, LANES), BF16)

        dxn = _dot(dproj_ref[...], w_ref[...])
        x = x_ref[...]
        r = lax.rsqrt(jnp.mean(x * x, axis=-1, keepdims=True) + EPS)
        xh = x * r
        dg_ref[...] += jnp.sum(dxn * xh, axis=0, keepdims=True)
        dxz = dxn * g_ref[...]
        dx_ref[...] = dh_ref[...] + r * (dxz - xh * jnp.mean(dxz * xh, axis=-1, keepdims=True))

    def rev(n):
        return pl.BlockSpec((tm, n), lambda i: (nt - 1 - i, 0))

    return _pcall(
        body, name="dproj_dx", grid=(nt,),
        in_specs=[rev(N_NORM_GROUPS * LANES), rev(A_HEADS * LANES), rev(A_KV_HEADS * LANES), rev(B_HEADS * LANES),
                  rev(B_HEADS * LANES), rev(A_KV_HEADS * LANES), rev(B_HEADS * LANES), rev(LANES), rev(D_MODEL),
                  rev(D_MODEL), _const((1, NP)), _const((tm, tm)), _const(sel.shape), _const((NP, D_MODEL)),
                  _const((1, D_MODEL))],
        out_specs=[rev(NP), _const((8, LANES)), rev(D_MODEL), _const((1, D_MODEL))],
        out_shape=[jax.ShapeDtypeStruct((t_all, NP), BF16), jax.ShapeDtypeStruct((8, LANES), F32),
                   jax.ShapeDtypeStruct((t_all, D_MODEL), F32), jax.ShapeDtypeStruct((1, D_MODEL), F32)],
        scratch_shapes=[pltpu.VMEM((1, LANES), F32)],
        compiler_params=_params(("arbitrary",)),
    )(pre, dqa, dka, dqb, dkb, dva, dvb, z, x2, dh, gain_row, triu, sel, w_pad_t, g1)


def _dwin(dproj, xn):
    t_all = xn.shape[0]
    tt = min(512, t_all)
    half = NP // 2

    def body(a_ref, b_ref, o_ref):
        @pl.when(pl.program_id(1) == 0)
        def _():
            o_ref[...] = jnp.zeros_like(o_ref)

        o_ref[...] += _dot_tn(a_ref[...], b_ref[...])

    return _pcall(
        body, name="dwin", grid=(2, t_all // tt),
        in_specs=[pl.BlockSpec((tt, half), lambda j, t: (t, j)), pl.BlockSpec((tt, D_MODEL), lambda j, t: (t, 0))],
        out_specs=pl.BlockSpec((half, D_MODEL), lambda j, t: (j, 0)),
        out_shape=jax.ShapeDtypeStruct((NP, D_MODEL), F32),
        compiler_params=_params(("parallel", "arbitrary")),
    )(dproj, xn)


ANY = pl.BlockSpec(memory_space=pl.ANY)


def _place():
    return lax.axis_index("x"), lax.axis_index("y"), lax.axis_index("c")


class _Gather:
    def __init__(self, srcs, outs, send_sems, recv_sems, local_sems):
        self.srcs, self.outs = srcs, outs
        self.send_sems, self.recv_sems, self.local_sems = send_sems, recv_sems, local_sems
        x, y, c = _place()
        self.c = c
        self.me, self.sibling = (x, y, c), (x, y, 1 - c)
        self.chips = [(1 - x, y), (x, 1 - y), (1 - x, 1 - y)]

    def _rows(self, a, px, py, pc):
        m = self.srcs[a].shape[0]
        return self.outs[a].at[pl.ds((4 * px + 2 * py + pc) * m, m), :]

    def _copy(self, a, k, block, to, from_src=False):
        return pltpu.make_async_remote_copy(
            src_ref=self.srcs[a] if from_src else self._rows(a, *block), dst_ref=self._rows(a, *block),
            send_sem=self.send_sems.at[k, a], recv_sem=self.recv_sems.at[k, a], device_id=to, device_id_type=MESH)

    def _own(self, a):
        return pltpu.make_async_copy(self.srcs[a], self._rows(a, *self.me), self.local_sems.at[a])

    def start(self):
        for a in range(len(self.srcs)):
            self._own(a).start()
            self._copy(a, 0, self.me, self.sibling, from_src=True).start()
            for j, chip in enumerate(self.chips):
                self._copy(a, 1 + j, self.me, (*chip, self.c), from_src=True).start()

    def forward(self):
        for a in range(len(self.srcs)):
            for j, chip in enumerate(self.chips):
                self._copy(a, 1 + j, (*chip, self.c), self.me).wait_recv()
                self._copy(a, 4 + j, (*chip, self.c), self.sibling).start()

    def finish(self):
        for a in range(len(self.srcs)):
            self._copy(a, 0, self.sibling, self.me).wait_recv()
            for j, chip in enumerate(self.chips):
                self._copy(a, 4 + j, (*chip, 1 - self.c), self.me).wait_recv()
            self._copy(a, 0, self.me, self.sibling, from_src=True).wait_send()
            for j, chip in enumerate(self.chips):
                self._copy(a, 1 + j, self.me, (*chip, self.c), from_src=True).wait_send()
                self._copy(a, 4 + j, (*chip, self.c), self.sibling).wait_send()
            self._own(a).wait()


def _gather_scratch(n_arrays):
    return [pltpu.SemaphoreType.DMA((7, n_arrays)), pltpu.SemaphoreType.DMA((7, n_arrays)),
            pltpu.SemaphoreType.DMA((n_arrays,))]


def _allgather_halves(mine):
    m_per, n = mine.shape

    def body(x_ref, out_ref, send_sems, recv_sems, local_sems):
        gather = _Gather((x_ref,), (out_ref,), send_sems, recv_sems, local_sems)
        gather.start()
        gather.forward()
        gather.finish()

    return _pcall(
        body, name="allgather_w_in",
        out_shape=jax.ShapeDtypeStruct((8 * m_per, n), mine.dtype),
        in_specs=[ANY], out_specs=ANY, scratch_shapes=_gather_scratch(1),
    )(mine)


def _rs_pair_exchange(g, name):
    def body(g_ref, out_ref, send_sem, recv_sem):
        x, y, c = _place()
        cp = pltpu.make_async_remote_copy(
            src_ref=g_ref.at[1 - c], dst_ref=out_ref, send_sem=send_sem, recv_sem=recv_sem,
            device_id=(x, y, 1 - c), device_id_type=MESH)
        cp.start()
        cp.wait()

    return _pcall(
        body, name=name, out_shape=jax.ShapeDtypeStruct(g.shape[1:], F32),
        in_specs=[ANY], out_specs=ANY, scratch_shapes=[pltpu.SemaphoreType.DMA, pltpu.SemaphoreType.DMA],
    )(g)


def _rs_pair_add(g, got, c_idx, name):
    rows = g.shape[2]

    def body(c_ref, a_ref, b_ref, o_ref, ob_ref):
        pair = a_ref[...] + b_ref[...]
        o_ref[...] = pair
        ob_ref[...] = pair.astype(BF16)

    blk = pl.BlockSpec((None, rows, D_MODEL), lambda s, c_ref: (s, 0, 0))
    return _pcall(
        body, name=name,
        grid_spec=pltpu.PrefetchScalarGridSpec(
            num_scalar_prefetch=1, grid=(N_CHIPS,),
            in_specs=[pl.BlockSpec((None, None, rows, D_MODEL), lambda s, c_ref: (c_ref[0], s, 0, 0)), blk],
            out_specs=[blk, blk]),
        out_shape=[jax.ShapeDtypeStruct((N_CHIPS, rows, D_MODEL), F32),
                   jax.ShapeDtypeStruct((N_CHIPS, rows, D_MODEL), BF16)],
        compiler_params=_params(("parallel",)),
    )(c_idx, g, got)


def _chip_exchange_copies(p_ref, out_ref, send_sems, recv_sems):
    x, y, c = _place()
    chips = [(1 - x, y), (x, 1 - y), (1 - x, 1 - y)]
    return [pltpu.make_async_remote_copy(
        src_ref=p_ref.at[2 * cx + cy], dst_ref=out_ref.at[j], send_sem=send_sems.at[j], recv_sem=recv_sems.at[j],
        device_id=(cx, cy, c), device_id_type=MESH) for j, (cx, cy) in enumerate(chips)]


def _rs_chip_exchange(p4, name):
    def body(p_ref, out_ref, send_sems, recv_sems):
        cps = _chip_exchange_copies(p_ref, out_ref, send_sems, recv_sems)
        for cp in cps:
            cp.start()
        for cp in cps:
            cp.wait()

    return _pcall(
        body, name=name, out_shape=jax.ShapeDtypeStruct((3,) + p4.shape[1:], p4.dtype),
        in_specs=[ANY], out_specs=ANY,
        scratch_shapes=[pltpu.SemaphoreType.DMA((3,)), pltpu.SemaphoreType.DMA((3,))],
    )(p4)


def _rs_chip_add(p4, got, sc_idx, name):
    rows = p4.shape[1]
    tr = next(rows // n for n in (8, 7, 6, 5, 4, 3, 2, 1) if rows % n == 0 and (rows // n) % 16 == 0)

    def body(sc_ref, a_ref, b_ref, o_ref):
        o_ref[...] = ((a_ref[...] + b_ref[0].astype(F32)) + b_ref[1].astype(F32)) + b_ref[2].astype(F32)

    return _pcall(
        body, name=name,
        grid_spec=pltpu.PrefetchScalarGridSpec(
            num_scalar_prefetch=1, grid=(rows // tr,),
            in_specs=[pl.BlockSpec((None, tr, D_MODEL), lambda i, sc_ref: (sc_ref[0], i, 0)),
                      pl.BlockSpec((3, tr, D_MODEL), lambda i, sc_ref: (0, i, 0))],
            out_specs=pl.BlockSpec((None, tr, D_MODEL), lambda i, sc_ref: (sc_ref[1], i, 0))),
        out_shape=jax.ShapeDtypeStruct((2, rows, D_MODEL), F32),
        compiler_params=_params(("parallel",)),
    )(sc_idx, p4, got)


def _rs_pair_share(halves, name):
    def body(r_ref, out_ref, send_sem, recv_sem):
        x, y, c = _place()
        cp = pltpu.make_async_remote_copy(
            src_ref=r_ref.at[c], dst_ref=out_ref.at[c], send_sem=send_sem, recv_sem=recv_sem,
            device_id=(x, y, 1 - c), device_id_type=MESH)
        cp.start()
        cp.wait()

    return _pcall(
        body, name=name, out_shape=jax.ShapeDtypeStruct(halves.shape, F32),
        in_specs=[ANY], out_specs=ANY, input_output_aliases={0: 0},
        scratch_shapes=[pltpu.SemaphoreType.DMA, pltpu.SemaphoreType.DMA],
    )(halves)


def _adam(w, g, m, v):
    m2 = ADAM_B1 * m + (1.0 - ADAM_B1) * g
    v2 = ADAM_B2 * v + (1.0 - ADAM_B2) * (g * g)
    m_hat = m2 / (1.0 - ADAM_B1 ** ADAM_STEP)
    v_hat = v2 / (1.0 - ADAM_B2 ** ADAM_STEP)
    return -ADAM_LR * (m_hat / (jnp.sqrt(v_hat) + ADAM_EPS) + ADAM_WD * w), m2, v2


def _small_allreduce(part):
    def body(p_ref, g_ref, buf, send_sems, recv_sems):
        x, y, c = _place()
        me = 4 * x + 2 * y + c
        cps = []
        for k in range(1, 8):
            peer = (1 - x if k & 4 else x, 1 - y if k & 2 else y, 1 - c if k & 1 else c)
            cps.append(pltpu.make_async_remote_copy(
                src_ref=p_ref, dst_ref=buf.at[me], send_sem=send_sems.at[k - 1], recv_sem=recv_sems.at[k - 1],
                device_id=peer, device_id_type=MESH))
        for cp in cps:
            cp.start()
        buf[me] = p_ref[...]
        for cp in cps:
            cp.wait()
        g = buf[0]
        for k in range(1, 8):
            g = g + buf[k]
        g_ref[...] = g

    vm = pl.BlockSpec(memory_space=pltpu.VMEM)
    return _pcall(
        body, name="small_allreduce",
        out_shape=jax.ShapeDtypeStruct((SMALL_ROWS, LANES), F32), in_specs=[vm], out_specs=vm,
        scratch_shapes=[pltpu.VMEM((8, SMALL_ROWS, LANES), F32), pltpu.SemaphoreType.DMA((7,)),
                        pltpu.SemaphoreType.DMA((7,))],
    )(part)


def _small_adamw(g, w, m, v):
    pieces = ((0, 8, LANES), (20, 1, B_HEADS), (16, 1, HEAD_DIM), (17, 1, HEAD_DIM), (21, 1, A_HEADS),
              (18, 1, HEAD_DIM), (19, 1, HEAD_DIM), (8, 8, LANES))

    def body(g_ref, w_ref, m_ref, v_ref, *rest):
        outs, (loss_ref, stage) = rest[:4 * len(pieces)], rest[4 * len(pieces):]
        g = g_ref[...]
        loss_ref[...] = g_ref[ROW_LOSS:ROW_LOSS + 1, :]
        for kind, packed in enumerate((g,) + _adam(w_ref[...], g, m_ref[...], v_ref[...])):
            stage[...] = packed
            for i, (row, rows, lanes) in enumerate(pieces):
                outs[kind * len(pieces) + i][...] = stage[row:row + rows, 0:lanes]

    vm = pl.BlockSpec(memory_space=pltpu.VMEM)
    shapes = [jax.ShapeDtypeStruct((rows, lanes), F32) for _ in range(4) for _, rows, lanes in pieces]
    shapes.append(jax.ShapeDtypeStruct((1, LANES), F32))
    res = _pcall(
        body, name="small_adamw", out_shape=shapes, in_specs=[vm, vm, vm, vm], out_specs=[vm] * len(shapes),
        scratch_shapes=[pltpu.VMEM((SMALL_ROWS, LANES), F32)],
    )(g, w, m, v)
    flat = [r.reshape(r.size) for r in res[:-1]]
    n = len(pieces)
    return [flat[k * n:(k + 1) * n] for k in range(4)], res[-1][0, 0]


def _adamw(w, g, m, v, name):
    rows, cols = w.shape
    tr = min(256, rows)

    def body(w_ref, g_ref, m_ref, v_ref, d_ref, m2_ref, v2_ref):
        d_ref[...], m2_ref[...], v2_ref[...] = _adam(w_ref[...], g_ref[...], m_ref[...], v_ref[...])

    spec = _rows(tr, cols)
    shp = jax.ShapeDtypeStruct((rows, cols), F32)
    return _pcall(
        body, name=name, grid=(rows // tr,), in_specs=[spec] * 4, out_specs=[spec] * 3, out_shape=[shp] * 3,
        compiler_params=_params(("parallel",)),
    )(w, g, m, v)


def _pad_lanes(v):
    return jnp.pad(v, (0, LANES - v.shape[0]))


def _pad_head_rows(w_t, heads):
    n = w_t.shape[1]
    return jnp.pad(w_t.reshape(heads, HEAD_DIM, n), ((0, 0), (0, LANES - HEAD_DIM), (0, 0))).reshape(heads * LANES, n)


def _unpad_head_rows(w_t, heads):
    n = w_t.shape[1]
    return w_t.reshape(heads, LANES, n)[:, :HEAD_DIM].reshape(heads * HEAD_DIM, n)


def _in_rows_pad(w_in_t):
    qa, ka, va, qb, kb, vb, f = jnp.split(w_in_t, [512, 640, 768, 1280, 1792, 2304], axis=0)
    f = jnp.pad(f, ((0, 2 * LANES - B_HEADS), (0, 0)))
    return jnp.concatenate([_pad_head_rows(qa, 8), _pad_head_rows(ka, 2), _pad_head_rows(qb, 8),
                            _pad_head_rows(kb, 8), _pad_head_rows(va, 2), _pad_head_rows(vb, 8), f], axis=0)


def _in_rows_unpad(d):
    qa = _unpad_head_rows(d[G_QA * LANES:G_KA * LANES], 8)
    ka = _unpad_head_rows(d[G_KA * LANES:G_QB * LANES], 2)
    qb = _unpad_head_rows(d[G_QB * LANES:G_KB * LANES], 8)
    kb = _unpad_head_rows(d[G_KB * LANES:G_VA * LANES], 8)
    va = _unpad_head_rows(d[G_VA * LANES:G_VB * LANES], 2)
    vb = _unpad_head_rows(d[G_VB * LANES:G_F * LANES], 8)
    f = d[G_F * LANES:G_F * LANES + B_HEADS]
    return jnp.concatenate([qa, ka, va, qb, kb, vb, f], axis=0)


def _pack_small(g1, bf, qa, ka, sk, qb, kb, g2, loss_row):
    rows = [g1.reshape(8, LANES), g2.reshape(8, LANES)]
    rows += [_pad_lanes(t)[None] for t in (qa, ka, qb, kb, bf, sk)]
    rows += [loss_row, jnp.zeros((1, LANES), F32)]
    return jnp.concatenate(rows, axis=0)


def kernel(x, attn_norm_g, w_in, b_forget, q_norm_a, k_norm_a, sink_logits, q_norm_b, k_norm_b, w_out, mlp_norm_g, w_up, w_down, loss_target, m_attn_norm_g, m_w_in, m_b_forget, m_q_norm_a, m_k_norm_a, m_sink_logits, m_q_norm_b, m_k_norm_b, m_w_out, m_mlp_norm_g, m_w_up, m_w_down, v_attn_norm_g, v_w_in, v_b_forget, v_q_norm_a, v_k_norm_a, v_sink_logits, v_q_norm_b, v_k_norm_b, v_w_out, v_mlp_norm_g, v_w_up, v_w_down):
    nb, seq, _ = x.shape
    t_all = nb * seq
    c_idx = lax.axis_index("c")
    s_idx = 2 * lax.axis_index("x") + lax.axis_index("y")

    def my_half(a):
        halves = a.astype(BF16).reshape(2, a.shape[0] // 2, a.shape[1])
        return lax.dynamic_slice_in_dim(halves, c_idx, 1, axis=0)[0]

    w_in_shard_t = jnp.pad(w_in.T, ((0, IN_SHARD_P - IN_SHARD), (0, 0)))
    gathered_in = _allgather_halves(my_half(w_in_shard_t)).reshape(N_CHIPS, IN_SHARD_P, D_MODEL)
    w_pad_t = _in_rows_pad(gathered_in[:, :IN_SHARD].reshape(IN_WIDTH, D_MODEL))

    ones = jnp.ones((LANES,), F32)
    gain_row = jnp.concatenate(
        [jnp.tile(_pad_lanes(q_norm_a), 8), jnp.tile(_pad_lanes(k_norm_a), 2), jnp.tile(_pad_lanes(q_norm_b), 8),
         jnp.tile(_pad_lanes(k_norm_b), 8), jnp.tile(ones, N_GROUPS - N_NORM_GROUPS)])[None]
    b_row = _pad_lanes(b_forget)[None]
    g1 = attn_norm_g[None]
    g2 = mlp_norm_g[None]
    slopes = jnp.exp2(-(8.0 / A_HEADS) * (jnp.arange(A_HEADS, dtype=F32) + 1.0))

    x2 = x.reshape(t_all, D_MODEL)
    tgt = loss_target.reshape(t_all, D_MODEL)

    (xn, pre, qa, ka, va, qb, kb, vb, z), (w_out_g, w_up_g, w_down_f) = _inproj(
        x2, g1, w_pad_t, gain_row, b_row, seq, [my_half(w_out), my_half(w_up), my_half(w_down)])
    wo_pad = _pad_head_rows(w_out_g, A_HEADS + B_HEADS)
    w_up_blocks = w_up_g.reshape(N_CHIPS, D_MODEL, D_MODEL)
    swa_bias = _swa_bias(slopes)
    oa, la = _swa_fwd(qa, ka, va, sink_logits, swa_bias, nb, seq)
    ob, lse = _fox_fwd(qb, kb, vb, nb, seq)
    h, hn = _outproj(x2, oa, ob, wo_pad, g2)
    ru, dy, dyb, loss_acc = _mlp_fwd(hn, w_up_blocks, w_down_f, h, tgt)

    du, d_w_mlp = _mlp_bwd_w(dyb, w_down_f, ru, hn)
    c_arg = c_idx.reshape(1).astype(jnp.int32)
    sc_arg = jnp.stack([s_idx, c_idx]).astype(jnp.int32)
    dh, dhb, d_g2, sibling_w_mlp = _mlp_dhn(du, w_up_blocks, h, dy, g2, d_w_mlp)
    pair_m, pair_m_bf = _rs_pair_add(d_w_mlp, sibling_w_mlp, c_arg, "rs_pair_add_mlp")
    doa, dob, delta_b, d_wo = _dmixed(dhb, wo_pad, oa, ob)
    dqb, dkb, dvb, got_m = _fox_bwd(qb, kb, vb, dob, lse, delta_b, nb, seq, pair_m_bf)
    red_m = _rs_pair_share(_rs_chip_add(pair_m, got_m, sc_arg, "rs_chip_add_mlp"), "rs_pair_share_mlp")
    g_w_up, g_w_down = red_m[0], red_m[1]
    dqa, dka, dva, dsink = _swa_bwd(qa, ka, va, oa, doa, la, sink_logits, swa_bias, nb, seq)
    dproj, small, grad_x, d_g1 = _dproj_dx(pre, dqa, dka, dqb, dkb, dva, dvb, z, x2, dh, gain_row, w_pad_t, g1, seq)
    d_w_in_t = _dwin(dproj, xn)

    d_w_out = _unpad_head_rows(d_wo, A_HEADS + B_HEADS)
    d_w_in_rows = _in_rows_unpad(d_w_in_t)
    half = R_ATT // 2
    lower, upper = [], []
    for s in range(N_CHIPS):
        in_s = d_w_in_rows[s * IN_SHARD:(s + 1) * IN_SHARD]
        out_s = d_w_out[s * (D_MODEL // N_CHIPS):(s + 1) * (D_MODEL // N_CHIPS)]
        lower.append(in_s[:half])
        upper.append(jnp.concatenate([jnp.pad(in_s[half:], ((0, IN_SHARD_P - IN_SHARD), (0, 0))), out_s], axis=0))
    g_att = jnp.stack([jnp.stack(lower), jnp.stack(upper)])
    pair_a, pair_a_bf = _rs_pair_add(g_att, _rs_pair_exchange(g_att, "rs_pair_exchange_att"), c_arg, "rs_pair_add_att")
    got_a = _rs_chip_exchange(pair_a_bf, "rs_chip_exchange_att")
    red_a = _rs_pair_share(_rs_chip_add(pair_a, got_a, sc_arg, "rs_chip_add_att"), "rs_pair_share_att")
    red_a = red_a.reshape(R_ATT, D_MODEL)
    g_w_in = red_a[:IN_SHARD].T
    g_w_out = red_a[IN_SHARD_P:]

    loss_row = loss_acc[0:1] * (0.5 / D_MODEL)
    d_sink = dsink[:, :A_GROUP, 0].reshape(nb, A_HEADS).sum(axis=0)
    part = _pack_small(d_g1[0], small[4, :B_HEADS], small[0, :HEAD_DIM], small[1, :HEAD_DIM], d_sink,
                       small[2, :HEAD_DIM], small[3, :HEAD_DIM], d_g2[0], loss_row)
    zero_row = jnp.zeros((1, LANES), F32)
    smalls = lambda t: _pack_small(*t, zero_row)
    w_small = smalls((attn_norm_g, b_forget, q_norm_a, k_norm_a, sink_logits, q_norm_b, k_norm_b, mlp_norm_g))
    m_small = smalls((m_attn_norm_g, m_b_forget, m_q_norm_a, m_k_norm_a, m_sink_logits, m_q_norm_b, m_k_norm_b,
                      m_mlp_norm_g))
    v_small = smalls((v_attn_norm_g, v_b_forget, v_q_norm_a, v_k_norm_a, v_sink_logits, v_q_norm_b, v_k_norm_b,
                      v_mlp_norm_g))
    (g_s, d_s, m_s, v_s), loss = _small_adamw(_small_allreduce(part), w_small, m_small, v_small)

    big = {}
    for name, w, g, m, v in (("adamw_w_in", w_in, g_w_in, m_w_in, v_w_in),
                             ("adamw_w_out", w_out, g_w_out, m_w_out, v_w_out),
                             ("adamw_w_up", w_up, g_w_up, m_w_up, v_w_up),
                             ("adamw_w_down", w_down, g_w_down, m_w_down, v_w_down)):
        big[name] = (g,) + tuple(_adamw(w, g, m, v, name))

    def assemble(k, s):
        return (s[0], big["adamw_w_in"][k], s[1], s[2], s[3], s[4], s[5], s[6], big["adamw_w_out"][k], s[7],
                big["adamw_w_up"][k], big["adamw_w_down"][k])

    return (loss, grad_x.reshape(nb, seq, D_MODEL), *assemble(0, g_s), *assemble(1, d_s), *assemble(2, m_s),
            *assemble(3, v_s))
```

```python
import functools

import numpy as np
import jax
import jax.numpy as jnp
from jax import lax
from jax.experimental import pallas as pl
from jax.experimental.pallas import tpu as pltpu

F32 = jnp.float32
BF16 = jnp.bfloat16

D_MODEL = 1024
HEAD_DIM = 64
LANES = 128
A_HEADS = 8
A_KV_HEADS = 2
A_GROUP = A_HEADS // A_KV_HEADS
B_HEADS = 8
WINDOW = 128
D_FF = 4096
IN_WIDTH = 2312
EPS = 1e-6
SCALE = 0.125
LOG2E = 1.4426950408889634
LN2 = 0.6931471805599453
CHUNK = 32
FOX_TK = 512
FOX_PARTS = 8
FOX_PARTS_BWD = 4
NEG = -1e30

G_QA, G_KA, G_QB, G_KB, G_VA, G_VB, G_F = 0, 8, 10, 18, 26, 28, 36
N_NORM_GROUPS = 26
N_GROUPS = 38
NP = N_GROUPS * LANES
MIXED_P = (A_HEADS + B_HEADS) * LANES

N_CHIPS = 4
IN_SHARD = IN_WIDTH // N_CHIPS
IN_SHARD_P = 608
R_ATT = IN_SHARD_P + D_MODEL // N_CHIPS

SMALL_ROWS = 24
ROW_LOSS = 22

ADAM_LR = 0.001
ADAM_B1 = 0.9
ADAM_B2 = 0.999
ADAM_EPS = 1e-08
ADAM_WD = 0.01
ADAM_STEP = 10

VMEM_LIMIT = 52 * 1024 * 1024
MESH = pl.DeviceIdType.MESH


def _pcall(body, **kw):
    return pl.pallas_call(body, **kw)


def _params(sem=None):
    return pltpu.CompilerParams(dimension_semantics=sem, vmem_limit_bytes=VMEM_LIMIT)


def _dot(a, b):
    return jnp.dot(a, b, preferred_element_type=F32)


def _dot_nt(a, b):
    return lax.dot_general(a, b, (((1,), (1,)), ((), ())), preferred_element_type=F32)


def _dot_tn(a, b):
    return lax.dot_general(a, b, (((0,), (0,)), ((), ())), preferred_element_type=F32)


def _split3(x):
    hi = x.astype(BF16)
    r1 = x - hi.astype(F32)
    mid = r1.astype(BF16)
    lo = (r1 - mid.astype(F32)).astype(BF16)
    return hi, mid, lo


def _dot_exact(mat, x):
    hi, mid, lo = _split3(x)
    return _dot(mat, lo) + _dot(mat, mid) + _dot(mat, hi)


def _const(shape):
    zeros = (0,) * len(shape)
    return pl.BlockSpec(shape, lambda *_: zeros)


def _resident(shape):
    zeros = (0,) * len(shape)
    return pl.BlockSpec(shape, lambda *_: zeros, pipeline_mode=pl.Buffered(1))


def _rows(tm, n):
    return pl.BlockSpec((tm, n), lambda i: (i, 0))


def _aug_select():
    e = np.zeros((3 * LANES, 2 * B_HEADS * LANES), np.float32)
    for j in range(3):
        for h in range(B_HEADS):
            e[j * LANES + h, h * LANES + HEAD_DIM + j] = 1.0
            e[j * LANES + h, (B_HEADS + h) * LANES + HEAD_DIM + 3 + j] = -1.0
    return jnp.asarray(e, BF16)


def _dc_select():
    e = np.zeros((2 * B_HEADS * LANES, LANES), np.float32)
    for h in range(B_HEADS):
        e[h * LANES + HEAD_DIM, h] = 1.0
        e[(B_HEADS + h) * LANES + HEAD_DIM + 3, h] = -1.0
    return jnp.asarray(e, BF16)


def _tri(n, upper):
    t = np.tril(np.ones((n, n), np.float32))
    return jnp.asarray(t.T if upper else t, BF16)


def _inproj(x2, g1, w_pad_t, gain_row, b_row, seq, later_weights):
    t_all = x2.shape[0]
    tm = min(256, seq)
    n_steps = t_all // tm
    forward_step = max(n_steps - 2, 0)
    tiles_per_seq = seq // tm
    tri = _tri(tm, False)
    esel = _aug_select()
    n_later = len(later_weights)

    def body(x_ref, g_ref, w_ref, gain_ref, b_ref, tri_ref, e_ref, *rest):
        later_src, rest = rest[:n_later], rest[n_later:]
        xn_ref, pre_ref, qa_ref, ka_ref, va_ref, qb_ref, kb_ref, vb_ref, z_ref = rest[:9]
        later_out, (carry_ref, send_sems, recv_sems, local_sems) = rest[9:9 + n_later], rest[9 + n_later:]
        i = pl.program_id(0)
        gather = _Gather(later_src, later_out, send_sems, recv_sems, local_sems)

        @pl.when(i == 0)
        def _():
            gather.start()

        @pl.when(i == forward_step)
        def _():
            gather.forward()

        @pl.when(i % tiles_per_seq == 0)
        def _():
            carry_ref[...] = jnp.zeros_like(carry_ref)

        x = x_ref[...]
        r = lax.rsqrt(jnp.mean(x * x, axis=-1, keepdims=True) + EPS)
        xn = (x * r * g_ref[...]).astype(BF16)
        xn_ref[...] = xn
        proj = _dot_nt(xn, w_ref[...])
        pre_ref[...] = proj[:, :N_NORM_GROUPS * LANES].astype(BF16)
        lane = lax.broadcasted_iota(jnp.int32, (tm, LANES), 1)

        z = proj[:, G_F * LANES:(G_F + 1) * LANES] + b_ref[...]
        z_ref[...] = z
        lf = jnp.minimum(z, 0.0) - jnp.log(1.0 + jnp.exp(-jnp.abs(z)))
        lf = jnp.where(lane < B_HEADS, lf, 0.0)
        c = _dot_exact(tri_ref[...], lf) + carry_ref[...]
        carry_ref[...] += jnp.sum(lf, axis=0, keepdims=True)
        aug = _dot(jnp.concatenate(_split3(c * LOG2E), axis=1), e_ref[...])

        def hnorm(g):
            p = proj[:, g * LANES:(g + 1) * LANES]
            rr = lax.rsqrt(jnp.sum(p * p, axis=-1, keepdims=True) * (1.0 / HEAD_DIM) + EPS)
            return p * rr * gain_ref[:, g * LANES:(g + 1) * LANES]

        ones_q = jnp.where((lane >= HEAD_DIM + 3) & (lane < HEAD_DIM + 6), 1.0, 0.0)
        ones_k = jnp.where((lane >= HEAD_DIM) & (lane < HEAD_DIM + 3), 1.0, 0.0)
        for h in range(A_HEADS):
            qa_ref[:, h * LANES:(h + 1) * LANES] = (hnorm(G_QA + h) * SCALE).astype(BF16)
        for h in range(A_KV_HEADS):
            ka_ref[:, h * LANES:(h + 1) * LANES] = hnorm(G_KA + h).astype(BF16)
        for h in range(B_HEADS):
            qb_ref[:, h * LANES:(h + 1) * LANES] = (
                hnorm(G_QB + h) * (SCALE * LOG2E) + aug[:, h * LANES:(h + 1) * LANES] + ones_q).astype(BF16)
            kb_ref[:, h * LANES:(h + 1) * LANES] = (
                hnorm(G_KB + h) + aug[:, (B_HEADS + h) * LANES:(B_HEADS + h + 1) * LANES] + ones_k).astype(BF16)
        va_ref[...] = proj[:, G_VA * LANES:G_VB * LANES].astype(BF16)
        one_v = jnp.where(lane == HEAD_DIM, 1.0, 0.0)
        for h in range(B_HEADS):
            cols = slice((G_VB + h) * LANES, (G_VB + h + 1) * LANES)
            vb_ref[:, h * LANES:(h + 1) * LANES] = (proj[:, cols] + one_v).astype(BF16)

        @pl.when(i == n_steps - 1)
        def _():
            gather.finish()

    widths = [(D_MODEL, BF16), (N_NORM_GROUPS * LANES, BF16), (A_HEADS * LANES, BF16), (A_KV_HEADS * LANES, BF16),
              (A_KV_HEADS * LANES, BF16), (B_HEADS * LANES, BF16), (B_HEADS * LANES, BF16), (B_HEADS * LANES, BF16),
              (LANES, F32)]
    res = _pcall(
        body, name="inproj", grid=(n_steps,),
        in_specs=[_rows(tm, D_MODEL), _const((1, D_MODEL)), _const((NP, D_MODEL)), _const((1, NP)),
                  _const((1, LANES)), _const((tm, tm)), _const(esel.shape)] + [ANY] * n_later,
        out_specs=[_rows(tm, w) for w, _ in widths] + [ANY] * n_later,
        out_shape=[jax.ShapeDtypeStruct((t_all, w), dt) for w, dt in widths]
        + [jax.ShapeDtypeStruct((8 * w.shape[0], w.shape[1]), w.dtype) for w in later_weights],
        scratch_shapes=[pltpu.VMEM((1, LANES), F32)] + _gather_scratch(n_later),
        compiler_params=_params(("arbitrary",)),
    )(x2, g1, w_pad_t, gain_row, b_row, tri, esel, *later_weights)
    return res[:9], res[9:]


def _fox_fwd(qb, kb, vb, nb, seq):
    t_all = qb.shape[0]
    tk = min(FOX_TK, seq // FOX_PARTS)
    tq = FOX_PARTS * tk
    nq = seq // tq

    def body(q_ref, k_ref, v_ref, o_ref, lse_ref, s_ref, p_ref, m_ref, alpha_ref, acc_ref):
        qi = pl.program_id(2)
        q = q_ref[...]
        m_ref[...] = jnp.full((tq, LANES), NEG, F32)
        acc_ref[...] = jnp.zeros((tq, LANES), F32)

        def step(j, modes):
            off = pl.multiple_of(j * tk, tk)
            k = k_ref[pl.ds(off, tk), :]
            v = v_ref[pl.ds(off, tk), :]
            live = [hf for hf in range(FOX_PARTS) if modes[hf] is not None]
            for hf in live:
                s_ref[hf] = _dot_nt(q[hf * tk:(hf + 1) * tk], k)
            for hf in live:
                for r in range(0, tk, CHUNK):
                    rows = slice(r, r + CHUNK)
                    grows = slice(hf * tk + r, hf * tk + r + CHUNK)
                    tiles = []
                    for jt in range(tk // LANES):
                        sc = s_ref[hf, rows, jt * LANES:(jt + 1) * LANES]
                        if modes[hf] == "diag":
                            row = r + lax.broadcasted_iota(jnp.int32, (CHUNK, LANES), 0)
                            col = jt * LANES + lax.broadcasted_iota(jnp.int32, (CHUNK, LANES), 1)
                            sc = jnp.where(row >= col, sc, NEG)
                        tiles.append(sc)
                    m_prev = m_ref[grows, :]
                    m_cur = functools.reduce(jnp.maximum, tiles)
                    m_new = jnp.maximum(m_prev, jnp.max(m_cur, axis=-1, keepdims=True))
                    m_ref[grows, :] = m_new
                    alpha_ref[grows, :] = jnp.exp2(m_prev - m_new)
                    for jt, sc in enumerate(tiles):
                        p_ref[hf, rows, jt * LANES:(jt + 1) * LANES] = jnp.exp2(sc - m_new).astype(BF16)
                hrows = slice(hf * tk, (hf + 1) * tk)
                acc_ref[hrows, :] = alpha_ref[hrows, :] * acc_ref[hrows, :] + _dot(p_ref[hf], v)

        def past(j, carry):
            step(j, ("full",) * FOX_PARTS)
            return carry

        lax.fori_loop(0, FOX_PARTS * qi, past, 0)
        for d in range(FOX_PARTS):
            step(FOX_PARTS * qi + d, (None,) * d + ("diag",) + ("full",) * (FOX_PARTS - 1 - d))
        acc = acc_ref[...]
        lane = lax.broadcasted_iota(jnp.int32, (tq, LANES), 1)
        l = jnp.sum(jnp.where(lane == HEAD_DIM, acc, 0.0), axis=-1, keepdims=True)
        o_ref[...] = (acc / l).astype(BF16)
        lse_ref[...] = m_ref[...] + jnp.log2(l)

    qspec = pl.BlockSpec((tq, LANES), lambda b, h, i: (b * nq + i, h))
    kspec = pl.BlockSpec((seq, LANES), lambda b, h, i: (b, h))
    return _pcall(
        body, name="fox_fwd", grid=(nb, B_HEADS, nq),
        in_specs=[qspec, kspec, kspec], out_specs=[qspec, qspec],
        out_shape=[jax.ShapeDtypeStruct((t_all, B_HEADS * LANES), BF16),
                   jax.ShapeDtypeStruct((t_all, B_HEADS * LANES), F32)],
        scratch_shapes=[pltpu.VMEM((FOX_PARTS, tk, tk), F32), pltpu.VMEM((FOX_PARTS, tk, tk), BF16),
                        pltpu.VMEM((tq, LANES), F32),
                        pltpu.VMEM((tq, LANES), F32), pltpu.VMEM((tq, LANES), F32)],
        compiler_params=_params(("parallel", "parallel", "arbitrary")),
    )(qb, kb, vb)


def _swa_bias(slopes):
    row = jnp.arange(A_GROUP * WINDOW, dtype=jnp.int32)[:, None] % WINDOW
    col = jnp.arange(2 * WINDOW, dtype=jnp.int32)[None, :]
    slope_rows = jnp.repeat(slopes.reshape(A_KV_HEADS, A_GROUP), WINDOW, axis=1)[:, :, None]
    out = []
    for t_rel in (0, WINDOW):
        dist = t_rel + row - col
        valid = (dist >= 0) & (dist < WINDOW)
        out.append(jnp.where(valid[None], -slope_rows * dist.astype(F32)[None], NEG))
    return jnp.stack(out)


def _stack_heads(ref, rows):
    return jnp.concatenate([ref[rows, j * LANES:(j + 1) * LANES] for j in range(A_GROUP)], axis=0)


def _sink_rows(sink_ref, g):
    return jnp.concatenate([jnp.full((WINDOW, LANES), sink_ref[g * A_GROUP + j], F32) for j in range(A_GROUP)], axis=0)


def _rep(col):
    return jnp.broadcast_to(col, (col.shape[0], LANES))


def _swa_specs(nq, tq, seq):
    smem = pl.BlockSpec(memory_space=pltpu.SMEM)
    qspec = pl.BlockSpec((tq, A_GROUP * LANES), lambda b, g, i: (b * nq + i, g))
    kspec = pl.BlockSpec((seq, LANES), lambda b, g, i: (b, g))
    bias_first = pl.BlockSpec((None, None, A_GROUP * WINDOW, 2 * WINDOW),
                              lambda b, g, i: (jnp.minimum(i, 1), g, 0, 0))
    bias_rest = pl.BlockSpec((None, None, A_GROUP * WINDOW, 2 * WINDOW), lambda b, g, i: (1, g, 0, 0))
    return smem, qspec, kspec, bias_first, bias_rest


def _swa_fwd(qa, ka, va, sinks, bias, nb, seq):
    t_all = qa.shape[0]
    tq = min(512, seq)
    nq = seq // tq

    def body(sink_ref, q_ref, k_ref, v_ref, bias0_ref, bias_ref, o_ref, l_ref):
        qi = pl.program_id(2)
        sink = _sink_rows(sink_ref, pl.program_id(1))
        for a in range(tq // WINDOW):
            t0 = qi * tq + a * WINDOW
            start = pl.multiple_of(jnp.maximum(t0 - WINDOW, 0), WINDOW)
            rows = slice(a * WINDOW, (a + 1) * WINDOW)
            k = k_ref[pl.ds(start, 2 * WINDOW), :]
            v = v_ref[pl.ds(start, 2 * WINDOW), :]
            s = _dot_nt(_stack_heads(q_ref, rows), k) + (bias0_ref if a == 0 else bias_ref)[...]
            s0, s1 = s[:, :LANES], s[:, LANES:]
            m = jnp.maximum(_rep(jnp.max(jnp.maximum(s0, s1), axis=-1, keepdims=True)), sink)
            p0, p1 = jnp.exp(s0 - m), jnp.exp(s1 - m)
            den = _rep(jnp.sum(p0 + p1, axis=-1, keepdims=True)) + jnp.exp(sink - m)
            inv = 1.0 / den
            o = _dot(jnp.concatenate([(p0 * inv).astype(BF16), (p1 * inv).astype(BF16)], axis=1), v).astype(BF16)
            lrow = m + jnp.log(den)
            for j in range(A_GROUP):
                o_ref[rows, j * LANES:(j + 1) * LANES] = o[j * WINDOW:(j + 1) * WINDOW]
                l_ref[rows, j * LANES:(j + 1) * LANES] = lrow[j * WINDOW:(j + 1) * WINDOW]

    smem, qspec, kspec, bias_first, bias_rest = _swa_specs(nq, tq, seq)
    return _pcall(
        body, name="swa_fwd", grid=(nb, A_KV_HEADS, nq),
        in_specs=[smem, qspec, kspec, kspec, bias_first, bias_rest], out_specs=[qspec, qspec],
        out_shape=[jax.ShapeDtypeStruct((t_all, A_HEADS * LANES), BF16),
                   jax.ShapeDtypeStruct((t_all, A_HEADS * LANES), F32)],
        compiler_params=_params(("parallel", "parallel", "arbitrary")),
    )(sinks, qa, ka, va, bias, bias)


def _outproj(x2, oa, ob, wo_pad, g2):
    t_all = x2.shape[0]
    tm = min(512, t_all)
    half = A_HEADS * LANES

    def body(x_ref, oa_ref, ob_ref, w_ref, g_ref, h_ref, hn_ref):
        h = x_ref[...] + _dot(oa_ref[...], w_ref[:half, :]) + _dot(ob_ref[...], w_ref[half:, :])
        h_ref[...] = h
        r = lax.rsqrt(jnp.mean(h * h, axis=-1, keepdims=True) + EPS)
        hn_ref[...] = (h * r * g_ref[...]).astype(BF16)

    return _pcall(
        body, name="outproj", grid=(t_all // tm,),
        in_specs=[_rows(tm, D_MODEL), _rows(tm, half), _rows(tm, half), _const((MIXED_P, D_MODEL)),
                  _const((1, D_MODEL))],
        out_specs=[_rows(tm, D_MODEL), _rows(tm, D_MODEL)],
        out_shape=[jax.ShapeDtypeStruct((t_all, D_MODEL), F32), jax.ShapeDtypeStruct((t_all, D_MODEL), BF16)],
        compiler_params=_params(("parallel",)),
    )(x2, oa, ob, wo_pad, g2)


def _mlp_fwd(hn, w_up_blocks, w_down, h, tgt):
    t_all = h.shape[0]
    tm = min(512, t_all)
    nj = D_FF // D_MODEL

    def body(a_ref, wu_ref, wd_ref, h_ref, t_ref, ru_ref, dy_ref, dyb_ref, loss_ref):
        @pl.when(pl.program_id(0) == 0)
        def _():
            loss_ref[...] = jnp.zeros_like(loss_ref)

        a = a_ref[...]
        y = h_ref[...]
        for j in range(nj):
            cols = slice(j * D_MODEL, (j + 1) * D_MODEL)
            ru = jnp.maximum(_dot(a, wu_ref[j]), 0.0)
            ru_ref[:, cols] = ru.astype(BF16)
            y = y + _dot((ru * ru).astype(BF16), wd_ref[cols, :])
        err = y - t_ref[...]
        loss_ref[...] += jnp.sum(err * err)
        dy = err * (1.0 / D_MODEL)
        dy_ref[...] = dy
        dyb_ref[...] = dy.astype(BF16)

    return _pcall(
        body, name="mlp_fwd", grid=(t_all // tm,),
        in_specs=[_rows(tm, D_MODEL), _resident((nj, D_MODEL, D_MODEL)), _resident((D_FF, D_MODEL)), _rows(tm, D_MODEL),
                  _rows(tm, D_MODEL)],
        out_specs=[_rows(tm, D_FF), _rows(tm, D_MODEL), _rows(tm, D_MODEL), _const((8, LANES))],
        out_shape=[jax.ShapeDtypeStruct((t_all, D_FF), BF16), jax.ShapeDtypeStruct((t_all, D_MODEL), F32),
                   jax.ShapeDtypeStruct((t_all, D_MODEL), BF16), jax.ShapeDtypeStruct((8, LANES), F32)],
        compiler_params=_params(("arbitrary",)),
    )(hn, w_up_blocks, w_down, h, tgt)


def _mlp_bwd_w(dyb, w_down, ru, hn):
    t_all = dyb.shape[0]
    tm = min(512, t_all)
    nj = D_FF // D_MODEL

    def body(dy_ref, w_ref, ru_ref, hn_ref, du_ref, dw_ref):
        @pl.when(pl.program_id(1) == 0)
        def _():
            dw_ref[...] = jnp.zeros_like(dw_ref)

        dy = dy_ref[...]
        ru = ru_ref[...].astype(F32)
        du = (_dot_nt(dy, w_ref[...]) * (2.0 * ru)).astype(BF16)
        du_ref[...] = du
        dw_ref[0] += _dot_tn(hn_ref[...], du)
        dw_ref[1] += _dot_tn((ru * ru).astype(BF16), dy)

    tok = pl.BlockSpec((tm, D_MODEL), lambda j, i: (i, 0))
    blk = pl.BlockSpec((tm, D_MODEL), lambda j, i: (i, j))
    wspec = pl.BlockSpec((2, None, D_MODEL, D_MODEL), lambda j, i: (0, j, 0, 0))
    return _pcall(
        body, name="mlp_bwd_w", grid=(nj, t_all // tm),
        in_specs=[tok, pl.BlockSpec((D_MODEL, D_MODEL), lambda j, i: (j, 0)), blk, tok],
        out_specs=[blk, wspec],
        out_shape=[jax.ShapeDtypeStruct((t_all, D_FF), BF16), jax.ShapeDtypeStruct((2, nj, D_MODEL, D_MODEL), F32)],
        compiler_params=_params(("parallel", "arbitrary")),
    )(dyb, w_down, ru, hn)


def _pair_exchange_copy(g_ref, out_ref, send_sem, recv_sem):
    x, y, c = _place()
    return pltpu.make_async_remote_copy(
        src_ref=g_ref.at[1 - c], dst_ref=out_ref, send_sem=send_sem, recv_sem=recv_sem,
        device_id=(x, y, 1 - c), device_id_type=MESH)


def _mlp_dhn(du, w_up_blocks, h, dy, g2, d_w_mlp):
    t_all = h.shape[0]
    tm = min(512, t_all)
    n_steps = t_all // tm

    def body(a_ref, w_ref, h_ref, dy_ref, g_ref, dw_ref, dh_ref, dhb_ref, dg_ref, got_ref, send_sem, recv_sem):
        @pl.when(pl.program_id(0) == 0)
        def _():
            dg_ref[...] = jnp.zeros_like(dg_ref)
            _pair_exchange_copy(dw_ref, got_ref, send_sem, recv_sem).start()

        dhn = _dot_nt(a_ref[:, :D_MODEL], w_ref[0])
        for j in range(1, D_FF // D_MODEL):
            dhn = dhn + _dot_nt(a_ref[:, j * D_MODEL:(j + 1) * D_MODEL], w_ref[j])
        h = h_ref[...]
        r = lax.rsqrt(jnp.mean(h * h, axis=-1, keepdims=True) + EPS)
        hh = h * r
        dg_ref[...] += jnp.sum(dhn * hh, axis=0, keepdims=True)
        dz = dhn * g_ref[...]
        dh = dy_ref[...] + r * (dz - hh * jnp.mean(dz * hh, axis=-1, keepdims=True))
        dh_ref[...] = dh
        dhb_ref[...] = dh.astype(BF16)

        @pl.when(pl.program_id(0) == n_steps - 1)
        def _():
            _pair_exchange_copy(dw_ref, got_ref, send_sem, recv_sem).wait()

    return _pcall(
        body, name="mlp_dhn", grid=(n_steps,),
        in_specs=[_rows(tm, D_FF), _resident((D_FF // D_MODEL, D_MODEL, D_MODEL)), _rows(tm, D_MODEL),
                  _rows(tm, D_MODEL), _const((1, D_MODEL)), ANY],
        out_specs=[_rows(tm, D_MODEL), _rows(tm, D_MODEL), _const((1, D_MODEL)), ANY],
        out_shape=[jax.ShapeDtypeStruct((t_all, D_MODEL), F32), jax.ShapeDtypeStruct((t_all, D_MODEL), BF16),
                   jax.ShapeDtypeStruct((1, D_MODEL), F32), jax.ShapeDtypeStruct(d_w_mlp.shape[1:], F32)],
        scratch_shapes=[pltpu.SemaphoreType.DMA, pltpu.SemaphoreType.DMA],
        compiler_params=_params(("arbitrary",)),
    )(du, w_up_blocks, h, dy, g2, d_w_mlp)


def _dmixed(dhb, wo_pad, oa, ob):
    t_all = dhb.shape[0]
    tm = min(512, t_all)
    half = A_HEADS * LANES

    def body(a_ref, w_ref, oa_ref, ob_ref, da_ref, db_ref, delta_ref, dwo_ref):
        @pl.when(pl.program_id(0) == 0)
        def _():
            dwo_ref[...] = jnp.zeros_like(dwo_ref)

        a = a_ref[...]
        d = _dot_nt(a, w_ref[...])
        da_ref[...] = d[:, :half].astype(BF16)
        db_ref[...] = d[:, half:].astype(BF16)
        for h in range(B_HEADS):
            cols = slice(h * LANES, (h + 1) * LANES)
            prod = d[:, half + h * LANES:half + (h + 1) * LANES] * ob_ref[:, cols].astype(F32)
            delta_ref[:, cols] = jnp.broadcast_to(jnp.sum(prod, axis=-1, keepdims=True), (tm, LANES))
        dwo_ref[:half, :] += _dot_tn(oa_ref[...], a)
        dwo_ref[half:, :] += _dot_tn(ob_ref[...], a)

    return _pcall(
        body, name="dmixed", grid=(t_all // tm,),
        in_specs=[_rows(tm, D_MODEL), _const((MIXED_P, D_MODEL)), _rows(tm, half), _rows(tm, half)],
        out_specs=[_rows(tm, half), _rows(tm, half), _rows(tm, half), _const((MIXED_P, D_MODEL))],
        out_shape=[jax.ShapeDtypeStruct((t_all, half), BF16), jax.ShapeDtypeStruct((t_all, half), BF16),
                   jax.ShapeDtypeStruct((t_all, half), F32), jax.ShapeDtypeStruct((MIXED_P, D_MODEL), F32)],
        compiler_params=_params(("arbitrary",)),
    )(dhb, wo_pad, oa, ob)


def _fox_bwd(qb, kb, vb, dob, lse, delta, nb, seq, pair_sums):
    t_all = qb.shape[0]
    tk = min(FOX_TK, seq // FOX_PARTS_BWD)
    tq = FOX_PARTS_BWD * tk
    nk = seq // tk

    def body(q_ref, k_ref, v_ref, do_ref, lse_ref, delta_ref, pair_ref, dq_ref, dk_ref, dv_ref, got_ref,
             s_ref, dp_ref, p_ref, ds_ref, dk_acc, dv_acc, send_sems, recv_sems):
        kj = pl.program_id(2)
        bh = pl.program_id(0) * B_HEADS + pl.program_id(1)

        @pl.when((bh == 0) & (kj == 0))
        def _():
            for cp in _chip_exchange_copies(pair_ref, got_ref, send_sems, recv_sems):
                cp.start()

        @pl.when(kj == 0)
        def _():
            dq_ref[...] = jnp.zeros_like(dq_ref)

        dk_acc[...] = jnp.zeros_like(dk_acc)
        dv_acc[...] = jnp.zeros_like(dv_acc)
        k = k_ref[...]
        v = v_ref[...]

        def block(off, r0, r1, masked):
            qrows = pl.ds(pl.multiple_of(off + r0, CHUNK), r1 - r0)
            q = q_ref[qrows, :]
            do = do_ref[qrows, :]
            s_ref[r0:r1, :] = _dot_nt(q, k)
            dp_ref[r0:r1, :] = _dot_nt(do, v)
            for r in range(r0, r1, CHUNK):
                rows = slice(r, r + CHUNK)
                chunk = pl.ds(pl.multiple_of(off + r, CHUNK), CHUNK)
                lse_c = lse_ref[chunk, :]
                delta_c = delta_ref[chunk, :]
                for jt in range(tk // LANES):
                    cols = slice(jt * LANES, (jt + 1) * LANES)
                    p = jnp.exp2(s_ref[rows, cols] - lse_c)
                    if masked:
                        row = r - r0 + lax.broadcasted_iota(jnp.int32, (CHUNK, LANES), 0)
                        col = jt * LANES + lax.broadcasted_iota(jnp.int32, (CHUNK, LANES), 1)
                        p = jnp.where(row >= col, p, 0.0)
                    p_ref[rows, cols] = p.astype(BF16)
                    ds_ref[rows, cols] = (p * (dp_ref[rows, cols] - delta_c)).astype(BF16)
            dv_acc[...] += _dot_tn(p_ref[r0:r1, :], do)
            dk_acc[...] += _dot_tn(ds_ref[r0:r1, :], q)
            dq_ref[qrows, :] += _dot(ds_ref[r0:r1, :], k)

        first = kj // FOX_PARTS_BWD
        off_first = pl.multiple_of(first * tq, tq)
        for d in range(FOX_PARTS_BWD):
            @pl.when(kj % FOX_PARTS_BWD == d)
            def _(d=d):
                block(off_first, d * tk, (d + 1) * tk, True)
                if d < FOX_PARTS_BWD - 1:
                    block(off_first, (d + 1) * tk, tq, False)

        def later(i, carry):
            block(pl.multiple_of(i * tq, tq), 0, tq, False)
            return carry

        lax.fori_loop(first + 1, seq // tq, later, 0)
        dk_ref[...] = dk_acc[...]
        dv_ref[...] = dv_acc[...]

        @pl.when((bh == nb * B_HEADS - 1) & (kj == nk - 1))
        def _():
            for cp in _chip_exchange_copies(pair_ref, got_ref, send_sems, recv_sems):
                cp.wait()

    full = pl.BlockSpec((seq, LANES), lambda b, h, j: (b, h))
    tile = pl.BlockSpec((tk, LANES), lambda b, h, j: (b * nk + j, h))
    shp = jax.ShapeDtypeStruct((t_all, B_HEADS * LANES), F32)
    return _pcall(
        body, name="fox_bwd", grid=(nb, B_HEADS, nk),
        in_specs=[full, tile, tile, full, full, full, ANY], out_specs=[full, tile, tile, ANY],
        out_shape=[shp, shp, shp, jax.ShapeDtypeStruct((3,) + pair_sums.shape[1:], pair_sums.dtype)],
        scratch_shapes=[pltpu.VMEM((tq, tk), F32), pltpu.VMEM((tq, tk), F32), pltpu.VMEM((tq, tk), BF16),
                        pltpu.VMEM((tq, tk), BF16), pltpu.VMEM((tk, LANES), F32), pltpu.VMEM((tk, LANES), F32),
                        pltpu.SemaphoreType.DMA((3,)), pltpu.SemaphoreType.DMA((3,))],
        compiler_params=_params(("arbitrary", "arbitrary", "arbitrary")),
    )(qb, kb, vb, dob, lse, delta, pair_sums)


def _swa_bwd(qa, ka, va, oa, doa, lrow, sinks, bias, nb, seq):
    t_all = qa.shape[0]
    tq = min(512, seq)
    nq = seq // tq

    def body(sink_ref, q_ref, k_ref, v_ref, bias0_ref, bias_ref, o_ref, do_ref, l_ref,
             dq_ref, dk_ref, dv_ref, dsink_ref):
        qi = pl.program_id(2)
        sink = _sink_rows(sink_ref, pl.program_id(1))

        @pl.when(qi == 0)
        def _():
            dk_ref[...] = jnp.zeros_like(dk_ref)
            dv_ref[...] = jnp.zeros_like(dv_ref)
            dsink_ref[...] = jnp.zeros_like(dsink_ref)

        for a in range(tq // WINDOW):
            t0 = qi * tq + a * WINDOW
            start = pl.multiple_of(jnp.maximum(t0 - WINDOW, 0), WINDOW)
            rows = slice(a * WINDOW, (a + 1) * WINDOW)
            win = pl.ds(start, 2 * WINDOW)
            q = _stack_heads(q_ref, rows)
            k = k_ref[win, :]
            v = v_ref[win, :]
            do = _stack_heads(do_ref, rows)
            lrow = _stack_heads(l_ref, rows)
            s = _dot_nt(q, k) + (bias0_ref if a == 0 else bias_ref)[...]
            dp = _dot_nt(do, v)
            delta = _rep(jnp.sum(do.astype(F32) * _stack_heads(o_ref, rows).astype(F32), axis=-1, keepdims=True))
            p = [jnp.exp(s[:, t * LANES:(t + 1) * LANES] - lrow) for t in range(2)]
            ds = jnp.concatenate([(p[t] * (dp[:, t * LANES:(t + 1) * LANES] - delta)).astype(BF16) for t in range(2)],
                                 axis=1)
            dq = _dot(ds, k)
            dk_ref[win, :] += _dot_tn(ds, q)
            dv_ref[win, :] += _dot_tn(jnp.concatenate([p[0].astype(BF16), p[1].astype(BF16)], axis=1), do)
            sink_term = jnp.exp(sink - lrow) * delta
            for j in range(A_GROUP):
                part = slice(j * WINDOW, (j + 1) * WINDOW)
                dq_ref[rows, j * LANES:(j + 1) * LANES] = dq[part]
                dsink_ref[j:j + 1, :] -= jnp.sum(sink_term[part], axis=0, keepdims=True)

    smem, qspec, kspec, bias_first, bias_rest = _swa_specs(nq, tq, seq)
    return _pcall(
        body, name="swa_bwd", grid=(nb, A_KV_HEADS, nq),
        in_specs=[smem, qspec, kspec, kspec, bias_first, bias_rest, qspec, qspec, qspec],
        out_specs=[qspec, kspec, kspec, pl.BlockSpec((None, 8, LANES), lambda b, g, i: (b * A_KV_HEADS + g, 0, 0))],
        out_shape=[jax.ShapeDtypeStruct((t_all, A_HEADS * LANES), F32),
                   jax.ShapeDtypeStruct((t_all, A_KV_HEADS * LANES), F32),
                   jax.ShapeDtypeStruct((t_all, A_KV_HEADS * LANES), F32),
                   jax.ShapeDtypeStruct((nb * A_KV_HEADS, 8, LANES), F32)],
        compiler_params=_params(("parallel", "parallel", "arbitrary")),
    )(sinks, qa, ka, va, bias, bias, oa, doa, lrow)


def _dproj_dx(pre, dqa, dka, dqb, dkb, dva, dvb, z, x2, dh, gain_row, w_pad_t, g1, seq):
    t_all = pre.shape[0]
    tm = min(256, seq)
    nt = t_all // tm
    tiles_per_seq = seq // tm
    triu = _tri(tm, True)
    sel = _dc_select()

    def body(pre_ref, dqa_ref, dka_ref, dqb_ref, dkb_ref, dva_ref, dvb_ref, z_ref, x_ref, dh_ref, gain_ref, triu_ref,
             sel_ref, w_ref, g_ref, dproj_ref, small_ref, dx_ref, dg_ref, carry_ref):
        i = pl.program_id(0)

        @pl.when(i == 0)
        def _():
            small_ref[...] = jnp.zeros_like(small_ref)
            dg_ref[...] = jnp.zeros_like(dg_ref)

        @pl.when(i % tiles_per_seq == 0)
        def _():
            carry_ref[...] = jnp.zeros_like(carry_ref)

        def norm_bwd(g, dhat):
            cols = slice(g * LANES, (g + 1) * LANES)
            p = pre_ref[:, cols].astype(F32)
            rr = lax.rsqrt(jnp.sum(p * p, axis=-1, keepdims=True) * (1.0 / HEAD_DIM) + EPS)
            n = p * rr
            dz = dhat * gain_ref[:, cols]
            dproj_ref[:, cols] = (rr * (dz - n * (jnp.sum(dz * n, axis=-1, keepdims=True) * (1.0 / HEAD_DIM)))
                                  ).astype(BF16)
            return jnp.sum(dhat * n, axis=0, keepdims=True)

        def group_sum(g0, d_ref, count, scale):
            acc = jnp.zeros((1, LANES), F32)
            for h in range(count):
                d = d_ref[:, h * LANES:(h + 1) * LANES]
                acc = acc + norm_bwd(g0 + h, d * scale if scale != 1.0 else d)
            return acc

        small_ref[0:1, :] += group_sum(G_QA, dqa_ref, A_HEADS, SCALE)
        small_ref[1:2, :] += group_sum(G_KA, dka_ref, A_KV_HEADS, 1.0)
        small_ref[2:3, :] += group_sum(G_QB, dqb_ref, B_HEADS, SCALE)
        small_ref[3:4, :] += group_sum(G_KB, dkb_ref, B_HEADS, LN2)
        dproj_ref[:, G_VA * LANES:G_VB * LANES] = dva_ref[...].astype(BF16)
        dproj_ref[:, G_VB * LANES:G_F * LANES] = dvb_ref[...].astype(BF16)

        dc = jnp.zeros((tm, LANES), F32)
        for piece_q, piece_k in zip(_split3(dqb_ref[...]), _split3(dkb_ref[...])):
            dc = dc + _dot(jnp.concatenate([piece_q, piece_k], axis=1), sel_ref[...])
        dlf = _dot_exact(triu_ref[...], dc) + carry_ref[...]
        carry_ref[...] += jnp.sum(dc, axis=0, keepdims=True)
        dz = dlf / (1.0 + jnp.exp(z_ref[...]))
        small_ref[4:5, :] += jnp.sum(dz, axis=0, keepdims=True)
        dproj_ref[:, G_F * LANES:(G_F + 1) * LANES] = dz.astype(BF16)
        dproj_ref[:, (G_F + 1) * LANES:] = jnp.zeros((tm, LANES), BF16)

        dxn = _dot(dproj_ref[...], w_ref[...])
        x = x_ref[...]
        r = lax.rsqrt(jnp.mean(x * x, axis=-1, keepdims=True) + EPS)
        xh = x * r
        dg_ref[...] += jnp.sum(dxn * xh, axis=0, keepdims=True)
        dxz = dxn * g_ref[...]
        dx_ref[...] = dh_ref[...] + r * (dxz - xh * jnp.mean(dxz * xh, axis=-1, keepdims=True))

    def rev(n):
        return pl.BlockSpec((tm, n), lambda i: (nt - 1 - i, 0))

    return _pcall(
        body, name="dproj_dx", grid=(nt,),
        in_specs=[rev(N_NORM_GROUPS * LANES), rev(A_HEADS * LANES), rev(A_KV_HEADS * LANES), rev(B_HEADS * LANES),
                  rev(B_HEADS * LANES), rev(A_KV_HEADS * LANES), rev(B_HEADS * LANES), rev(LANES), rev(D_MODEL),
                  rev(D_MODEL), _const((1, NP)), _const((tm, tm)), _const(sel.shape), _const((NP, D_MODEL)),
                  _const((1, D_MODEL))],
        out_specs=[rev(NP), _const((8, LANES)), rev(D_MODEL), _const((1, D_MODEL))],
        out_shape=[jax.ShapeDtypeStruct((t_all, NP), BF16), jax.ShapeDtypeStruct((8, LANES), F32),
                   jax.ShapeDtypeStruct((t_all, D_MODEL), F32), jax.ShapeDtypeStruct((1, D_MODEL), F32)],
        scratch_shapes=[pltpu.VMEM((1, LANES), F32)],
        compiler_params=_params(("arbitrary",)),
    )(pre, dqa, dka, dqb, dkb, dva, dvb, z, x2, dh, gain_row, triu, sel, w_pad_t, g1)


def _dwin(dproj, xn):
    t_all = xn.shape[0]
    tt = min(512, t_all)
    half = NP // 2

    def body(a_ref, b_ref, o_ref):
        @pl.when(pl.program_id(1) == 0)
        def _():
            o_ref[...] = jnp.zeros_like(o_ref)

        o_ref[...] += _dot_tn(a_ref[...], b_ref[...])

    return _pcall(
        body, name="dwin", grid=(2, t_all // tt),
        in_specs=[pl.BlockSpec((tt, half), lambda j, t: (t, j)), pl.BlockSpec((tt, D_MODEL), lambda j, t: (t, 0))],
        out_specs=pl.BlockSpec((half, D_MODEL), lambda j, t: (j, 0)),
        out_shape=jax.ShapeDtypeStruct((NP, D_MODEL), F32),
        compiler_params=_params(("parallel", "arbitrary")),
    )(dproj, xn)


ANY = pl.BlockSpec(memory_space=pl.ANY)


def _place():
    return lax.axis_index("x"), lax.axis_index("y"), lax.axis_index("c")


class _Gather:
    def __init__(self, srcs, outs, send_sems, recv_sems, local_sems):
        self.srcs, self.outs = srcs, outs
        self.send_sems, self.recv_sems, self.local_sems = send_sems, recv_sems, local_sems
        x, y, c = _place()
        self.c = c
        self.me, self.sibling = (x, y, c), (x, y, 1 - c)
        self.chips = [(1 - x, y), (x, 1 - y), (1 - x, 1 - y)]

    def _rows(self, a, px, py, pc):
        m = self.srcs[a].shape[0]
        return self.outs[a].at[pl.ds((4 * px + 2 * py + pc) * m, m), :]

    def _copy(self, a, k, block, to, from_src=False):
        return pltpu.make_async_remote_copy(
            src_ref=self.srcs[a] if from_src else self._rows(a, *block), dst_ref=self._rows(a, *block),
            send_sem=self.send_sems.at[k, a], recv_sem=self.recv_sems.at[k, a], device_id=to, device_id_type=MESH)

    def _own(self, a):
        return pltpu.make_async_copy(self.srcs[a], self._rows(a, *self.me), self.local_sems.at[a])

    def start(self):
        for a in range(len(self.srcs)):
            self._own(a).start()
            self._copy(a, 0, self.me, self.sibling, from_src=True).start()
            for j, chip in enumerate(self.chips):
                self._copy(a, 1 + j, self.me, (*chip, self.c), from_src=True).start()

    def forward(self):
        for a in range(len(self.srcs)):
            for j, chip in enumerate(self.chips):
                self._copy(a, 1 + j, (*chip, self.c), self.me).wait_recv()
                self._copy(a, 4 + j, (*chip, self.c), self.sibling).start()

    def finish(self):
        for a in range(len(self.srcs)):
            self._copy(a, 0, self.sibling, self.me).wait_recv()
            for j, chip in enumerate(self.chips):
                self._copy(a, 4 + j, (*chip, 1 - self.c), self.me).wait_recv()
            self._copy(a, 0, self.me, self.sibling, from_src=True).wait_send()
            for j, chip in enumerate(self.chips):
                self._copy(a, 1 + j, self.me, (*chip, self.c), from_src=True).wait_send()
                self._copy(a, 4 + j, (*chip, self.c), self.sibling).wait_send()
            self._own(a).wait()


def _gather_scratch(n_arrays):
    return [pltpu.SemaphoreType.DMA((7, n_arrays)), pltpu.SemaphoreType.DMA((7, n_arrays)),
            pltpu.SemaphoreType.DMA((n_arrays,))]


def _allgather_halves(mine):
    m_per, n = mine.shape

    def body(x_ref, out_ref, send_sems, recv_sems, local_sems):
        gather = _Gather((x_ref,), (out_ref,), send_sems, recv_sems, local_sems)
        gather.start()
        gather.forward()
        gather.finish()

    return _pcall(
        body, name="allgather_w_in",
        out_shape=jax.ShapeDtypeStruct((8 * m_per, n), mine.dtype),
        in_specs=[ANY], out_specs=ANY, scratch_shapes=_gather_scratch(1),
    )(mine)


def _rs_pair_exchange(g, name):
    def body(g_ref, out_ref, send_sem, recv_sem):
        x, y, c = _place()
        cp = pltpu.make_async_remote_copy(
            src_ref=g_ref.at[1 - c], dst_ref=out_ref, send_sem=send_sem, recv_sem=recv_sem,
            device_id=(x, y, 1 - c), device_id_type=MESH)
        cp.start()
        cp.wait()

    return _pcall(
        body, name=name, out_shape=jax.ShapeDtypeStruct(g.shape[1:], F32),
        in_specs=[ANY], out_specs=ANY, scratch_shapes=[pltpu.SemaphoreType.DMA, pltpu.SemaphoreType.DMA],
    )(g)


def _rs_pair_add(g, got, c_idx, name):
    rows = g.shape[2]

    def body(c_ref, a_ref, b_ref, o_ref, ob_ref):
        pair = a_ref[...] + b_ref[...]
        o_ref[...] = pair
        ob_ref[...] = pair.astype(BF16)

    blk = pl.BlockSpec((None, rows, D_MODEL), lambda s, c_ref: (s, 0, 0))
    return _pcall(
        body, name=name,
        grid_spec=pltpu.PrefetchScalarGridSpec(
            num_scalar_prefetch=1, grid=(N_CHIPS,),
            in_specs=[pl.BlockSpec((None, None, rows, D_MODEL), lambda s, c_ref: (c_ref[0], s, 0, 0)), blk],
            out_specs=[blk, blk]),
        out_shape=[jax.ShapeDtypeStruct((N_CHIPS, rows, D_MODEL), F32),
                   jax.ShapeDtypeStruct((N_CHIPS, rows, D_MODEL), BF16)],
        compiler_params=_params(("parallel",)),
    )(c_idx, g, got)


def _chip_exchange_copies(p_ref, out_ref, send_sems, recv_sems):
    x, y, c = _place()
    chips = [(1 - x, y), (x, 1 - y), (1 - x, 1 - y)]
    return [pltpu.make_async_remote_copy(
        src_ref=p_ref.at[2 * cx + cy], dst_ref=out_ref.at[j], send_sem=send_sems.at[j], recv_sem=recv_sems.at[j],
        device_id=(cx, cy, c), device_id_type=MESH) for j, (cx, cy) in enumerate(chips)]


def _rs_chip_exchange(p4, name):
    def body(p_ref, out_ref, send_sems, recv_sems):
        cps = _chip_exchange_copies(p_ref, out_ref, send_sems, recv_sems)
        for cp in cps:
            cp.start()
        for cp in cps:
            cp.wait()

    return _pcall(
        body, name=name, out_shape=jax.ShapeDtypeStruct((3,) + p4.shape[1:], p4.dtype),
        in_specs=[ANY], out_specs=ANY,
        scratch_shapes=[pltpu.SemaphoreType.DMA((3,)), pltpu.SemaphoreType.DMA((3,))],
    )(p4)


def _rs_chip_add(p4, got, sc_idx, name):
    rows = p4.shape[1]
    tr = next(rows // n for n in (8, 7, 6, 5, 4, 3, 2, 1) if rows % n == 0 and (rows // n) % 16 == 0)

    def body(sc_ref, a_ref, b_ref, o_ref):
        o_ref[...] = ((a_ref[...] + b_ref[0].astype(F32)) + b_ref[1].astype(F32)) + b_ref[2].astype(F32)

    return _pcall(
        body, name=name,
        grid_spec=pltpu.PrefetchScalarGridSpec(
            num_scalar_prefetch=1, grid=(rows // tr,),
            in_specs=[pl.BlockSpec((None, tr, D_MODEL), lambda i, sc_ref: (sc_ref[0], i, 0)),
                      pl.BlockSpec((3, tr, D_MODEL), lambda i, sc_ref: (0, i, 0))],
            out_specs=pl.BlockSpec((None, tr, D_MODEL), lambda i, sc_ref: (sc_ref[1], i, 0))),
        out_shape=jax.ShapeDtypeStruct((2, rows, D_MODEL), F32),
        compiler_params=_params(("parallel",)),
    )(sc_idx, p4, got)


def _rs_pair_share(halves, name):
    def body(r_ref, out_ref, send_sem, recv_sem):
        x, y, c = _place()
        cp = pltpu.make_async_remote_copy(
            src_ref=r_ref.at[c], dst_ref=out_ref.at[c], send_sem=send_sem, recv_sem=recv_sem,
            device_id=(x, y, 1 - c), device_id_type=MESH)
        cp.start()
        cp.wait()

    return _pcall(
        body, name=name, out_shape=jax.ShapeDtypeStruct(halves.shape, F32),
        in_specs=[ANY], out_specs=ANY, input_output_aliases={0: 0},
        scratch_shapes=[pltpu.SemaphoreType.DMA, pltpu.SemaphoreType.DMA],
    )(halves)


def _adam(w, g, m, v):
    m2 = ADAM_B1 * m + (1.0 - ADAM_B1) * g
    v2 = ADAM_B2 * v + (1.0 - ADAM_B2) * (g * g)
    m_hat = m2 / (1.0 - ADAM_B1 ** ADAM_STEP)
    v_hat = v2 / (1.0 - ADAM_B2 ** ADAM_STEP)
    return -ADAM_LR * (m_hat / (jnp.sqrt(v_hat) + ADAM_EPS) + ADAM_WD * w), m2, v2


def _small_allreduce(part):
    def body(p_ref, g_ref, buf, send_sems, recv_sems):
        x, y, c = _place()
        me = 4 * x + 2 * y + c
        cps = []
        for k in range(1, 8):
            peer = (1 - x if k & 4 else x, 1 - y if k & 2 else y, 1 - c if k & 1 else c)
            cps.append(pltpu.make_async_remote_copy(
                src_ref=p_ref, dst_ref=buf.at[me], send_sem=send_sems.at[k - 1], recv_sem=recv_sems.at[k - 1],
                device_id=peer, device_id_type=MESH))
        for cp in cps:
            cp.start()
        buf[me] = p_ref[...]
        for cp in cps:
            cp.wait()
        g = buf[0]
        for k in range(1, 8):
            g = g + buf[k]
        g_ref[...] = g

    vm = pl.BlockSpec(memory_space=pltpu.VMEM)
    return _pcall(
        body, name="small_allreduce",
        out_shape=jax.ShapeDtypeStruct((SMALL_ROWS, LANES), F32), in_specs=[vm], out_specs=vm,
        scratch_shapes=[pltpu.VMEM((8, SMALL_ROWS, LANES), F32), pltpu.SemaphoreType.DMA((7,)),
                        pltpu.SemaphoreType.DMA((7,))],
    )(part)


def _small_adamw(g, w, m, v):
    pieces = ((0, 8, LANES), (20, 1, B_HEADS), (16, 1, HEAD_DIM), (17, 1, HEAD_DIM), (21, 1, A_HEADS),
              (18, 1, HEAD_DIM), (19, 1, HEAD_DIM), (8, 8, LANES))

    def body(g_ref, w_ref, m_ref, v_ref, *rest):
        outs, (loss_ref, stage) = rest[:4 * len(pieces)], rest[4 * len(pieces):]
        g = g_ref[...]
        loss_ref[...] = g_ref[ROW_LOSS:ROW_LOSS + 1, :]
        for kind, packed in enumerate((g,) + _adam(w_ref[...], g, m_ref[...], v_ref[...])):
            stage[...] = packed
            for i, (row, rows, lanes) in enumerate(pieces):
                outs[kind * len(pieces) + i][...] = stage[row:row + rows, 0:lanes]

    vm = pl.BlockSpec(memory_space=pltpu.VMEM)
    shapes = [jax.ShapeDtypeStruct((rows, lanes), F32) for _ in range(4) for _, rows, lanes in pieces]
    shapes.append(jax.ShapeDtypeStruct((1, LANES), F32))
    res = _pcall(
        body, name="small_adamw", out_shape=shapes, in_specs=[vm, vm, vm, vm], out_specs=[vm] * len(shapes),
        scratch_shapes=[pltpu.VMEM((SMALL_ROWS, LANES), F32)],
    )(g, w, m, v)
    flat = [r.reshape(r.size) for r in res[:-1]]
    n = len(pieces)
    return [flat[k * n:(k + 1) * n] for k in range(4)], res[-1][0, 0]


def _adamw(w, g, m, v, name):
    rows, cols = w.shape
    tr = min(256, rows)

    def body(w_ref, g_ref, m_ref, v_ref, d_ref, m2_ref, v2_ref):
        d_ref[...], m2_ref[...], v2_ref[...] = _adam(w_ref[...], g_ref[...], m_ref[...], v_ref[...])

    spec = _rows(tr, cols)
    shp = jax.ShapeDtypeStruct((rows, cols), F32)
    return _pcall(
        body, name=name, grid=(rows // tr,), in_specs=[spec] * 4, out_specs=[spec] * 3, out_shape=[shp] * 3,
        compiler_params=_params(("parallel",)),
    )(w, g, m, v)


def _pad_lanes(v):
    return jnp.pad(v, (0, LANES - v.shape[0]))


def _pad_head_rows(w_t, heads):
    n = w_t.shape[1]
    return jnp.pad(w_t.reshape(heads, HEAD_DIM, n), ((0, 0), (0, LANES - HEAD_DIM), (0, 0))).reshape(heads * LANES, n)


def _unpad_head_rows(w_t, heads):
    n = w_t.shape[1]
    return w_t.reshape(heads, LANES, n)[:, :HEAD_DIM].reshape(heads * HEAD_DIM, n)


def _in_rows_pad(w_in_t):
    qa, ka, va, qb, kb, vb, f = jnp.split(w_in_t, [512, 640, 768, 1280, 1792, 2304], axis=0)
    f = jnp.pad(f, ((0, 2 * LANES - B_HEADS), (0, 0)))
    return jnp.concatenate([_pad_head_rows(qa, 8), _pad_head_rows(ka, 2), _pad_head_rows(qb, 8),
                            _pad_head_rows(kb, 8), _pad_head_rows(va, 2), _pad_head_rows(vb, 8), f], axis=0)


def _in_rows_unpad(d):
    qa = _unpad_head_rows(d[G_QA * LANES:G_KA * LANES], 8)
    ka = _unpad_head_rows(d[G_KA * LANES:G_QB * LANES], 2)
    qb = _unpad_head_rows(d[G_QB * LANES:G_KB * LANES], 8)
    kb = _unpad_head_rows(d[G_KB * LANES:G_VA * LANES], 8)
    va = _unpad_head_rows(d[G_VA * LANES:G_VB * LANES], 2)
    vb = _unpad_head_rows(d[G_VB * LANES:G_F * LANES], 8)
    f = d[G_F * LANES:G_F * LANES + B_HEADS]
    return jnp.concatenate([qa, ka, va, qb, kb, vb, f], axis=0)


def _pack_small(g1, bf, qa, ka, sk, qb, kb, g2, loss_row):
    rows = [g1.reshape(8, LANES), g2.reshape(8, LANES)]
    rows += [_pad_lanes(t)[None] for t in (qa, ka, qb, kb, bf, sk)]
    rows += [loss_row, jnp.zeros((1, LANES), F32)]
    return jnp.concatenate(rows, axis=0)


def kernel(x, attn_norm_g, w_in, b_forget, q_norm_a, k_norm_a, sink_logits, q_norm_b, k_norm_b, w_out, mlp_norm_g, w_up, w_down, loss_target, m_attn_norm_g, m_w_in, m_b_forget, m_q_norm_a, m_k_norm_a, m_sink_logits, m_q_norm_b, m_k_norm_b, m_w_out, m_mlp_norm_g, m_w_up, m_w_down, v_attn_norm_g, v_w_in, v_b_forget, v_q_norm_a, v_k_norm_a, v_sink_logits, v_q_norm_b, v_k_norm_b, v_w_out, v_mlp_norm_g, v_w_up, v_w_down):
    nb, seq, _ = x.shape
    t_all = nb * seq
    c_idx = lax.axis_index("c")
    s_idx = 2 * lax.axis_index("x") + lax.axis_index("y")

    def my_half(a):
        halves = a.astype(BF16).reshape(2, a.shape[0] // 2, a.shape[1])
        return lax.dynamic_slice_in_dim(halves, c_idx, 1, axis=0)[0]

    w_in_shard_t = jnp.pad(w_in.T, ((0, IN_SHARD_P - IN_SHARD), (0, 0)))
    gathered_in = _allgather_halves(my_half(w_in_shard_t)).reshape(N_CHIPS, IN_SHARD_P, D_MODEL)
    w_pad_t = _in_rows_pad(gathered_in[:, :IN_SHARD].reshape(IN_WIDTH, D_MODEL))

    ones = jnp.ones((LANES,), F32)
    gain_row = jnp.concatenate(
        [jnp.tile(_pad_lanes(q_norm_a), 8), jnp.tile(_pad_lanes(k_norm_a), 2), jnp.tile(_pad_lanes(q_norm_b), 8),
         jnp.tile(_pad_lanes(k_norm_b), 8), jnp.tile(ones, N_GROUPS - N_NORM_GROUPS)])[None]
    b_row = _pad_lanes(b_forget)[None]
    g1 = attn_norm_g[None]
    g2 = mlp_norm_g[None]
    slopes = jnp.exp2(-(8.0 / A_HEADS) * (jnp.arange(A_HEADS, dtype=F32) + 1.0))

    x2 = x.reshape(t_all, D_MODEL)
    tgt = loss_target.reshape(t_all, D_MODEL)

    (xn, pre, qa, ka, va, qb, kb, vb, z), (w_out_g, w_up_g, w_down_f) = _inproj(
        x2, g1, w_pad_t, gain_row, b_row, seq, [my_half(w_out), my_half(w_up), my_half(w_down)])
    wo_pad = _pad_head_rows(w_out_g, A_HEADS + B_HEADS)
    w_up_blocks = w_up_g.reshape(N_CHIPS, D_MODEL, D_MODEL)
    swa_bias = _swa_bias(slopes)
    oa, la = _swa_fwd(qa, ka, va, sink_logits, swa_bias, nb, seq)
    ob, lse = _fox_fwd(qb, kb, vb, nb, seq)
    h, hn = _outproj(x2, oa, ob, wo_pad, g2)
    ru, dy, dyb, loss_acc = _mlp_fwd(hn, w_up_blocks, w_down_f, h, tgt)

    du, d_w_mlp = _mlp_bwd_w(dyb, w_down_f, ru, hn)
    c_arg = c_idx.reshape(1).astype(jnp.int32)
    sc_arg = jnp.stack([s_idx, c_idx]).astype(jnp.int32)
    dh, dhb, d_g2, sibling_w_mlp = _mlp_dhn(du, w_up_blocks, h, dy, g2, d_w_mlp)
    pair_m, pair_m_bf = _rs_pair_add(d_w_mlp, sibling_w_mlp, c_arg, "rs_pair_add_mlp")
    doa, dob, delta_b, d_wo = _dmixed(dhb, wo_pad, oa, ob)
    dqb, dkb, dvb, got_m = _fox_bwd(qb, kb, vb, dob, lse, delta_b, nb, seq, pair_m_bf)
    red_m = _rs_pair_share(_rs_chip_add(pair_m, got_m, sc_arg, "rs_chip_add_mlp"), "rs_pair_share_mlp")
    g_w_up, g_w_down = red_m[0], red_m[1]
    dqa, dka, dva, dsink = _swa_bwd(qa, ka, va, oa, doa, la, sink_logits, swa_bias, nb, seq)
    dproj, small, grad_x, d_g1 = _dproj_dx(pre, dqa, dka, dqb, dkb, dva, dvb, z, x2, dh, gain_row, w_pad_t, g1, seq)
    d_w_in_t = _dwin(dproj, xn)

    d_w_out = _unpad_head_rows(d_wo, A_HEADS + B_HEADS)
    g_att = jnp.concatenate([
        jnp.pad(_in_rows_unpad(d_w_in_t).reshape(N_CHIPS, IN_SHARD, D_MODEL),
                ((0, 0), (0, IN_SHARD_P - IN_SHARD), (0, 0))),
        d_w_out.reshape(N_CHIPS, D_MODEL // N_CHIPS, D_MODEL)], axis=1)
    g_att = jnp.stack([g_att[:, :R_ATT // 2], g_att[:, R_ATT // 2:]])
    pair_a, pair_a_bf = _rs_pair_add(g_att, _rs_pair_exchange(g_att, "rs_pair_exchange_att"), c_arg, "rs_pair_add_att")
    got_a = _rs_chip_exchange(pair_a_bf, "rs_chip_exchange_att")
    red_a = _rs_pair_share(_rs_chip_add(pair_a, got_a, sc_arg, "rs_chip_add_att"), "rs_pair_share_att")
    red_a = red_a.reshape(R_ATT, D_MODEL)
    g_w_in = red_a[:IN_SHARD].T
    g_w_out = red_a[IN_SHARD_P:]

    loss_row = loss_acc[0:1] * (0.5 / D_MODEL)
    d_sink = dsink[:, :A_GROUP, 0].reshape(nb, A_HEADS).sum(axis=0)
    part = _pack_small(d_g1[0], small[4, :B_HEADS], small[0, :HEAD_DIM], small[1, :HEAD_DIM], d_sink,
                       small[2, :HEAD_DIM], small[3, :HEAD_DIM], d_g2[0], loss_row)
    zero_row = jnp.zeros((1, LANES), F32)
    smalls = lambda t: _pack_small(*t, zero_row)
    w_small = smalls((attn_norm_g, b_forget, q_norm_a, k_norm_a, sink_logits, q_norm_b, k_norm_b, mlp_norm_g))
    m_small = smalls((m_attn_norm_g, m_b_forget, m_q_norm_a, m_k_norm_a, m_sink_logits, m_q_norm_b, m_k_norm_b,
                      m_mlp_norm_g))
    v_small = smalls((v_attn_norm_g, v_b_forget, v_q_norm_a, v_k_norm_a, v_sink_logits, v_q_norm_b, v_k_norm_b,
                      v_mlp_norm_g))
    (g_s, d_s, m_s, v_s), loss = _small_adamw(_small_allreduce(part), w_small, m_small, v_small)

    big = {}
    for name, w, g, m, v in (("adamw_w_in", w_in, g_w_in, m_w_in, v_w_in),
                             ("adamw_w_out", w_out, g_w_out, m_w_out, v_w_out),
                             ("adamw_w_up", w_up, g_w_up, m_w_up, v_w_up),
                             ("adamw_w_down", w_down, g_w_down, m_w_down, v_w_down)):
        big[name] = (g,) + tuple(_adamw(w, g, m, v, name))

    def assemble(k, s):
        return (s[0], big["adamw_w_in"][k], s[1], s[2], s[3], s[4], s[5], s[6], big["adamw_w_out"][k], s[7],
                big["adamw_w_up"][k], big["adamw_w_down"][k])

    return (loss, grad_x.reshape(nb, seq, D_MODEL), *assemble(0, g_s), *assemble(1, d_s), *assemble(2, m_s),
            *assemble(3, v_s))
```

```python
import functools

import numpy as np
import jax
import jax.numpy as jnp
from jax import lax
from jax.experimental import pallas as pl
from jax.experimental.pallas import tpu as pltpu

F32 = jnp.float32
BF16 = jnp.bfloat16

D_MODEL = 1024
HEAD_DIM = 64
LANES = 128
A_HEADS = 8
A_KV_HEADS = 2
A_GROUP = A_HEADS // A_KV_HEADS
B_HEADS = 8
WINDOW = 128
D_FF = 4096
IN_WIDTH = 2312
EPS = 1e-6
SCALE = 0.125
LOG2E = 1.4426950408889634
LN2 = 0.6931471805599453
CHUNK = 32
FOX_TK = 512
FOX_PARTS = 8
FOX_PARTS_BWD = 4
NEG = -1e30

G_QA, G_KA, G_QB, G_KB, G_VA, G_VB, G_F = 0, 8, 10, 18, 26, 28, 36
N_NORM_GROUPS = 26
N_GROUPS = 38
NP = N_GROUPS * LANES
MIXED_P = (A_HEADS + B_HEADS) * LANES

N_CHIPS = 4
IN_SHARD = IN_WIDTH // N_CHIPS
IN_SHARD_P = 608
R_ATT = IN_SHARD_P + D_MODEL // N_CHIPS

SMALL_ROWS = 24
ROW_LOSS = 22

ADAM_LR = 0.001
ADAM_B1 = 0.9
ADAM_B2 = 0.999
ADAM_EPS = 1e-08
ADAM_WD = 0.01
ADAM_STEP = 10

VMEM_LIMIT = 52 * 1024 * 1024
MESH = pl.DeviceIdType.MESH


def _pcall(body, **kw):
    return pl.pallas_call(body, **kw)


def _params(sem=None):
    return pltpu.CompilerParams(dimension_semantics=sem, vmem_limit_bytes=VMEM_LIMIT)


def _dot(a, b):
    return jnp.dot(a, b, preferred_element_type=F32)


def _dot_nt(a, b):
    return lax.dot_general(a, b, (((1,), (1,)), ((), ())), preferred_element_type=F32)


def _dot_tn(a, b):
    return lax.dot_general(a, b, (((0,), (0,)), ((), ())), preferred_element_type=F32)


def _split3(x):
    hi = x.astype(BF16)
    r1 = x - hi.astype(F32)
    mid = r1.astype(BF16)
    lo = (r1 - mid.astype(F32)).astype(BF16)
    return hi, mid, lo


def _dot_exact(mat, x):
    hi, mid, lo = _split3(x)
    return _dot(mat, lo) + _dot(mat, mid) + _dot(mat, hi)


def _const(shape):
    zeros = (0,) * len(shape)
    return pl.BlockSpec(shape, lambda *_: zeros)


def _resident(shape):
    zeros = (0,) * len(shape)
    return pl.BlockSpec(shape, lambda *_: zeros, pipeline_mode=pl.Buffered(1))


def _rows(tm, n):
    return pl.BlockSpec((tm, n), lambda i: (i, 0))


def _aug_select():
    e = np.zeros((3 * LANES, 2 * B_HEADS * LANES), np.float32)
    for j in range(3):
        for h in range(B_HEADS):
            e[j * LANES + h, h * LANES + HEAD_DIM + j] = 1.0
            e[j * LANES + h, (B_HEADS + h) * LANES + HEAD_DIM + 3 + j] = -1.0
    return jnp.asarray(e, BF16)


def _dc_select():
    e = np.zeros((2 * B_HEADS * LANES, LANES), np.float32)
    for h in range(B_HEADS):
        e[h * LANES + HEAD_DIM, h] = 1.0
        e[(B_HEADS + h) * LANES + HEAD_DIM + 3, h] = -1.0
    return jnp.asarray(e, BF16)


def _tri(n, upper):
    t = np.tril(np.ones((n, n), np.float32))
    return jnp.asarray(t.T if upper else t, BF16)


def _inproj(x2, g1, w_pad_t, gain_row, b_row, seq, later_weights):
    t_all = x2.shape[0]
    tm = min(256, seq)
    n_steps = t_all // tm
    forward_step = max(n_steps - 2, 0)
    tiles_per_seq = seq // tm
    tri = _tri(tm, False)
    esel = _aug_select()
    n_later = len(later_weights)

    def body(x_ref, g_ref, w_ref, gain_ref, b_ref, tri_ref, e_ref, *rest):
        later_src, rest = rest[:n_later], rest[n_later:]
        xn_ref, pre_ref, qa_ref, ka_ref, va_ref, qb_ref, kb_ref, vb_ref, z_ref = rest[:9]
        later_out, (carry_ref, send_sems, recv_sems, local_sems) = rest[9:9 + n_later], rest[9 + n_later:]
        i = pl.program_id(0)
        gather = _Gather(later_src, later_out, send_sems, recv_sems, local_sems)

        @pl.when(i == 0)
        def _():
            gather.start()

        @pl.when(i == forward_step)
        def _():
            gather.forward()

        @pl.when(i % tiles_per_seq == 0)
        def _():
            carry_ref[...] = jnp.zeros_like(carry_ref)

        x = x_ref[...]
        r = lax.rsqrt(jnp.mean(x * x, axis=-1, keepdims=True) + EPS)
        xn = (x * r * g_ref[...]).astype(BF16)
        xn_ref[...] = xn
        proj = _dot_nt(xn, w_ref[...])
        pre_ref[...] = proj[:, :N_NORM_GROUPS * LANES].astype(BF16)
        lane = lax.broadcasted_iota(jnp.int32, (tm, LANES), 1)

        z = proj[:, G_F * LANES:(G_F + 1) * LANES] + b_ref[...]
        z_ref[...] = z
        lf = jnp.minimum(z, 0.0) - jnp.log(1.0 + jnp.exp(-jnp.abs(z)))
        lf = jnp.where(lane < B_HEADS, lf, 0.0)
        c = _dot_exact(tri_ref[...], lf) + carry_ref[...]
        carry_ref[...] += jnp.sum(lf, axis=0, keepdims=True)
        aug = _dot(jnp.concatenate(_split3(c * LOG2E), axis=1), e_ref[...])

        def hnorm(g):
            p = proj[:, g * LANES:(g + 1) * LANES]
            rr = lax.rsqrt(jnp.sum(p * p, axis=-1, keepdims=True) * (1.0 / HEAD_DIM) + EPS)
            return p * rr * gain_ref[:, g * LANES:(g + 1) * LANES]

        ones_q = jnp.where((lane >= HEAD_DIM + 3) & (lane < HEAD_DIM + 6), 1.0, 0.0)
        ones_k = jnp.where((lane >= HEAD_DIM) & (lane < HEAD_DIM + 3), 1.0, 0.0)
        for h in range(A_HEADS):
            qa_ref[:, h * LANES:(h + 1) * LANES] = (hnorm(G_QA + h) * SCALE).astype(BF16)
        for h in range(A_KV_HEADS):
            ka_ref[:, h * LANES:(h + 1) * LANES] = hnorm(G_KA + h).astype(BF16)
        for h in range(B_HEADS):
            qb_ref[:, h * LANES:(h + 1) * LANES] = (
                hnorm(G_QB + h) * (SCALE * LOG2E) + aug[:, h * LANES:(h + 1) * LANES] + ones_q).astype(BF16)
            kb_ref[:, h * LANES:(h + 1) * LANES] = (
                hnorm(G_KB + h) + aug[:, (B_HEADS + h) * LANES:(B_HEADS + h + 1) * LANES] + ones_k).astype(BF16)
        va_ref[...] = proj[:, G_VA * LANES:G_VB * LANES].astype(BF16)
        one_v = jnp.where(lane == HEAD_DIM, 1.0, 0.0)
        for h in range(B_HEADS):
            cols = slice((G_VB + h) * LANES, (G_VB + h + 1) * LANES)
            vb_ref[:, h * LANES:(h + 1) * LANES] = (proj[:, cols] + one_v).astype(BF16)

        @pl.when(i == n_steps - 1)
        def _():
            gather.finish()

    widths = [(D_MODEL, BF16), (N_NORM_GROUPS * LANES, BF16), (A_HEADS * LANES, BF16), (A_KV_HEADS * LANES, BF16),
              (A_KV_HEADS * LANES, BF16), (B_HEADS * LANES, BF16), (B_HEADS * LANES, BF16), (B_HEADS * LANES, BF16),
              (LANES, F32)]
    res = _pcall(
        body, name="inproj", grid=(n_steps,),
        in_specs=[_rows(tm, D_MODEL), _const((1, D_MODEL)), _const((NP, D_MODEL)), _const((1, NP)),
                  _const((1, LANES)), _const((tm, tm)), _const(esel.shape)] + [ANY] * n_later,
        out_specs=[_rows(tm, w) for w, _ in widths] + [ANY] * n_later,
        out_shape=[jax.ShapeDtypeStruct((t_all, w), dt) for w, dt in widths]
        + [jax.ShapeDtypeStruct((8 * w.shape[0], w.shape[1]), w.dtype) for w in later_weights],
        scratch_shapes=[pltpu.VMEM((1, LANES), F32)] + _gather_scratch(n_later),
        compiler_params=_params(("arbitrary",)),
    )(x2, g1, w_pad_t, gain_row, b_row, tri, esel, *later_weights)
    return res[:9], res[9:]


def _fox_fwd(qb, kb, vb, nb, seq):
    t_all = qb.shape[0]
    tk = min(FOX_TK, seq // FOX_PARTS)
    tq = FOX_PARTS * tk
    nq = seq // tq

    def body(q_ref, k_ref, v_ref, o_ref, lse_ref, s_ref, p_ref, m_ref, alpha_ref, acc_ref):
        qi = pl.program_id(2)
        q = q_ref[...]
        m_ref[...] = jnp.full((tq, LANES), NEG, F32)
        acc_ref[...] = jnp.zeros((tq, LANES), F32)

        def step(j, modes):
            off = pl.multiple_of(j * tk, tk)
            k = k_ref[pl.ds(off, tk), :]
            v = v_ref[pl.ds(off, tk), :]
            live = [hf for hf in range(FOX_PARTS) if modes[hf] is not None]
            for hf in live:
                s_ref[hf] = _dot_nt(q[hf * tk:(hf + 1) * tk], k)
            for hf in live:
                for r in range(0, tk, CHUNK):
                    rows = slice(r, r + CHUNK)
                    grows = slice(hf * tk + r, hf * tk + r + CHUNK)
                    tiles = []
                    for jt in range(tk // LANES):
                        sc = s_ref[hf, rows, jt * LANES:(jt + 1) * LANES]
                        if modes[hf] == "diag":
                            row = r + lax.broadcasted_iota(jnp.int32, (CHUNK, LANES), 0)
                            col = jt * LANES + lax.broadcasted_iota(jnp.int32, (CHUNK, LANES), 1)
                            sc = jnp.where(row >= col, sc, NEG)
                        tiles.append(sc)
                    m_prev = m_ref[grows, :]
                    m_cur = functools.reduce(jnp.maximum, tiles)
                    m_new = jnp.maximum(m_prev, jnp.max(m_cur, axis=-1, keepdims=True))
                    m_ref[grows, :] = m_new
                    alpha_ref[grows, :] = jnp.exp2(m_prev - m_new)
                    for jt, sc in enumerate(tiles):
                        p_ref[hf, rows, jt * LANES:(jt + 1) * LANES] = jnp.exp2(sc - m_new).astype(BF16)
                hrows = slice(hf * tk, (hf + 1) * tk)
                acc_ref[hrows, :] = alpha_ref[hrows, :] * acc_ref[hrows, :] + _dot(p_ref[hf], v)

        def past(j, carry):
            step(j, ("full",) * FOX_PARTS)
            return carry

        lax.fori_loop(0, FOX_PARTS * qi, past, 0)
        for d in range(FOX_PARTS):
            step(FOX_PARTS * qi + d, (None,) * d + ("diag",) + ("full",) * (FOX_PARTS - 1 - d))
        acc = acc_ref[...]
        lane = lax.broadcasted_iota(jnp.int32, (tq, LANES), 1)
        l = jnp.sum(jnp.where(lane == HEAD_DIM, acc, 0.0), axis=-1, keepdims=True)
        o_ref[...] = (acc / l).astype(BF16)
        lse_ref[...] = m_ref[...] + jnp.log2(l)

    qspec = pl.BlockSpec((tq, LANES), lambda b, h, i: (b * nq + i, h))
    kspec = pl.BlockSpec((seq, LANES), lambda b, h, i: (b, h))
    return _pcall(
        body, name="fox_fwd", grid=(nb, B_HEADS, nq),
        in_specs=[qspec, kspec, kspec], out_specs=[qspec, qspec],
        out_shape=[jax.ShapeDtypeStruct((t_all, B_HEADS * LANES), BF16),
                   jax.ShapeDtypeStruct((t_all, B_HEADS * LANES), F32)],
        scratch_shapes=[pltpu.VMEM((FOX_PARTS, tk, tk), F32), pltpu.VMEM((FOX_PARTS, tk, tk), BF16),
                        pltpu.VMEM((tq, LANES), F32),
                        pltpu.VMEM((tq, LANES), F32), pltpu.VMEM((tq, LANES), F32)],
        compiler_params=_params(("parallel", "parallel", "arbitrary")),
    )(qb, kb, vb)


def _swa_bias(slopes):
    row = jnp.arange(A_GROUP * WINDOW, dtype=jnp.int32)[:, None] % WINDOW
    col = jnp.arange(2 * WINDOW, dtype=jnp.int32)[None, :]
    slope_rows = jnp.repeat(slopes.reshape(A_KV_HEADS, A_GROUP), WINDOW, axis=1)[:, :, None]
    out = []
    for t_rel in (0, WINDOW):
        dist = t_rel + row - col
        valid = (dist >= 0) & (dist < WINDOW)
        out.append(jnp.where(valid[None], -slope_rows * dist.astype(F32)[None], NEG))
    return jnp.stack(out)


def _stack_heads(ref, rows):
    return jnp.concatenate([ref[rows, j * LANES:(j + 1) * LANES] for j in range(A_GROUP)], axis=0)


def _sink_rows(sink_ref, g):
    return jnp.concatenate([jnp.full((WINDOW, LANES), sink_ref[g * A_GROUP + j], F32) for j in range(A_GROUP)], axis=0)


def _rep(col):
    return jnp.broadcast_to(col, (col.shape[0], LANES))


def _swa_specs(nq, tq, seq):
    smem = pl.BlockSpec(memory_space=pltpu.SMEM)
    qspec = pl.BlockSpec((tq, A_GROUP * LANES), lambda b, g, i: (b * nq + i, g))
    kspec = pl.BlockSpec((seq, LANES), lambda b, g, i: (b, g))
    bias_first = pl.BlockSpec((None, None, A_GROUP * WINDOW, 2 * WINDOW),
                              lambda b, g, i: (jnp.minimum(i, 1), g, 0, 0))
    bias_rest = pl.BlockSpec((None, None, A_GROUP * WINDOW, 2 * WINDOW), lambda b, g, i: (1, g, 0, 0))
    return smem, qspec, kspec, bias_first, bias_rest


def _swa_fwd(qa, ka, va, sinks, bias, nb, seq):
    t_all = qa.shape[0]
    tq = min(1024, seq)
    nq = seq // tq

    def body(sink_ref, q_ref, k_ref, v_ref, bias0_ref, bias_ref, o_ref, l_ref):
        qi = pl.program_id(2)
        sink = _sink_rows(sink_ref, pl.program_id(1))
        for a in range(tq // WINDOW):
            t0 = qi * tq + a * WINDOW
            start = pl.multiple_of(jnp.maximum(t0 - WINDOW, 0), WINDOW)
            rows = slice(a * WINDOW, (a + 1) * WINDOW)
            k = k_ref[pl.ds(start, 2 * WINDOW), :]
            v = v_ref[pl.ds(start, 2 * WINDOW), :]
            s = _dot_nt(_stack_heads(q_ref, rows), k) + (bias0_ref if a == 0 else bias_ref)[...]
            s0, s1 = s[:, :LANES], s[:, LANES:]
            m = jnp.maximum(_rep(jnp.max(jnp.maximum(s0, s1), axis=-1, keepdims=True)), sink)
            p0, p1 = jnp.exp(s0 - m), jnp.exp(s1 - m)
            den = _rep(jnp.sum(p0 + p1, axis=-1, keepdims=True)) + jnp.exp(sink - m)
            inv = 1.0 / den
            o = _dot(jnp.concatenate([(p0 * inv).astype(BF16), (p1 * inv).astype(BF16)], axis=1), v).astype(BF16)
            lrow = m + jnp.log(den)
            for j in range(A_GROUP):
                o_ref[rows, j * LANES:(j + 1) * LANES] = o[j * WINDOW:(j + 1) * WINDOW]
                l_ref[rows, j * LANES:(j + 1) * LANES] = lrow[j * WINDOW:(j + 1) * WINDOW]

    smem, qspec, kspec, bias_first, bias_rest = _swa_specs(nq, tq, seq)
    return _pcall(
        body, name="swa_fwd", grid=(nb, A_KV_HEADS, nq),
        in_specs=[smem, qspec, kspec, kspec, bias_first, bias_rest], out_specs=[qspec, qspec],
        out_shape=[jax.ShapeDtypeStruct((t_all, A_HEADS * LANES), BF16),
                   jax.ShapeDtypeStruct((t_all, A_HEADS * LANES), F32)],
        compiler_params=_params(("parallel", "parallel", "arbitrary")),
    )(sinks, qa, ka, va, bias, bias)


def _outproj(x2, oa, ob, wo_pad, g2):
    t_all = x2.shape[0]
    tm = min(512, t_all)
    half = A_HEADS * LANES

    def body(x_ref, oa_ref, ob_ref, w_ref, g_ref, h_ref, hn_ref):
        h = x_ref[...] + _dot(oa_ref[...], w_ref[:half, :]) + _dot(ob_ref[...], w_ref[half:, :])
        h_ref[...] = h
        r = lax.rsqrt(jnp.mean(h * h, axis=-1, keepdims=True) + EPS)
        hn_ref[...] = (h * r * g_ref[...]).astype(BF16)

    return _pcall(
        body, name="outproj", grid=(t_all // tm,),
        in_specs=[_rows(tm, D_MODEL), _rows(tm, half), _rows(tm, half), _const((MIXED_P, D_MODEL)),
                  _const((1, D_MODEL))],
        out_specs=[_rows(tm, D_MODEL), _rows(tm, D_MODEL)],
        out_shape=[jax.ShapeDtypeStruct((t_all, D_MODEL), F32), jax.ShapeDtypeStruct((t_all, D_MODEL), BF16)],
        compiler_params=_params(("parallel",)),
    )(x2, oa, ob, wo_pad, g2)


def _mlp_fwd(hn, w_up_blocks, w_down, h, tgt):
    t_all = h.shape[0]
    tm = min(512, t_all)
    nj = D_FF // D_MODEL

    def body(a_ref, wu_ref, wd_ref, h_ref, t_ref, ru_ref, dy_ref, dyb_ref, loss_ref):
        @pl.when(pl.program_id(0) == 0)
        def _():
            loss_ref[...] = jnp.zeros_like(loss_ref)

        a = a_ref[...]
        y = h_ref[...]
        for j in range(nj):
            cols = slice(j * D_MODEL, (j + 1) * D_MODEL)
            ru = jnp.maximum(_dot(a, wu_ref[j]), 0.0)
            ru_ref[:, cols] = ru.astype(BF16)
            y = y + _dot((ru * ru).astype(BF16), wd_ref[cols, :])
        err = y - t_ref[...]
        loss_ref[...] += jnp.sum(err * err)
        dy = err * (1.0 / D_MODEL)
        dy_ref[...] = dy
        dyb_ref[...] = dy.astype(BF16)

    return _pcall(
        body, name="mlp_fwd", grid=(t_all // tm,),
        in_specs=[_rows(tm, D_MODEL), _resident((nj, D_MODEL, D_MODEL)), _resident((D_FF, D_MODEL)), _rows(tm, D_MODEL),
                  _rows(tm, D_MODEL)],
        out_specs=[_rows(tm, D_FF), _rows(tm, D_MODEL), _rows(tm, D_MODEL), _const((8, LANES))],
        out_shape=[jax.ShapeDtypeStruct((t_all, D_FF), BF16), jax.ShapeDtypeStruct((t_all, D_MODEL), F32),
                   jax.ShapeDtypeStruct((t_all, D_MODEL), BF16), jax.ShapeDtypeStruct((8, LANES), F32)],
        compiler_params=_params(("arbitrary",)),
    )(hn, w_up_blocks, w_down, h, tgt)


def _mlp_bwd_w(dyb, w_down, ru, hn):
    t_all = dyb.shape[0]
    tm = min(512, t_all)
    nj = D_FF // D_MODEL

    def body(dy_ref, w_ref, ru_ref, hn_ref, du_ref, dw_ref):
        @pl.when(pl.program_id(1) == 0)
        def _():
            dw_ref[...] = jnp.zeros_like(dw_ref)

        dy = dy_ref[...]
        ru = ru_ref[...].astype(F32)
        du = (_dot_nt(dy, w_ref[...]) * (2.0 * ru)).astype(BF16)
        du_ref[...] = du
        dw_ref[0] += _dot_tn(hn_ref[...], du)
        dw_ref[1] += _dot_tn((ru * ru).astype(BF16), dy)

    tok = pl.BlockSpec((tm, D_MODEL), lambda j, i: (i, 0))
    blk = pl.BlockSpec((tm, D_MODEL), lambda j, i: (i, j))
    wspec = pl.BlockSpec((2, None, D_MODEL, D_MODEL), lambda j, i: (0, j, 0, 0))
    return _pcall(
        body, name="mlp_bwd_w", grid=(nj, t_all // tm),
        in_specs=[tok, pl.BlockSpec((D_MODEL, D_MODEL), lambda j, i: (j, 0)), blk, tok],
        out_specs=[blk, wspec],
        out_shape=[jax.ShapeDtypeStruct((t_all, D_FF), BF16), jax.ShapeDtypeStruct((2, nj, D_MODEL, D_MODEL), F32)],
        compiler_params=_params(("parallel", "arbitrary")),
    )(dyb, w_down, ru, hn)


def _pair_exchange_copy(g_ref, out_ref, send_sem, recv_sem):
    x, y, c = _place()
    return pltpu.make_async_remote_copy(
        src_ref=g_ref.at[1 - c], dst_ref=out_ref, send_sem=send_sem, recv_sem=recv_sem,
        device_id=(x, y, 1 - c), device_id_type=MESH)


def _mlp_dhn(du, w_up_blocks, h, dy, g2, d_w_mlp):
    t_all = h.shape[0]
    tm = min(512, t_all)
    n_steps = t_all // tm

    def body(a_ref, w_ref, h_ref, dy_ref, g_ref, dw_ref, dh_ref, dhb_ref, dg_ref, got_ref, send_sem, recv_sem):
        @pl.when(pl.program_id(0) == 0)
        def _():
            dg_ref[...] = jnp.zeros_like(dg_ref)
            _pair_exchange_copy(dw_ref, got_ref, send_sem, recv_sem).start()

        dhn = _dot_nt(a_ref[:, :D_MODEL], w_ref[0])
        for j in range(1, D_FF // D_MODEL):
            dhn = dhn + _dot_nt(a_ref[:, j * D_MODEL:(j + 1) * D_MODEL], w_ref[j])
        h = h_ref[...]
        r = lax.rsqrt(jnp.mean(h * h, axis=-1, keepdims=True) + EPS)
        hh = h * r
        dg_ref[...] += jnp.sum(dhn * hh, axis=0, keepdims=True)
        dz = dhn * g_ref[...]
        dh = dy_ref[...] + r * (dz - hh * jnp.mean(dz * hh, axis=-1, keepdims=True))
        dh_ref[...] = dh
        dhb_ref[...] = dh.astype(BF16)

        @pl.when(pl.program_id(0) == n_steps - 1)
        def _():
            _pair_exchange_copy(dw_ref, got_ref, send_sem, recv_sem).wait()

    return _pcall(
        body, name="mlp_dhn", grid=(n_steps,),
        in_specs=[_rows(tm, D_FF), _resident((D_FF // D_MODEL, D_MODEL, D_MODEL)), _rows(tm, D_MODEL),
                  _rows(tm, D_MODEL), _const((1, D_MODEL)), ANY],
        out_specs=[_rows(tm, D_MODEL), _rows(tm, D_MODEL), _const((1, D_MODEL)), ANY],
        out_shape=[jax.ShapeDtypeStruct((t_all, D_MODEL), F32), jax.ShapeDtypeStruct((t_all, D_MODEL), BF16),
                   jax.ShapeDtypeStruct((1, D_MODEL), F32), jax.ShapeDtypeStruct(d_w_mlp.shape[1:], F32)],
        scratch_shapes=[pltpu.SemaphoreType.DMA, pltpu.SemaphoreType.DMA],
        compiler_params=_params(("arbitrary",)),
    )(du, w_up_blocks, h, dy, g2, d_w_mlp)


def _dmixed(dhb, wo_pad, oa, ob):
    t_all = dhb.shape[0]
    tm = min(512, t_all)
    half = A_HEADS * LANES

    def body(a_ref, w_ref, oa_ref, ob_ref, da_ref, db_ref, delta_ref, dwo_ref):
        @pl.when(pl.program_id(0) == 0)
        def _():
            dwo_ref[...] = jnp.zeros_like(dwo_ref)

        a = a_ref[...]
        d = _dot_nt(a, w_ref[...])
        da_ref[...] = d[:, :half].astype(BF16)
        db_ref[...] = d[:, half:].astype(BF16)
        for h in range(B_HEADS):
            cols = slice(h * LANES, (h + 1) * LANES)
            prod = d[:, half + h * LANES:half + (h + 1) * LANES] * ob_ref[:, cols].astype(F32)
            delta_ref[:, cols] = jnp.broadcast_to(jnp.sum(prod, axis=-1, keepdims=True), (tm, LANES))
        dwo_ref[:half, :] += _dot_tn(oa_ref[...], a)
        dwo_ref[half:, :] += _dot_tn(ob_ref[...], a)

    return _pcall(
        body, name="dmixed", grid=(t_all // tm,),
        in_specs=[_rows(tm, D_MODEL), _const((MIXED_P, D_MODEL)), _rows(tm, half), _rows(tm, half)],
        out_specs=[_rows(tm, half), _rows(tm, half), _rows(tm, half), _const((MIXED_P, D_MODEL))],
        out_shape=[jax.ShapeDtypeStruct((t_all, half), BF16), jax.ShapeDtypeStruct((t_all, half), BF16),
                   jax.ShapeDtypeStruct((t_all, half), F32), jax.ShapeDtypeStruct((MIXED_P, D_MODEL), F32)],
        compiler_params=_params(("arbitrary",)),
    )(dhb, wo_pad, oa, ob)


def _fox_bwd(qb, kb, vb, dob, lse, delta, nb, seq, pair_sums):
    t_all = qb.shape[0]
    tk = min(FOX_TK, seq // FOX_PARTS_BWD)
    tq = FOX_PARTS_BWD * tk
    nk = seq // tk

    def body(q_ref, k_ref, v_ref, do_ref, lse_ref, delta_ref, pair_ref, dq_ref, dk_ref, dv_ref, got_ref,
             s_ref, dp_ref, p_ref, ds_ref, dk_acc, dv_acc, send_sems, recv_sems):
        kj = pl.program_id(2)
        bh = pl.program_id(0) * B_HEADS + pl.program_id(1)

        @pl.when((bh == 0) & (kj == 0))
        def _():
            for cp in _chip_exchange_copies(pair_ref, got_ref, send_sems, recv_sems):
                cp.start()

        @pl.when(kj == 0)
        def _():
            dq_ref[...] = jnp.zeros_like(dq_ref)

        dk_acc[...] = jnp.zeros_like(dk_acc)
        dv_acc[...] = jnp.zeros_like(dv_acc)
        k = k_ref[...]
        v = v_ref[...]

        def block(off, r0, r1, masked):
            qrows = pl.ds(pl.multiple_of(off + r0, CHUNK), r1 - r0)
            q = q_ref[qrows, :]
            do = do_ref[qrows, :]
            s_ref[r0:r1, :] = _dot_nt(q, k)
            dp_ref[r0:r1, :] = _dot_nt(do, v)
            for r in range(r0, r1, CHUNK):
                rows = slice(r, r + CHUNK)
                chunk = pl.ds(pl.multiple_of(off + r, CHUNK), CHUNK)
                lse_c = lse_ref[chunk, :]
                delta_c = delta_ref[chunk, :]
                for jt in range(tk // LANES):
                    cols = slice(jt * LANES, (jt + 1) * LANES)
                    p = jnp.exp2(s_ref[rows, cols] - lse_c)
                    if masked:
                        row = r - r0 + lax.broadcasted_iota(jnp.int32, (CHUNK, LANES), 0)
                        col = jt * LANES + lax.broadcasted_iota(jnp.int32, (CHUNK, LANES), 1)
                        p = jnp.where(row >= col, p, 0.0)
                    p_ref[rows, cols] = p.astype(BF16)
                    ds_ref[rows, cols] = (p * (dp_ref[rows, cols] - delta_c)).astype(BF16)
            dv_acc[...] += _dot_tn(p_ref[r0:r1, :], do)
            dk_acc[...] += _dot_tn(ds_ref[r0:r1, :], q)
            dq_ref[qrows, :] += _dot(ds_ref[r0:r1, :], k)

        first = kj // FOX_PARTS_BWD
        off_first = pl.multiple_of(first * tq, tq)
        for d in range(FOX_PARTS_BWD):
            @pl.when(kj % FOX_PARTS_BWD == d)
            def _(d=d):
                block(off_first, d * tk, (d + 1) * tk, True)
                if d < FOX_PARTS_BWD - 1:
                    block(off_first, (d + 1) * tk, tq, False)

        def later(i, carry):
            block(pl.multiple_of(i * tq, tq), 0, tq, False)
            return carry

        lax.fori_loop(first + 1, seq // tq, later, 0)
        dk_ref[...] = dk_acc[...]
        dv_ref[...] = dv_acc[...]

        @pl.when((bh == nb * B_HEADS - 1) & (kj == nk - 1))
        def _():
            for cp in _chip_exchange_copies(pair_ref, got_ref, send_sems, recv_sems):
                cp.wait()

    full = pl.BlockSpec((seq, LANES), lambda b, h, j: (b, h))
    tile = pl.BlockSpec((tk, LANES), lambda b, h, j: (b * nk + j, h))
    shp = jax.ShapeDtypeStruct((t_all, B_HEADS * LANES), F32)
    return _pcall(
        body, name="fox_bwd", grid=(nb, B_HEADS, nk),
        in_specs=[full, tile, tile, full, full, full, ANY], out_specs=[full, tile, tile, ANY],
        out_shape=[shp, shp, shp, jax.ShapeDtypeStruct((3,) + pair_sums.shape[1:], pair_sums.dtype)],
        scratch_shapes=[pltpu.VMEM((tq, tk), F32), pltpu.VMEM((tq, tk), F32), pltpu.VMEM((tq, tk), BF16),
                        pltpu.VMEM((tq, tk), BF16), pltpu.VMEM((tk, LANES), F32), pltpu.VMEM((tk, LANES), F32),
                        pltpu.SemaphoreType.DMA((3,)), pltpu.SemaphoreType.DMA((3,))],
        compiler_params=_params(("arbitrary", "arbitrary", "arbitrary")),
    )(qb, kb, vb, dob, lse, delta, pair_sums)


def _swa_bwd(qa, ka, va, oa, doa, lrow, sinks, bias, nb, seq):
    t_all = qa.shape[0]
    tq = min(1024, seq)
    nq = seq // tq

    def body(sink_ref, q_ref, k_ref, v_ref, bias0_ref, bias_ref, o_ref, do_ref, l_ref,
             dq_ref, dk_ref, dv_ref, dsink_ref):
        qi = pl.program_id(2)
        sink = _sink_rows(sink_ref, pl.program_id(1))

        @pl.when(qi == 0)
        def _():
            dk_ref[...] = jnp.zeros_like(dk_ref)
            dv_ref[...] = jnp.zeros_like(dv_ref)
            dsink_ref[...] = jnp.zeros_like(dsink_ref)

        for a in range(tq // WINDOW):
            t0 = qi * tq + a * WINDOW
            start = pl.multiple_of(jnp.maximum(t0 - WINDOW, 0), WINDOW)
            rows = slice(a * WINDOW, (a + 1) * WINDOW)
            win = pl.ds(start, 2 * WINDOW)
            q = _stack_heads(q_ref, rows)
            k = k_ref[win, :]
            v = v_ref[win, :]
            do = _stack_heads(do_ref, rows)
            lrow = _stack_heads(l_ref, rows)
            s = _dot_nt(q, k) + (bias0_ref if a == 0 else bias_ref)[...]
            dp = _dot_nt(do, v)
            delta = _rep(jnp.sum(do.astype(F32) * _stack_heads(o_ref, rows).astype(F32), axis=-1, keepdims=True))
            p = [jnp.exp(s[:, t * LANES:(t + 1) * LANES] - lrow) for t in range(2)]
            ds = jnp.concatenate([(p[t] * (dp[:, t * LANES:(t + 1) * LANES] - delta)).astype(BF16) for t in range(2)],
                                 axis=1)
            dq = _dot(ds, k)
            dk_ref[win, :] += _dot_tn(ds, q)
            dv_ref[win, :] += _dot_tn(jnp.concatenate([p[0].astype(BF16), p[1].astype(BF16)], axis=1), do)
            sink_term = jnp.exp(sink - lrow) * delta
            for j in range(A_GROUP):
                part = slice(j * WINDOW, (j + 1) * WINDOW)
                dq_ref[rows, j * LANES:(j + 1) * LANES] = dq[part]
                dsink_ref[j:j + 1, :] -= jnp.sum(sink_term[part], axis=0, keepdims=True)

    smem, qspec, kspec, bias_first, bias_rest = _swa_specs(nq, tq, seq)
    return _pcall(
        body, name="swa_bwd", grid=(nb, A_KV_HEADS, nq),
        in_specs=[smem, qspec, kspec, kspec, bias_first, bias_rest, qspec, qspec, qspec],
        out_specs=[qspec, kspec, kspec, pl.BlockSpec((None, 8, LANES), lambda b, g, i: (b * A_KV_HEADS + g, 0, 0))],
        out_shape=[jax.ShapeDtypeStruct((t_all, A_HEADS * LANES), F32),
                   jax.ShapeDtypeStruct((t_all, A_KV_HEADS * LANES), F32),
                   jax.ShapeDtypeStruct((t_all, A_KV_HEADS * LANES), F32),
                   jax.ShapeDtypeStruct((nb * A_KV_HEADS, 8, LANES), F32)],
        compiler_params=_params(("parallel", "parallel", "arbitrary")),
    )(sinks, qa, ka, va, bias, bias, oa, doa, lrow)


def _dproj_dx(pre, dqa, dka, dqb, dkb, dva, dvb, z, x2, dh, gain_row, w_pad_t, g1, seq):
    t_all = pre.shape[0]
    tm = min(256, seq)
    nt = t_all // tm
    tiles_per_seq = seq // tm
    triu = _tri(tm, True)
    sel = _dc_select()

    def body(pre_ref, dqa_ref, dka_ref, dqb_ref, dkb_ref, dva_ref, dvb_ref, z_ref, x_ref, dh_ref, gain_ref, triu_ref,
             sel_ref, w_ref, g_ref, dproj_ref, small_ref, dx_ref, dg_ref, carry_ref):
        i = pl.program_id(0)

        @pl.when(i == 0)
        def _():
            small_ref[...] = jnp.zeros_like(small_ref)
            dg_ref[...] = jnp.zeros_like(dg_ref)

        @pl.when(i % tiles_per_seq == 0)
        def _():
            carry_ref[...] = jnp.zeros_like(carry_ref)

        def norm_bwd(g, dhat):
            cols = slice(g * LANES, (g + 1) * LANES)
            p = pre_ref[:, cols].astype(F32)
            rr = lax.rsqrt(jnp.sum(p * p, axis=-1, keepdims=True) * (1.0 / HEAD_DIM) + EPS)
            n = p * rr
            dz = dhat * gain_ref[:, cols]
            dproj_ref[:, cols] = (rr * (dz - n * (jnp.sum(dz * n, axis=-1, keepdims=True) * (1.0 / HEAD_DIM)))
                                  ).astype(BF16)
            return jnp.sum(dhat * n, axis=0, keepdims=True)

        def group_sum(g0, d_ref, count, scale):
            acc = jnp.zeros((1, LANES), F32)
            for h in range(count):
                d = d_ref[:, h * LANES:(h + 1) * LANES]
                acc = acc + norm_bwd(g0 + h, d * scale if scale != 1.0 else d)
            return acc

        small_ref[0:1, :] += group_sum(G_QA, dqa_ref, A_HEADS, SCALE)
        small_ref[1:2, :] += group_sum(G_KA, dka_ref, A_KV_HEADS, 1.0)
        small_ref[2:3, :] += group_sum(G_QB, dqb_ref, B_HEADS, SCALE)
        small_ref[3:4, :] += group_sum(G_KB, dkb_ref, B_HEADS, LN2)
        dproj_ref[:, G_VA * LANES:G_VB * LANES] = dva_ref[...].astype(BF16)
        dproj_ref[:, G_VB * LANES:G_F * LANES] = dvb_ref[...].astype(BF16)

        dc = jnp.zeros((tm, LANES), F32)
        for piece_q, piece_k in zip(_split3(dqb_ref[...]), _split3(dkb_ref[...])):
            dc = dc + _dot(jnp.concatenate([piece_q, piece_k], axis=1), sel_ref[...])
        dlf = _dot_exact(triu_ref[...], dc) + carry_ref[...]
        carry_ref[...] += jnp.sum(dc, axis=0, keepdims=True)
        dz = dlf / (1.0 + jnp.exp(z_ref[...]))
        small_ref[4:5, :] += jnp.sum(dz, axis=0, keepdims=True)
        dproj_ref[:, G_F * LANES:(G_F + 1) * LANES] = dz.astype(BF16)
        dproj_ref[:, (G_F + 1) * LANES:] = jnp.zeros((tm, LANES), BF16)

        dxn = _dot(dproj_ref[...], w_ref[...])
        x = x_ref[...]
        r = lax.rsqrt(jnp.mean(x * x, axis=-1, keepdims=True) + EPS)
        xh = x * r
        dg_ref[...] += jnp.sum(dxn * xh, axis=0, keepdims=True)
        dxz = dxn * g_ref[...]
        dx_ref[...] = dh_ref[...] + r * (dxz - xh * jnp.mean(dxz * xh, axis=-1, keepdims=True))

    def rev(n):
        return pl.BlockSpec((tm, n), lambda i: (nt - 1 - i, 0))

    return _pcall(
        body, name="dproj_dx", grid=(nt,),
        in_specs=[rev(N_NORM_GROUPS * LANES), rev(A_HEADS * LANES), rev(A_KV_HEADS * LANES), rev(B_HEADS * LANES),
                  rev(B_HEADS * LANES), rev(A_KV_HEADS * LANES), rev(B_HEADS * LANES), rev(LANES), rev(D_MODEL),
                  rev(D_MODEL), _const((1, NP)), _const((tm, tm)), _const(sel.shape), _const((NP, D_MODEL)),
                  _const((1, D_MODEL))],
        out_specs=[rev(NP), _const((8, LANES)), rev(D_MODEL), _const((1, D_MODEL))],
        out_shape=[jax.ShapeDtypeStruct((t_all, NP), BF16), jax.ShapeDtypeStruct((8, LANES), F32),
                   jax.ShapeDtypeStruct((t_all, D_MODEL), F32), jax.ShapeDtypeStruct((1, D_MODEL), F32)],
        scratch_shapes=[pltpu.VMEM((1, LANES), F32)],
        compiler_params=_params(("arbitrary",)),
    )(pre, dqa, dka, dqb, dkb, dva, dvb, z, x2, dh, gain_row, triu, sel, w_pad_t, g1)


def _dwin(dproj, xn):
    t_all = xn.shape[0]
    tt = min(512, t_all)
    half = NP // 2

    def body(a_ref, b_ref, o_ref):
        @pl.when(pl.program_id(1) == 0)
        def _():
            o_ref[...] = jnp.zeros_like(o_ref)

        o_ref[...] += _dot_tn(a_ref[...], b_ref[...])

    return _pcall(
        body, name="dwin", grid=(2, t_all // tt),
        in_specs=[pl.BlockSpec((tt, half), lambda j, t: (t, j)), pl.BlockSpec((tt, D_MODEL), lambda j, t: (t, 0))],
        out_specs=pl.BlockSpec((half, D_MODEL), lambda j, t: (j, 0)),
        out_shape=jax.ShapeDtypeStruct((NP, D_MODEL), F32),
        compiler_params=_params(("parallel", "arbitrary")),
    )(dproj, xn)


ANY = pl.BlockSpec(memory_space=pl.ANY)


def _place():
    return lax.axis_index("x"), lax.axis_index("y"), lax.axis_index("c")


class _Gather:
    def __init__(self, srcs, outs, send_sems, recv_sems, local_sems):
        self.srcs, self.outs = srcs, outs
        self.send_sems, self.recv_sems, self.local_sems = send_sems, recv_sems, local_sems
        x, y, c = _place()
        self.c = c
        self.me, self.sibling = (x, y, c), (x, y, 1 - c)
        self.chips = [(1 - x, y), (x, 1 - y), (1 - x, 1 - y)]

    def _rows(self, a, px, py, pc):
        m = self.srcs[a].shape[0]
        return self.outs[a].at[pl.ds((4 * px + 2 * py + pc) * m, m), :]

    def _copy(self, a, k, block, to, from_src=False):
        return pltpu.make_async_remote_copy(
            src_ref=self.srcs[a] if from_src else self._rows(a, *block), dst_ref=self._rows(a, *block),
            send_sem=self.send_sems.at[k, a], recv_sem=self.recv_sems.at[k, a], device_id=to, device_id_type=MESH)

    def _own(self, a):
        return pltpu.make_async_copy(self.srcs[a], self._rows(a, *self.me), self.local_sems.at[a])

    def start(self):
        for a in range(len(self.srcs)):
            self._own(a).start()
            self._copy(a, 0, self.me, self.sibling, from_src=True).start()
            for j, chip in enumerate(self.chips):
                self._copy(a, 1 + j, self.me, (*chip, self.c), from_src=True).start()

    def forward(self):
        for a in range(len(self.srcs)):
            for j, chip in enumerate(self.chips):
                self._copy(a, 1 + j, (*chip, self.c), self.me).wait_recv()
                self._copy(a, 4 + j, (*chip, self.c), self.sibling).start()

    def finish(self):
        for a in range(len(self.srcs)):
            self._copy(a, 0, self.sibling, self.me).wait_recv()
            for j, chip in enumerate(self.chips):
                self._copy(a, 4 + j, (*chip, 1 - self.c), self.me).wait_recv()
            self._copy(a, 0, self.me, self.sibling, from_src=True).wait_send()
            for j, chip in enumerate(self.chips):
                self._copy(a, 1 + j, self.me, (*chip, self.c), from_src=True).wait_send()
                self._copy(a, 4 + j, (*chip, self.c), self.sibling).wait_send()
            self._own(a).wait()


def _gather_scratch(n_arrays):
    return [pltpu.SemaphoreType.DMA((7, n_arrays)), pltpu.SemaphoreType.DMA((7, n_arrays)),
            pltpu.SemaphoreType.DMA((n_arrays,))]


def _allgather_halves(mine):
    m_per, n = mine.shape

    def body(x_ref, out_ref, send_sems, recv_sems, local_sems):
        gather = _Gather((x_ref,), (out_ref,), send_sems, recv_sems, local_sems)
        gather.start()
        gather.forward()
        gather.finish()

    return _pcall(
        body, name="allgather_w_in",
        out_shape=jax.ShapeDtypeStruct((8 * m_per, n), mine.dtype),
        in_specs=[ANY], out_specs=ANY, scratch_shapes=_gather_scratch(1),
    )(mine)


def _rs_pair_exchange(g, name):
    def body(g_ref, out_ref, send_sem, recv_sem):
        x, y, c = _place()
        cp = pltpu.make_async_remote_copy(
            src_ref=g_ref.at[1 - c], dst_ref=out_ref, send_sem=send_sem, recv_sem=recv_sem,
            device_id=(x, y, 1 - c), device_id_type=MESH)
        cp.start()
        cp.wait()

    return _pcall(
        body, name=name, out_shape=jax.ShapeDtypeStruct(g.shape[1:], F32),
        in_specs=[ANY], out_specs=ANY, scratch_shapes=[pltpu.SemaphoreType.DMA, pltpu.SemaphoreType.DMA],
    )(g)


def _rs_pair_add(g, got, c_idx, name):
    rows = g.shape[2]

    def body(c_ref, a_ref, b_ref, o_ref, ob_ref):
        pair = a_ref[...] + b_ref[...]
        o_ref[...] = pair
        ob_ref[...] = pair.astype(BF16)

    blk = pl.BlockSpec((None, rows, D_MODEL), lambda s, c_ref: (s, 0, 0))
    return _pcall(
        body, name=name,
        grid_spec=pltpu.PrefetchScalarGridSpec(
            num_scalar_prefetch=1, grid=(N_CHIPS,),
            in_specs=[pl.BlockSpec((None, None, rows, D_MODEL), lambda s, c_ref: (c_ref[0], s, 0, 0)), blk],
            out_specs=[blk, blk]),
        out_shape=[jax.ShapeDtypeStruct((N_CHIPS, rows, D_MODEL), F32),
                   jax.ShapeDtypeStruct((N_CHIPS, rows, D_MODEL), BF16)],
        compiler_params=_params(("parallel",)),
    )(c_idx, g, got)


def _chip_exchange_copies(p_ref, out_ref, send_sems, recv_sems):
    x, y, c = _place()
    chips = [(1 - x, y), (x, 1 - y), (1 - x, 1 - y)]
    return [pltpu.make_async_remote_copy(
        src_ref=p_ref.at[2 * cx + cy], dst_ref=out_ref.at[j], send_sem=send_sems.at[j], recv_sem=recv_sems.at[j],
        device_id=(cx, cy, c), device_id_type=MESH) for j, (cx, cy) in enumerate(chips)]


def _rs_chip_exchange(p4, name):
    def body(p_ref, out_ref, send_sems, recv_sems):
        cps = _chip_exchange_copies(p_ref, out_ref, send_sems, recv_sems)
        for cp in cps:
            cp.start()
        for cp in cps:
            cp.wait()

    return _pcall(
        body, name=name, out_shape=jax.ShapeDtypeStruct((3,) + p4.shape[1:], p4.dtype),
        in_specs=[ANY], out_specs=ANY,
        scratch_shapes=[pltpu.SemaphoreType.DMA((3,)), pltpu.SemaphoreType.DMA((3,))],
    )(p4)


def _rs_chip_add(p4, got, sc_idx, name):
    rows = p4.shape[1]
    tr = next(rows // n for n in (8, 7, 6, 5, 4, 3, 2, 1) if rows % n == 0 and (rows // n) % 16 == 0)

    def body(sc_ref, a_ref, b_ref, o_ref):
        o_ref[...] = ((a_ref[...] + b_ref[0].astype(F32)) + b_ref[1].astype(F32)) + b_ref[2].astype(F32)

    return _pcall(
        body, name=name,
        grid_spec=pltpu.PrefetchScalarGridSpec(
            num_scalar_prefetch=1, grid=(rows // tr,),
            in_specs=[pl.BlockSpec((None, tr, D_MODEL), lambda i, sc_ref: (sc_ref[0], i, 0)),
                      pl.BlockSpec((3, tr, D_MODEL), lambda i, sc_ref: (0, i, 0))],
            out_specs=pl.BlockSpec((None, tr, D_MODEL), lambda i, sc_ref: (sc_ref[1], i, 0))),
        out_shape=jax.ShapeDtypeStruct((2, rows, D_MODEL), F32),
        compiler_params=_params(("parallel",)),
    )(sc_idx, p4, got)


def _rs_pair_share(halves, name):
    def body(r_ref, out_ref, send_sem, recv_sem):
        x, y, c = _place()
        cp = pltpu.make_async_remote_copy(
            src_ref=r_ref.at[c], dst_ref=out_ref.at[c], send_sem=send_sem, recv_sem=recv_sem,
            device_id=(x, y, 1 - c), device_id_type=MESH)
        cp.start()
        cp.wait()

    return _pcall(
        body, name=name, out_shape=jax.ShapeDtypeStruct(halves.shape, F32),
        in_specs=[ANY], out_specs=ANY, input_output_aliases={0: 0},
        scratch_shapes=[pltpu.SemaphoreType.DMA, pltpu.SemaphoreType.DMA],
    )(halves)


def _adam(w, g, m, v):
    m2 = ADAM_B1 * m + (1.0 - ADAM_B1) * g
    v2 = ADAM_B2 * v + (1.0 - ADAM_B2) * (g * g)
    m_hat = m2 / (1.0 - ADAM_B1 ** ADAM_STEP)
    v_hat = v2 / (1.0 - ADAM_B2 ** ADAM_STEP)
    return -ADAM_LR * (m_hat / (jnp.sqrt(v_hat) + ADAM_EPS) + ADAM_WD * w), m2, v2


def _small_allreduce(part):
    def body(p_ref, g_ref, buf, send_sems, recv_sems):
        x, y, c = _place()
        me = 4 * x + 2 * y + c
        cps = []
        for k in range(1, 8):
            peer = (1 - x if k & 4 else x, 1 - y if k & 2 else y, 1 - c if k & 1 else c)
            cps.append(pltpu.make_async_remote_copy(
                src_ref=p_ref, dst_ref=buf.at[me], send_sem=send_sems.at[k - 1], recv_sem=recv_sems.at[k - 1],
                device_id=peer, device_id_type=MESH))
        for cp in cps:
            cp.start()
        buf[me] = p_ref[...]
        for cp in cps:
            cp.wait()
        g = buf[0]
        for k in range(1, 8):
            g = g + buf[k]
        g_ref[...] = g

    vm = pl.BlockSpec(memory_space=pltpu.VMEM)
    return _pcall(
        body, name="small_allreduce",
        out_shape=jax.ShapeDtypeStruct((SMALL_ROWS, LANES), F32), in_specs=[vm], out_specs=vm,
        scratch_shapes=[pltpu.VMEM((8, SMALL_ROWS, LANES), F32), pltpu.SemaphoreType.DMA((7,)),
                        pltpu.SemaphoreType.DMA((7,))],
    )(part)


def _small_adamw(g, w, m, v):
    pieces = ((0, 8, LANES), (20, 1, B_HEADS), (16, 1, HEAD_DIM), (17, 1, HEAD_DIM), (21, 1, A_HEADS),
              (18, 1, HEAD_DIM), (19, 1, HEAD_DIM), (8, 8, LANES))

    def body(g_ref, w_ref, m_ref, v_ref, *rest):
        outs, (loss_ref, stage) = rest[:4 * len(pieces)], rest[4 * len(pieces):]
        g = g_ref[...]
        loss_ref[...] = g_ref[ROW_LOSS:ROW_LOSS + 1, :]
        for kind, packed in enumerate((g,) + _adam(w_ref[...], g, m_ref[...], v_ref[...])):
            stage[...] = packed
            for i, (row, rows, lanes) in enumerate(pieces):
                outs[kind * len(pieces) + i][...] = stage[row:row + rows, 0:lanes]

    vm = pl.BlockSpec(memory_space=pltpu.VMEM)
    shapes = [jax.ShapeDtypeStruct((rows, lanes), F32) for _ in range(4) for _, rows, lanes in pieces]
    shapes.append(jax.ShapeDtypeStruct((1, LANES), F32))
    res = _pcall(
        body, name="small_adamw", out_shape=shapes, in_specs=[vm, vm, vm, vm], out_specs=[vm] * len(shapes),
        scratch_shapes=[pltpu.VMEM((SMALL_ROWS, LANES), F32)],
    )(g, w, m, v)
    flat = [r.reshape(r.size) for r in res[:-1]]
    n = len(pieces)
    return [flat[k * n:(k + 1) * n] for k in range(4)], res[-1][0, 0]


def _adamw(w, g, m, v, name):
    rows, cols = w.shape
    tr = min(256, rows)

    def body(w_ref, g_ref, m_ref, v_ref, d_ref, m2_ref, v2_ref):
        d_ref[...], m2_ref[...], v2_ref[...] = _adam(w_ref[...], g_ref[...], m_ref[...], v_ref[...])

    spec = _rows(tr, cols)
    shp = jax.ShapeDtypeStruct((rows, cols), F32)
    return _pcall(
        body, name=name, grid=(rows // tr,), in_specs=[spec] * 4, out_specs=[spec] * 3, out_shape=[shp] * 3,
        compiler_params=_params(("parallel",)),
    )(w, g, m, v)


def _pad_lanes(v):
    return jnp.pad(v, (0, LANES - v.shape[0]))


def _pad_head_rows(w_t, heads):
    n = w_t.shape[1]
    return jnp.pad(w_t.reshape(heads, HEAD_DIM, n), ((0, 0), (0, LANES - HEAD_DIM), (0, 0))).reshape(heads * LANES, n)


def _unpad_head_rows(w_t, heads):
    n = w_t.shape[1]
    return w_t.reshape(heads, LANES, n)[:, :HEAD_DIM].reshape(heads * HEAD_DIM, n)


def _in_rows_pad(w_in_t):
    qa, ka, va, qb, kb, vb, f = jnp.split(w_in_t, [512, 640, 768, 1280, 1792, 2304], axis=0)
    f = jnp.pad(f, ((0, 2 * LANES - B_HEADS), (0, 0)))
    return jnp.concatenate([_pad_head_rows(qa, 8), _pad_head_rows(ka, 2), _pad_head_rows(qb, 8),
                            _pad_head_rows(kb, 8), _pad_head_rows(va, 2), _pad_head_rows(vb, 8), f], axis=0)


def _in_rows_unpad(d):
    qa = _unpad_head_rows(d[G_QA * LANES:G_KA * LANES], 8)
    ka = _unpad_head_rows(d[G_KA * LANES:G_QB * LANES], 2)
    qb = _unpad_head_rows(d[G_QB * LANES:G_KB * LANES], 8)
    kb = _unpad_head_rows(d[G_KB * LANES:G_VA * LANES], 8)
    va = _unpad_head_rows(d[G_VA * LANES:G_VB * LANES], 2)
    vb = _unpad_head_rows(d[G_VB * LANES:G_F * LANES], 8)
    f = d[G_F * LANES:G_F * LANES + B_HEADS]
    return jnp.concatenate([qa, ka, va, qb, kb, vb, f], axis=0)


def _pack_small(g1, bf, qa, ka, sk, qb, kb, g2, loss_row):
    rows = [g1.reshape(8, LANES), g2.reshape(8, LANES)]
    rows += [_pad_lanes(t)[None] for t in (qa, ka, qb, kb, bf, sk)]
    rows += [loss_row, jnp.zeros((1, LANES), F32)]
    return jnp.concatenate(rows, axis=0)


def kernel(x, attn_norm_g, w_in, b_forget, q_norm_a, k_norm_a, sink_logits, q_norm_b, k_norm_b, w_out, mlp_norm_g, w_up, w_down, loss_target, m_attn_norm_g, m_w_in, m_b_forget, m_q_norm_a, m_k_norm_a, m_sink_logits, m_q_norm_b, m_k_norm_b, m_w_out, m_mlp_norm_g, m_w_up, m_w_down, v_attn_norm_g, v_w_in, v_b_forget, v_q_norm_a, v_k_norm_a, v_sink_logits, v_q_norm_b, v_k_norm_b, v_w_out, v_mlp_norm_g, v_w_up, v_w_down):
    nb, seq, _ = x.shape
    t_all = nb * seq
    c_idx = lax.axis_index("c")
    s_idx = 2 * lax.axis_index("x") + lax.axis_index("y")

    def my_half(a):
        halves = a.astype(BF16).reshape(2, a.shape[0] // 2, a.shape[1])
        return lax.dynamic_slice_in_dim(halves, c_idx, 1, axis=0)[0]

    w_in_shard_t = jnp.pad(w_in.T, ((0, IN_SHARD_P - IN_SHARD), (0, 0)))
    gathered_in = _allgather_halves(my_half(w_in_shard_t)).reshape(N_CHIPS, IN_SHARD_P, D_MODEL)
    w_pad_t = _in_rows_pad(gathered_in[:, :IN_SHARD].reshape(IN_WIDTH, D_MODEL))

    ones = jnp.ones((LANES,), F32)
    gain_row = jnp.concatenate(
        [jnp.tile(_pad_lanes(q_norm_a), 8), jnp.tile(_pad_lanes(k_norm_a), 2), jnp.tile(_pad_lanes(q_norm_b), 8),
         jnp.tile(_pad_lanes(k_norm_b), 8), jnp.tile(ones, N_GROUPS - N_NORM_GROUPS)])[None]
    b_row = _pad_lanes(b_forget)[None]
    g1 = attn_norm_g[None]
    g2 = mlp_norm_g[None]
    slopes = jnp.exp2(-(8.0 / A_HEADS) * (jnp.arange(A_HEADS, dtype=F32) + 1.0))

    x2 = x.reshape(t_all, D_MODEL)
    tgt = loss_target.reshape(t_all, D_MODEL)

    (xn, pre, qa, ka, va, qb, kb, vb, z), (w_out_g, w_up_g, w_down_f) = _inproj(
        x2, g1, w_pad_t, gain_row, b_row, seq, [my_half(w_out), my_half(w_up), my_half(w_down)])
    wo_pad = _pad_head_rows(w_out_g, A_HEADS + B_HEADS)
    w_up_blocks = w_up_g.reshape(N_CHIPS, D_MODEL, D_MODEL)
    swa_bias = _swa_bias(slopes)
    oa, la = _swa_fwd(qa, ka, va, sink_logits, swa_bias, nb, seq)
    ob, lse = _fox_fwd(qb, kb, vb, nb, seq)
    h, hn = _outproj(x2, oa, ob, wo_pad, g2)
    ru, dy, dyb, loss_acc = _mlp_fwd(hn, w_up_blocks, w_down_f, h, tgt)

    du, d_w_mlp = _mlp_bwd_w(dyb, w_down_f, ru, hn)
    c_arg = c_idx.reshape(1).astype(jnp.int32)
    sc_arg = jnp.stack([s_idx, c_idx]).astype(jnp.int32)
    dh, dhb, d_g2, sibling_w_mlp = _mlp_dhn(du, w_up_blocks, h, dy, g2, d_w_mlp)
    pair_m, pair_m_bf = _rs_pair_add(d_w_mlp, sibling_w_mlp, c_arg, "rs_pair_add_mlp")
    doa, dob, delta_b, d_wo = _dmixed(dhb, wo_pad, oa, ob)
    dqb, dkb, dvb, got_m = _fox_bwd(qb, kb, vb, dob, lse, delta_b, nb, seq, pair_m_bf)
    red_m = _rs_pair_share(_rs_chip_add(pair_m, got_m, sc_arg, "rs_chip_add_mlp"), "rs_pair_share_mlp")
    g_w_up, g_w_down = red_m[0], red_m[1]
    dqa, dka, dva, dsink = _swa_bwd(qa, ka, va, oa, doa, la, sink_logits, swa_bias, nb, seq)
    dproj, small, grad_x, d_g1 = _dproj_dx(pre, dqa, dka, dqb, dkb, dva, dvb, z, x2, dh, gain_row, w_pad_t, g1, seq)
    d_w_in_t = _dwin(dproj, xn)

    d_w_out = _unpad_head_rows(d_wo, A_HEADS + B_HEADS)
    g_att = jnp.concatenate([
        jnp.pad(_in_rows_unpad(d_w_in_t).reshape(N_CHIPS, IN_SHARD, D_MODEL),
                ((0, 0), (0, IN_SHARD_P - IN_SHARD), (0, 0))),
        d_w_out.reshape(N_CHIPS, D_MODEL // N_CHIPS, D_MODEL)], axis=1)
    g_att = jnp.stack([g_att[:, :R_ATT // 2], g_att[:, R_ATT // 2:]])
    pair_a, pair_a_bf = _rs_pair_add(g_att, _rs_pair_exchange(g_att, "rs_pair_exchange_att"), c_arg, "rs_pair_add_att")
    got_a = _rs_chip_exchange(pair_a_bf, "rs_chip_exchange_att")
    red_a = _rs_pair_share(_rs_chip_add(pair_a, got_a, sc_arg, "rs_chip_add_att"), "rs_pair_share_att")
    red_a = red_a.reshape(R_ATT, D_MODEL)
    g_w_in = red_a[:IN_SHARD].T
    g_w_out = red_a[IN_SHARD_P:]

    loss_row = loss_acc[0:1] * (0.5 / D_MODEL)
    d_sink = dsink[:, :A_GROUP, 0].reshape(nb, A_HEADS).sum(axis=0)
    part = _pack_small(d_g1[0], small[4, :B_HEADS], small[0, :HEAD_DIM], small[1, :HEAD_DIM], d_sink,
                       small[2, :HEAD_DIM], small[3, :HEAD_DIM], d_g2[0], loss_row)
    zero_row = jnp.zeros((1, LANES), F32)
    smalls = lambda t: _pack_small(*t, zero_row)
    w_small = smalls((attn_norm_g, b_forget, q_norm_a, k_norm_a, sink_logits, q_norm_b, k_norm_b, mlp_norm_g))
    m_small = smalls((m_attn_norm_g, m_b_forget, m_q_norm_a, m_k_norm_a, m_sink_logits, m_q_norm_b, m_k_norm_b,
                      m_mlp_norm_g))
    v_small = smalls((v_attn_norm_g, v_b_forget, v_q_norm_a, v_k_norm_a, v_sink_logits, v_q_norm_b, v_k_norm_b,
                      v_mlp_norm_g))
    (g_s, d_s, m_s, v_s), loss = _small_adamw(_small_allreduce(part), w_small, m_small, v_small)

    big = {}
    for name, w, g, m, v in (("adamw_w_in", w_in, g_w_in, m_w_in, v_w_in),
                             ("adamw_w_out", w_out, g_w_out, m_w_out, v_w_out),
                             ("adamw_w_up", w_up, g_w_up, m_w_up, v_w_up),
                             ("adamw_w_down", w_down, g_w_down, m_w_down, v_w_down)):
        big[name] = (g,) + tuple(_adamw(w, g, m, v, name))

    def assemble(k, s):
        return (s[0], big["adamw_w_in"][k], s[1], s[2], s[3], s[4], s[5], s[6], big["adamw_w_out"][k], s[7],
                big["adamw_w_up"][k], big["adamw_w_down"][k])

    return (loss, grad_x.reshape(nb, seq, D_MODEL), *assemble(0, g_s), *assemble(1, d_s), *assemble(2, m_s),
            *assemble(3, v_s))
```

```python
import functools

import numpy as np
import jax
import jax.numpy as jnp
from jax import lax
from jax.experimental import pallas as pl
from jax.experimental.pallas import tpu as pltpu

F32 = jnp.float32
BF16 = jnp.bfloat16

D_MODEL = 1024
HEAD_DIM = 64
LANES = 128
A_HEADS = 8
A_KV_HEADS = 2
A_GROUP = A_HEADS // A_KV_HEADS
B_HEADS = 8
WINDOW = 128
D_FF = 4096
IN_WIDTH = 2312
EPS = 1e-6
SCALE = 0.125
LOG2E = 1.4426950408889634
LN2 = 0.6931471805599453
CHUNK = 32
FOX_TK = 512
FOX_PARTS = 8
FOX_PARTS_BWD = 4
NEG = -1e30

G_QA, G_KA, G_QB, G_KB, G_VA, G_VB, G_F = 0, 8, 10, 18, 26, 28, 36
N_NORM_GROUPS = 26
N_GROUPS = 38
NP = N_GROUPS * LANES
MIXED_P = (A_HEADS + B_HEADS) * LANES

N_CHIPS = 4
IN_SHARD = IN_WIDTH // N_CHIPS
IN_SHARD_P = 608
R_ATT = IN_SHARD_P + D_MODEL // N_CHIPS

SMALL_ROWS = 24
ROW_LOSS = 22

ADAM_LR = 0.001
ADAM_B1 = 0.9
ADAM_B2 = 0.999
ADAM_EPS = 1e-08
ADAM_WD = 0.01
ADAM_STEP = 10

VMEM_LIMIT = 52 * 1024 * 1024
MESH = pl.DeviceIdType.MESH


def _pcall(body, **kw):
    return pl.pallas_call(body, **kw)


def _params(sem=None):
    return pltpu.CompilerParams(dimension_semantics=sem, vmem_limit_bytes=VMEM_LIMIT)


def _dot(a, b):
    return jnp.dot(a, b, preferred_element_type=F32)


def _dot_nt(a, b):
    return lax.dot_general(a, b, (((1,), (1,)), ((), ())), preferred_element_type=F32)


def _dot_tn(a, b):
    return lax.dot_general(a, b, (((0,), (0,)), ((), ())), preferred_element_type=F32)


def _split3(x):
    hi = x.astype(BF16)
    r1 = x - hi.astype(F32)
    mid = r1.astype(BF16)
    lo = (r1 - mid.astype(F32)).astype(BF16)
    return hi, mid, lo


def _dot_exact(mat, x):
    hi, mid, lo = _split3(x)
    return _dot(mat, lo) + _dot(mat, mid) + _dot(mat, hi)


def _const(shape):
    zeros = (0,) * len(shape)
    return pl.BlockSpec(shape, lambda *_: zeros)


def _resident(shape):
    zeros = (0,) * len(shape)
    return pl.BlockSpec(shape, lambda *_: zeros, pipeline_mode=pl.Buffered(1))


def _rows(tm, n):
    return pl.BlockSpec((tm, n), lambda i: (i, 0))


def _aug_select():
    e = np.zeros((3 * LANES, 2 * B_HEADS * LANES), np.float32)
    for j in range(3):
        for h in range(B_HEADS):
            e[j * LANES + h, h * LANES + HEAD_DIM + j] = 1.0
            e[j * LANES + h, (B_HEADS + h) * LANES + HEAD_DIM + 3 + j] = -1.0
    return jnp.asarray(e, BF16)


def _dc_select():
    e = np.zeros((2 * B_HEADS * LANES, LANES), np.float32)
    for h in range(B_HEADS):
        e[h * LANES + HEAD_DIM, h] = 1.0
        e[(B_HEADS + h) * LANES + HEAD_DIM + 3, h] = -1.0
    return jnp.asarray(e, BF16)


def _tri(n, upper):
    t = np.tril(np.ones((n, n), np.float32))
    return jnp.asarray(t.T if upper else t, BF16)


def _inproj(x2, g1, w_pad_t, gain_row, b_row, seq, later_weights):
    t_all = x2.shape[0]
    tm = min(256, seq)
    n_steps = t_all // tm
    forward_step = max(n_steps - 2, 0)
    tiles_per_seq = seq // tm
    tri = _tri(tm, False)
    esel = _aug_select()
    n_later = len(later_weights)

    def body(x_ref, g_ref, w_ref, gain_ref, b_ref, tri_ref, e_ref, *rest):
        later_src, rest = rest[:n_later], rest[n_later:]
        xn_ref, pre_ref, qa_ref, ka_ref, va_ref, qb_ref, kb_ref, vb_ref, z_ref = rest[:9]
        later_out, (carry_ref, send_sems, recv_sems, local_sems) = rest[9:9 + n_later], rest[9 + n_later:]
        i = pl.program_id(0)
        gather = _Gather(later_src, later_out, send_sems, recv_sems, local_sems)

        @pl.when(i == 0)
        def _():
            gather.start()

        @pl.when(i == forward_step)
        def _():
            gather.forward()

        @pl.when(i % tiles_per_seq == 0)
        def _():
            carry_ref[...] = jnp.zeros_like(carry_ref)

        x = x_ref[...]
        r = lax.rsqrt(jnp.mean(x * x, axis=-1, keepdims=True) + EPS)
        xn = (x * r * g_ref[...]).astype(BF16)
        xn_ref[...] = xn
        proj = _dot_nt(xn, w_ref[...])
        pre_ref[...] = proj[:, :N_NORM_GROUPS * LANES].astype(BF16)
        lane = lax.broadcasted_iota(jnp.int32, (tm, LANES), 1)

        z = proj[:, G_F * LANES:(G_F + 1) * LANES] + b_ref[...]
        z_ref[...] = z
        lf = jnp.minimum(z, 0.0) - jnp.log(1.0 + jnp.exp(-jnp.abs(z)))
        lf = jnp.where(lane < B_HEADS, lf, 0.0)
        c = _dot_exact(tri_ref[...], lf) + carry_ref[...]
        carry_ref[...] += jnp.sum(lf, axis=0, keepdims=True)
        aug = _dot(jnp.concatenate(_split3(c * LOG2E), axis=1), e_ref[...])

        def hnorm(g):
            p = proj[:, g * LANES:(g + 1) * LANES]
            rr = lax.rsqrt(jnp.sum(p * p, axis=-1, keepdims=True) * (1.0 / HEAD_DIM) + EPS)
            return p * rr * gain_ref[:, g * LANES:(g + 1) * LANES]

        ones_q = jnp.where((lane >= HEAD_DIM + 3) & (lane < HEAD_DIM + 6), 1.0, 0.0)
        ones_k = jnp.where((lane >= HEAD_DIM) & (lane < HEAD_DIM + 3), 1.0, 0.0)
        for h in range(A_HEADS):
            qa_ref[:, h * LANES:(h + 1) * LANES] = (hnorm(G_QA + h) * SCALE).astype(BF16)
        for h in range(A_KV_HEADS):
            ka_ref[:, h * LANES:(h + 1) * LANES] = hnorm(G_KA + h).astype(BF16)
        for h in range(B_HEADS):
            qb_ref[:, h * LANES:(h + 1) * LANES] = (
                hnorm(G_QB + h) * (SCALE * LOG2E) + aug[:, h * LANES:(h + 1) * LANES] + ones_q).astype(BF16)
            kb_ref[:, h * LANES:(h + 1) * LANES] = (
                hnorm(G_KB + h) + aug[:, (B_HEADS + h) * LANES:(B_HEADS + h + 1) * LANES] + ones_k).astype(BF16)
        va_ref[...] = proj[:, G_VA * LANES:G_VB * LANES].astype(BF16)
        one_v = jnp.where(lane == HEAD_DIM, 1.0, 0.0)
        for h in range(B_HEADS):
            cols = slice((G_VB + h) * LANES, (G_VB + h + 1) * LANES)
            vb_ref[:, h * LANES:(h + 1) * LANES] = (proj[:, cols] + one_v).astype(BF16)

        @pl.when(i == n_steps - 1)
        def _():
            gather.finish()

    widths = [(D_MODEL, BF16), (N_NORM_GROUPS * LANES, BF16), (A_HEADS * LANES, BF16), (A_KV_HEADS * LANES, BF16),
              (A_KV_HEADS * LANES, BF16), (B_HEADS * LANES, BF16), (B_HEADS * LANES, BF16), (B_HEADS * LANES, BF16),
              (LANES, F32)]
    res = _pcall(
        body, name="inproj", grid=(n_steps,),
        in_specs=[_rows(tm, D_MODEL), _const((1, D_MODEL)), _const((NP, D_MODEL)), _const((1, NP)),
                  _const((1, LANES)), _const((tm, tm)), _const(esel.shape)] + [ANY] * n_later,
        out_specs=[_rows(tm, w) for w, _ in widths] + [ANY] * n_later,
        out_shape=[jax.ShapeDtypeStruct((t_all, w), dt) for w, dt in widths]
        + [jax.ShapeDtypeStruct((8 * w.shape[0], w.shape[1]), w.dtype) for w in later_weights],
        scratch_shapes=[pltpu.VMEM((1, LANES), F32)] + _gather_scratch(n_later),
        compiler_params=_params(("arbitrary",)),
    )(x2, g1, w_pad_t, gain_row, b_row, tri, esel, *later_weights)
    return res[:9], res[9:]


def _fox_fwd(qb, kb, vb, nb, seq):
    t_all = qb.shape[0]
    tk = min(FOX_TK, seq // FOX_PARTS)
    tq = FOX_PARTS * tk
    nq = seq // tq

    def body(q_ref, k_ref, v_ref, o_ref, lse_ref, s_ref, p_ref, m_ref, alpha_ref, acc_ref):
        qi = pl.program_id(2)
        q = q_ref[...]
        m_ref[...] = jnp.full((tq, LANES), NEG, F32)
        acc_ref[...] = jnp.zeros((tq, LANES), F32)

        def step(j, modes):
            off = pl.multiple_of(j * tk, tk)
            k = k_ref[pl.ds(off, tk), :]
            v = v_ref[pl.ds(off, tk), :]
            live = [hf for hf in range(FOX_PARTS) if modes[hf] is not None]
            for hf in live:
                s_ref[hf] = _dot_nt(q[hf * tk:(hf + 1) * tk], k)
            for hf in live:
                for r in range(0, tk, CHUNK):
                    rows = slice(r, r + CHUNK)
                    grows = slice(hf * tk + r, hf * tk + r + CHUNK)
                    tiles = []
                    for jt in range(tk // LANES):
                        sc = s_ref[hf, rows, jt * LANES:(jt + 1) * LANES]
                        if modes[hf] == "diag":
                            row = r + lax.broadcasted_iota(jnp.int32, (CHUNK, LANES), 0)
                            col = jt * LANES + lax.broadcasted_iota(jnp.int32, (CHUNK, LANES), 1)
                            sc = jnp.where(row >= col, sc, NEG)
                        tiles.append(sc)
                    m_prev = m_ref[grows, :]
                    m_cur = functools.reduce(jnp.maximum, tiles)
                    m_new = jnp.maximum(m_prev, jnp.max(m_cur, axis=-1, keepdims=True))
                    m_ref[grows, :] = m_new
                    alpha_ref[grows, :] = jnp.exp2(m_prev - m_new)
                    for jt, sc in enumerate(tiles):
                        p_ref[hf, rows, jt * LANES:(jt + 1) * LANES] = jnp.exp2(sc - m_new).astype(BF16)
                hrows = slice(hf * tk, (hf + 1) * tk)
                acc_ref[hrows, :] = alpha_ref[hrows, :] * acc_ref[hrows, :] + _dot(p_ref[hf], v)

        def past(j, carry):
            step(j, ("full",) * FOX_PARTS)
            return carry

        lax.fori_loop(0, FOX_PARTS * qi, past, 0)
        for d in range(FOX_PARTS):
            step(FOX_PARTS * qi + d, (None,) * d + ("diag",) + ("full",) * (FOX_PARTS - 1 - d))
        acc = acc_ref[...]
        lane = lax.broadcasted_iota(jnp.int32, (tq, LANES), 1)
        l = jnp.sum(jnp.where(lane == HEAD_DIM, acc, 0.0), axis=-1, keepdims=True)
        o_ref[...] = (acc / l).astype(BF16)
        lse_ref[...] = m_ref[...] + jnp.log2(l)

    qspec = pl.BlockSpec((tq, LANES), lambda b, h, i: (b * nq + i, h))
    kspec = pl.BlockSpec((seq, LANES), lambda b, h, i: (b, h))
    return _pcall(
        body, name="fox_fwd", grid=(nb, B_HEADS, nq),
        in_specs=[qspec, kspec, kspec], out_specs=[qspec, qspec],
        out_shape=[jax.ShapeDtypeStruct((t_all, B_HEADS * LANES), BF16),
                   jax.ShapeDtypeStruct((t_all, B_HEADS * LANES), F32)],
        scratch_shapes=[pltpu.VMEM((FOX_PARTS, tk, tk), F32), pltpu.VMEM((FOX_PARTS, tk, tk), BF16),
                        pltpu.VMEM((tq, LANES), F32),
                        pltpu.VMEM((tq, LANES), F32), pltpu.VMEM((tq, LANES), F32)],
        compiler_params=_params(("parallel", "parallel", "arbitrary")),
    )(qb, kb, vb)


def _swa_bias(slopes):
    row = jnp.arange(A_GROUP * WINDOW, dtype=jnp.int32)[:, None] % WINDOW
    col = jnp.arange(2 * WINDOW, dtype=jnp.int32)[None, :]
    slope_rows = jnp.repeat(slopes.reshape(A_KV_HEADS, A_GROUP), WINDOW, axis=1)[:, :, None]
    out = []
    for t_rel in (0, WINDOW):
        dist = t_rel + row - col
        valid = (dist >= 0) & (dist < WINDOW)
        out.append(jnp.where(valid[None], -slope_rows * dist.astype(F32)[None], NEG))
    return jnp.stack(out)


def _stack_heads(ref, rows):
    return jnp.concatenate([ref[rows, j * LANES:(j + 1) * LANES] for j in range(A_GROUP)], axis=0)


def _sink_rows(sink_ref, g):
    return jnp.concatenate([jnp.full((WINDOW, LANES), sink_ref[g * A_GROUP + j], F32) for j in range(A_GROUP)], axis=0)


def _rep(col):
    return jnp.broadcast_to(col, (col.shape[0], LANES))


def _swa_specs(nq, tq, seq):
    smem = pl.BlockSpec(memory_space=pltpu.SMEM)
    qspec = pl.BlockSpec((tq, A_GROUP * LANES), lambda b, g, i: (b * nq + i, g))
    kspec = pl.BlockSpec((seq, LANES), lambda b, g, i: (b, g))
    bias_first = pl.BlockSpec((None, None, A_GROUP * WINDOW, 2 * WINDOW),
                              lambda b, g, i: (jnp.minimum(i, 1), g, 0, 0))
    bias_rest = pl.BlockSpec((None, None, A_GROUP * WINDOW, 2 * WINDOW), lambda b, g, i: (1, g, 0, 0))
    return smem, qspec, kspec, bias_first, bias_rest


def _swa_fwd(qa, ka, va, sinks, bias, nb, seq):
    t_all = qa.shape[0]
    tq = min(1024, seq)
    nq = seq // tq

    def body(sink_ref, q_ref, k_ref, v_ref, bias0_ref, bias_ref, o_ref, l_ref):
        qi = pl.program_id(2)
        sink = _sink_rows(sink_ref, pl.program_id(1))
        for a in range(tq // WINDOW):
            t0 = qi * tq + a * WINDOW
            start = pl.multiple_of(jnp.maximum(t0 - WINDOW, 0), WINDOW)
            rows = slice(a * WINDOW, (a + 1) * WINDOW)
            k = k_ref[pl.ds(start, 2 * WINDOW), :]
            v = v_ref[pl.ds(start, 2 * WINDOW), :]
            s = _dot_nt(_stack_heads(q_ref, rows), k) + (bias0_ref if a == 0 else bias_ref)[...]
            s0, s1 = s[:, :LANES], s[:, LANES:]
            m = jnp.maximum(_rep(jnp.max(jnp.maximum(s0, s1), axis=-1, keepdims=True)), sink)
            p0, p1 = jnp.exp(s0 - m), jnp.exp(s1 - m)
            den = _rep(jnp.sum(p0 + p1, axis=-1, keepdims=True)) + jnp.exp(sink - m)
            inv = 1.0 / den
            o = _dot(jnp.concatenate([(p0 * inv).astype(BF16), (p1 * inv).astype(BF16)], axis=1), v).astype(BF16)
            lrow = m + jnp.log(den)
            for j in range(A_GROUP):
                o_ref[rows, j * LANES:(j + 1) * LANES] = o[j * WINDOW:(j + 1) * WINDOW]
                l_ref[rows, j * LANES:(j + 1) * LANES] = lrow[j * WINDOW:(j + 1) * WINDOW]

    smem, qspec, kspec, bias_first, bias_rest = _swa_specs(nq, tq, seq)
    return _pcall(
        body, name="swa_fwd", grid=(nb, A_KV_HEADS, nq),
        in_specs=[smem, qspec, kspec, kspec, bias_first, bias_rest], out_specs=[qspec, qspec],
        out_shape=[jax.ShapeDtypeStruct((t_all, A_HEADS * LANES), BF16),
                   jax.ShapeDtypeStruct((t_all, A_HEADS * LANES), F32)],
        compiler_params=_params(("parallel", "parallel", "arbitrary")),
    )(sinks, qa, ka, va, bias, bias)


def _outproj(x2, oa, ob, wo_pad, g2):
    t_all = x2.shape[0]
    tm = min(512, t_all)
    half = A_HEADS * LANES

    def body(x_ref, oa_ref, ob_ref, w_ref, g_ref, h_ref, hn_ref):
        h = x_ref[...] + _dot(oa_ref[...], w_ref[:half, :]) + _dot(ob_ref[...], w_ref[half:, :])
        h_ref[...] = h
        r = lax.rsqrt(jnp.mean(h * h, axis=-1, keepdims=True) + EPS)
        hn_ref[...] = (h * r * g_ref[...]).astype(BF16)

    return _pcall(
        body, name="outproj", grid=(t_all // tm,),
        in_specs=[_rows(tm, D_MODEL), _rows(tm, half), _rows(tm, half), _const((MIXED_P, D_MODEL)),
                  _const((1, D_MODEL))],
        out_specs=[_rows(tm, D_MODEL), _rows(tm, D_MODEL)],
        out_shape=[jax.ShapeDtypeStruct((t_all, D_MODEL), F32), jax.ShapeDtypeStruct((t_all, D_MODEL), BF16)],
        compiler_params=_params(("parallel",)),
    )(x2, oa, ob, wo_pad, g2)


def _mlp_fwd(hn, w_up_blocks, w_down, h, tgt):
    t_all = h.shape[0]
    tm = min(512, t_all)
    nj = D_FF // D_MODEL

    def body(a_ref, wu_ref, wd_ref, h_ref, t_ref, ru_ref, dy_ref, dyb_ref, loss_ref):
        @pl.when(pl.program_id(0) == 0)
        def _():
            loss_ref[...] = jnp.zeros_like(loss_ref)

        a = a_ref[...]
        y = h_ref[...]
        for j in range(nj):
            cols = slice(j * D_MODEL, (j + 1) * D_MODEL)
            ru = jnp.maximum(_dot(a, wu_ref[j]), 0.0)
            ru_ref[:, cols] = ru.astype(BF16)
            y = y + _dot((ru * ru).astype(BF16), wd_ref[cols, :])
        err = y - t_ref[...]
        loss_ref[...] += jnp.sum(err * err)
        dy = err * (1.0 / D_MODEL)
        dy_ref[...] = dy
        dyb_ref[...] = dy.astype(BF16)

    return _pcall(
        body, name="mlp_fwd", grid=(t_all // tm,),
        in_specs=[_rows(tm, D_MODEL), _resident((nj, D_MODEL, D_MODEL)), _resident((D_FF, D_MODEL)), _rows(tm, D_MODEL),
                  _rows(tm, D_MODEL)],
        out_specs=[_rows(tm, D_FF), _rows(tm, D_MODEL), _rows(tm, D_MODEL), _const((8, LANES))],
        out_shape=[jax.ShapeDtypeStruct((t_all, D_FF), BF16), jax.ShapeDtypeStruct((t_all, D_MODEL), F32),
                   jax.ShapeDtypeStruct((t_all, D_MODEL), BF16), jax.ShapeDtypeStruct((8, LANES), F32)],
        compiler_params=_params(("arbitrary",)),
    )(hn, w_up_blocks, w_down, h, tgt)


def _mlp_bwd_w(dyb, w_down, ru, hn):
    t_all = dyb.shape[0]
    tm = min(512, t_all)
    nj = D_FF // D_MODEL

    def body(dy_ref, w_ref, ru_ref, hn_ref, du_ref, dw_ref):
        @pl.when(pl.program_id(1) == 0)
        def _():
            dw_ref[...] = jnp.zeros_like(dw_ref)

        dy = dy_ref[...]
        ru = ru_ref[...].astype(F32)
        du = (_dot_nt(dy, w_ref[...]) * (2.0 * ru)).astype(BF16)
        du_ref[...] = du
        dw_ref[0] += _dot_tn(hn_ref[...], du)
        dw_ref[1] += _dot_tn((ru * ru).astype(BF16), dy)

    tok = pl.BlockSpec((tm, D_MODEL), lambda j, i: (i, 0))
    blk = pl.BlockSpec((tm, D_MODEL), lambda j, i: (i, j))
    wspec = pl.BlockSpec((2, None, D_MODEL, D_MODEL), lambda j, i: (0, j, 0, 0))
    return _pcall(
        body, name="mlp_bwd_w", grid=(nj, t_all // tm),
        in_specs=[tok, pl.BlockSpec((D_MODEL, D_MODEL), lambda j, i: (j, 0)), blk, tok],
        out_specs=[blk, wspec],
        out_shape=[jax.ShapeDtypeStruct((t_all, D_FF), BF16), jax.ShapeDtypeStruct((2, nj, D_MODEL, D_MODEL), F32)],
        compiler_params=_params(("parallel", "arbitrary")),
    )(dyb, w_down, ru, hn)


def _pair_exchange_copy(g_ref, out_ref, send_sem, recv_sem):
    x, y, c = _place()
    return pltpu.make_async_remote_copy(
        src_ref=g_ref.at[1 - c], dst_ref=out_ref, send_sem=send_sem, recv_sem=recv_sem,
        device_id=(x, y, 1 - c), device_id_type=MESH)


def _mlp_dhn(du, w_up_blocks, h, dy, g2, d_w_mlp):
    t_all = h.shape[0]
    tm = min(512, t_all)
    n_steps = t_all // tm

    def body(a_ref, w_ref, h_ref, dy_ref, g_ref, dw_ref, dh_ref, dhb_ref, dg_ref, got_ref, send_sem, recv_sem):
        @pl.when(pl.program_id(0) == 0)
        def _():
            dg_ref[...] = jnp.zeros_like(dg_ref)
            _pair_exchange_copy(dw_ref, got_ref, send_sem, recv_sem).start()

        dhn = _dot_nt(a_ref[:, :D_MODEL], w_ref[0])
        for j in range(1, D_FF // D_MODEL):
            dhn = dhn + _dot_nt(a_ref[:, j * D_MODEL:(j + 1) * D_MODEL], w_ref[j])
        h = h_ref[...]
        r = lax.rsqrt(jnp.mean(h * h, axis=-1, keepdims=True) + EPS)
        hh = h * r
        dg_ref[...] += jnp.sum(dhn * hh, axis=0, keepdims=True)
        dz = dhn * g_ref[...]
        dh = dy_ref[...] + r * (dz - hh * jnp.mean(dz * hh, axis=-1, keepdims=True))
        dh_ref[...] = dh
        dhb_ref[...] = dh.astype(BF16)

        @pl.when(pl.program_id(0) == n_steps - 1)
        def _():
            _pair_exchange_copy(dw_ref, got_ref, send_sem, recv_sem).wait()

    return _pcall(
        body, name="mlp_dhn", grid=(n_steps,),
        in_specs=[_rows(tm, D_FF), _resident((D_FF // D_MODEL, D_MODEL, D_MODEL)), _rows(tm, D_MODEL),
                  _rows(tm, D_MODEL), _const((1, D_MODEL)), ANY],
        out_specs=[_rows(tm, D_MODEL), _rows(tm, D_MODEL), _const((1, D_MODEL)), ANY],
        out_shape=[jax.ShapeDtypeStruct((t_all, D_MODEL), F32), jax.ShapeDtypeStruct((t_all, D_MODEL), BF16),
                   jax.ShapeDtypeStruct((1, D_MODEL), F32), jax.ShapeDtypeStruct(d_w_mlp.shape[1:], F32)],
        scratch_shapes=[pltpu.SemaphoreType.DMA, pltpu.SemaphoreType.DMA],
        compiler_params=_params(("arbitrary",)),
    )(du, w_up_blocks, h, dy, g2, d_w_mlp)


def _dmixed(dhb, wo_pad, oa, ob):
    t_all = dhb.shape[0]
    tm = min(512, t_all)
    half = A_HEADS * LANES

    def body(a_ref, w_ref, oa_ref, ob_ref, da_ref, db_ref, delta_ref, dwo_ref):
        @pl.when(pl.program_id(0) == 0)
        def _():
            dwo_ref[...] = jnp.zeros_like(dwo_ref)

        a = a_ref[...]
        d = _dot_nt(a, w_ref[...])
        da_ref[...] = d[:, :half].astype(BF16)
        db_ref[...] = d[:, half:].astype(BF16)
        for h in range(B_HEADS):
            cols = slice(h * LANES, (h + 1) * LANES)
            prod = d[:, half + h * LANES:half + (h + 1) * LANES] * ob_ref[:, cols].astype(F32)
            delta_ref[:, cols] = jnp.broadcast_to(jnp.sum(prod, axis=-1, keepdims=True), (tm, LANES))
        dwo_ref[:half, :] += _dot_tn(oa_ref[...], a)
        dwo_ref[half:, :] += _dot_tn(ob_ref[...], a)

    return _pcall(
        body, name="dmixed", grid=(t_all // tm,),
        in_specs=[_rows(tm, D_MODEL), _const((MIXED_P, D_MODEL)), _rows(tm, half), _rows(tm, half)],
        out_specs=[_rows(tm, half), _rows(tm, half), _rows(tm, half), _const((MIXED_P, D_MODEL))],
        out_shape=[jax.ShapeDtypeStruct((t_all, half), BF16), jax.ShapeDtypeStruct((t_all, half), BF16),
                   jax.ShapeDtypeStruct((t_all, half), F32), jax.ShapeDtypeStruct((MIXED_P, D_MODEL), F32)],
        compiler_params=_params(("arbitrary",)),
    )(dhb, wo_pad, oa, ob)


def _fox_bwd(qb, kb, vb, dob, lse, delta, nb, seq, pair_sums):
    t_all = qb.shape[0]
    tk = min(FOX_TK, seq // FOX_PARTS_BWD)
    tq = FOX_PARTS_BWD * tk
    nk = seq // tk

    def body(q_ref, k_ref, v_ref, do_ref, lse_ref, delta_ref, pair_ref, dq_ref, dk_ref, dv_ref, got_ref,
             s_ref, dp_ref, p_ref, ds_ref, dk_acc, dv_acc, send_sems, recv_sems):
        kj = pl.program_id(2)
        bh = pl.program_id(0) * B_HEADS + pl.program_id(1)

        @pl.when((bh == 0) & (kj == 0))
        def _():
            for cp in _chip_exchange_copies(pair_ref, got_ref, send_sems, recv_sems):
                cp.start()

        @pl.when(kj == 0)
        def _():
            dq_ref[...] = jnp.zeros_like(dq_ref)

        dk_acc[...] = jnp.zeros_like(dk_acc)
        dv_acc[...] = jnp.zeros_like(dv_acc)
        k = k_ref[...]
        v = v_ref[...]

        def block(off, r0, r1, masked):
            qrows = pl.ds(pl.multiple_of(off + r0, CHUNK), r1 - r0)
            q = q_ref[qrows, :]
            do = do_ref[qrows, :]
            s_ref[r0:r1, :] = _dot_nt(q, k)
            dp_ref[r0:r1, :] = _dot_nt(do, v)
            for r in range(r0, r1, CHUNK):
                rows = slice(r, r + CHUNK)
                chunk = pl.ds(pl.multiple_of(off + r, CHUNK), CHUNK)
                lse_c = lse_ref[chunk, :]
                delta_c = delta_ref[chunk, :]
                for jt in range(tk // LANES):
                    cols = slice(jt * LANES, (jt + 1) * LANES)
                    p = jnp.exp2(s_ref[rows, cols] - lse_c)
                    if masked:
                        row = r - r0 + lax.broadcasted_iota(jnp.int32, (CHUNK, LANES), 0)
                        col = jt * LANES + lax.broadcasted_iota(jnp.int32, (CHUNK, LANES), 1)
                        p = jnp.where(row >= col, p, 0.0)
                    p_ref[rows, cols] = p.astype(BF16)
                    ds_ref[rows, cols] = (p * (dp_ref[rows, cols] - delta_c)).astype(BF16)
            dv_acc[...] += _dot_tn(p_ref[r0:r1, :], do)
            dk_acc[...] += _dot_tn(ds_ref[r0:r1, :], q)
            dq_ref[qrows, :] += _dot(ds_ref[r0:r1, :], k)

        first = kj // FOX_PARTS_BWD
        off_first = pl.multiple_of(first * tq, tq)
        for d in range(FOX_PARTS_BWD):
            @pl.when(kj % FOX_PARTS_BWD == d)
            def _(d=d):
                block(off_first, d * tk, (d + 1) * tk, True)
                if d < FOX_PARTS_BWD - 1:
                    block(off_first, (d + 1) * tk, tq, False)

        def later(i, carry):
            block(pl.multiple_of(i * tq, tq), 0, tq, False)
            return carry

        lax.fori_loop(first + 1, seq // tq, later, 0)
        dk_ref[...] = dk_acc[...]
        dv_ref[...] = dv_acc[...]

        @pl.when((bh == nb * B_HEADS - 1) & (kj == nk - 1))
        def _():
            for cp in _chip_exchange_copies(pair_ref, got_ref, send_sems, recv_sems):
                cp.wait()

    full = pl.BlockSpec((seq, LANES), lambda b, h, j: (b, h))
    tile = pl.BlockSpec((tk, LANES), lambda b, h, j: (b * nk + j, h))
    shp = jax.ShapeDtypeStruct((t_all, B_HEADS * LANES), F32)
    return _pcall(
        body, name="fox_bwd", grid=(nb, B_HEADS, nk),
        in_specs=[full, tile, tile, full, full, full, ANY], out_specs=[full, tile, tile, ANY],
        out_shape=[shp, shp, shp, jax.ShapeDtypeStruct((3,) + pair_sums.shape[1:], pair_sums.dtype)],
        scratch_shapes=[pltpu.VMEM((tq, tk), F32), pltpu.VMEM((tq, tk), F32), pltpu.VMEM((tq, tk), BF16),
                        pltpu.VMEM((tq, tk), BF16), pltpu.VMEM((tk, LANES), F32), pltpu.VMEM((tk, LANES), F32),
                        pltpu.SemaphoreType.DMA((3,)), pltpu.SemaphoreType.DMA((3,))],
        compiler_params=_params(("arbitrary", "arbitrary", "arbitrary")),
    )(qb, kb, vb, dob, lse, delta, pair_sums)


def _swa_bwd(qa, ka, va, oa, doa, lrow, sinks, bias, nb, seq):
    t_all = qa.shape[0]
    tq = min(1024, seq)
    nq = seq // tq

    def body(sink_ref, q_ref, k_ref, v_ref, bias0_ref, bias_ref, o_ref, do_ref, l_ref,
             dq_ref, dk_ref, dv_ref, dsink_ref):
        qi = pl.program_id(2)
        sink = _sink_rows(sink_ref, pl.program_id(1))

        @pl.when(qi == 0)
        def _():
            dk_ref[...] = jnp.zeros_like(dk_ref)
            dv_ref[...] = jnp.zeros_like(dv_ref)
            dsink_ref[...] = jnp.zeros_like(dsink_ref)

        for a in range(tq // WINDOW):
            t0 = qi * tq + a * WINDOW
            start = pl.multiple_of(jnp.maximum(t0 - WINDOW, 0), WINDOW)
            rows = slice(a * WINDOW, (a + 1) * WINDOW)
            win = pl.ds(start, 2 * WINDOW)
            q = _stack_heads(q_ref, rows)
            k = k_ref[win, :]
            v = v_ref[win, :]
            do = _stack_heads(do_ref, rows)
            lrow = _stack_heads(l_ref, rows)
            s = _dot_nt(q, k) + (bias0_ref if a == 0 else bias_ref)[...]
            dp = _dot_nt(do, v)
            delta = _rep(jnp.sum(do.astype(F32) * _stack_heads(o_ref, rows).astype(F32), axis=-1, keepdims=True))
            p = [jnp.exp(s[:, t * LANES:(t + 1) * LANES] - lrow) for t in range(2)]
            ds = jnp.concatenate([(p[t] * (dp[:, t * LANES:(t + 1) * LANES] - delta)).astype(BF16) for t in range(2)],
                                 axis=1)
            dq = _dot(ds, k)
            dk_ref[win, :] += _dot_tn(ds, q)
            dv_ref[win, :] += _dot_tn(jnp.concatenate([p[0].astype(BF16), p[1].astype(BF16)], axis=1), do)
            sink_term = jnp.exp(sink - lrow) * delta
            for j in range(A_GROUP):
                part = slice(j * WINDOW, (j + 1) * WINDOW)
                dq_ref[rows, j * LANES:(j + 1) * LANES] = dq[part]
                dsink_ref[j:j + 1, :] -= jnp.sum(sink_term[part], axis=0, keepdims=True)

    smem, qspec, kspec, bias_first, bias_rest = _swa_specs(nq, tq, seq)
    return _pcall(
        body, name="swa_bwd", grid=(nb, A_KV_HEADS, nq),
        in_specs=[smem, qspec, kspec, kspec, bias_first, bias_rest, qspec, qspec, qspec],
        out_specs=[qspec, kspec, kspec, pl.BlockSpec((None, 8, LANES), lambda b, g, i: (b * A_KV_HEADS + g, 0, 0))],
        out_shape=[jax.ShapeDtypeStruct((t_all, A_HEADS * LANES), F32),
                   jax.ShapeDtypeStruct((t_all, A_KV_HEADS * LANES), F32),
                   jax.ShapeDtypeStruct((t_all, A_KV_HEADS * LANES), F32),
                   jax.ShapeDtypeStruct((nb * A_KV_HEADS, 8, LANES), F32)],
        compiler_params=_params(("parallel", "parallel", "arbitrary")),
    )(sinks, qa, ka, va, bias, bias, oa, doa, lrow)


def _dproj_dx(pre, dqa, dka, dqb, dkb, dva, dvb, z, x2, dh, gain_row, w_pad_t, g1, seq):
    t_all = pre.shape[0]
    tm = min(256, seq)
    nt = t_all // tm
    tiles_per_seq = seq // tm
    triu = _tri(tm, True)
    sel = _dc_select()

    def body(pre_ref, dqa_ref, dka_ref, dqb_ref, dkb_ref, dva_ref, dvb_ref, z_ref, x_ref, dh_ref, gain_ref, triu_ref,
             sel_ref, w_ref, g_ref, dproj_ref, small_ref, dx_ref, dg_ref, carry_ref):
        i = pl.program_id(0)

        @pl.when(i == 0)
        def _():
            small_ref[...] = jnp.zeros_like(small_ref)
            dg_ref[...] = jnp.zeros_like(dg_ref)

        @pl.when(i % tiles_per_seq == 0)
        def _():
            carry_ref[...] = jnp.zeros_like(carry_ref)

        def norm_bwd(g, dhat):
            cols = slice(g * LANES, (g + 1) * LANES)
            p = pre_ref[:, cols].astype(F32)
            rr = lax.rsqrt(jnp.sum(p * p, axis=-1, keepdims=True) * (1.0 / HEAD_DIM) + EPS)
            n = p * rr
            dz = dhat * gain_ref[:, cols]
            dproj_ref[:, cols] = (rr * (dz - n * (jnp.sum(dz * n, axis=-1, keepdims=True) * (1.0 / HEAD_DIM)))
                                  ).astype(BF16)
            return jnp.sum(dhat * n, axis=0, keepdims=True)

        def group_sum(g0, d_ref, count, scale):
            acc = jnp.zeros((1, LANES), F32)
            for h in range(count):
                d = d_ref[:, h * LANES:(h + 1) * LANES]
                acc = acc + norm_bwd(g0 + h, d * scale if scale != 1.0 else d)
            return acc

        small_ref[0:1, :] += group_sum(G_QA, dqa_ref, A_HEADS, SCALE)
        small_ref[1:2, :] += group_sum(G_KA, dka_ref, A_KV_HEADS, 1.0)
        small_ref[2:3, :] += group_sum(G_QB, dqb_ref, B_HEADS, SCALE)
        small_ref[3:4, :] += group_sum(G_KB, dkb_ref, B_HEADS, LN2)
        dproj_ref[:, G_VA * LANES:G_VB * LANES] = dva_ref[...].astype(BF16)
        dproj_ref[:, G_VB * LANES:G_F * LANES] = dvb_ref[...].astype(BF16)

        dc = jnp.zeros((tm, LANES), F32)
        for h in range(B_HEADS):
            for d_ref, first_row in ((dqb_ref, h * LANES), ((dkb_ref, (B_HEADS + h) * LANES))):
                for piece in _split3(d_ref[:, h * LANES:(h + 1) * LANES]):
                    dc = dc + _dot(piece, sel_ref[first_row:first_row + LANES, :])
        dlf = _dot_exact(triu_ref[...], dc) + carry_ref[...]
        carry_ref[...] += jnp.sum(dc, axis=0, keepdims=True)
        dz = dlf / (1.0 + jnp.exp(z_ref[...]))
        small_ref[4:5, :] += jnp.sum(dz, axis=0, keepdims=True)
        dproj_ref[:, G_F * LANES:(G_F + 1) * LANES] = dz.astype(BF16)
        dproj_ref[:, (G_F + 1) * LANES:] = jnp.zeros((tm, LANES), BF16)

        dxn = _dot(dproj_ref[...], w_ref[...])
        x = x_ref[...]
        r = lax.rsqrt(jnp.mean(x * x, axis=-1, keepdims=True) + EPS)
        xh = x * r
        dg_ref[...] += jnp.sum(dxn * xh, axis=0, keepdims=True)
        dxz = dxn * g_ref[...]
        dx_ref[...] = dh_ref[...] + r * (dxz - xh * jnp.mean(dxz * xh, axis=-1, keepdims=True))

    def rev(n):
        return pl.BlockSpec((tm, n), lambda i: (nt - 1 - i, 0))

    return _pcall(
        body, name="dproj_dx", grid=(nt,),
        in_specs=[rev(N_NORM_GROUPS * LANES), rev(A_HEADS * LANES), rev(A_KV_HEADS * LANES), rev(B_HEADS * LANES),
                  rev(B_HEADS * LANES), rev(A_KV_HEADS * LANES), rev(B_HEADS * LANES), rev(LANES), rev(D_MODEL),
                  rev(D_MODEL), _const((1, NP)), _const((tm, tm)), _const(sel.shape), _const((NP, D_MODEL)),
                  _const((1, D_MODEL))],
        out_specs=[rev(NP), _const((8, LANES)), rev(D_MODEL), _const((1, D_MODEL))],
        out_shape=[jax.ShapeDtypeStruct((t_all, NP), BF16), jax.ShapeDtypeStruct((8, LANES), F32),
                   jax.ShapeDtypeStruct((t_all, D_MODEL), F32), jax.ShapeDtypeStruct((1, D_MODEL), F32)],
        scratch_shapes=[pltpu.VMEM((1, LANES), F32)],
        compiler_params=_params(("arbitrary",)),
    )(pre, dqa, dka, dqb, dkb, dva, dvb, z, x2, dh, gain_row, triu, sel, w_pad_t, g1)


def _dwin(dproj, xn):
    t_all = xn.shape[0]
    tt = min(512, t_all)
    half = NP // 2

    def body(a_ref, b_ref, o_ref):
        @pl.when(pl.program_id(1) == 0)
        def _():
            o_ref[...] = jnp.zeros_like(o_ref)

        o_ref[...] += _dot_tn(a_ref[...], b_ref[...])

    return _pcall(
        body, name="dwin", grid=(2, t_all // tt),
        in_specs=[pl.BlockSpec((tt, half), lambda j, t: (t, j)), pl.BlockSpec((tt, D_MODEL), lambda j, t: (t, 0))],
        out_specs=pl.BlockSpec((half, D_MODEL), lambda j, t: (j, 0)),
        out_shape=jax.ShapeDtypeStruct((NP, D_MODEL), F32),
        compiler_params=_params(("parallel", "arbitrary")),
    )(dproj, xn)


ANY = pl.BlockSpec(memory_space=pl.ANY)


def _place():
    return lax.axis_index("x"), lax.axis_index("y"), lax.axis_index("c")


class _Gather:
    def __init__(self, srcs, outs, send_sems, recv_sems, local_sems):
        self.srcs, self.outs = srcs, outs
        self.send_sems, self.recv_sems, self.local_sems = send_sems, recv_sems, local_sems
        x, y, c = _place()
        self.c = c
        self.me, self.sibling = (x, y, c), (x, y, 1 - c)
        self.chips = [(1 - x, y), (x, 1 - y), (1 - x, 1 - y)]

    def _rows(self, a, px, py, pc):
        m = self.srcs[a].shape[0]
        return self.outs[a].at[pl.ds((4 * px + 2 * py + pc) * m, m), :]

    def _copy(self, a, k, block, to, from_src=False):
        return pltpu.make_async_remote_copy(
            src_ref=self.srcs[a] if from_src else self._rows(a, *block), dst_ref=self._rows(a, *block),
            send_sem=self.send_sems.at[k, a], recv_sem=self.recv_sems.at[k, a], device_id=to, device_id_type=MESH)

    def _own(self, a):
        return pltpu.make_async_copy(self.srcs[a], self._rows(a, *self.me), self.local_sems.at[a])

    def start(self):
        for a in range(len(self.srcs)):
            self._own(a).start()
            self._copy(a, 0, self.me, self.sibling, from_src=True).start()
            for j, chip in enumerate(self.chips):
                self._copy(a, 1 + j, self.me, (*chip, self.c), from_src=True).start()

    def forward(self):
        for a in range(len(self.srcs)):
            for j, chip in enumerate(self.chips):
                self._copy(a, 1 + j, (*chip, self.c), self.me).wait_recv()
                self._copy(a, 4 + j, (*chip, self.c), self.sibling).start()

    def finish(self):
        for a in range(len(self.srcs)):
            self._copy(a, 0, self.sibling, self.me).wait_recv()
            for j, chip in enumerate(self.chips):
                self._copy(a, 4 + j, (*chip, 1 - self.c), self.me).wait_recv()
            self._copy(a, 0, self.me, self.sibling, from_src=True).wait_send()
            for j, chip in enumerate(self.chips):
                self._copy(a, 1 + j, self.me, (*chip, self.c), from_src=True).wait_send()
                self._copy(a, 4 + j, (*chip, self.c), self.sibling).wait_send()
            self._own(a).wait()


def _gather_scratch(n_arrays):
    return [pltpu.SemaphoreType.DMA((7, n_arrays)), pltpu.SemaphoreType.DMA((7, n_arrays)),
            pltpu.SemaphoreType.DMA((n_arrays,))]


def _allgather_halves(mine):
    m_per, n = mine.shape

    def body(x_ref, out_ref, send_sems, recv_sems, local_sems):
        gather = _Gather((x_ref,), (out_ref,), send_sems, recv_sems, local_sems)
        gather.start()
        gather.forward()
        gather.finish()

    return _pcall(
        body, name="allgather_w_in",
        out_shape=jax.ShapeDtypeStruct((8 * m_per, n), mine.dtype),
        in_specs=[ANY], out_specs=ANY, scratch_shapes=_gather_scratch(1),
    )(mine)


def _rs_pair_exchange(g, name):
    def body(g_ref, out_ref, send_sem, recv_sem):
        x, y, c = _place()
        cp = pltpu.make_async_remote_copy(
            src_ref=g_ref.at[1 - c], dst_ref=out_ref, send_sem=send_sem, recv_sem=recv_sem,
            device_id=(x, y, 1 - c), device_id_type=MESH)
        cp.start()
        cp.wait()

    return _pcall(
        body, name=name, out_shape=jax.ShapeDtypeStruct(g.shape[1:], F32),
        in_specs=[ANY], out_specs=ANY, scratch_shapes=[pltpu.SemaphoreType.DMA, pltpu.SemaphoreType.DMA],
    )(g)


def _rs_pair_add(g, got, c_idx, name):
    rows = g.shape[2]

    def body(c_ref, a_ref, b_ref, o_ref, ob_ref):
        pair = a_ref[...] + b_ref[...]
        o_ref[...] = pair
        ob_ref[...] = pair.astype(BF16)

    blk = pl.BlockSpec((None, rows, D_MODEL), lambda s, c_ref: (s, 0, 0))
    return _pcall(
        body, name=name,
        grid_spec=pltpu.PrefetchScalarGridSpec(
            num_scalar_prefetch=1, grid=(N_CHIPS,),
            in_specs=[pl.BlockSpec((None, None, rows, D_MODEL), lambda s, c_ref: (c_ref[0], s, 0, 0)), blk],
            out_specs=[blk, blk]),
        out_shape=[jax.ShapeDtypeStruct((N_CHIPS, rows, D_MODEL), F32),
                   jax.ShapeDtypeStruct((N_CHIPS, rows, D_MODEL), BF16)],
        compiler_params=_params(("parallel",)),
    )(c_idx, g, got)


def _chip_exchange_copies(p_ref, out_ref, send_sems, recv_sems):
    x, y, c = _place()
    chips = [(1 - x, y), (x, 1 - y), (1 - x, 1 - y)]
    return [pltpu.make_async_remote_copy(
        src_ref=p_ref.at[2 * cx + cy], dst_ref=out_ref.at[j], send_sem=send_sems.at[j], recv_sem=recv_sems.at[j],
        device_id=(cx, cy, c), device_id_type=MESH) for j, (cx, cy) in enumerate(chips)]


def _rs_chip_exchange(p4, name):
    def body(p_ref, out_ref, send_sems, recv_sems):
        cps = _chip_exchange_copies(p_ref, out_ref, send_sems, recv_sems)
        for cp in cps:
            cp.start()
        for cp in cps:
            cp.wait()

    return _pcall(
        body, name=name, out_shape=jax.ShapeDtypeStruct((3,) + p4.shape[1:], p4.dtype),
        in_specs=[ANY], out_specs=ANY,
        scratch_shapes=[pltpu.SemaphoreType.DMA((3,)), pltpu.SemaphoreType.DMA((3,))],
    )(p4)


def _rs_chip_add(p4, got, sc_idx, name):
    rows = p4.shape[1]
    tr = next(rows // n for n in (8, 7, 6, 5, 4, 3, 2, 1) if rows % n == 0 and (rows // n) % 16 == 0)

    def body(sc_ref, a_ref, b_ref, o_ref):
        o_ref[...] = ((a_ref[...] + b_ref[0].astype(F32)) + b_ref[1].astype(F32)) + b_ref[2].astype(F32)

    return _pcall(
        body, name=name,
        grid_spec=pltpu.PrefetchScalarGridSpec(
            num_scalar_prefetch=1, grid=(rows // tr,),
            in_specs=[pl.BlockSpec((None, tr, D_MODEL), lambda i, sc_ref: (sc_ref[0], i, 0)),
                      pl.BlockSpec((3, tr, D_MODEL), lambda i, sc_ref: (0, i, 0))],
            out_specs=pl.BlockSpec((None, tr, D_MODEL), lambda i, sc_ref: (sc_ref[1], i, 0))),
        out_shape=jax.ShapeDtypeStruct((2, rows, D_MODEL), F32),
        compiler_params=_params(("parallel",)),
    )(sc_idx, p4, got)


def _rs_pair_share(halves, name):
    def body(r_ref, out_ref, send_sem, recv_sem):
        x, y, c = _place()
        cp = pltpu.make_async_remote_copy(
            src_ref=r_ref.at[c], dst_ref=out_ref.at[c], send_sem=send_sem, recv_sem=recv_sem,
            device_id=(x, y, 1 - c), device_id_type=MESH)
        cp.start()
        cp.wait()

    return _pcall(
        body, name=name, out_shape=jax.ShapeDtypeStruct(halves.shape, F32),
        in_specs=[ANY], out_specs=ANY, input_output_aliases={0: 0},
        scratch_shapes=[pltpu.SemaphoreType.DMA, pltpu.SemaphoreType.DMA],
    )(halves)


def _adam(w, g, m, v):
    m2 = ADAM_B1 * m + (1.0 - ADAM_B1) * g
    v2 = ADAM_B2 * v + (1.0 - ADAM_B2) * (g * g)
    m_hat = m2 / (1.0 - ADAM_B1 ** ADAM_STEP)
    v_hat = v2 / (1.0 - ADAM_B2 ** ADAM_STEP)
    return -ADAM_LR * (m_hat / (jnp.sqrt(v_hat) + ADAM_EPS) + ADAM_WD * w), m2, v2


def _small_allreduce(part):
    def body(p_ref, g_ref, buf, send_sems, recv_sems):
        x, y, c = _place()
        me = 4 * x + 2 * y + c
        cps = []
        for k in range(1, 8):
            peer = (1 - x if k & 4 else x, 1 - y if k & 2 else y, 1 - c if k & 1 else c)
            cps.append(pltpu.make_async_remote_copy(
                src_ref=p_ref, dst_ref=buf.at[me], send_sem=send_sems.at[k - 1], recv_sem=recv_sems.at[k - 1],
                device_id=peer, device_id_type=MESH))
        for cp in cps:
            cp.start()
        buf[me] = p_ref[...]
        for cp in cps:
            cp.wait()
        g = buf[0]
        for k in range(1, 8):
            g = g + buf[k]
        g_ref[...] = g

    vm = pl.BlockSpec(memory_space=pltpu.VMEM)
    return _pcall(
        body, name="small_allreduce",
        out_shape=jax.ShapeDtypeStruct((SMALL_ROWS, LANES), F32), in_specs=[vm], out_specs=vm,
        scratch_shapes=[pltpu.VMEM((8, SMALL_ROWS, LANES), F32), pltpu.SemaphoreType.DMA((7,)),
                        pltpu.SemaphoreType.DMA((7,))],
    )(part)


def _small_adamw(g, w, m, v):
    pieces = ((0, 8, LANES), (20, 1, B_HEADS), (16, 1, HEAD_DIM), (17, 1, HEAD_DIM), (21, 1, A_HEADS),
              (18, 1, HEAD_DIM), (19, 1, HEAD_DIM), (8, 8, LANES))

    def body(g_ref, w_ref, m_ref, v_ref, *rest):
        outs, (loss_ref, stage) = rest[:4 * len(pieces)], rest[4 * len(pieces):]
        g = g_ref[...]
        loss_ref[...] = g_ref[ROW_LOSS:ROW_LOSS + 1, :]
        for kind, packed in enumerate((g,) + _adam(w_ref[...], g, m_ref[...], v_ref[...])):
            stage[...] = packed
            for i, (row, rows, lanes) in enumerate(pieces):
                outs[kind * len(pieces) + i][...] = stage[row:row + rows, 0:lanes]

    vm = pl.BlockSpec(memory_space=pltpu.VMEM)
    shapes = [jax.ShapeDtypeStruct((rows, lanes), F32) for _ in range(4) for _, rows, lanes in pieces]
    shapes.append(jax.ShapeDtypeStruct((1, LANES), F32))
    res = _pcall(
        body, name="small_adamw", out_shape=shapes, in_specs=[vm, vm, vm, vm], out_specs=[vm] * len(shapes),
        scratch_shapes=[pltpu.VMEM((SMALL_ROWS, LANES), F32)],
    )(g, w, m, v)
    flat = [r.reshape(r.size) for r in res[:-1]]
    n = len(pieces)
    return [flat[k * n:(k + 1) * n] for k in range(4)], res[-1][0, 0]


def _adamw(w, g, m, v, name):
    rows, cols = w.shape
    tr = min(256, rows)

    def body(w_ref, g_ref, m_ref, v_ref, d_ref, m2_ref, v2_ref):
        d_ref[...], m2_ref[...], v2_ref[...] = _adam(w_ref[...], g_ref[...], m_ref[...], v_ref[...])

    spec = _rows(tr, cols)
    shp = jax.ShapeDtypeStruct((rows, cols), F32)
    return _pcall(
        body, name=name, grid=(rows // tr,), in_specs=[spec] * 4, out_specs=[spec] * 3, out_shape=[shp] * 3,
        compiler_params=_params(("parallel",)),
    )(w, g, m, v)


def _pad_lanes(v):
    return jnp.pad(v, (0, LANES - v.shape[0]))


def _pad_head_rows(w_t, heads):
    n = w_t.shape[1]
    return jnp.pad(w_t.reshape(heads, HEAD_DIM, n), ((0, 0), (0, LANES - HEAD_DIM), (0, 0))).reshape(heads * LANES, n)


def _unpad_head_rows(w_t, heads):
    n = w_t.shape[1]
    return w_t.reshape(heads, LANES, n)[:, :HEAD_DIM].reshape(heads * HEAD_DIM, n)


def _in_rows_pad(w_in_t):
    qa, ka, va, qb, kb, vb, f = jnp.split(w_in_t, [512, 640, 768, 1280, 1792, 2304], axis=0)
    f = jnp.pad(f, ((0, 2 * LANES - B_HEADS), (0, 0)))
    return jnp.concatenate([_pad_head_rows(qa, 8), _pad_head_rows(ka, 2), _pad_head_rows(qb, 8),
                            _pad_head_rows(kb, 8), _pad_head_rows(va, 2), _pad_head_rows(vb, 8), f], axis=0)


def _in_rows_unpad(d):
    qa = _unpad_head_rows(d[G_QA * LANES:G_KA * LANES], 8)
    ka = _unpad_head_rows(d[G_KA * LANES:G_QB * LANES], 2)
    qb = _unpad_head_rows(d[G_QB * LANES:G_KB * LANES], 8)
    kb = _unpad_head_rows(d[G_KB * LANES:G_VA * LANES], 8)
    va = _unpad_head_rows(d[G_VA * LANES:G_VB * LANES], 2)
    vb = _unpad_head_rows(d[G_VB * LANES:G_F * LANES], 8)
    f = d[G_F * LANES:G_F * LANES + B_HEADS]
    return jnp.concatenate([qa, ka, va, qb, kb, vb, f], axis=0)


def _pack_small(g1, bf, qa, ka, sk, qb, kb, g2, loss_row):
    rows = [g1.reshape(8, LANES), g2.reshape(8, LANES)]
    rows += [_pad_lanes(t)[None] for t in (qa, ka, qb, kb, bf, sk)]
    rows += [loss_row, jnp.zeros((1, LANES), F32)]
    return jnp.concatenate(rows, axis=0)


def kernel(x, attn_norm_g, w_in, b_forget, q_norm_a, k_norm_a, sink_logits, q_norm_b, k_norm_b, w_out, mlp_norm_g, w_up, w_down, loss_target, m_attn_norm_g, m_w_in, m_b_forget, m_q_norm_a, m_k_norm_a, m_sink_logits, m_q_norm_b, m_k_norm_b, m_w_out, m_mlp_norm_g, m_w_up, m_w_down, v_attn_norm_g, v_w_in, v_b_forget, v_q_norm_a, v_k_norm_a, v_sink_logits, v_q_norm_b, v_k_norm_b, v_w_out, v_mlp_norm_g, v_w_up, v_w_down):
    nb, seq, _ = x.shape
    t_all = nb * seq
    c_idx = lax.axis_index("c")
    s_idx = 2 * lax.axis_index("x") + lax.axis_index("y")

    def my_half(a):
        halves = a.astype(BF16).reshape(2, a.shape[0] // 2, a.shape[1])
        return lax.dynamic_slice_in_dim(halves, c_idx, 1, axis=0)[0]

    w_in_shard_t = jnp.pad(w_in.T, ((0, IN_SHARD_P - IN_SHARD), (0, 0)))
    gathered_in = _allgather_halves(my_half(w_in_shard_t)).reshape(N_CHIPS, IN_SHARD_P, D_MODEL)
    w_pad_t = _in_rows_pad(gathered_in[:, :IN_SHARD].reshape(IN_WIDTH, D_MODEL))

    ones = jnp.ones((LANES,), F32)
    gain_row = jnp.concatenate(
        [jnp.tile(_pad_lanes(q_norm_a), 8), jnp.tile(_pad_lanes(k_norm_a), 2), jnp.tile(_pad_lanes(q_norm_b), 8),
         jnp.tile(_pad_lanes(k_norm_b), 8), jnp.tile(ones, N_GROUPS - N_NORM_GROUPS)])[None]
    b_row = _pad_lanes(b_forget)[None]
    g1 = attn_norm_g[None]
    g2 = mlp_norm_g[None]
    slopes = jnp.exp2(-(8.0 / A_HEADS) * (jnp.arange(A_HEADS, dtype=F32) + 1.0))

    x2 = x.reshape(t_all, D_MODEL)
    tgt = loss_target.reshape(t_all, D_MODEL)

    (xn, pre, qa, ka, va, qb, kb, vb, z), (w_out_g, w_up_g, w_down_f) = _inproj(
        x2, g1, w_pad_t, gain_row, b_row, seq, [my_half(w_out), my_half(w_up), my_half(w_down)])
    wo_pad = _pad_head_rows(w_out_g, A_HEADS + B_HEADS)
    w_up_blocks = w_up_g.reshape(N_CHIPS, D_MODEL, D_MODEL)
    swa_bias = _swa_bias(slopes)
    oa, la = _swa_fwd(qa, ka, va, sink_logits, swa_bias, nb, seq)
    ob, lse = _fox_fwd(qb, kb, vb, nb, seq)
    h, hn = _outproj(x2, oa, ob, wo_pad, g2)
    ru, dy, dyb, loss_acc = _mlp_fwd(hn, w_up_blocks, w_down_f, h, tgt)

    du, d_w_mlp = _mlp_bwd_w(dyb, w_down_f, ru, hn)
    c_arg = c_idx.reshape(1).astype(jnp.int32)
    sc_arg = jnp.stack([s_idx, c_idx]).astype(jnp.int32)
    dh, dhb, d_g2, sibling_w_mlp = _mlp_dhn(du, w_up_blocks, h, dy, g2, d_w_mlp)
    pair_m, pair_m_bf = _rs_pair_add(d_w_mlp, sibling_w_mlp, c_arg, "rs_pair_add_mlp")
    doa, dob, delta_b, d_wo = _dmixed(dhb, wo_pad, oa, ob)
    dqb, dkb, dvb, got_m = _fox_bwd(qb, kb, vb, dob, lse, delta_b, nb, seq, pair_m_bf)
    red_m = _rs_pair_share(_rs_chip_add(pair_m, got_m, sc_arg, "rs_chip_add_mlp"), "rs_pair_share_mlp")
    g_w_up, g_w_down = red_m[0], red_m[1]
    dqa, dka, dva, dsink = _swa_bwd(qa, ka, va, oa, doa, la, sink_logits, swa_bias, nb, seq)
    dproj, small, grad_x, d_g1 = _dproj_dx(pre, dqa, dka, dqb, dkb, dva, dvb, z, x2, dh, gain_row, w_pad_t, g1, seq)
    d_w_in_t = _dwin(dproj, xn)

    d_w_out = _unpad_head_rows(d_wo, A_HEADS + B_HEADS)
    g_att = jnp.concatenate([
        jnp.pad(_in_rows_unpad(d_w_in_t).reshape(N_CHIPS, IN_SHARD, D_MODEL),
                ((0, 0), (0, IN_SHARD_P - IN_SHARD), (0, 0))),
        d_w_out.reshape(N_CHIPS, D_MODEL // N_CHIPS, D_MODEL)], axis=1)
    g_att = jnp.stack([g_att[:, :R_ATT // 2], g_att[:, R_ATT // 2:]])
    pair_a, pair_a_bf = _rs_pair_add(g_att, _rs_pair_exchange(g_att, "rs_pair_exchange_att"), c_arg, "rs_pair_add_att")
    got_a = _rs_chip_exchange(pair_a_bf, "rs_chip_exchange_att")
    red_a = _rs_pair_share(_rs_chip_add(pair_a, got_a, sc_arg, "rs_chip_add_att"), "rs_pair_share_att")
    red_a = red_a.reshape(R_ATT, D_MODEL)
    g_w_in = red_a[:IN_SHARD].T
    g_w_out = red_a[IN_SHARD_P:]

    loss_row = loss_acc[0:1] * (0.5 / D_MODEL)
    d_sink = dsink[:, :A_GROUP, 0].reshape(nb, A_HEADS).sum(axis=0)
    part = _pack_small(d_g1[0], small[4, :B_HEADS], small[0, :HEAD_DIM], small[1, :HEAD_DIM], d_sink,
                       small[2, :HEAD_DIM], small[3, :HEAD_DIM], d_g2[0], loss_row)
    zero_row = jnp.zeros((1, LANES), F32)
    smalls = lambda t: _pack_small(*t, zero_row)
    w_small = smalls((attn_norm_g, b_forget, q_norm_a, k_norm_a, sink_logits, q_norm_b, k_norm_b, mlp_norm_g))
    m_small = smalls((m_attn_norm_g, m_b_forget, m_q_norm_a, m_k_norm_a, m_sink_logits, m_q_norm_b, m_k_norm_b,
                      m_mlp_norm_g))
    v_small = smalls((v_attn_norm_g, v_b_forget, v_q_norm_a, v_k_norm_a, v_sink_logits, v_q_norm_b, v_k_norm_b,
                      v_mlp_norm_g))
    (g_s, d_s, m_s, v_s), loss = _small_adamw(_small_allreduce(part), w_small, m_small, v_small)

    big = {}
    for name, w, g, m, v in (("adamw_w_in", w_in, g_w_in, m_w_in, v_w_in),
                             ("adamw_w_out", w_out, g_w_out, m_w_out, v_w_out),
                             ("adamw_w_up", w_up, g_w_up, m_w_up, v_w_up),
                             ("adamw_w_down", w_down, g_w_down, m_w_down, v_w_down)):
        big[name] = (g,) + tuple(_adamw(w, g, m, v, name))

    def assemble(k, s):
        return (s[0], big["adamw_w_in"][k], s[1], s[2], s[3], s[4], s[5], s[6], big["adamw_w_out"][k], s[7],
                big["adamw_w_up"][k], big["adamw_w_down"][k])

    return (loss, grad_x.reshape(nb, seq, D_MODEL), *assemble(0, g_s), *assemble(1, d_s), *assemble(2, m_s),
            *assemble(3, v_s))
```
